```python
import jax, jax.numpy as jnp
from jax import lax
import numpy as np

D_MODEL = 1024
BATCH = 8
SEQ = 2048
DEPTH = 1

CHUNK = 64
Q_BLOCK = 128
ROPE_THETA = 10000.0
EPS = 1e-6

A_HEADS = 8
A_KV_HEADS = 2
A_HEAD_DIM = 64
IDX_HEADS = 8
IDX_DIM = 64
TOPK_MAX = 256

B_HEADS = 4
B_HEAD_DIM = 128
CONV_WIDTH = 4

D_FF = 4 * D_MODEL
MIX_WIDTH = A_HEADS * A_HEAD_DIM + B_HEADS * B_HEAD_DIM
B_CONV_CH = 3 * B_HEADS * B_HEAD_DIM

SPLIT_SIZES = (A_HEADS * A_HEAD_DIM, A_KV_HEADS * A_HEAD_DIM, A_KV_HEADS * A_HEAD_DIM,
               IDX_HEADS * IDX_DIM, IDX_DIM, IDX_HEADS,
               B_HEADS * B_HEAD_DIM, B_HEADS * B_HEAD_DIM, B_HEADS * B_HEAD_DIM,
               B_HEADS * B_HEAD_DIM, B_HEADS, B_HEADS)
IN_DIM = sum(SPLIT_SIZES)
SPLIT_POINTS = tuple(int(v) for v in np.cumsum(SPLIT_SIZES)[:-1])

kernel_name = 'hybrid_dsa_gdn_block'


def rms_norm(x, g):
    xf = x.astype(jnp.float32)
    y = xf * lax.rsqrt(jnp.mean(jnp.square(xf), axis=-1, keepdims=True) + EPS)
    return (y * g.astype(jnp.float32)).astype(x.dtype)


def l2_norm(x):
    return x * lax.rsqrt(jnp.sum(jnp.square(x), axis=-1, keepdims=True) + EPS)


def rope(x, pos):
    d = x.shape[-1]
    half = d // 2
    inv_freq = 1.0 / (ROPE_THETA ** (jnp.arange(half, dtype=jnp.float32) / half))
    ang = pos.astype(jnp.float32)[:, None] * inv_freq[None, :]
    cos = jnp.cos(ang)[:, None, :]
    sin = jnp.sin(ang)[:, None, :]
    xf = x.astype(jnp.float32)
    x1, x2 = xf[..., :half], xf[..., half:]
    return jnp.concatenate([x1 * cos - x2 * sin, x2 * cos + x1 * sin], axis=-1).astype(x.dtype)


def causal_depthwise_conv(x, w):
    ch = x.shape[-1]
    return lax.conv_general_dilated(
        x, w[:, None, :].astype(x.dtype), window_strides=(1,),
        padding=[(CONV_WIDTH - 1, 0)], dimension_numbers=('NWC', 'WIO', 'NWC'),
        feature_group_count=ch)


def sparse_indexer_attention(q, k, v, q_idx, k_idx, w_idx):
    bsz, seq = q.shape[0], q.shape[1]
    n_keys = seq
    topk = min(TOPK_MAX, n_keys // 4)
    n_blocks = seq // Q_BLOCK
    group = A_HEADS // A_KV_HEADS
    key_pos = jnp.arange(n_keys)
    gather = jax.vmap(lambda table, idx: table[idx])

    def blocks(a):
        return jnp.moveaxis(a.reshape((bsz, n_blocks, Q_BLOCK) + a.shape[2:]), 1, 0)

    def one_block(args):
        qb, qib, wb, blk = args
        q_pos = blk * Q_BLOCK + jnp.arange(Q_BLOCK)
        limit = (q_pos // CHUNK + 1) * CHUNK
        rel = jax.nn.relu(jnp.einsum('bqhd,bsd->bqhs', qib, k_idx).astype(jnp.float32))
        iscore = jnp.einsum('bqhs,bqh->bqs', rel, wb.astype(jnp.float32))
        iscore = jnp.where(key_pos[None, None, :] < limit[None, :, None], iscore, -jnp.inf)
        _, sel = lax.top_k(iscore, topk)
        valid = sel < limit[None, :, None]
        k_sel = gather(k, sel)
        v_sel = gather(v, sel)
        qg = qb.reshape(bsz, Q_BLOCK, A_KV_HEADS, group, A_HEAD_DIM)
        s = jnp.einsum('bqngd,bqknd->bqngk', qg, k_sel).astype(jnp.float32) * (A_HEAD_DIM ** -0.5)
        s = jnp.where(valid[:, :, None, None, :], s, -jnp.inf)
        p = jax.nn.softmax(s, axis=-1).astype(v.dtype)
        o = jnp.einsum('bqngk,bqknd->bqngd', p, v_sel)
        return o.reshape(bsz, Q_BLOCK, A_HEADS * A_HEAD_DIM)

    out = lax.map(one_block, (blocks(q), blocks(q_idx), blocks(w_idx), jnp.arange(n_blocks)))
    return jnp.moveaxis(out, 0, 1).reshape(bsz, seq, A_HEADS * A_HEAD_DIM)


def gated_delta_rule_chunked(q, k, v, g, beta):
    bsz, seq, nh, dk = q.shape
    dv = v.shape[-1]
    n = seq // CHUNK

    def to_chunks(a):
        return a.reshape(bsz, n, CHUNK, nh, -1).transpose(0, 3, 1, 2, 4)

    q = to_chunks(q) * (dk ** -0.5)
    k = to_chunks(k)
    v = to_chunks(v)
    g = g.reshape(bsz, n, CHUNK, nh).transpose(0, 3, 1, 2)
    beta = beta.reshape(bsz, n, CHUNK, nh).transpose(0, 3, 1, 2)
    gc = jnp.cumsum(g, axis=-1)
    incl = jnp.tril(jnp.ones((CHUNK, CHUNK), dtype=bool))
    strict = jnp.tril(jnp.ones((CHUNK, CHUNK), dtype=bool), k=-1)
    decay = jnp.exp(jnp.where(incl, gc[..., :, None] - gc[..., None, :], -jnp.inf))
    kb = k * beta[..., None]
    m = jnp.where(strict, jnp.einsum('bhncd,bhnsd->bhncs', kb, k) * decay, 0.0)
    a = m + jnp.eye(CHUNK, dtype=m.dtype)
    rhs = jnp.concatenate([v * beta[..., None], kb * jnp.exp(gc)[..., None]], axis=-1)
    sol = lax.linalg.triangular_solve(a, rhs, left_side=True, lower=True, unit_diagonal=True)
    u, w = sol[..., :dv], sol[..., dv:]
    att = jnp.einsum('bhncd,bhnsd->bhncs', q, k) * decay
    q_dec = q * jnp.exp(gc)[..., None]
    k_tail = k * jnp.exp(gc[..., -1:] - gc)[..., None]
    g_last = jnp.exp(gc[..., -1])

    def step(state, inp):
        u_n, w_n, qd_n, att_n, kt_n, gl_n = inp
        v_new = u_n - jnp.einsum('bhcd,bhde->bhce', w_n, state)
        o_n = jnp.einsum('bhcd,bhde->bhce', qd_n, state) + jnp.einsum('bhcs,bhse->bhce', att_n, v_new)
        state = state * gl_n[..., None, None] + jnp.einsum('bhcd,bhce->bhde', kt_n, v_new)
        return state, o_n

    xs = (jnp.moveaxis(u, 2, 0), jnp.moveaxis(w, 2, 0), jnp.moveaxis(q_dec, 2, 0),
          jnp.moveaxis(att, 2, 0), jnp.moveaxis(k_tail, 2, 0), jnp.moveaxis(g_last, 2, 0))
    s0 = jnp.zeros((bsz, nh, dk, dv), jnp.float32)
    _, o = lax.scan(step, s0, xs)
    return o.transpose(1, 0, 3, 2, 4).reshape(bsz, seq, nh, dv)


def gated_deltanet(q, k, v, z, b, a, conv_w, a_log, dt_bias, norm_g):
    bsz, seq = q.shape[0], q.shape[1]
    qkv = jax.nn.silu(causal_depthwise_conv(jnp.concatenate([q, k, v], axis=-1), conv_w))
    qkv = qkv.astype(jnp.float32)
    q, k, v = jnp.split(qkv, 3, axis=-1)
    q = l2_norm(q.reshape(bsz, seq, B_HEADS, B_HEAD_DIM))
    k = l2_norm(k.reshape(bsz, seq, B_HEADS, B_HEAD_DIM))
    v = v.reshape(bsz, seq, B_HEADS, B_HEAD_DIM)
    beta = jax.nn.sigmoid(b.astype(jnp.float32))
    g = -jnp.exp(a_log.astype(jnp.float32)) * jax.nn.softplus(a.astype(jnp.float32) + dt_bias.astype(jnp.float32))
    o = gated_delta_rule_chunked(q, k, v, g, beta)
    zf = z.reshape(bsz, seq, B_HEADS, B_HEAD_DIM).astype(jnp.float32)
    o = rms_norm(o, norm_g) * jax.nn.silu(zf)
    return o.reshape(bsz, seq, B_HEADS * B_HEAD_DIM).astype(z.dtype)


def setup_inputs(seed: int = 0) -> dict:
    key = jax.random.key(seed)
    ks = jax.random.split(key, 13)
    f32 = jnp.float32
    x = jax.random.normal(ks[0], (BATCH, SEQ, D_MODEL), f32)
    norm_mix_g = 1.0 + 0.02 * jax.random.normal(ks[1], (DEPTH, D_MODEL), f32)
    w_in = jax.random.normal(ks[2], (DEPTH, D_MODEL, IN_DIM), f32) * (D_MODEL ** -0.5)
    conv_w = jax.random.normal(ks[3], (DEPTH, CONV_WIDTH, B_CONV_CH), f32) * (CONV_WIDTH ** -0.5)
    a_log = jnp.log(jax.random.uniform(ks[4], (DEPTH, B_HEADS), f32, minval=1.0, maxval=16.0))
    dt = jnp.exp(jax.random.uniform(ks[5], (DEPTH, B_HEADS), f32, minval=np.log(1e-3), maxval=np.log(1e-1)))
    dt_bias = dt + jnp.log(-jnp.expm1(-dt))
    gdn_norm_g = 1.0 + 0.02 * jax.random.normal(ks[6], (DEPTH, B_HEAD_DIM), f32)
    w_out = jax.random.normal(ks[7], (DEPTH, MIX_WIDTH, D_MODEL), f32) * (MIX_WIDTH ** -0.5)
    norm_ffn_g = 1.0 + 0.02 * jax.random.normal(ks[8], (DEPTH, D_MODEL), f32)
    w_ff1 = jax.random.normal(ks[9], (DEPTH, D_MODEL, D_FF), f32) * (D_MODEL ** -0.5)
    w_ff2 = jax.random.normal(ks[10], (DEPTH, D_FF, D_MODEL), f32) * (D_FF ** -0.5)
    norm_final_g = 1.0 + 0.02 * jax.random.normal(ks[11], (D_MODEL,), f32)
    return {'x': x, 'norm_mix_g': norm_mix_g, 'w_in': w_in, 'conv_w': conv_w,
            'a_log': a_log, 'dt_bias': dt_bias, 'gdn_norm_g': gdn_norm_g, 'w_out': w_out,
            'norm_ffn_g': norm_ffn_g, 'w_ff1': w_ff1, 'w_ff2': w_ff2, 'norm_final_g': norm_final_g}


def reference(x, norm_mix_g, w_in, conv_w, a_log, dt_bias, gdn_norm_g, w_out,
              norm_ffn_g, w_ff1, w_ff2, norm_final_g):
    bsz, seq = x.shape[0], x.shape[1]
    pos = jnp.arange(seq)
    for l in range(DEPTH):
        h = rms_norm(x, norm_mix_g[l])
        proj = h @ w_in[l]
        (qa, ka, va, qi, ki, wi, qb, kb, vb, zb, bb, ab) = jnp.split(proj, SPLIT_POINTS, axis=-1)
        qa = rope(qa.reshape(bsz, seq, A_HEADS, A_HEAD_DIM), pos)
        ka = rope(ka.reshape(bsz, seq, A_KV_HEADS, A_HEAD_DIM), pos)
        va = va.reshape(bsz, seq, A_KV_HEADS, A_HEAD_DIM)
        qi = rope(qi.reshape(bsz, seq, IDX_HEADS, IDX_DIM), pos)
        ki = rope(ki.reshape(bsz, seq, 1, IDX_DIM), pos)[:, :, 0]
        wi = wi * ((IDX_HEADS ** -0.5) * (IDX_DIM ** -0.5))
        o_a = sparse_indexer_attention(qa, ka, va, qi, ki, wi)
        o_b = gated_deltanet(qb, kb, vb, zb, bb, ab, conv_w[l], a_log[l], dt_bias[l], gdn_norm_g[l])
        x = x + jnp.concatenate([o_a, o_b], axis=-1) @ w_out[l]
        h = rms_norm(x, norm_ffn_g[l])
        x = x + jnp.square(jax.nn.relu(h @ w_ff1[l])) @ w_ff2[l]
    return rms_norm(x, norm_final_g)
```

```python
import functools

import jax
import jax.numpy as jnp
from jax import lax
from jax.experimental import pallas as pl
from jax.experimental.pallas import tpu as pltpu

F32 = jnp.float32
BF16 = jnp.bfloat16
I32 = jnp.int32

D_MODEL = 1024
CHUNK = 64
Q_BLOCK = 128
ROPE_THETA = 10000.0
EPS = 1e-6
A_HEADS = 8
A_KV_HEADS = 2
A_HEAD_DIM = 64
IDX_HEADS = 8
IDX_DIM = 64
TOPK_MAX = 256
B_HEADS = 4
B_HEAD_DIM = 128
CONV_WIDTH = 4
D_FF = 4 * D_MODEL

LANES = 128
A_WIDTH = A_HEADS * A_HEAD_DIM
KV_WIDTH = A_KV_HEADS * A_HEAD_DIM
B_WIDTH = B_HEADS * B_HEAD_DIM
GDN_WIDTH = 4 * B_WIDTH
SM_KI = 0
SM_WI = IDX_DIM
SM_BETA = SM_WI + IDX_HEADS
SM_DECAY = SM_BETA + B_HEADS
C_QA = 0
C_QI = C_QA + A_WIDTH
C_KV = C_QI + A_WIDTH
C_GDN = C_KV + 2 * KV_WIDTH
C_SM = C_GDN + GDN_WIDTH
IN_COLS = C_SM + LANES

VMEM_LIMIT = 56 * 1024 * 1024
NEG_BIG = -1e30
F32_INF_BITS = 0x7F800000


def _rms(x, g):
    return x * lax.rsqrt(jnp.mean(x * x, axis=-1, keepdims=True) + EPS) * g


def _dot(a, b):
    return jnp.dot(a, b, preferred_element_type=F32)


def _dot_nt(a, b):
    return lax.dot_general(a, b, (((1,), (1,)), ((), ())), preferred_element_type=F32)


def _inproj_body(x_ref, g_ref, w_ref, cos_ref, sin_ref,
                 qa_ref, qi_ref, kv_ref, gdn_ref, sm_ref):
    h = _rms(x_ref[...], g_ref[...]).astype(BF16)
    cos = cos_ref[...]
    sin = sin_ref[...]
    lane = lax.broadcasted_iota(I32, cos.shape, 1)
    first_half = (lane & (A_HEAD_DIM - 1)) < A_HEAD_DIM // 2

    def rope(t):
        swapped = jnp.where(first_half, pltpu.roll(t, LANES - A_HEAD_DIM // 2, 1),
                            pltpu.roll(t, A_HEAD_DIM // 2, 1))
        return t * cos + swapped * sin

    def proj(c0, width):
        return _dot(h, w_ref[:, c0:c0 + width])

    acc = proj(C_QA, A_WIDTH)
    for j in range(A_WIDTH // LANES):
        sl = slice(j * LANES, (j + 1) * LANES)
        qa_ref[:, sl] = (rope(acc[:, sl]) * (A_HEAD_DIM ** -0.5)).astype(BF16)
    acc = proj(C_QI, A_WIDTH)
    for j in range(A_WIDTH // LANES):
        sl = slice(j * LANES, (j + 1) * LANES)
        qi_ref[:, sl] = rope(acc[:, sl]).astype(BF16)
    acc = proj(C_KV, 2 * KV_WIDTH)
    kv_ref[:, 0:KV_WIDTH] = rope(acc[:, 0:KV_WIDTH]).astype(BF16)
    kv_ref[:, KV_WIDTH:] = acc[:, KV_WIDTH:].astype(BF16)
    for j in range(GDN_WIDTH // B_WIDTH):
        gdn_ref[:, j * B_WIDTH:(j + 1) * B_WIDTH] = proj(C_GDN + j * B_WIDTH, B_WIDTH)
    acc = proj(C_SM, LANES)
    sm_ref[...] = jnp.where(lane < IDX_DIM, rope(acc), acc)


def _inproj(x2, g, w, cos_t, sin_t, seq, tm):
    m = x2.shape[0]
    nt = seq // tm
    row = lambda i: (i, 0)
    const = lambda i: (0, 0)
    return pl.pallas_call(
        _inproj_body,
        grid=(m // tm,),
        in_specs=[
            pl.BlockSpec((tm, D_MODEL), row),
            pl.BlockSpec((1, D_MODEL), const),
            pl.BlockSpec((D_MODEL, IN_COLS), const, pipeline_mode=pl.Buffered(1)),
            pl.BlockSpec((tm, LANES), lambda i: (i % nt, 0)),
            pl.BlockSpec((tm, LANES), lambda i: (i % nt, 0)),
        ],
        out_specs=[
            pl.BlockSpec((tm, A_WIDTH), row),
            pl.BlockSpec((tm, A_WIDTH), row),
            pl.BlockSpec((tm, 2 * KV_WIDTH), row),
            pl.BlockSpec((tm, GDN_WIDTH), row),
            pl.BlockSpec((tm, LANES), row),
        ],
        out_shape=[
            jax.ShapeDtypeStruct((m, A_WIDTH), BF16),
            jax.ShapeDtypeStruct((m, A_WIDTH), BF16),
            jax.ShapeDtypeStruct((m, 2 * KV_WIDTH), BF16),
            jax.ShapeDtypeStruct((m, GDN_WIDTH), F32),
            jax.ShapeDtypeStruct((m, LANES), F32),
        ],
        compiler_params=pltpu.CompilerParams(
            dimension_semantics=("arbitrary",), vmem_limit_bytes=VMEM_LIMIT),
        name="inproj",
    )(x2, g, w, cos_t, sin_t)


def _mixer_a_body(qa_ref, qi_ref, kv_ref, sm_ref, o_ref,
                  isc_ref, vt_ref, qit_ref, qat_ref, acc_ref, *, seq, topk):
    j = pl.program_id(1)
    nkt = j + 1
    n_heads_pair = A_WIDTH // LANES
    group = A_HEADS // A_KV_HEADS
    gw = group * Q_BLOCK

    @pl.when(j == 0)
    def _():
        for t in range(seq // LANES):
            vt = kv_ref[t * LANES:(t + 1) * LANES, KV_WIDTH:].astype(F32)
            vt_ref[t] = vt.T.astype(BF16)

    for p in range(n_heads_pair):
        sl = slice(p * LANES, (p + 1) * LANES)
        t = qi_ref[:, sl].astype(F32).T
        qit_ref[:, (2 * p) * Q_BLOCK:(2 * p + 1) * Q_BLOCK] = t[0:IDX_DIM].astype(BF16)
        qit_ref[:, (2 * p + 1) * Q_BLOCK:(2 * p + 2) * Q_BLOCK] = t[IDX_DIM:].astype(BF16)
        t = qa_ref[:, sl].astype(F32).T
        qat_ref[:, (2 * p) * Q_BLOCK:(2 * p + 1) * Q_BLOCK] = t[0:A_HEAD_DIM].astype(BF16)
        qat_ref[:, (2 * p + 1) * Q_BLOCK:(2 * p + 2) * Q_BLOCK] = t[A_HEAD_DIM:].astype(BF16)

    q0 = pl.multiple_of(j * Q_BLOCK, Q_BLOCK)
    w_t = sm_ref[pl.ds(q0, Q_BLOCK), :].T[SM_WI:SM_WI + IDX_HEADS, :]
    w_t = w_t * ((IDX_HEADS ** -0.5) * (IDX_DIM ** -0.5))

    qlane = lax.broadcasted_iota(I32, (1, Q_BLOCK), 1)
    limit = q0 + jnp.where(qlane < CHUNK, CHUNK, 2 * CHUNK)
    row_iota = lax.broadcasted_iota(I32, (LANES, Q_BLOCK), 0)

    def key_index(kt):
        return kt * LANES + row_iota

    def isc_body(kt, carry):
        r0 = pl.multiple_of(kt * LANES, LANES)
        kid = sm_ref[pl.ds(r0, LANES), SM_KI:SM_KI + IDX_DIM].astype(BF16)
        rel = _dot(kid, qit_ref[...])
        acc = jnp.zeros((LANES, Q_BLOCK), F32)
        for h in range(IDX_HEADS):
            acc = acc + w_t[h:h + 1, :] * jnp.maximum(rel[:, h * Q_BLOCK:(h + 1) * Q_BLOCK], 0.0)
        isc_ref[pl.ds(r0, LANES), :] = jnp.where(key_index(kt) < limit, acc, -jnp.inf)
        return carry

    lax.fori_loop(0, nkt, isc_body, 0)

    def count(pred):
        def body(kt, acc):
            r0 = pl.multiple_of(kt * LANES, LANES)
            hit = jnp.where(pred(isc_ref[pl.ds(r0, LANES), :], key_index(kt)), 1.0, 0.0)
            return acc + hit.reshape(LANES // 8, 8, Q_BLOCK).sum(axis=0)
        acc = lax.fori_loop(0, nkt, body, jnp.zeros((8, Q_BLOCK), F32))
        return acc.sum(axis=0, keepdims=True)

    kf = float(topk)
    neg = count(lambda x, s: x >= 0.0) < kf

    def mag_body(i, mag):
        trial = mag | lax.shift_left(jnp.int32(1), 30 - i)
        tf = lax.bitcast_convert_type(trial, F32)
        thr = jnp.where(neg, -tf, tf)
        c = count(lambda x, s: x >= thr)
        keep = (neg & (c < kf) & (trial < F32_INF_BITS)) | (jnp.logical_not(neg) & (c >= kf))
        return jnp.where(keep, trial, mag)

    mag = lax.fori_loop(0, 31, mag_body, jnp.zeros((1, Q_BLOCK), I32))
    kth = jnp.where(neg, -lax.bitcast_convert_type(mag + 1, F32),
                    lax.bitcast_convert_type(mag, F32))
    need = kf - count(lambda x, s: x > kth)

    idx_bits = (seq - 1).bit_length()

    def tie_body(i, last):
        trial = last | lax.shift_left(jnp.int32(1), idx_bits - 1 - i)
        c = count(lambda x, s: (x == kth) & (s < trial))
        return jnp.where(c < need, trial, last)

    last = lax.fori_loop(0, idx_bits, tie_body, jnp.zeros((1, Q_BLOCK), I32))

    acc_ref[...] = jnp.zeros_like(acc_ref)

    def att_body(kt, carry):
        m_prev, l_prev = carry
        r0 = pl.multiple_of(kt * LANES, LANES)
        x = isc_ref[pl.ds(r0, LANES), :]
        s_idx = key_index(kt)
        sel = (s_idx < limit) & ((x > kth) | ((x == kth) & (s_idx <= last)))
        sel_g = jnp.concatenate([jnp.where(sel, 1.0, 0.0)] * group, axis=1) > 0.0
        k_tile = kv_ref[pl.ds(r0, LANES), 0:KV_WIDTH]
        v_t = vt_ref[kt]
        m_out, l_out = [], []
        for g in range(A_KV_HEADS):
            gs = slice(g * gw, (g + 1) * gw)
            s = _dot(k_tile[:, g * A_HEAD_DIM:(g + 1) * A_HEAD_DIM], qat_ref[:, gs])
            s = jnp.where(sel_g, s, NEG_BIG)
            m_new = jnp.maximum(m_prev[:, gs], jnp.max(s, axis=0, keepdims=True))
            alpha = jnp.exp(m_prev[:, gs] - m_new)
            p = jnp.where(sel_g, jnp.exp(s - m_new), 0.0)
            l_out.append(alpha * l_prev[:, gs] + jnp.sum(p, axis=0, keepdims=True))
            m_out.append(m_new)
            ds = slice(g * A_HEAD_DIM, (g + 1) * A_HEAD_DIM)
            acc_ref[ds, :] = alpha * acc_ref[ds, :] + _dot(v_t[ds, :], p.astype(BF16))
        return jnp.concatenate(m_out, axis=1), jnp.concatenate(l_out, axis=1)

    m0 = jnp.full((1, A_HEADS * Q_BLOCK), NEG_BIG, F32)
    l0 = jnp.zeros((1, A_HEADS * Q_BLOCK), F32)
    _, l_fin = lax.fori_loop(0, nkt, att_body, (m0, l0))

    for p in range(n_heads_pair):
        g = (2 * p) // group
        ds = slice(g * A_HEAD_DIM, (g + 1) * A_HEAD_DIM)
        parts = []
        for h in (2 * p, 2 * p + 1):
            hl = h % group
            parts.append(acc_ref[ds, hl * Q_BLOCK:(hl + 1) * Q_BLOCK]
                         / l_fin[:, h * Q_BLOCK:(h + 1) * Q_BLOCK])
        o_ref[:, p * LANES:(p + 1) * LANES] = jnp.concatenate(parts, axis=0).T.astype(BF16)


def _mixer_a(qa, qi, kv, sm, bsz, seq):
    nq = seq // Q_BLOCK
    topk = min(TOPK_MAX, seq // 4)
    qrow = lambda b, j: (b * nq + j, 0)
    brow = lambda b, j: (b, 0)
    return pl.pallas_call(
        functools.partial(_mixer_a_body, seq=seq, topk=topk),
        grid=(bsz, nq),
        in_specs=[
            pl.BlockSpec((Q_BLOCK, A_WIDTH), qrow),
            pl.BlockSpec((Q_BLOCK, A_WIDTH), qrow),
            pl.BlockSpec((seq, 2 * KV_WIDTH), brow),
            pl.BlockSpec((seq, LANES), brow),
        ],
        out_specs=pl.BlockSpec((Q_BLOCK, A_WIDTH), qrow),
        out_shape=jax.ShapeDtypeStruct((bsz * seq, A_WIDTH), BF16),
        scratch_shapes=[
            pltpu.VMEM((seq, Q_BLOCK), F32),
            pltpu.VMEM((seq // LANES, KV_WIDTH, LANES), BF16),
            pltpu.VMEM((IDX_DIM, IDX_HEADS * Q_BLOCK), BF16),
            pltpu.VMEM((A_HEAD_DIM, A_HEADS * Q_BLOCK), BF16),
            pltpu.VMEM((KV_WIDTH, (A_HEADS // A_KV_HEADS) * Q_BLOCK), F32),
        ],
        compiler_params=pltpu.CompilerParams(
            dimension_semantics=("arbitrary", "arbitrary"), vmem_limit_bytes=VMEM_LIMIT),
        name="mixer_a",
    )(qa, qi, kv, sm)


def _gdn_body(gdn_ref, sm_ref, cw_ref, alog_ref, dtb_ref, ng_ref, o_ref,
              xpad_ref, q_ref, k_ref, v_ref, gate_ref, gct_ref, state_ref, oc_ref, *, tc):
    tb = pl.program_id(1)
    n_chunks = tc // CHUNK
    conv_cols = 3 * B_WIDTH
    pad = 8

    @pl.when(tb == 0)
    def _():
        xpad_ref[0:pad, :] = jnp.zeros((pad, conv_cols), F32)
        state_ref[...] = jnp.zeros_like(state_ref)

    @pl.when(tb > 0)
    def _():
        xpad_ref[0:pad, :] = xpad_ref[tc:tc + pad, :]

    xpad_ref[pad:pad + tc, :] = gdn_ref[:, 0:conv_cols]

    for ct in range(conv_cols // LANES):
        cs = slice(ct * LANES, (ct + 1) * LANES)
        y = jnp.zeros((tc, LANES), F32)
        for jj in range(CONV_WIDTH):
            r = pad - (CONV_WIDTH - 1) + jj
            y = y + cw_ref[jj:jj + 1, cs] * xpad_ref[r:r + tc, cs]
        y = y * jax.nn.sigmoid(y)
        seg, hh = divmod(ct, B_HEADS)
        hs = slice(hh * LANES, (hh + 1) * LANES)
        if seg < 2:
            y = y * lax.rsqrt(jnp.sum(y * y, axis=-1, keepdims=True) + EPS)
        if seg == 0:
            q_ref[:, hs] = y * (B_HEAD_DIM ** -0.5)
        elif seg == 1:
            k_ref[:, hs] = y
        else:
            v_ref[:, hs] = y

    sm = sm_ref[...]
    beta = jax.nn.sigmoid(sm)
    z = sm + dtb_ref[...]
    softplus = jnp.maximum(z, 0.0) + jnp.log(1.0 + jnp.exp(-jnp.abs(z)))
    g = -jnp.exp(alog_ref[...]) * softplus
    rin = lax.broadcasted_iota(I32, (tc, LANES), 0) & (CHUNK - 1)
    gc = g
    step = 1
    while step < CHUNK:
        gc = gc + jnp.where(rin >= step, pltpu.roll(gc, step, 0), 0.0)
        step *= 2
    gc3 = gc.reshape(n_chunks, CHUNK, LANES)
    g_last = jnp.broadcast_to(gc3[:, CHUNK - 1:CHUNK, :], gc3.shape).reshape(tc, LANES)
    gate_ref[0] = beta
    gate_ref[1] = gc
    gate_ref[2] = jnp.exp(gc)
    gate_ref[3] = jnp.exp(g_last - gc)
    gate_ref[4] = jnp.exp(g_last)
    for i in range(tc // LANES):
        t = gc[i * LANES:(i + 1) * LANES, :].T
        for half in range(LANES // CHUNK):
            gct_ref[i * (LANES // CHUNK) + half] = t[:, half * CHUNK:(half + 1) * CHUNK]

    ci = lax.broadcasted_iota(I32, (CHUNK, CHUNK), 0)
    si = lax.broadcasted_iota(I32, (CHUNK, CHUNK), 1)
    eye = jnp.where(ci == si, 1.0, 0.0)

    def chunk_body(c, carry):
        r0 = pl.multiple_of(c * CHUNK, CHUNK)
        rows = pl.ds(r0, CHUNK)
        gct = gct_ref[c]
        for h in range(B_HEADS):
            hs = slice(h * LANES, (h + 1) * LANES)
            bl = slice(SM_BETA + h, SM_BETA + h + 1)
            dl = slice(SM_DECAY + h, SM_DECAY + h + 1)
            q = q_ref[rows, hs]
            k = k_ref[rows, hs]
            v = v_ref[rows, hs]
            beta_c = gate_ref[0, rows, bl]
            gc_c = gate_ref[1, rows, dl]
            eg_c = gate_ref[2, rows, dl]
            et_c = gate_ref[3, rows, dl]
            gl = gate_ref[4, rows, dl][0:1, :]
            d = gc_c - gct[SM_DECAY + h:SM_DECAY + h + 1, :]
            decay = jnp.where(ci >= si, jnp.exp(jnp.where(ci >= si, d, 0.0)), 0.0)
            kb = k * beta_c
            kq = _dot_nt(jnp.concatenate([kb, q], axis=0).astype(BF16), k.astype(BF16))
            n_mat = jnp.where(ci > si, -(kq[0:CHUNK] * decay), 0.0)
            att = kq[CHUNK:] * decay
            x_mat = eye + n_mat
            p_mat = n_mat
            pw = 1
            while 2 * pw < CHUNK:
                pb = p_mat.astype(BF16)
                p_mat = _dot(pb, pb)
                x_mat = x_mat + _dot(x_mat.astype(BF16), p_mat.astype(BF16))
                pw *= 2
            rhs = jnp.concatenate([v * beta_c, kb * eg_c], axis=1).astype(BF16)
            sol = _dot(x_mat.astype(BF16), rhs)
            u = sol[:, 0:B_HEAD_DIM]
            w = sol[:, B_HEAD_DIM:]
            s_prev = state_ref[h]
            wq = jnp.concatenate([w, q * eg_c], axis=0).astype(BF16)
            ws = _dot(wq, s_prev.astype(BF16))
            v_new = u - ws[0:CHUNK]
            vb = v_new.astype(BF16)
            oc_ref[rows, hs] = ws[CHUNK:] + _dot(att.astype(BF16), vb)
            k_tail_t = (k * et_c).T.astype(BF16)
            state_ref[h] = s_prev * gl + _dot(k_tail_t, vb)
        return carry

    lax.fori_loop(0, n_chunks, chunk_body, 0)

    for h in range(B_HEADS):
        hs = slice(h * LANES, (h + 1) * LANES)
        zf = gdn_ref[:, conv_cols + h * LANES:conv_cols + (h + 1) * LANES]
        o_ref[:, hs] = (_rms(oc_ref[:, hs], ng_ref[...]) * (zf * jax.nn.sigmoid(zf))).astype(BF16)


def _gdn(gdn_in, sm, conv_w, alog_row, dtb_row, norm_g, bsz, seq, tc):
    nt = seq // tc
    trow = lambda b, t: (b * nt + t, 0)
    const = lambda b, t: (0, 0)
    n_chunks = tc // CHUNK
    return pl.pallas_call(
        functools.partial(_gdn_body, tc=tc),
        grid=(bsz, nt),
        in_specs=[
            pl.BlockSpec((tc, GDN_WIDTH), trow),
            pl.BlockSpec((tc, LANES), trow),
            pl.BlockSpec((CONV_WIDTH, 3 * B_WIDTH), const),
            pl.BlockSpec((1, LANES), const),
            pl.BlockSpec((1, LANES), const),
            pl.BlockSpec((1, B_HEAD_DIM), const),
        ],
        out_specs=pl.BlockSpec((tc, B_WIDTH), trow),
        out_shape=jax.ShapeDtypeStruct((bsz * seq, B_WIDTH), BF16),
        scratch_shapes=[
            pltpu.VMEM((tc + 8, 3 * B_WIDTH), F32),
            pltpu.VMEM((tc, B_WIDTH), F32),
            pltpu.VMEM((tc, B_WIDTH), F32),
            pltpu.VMEM((tc, B_WIDTH), F32),
            pltpu.VMEM((5, tc, LANES), F32),
            pltpu.VMEM((n_chunks, LANES, CHUNK), F32),
            pltpu.VMEM((B_HEADS, B_HEAD_DIM, B_HEAD_DIM), F32),
            pltpu.VMEM((tc, B_WIDTH), F32),
        ],
        compiler_params=pltpu.CompilerParams(
            dimension_semantics=("arbitrary", "arbitrary"), vmem_limit_bytes=VMEM_LIMIT),
        name="gdn",
    )(gdn_in, sm, conv_w, alog_row, dtb_row, norm_g)


def _ffn_body(x_ref, oa_ref, ob_ref, wo_ref, g2_ref, w1_ref, w2_ref, g3_ref, out_ref, *,
              final_norm):
    y = (x_ref[...] + _dot(oa_ref[...], wo_ref[0:A_WIDTH, :])
         + _dot(ob_ref[...], wo_ref[A_WIDTH:, :]))
    h = _rms(y, g2_ref[...]).astype(BF16)
    a = jnp.square(jnp.maximum(_dot(h, w1_ref[...]), 0.0)).astype(BF16)
    acc = y + _dot(a, w2_ref[...])
    out_ref[...] = _rms(acc, g3_ref[...]) if final_norm else acc


def _ffn(x2, oa, ob, wo, g2, w1, w2, g3, tm, final_norm):
    m = x2.shape[0]
    row = lambda i: (i, 0)
    const = lambda i: (0, 0)
    resident = functools.partial(pl.BlockSpec, index_map=const, pipeline_mode=pl.Buffered(1))
    return pl.pallas_call(
        functools.partial(_ffn_body, final_norm=final_norm),
        grid=(m // tm,),
        in_specs=[
            pl.BlockSpec((tm, D_MODEL), row),
            pl.BlockSpec((tm, A_WIDTH), row),
            pl.BlockSpec((tm, B_WIDTH), row),
            resident((D_MODEL, D_MODEL)),
            pl.BlockSpec((1, D_MODEL), const),
            resident((D_MODEL, D_FF)),
            resident((D_FF, D_MODEL)),
            pl.BlockSpec((1, D_MODEL), const),
        ],
        out_specs=pl.BlockSpec((tm, D_MODEL), row),
        out_shape=jax.ShapeDtypeStruct((m, D_MODEL), F32),
        compiler_params=pltpu.CompilerParams(
            dimension_semantics=("arbitrary",), vmem_limit_bytes=VMEM_LIMIT),
        name="ffn",
    )(x2, oa, ob, wo, g2, w1, w2, g3)


def _rope_tables(seq):
    half = A_HEAD_DIM // 2
    inv_freq = 1.0 / (ROPE_THETA ** (jnp.arange(half, dtype=F32) / half))
    ang = jnp.arange(seq).astype(F32)[:, None] * inv_freq[None, :]
    cos = jnp.cos(ang)
    sin = jnp.sin(ang)
    reps = LANES // A_HEAD_DIM
    return (jnp.tile(cos, (1, 2 * reps)),
            jnp.tile(jnp.concatenate([-sin, sin], axis=1), (1, reps)))


def _permute_in_weight(w):
    sizes = (A_WIDTH, KV_WIDTH, KV_WIDTH, IDX_HEADS * IDX_DIM, IDX_DIM, IDX_HEADS,
             B_WIDTH, B_WIDTH, B_WIDTH, B_WIDTH, B_HEADS, B_HEADS)
    parts, off = [], 0
    for s in sizes:
        parts.append(w[:, off:off + s])
        off += s
    qa, ka, va, qi, ki, wi, qb, kb, vb, zb, bb, ab = parts
    pad = jnp.zeros((w.shape[0], LANES - (IDX_DIM + IDX_HEADS + 2 * B_HEADS)), w.dtype)
    return jnp.concatenate([qa, qi, ka, va, qb, kb, vb, zb, ki, wi, bb, ab, pad],
                           axis=1).astype(BF16)


def _lane_row(vals, offset):
    return jnp.zeros((1, LANES), F32).at[0, offset:offset + vals.shape[0]].set(vals.astype(F32))


def kernel(x, norm_mix_g, w_in, conv_w, a_log, dt_bias, gdn_norm_g, w_out,
           norm_ffn_g, w_ff1, w_ff2, norm_final_g):
    bsz, seq, d = x.shape
    depth = w_in.shape[0]
    m = bsz * seq
    cos_t, sin_t = _rope_tables(seq)
    x2 = x.reshape(m, d)
    for l in range(depth):
        qa, qi, kv, gdn_in, sm = _inproj(
            x2, norm_mix_g[l][None, :], _permute_in_weight(w_in[l]), cos_t, sin_t, seq, tm=512)
        o_a = _mixer_a(qa, qi, kv, sm, bsz, seq)
        o_b = _gdn(gdn_in, sm, conv_w[l], _lane_row(a_log[l], SM_DECAY),
                   _lane_row(dt_bias[l], SM_DECAY), gdn_norm_g[l][None, :], bsz, seq, tc=512)
        x2 = _ffn(x2, o_a, o_b, w_out[l].astype(BF16), norm_ffn_g[l][None, :],
                  w_ff1[l].astype(BF16), w_ff2[l].astype(BF16), norm_final_g[None, :],
                  tm=512, final_norm=(l == depth - 1))
    return x2.reshape(bsz, seq, d)
```

```python
import functools

import jax
import jax.numpy as jnp
from jax import lax
from jax.experimental import pallas as pl
from jax.experimental.pallas import tpu as pltpu

F32 = jnp.float32
BF16 = jnp.bfloat16
I32 = jnp.int32

D_MODEL = 1024
CHUNK = 64
Q_BLOCK = 128
ROPE_THETA = 10000.0
EPS = 1e-6
A_HEADS = 8
A_KV_HEADS = 2
A_HEAD_DIM = 64
IDX_HEADS = 8
IDX_DIM = 64
TOPK_MAX = 256
B_HEADS = 4
B_HEAD_DIM = 128
CONV_WIDTH = 4
D_FF = 4 * D_MODEL

LANES = 128
A_WIDTH = A_HEADS * A_HEAD_DIM
KV_WIDTH = A_KV_HEADS * A_HEAD_DIM
B_WIDTH = B_HEADS * B_HEAD_DIM
GDN_WIDTH = 4 * B_WIDTH
SM_KI = 0
SM_WI = IDX_DIM
SM_BETA = SM_WI + IDX_HEADS
SM_DECAY = SM_BETA + B_HEADS
C_QA = 0
C_QI = C_QA + A_WIDTH
C_KV = C_QI + A_WIDTH
C_GDN = C_KV + 2 * KV_WIDTH
C_SM = C_GDN + GDN_WIDTH
IN_COLS = C_SM + LANES

VMEM_LIMIT = 56 * 1024 * 1024
COUNT_GROUP = 4
NEG_BIG = -1e30
F32_INF_BITS = 0x7F800000


def _rms(x, g):
    return x * lax.rsqrt(jnp.mean(x * x, axis=-1, keepdims=True) + EPS) * g


def _dot(a, b):
    return jnp.dot(a, b, preferred_element_type=F32)


def _dot_nt(a, b):
    return lax.dot_general(a, b, (((1,), (1,)), ((), ())), preferred_element_type=F32)


def _inproj_body(x_ref, g_ref, w_ref, cos_ref, sin_ref,
                 qa_ref, qi_ref, kv_ref, gdn_ref, sm_ref):
    h = _rms(x_ref[...], g_ref[...]).astype(BF16)
    cos = cos_ref[...]
    sin = sin_ref[...]
    lane = lax.broadcasted_iota(I32, cos.shape, 1)
    first_half = (lane & (A_HEAD_DIM - 1)) < A_HEAD_DIM // 2

    def rope(t):
        swapped = jnp.where(first_half, pltpu.roll(t, LANES - A_HEAD_DIM // 2, 1),
                            pltpu.roll(t, A_HEAD_DIM // 2, 1))
        return t * cos + swapped * sin

    def proj(c0, width):
        return _dot(h, w_ref[:, c0:c0 + width])

    acc = proj(C_QA, A_WIDTH)
    for j in range(A_WIDTH // LANES):
        sl = slice(j * LANES, (j + 1) * LANES)
        qa_ref[:, sl] = (rope(acc[:, sl]) * (A_HEAD_DIM ** -0.5)).astype(BF16)
    acc = proj(C_QI, A_WIDTH)
    for j in range(A_WIDTH // LANES):
        sl = slice(j * LANES, (j + 1) * LANES)
        qi_ref[:, sl] = rope(acc[:, sl]).astype(BF16)
    acc = proj(C_KV, 2 * KV_WIDTH)
    kv_ref[:, 0:KV_WIDTH] = rope(acc[:, 0:KV_WIDTH]).astype(BF16)
    kv_ref[:, KV_WIDTH:] = acc[:, KV_WIDTH:].astype(BF16)
    for j in range(GDN_WIDTH // B_WIDTH):
        gdn_ref[:, j * B_WIDTH:(j + 1) * B_WIDTH] = proj(C_GDN + j * B_WIDTH, B_WIDTH)
    acc = proj(C_SM, LANES)
    sm_ref[...] = jnp.where(lane < IDX_DIM, rope(acc), acc)


def _inproj(x2, g, w, cos_t, sin_t, seq, tm):
    m = x2.shape[0]
    nt = seq // tm
    row = lambda i: (i, 0)
    const = lambda i: (0, 0)
    return pl.pallas_call(
        _inproj_body,
        grid=(m // tm,),
        in_specs=[
            pl.BlockSpec((tm, D_MODEL), row),
            pl.BlockSpec((1, D_MODEL), const),
            pl.BlockSpec((D_MODEL, IN_COLS), const, pipeline_mode=pl.Buffered(1)),
            pl.BlockSpec((tm, LANES), lambda i: (i % nt, 0)),
            pl.BlockSpec((tm, LANES), lambda i: (i % nt, 0)),
        ],
        out_specs=[
            pl.BlockSpec((tm, A_WIDTH), row),
            pl.BlockSpec((tm, A_WIDTH), row),
            pl.BlockSpec((tm, 2 * KV_WIDTH), row),
            pl.BlockSpec((tm, GDN_WIDTH), row),
            pl.BlockSpec((tm, LANES), row),
        ],
        out_shape=[
            jax.ShapeDtypeStruct((m, A_WIDTH), BF16),
            jax.ShapeDtypeStruct((m, A_WIDTH), BF16),
            jax.ShapeDtypeStruct((m, 2 * KV_WIDTH), BF16),
            jax.ShapeDtypeStruct((m, GDN_WIDTH), F32),
            jax.ShapeDtypeStruct((m, LANES), F32),
        ],
        compiler_params=pltpu.CompilerParams(
            dimension_semantics=("arbitrary",), vmem_limit_bytes=VMEM_LIMIT),
        name="inproj",
    )(x2, g, w, cos_t, sin_t)


def _mixer_a_body(qa_ref, qi_ref, kv_ref, sm_ref, o_ref,
                  isc_ref, vt_ref, qit_ref, qat_ref, acc_ref, *, seq, topk):
    j = pl.program_id(1)
    nkt = j + 1
    n_heads_pair = A_WIDTH // LANES
    group = A_HEADS // A_KV_HEADS
    gw = group * Q_BLOCK

    @pl.when(j == 0)
    def _():
        for t in range(seq // LANES):
            vt = kv_ref[t * LANES:(t + 1) * LANES, KV_WIDTH:].astype(F32)
            vt_ref[t] = vt.T.astype(BF16)

    for p in range(n_heads_pair):
        sl = slice(p * LANES, (p + 1) * LANES)
        t = qi_ref[:, sl].astype(F32).T
        qit_ref[:, (2 * p) * Q_BLOCK:(2 * p + 1) * Q_BLOCK] = t[0:IDX_DIM].astype(BF16)
        qit_ref[:, (2 * p + 1) * Q_BLOCK:(2 * p + 2) * Q_BLOCK] = t[IDX_DIM:].astype(BF16)
        t = qa_ref[:, sl].astype(F32).T
        qat_ref[:, (2 * p) * Q_BLOCK:(2 * p + 1) * Q_BLOCK] = t[0:A_HEAD_DIM].astype(BF16)
        qat_ref[:, (2 * p + 1) * Q_BLOCK:(2 * p + 2) * Q_BLOCK] = t[A_HEAD_DIM:].astype(BF16)

    q0 = pl.multiple_of(j * Q_BLOCK, Q_BLOCK)
    w_t = sm_ref[pl.ds(q0, Q_BLOCK), :].T[SM_WI:SM_WI + IDX_HEADS, :]
    w_t = w_t * ((IDX_HEADS ** -0.5) * (IDX_DIM ** -0.5))

    qlane = lax.broadcasted_iota(I32, (1, Q_BLOCK), 1)
    limit = q0 + jnp.where(qlane < CHUNK, CHUNK, 2 * CHUNK)
    row_iota = lax.broadcasted_iota(I32, (LANES, Q_BLOCK), 0)

    def key_index(kt):
        return kt * LANES + row_iota

    def isc_body(kt, carry):
        r0 = pl.multiple_of(kt * LANES, LANES)
        kid = sm_ref[pl.ds(r0, LANES), SM_KI:SM_KI + IDX_DIM].astype(BF16)
        rel = _dot(kid, qit_ref[...])
        acc = jnp.zeros((LANES, Q_BLOCK), F32)
        for h in range(IDX_HEADS):
            acc = acc + w_t[h:h + 1, :] * jnp.maximum(rel[:, h * Q_BLOCK:(h + 1) * Q_BLOCK], 0.0)
        isc_ref[pl.ds(r0, LANES), :] = jnp.where(key_index(kt) < limit, acc, -jnp.inf)
        return carry

    lax.fori_loop(0, nkt, isc_body, 0)

    ngrp = lax.shift_right_logical(nkt + (COUNT_GROUP - 1), COUNT_GROUP.bit_length() - 1)

    def fill_body(kt, carry):
        r0 = pl.multiple_of(kt * LANES, LANES)
        isc_ref[pl.ds(r0, LANES), :] = jnp.full((LANES, Q_BLOCK), -jnp.inf, F32)
        return carry

    lax.fori_loop(nkt, ngrp * COUNT_GROUP, fill_body, 0)

    def count(pred):
        def body(g, acc):
            for t in range(COUNT_GROUP):
                kt = g * COUNT_GROUP + t
                r0 = pl.multiple_of(kt * LANES, LANES)
                acc = jnp.where(pred(isc_ref[pl.ds(r0, LANES), :], key_index(kt)), acc + 1.0, acc)
            return acc
        acc = lax.fori_loop(0, ngrp, body, jnp.zeros((LANES, Q_BLOCK), F32))
        return acc.sum(axis=0, keepdims=True)

    kf = float(topk)
    neg = count(lambda x, s: x >= 0.0) < kf

    def mag_body(i, mag):
        trial = mag | lax.shift_left(jnp.int32(1), 30 - i)
        tf = lax.bitcast_convert_type(trial, F32)
        thr = jnp.where(neg, -tf, tf)
        c = count(lambda x, s: x >= thr)
        keep = (neg & (c < kf) & (trial < F32_INF_BITS)) | (jnp.logical_not(neg) & (c >= kf))
        return jnp.where(keep, trial, mag)

    mag = lax.fori_loop(0, 31, mag_body, jnp.zeros((1, Q_BLOCK), I32))
    kth = jnp.where(neg, -lax.bitcast_convert_type(mag + 1, F32),
                    lax.bitcast_convert_type(mag, F32))
    need = kf - count(lambda x, s: x > kth)

    idx_bits = (seq - 1).bit_length()

    def tie_body(i, last):
        trial = last | lax.shift_left(jnp.int32(1), idx_bits - 1 - i)
        c = count(lambda x, s: (x == kth) & (s < trial))
        return jnp.where(c < need, trial, last)

    def tie_search():
        return lax.fori_loop(0, idx_bits, tie_body, jnp.zeros((1, Q_BLOCK), I32))

    def take_all_ties():
        return jnp.full((1, Q_BLOCK), (1 << idx_bits) - 1, I32)

    n_ge = count(lambda x, s: x >= kth)
    last = lax.cond(jnp.max(n_ge) > kf, tie_search, take_all_ties)

    acc_ref[...] = jnp.zeros_like(acc_ref)

    def att_body(kt, carry):
        m_prev, l_prev = carry
        r0 = pl.multiple_of(kt * LANES, LANES)
        x = isc_ref[pl.ds(r0, LANES), :]
        s_idx = key_index(kt)
        sel = (s_idx < limit) & ((x > kth) | ((x == kth) & (s_idx <= last)))
        sel_g = jnp.concatenate([jnp.where(sel, 1.0, 0.0)] * group, axis=1) > 0.0
        k_tile = kv_ref[pl.ds(r0, LANES), 0:KV_WIDTH]
        v_t = vt_ref[kt]
        groups = range(A_KV_HEADS)
        gs = [slice(g * gw, (g + 1) * gw) for g in groups]
        ds = [slice(g * A_HEAD_DIM, (g + 1) * A_HEAD_DIM) for g in groups]
        s = [_dot(k_tile[:, ds[g]], qat_ref[:, gs[g]]) for g in groups]
        s = [jnp.where(sel_g, s[g], NEG_BIG) for g in groups]
        m_new = [jnp.maximum(m_prev[:, gs[g]], jnp.max(s[g], axis=0, keepdims=True))
                 for g in groups]
        alpha = [jnp.exp(m_prev[:, gs[g]] - m_new[g]) for g in groups]
        p = [jnp.where(sel_g, jnp.exp(s[g] - m_new[g]), 0.0) for g in groups]
        pv = [_dot(v_t[ds[g], :], p[g].astype(BF16)) for g in groups]
        l_new = [alpha[g] * l_prev[:, gs[g]] + jnp.sum(p[g], axis=0, keepdims=True)
                 for g in groups]
        for g in groups:
            acc_ref[ds[g], :] = alpha[g] * acc_ref[ds[g], :] + pv[g]
        return jnp.concatenate(m_new, axis=1), jnp.concatenate(l_new, axis=1)

    m0 = jnp.full((1, A_HEADS * Q_BLOCK), NEG_BIG, F32)
    l0 = jnp.zeros((1, A_HEADS * Q_BLOCK), F32)
    _, l_fin = lax.fori_loop(0, nkt, att_body, (m0, l0))

    for p in range(n_heads_pair):
        g = (2 * p) // group
        ds = slice(g * A_HEAD_DIM, (g + 1) * A_HEAD_DIM)
        parts = []
        for h in (2 * p, 2 * p + 1):
            hl = h % group
            parts.append(acc_ref[ds, hl * Q_BLOCK:(hl + 1) * Q_BLOCK]
                         / l_fin[:, h * Q_BLOCK:(h + 1) * Q_BLOCK])
        o_ref[:, p * LANES:(p + 1) * LANES] = jnp.concatenate(parts, axis=0).T.astype(BF16)


def _mixer_a(qa, qi, kv, sm, bsz, seq):
    nq = seq // Q_BLOCK
    assert seq % (COUNT_GROUP * LANES) == 0, seq
    topk = min(TOPK_MAX, seq // 4)
    qrow = lambda b, j: (b * nq + j, 0)
    brow = lambda b, j: (b, 0)
    return pl.pallas_call(
        functools.partial(_mixer_a_body, seq=seq, topk=topk),
        grid=(bsz, nq),
        in_specs=[
            pl.BlockSpec((Q_BLOCK, A_WIDTH), qrow),
            pl.BlockSpec((Q_BLOCK, A_WIDTH), qrow),
            pl.BlockSpec((seq, 2 * KV_WIDTH), brow),
            pl.BlockSpec((seq, LANES), brow),
        ],
        out_specs=pl.BlockSpec((Q_BLOCK, A_WIDTH), qrow),
        out_shape=jax.ShapeDtypeStruct((bsz * seq, A_WIDTH), BF16),
        scratch_shapes=[
            pltpu.VMEM((seq, Q_BLOCK), F32),
            pltpu.VMEM((seq // LANES, KV_WIDTH, LANES), BF16),
            pltpu.VMEM((IDX_DIM, IDX_HEADS * Q_BLOCK), BF16),
            pltpu.VMEM((A_HEAD_DIM, A_HEADS * Q_BLOCK), BF16),
            pltpu.VMEM((KV_WIDTH, (A_HEADS // A_KV_HEADS) * Q_BLOCK), F32),
        ],
        compiler_params=pltpu.CompilerParams(
            dimension_semantics=("arbitrary", "arbitrary"), vmem_limit_bytes=VMEM_LIMIT),
        name="mixer_a",
    )(qa, qi, kv, sm)


def _gdn_body(gdn_ref, sm_ref, cw_ref, alog_ref, dtb_ref, ng_ref, o_ref,
              xpad_ref, q_ref, k_ref, v_ref, gate_ref, gct_ref, state_ref, oc_ref,
              u_ref, wq_ref, ak_ref, *, tc):
    tb = pl.program_id(1)
    n_chunks = tc // CHUNK
    prep_unroll = 4
    conv_cols = 3 * B_WIDTH
    pad = 8

    @pl.when(tb == 0)
    def _():
        xpad_ref[0:pad, :] = jnp.zeros((pad, conv_cols), F32)
        state_ref[...] = jnp.zeros_like(state_ref)

    @pl.when(tb > 0)
    def _():
        xpad_ref[0:pad, :] = xpad_ref[tc:tc + pad, :]

    xpad_ref[pad:pad + tc, :] = gdn_ref[:, 0:conv_cols]

    for ct in range(conv_cols // LANES):
        cs = slice(ct * LANES, (ct + 1) * LANES)
        y = jnp.zeros((tc, LANES), F32)
        for jj in range(CONV_WIDTH):
            r = pad - (CONV_WIDTH - 1) + jj
            y = y + cw_ref[jj:jj + 1, cs] * xpad_ref[r:r + tc, cs]
        y = y * jax.nn.sigmoid(y)
        seg, hh = divmod(ct, B_HEADS)
        hs = slice(hh * LANES, (hh + 1) * LANES)
        if seg < 2:
            y = y * lax.rsqrt(jnp.sum(y * y, axis=-1, keepdims=True) + EPS)
        if seg == 0:
            q_ref[:, hs] = y * (B_HEAD_DIM ** -0.5)
        elif seg == 1:
            k_ref[:, hs] = y
        else:
            v_ref[:, hs] = y

    sm = sm_ref[...]
    beta = jax.nn.sigmoid(sm)
    z = sm + dtb_ref[...]
    softplus = jnp.maximum(z, 0.0) + jnp.log(1.0 + jnp.exp(-jnp.abs(z)))
    g = -jnp.exp(alog_ref[...]) * softplus
    rin = lax.broadcasted_iota(I32, (tc, LANES), 0) & (CHUNK - 1)
    gc = g
    step = 1
    while step < CHUNK:
        gc = gc + jnp.where(rin >= step, pltpu.roll(gc, step, 0), 0.0)
        step *= 2
    gc3 = gc.reshape(n_chunks, CHUNK, LANES)
    g_last = jnp.broadcast_to(gc3[:, CHUNK - 1:CHUNK, :], gc3.shape).reshape(tc, LANES)
    gate_ref[0] = beta
    gate_ref[1] = gc
    gate_ref[2] = jnp.exp(gc)
    gate_ref[3] = jnp.exp(g_last - gc)
    gate_ref[4] = jnp.exp(g_last)
    for i in range(tc // LANES):
        t = gc[i * LANES:(i + 1) * LANES, :].T
        for half in range(LANES // CHUNK):
            gct_ref[i * (LANES // CHUNK) + half] = t[:, half * CHUNK:(half + 1) * CHUNK]

    ci = lax.broadcasted_iota(I32, (CHUNK, CHUNK), 0)
    si = lax.broadcasted_iota(I32, (CHUNK, CHUNK), 1)
    wl = lax.broadcasted_iota(I32, (CHUNK, 2 * CHUNK), 1)
    wr = lax.broadcasted_iota(I32, (CHUNK, 2 * CHUNK), 0)
    right = wl >= CHUNK
    eye_right = jnp.where(wl == wr + CHUNK, 1.0, 0.0)

    def prep_body(cg, carry):
        units = [(cg * prep_unroll + cc, h) for cc in range(prep_unroll) for h in range(B_HEADS)]
        rows = [pl.ds(pl.multiple_of(c * CHUNK, CHUNK), CHUNK) for c, _ in units]
        hsl = [slice(h * LANES, (h + 1) * LANES) for _, h in units]
        idx = range(len(units))

        def col(i, gate, off):
            h = units[i][1]
            return gate_ref[gate, rows[i], off + h:off + h + 1]

        q = [q_ref[rows[i], hsl[i]] for i in idx]
        k = [k_ref[rows[i], hsl[i]] for i in idx]
        kb = [k[i] * col(i, 0, SM_BETA) for i in idx]
        kq = [_dot_nt(jnp.concatenate([kb[i], q[i]], axis=0).astype(BF16), k[i].astype(BF16))
              for i in idx]
        decay = []
        for i, (c, h) in enumerate(units):
            d = col(i, 1, SM_DECAY) - gct_ref[c][SM_DECAY + h:SM_DECAY + h + 1, :]
            decay.append(jnp.where(ci >= si, jnp.exp(jnp.where(ci >= si, d, 0.0)), 0.0))
        wmat = []
        for i, (c, h) in enumerate(units):
            n_mat = jnp.where(ci > si, -(kq[i][0:CHUNK] * decay[i]), 0.0)
            wmat.append(jnp.concatenate([n_mat, jnp.zeros_like(n_mat)], axis=1) + eye_right)
            ak_ref[c, h, 0:CHUNK, :] = (kq[i][CHUNK:] * decay[i]).astype(BF16)
        pw = 1
        while pw < CHUNK:
            wb = [wmat[i].astype(BF16) for i in idx]
            wmat = [_dot(wb[i][:, 0:CHUNK], wb[i]) + jnp.where(right, wmat[i], 0.0) for i in idx]
            pw *= 2
        eg = [col(i, 2, SM_DECAY) for i in idx]
        rhs = [jnp.concatenate([v_ref[rows[i], hsl[i]] * col(i, 0, SM_BETA), kb[i] * eg[i]],
                               axis=1).astype(BF16) for i in idx]
        sol = [_dot(wmat[i][:, CHUNK:].astype(BF16), rhs[i]) for i in idx]
        for i, (c, h) in enumerate(units):
            u_ref[rows[i], hsl[i]] = sol[i][:, 0:B_HEAD_DIM]
            wq_ref[c, h] = jnp.concatenate([sol[i][:, B_HEAD_DIM:], q[i] * eg[i]],
                                           axis=0).astype(BF16)
            ak_ref[c, h, CHUNK:, :] = (k[i] * col(i, 3, SM_DECAY)).T.astype(BF16)
        return carry

    lax.fori_loop(0, n_chunks // prep_unroll, prep_body, 0)

    def scan_body(c, carry):
        r0 = pl.multiple_of(c * CHUNK, CHUNK)
        rows = pl.ds(r0, CHUNK)
        heads = range(B_HEADS)
        hsl = [slice(h * LANES, (h + 1) * LANES) for h in heads]
        s_prev = [state_ref[h] for h in heads]
        ws = [_dot(wq_ref[c, h], s_prev[h].astype(BF16)) for h in heads]
        v_new = [(u_ref[rows, hsl[h]] - ws[h][0:CHUNK]).astype(BF16) for h in heads]
        r = [_dot(ak_ref[c, h], v_new[h]) for h in heads]
        for h in heads:
            gl = gate_ref[4, rows, SM_DECAY + h:SM_DECAY + h + 1][0:1, :]
            oc_ref[rows, hsl[h]] = ws[h][CHUNK:] + r[h][0:CHUNK]
            state_ref[h] = s_prev[h] * gl + r[h][CHUNK:]
        return carry

    lax.fori_loop(0, n_chunks, scan_body, 0)

    for h in range(B_HEADS):
        hs = slice(h * LANES, (h + 1) * LANES)
        zf = gdn_ref[:, conv_cols + h * LANES:conv_cols + (h + 1) * LANES]
        o_ref[:, hs] = (_rms(oc_ref[:, hs], ng_ref[...]) * (zf * jax.nn.sigmoid(zf))).astype(BF16)


def _gdn(gdn_in, sm, conv_w, alog_row, dtb_row, norm_g, bsz, seq, tc):
    nt = seq // tc
    trow = lambda b, t: (b * nt + t, 0)
    const = lambda b, t: (0, 0)
    n_chunks = tc // CHUNK
    return pl.pallas_call(
        functools.partial(_gdn_body, tc=tc),
        grid=(bsz, nt),
        in_specs=[
            pl.BlockSpec((tc, GDN_WIDTH), trow),
            pl.BlockSpec((tc, LANES), trow),
            pl.BlockSpec((CONV_WIDTH, 3 * B_WIDTH), const),
            pl.BlockSpec((1, LANES), const),
            pl.BlockSpec((1, LANES), const),
            pl.BlockSpec((1, B_HEAD_DIM), const),
        ],
        out_specs=pl.BlockSpec((tc, B_WIDTH), trow),
        out_shape=jax.ShapeDtypeStruct((bsz * seq, B_WIDTH), BF16),
        scratch_shapes=[
            pltpu.VMEM((tc + 8, 3 * B_WIDTH), F32),
            pltpu.VMEM((tc, B_WIDTH), F32),
            pltpu.VMEM((tc, B_WIDTH), F32),
            pltpu.VMEM((tc, B_WIDTH), F32),
            pltpu.VMEM((5, tc, LANES), F32),
            pltpu.VMEM((n_chunks, LANES, CHUNK), F32),
            pltpu.VMEM((B_HEADS, B_HEAD_DIM, B_HEAD_DIM), F32),
            pltpu.VMEM((tc, B_WIDTH), F32),
            pltpu.VMEM((tc, B_WIDTH), F32),
            pltpu.VMEM((n_chunks, B_HEADS, 2 * CHUNK, B_HEAD_DIM), BF16),
            pltpu.VMEM((n_chunks, B_HEADS, CHUNK + B_HEAD_DIM, CHUNK), BF16),
        ],
        compiler_params=pltpu.CompilerParams(
            dimension_semantics=("arbitrary", "arbitrary"), vmem_limit_bytes=VMEM_LIMIT),
        name="gdn",
    )(gdn_in, sm, conv_w, alog_row, dtb_row, norm_g)


def _ffn_body(x_ref, oa_ref, ob_ref, wo_ref, g2_ref, w1_ref, w2_ref, g3_ref, out_ref, *,
              final_norm):
    y = (x_ref[...] + _dot(oa_ref[...], wo_ref[0:A_WIDTH, :])
         + _dot(ob_ref[...], wo_ref[A_WIDTH:, :]))
    h = _rms(y, g2_ref[...]).astype(BF16)
    a = jnp.square(jnp.maximum(_dot(h, w1_ref[...]), 0.0)).astype(BF16)
    acc = y + _dot(a, w2_ref[...])
    out_ref[...] = _rms(acc, g3_ref[...]) if final_norm else acc


def _ffn(x2, oa, ob, wo, g2, w1, w2, g3, tm, final_norm):
    m = x2.shape[0]
    row = lambda i: (i, 0)
    const = lambda i: (0, 0)
    resident = functools.partial(pl.BlockSpec, index_map=const, pipeline_mode=pl.Buffered(1))
    return pl.pallas_call(
        functools.partial(_ffn_body, final_norm=final_norm),
        grid=(m // tm,),
        in_specs=[
            pl.BlockSpec((tm, D_MODEL), row),
            pl.BlockSpec((tm, A_WIDTH), row),
            pl.BlockSpec((tm, B_WIDTH), row),
            resident((D_MODEL, D_MODEL)),
            pl.BlockSpec((1, D_MODEL), const),
            resident((D_MODEL, D_FF)),
            resident((D_FF, D_MODEL)),
            pl.BlockSpec((1, D_MODEL), const),
        ],
        out_specs=pl.BlockSpec((tm, D_MODEL), row),
        out_shape=jax.ShapeDtypeStruct((m, D_MODEL), F32),
        compiler_params=pltpu.CompilerParams(
            dimension_semantics=("arbitrary",), vmem_limit_bytes=VMEM_LIMIT),
        name="ffn",
    )(x2, oa, ob, wo, g2, w1, w2, g3)


def _rope_tables(seq):
    half = A_HEAD_DIM // 2
    inv_freq = 1.0 / (ROPE_THETA ** (jnp.arange(half, dtype=F32) / half))
    ang = jnp.arange(seq).astype(F32)[:, None] * inv_freq[None, :]
    cos = jnp.cos(ang)
    sin = jnp.sin(ang)
    reps = LANES // A_HEAD_DIM
    return (jnp.tile(cos, (1, 2 * reps)),
            jnp.tile(jnp.concatenate([-sin, sin], axis=1), (1, reps)))


def _permute_in_weight(w):
    sizes = (A_WIDTH, KV_WIDTH, KV_WIDTH, IDX_HEADS * IDX_DIM, IDX_DIM, IDX_HEADS,
             B_WIDTH, B_WIDTH, B_WIDTH, B_WIDTH, B_HEADS, B_HEADS)
    parts, off = [], 0
    for s in sizes:
        parts.append(w[:, off:off + s])
        off += s
    qa, ka, va, qi, ki, wi, qb, kb, vb, zb, bb, ab = parts
    pad = jnp.zeros((w.shape[0], LANES - (IDX_DIM + IDX_HEADS + 2 * B_HEADS)), w.dtype)
    return jnp.concatenate([qa, qi, ka, va, qb, kb, vb, zb, ki, wi, bb, ab, pad],
                           axis=1).astype(BF16)


def _lane_row(vals, offset):
    return jnp.zeros((1, LANES), F32).at[0, offset:offset + vals.shape[0]].set(vals.astype(F32))


def kernel(x, norm_mix_g, w_in, conv_w, a_log, dt_bias, gdn_norm_g, w_out,
           norm_ffn_g, w_ff1, w_ff2, norm_final_g):
    bsz, seq, d = x.shape
    depth = w_in.shape[0]
    m = bsz * seq
    cos_t, sin_t = _rope_tables(seq)
    x2 = x.reshape(m, d)
    for l in range(depth):
        qa, qi, kv, gdn_in, sm = _inproj(
            x2, norm_mix_g[l][None, :], _permute_in_weight(w_in[l]), cos_t, sin_t, seq, tm=512)
        o_a = _mixer_a(qa, qi, kv, sm, bsz, seq)
        o_b = _gdn(gdn_in, sm, conv_w[l], _lane_row(a_log[l], SM_DECAY),
                   _lane_row(dt_bias[l], SM_DECAY), gdn_norm_g[l][None, :], bsz, seq, tc=512)
        x2 = _ffn(x2, o_a, o_b, w_out[l].astype(BF16), norm_ffn_g[l][None, :],
                  w_ff1[l].astype(BF16), w_ff2[l].astype(BF16), norm_final_g[None, :],
                  tm=512, final_norm=(l == depth - 1))
    return x2.reshape(bsz, seq, d)
```

```python
import functools

import jax
import jax.numpy as jnp
from jax import lax
from jax.experimental import pallas as pl
from jax.experimental.pallas import tpu as pltpu

F32 = jnp.float32
BF16 = jnp.bfloat16
I32 = jnp.int32

D_MODEL = 1024
CHUNK = 64
Q_BLOCK = 128
ROPE_THETA = 10000.0
EPS = 1e-6
A_HEADS = 8
A_KV_HEADS = 2
A_HEAD_DIM = 64
IDX_HEADS = 8
IDX_DIM = 64
TOPK_MAX = 256
B_HEADS = 4
B_HEAD_DIM = 128
CONV_WIDTH = 4
D_FF = 4 * D_MODEL

LANES = 128
A_WIDTH = A_HEADS * A_HEAD_DIM
KV_WIDTH = A_KV_HEADS * A_HEAD_DIM
B_WIDTH = B_HEADS * B_HEAD_DIM
GDN_WIDTH = 4 * B_WIDTH
SM_KI = 0
SM_WI = IDX_DIM
SM_BETA = SM_WI + IDX_HEADS
SM_DECAY = SM_BETA + B_HEADS
C_QA = 0
C_QI = C_QA + A_WIDTH
C_KV = C_QI + A_WIDTH
C_GDN = C_KV + 2 * KV_WIDTH
C_SM = C_GDN + GDN_WIDTH
IN_COLS = C_SM + LANES

VMEM_LIMIT = 56 * 1024 * 1024
COUNT_GROUP = 4
KEY_BLOCK = 2 * LANES
NEG_BIG = -1e30
NEG_INF_KEY = -(2 ** 31) + 0x7FFFFF


def _rms(x, g):
    return x * lax.rsqrt(jnp.mean(x * x, axis=-1, keepdims=True) + EPS) * g


def _dot(a, b):
    return jnp.dot(a, b, preferred_element_type=F32)


def _dot_nt(a, b):
    return lax.dot_general(a, b, (((1,), (1,)), ((), ())), preferred_element_type=F32)


def _inproj_body(x_ref, g_ref, w_ref, cos_ref, sin_ref,
                 qa_ref, qi_ref, kv_ref, gdn_ref, sm_ref):
    h = _rms(x_ref[...], g_ref[...]).astype(BF16)
    cos = cos_ref[...]
    sin = sin_ref[...]
    lane = lax.broadcasted_iota(I32, cos.shape, 1)
    first_half = (lane & (A_HEAD_DIM - 1)) < A_HEAD_DIM // 2

    def rope(t):
        swapped = jnp.where(first_half, pltpu.roll(t, LANES - A_HEAD_DIM // 2, 1),
                            pltpu.roll(t, A_HEAD_DIM // 2, 1))
        return t * cos + swapped * sin

    def proj(c0, width):
        return _dot(h, w_ref[:, c0:c0 + width])

    acc = proj(C_QA, A_WIDTH)
    for j in range(A_WIDTH // LANES):
        sl = slice(j * LANES, (j + 1) * LANES)
        qa_ref[:, sl] = (rope(acc[:, sl]) * (A_HEAD_DIM ** -0.5)).astype(BF16)
    acc = proj(C_QI, A_WIDTH)
    for j in range(A_WIDTH // LANES):
        sl = slice(j * LANES, (j + 1) * LANES)
        qi_ref[:, sl] = rope(acc[:, sl]).astype(BF16)
    acc = proj(C_KV, 2 * KV_WIDTH)
    kv_ref[:, 0:KV_WIDTH] = rope(acc[:, 0:KV_WIDTH]).astype(BF16)
    kv_ref[:, KV_WIDTH:] = acc[:, KV_WIDTH:].astype(BF16)
    for j in range(GDN_WIDTH // B_WIDTH):
        gdn_ref[:, j * B_WIDTH:(j + 1) * B_WIDTH] = proj(C_GDN + j * B_WIDTH, B_WIDTH)
    acc = proj(C_SM, LANES)
    sm_ref[...] = jnp.where(lane < IDX_DIM, rope(acc), acc)


def _inproj(x2, g, w, cos_t, sin_t, seq, tm):
    m = x2.shape[0]
    nt = seq // tm
    row = lambda i: (i, 0)
    const = lambda i: (0, 0)
    return pl.pallas_call(
        _inproj_body,
        grid=(m // tm,),
        in_specs=[
            pl.BlockSpec((tm, D_MODEL), row),
            pl.BlockSpec((1, D_MODEL), const),
            pl.BlockSpec((D_MODEL, IN_COLS), const, pipeline_mode=pl.Buffered(1)),
            pl.BlockSpec((tm, LANES), lambda i: (i % nt, 0)),
            pl.BlockSpec((tm, LANES), lambda i: (i % nt, 0)),
        ],
        out_specs=[
            pl.BlockSpec((tm, A_WIDTH), row),
            pl.BlockSpec((tm, A_WIDTH), row),
            pl.BlockSpec((tm, 2 * KV_WIDTH), row),
            pl.BlockSpec((tm, GDN_WIDTH), row),
            pl.BlockSpec((tm, LANES), row),
        ],
        out_shape=[
            jax.ShapeDtypeStruct((m, A_WIDTH), BF16),
            jax.ShapeDtypeStruct((m, A_WIDTH), BF16),
            jax.ShapeDtypeStruct((m, 2 * KV_WIDTH), BF16),
            jax.ShapeDtypeStruct((m, GDN_WIDTH), F32),
            jax.ShapeDtypeStruct((m, LANES), F32),
        ],
        compiler_params=pltpu.CompilerParams(
            dimension_semantics=("arbitrary",), vmem_limit_bytes=VMEM_LIMIT),
        name="inproj",
    )(x2, g, w, cos_t, sin_t)


def _mixer_a_body(qa_ref, qi_ref, kv_ref, sm_ref, o_ref,
                  isc_ref, vt_ref, qit_ref, qat_ref, acc_ref, *, seq, topk):
    j = pl.program_id(1)
    nkt = j + 1
    n_heads_pair = A_WIDTH // LANES
    group = A_HEADS // A_KV_HEADS
    gw = group * Q_BLOCK

    @pl.when(j == 0)
    def _():
        per_block = KEY_BLOCK // LANES
        for t in range(seq // LANES):
            vt = kv_ref[t * LANES:(t + 1) * LANES, KV_WIDTH:].astype(F32)
            c0 = (t % per_block) * LANES
            vt_ref[t // per_block, :, c0:c0 + LANES] = vt.T.astype(BF16)

    for p in range(n_heads_pair):
        sl = slice(p * LANES, (p + 1) * LANES)
        t = qi_ref[:, sl].astype(F32).T
        qit_ref[:, (2 * p) * Q_BLOCK:(2 * p + 1) * Q_BLOCK] = t[0:IDX_DIM].astype(BF16)
        qit_ref[:, (2 * p + 1) * Q_BLOCK:(2 * p + 2) * Q_BLOCK] = t[IDX_DIM:].astype(BF16)
        t = qa_ref[:, sl].astype(F32).T
        qat_ref[:, (2 * p) * Q_BLOCK:(2 * p + 1) * Q_BLOCK] = t[0:A_HEAD_DIM].astype(BF16)
        qat_ref[:, (2 * p + 1) * Q_BLOCK:(2 * p + 2) * Q_BLOCK] = t[A_HEAD_DIM:].astype(BF16)

    q0 = pl.multiple_of(j * Q_BLOCK, Q_BLOCK)
    w_t = sm_ref[pl.ds(q0, Q_BLOCK), :].T[SM_WI:SM_WI + IDX_HEADS, :]
    w_t = w_t * ((IDX_HEADS ** -0.5) * (IDX_DIM ** -0.5))

    qlane = lax.broadcasted_iota(I32, (1, Q_BLOCK), 1)
    limit = q0 + jnp.where(qlane < CHUNK, CHUNK, 2 * CHUNK)
    tile_iota = lax.broadcasted_iota(I32, (LANES, Q_BLOCK), 0)
    block_iota = lax.broadcasted_iota(I32, (KEY_BLOCK, Q_BLOCK), 0)
    nkb = lax.shift_right_logical(nkt + (KEY_BLOCK // LANES - 1), (KEY_BLOCK // LANES).bit_length() - 1)

    def isc_body(kb, carry):
        r0 = pl.multiple_of(kb * KEY_BLOCK, KEY_BLOCK)
        kid = sm_ref[pl.ds(r0, KEY_BLOCK), SM_KI:SM_KI + IDX_DIM].astype(BF16)
        rel = _dot(kid, qit_ref[...])
        acc = jnp.zeros((KEY_BLOCK, Q_BLOCK), F32)
        for h in range(IDX_HEADS):
            acc = acc + w_t[h:h + 1, :] * jnp.maximum(rel[:, h * Q_BLOCK:(h + 1) * Q_BLOCK], 0.0)
        isc_ref[pl.ds(r0, KEY_BLOCK), :] = jnp.where(r0 + block_iota < limit, acc, -jnp.inf)
        return carry

    lax.fori_loop(0, nkb, isc_body, 0)

    ngrp = lax.shift_right_logical(nkt + (COUNT_GROUP - 1), COUNT_GROUP.bit_length() - 1)

    def fill_body(kt, carry):
        r0 = pl.multiple_of(kt * LANES, LANES)
        isc_ref[pl.ds(r0, LANES), :] = jnp.full((LANES, Q_BLOCK), -jnp.inf, F32)
        return carry

    lax.fori_loop(nkb * (KEY_BLOCK // LANES), ngrp * COUNT_GROUP, fill_body, 0)

    def count(pred):
        def body(g, acc):
            for t in range(COUNT_GROUP):
                r0 = pl.multiple_of((g * COUNT_GROUP + t) * LANES, LANES)
                acc = jnp.where(pred(isc_ref[pl.ds(r0, LANES), :], r0 + tile_iota), acc + 1.0, acc)
            return acc
        acc = lax.fori_loop(0, ngrp, body, jnp.zeros((LANES, Q_BLOCK), F32))
        return acc.sum(axis=0, keepdims=True)

    kf = float(topk)

    def thr_of(key):
        bits = jnp.where(key >= 0, key, key ^ jnp.int32(0x7FFFFFFF))
        return lax.bitcast_convert_type(bits, F32)

    c0 = count(lambda x, s: x >= 0.0)
    lo0 = jnp.where(c0 >= kf, jnp.int32(0), jnp.int32(-2 ** 31))
    c_lo0 = jnp.where(c0 >= kf, c0, -1.0)

    def bis_body(i, st):
        lo, c_lo = st
        trial = lo | lax.shift_left(jnp.int32(1), 30 - i)
        thr = thr_of(trial)
        c = count(lambda x, s: x >= thr)
        ok = c >= kf
        return jnp.where(ok, trial, lo), jnp.where(ok, c, c_lo)

    lo, c_lo = lax.fori_loop(0, 31, bis_body, (lo0, c_lo0))
    kth = jnp.where(lo < jnp.int32(NEG_INF_KEY), -jnp.inf, thr_of(lo))

    idx_bits = (seq - 1).bit_length()

    def tie_search():
        need = kf - count(lambda x, s: x > kth)

        def tie_body(i, last):
            trial = last | lax.shift_left(jnp.int32(1), idx_bits - 1 - i)
            c = count(lambda x, s: (x == kth) & (s < trial))
            return jnp.where(c < need, trial, last)

        return lax.fori_loop(0, idx_bits, tie_body, jnp.zeros((1, Q_BLOCK), I32))

    def take_all_ties():
        return jnp.full((1, Q_BLOCK), (1 << idx_bits) - 1, I32)

    n_ge = jnp.where(c_lo < 0.0, jnp.inf, c_lo)
    last = lax.cond(jnp.max(n_ge) > kf, tie_search, take_all_ties)

    acc_ref[...] = jnp.zeros_like(acc_ref)

    def att_body(kb, carry):
        m_prev, l_prev = carry
        r0 = pl.multiple_of(kb * KEY_BLOCK, KEY_BLOCK)
        x = isc_ref[pl.ds(r0, KEY_BLOCK), :]
        s_idx = r0 + block_iota
        sel = (s_idx < limit) & ((x > kth) | ((x == kth) & (s_idx <= last)))
        sel_g = jnp.concatenate([jnp.where(sel, 1.0, 0.0)] * group, axis=1) > 0.0
        k_tile = kv_ref[pl.ds(r0, KEY_BLOCK), 0:KV_WIDTH]
        v_t = vt_ref[kb]
        groups = range(A_KV_HEADS)
        gs = [slice(g * gw, (g + 1) * gw) for g in groups]
        ds = [slice(g * A_HEAD_DIM, (g + 1) * A_HEAD_DIM) for g in groups]
        s = [_dot(k_tile[:, ds[g]], qat_ref[:, gs[g]]) for g in groups]
        s = [jnp.where(sel_g, s[g], NEG_BIG) for g in groups]
        m_new = [jnp.maximum(m_prev[:, gs[g]], jnp.max(s[g], axis=0, keepdims=True))
                 for g in groups]
        alpha = [jnp.exp(m_prev[:, gs[g]] - m_new[g]) for g in groups]
        p = [jnp.where(sel_g, jnp.exp(s[g] - m_new[g]), 0.0) for g in groups]
        pv = [_dot(v_t[ds[g], :], p[g].astype(BF16)) for g in groups]
        l_new = [alpha[g] * l_prev[:, gs[g]] + jnp.sum(p[g], axis=0, keepdims=True)
                 for g in groups]
        for g in groups:
            acc_ref[ds[g], :] = alpha[g] * acc_ref[ds[g], :] + pv[g]
        return jnp.concatenate(m_new, axis=1), jnp.concatenate(l_new, axis=1)

    m0 = jnp.full((1, A_HEADS * Q_BLOCK), NEG_BIG, F32)
    l0 = jnp.zeros((1, A_HEADS * Q_BLOCK), F32)
    _, l_fin = lax.fori_loop(0, nkb, att_body, (m0, l0))

    for p in range(n_heads_pair):
        g = (2 * p) // group
        ds = slice(g * A_HEAD_DIM, (g + 1) * A_HEAD_DIM)
        parts = []
        for h in (2 * p, 2 * p + 1):
            hl = h % group
            parts.append(acc_ref[ds, hl * Q_BLOCK:(hl + 1) * Q_BLOCK]
                         / l_fin[:, h * Q_BLOCK:(h + 1) * Q_BLOCK])
        o_ref[:, p * LANES:(p + 1) * LANES] = jnp.concatenate(parts, axis=0).T.astype(BF16)


def _mixer_a(qa, qi, kv, sm, bsz, seq):
    nq = seq // Q_BLOCK
    assert seq % (COUNT_GROUP * LANES) == 0, seq
    topk = min(TOPK_MAX, seq // 4)
    qrow = lambda b, j: (b * nq + j, 0)
    brow = lambda b, j: (b, 0)
    return pl.pallas_call(
        functools.partial(_mixer_a_body, seq=seq, topk=topk),
        grid=(bsz, nq),
        in_specs=[
            pl.BlockSpec((Q_BLOCK, A_WIDTH), qrow),
            pl.BlockSpec((Q_BLOCK, A_WIDTH), qrow),
            pl.BlockSpec((seq, 2 * KV_WIDTH), brow),
            pl.BlockSpec((seq, LANES), brow),
        ],
        out_specs=pl.BlockSpec((Q_BLOCK, A_WIDTH), qrow),
        out_shape=jax.ShapeDtypeStruct((bsz * seq, A_WIDTH), BF16),
        scratch_shapes=[
            pltpu.VMEM((seq, Q_BLOCK), F32),
            pltpu.VMEM((seq // KEY_BLOCK, KV_WIDTH, KEY_BLOCK), BF16),
            pltpu.VMEM((IDX_DIM, IDX_HEADS * Q_BLOCK), BF16),
            pltpu.VMEM((A_HEAD_DIM, A_HEADS * Q_BLOCK), BF16),
            pltpu.VMEM((KV_WIDTH, (A_HEADS // A_KV_HEADS) * Q_BLOCK), F32),
        ],
        compiler_params=pltpu.CompilerParams(
            dimension_semantics=("arbitrary", "arbitrary"), vmem_limit_bytes=VMEM_LIMIT),
        name="mixer_a",
    )(qa, qi, kv, sm)


def _gdn_body(gdn_ref, sm_ref, cw_ref, alog_ref, dtb_ref, ng_ref, o_ref,
              xpad_ref, q_ref, k_ref, v_ref, gate_ref, gct_ref, state_ref, oc_ref,
              u_ref, wq_ref, ak_ref, *, tc):
    tb = pl.program_id(1)
    n_chunks = tc // CHUNK
    prep_unroll = 4
    conv_cols = 3 * B_WIDTH
    pad = 8

    @pl.when(tb == 0)
    def _():
        xpad_ref[0:pad, :] = jnp.zeros((pad, conv_cols), F32)
        state_ref[...] = jnp.zeros_like(state_ref)

    @pl.when(tb > 0)
    def _():
        xpad_ref[0:pad, :] = xpad_ref[tc:tc + pad, :]

    xpad_ref[pad:pad + tc, :] = gdn_ref[:, 0:conv_cols]

    for ct in range(conv_cols // LANES):
        cs = slice(ct * LANES, (ct + 1) * LANES)
        y = jnp.zeros((tc, LANES), F32)
        for jj in range(CONV_WIDTH):
            r = pad - (CONV_WIDTH - 1) + jj
            y = y + cw_ref[jj:jj + 1, cs] * xpad_ref[r:r + tc, cs]
        y = y * jax.nn.sigmoid(y)
        seg, hh = divmod(ct, B_HEADS)
        hs = slice(hh * LANES, (hh + 1) * LANES)
        if seg < 2:
            y = y * lax.rsqrt(jnp.sum(y * y, axis=-1, keepdims=True) + EPS)
        if seg == 0:
            q_ref[:, hs] = y * (B_HEAD_DIM ** -0.5)
        elif seg == 1:
            k_ref[:, hs] = y
        else:
            v_ref[:, hs] = y

    sm = sm_ref[...]
    beta = jax.nn.sigmoid(sm)
    z = sm + dtb_ref[...]
    softplus = jnp.maximum(z, 0.0) + jnp.log(1.0 + jnp.exp(-jnp.abs(z)))
    g = -jnp.exp(alog_ref[...]) * softplus
    rin = lax.broadcasted_iota(I32, (tc, LANES), 0) & (CHUNK - 1)
    gc = g
    step = 1
    while step < CHUNK:
        gc = gc + jnp.where(rin >= step, pltpu.roll(gc, step, 0), 0.0)
        step *= 2
    gc3 = gc.reshape(n_chunks, CHUNK, LANES)
    g_last = jnp.broadcast_to(gc3[:, CHUNK - 1:CHUNK, :], gc3.shape).reshape(tc, LANES)
    gate_ref[0] = beta
    gate_ref[1] = gc
    gate_ref[2] = jnp.exp(gc)
    gate_ref[3] = jnp.exp(g_last - gc)
    gate_ref[4] = jnp.exp(g_last)
    for i in range(tc // LANES):
        t = gc[i * LANES:(i + 1) * LANES, :].T
        for half in range(LANES // CHUNK):
            gct_ref[i * (LANES // CHUNK) + half] = t[:, half * CHUNK:(half + 1) * CHUNK]

    ci = lax.broadcasted_iota(I32, (CHUNK, CHUNK), 0)
    si = lax.broadcasted_iota(I32, (CHUNK, CHUNK), 1)
    wl = lax.broadcasted_iota(I32, (CHUNK, 2 * CHUNK), 1)
    wr = lax.broadcasted_iota(I32, (CHUNK, 2 * CHUNK), 0)
    right = wl >= CHUNK
    eye_right = jnp.where(wl == wr + CHUNK, 1.0, 0.0)

    def prep_body(cg, carry):
        units = [(cg * prep_unroll + cc, h) for cc in range(prep_unroll) for h in range(B_HEADS)]
        rows = [pl.ds(pl.multiple_of(c * CHUNK, CHUNK), CHUNK) for c, _ in units]
        hsl = [slice(h * LANES, (h + 1) * LANES) for _, h in units]
        idx = range(len(units))

        def col(i, gate, off):
            h = units[i][1]
            return gate_ref[gate, rows[i], off + h:off + h + 1]

        q = [q_ref[rows[i], hsl[i]] for i in idx]
        k = [k_ref[rows[i], hsl[i]] for i in idx]
        kb = [k[i] * col(i, 0, SM_BETA) for i in idx]
        kq = [_dot_nt(jnp.concatenate([kb[i], q[i]], axis=0).astype(BF16), k[i].astype(BF16))
              for i in idx]
        decay = []
        for i, (c, h) in enumerate(units):
            d = col(i, 1, SM_DECAY) - gct_ref[c][SM_DECAY + h:SM_DECAY + h + 1, :]
            decay.append(jnp.where(ci >= si, jnp.exp(jnp.where(ci >= si, d, 0.0)), 0.0))
        wmat = []
        for i, (c, h) in enumerate(units):
            n_mat = jnp.where(ci > si, -(kq[i][0:CHUNK] * decay[i]), 0.0)
            wmat.append(jnp.concatenate([n_mat, jnp.zeros_like(n_mat)], axis=1) + eye_right)
            ak_ref[c, h, 0:CHUNK, :] = (kq[i][CHUNK:] * decay[i]).astype(BF16)
        pw = 1
        while pw < CHUNK:
            wb = [wmat[i].astype(BF16) for i in idx]
            wmat = [_dot(wb[i][:, 0:CHUNK], wb[i]) + jnp.where(right, wmat[i], 0.0) for i in idx]
            pw *= 2
        eg = [col(i, 2, SM_DECAY) for i in idx]
        rhs = [jnp.concatenate([v_ref[rows[i], hsl[i]] * col(i, 0, SM_BETA), kb[i] * eg[i]],
                               axis=1).astype(BF16) for i in idx]
        sol = [_dot(wmat[i][:, CHUNK:].astype(BF16), rhs[i]) for i in idx]
        for i, (c, h) in enumerate(units):
            u_ref[rows[i], hsl[i]] = sol[i][:, 0:B_HEAD_DIM]
            wq_ref[c, h] = jnp.concatenate([sol[i][:, B_HEAD_DIM:], q[i] * eg[i]],
                                           axis=0).astype(BF16)
            ak_ref[c, h, CHUNK:, :] = (k[i] * col(i, 3, SM_DECAY)).T.astype(BF16)
        return carry

    lax.fori_loop(0, n_chunks // prep_unroll, prep_body, 0)

    def scan_body(c, carry):
        r0 = pl.multiple_of(c * CHUNK, CHUNK)
        rows = pl.ds(r0, CHUNK)
        heads = range(B_HEADS)
        hsl = [slice(h * LANES, (h + 1) * LANES) for h in heads]
        s_prev = [state_ref[h] for h in heads]
        ws = [_dot(wq_ref[c, h], s_prev[h].astype(BF16)) for h in heads]
        v_new = [(u_ref[rows, hsl[h]] - ws[h][0:CHUNK]).astype(BF16) for h in heads]
        r = [_dot(ak_ref[c, h], v_new[h]) for h in heads]
        for h in heads:
            gl = gate_ref[4, rows, SM_DECAY + h:SM_DECAY + h + 1][0:1, :]
            oc_ref[rows, hsl[h]] = ws[h][CHUNK:] + r[h][0:CHUNK]
            state_ref[h] = s_prev[h] * gl + r[h][CHUNK:]
        return carry

    lax.fori_loop(0, n_chunks, scan_body, 0)

    for h in range(B_HEADS):
        hs = slice(h * LANES, (h + 1) * LANES)
        zf = gdn_ref[:, conv_cols + h * LANES:conv_cols + (h + 1) * LANES]
        o_ref[:, hs] = (_rms(oc_ref[:, hs], ng_ref[...]) * (zf * jax.nn.sigmoid(zf))).astype(BF16)


def _gdn(gdn_in, sm, conv_w, alog_row, dtb_row, norm_g, bsz, seq, tc):
    nt = seq // tc
    trow = lambda b, t: (b * nt + t, 0)
    const = lambda b, t: (0, 0)
    n_chunks = tc // CHUNK
    return pl.pallas_call(
        functools.partial(_gdn_body, tc=tc),
        grid=(bsz, nt),
        in_specs=[
            pl.BlockSpec((tc, GDN_WIDTH), trow),
            pl.BlockSpec((tc, LANES), trow),
            pl.BlockSpec((CONV_WIDTH, 3 * B_WIDTH), const),
            pl.BlockSpec((1, LANES), const),
            pl.BlockSpec((1, LANES), const),
            pl.BlockSpec((1, B_HEAD_DIM), const),
        ],
        out_specs=pl.BlockSpec((tc, B_WIDTH), trow),
        out_shape=jax.ShapeDtypeStruct((bsz * seq, B_WIDTH), BF16),
        scratch_shapes=[
            pltpu.VMEM((tc + 8, 3 * B_WIDTH), F32),
            pltpu.VMEM((tc, B_WIDTH), F32),
            pltpu.VMEM((tc, B_WIDTH), F32),
            pltpu.VMEM((tc, B_WIDTH), F32),
            pltpu.VMEM((5, tc, LANES), F32),
            pltpu.VMEM((n_chunks, LANES, CHUNK), F32),
            pltpu.VMEM((B_HEADS, B_HEAD_DIM, B_HEAD_DIM), F32),
            pltpu.VMEM((tc, B_WIDTH), F32),
            pltpu.VMEM((tc, B_WIDTH), F32),
            pltpu.VMEM((n_chunks, B_HEADS, 2 * CHUNK, B_HEAD_DIM), BF16),
            pltpu.VMEM((n_chunks, B_HEADS, CHUNK + B_HEAD_DIM, CHUNK), BF16),
        ],
        compiler_params=pltpu.CompilerParams(
            dimension_semantics=("arbitrary", "arbitrary"), vmem_limit_bytes=VMEM_LIMIT),
        name="gdn",
    )(gdn_in, sm, conv_w, alog_row, dtb_row, norm_g)


def _ffn_body(x_ref, oa_ref, ob_ref, wo_ref, g2_ref, w1_ref, w2_ref, g3_ref, out_ref, *,
              final_norm):
    y = (x_ref[...] + _dot(oa_ref[...], wo_ref[0:A_WIDTH, :])
         + _dot(ob_ref[...], wo_ref[A_WIDTH:, :]))
    h = _rms(y, g2_ref[...]).astype(BF16)
    a = jnp.square(jnp.maximum(_dot(h, w1_ref[...]), 0.0)).astype(BF16)
    acc = y + _dot(a, w2_ref[...])
    out_ref[...] = _rms(acc, g3_ref[...]) if final_norm else acc


def _ffn(x2, oa, ob, wo, g2, w1, w2, g3, tm, final_norm):
    m = x2.shape[0]
    row = lambda i: (i, 0)
    const = lambda i: (0, 0)
    resident = functools.partial(pl.BlockSpec, index_map=const, pipeline_mode=pl.Buffered(1))
    return pl.pallas_call(
        functools.partial(_ffn_body, final_norm=final_norm),
        grid=(m // tm,),
        in_specs=[
            pl.BlockSpec((tm, D_MODEL), row),
            pl.BlockSpec((tm, A_WIDTH), row),
            pl.BlockSpec((tm, B_WIDTH), row),
            resident((D_MODEL, D_MODEL)),
            pl.BlockSpec((1, D_MODEL), const),
            resident((D_MODEL, D_FF)),
            resident((D_FF, D_MODEL)),
            pl.BlockSpec((1, D_MODEL), const),
        ],
        out_specs=pl.BlockSpec((tm, D_MODEL), row),
        out_shape=jax.ShapeDtypeStruct((m, D_MODEL), F32),
        compiler_params=pltpu.CompilerParams(
            dimension_semantics=("arbitrary",), vmem_limit_bytes=VMEM_LIMIT),
        name="ffn",
    )(x2, oa, ob, wo, g2, w1, w2, g3)


def _rope_tables(seq):
    half = A_HEAD_DIM // 2
    inv_freq = 1.0 / (ROPE_THETA ** (jnp.arange(half, dtype=F32) / half))
    ang = jnp.arange(seq).astype(F32)[:, None] * inv_freq[None, :]
    cos = jnp.cos(ang)
    sin = jnp.sin(ang)
    reps = LANES // A_HEAD_DIM
    return (jnp.tile(cos, (1, 2 * reps)),
            jnp.tile(jnp.concatenate([-sin, sin], axis=1), (1, reps)))


def _permute_in_weight(w):
    sizes = (A_WIDTH, KV_WIDTH, KV_WIDTH, IDX_HEADS * IDX_DIM, IDX_DIM, IDX_HEADS,
             B_WIDTH, B_WIDTH, B_WIDTH, B_WIDTH, B_HEADS, B_HEADS)
    parts, off = [], 0
    for s in sizes:
        parts.append(w[:, off:off + s])
        off += s
    qa, ka, va, qi, ki, wi, qb, kb, vb, zb, bb, ab = parts
    pad = jnp.zeros((w.shape[0], LANES - (IDX_DIM + IDX_HEADS + 2 * B_HEADS)), w.dtype)
    return jnp.concatenate([qa, qi, ka, va, qb, kb, vb, zb, ki, wi, bb, ab, pad],
                           axis=1).astype(BF16)


def _lane_row(vals, offset):
    return jnp.zeros((1, LANES), F32).at[0, offset:offset + vals.shape[0]].set(vals.astype(F32))


def kernel(x, norm_mix_g, w_in, conv_w, a_log, dt_bias, gdn_norm_g, w_out,
           norm_ffn_g, w_ff1, w_ff2, norm_final_g):
    bsz, seq, d = x.shape
    depth = w_in.shape[0]
    m = bsz * seq
    cos_t, sin_t = _rope_tables(seq)
    x2 = x.reshape(m, d)
    for l in range(depth):
        qa, qi, kv, gdn_in, sm = _inproj(
            x2, norm_mix_g[l][None, :], _permute_in_weight(w_in[l]), cos_t, sin_t, seq, tm=512)
        o_a = _mixer_a(qa, qi, kv, sm, bsz, seq)
        o_b = _gdn(gdn_in, sm, conv_w[l], _lane_row(a_log[l], SM_DECAY),
                   _lane_row(dt_bias[l], SM_DECAY), gdn_norm_g[l][None, :], bsz, seq, tc=512)
        x2 = _ffn(x2, o_a, o_b, w_out[l].astype(BF16), norm_ffn_g[l][None, :],
                  w_ff1[l].astype(BF16), w_ff2[l].astype(BF16), norm_final_g[None, :],
                  tm=512, final_norm=(l == depth - 1))
    return x2.reshape(bsz, seq, d)
```

```python
import functools

import jax
import jax.numpy as jnp
from jax import lax
from jax.experimental import pallas as pl
from jax.experimental.pallas import tpu as pltpu

F32 = jnp.float32
BF16 = jnp.bfloat16
I32 = jnp.int32

D_MODEL = 1024
CHUNK = 64
A_QUERIES = 256
ROPE_THETA = 10000.0
EPS = 1e-6
A_HEADS = 8
A_KV_HEADS = 2
A_HEAD_DIM = 64
IDX_HEADS = 8
IDX_DIM = 64
TOPK_MAX = 256
B_HEADS = 4
B_HEAD_DIM = 128
CONV_WIDTH = 4
D_FF = 4 * D_MODEL

LANES = 128
A_WIDTH = A_HEADS * A_HEAD_DIM
KV_WIDTH = A_KV_HEADS * A_HEAD_DIM
B_WIDTH = B_HEADS * B_HEAD_DIM
GDN_WIDTH = 4 * B_WIDTH
SM_KI = 0
SM_WI = IDX_DIM
SM_BETA = SM_WI + IDX_HEADS
SM_DECAY = SM_BETA + B_HEADS
C_QA = 0
C_QI = C_QA + A_WIDTH
C_KV = C_QI + A_WIDTH
C_GDN = C_KV + 2 * KV_WIDTH
C_SM = C_GDN + GDN_WIDTH
IN_COLS = C_SM + LANES

VMEM_LIMIT = 56 * 1024 * 1024
COUNT_GROUP = 4
COUNT_ROWS = 32
KEY_BLOCK = 2 * LANES
NEG_BIG = -1e30
NEG_INF_KEY = -(2 ** 31) + 0x7FFFFF


def _rms(x, g):
    return x * lax.rsqrt(jnp.mean(x * x, axis=-1, keepdims=True) + EPS) * g


def _dot(a, b):
    return jnp.dot(a, b, preferred_element_type=F32)


def _dot_nt(a, b):
    return lax.dot_general(a, b, (((1,), (1,)), ((), ())), preferred_element_type=F32)


def _inproj_body(x_ref, g_ref, w_ref, cos_ref, sin_ref,
                 qa_ref, qi_ref, kv_ref, gdn_ref, sm_ref):
    h = _rms(x_ref[...], g_ref[...]).astype(BF16)
    cos = cos_ref[...]
    sin = sin_ref[...]
    lane = lax.broadcasted_iota(I32, cos.shape, 1)
    first_half = (lane & (A_HEAD_DIM - 1)) < A_HEAD_DIM // 2

    def rope(t):
        swapped = jnp.where(first_half, pltpu.roll(t, LANES - A_HEAD_DIM // 2, 1),
                            pltpu.roll(t, A_HEAD_DIM // 2, 1))
        return t * cos + swapped * sin

    def proj(c0, width):
        return _dot(h, w_ref[:, c0:c0 + width])

    acc = proj(C_QA, A_WIDTH)
    for j in range(A_WIDTH // LANES):
        sl = slice(j * LANES, (j + 1) * LANES)
        qa_ref[:, sl] = (rope(acc[:, sl]) * (A_HEAD_DIM ** -0.5)).astype(BF16)
    acc = proj(C_QI, A_WIDTH)
    for j in range(A_WIDTH // LANES):
        sl = slice(j * LANES, (j + 1) * LANES)
        qi_ref[:, sl] = rope(acc[:, sl]).astype(BF16)
    acc = proj(C_KV, 2 * KV_WIDTH)
    kv_ref[:, 0:KV_WIDTH] = rope(acc[:, 0:KV_WIDTH]).astype(BF16)
    kv_ref[:, KV_WIDTH:] = acc[:, KV_WIDTH:].astype(BF16)
    for j in range(GDN_WIDTH // B_WIDTH):
        gdn_ref[:, j * B_WIDTH:(j + 1) * B_WIDTH] = proj(C_GDN + j * B_WIDTH, B_WIDTH)
    acc = proj(C_SM, LANES)
    sm_ref[...] = jnp.where(lane < IDX_DIM, rope(acc), acc)


def _inproj(x2, g, w, cos_t, sin_t, seq, tm):
    m = x2.shape[0]
    nt = seq // tm
    row = lambda i: (i, 0)
    const = lambda i: (0, 0)
    return pl.pallas_call(
        _inproj_body,
        grid=(m // tm,),
        in_specs=[
            pl.BlockSpec((tm, D_MODEL), row),
            pl.BlockSpec((1, D_MODEL), const),
            pl.BlockSpec((D_MODEL, IN_COLS), const, pipeline_mode=pl.Buffered(1)),
            pl.BlockSpec((tm, LANES), lambda i: (i % nt, 0)),
            pl.BlockSpec((tm, LANES), lambda i: (i % nt, 0)),
        ],
        out_specs=[
            pl.BlockSpec((tm, A_WIDTH), row),
            pl.BlockSpec((tm, A_WIDTH), row),
            pl.BlockSpec((tm, 2 * KV_WIDTH), row),
            pl.BlockSpec((tm, GDN_WIDTH), row),
            pl.BlockSpec((tm, LANES), row),
        ],
        out_shape=[
            jax.ShapeDtypeStruct((m, A_WIDTH), BF16),
            jax.ShapeDtypeStruct((m, A_WIDTH), BF16),
            jax.ShapeDtypeStruct((m, 2 * KV_WIDTH), BF16),
            jax.ShapeDtypeStruct((m, GDN_WIDTH), F32),
            jax.ShapeDtypeStruct((m, LANES), F32),
        ],
        compiler_params=pltpu.CompilerParams(
            dimension_semantics=("arbitrary",), vmem_limit_bytes=VMEM_LIMIT),
        name="inproj",
    )(x2, g, w, cos_t, sin_t)


def _mixer_a_body(qa_ref, qi_ref, kv_ref, sm_ref, o_ref,
                  isc_ref, vt_ref, qit_ref, qat_ref, acc_ref, *, seq, topk):
    j = pl.program_id(1)
    nkt = (j + 1) * (A_QUERIES // LANES)
    n_heads_pair = A_WIDTH // LANES
    group = A_HEADS // A_KV_HEADS
    gw = group * A_QUERIES

    @pl.when(j == 0)
    def _():
        per_block = KEY_BLOCK // LANES
        for t in range(seq // LANES):
            vt = kv_ref[t * LANES:(t + 1) * LANES, KV_WIDTH:].astype(F32)
            c0 = (t % per_block) * LANES
            vt_ref[t // per_block, :, c0:c0 + LANES] = vt.T.astype(BF16)

    for p in range(n_heads_pair):
        sl = slice(p * LANES, (p + 1) * LANES)
        t = qi_ref[:, sl].astype(F32).T
        qit_ref[:, (2 * p) * A_QUERIES:(2 * p + 1) * A_QUERIES] = t[0:IDX_DIM].astype(BF16)
        qit_ref[:, (2 * p + 1) * A_QUERIES:(2 * p + 2) * A_QUERIES] = t[IDX_DIM:].astype(BF16)
        t = qa_ref[:, sl].astype(F32).T
        qat_ref[:, (2 * p) * A_QUERIES:(2 * p + 1) * A_QUERIES] = t[0:A_HEAD_DIM].astype(BF16)
        qat_ref[:, (2 * p + 1) * A_QUERIES:(2 * p + 2) * A_QUERIES] = t[A_HEAD_DIM:].astype(BF16)

    q0 = pl.multiple_of(j * A_QUERIES, A_QUERIES)
    w_t = sm_ref[pl.ds(q0, A_QUERIES), :].T[SM_WI:SM_WI + IDX_HEADS, :]
    w_t = w_t * ((IDX_HEADS ** -0.5) * (IDX_DIM ** -0.5))

    qlane = lax.broadcasted_iota(I32, (1, A_QUERIES), 1)
    limit = q0 + (lax.shift_right_logical(qlane, CHUNK.bit_length() - 1) + 1) * CHUNK
    tile_iota = lax.broadcasted_iota(I32, (LANES, A_QUERIES), 0)
    block_iota = lax.broadcasted_iota(I32, (KEY_BLOCK, A_QUERIES), 0)
    nkb = lax.shift_right_logical(nkt + (KEY_BLOCK // LANES - 1), (KEY_BLOCK // LANES).bit_length() - 1)

    def isc_body(kb, carry):
        r0 = pl.multiple_of(kb * KEY_BLOCK, KEY_BLOCK)
        kid = sm_ref[pl.ds(r0, KEY_BLOCK), SM_KI:SM_KI + IDX_DIM].astype(BF16)
        rel = _dot(kid, qit_ref[...])
        acc = jnp.zeros((KEY_BLOCK, A_QUERIES), F32)
        for h in range(IDX_HEADS):
            acc = acc + w_t[h:h + 1, :] * jnp.maximum(rel[:, h * A_QUERIES:(h + 1) * A_QUERIES], 0.0)
        isc_ref[pl.ds(r0, KEY_BLOCK), :] = jnp.where(r0 + block_iota < limit, acc, -jnp.inf)
        return carry

    lax.fori_loop(0, nkb, isc_body, 0)

    ngrp = lax.shift_right_logical(nkt + (COUNT_GROUP - 1), COUNT_GROUP.bit_length() - 1)

    def fill_body(kt, carry):
        r0 = pl.multiple_of(kt * LANES, LANES)
        isc_ref[pl.ds(r0, LANES), :] = jnp.full((LANES, A_QUERIES), -jnp.inf, F32)
        return carry

    lax.fori_loop(nkb * (KEY_BLOCK // LANES), ngrp * COUNT_GROUP, fill_body, 0)

    def count(pred):
        def body(g, acc):
            for t in range(COUNT_GROUP):
                r0 = pl.multiple_of((g * COUNT_GROUP + t) * LANES, LANES)
                hit = jnp.where(pred(isc_ref[pl.ds(r0, LANES), :], r0 + tile_iota), 1.0, 0.0)
                acc = acc + hit.reshape(LANES // COUNT_ROWS, COUNT_ROWS, A_QUERIES).sum(axis=0)
            return acc
        acc = lax.fori_loop(0, ngrp, body, jnp.zeros((COUNT_ROWS, A_QUERIES), F32))
        return acc.sum(axis=0, keepdims=True)

    kf = float(topk)

    def thr_of(key):
        bits = jnp.where(key >= 0, key, key ^ jnp.int32(0x7FFFFFFF))
        return lax.bitcast_convert_type(bits, F32)

    c0 = count(lambda x, s: x >= 0.0)
    lo0 = jnp.where(c0 >= kf, jnp.int32(0), jnp.int32(-2 ** 31))
    c_lo0 = jnp.where(c0 >= kf, c0, -1.0)

    def bis_body(i, st):
        lo, c_lo = st
        trial = lo | lax.shift_left(jnp.int32(1), 30 - i)
        thr = thr_of(trial)
        c = count(lambda x, s: x >= thr)
        ok = c >= kf
        return jnp.where(ok, trial, lo), jnp.where(ok, c, c_lo)

    lo, c_lo = lax.fori_loop(0, 31, bis_body, (lo0, c_lo0))
    kth = jnp.where(lo < jnp.int32(NEG_INF_KEY), -jnp.inf, thr_of(lo))

    idx_bits = (seq - 1).bit_length()

    def tie_search():
        need = kf - count(lambda x, s: x > kth)

        def tie_body(i, last):
            trial = last | lax.shift_left(jnp.int32(1), idx_bits - 1 - i)
            c = count(lambda x, s: (x == kth) & (s < trial))
            return jnp.where(c < need, trial, last)

        return lax.fori_loop(0, idx_bits, tie_body, jnp.zeros((1, A_QUERIES), I32))

    def take_all_ties():
        return jnp.full((1, A_QUERIES), (1 << idx_bits) - 1, I32)

    n_ge = jnp.where(c_lo < 0.0, jnp.inf, c_lo)
    last = lax.cond(jnp.max(n_ge) > kf, tie_search, take_all_ties)

    acc_ref[...] = jnp.zeros_like(acc_ref)

    def att_body(kb, carry):
        m_prev, l_prev = carry
        r0 = pl.multiple_of(kb * KEY_BLOCK, KEY_BLOCK)
        x = isc_ref[pl.ds(r0, KEY_BLOCK), :]
        s_idx = r0 + block_iota
        sel = (s_idx < limit) & ((x > kth) | ((x == kth) & (s_idx <= last)))
        sel_g = jnp.concatenate([jnp.where(sel, 1.0, 0.0)] * group, axis=1) > 0.0
        k_tile = kv_ref[pl.ds(r0, KEY_BLOCK), 0:KV_WIDTH]
        v_t = vt_ref[kb]
        groups = range(A_KV_HEADS)
        gs = [slice(g * gw, (g + 1) * gw) for g in groups]
        ds = [slice(g * A_HEAD_DIM, (g + 1) * A_HEAD_DIM) for g in groups]
        s = [_dot(k_tile[:, ds[g]], qat_ref[:, gs[g]]) for g in groups]
        s = [jnp.where(sel_g, s[g], NEG_BIG) for g in groups]
        m_new = [jnp.maximum(m_prev[:, gs[g]], jnp.max(s[g], axis=0, keepdims=True))
                 for g in groups]
        alpha = [jnp.exp(m_prev[:, gs[g]] - m_new[g]) for g in groups]
        p = [jnp.where(sel_g, jnp.exp(s[g] - m_new[g]), 0.0) for g in groups]
        pv = [_dot(v_t[ds[g], :], p[g].astype(BF16)) for g in groups]
        l_new = [alpha[g] * l_prev[:, gs[g]] + jnp.sum(p[g], axis=0, keepdims=True)
                 for g in groups]
        for g in groups:
            acc_ref[ds[g], :] = alpha[g] * acc_ref[ds[g], :] + pv[g]
        return jnp.concatenate(m_new, axis=1), jnp.concatenate(l_new, axis=1)

    m0 = jnp.full((1, A_HEADS * A_QUERIES), NEG_BIG, F32)
    l0 = jnp.zeros((1, A_HEADS * A_QUERIES), F32)
    _, l_fin = lax.fori_loop(0, nkb, att_body, (m0, l0))

    for p in range(n_heads_pair):
        g = (2 * p) // group
        ds = slice(g * A_HEAD_DIM, (g + 1) * A_HEAD_DIM)
        parts = []
        for h in (2 * p, 2 * p + 1):
            hl = h % group
            parts.append(acc_ref[ds, hl * A_QUERIES:(hl + 1) * A_QUERIES]
                         / l_fin[:, h * A_QUERIES:(h + 1) * A_QUERIES])
        o_ref[:, p * LANES:(p + 1) * LANES] = jnp.concatenate(parts, axis=0).T.astype(BF16)


def _mixer_a(qa, qi, kv, sm, bsz, seq):
    nq = seq // A_QUERIES
    assert seq % (COUNT_GROUP * LANES) == 0, seq
    topk = min(TOPK_MAX, seq // 4)
    qrow = lambda b, j: (b * nq + j, 0)
    brow = lambda b, j: (b, 0)
    return pl.pallas_call(
        functools.partial(_mixer_a_body, seq=seq, topk=topk),
        grid=(bsz, nq),
        in_specs=[
            pl.BlockSpec((A_QUERIES, A_WIDTH), qrow),
            pl.BlockSpec((A_QUERIES, A_WIDTH), qrow),
            pl.BlockSpec((seq, 2 * KV_WIDTH), brow),
            pl.BlockSpec((seq, LANES), brow),
        ],
        out_specs=pl.BlockSpec((A_QUERIES, A_WIDTH), qrow),
        out_shape=jax.ShapeDtypeStruct((bsz * seq, A_WIDTH), BF16),
        scratch_shapes=[
            pltpu.VMEM((seq, A_QUERIES), F32),
            pltpu.VMEM((seq // KEY_BLOCK, KV_WIDTH, KEY_BLOCK), BF16),
            pltpu.VMEM((IDX_DIM, IDX_HEADS * A_QUERIES), BF16),
            pltpu.VMEM((A_HEAD_DIM, A_HEADS * A_QUERIES), BF16),
            pltpu.VMEM((KV_WIDTH, (A_HEADS // A_KV_HEADS) * A_QUERIES), F32),
        ],
        compiler_params=pltpu.CompilerParams(
            dimension_semantics=("arbitrary", "arbitrary"), vmem_limit_bytes=VMEM_LIMIT),
        name="mixer_a",
    )(qa, qi, kv, sm)


def _gdn_body(gdn_ref, sm_ref, cw_ref, alog_ref, dtb_ref, ng_ref, o_ref,
              xpad_ref, q_ref, k_ref, v_ref, gate_ref, gct_ref, state_ref, oc_ref,
              u_ref, wq_ref, ak_ref, *, tc):
    tb = pl.program_id(1)
    n_chunks = tc // CHUNK
    prep_unroll = 4
    conv_cols = 3 * B_WIDTH
    pad = 8

    @pl.when(tb == 0)
    def _():
        xpad_ref[0:pad, :] = jnp.zeros((pad, conv_cols), F32)
        state_ref[...] = jnp.zeros_like(state_ref)

    @pl.when(tb > 0)
    def _():
        xpad_ref[0:pad, :] = xpad_ref[tc:tc + pad, :]

    xpad_ref[pad:pad + tc, :] = gdn_ref[:, 0:conv_cols]

    for ct in range(conv_cols // LANES):
        cs = slice(ct * LANES, (ct + 1) * LANES)
        y = jnp.zeros((tc, LANES), F32)
        for jj in range(CONV_WIDTH):
            r = pad - (CONV_WIDTH - 1) + jj
            y = y + cw_ref[jj:jj + 1, cs] * xpad_ref[r:r + tc, cs]
        y = y * jax.nn.sigmoid(y)
        seg, hh = divmod(ct, B_HEADS)
        hs = slice(hh * LANES, (hh + 1) * LANES)
        if seg < 2:
            y = y * lax.rsqrt(jnp.sum(y * y, axis=-1, keepdims=True) + EPS)
        if seg == 0:
            q_ref[:, hs] = y * (B_HEAD_DIM ** -0.5)
        elif seg == 1:
            k_ref[:, hs] = y
        else:
            v_ref[:, hs] = y

    sm = sm_ref[...]
    beta = jax.nn.sigmoid(sm)
    z = sm + dtb_ref[...]
    softplus = jnp.maximum(z, 0.0) + jnp.log(1.0 + jnp.exp(-jnp.abs(z)))
    g = -jnp.exp(alog_ref[...]) * softplus
    rin = lax.broadcasted_iota(I32, (tc, LANES), 0) & (CHUNK - 1)
    gc = g
    step = 1
    while step < CHUNK:
        gc = gc + jnp.where(rin >= step, pltpu.roll(gc, step, 0), 0.0)
        step *= 2
    gc3 = gc.reshape(n_chunks, CHUNK, LANES)
    g_last = jnp.broadcast_to(gc3[:, CHUNK - 1:CHUNK, :], gc3.shape).reshape(tc, LANES)
    gate_ref[0] = beta
    gate_ref[1] = gc
    gate_ref[2] = jnp.exp(gc)
    gate_ref[3] = jnp.exp(g_last - gc)
    gate_ref[4] = jnp.exp(g_last)
    for i in range(tc // LANES):
        t = gc[i * LANES:(i + 1) * LANES, :].T
        for half in range(LANES // CHUNK):
            gct_ref[i * (LANES // CHUNK) + half] = t[:, half * CHUNK:(half + 1) * CHUNK]

    ci = lax.broadcasted_iota(I32, (CHUNK, CHUNK), 0)
    si = lax.broadcasted_iota(I32, (CHUNK, CHUNK), 1)
    wl = lax.broadcasted_iota(I32, (CHUNK, 2 * CHUNK), 1)
    wr = lax.broadcasted_iota(I32, (CHUNK, 2 * CHUNK), 0)
    right = wl >= CHUNK
    eye_right = jnp.where(wl == wr + CHUNK, 1.0, 0.0)

    def prep_body(cg, carry):
        units = [(cg * prep_unroll + cc, h) for cc in range(prep_unroll) for h in range(B_HEADS)]
        rows = [pl.ds(pl.multiple_of(c * CHUNK, CHUNK), CHUNK) for c, _ in units]
        hsl = [slice(h * LANES, (h + 1) * LANES) for _, h in units]
        idx = range(len(units))

        def col(i, gate, off):
            h = units[i][1]
            return gate_ref[gate, rows[i], off + h:off + h + 1]

        q = [q_ref[rows[i], hsl[i]] for i in idx]
        k = [k_ref[rows[i], hsl[i]] for i in idx]
        kb = [k[i] * col(i, 0, SM_BETA) for i in idx]
        kq = [_dot_nt(jnp.concatenate([kb[i], q[i]], axis=0).astype(BF16), k[i].astype(BF16))
              for i in idx]
        decay = []
        for i, (c, h) in enumerate(units):
            d = col(i, 1, SM_DECAY) - gct_ref[c][SM_DECAY + h:SM_DECAY + h + 1, :]
            decay.append(jnp.where(ci >= si, jnp.exp(jnp.where(ci >= si, d, 0.0)), 0.0))
        wmat = []
        for i, (c, h) in enumerate(units):
            n_mat = jnp.where(ci > si, -(kq[i][0:CHUNK] * decay[i]), 0.0)
            wmat.append(jnp.concatenate([n_mat, jnp.zeros_like(n_mat)], axis=1) + eye_right)
            ak_ref[c, h, 0:CHUNK, :] = (kq[i][CHUNK:] * decay[i]).astype(BF16)
        pw = 1
        while pw < CHUNK:
            wb = [wmat[i].astype(BF16) for i in idx]
            wmat = [_dot(wb[i][:, 0:CHUNK], wb[i]) + jnp.where(right, wmat[i], 0.0) for i in idx]
            pw *= 2
        eg = [col(i, 2, SM_DECAY) for i in idx]
        rhs = [jnp.concatenate([v_ref[rows[i], hsl[i]] * col(i, 0, SM_BETA), kb[i] * eg[i]],
                               axis=1).astype(BF16) for i in idx]
        sol = [_dot(wmat[i][:, CHUNK:].astype(BF16), rhs[i]) for i in idx]
        for i, (c, h) in enumerate(units):
            u_ref[rows[i], hsl[i]] = sol[i][:, 0:B_HEAD_DIM]
            wq_ref[c, h] = jnp.concatenate([sol[i][:, B_HEAD_DIM:], q[i] * eg[i]],
                                           axis=0).astype(BF16)
            ak_ref[c, h, CHUNK:, :] = (k[i] * col(i, 3, SM_DECAY)).T.astype(BF16)
        return carry

    lax.fori_loop(0, n_chunks // prep_unroll, prep_body, 0)

    def scan_body(c, carry):
        r0 = pl.multiple_of(c * CHUNK, CHUNK)
        rows = pl.ds(r0, CHUNK)
        heads = range(B_HEADS)
        hsl = [slice(h * LANES, (h + 1) * LANES) for h in heads]
        s_prev = [state_ref[h] for h in heads]
        ws = [_dot(wq_ref[c, h], s_prev[h].astype(BF16)) for h in heads]
        v_new = [(u_ref[rows, hsl[h]] - ws[h][0:CHUNK]).astype(BF16) for h in heads]
        r = [_dot(ak_ref[c, h], v_new[h]) for h in heads]
        for h in heads:
            gl = gate_ref[4, rows, SM_DECAY + h:SM_DECAY + h + 1][0:1, :]
            oc_ref[rows, hsl[h]] = ws[h][CHUNK:] + r[h][0:CHUNK]
            state_ref[h] = s_prev[h] * gl + r[h][CHUNK:]
        return carry

    lax.fori_loop(0, n_chunks, scan_body, 0)

    for h in range(B_HEADS):
        hs = slice(h * LANES, (h + 1) * LANES)
        zf = gdn_ref[:, conv_cols + h * LANES:conv_cols + (h + 1) * LANES]
        o_ref[:, hs] = (_rms(oc_ref[:, hs], ng_ref[...]) * (zf * jax.nn.sigmoid(zf))).astype(BF16)


def _gdn(gdn_in, sm, conv_w, alog_row, dtb_row, norm_g, bsz, seq, tc):
    nt = seq // tc
    trow = lambda b, t: (b * nt + t, 0)
    const = lambda b, t: (0, 0)
    n_chunks = tc // CHUNK
    return pl.pallas_call(
        functools.partial(_gdn_body, tc=tc),
        grid=(bsz, nt),
        in_specs=[
            pl.BlockSpec((tc, GDN_WIDTH), trow),
            pl.BlockSpec((tc, LANES), trow),
            pl.BlockSpec((CONV_WIDTH, 3 * B_WIDTH), const),
            pl.BlockSpec((1, LANES), const),
            pl.BlockSpec((1, LANES), const),
            pl.BlockSpec((1, B_HEAD_DIM), const),
        ],
        out_specs=pl.BlockSpec((tc, B_WIDTH), trow),
        out_shape=jax.ShapeDtypeStruct((bsz * seq, B_WIDTH), BF16),
        scratch_shapes=[
            pltpu.VMEM((tc + 8, 3 * B_WIDTH), F32),
            pltpu.VMEM((tc, B_WIDTH), F32),
            pltpu.VMEM((tc, B_WIDTH), F32),
            pltpu.VMEM((tc, B_WIDTH), F32),
            pltpu.VMEM((5, tc, LANES), F32),
            pltpu.VMEM((n_chunks, LANES, CHUNK), F32),
            pltpu.VMEM((B_HEADS, B_HEAD_DIM, B_HEAD_DIM), F32),
            pltpu.VMEM((tc, B_WIDTH), F32),
            pltpu.VMEM((tc, B_WIDTH), F32),
            pltpu.VMEM((n_chunks, B_HEADS, 2 * CHUNK, B_HEAD_DIM), BF16),
            pltpu.VMEM((n_chunks, B_HEADS, CHUNK + B_HEAD_DIM, CHUNK), BF16),
        ],
        compiler_params=pltpu.CompilerParams(
            dimension_semantics=("arbitrary", "arbitrary"), vmem_limit_bytes=VMEM_LIMIT),
        name="gdn",
    )(gdn_in, sm, conv_w, alog_row, dtb_row, norm_g)


def _ffn_body(x_ref, oa_ref, ob_ref, wo_ref, g2_ref, w1_ref, w2_ref, g3_ref, out_ref, *,
              final_norm):
    y = (x_ref[...] + _dot(oa_ref[...], wo_ref[0:A_WIDTH, :])
         + _dot(ob_ref[...], wo_ref[A_WIDTH:, :]))
    h = _rms(y, g2_ref[...]).astype(BF16)
    a = jnp.square(jnp.maximum(_dot(h, w1_ref[...]), 0.0)).astype(BF16)
    acc = y + _dot(a, w2_ref[...])
    out_ref[...] = _rms(acc, g3_ref[...]) if final_norm else acc


def _ffn(x2, oa, ob, wo, g2, w1, w2, g3, tm, final_norm):
    m = x2.shape[0]
    row = lambda i: (i, 0)
    const = lambda i: (0, 0)
    resident = functools.partial(pl.BlockSpec, index_map=const, pipeline_mode=pl.Buffered(1))
    return pl.pallas_call(
        functools.partial(_ffn_body, final_norm=final_norm),
        grid=(m // tm,),
        in_specs=[
            pl.BlockSpec((tm, D_MODEL), row),
            pl.BlockSpec((tm, A_WIDTH), row),
            pl.BlockSpec((tm, B_WIDTH), row),
            resident((D_MODEL, D_MODEL)),
            pl.BlockSpec((1, D_MODEL), const),
            resident((D_MODEL, D_FF)),
            resident((D_FF, D_MODEL)),
            pl.BlockSpec((1, D_MODEL), const),
        ],
        out_specs=pl.BlockSpec((tm, D_MODEL), row),
        out_shape=jax.ShapeDtypeStruct((m, D_MODEL), F32),
        compiler_params=pltpu.CompilerParams(
            dimension_semantics=("arbitrary",), vmem_limit_bytes=VMEM_LIMIT),
        name="ffn",
    )(x2, oa, ob, wo, g2, w1, w2, g3)


def _rope_tables(seq):
    half = A_HEAD_DIM // 2
    inv_freq = 1.0 / (ROPE_THETA ** (jnp.arange(half, dtype=F32) / half))
    ang = jnp.arange(seq).astype(F32)[:, None] * inv_freq[None, :]
    cos = jnp.cos(ang)
    sin = jnp.sin(ang)
    reps = LANES // A_HEAD_DIM
    return (jnp.tile(cos, (1, 2 * reps)),
            jnp.tile(jnp.concatenate([-sin, sin], axis=1), (1, reps)))


def _permute_in_weight(w):
    sizes = (A_WIDTH, KV_WIDTH, KV_WIDTH, IDX_HEADS * IDX_DIM, IDX_DIM, IDX_HEADS,
             B_WIDTH, B_WIDTH, B_WIDTH, B_WIDTH, B_HEADS, B_HEADS)
    parts, off = [], 0
    for s in sizes:
        parts.append(w[:, off:off + s])
        off += s
    qa, ka, va, qi, ki, wi, qb, kb, vb, zb, bb, ab = parts
    pad = jnp.zeros((w.shape[0], LANES - (IDX_DIM + IDX_HEADS + 2 * B_HEADS)), w.dtype)
    return jnp.concatenate([qa, qi, ka, va, qb, kb, vb, zb, ki, wi, bb, ab, pad],
                           axis=1).astype(BF16)


def _lane_row(vals, offset):
    return jnp.zeros((1, LANES), F32).at[0, offset:offset + vals.shape[0]].set(vals.astype(F32))


def kernel(x, norm_mix_g, w_in, conv_w, a_log, dt_bias, gdn_norm_g, w_out,
           norm_ffn_g, w_ff1, w_ff2, norm_final_g):
    bsz, seq, d = x.shape
    depth = w_in.shape[0]
    m = bsz * seq
    cos_t, sin_t = _rope_tables(seq)
    x2 = x.reshape(m, d)
    for l in range(depth):
        qa, qi, kv, gdn_in, sm = _inproj(
            x2, norm_mix_g[l][None, :], _permute_in_weight(w_in[l]), cos_t, sin_t, seq, tm=512)
        o_a = _mixer_a(qa, qi, kv, sm, bsz, seq)
        o_b = _gdn(gdn_in, sm, conv_w[l], _lane_row(a_log[l], SM_DECAY),
                   _lane_row(dt_bias[l], SM_DECAY), gdn_norm_g[l][None, :], bsz, seq, tc=512)
        x2 = _ffn(x2, o_a, o_b, w_out[l].astype(BF16), norm_ffn_g[l][None, :],
                  w_ff1[l].astype(BF16), w_ff2[l].astype(BF16), norm_final_g[None, :],
                  tm=512, final_norm=(l == depth - 1))
    return x2.reshape(bsz, seq, d)
```

```python
import functools

import jax
import jax.numpy as jnp
from jax import lax
from jax.experimental import pallas as pl
from jax.experimental.pallas import tpu as pltpu

F32 = jnp.float32
BF16 = jnp.bfloat16
I32 = jnp.int32

D_MODEL = 1024
CHUNK = 64
A_QUERIES = 256
ROPE_THETA = 10000.0
EPS = 1e-6
A_HEADS = 8
A_KV_HEADS = 2
A_HEAD_DIM = 64
IDX_HEADS = 8
IDX_DIM = 64
TOPK_MAX = 256
B_HEADS = 4
B_HEAD_DIM = 128
CONV_WIDTH = 4
D_FF = 4 * D_MODEL

LANES = 128
A_WIDTH = A_HEADS * A_HEAD_DIM
KV_WIDTH = A_KV_HEADS * A_HEAD_DIM
B_WIDTH = B_HEADS * B_HEAD_DIM
GDN_WIDTH = 4 * B_WIDTH
SM_KI = 0
SM_WI = IDX_DIM
SM_BETA = SM_WI + IDX_HEADS
SM_DECAY = SM_BETA + B_HEADS
C_QA = 0
C_QI = C_QA + A_WIDTH
C_KV = C_QI + A_WIDTH
C_GDN = C_KV + 2 * KV_WIDTH
C_SM = C_GDN + GDN_WIDTH
IN_COLS = C_SM + LANES

VMEM_LIMIT = 56 * 1024 * 1024
COUNT_GROUP = 4
COUNT_ROWS = 32
ATT_HEADS = 2
ATT_WAVE = 4
KEY_BLOCK = 2 * LANES
NEG_BIG = -1e30
LOG2_E = 1.4426950408889634
NEG_INF_KEY = -(2 ** 31) + 0x7FFFFF


def _rms(x, g):
    return x * lax.rsqrt(jnp.mean(x * x, axis=-1, keepdims=True) + EPS) * g


def _dot(a, b):
    return jnp.dot(a, b, preferred_element_type=F32)


def _dot_nt(a, b):
    return lax.dot_general(a, b, (((1,), (1,)), ((), ())), preferred_element_type=F32)


def _inproj_body(x_ref, g_ref, w_ref, cos_ref, sin_ref,
                 qa_ref, qi_ref, kv_ref, gdn_ref, sm_ref):
    h = _rms(x_ref[...], g_ref[...]).astype(BF16)
    cos = cos_ref[...]
    sin = sin_ref[...]
    lane = lax.broadcasted_iota(I32, cos.shape, 1)
    first_half = (lane & (A_HEAD_DIM - 1)) < A_HEAD_DIM // 2

    def rope(t):
        swapped = jnp.where(first_half, pltpu.roll(t, LANES - A_HEAD_DIM // 2, 1),
                            pltpu.roll(t, A_HEAD_DIM // 2, 1))
        return t * cos + swapped * sin

    def proj(c0, width):
        return _dot(h, w_ref[:, c0:c0 + width])

    acc = proj(C_QA, A_WIDTH)
    for j in range(A_WIDTH // LANES):
        sl = slice(j * LANES, (j + 1) * LANES)
        qa_ref[:, sl] = (rope(acc[:, sl]) * (A_HEAD_DIM ** -0.5 * LOG2_E)).astype(BF16)
    acc = proj(C_QI, A_WIDTH)
    for j in range(A_WIDTH // LANES):
        sl = slice(j * LANES, (j + 1) * LANES)
        qi_ref[:, sl] = rope(acc[:, sl]).astype(BF16)
    acc = proj(C_KV, 2 * KV_WIDTH)
    kv_ref[:, 0:KV_WIDTH] = rope(acc[:, 0:KV_WIDTH]).astype(BF16)
    kv_ref[:, KV_WIDTH:] = acc[:, KV_WIDTH:].astype(BF16)
    for j in range(GDN_WIDTH // B_WIDTH):
        gdn_ref[:, j * B_WIDTH:(j + 1) * B_WIDTH] = proj(C_GDN + j * B_WIDTH, B_WIDTH)
    acc = proj(C_SM, LANES)
    sm_ref[...] = jnp.where(lane < IDX_DIM, rope(acc), acc)


def _inproj(x2, g, w, cos_t, sin_t, seq, tm):
    m = x2.shape[0]
    nt = seq // tm
    row = lambda i: (i, 0)
    const = lambda i: (0, 0)
    return pl.pallas_call(
        _inproj_body,
        grid=(m // tm,),
        in_specs=[
            pl.BlockSpec((tm, D_MODEL), row),
            pl.BlockSpec((1, D_MODEL), const),
            pl.BlockSpec((D_MODEL, IN_COLS), const, pipeline_mode=pl.Buffered(1)),
            pl.BlockSpec((tm, LANES), lambda i: (i % nt, 0)),
            pl.BlockSpec((tm, LANES), lambda i: (i % nt, 0)),
        ],
        out_specs=[
            pl.BlockSpec((tm, A_WIDTH), row),
            pl.BlockSpec((tm, A_WIDTH), row),
            pl.BlockSpec((tm, 2 * KV_WIDTH), row),
            pl.BlockSpec((tm, GDN_WIDTH), row),
            pl.BlockSpec((tm, LANES), row),
        ],
        out_shape=[
            jax.ShapeDtypeStruct((m, A_WIDTH), BF16),
            jax.ShapeDtypeStruct((m, A_WIDTH), BF16),
            jax.ShapeDtypeStruct((m, 2 * KV_WIDTH), BF16),
            jax.ShapeDtypeStruct((m, GDN_WIDTH), F32),
            jax.ShapeDtypeStruct((m, LANES), F32),
        ],
        compiler_params=pltpu.CompilerParams(
            dimension_semantics=("arbitrary",), vmem_limit_bytes=VMEM_LIMIT),
        name="inproj",
    )(x2, g, w, cos_t, sin_t)


def _mixer_a_body(qa_ref, qi_ref, kv_ref, sm_ref, o_ref,
                  isc_ref, vt_ref, qit_ref, qat_ref, acc_ref, bias_ref, *, seq, topk):
    j = pl.program_id(1)
    nkt = (j + 1) * (A_QUERIES // LANES)
    n_heads_pair = A_WIDTH // LANES
    group = A_HEADS // A_KV_HEADS

    @pl.when(j == 0)
    def _():
        per_block = KEY_BLOCK // LANES
        for t in range(seq // LANES):
            vt = kv_ref[t * LANES:(t + 1) * LANES, KV_WIDTH:].astype(F32)
            c0 = (t % per_block) * LANES
            vt_ref[t // per_block, :, c0:c0 + LANES] = vt.T.astype(BF16)

    for p in range(n_heads_pair):
        sl = slice(p * LANES, (p + 1) * LANES)
        t = qi_ref[:, sl].astype(F32).T
        qit_ref[:, (2 * p) * A_QUERIES:(2 * p + 1) * A_QUERIES] = t[0:IDX_DIM].astype(BF16)
        qit_ref[:, (2 * p + 1) * A_QUERIES:(2 * p + 2) * A_QUERIES] = t[IDX_DIM:].astype(BF16)
        t = qa_ref[:, sl].astype(F32).T
        qat_ref[:, (2 * p) * A_QUERIES:(2 * p + 1) * A_QUERIES] = t[0:A_HEAD_DIM].astype(BF16)
        qat_ref[:, (2 * p + 1) * A_QUERIES:(2 * p + 2) * A_QUERIES] = t[A_HEAD_DIM:].astype(BF16)

    q0 = pl.multiple_of(j * A_QUERIES, A_QUERIES)
    w_t = sm_ref[pl.ds(q0, A_QUERIES), :].T[SM_WI:SM_WI + IDX_HEADS, :]
    w_t = w_t * ((IDX_HEADS ** -0.5) * (IDX_DIM ** -0.5))

    qlane = lax.broadcasted_iota(I32, (1, A_QUERIES), 1)
    limit = q0 + (lax.shift_right_logical(qlane, CHUNK.bit_length() - 1) + 1) * CHUNK
    tile_iota = lax.broadcasted_iota(I32, (LANES, A_QUERIES), 0)
    block_iota = lax.broadcasted_iota(I32, (KEY_BLOCK, A_QUERIES), 0)
    nkb = lax.shift_right_logical(nkt + (KEY_BLOCK // LANES - 1), (KEY_BLOCK // LANES).bit_length() - 1)

    def isc_body(kb, carry):
        r0 = pl.multiple_of(kb * KEY_BLOCK, KEY_BLOCK)
        kid = sm_ref[pl.ds(r0, KEY_BLOCK), SM_KI:SM_KI + IDX_DIM].astype(BF16)
        rel = _dot(kid, qit_ref[...])
        acc = jnp.zeros((KEY_BLOCK, A_QUERIES), F32)
        for h in range(IDX_HEADS):
            acc = acc + w_t[h:h + 1, :] * jnp.maximum(rel[:, h * A_QUERIES:(h + 1) * A_QUERIES], 0.0)
        isc_ref[pl.ds(r0, KEY_BLOCK), :] = jnp.where(r0 + block_iota < limit, acc, -jnp.inf)
        return carry

    lax.fori_loop(0, nkb, isc_body, 0)

    ngrp = lax.shift_right_logical(nkt + (COUNT_GROUP - 1), COUNT_GROUP.bit_length() - 1)

    def fill_body(kt, carry):
        r0 = pl.multiple_of(kt * LANES, LANES)
        isc_ref[pl.ds(r0, LANES), :] = jnp.full((LANES, A_QUERIES), -jnp.inf, F32)
        return carry

    lax.fori_loop(nkb * (KEY_BLOCK // LANES), ngrp * COUNT_GROUP, fill_body, 0)

    def count(pred):
        def body(g, acc):
            for t in range(COUNT_GROUP):
                r0 = pl.multiple_of((g * COUNT_GROUP + t) * LANES, LANES)
                hit = jnp.where(pred(isc_ref[pl.ds(r0, LANES), :], r0 + tile_iota), 1.0, 0.0)
                acc = acc + hit.reshape(LANES // COUNT_ROWS, COUNT_ROWS, A_QUERIES).sum(axis=0)
            return acc
        acc = lax.fori_loop(0, ngrp, body, jnp.zeros((COUNT_ROWS, A_QUERIES), F32))
        return acc.sum(axis=0, keepdims=True)

    kf = float(topk)

    def thr_of(key):
        bits = jnp.where(key >= 0, key, key ^ jnp.int32(0x7FFFFFFF))
        return lax.bitcast_convert_type(bits, F32)

    c0 = count(lambda x, s: x >= 0.0)
    lo0 = jnp.where(c0 >= kf, jnp.int32(0), jnp.int32(-2 ** 31))
    c_lo0 = jnp.where(c0 >= kf, c0, -1.0)

    def bis_body(i, st):
        lo, c_lo = st
        trial = lo | lax.shift_left(jnp.int32(1), 30 - i)
        thr = thr_of(trial)
        c = count(lambda x, s: x >= thr)
        ok = c >= kf
        return jnp.where(ok, trial, lo), jnp.where(ok, c, c_lo)

    lo, c_lo = lax.fori_loop(0, 31, bis_body, (lo0, c_lo0))
    kth = jnp.where(lo < jnp.int32(NEG_INF_KEY), -jnp.inf, thr_of(lo))

    idx_bits = (seq - 1).bit_length()

    def tie_search():
        need = kf - count(lambda x, s: x > kth)

        def tie_body(i, last):
            trial = last | lax.shift_left(jnp.int32(1), idx_bits - 1 - i)
            c = count(lambda x, s: (x == kth) & (s < trial))
            return jnp.where(c < need, trial, last)

        return lax.fori_loop(0, idx_bits, tie_body, jnp.zeros((1, A_QUERIES), I32))

    def take_all_ties():
        return jnp.full((1, A_QUERIES), (1 << idx_bits) - 1, I32)

    n_ge = jnp.where(c_lo < 0.0, jnp.inf, c_lo)
    last = lax.cond(jnp.max(n_ge) > kf, tie_search, take_all_ties)

    acc_ref[...] = jnp.zeros_like(acc_ref)

    def att_body(kb, carry):
        m_prev, l_prev = carry
        r0 = pl.multiple_of(kb * KEY_BLOCK, KEY_BLOCK)
        x = isc_ref[pl.ds(r0, KEY_BLOCK), :]
        s_idx = r0 + block_iota
        sel = (s_idx < limit) & ((x > kth) | ((x == kth) & (s_idx <= last)))
        bias_ref[...] = jnp.where(sel, 0.0, NEG_BIG)
        k_tile = kv_ref[pl.ds(r0, KEY_BLOCK), 0:KV_WIDTH]
        v_t = vt_ref[kb]

        units = list(range(A_HEADS // ATT_HEADS))
        uw = ATT_HEADS * A_QUERIES
        us = [slice(u * uw, (u + 1) * uw) for u in units]
        ds = [slice((u * ATT_HEADS // group) * A_HEAD_DIM, (u * ATT_HEADS // group + 1) * A_HEAD_DIM)
              for u in units]
        cs = [slice((u * ATT_HEADS % group) * A_QUERIES, (u * ATT_HEADS % group + ATT_HEADS) * A_QUERIES)
              for u in units]
        m_new, l_new = {}, {}
        for w0 in range(0, len(units), ATT_WAVE):
            wave = units[w0:w0 + ATT_WAVE]
            s = {u: _dot(k_tile[:, ds[u]], qat_ref[:, us[u]])
                 + jnp.concatenate([bias_ref[...]] * ATT_HEADS, axis=1) for u in wave}
            for u in wave:
                m_new[u] = jnp.maximum(m_prev[:, us[u]], jnp.max(s[u], axis=0, keepdims=True))
            alpha = {u: jnp.exp2(m_prev[:, us[u]] - m_new[u]) for u in wave}
            p = {u: jnp.exp2(s[u] - m_new[u]) for u in wave}
            pv = {u: _dot(v_t[ds[u], :], p[u].astype(BF16)) for u in wave}
            for u in wave:
                l_new[u] = alpha[u] * l_prev[:, us[u]] + jnp.sum(p[u], axis=0, keepdims=True)
            for u in wave:
                acc_ref[ds[u], cs[u]] = alpha[u] * acc_ref[ds[u], cs[u]] + pv[u]
        return (jnp.concatenate([m_new[u] for u in units], axis=1),
                jnp.concatenate([l_new[u] for u in units], axis=1))

    m0 = jnp.full((1, A_HEADS * A_QUERIES), NEG_BIG, F32)
    l0 = jnp.zeros((1, A_HEADS * A_QUERIES), F32)
    _, l_fin = lax.fori_loop(0, nkb, att_body, (m0, l0))

    for p in range(n_heads_pair):
        g = (2 * p) // group
        ds = slice(g * A_HEAD_DIM, (g + 1) * A_HEAD_DIM)
        parts = []
        for h in (2 * p, 2 * p + 1):
            hl = h % group
            parts.append(acc_ref[ds, hl * A_QUERIES:(hl + 1) * A_QUERIES]
                         / l_fin[:, h * A_QUERIES:(h + 1) * A_QUERIES])
        o_ref[:, p * LANES:(p + 1) * LANES] = jnp.concatenate(parts, axis=0).T.astype(BF16)


def _mixer_a(qa, qi, kv, sm, bsz, seq):
    nq = seq // A_QUERIES
    assert seq % (COUNT_GROUP * LANES) == 0, seq
    topk = min(TOPK_MAX, seq // 4)
    qrow = lambda b, j: (b * nq + j, 0)
    brow = lambda b, j: (b, 0)
    return pl.pallas_call(
        functools.partial(_mixer_a_body, seq=seq, topk=topk),
        grid=(bsz, nq),
        in_specs=[
            pl.BlockSpec((A_QUERIES, A_WIDTH), qrow),
            pl.BlockSpec((A_QUERIES, A_WIDTH), qrow),
            pl.BlockSpec((seq, 2 * KV_WIDTH), brow),
            pl.BlockSpec((seq, LANES), brow),
        ],
        out_specs=pl.BlockSpec((A_QUERIES, A_WIDTH), qrow),
        out_shape=jax.ShapeDtypeStruct((bsz * seq, A_WIDTH), BF16),
        scratch_shapes=[
            pltpu.VMEM((seq, A_QUERIES), F32),
            pltpu.VMEM((seq // KEY_BLOCK, KV_WIDTH, KEY_BLOCK), BF16),
            pltpu.VMEM((IDX_DIM, IDX_HEADS * A_QUERIES), BF16),
            pltpu.VMEM((A_HEAD_DIM, A_HEADS * A_QUERIES), BF16),
            pltpu.VMEM((KV_WIDTH, (A_HEADS // A_KV_HEADS) * A_QUERIES), F32),
            pltpu.VMEM((KEY_BLOCK, A_QUERIES), F32),
        ],
        compiler_params=pltpu.CompilerParams(
            dimension_semantics=("arbitrary", "arbitrary"), vmem_limit_bytes=VMEM_LIMIT),
        name="mixer_a",
    )(qa, qi, kv, sm)


def _gdn_body(gdn_ref, sm_ref, cw_ref, alog_ref, dtb_ref, ng_ref, o_ref,
              xpad_ref, q_ref, k_ref, v_ref, gate_ref, gct_ref, state_ref, oc_ref,
              u_ref, wq_ref, ak_ref, *, tc):
    tb = pl.program_id(1)
    n_chunks = tc // CHUNK
    prep_unroll = 4
    conv_cols = 3 * B_WIDTH
    pad = 8

    @pl.when(tb == 0)
    def _():
        xpad_ref[0:pad, :] = jnp.zeros((pad, conv_cols), F32)
        state_ref[...] = jnp.zeros_like(state_ref)

    @pl.when(tb > 0)
    def _():
        xpad_ref[0:pad, :] = xpad_ref[tc:tc + pad, :]

    xpad_ref[pad:pad + tc, :] = gdn_ref[:, 0:conv_cols]

    for ct in range(conv_cols // LANES):
        cs = slice(ct * LANES, (ct + 1) * LANES)
        y = jnp.zeros((tc, LANES), F32)
        for jj in range(CONV_WIDTH):
            r = pad - (CONV_WIDTH - 1) + jj
            y = y + cw_ref[jj:jj + 1, cs] * xpad_ref[r:r + tc, cs]
        y = y * jax.nn.sigmoid(y)
        seg, hh = divmod(ct, B_HEADS)
        hs = slice(hh * LANES, (hh + 1) * LANES)
        if seg < 2:
            y = y * lax.rsqrt(jnp.sum(y * y, axis=-1, keepdims=True) + EPS)
        if seg == 0:
            q_ref[:, hs] = y * (B_HEAD_DIM ** -0.5)
        elif seg == 1:
            k_ref[:, hs] = y
        else:
            v_ref[:, hs] = y

    sm = sm_ref[...]
    beta = jax.nn.sigmoid(sm)
    z = sm + dtb_ref[...]
    softplus = jnp.maximum(z, 0.0) + jnp.log(1.0 + jnp.exp(-jnp.abs(z)))
    g = -jnp.exp(alog_ref[...]) * softplus
    rin = lax.broadcasted_iota(I32, (tc, LANES), 0) & (CHUNK - 1)
    gc = g
    step = 1
    while step < CHUNK:
        gc = gc + jnp.where(rin >= step, pltpu.roll(gc, step, 0), 0.0)
        step *= 2
    gc3 = gc.reshape(n_chunks, CHUNK, LANES)
    g_last = jnp.broadcast_to(gc3[:, CHUNK - 1:CHUNK, :], gc3.shape).reshape(tc, LANES)
    gate_ref[0] = beta
    gate_ref[1] = gc
    gate_ref[2] = jnp.exp(gc)
    gate_ref[3] = jnp.exp(g_last - gc)
    gate_ref[4] = jnp.exp(g_last)
    for i in range(tc // LANES):
        t = gc[i * LANES:(i + 1) * LANES, :].T
        for half in range(LANES // CHUNK):
            gct_ref[i * (LANES // CHUNK) + half] = t[:, half * CHUNK:(half + 1) * CHUNK]

    ci = lax.broadcasted_iota(I32, (CHUNK, CHUNK), 0)
    si = lax.broadcasted_iota(I32, (CHUNK, CHUNK), 1)
    wl = lax.broadcasted_iota(I32, (CHUNK, 2 * CHUNK), 1)
    wr = lax.broadcasted_iota(I32, (CHUNK, 2 * CHUNK), 0)
    right = wl >= CHUNK
    eye_right = jnp.where(wl == wr + CHUNK, 1.0, 0.0)

    def prep_body(cg, carry):
        units = [(cg * prep_unroll + cc, h) for cc in range(prep_unroll) for h in range(B_HEADS)]
        rows = [pl.ds(pl.multiple_of(c * CHUNK, CHUNK), CHUNK) for c, _ in units]
        hsl = [slice(h * LANES, (h + 1) * LANES) for _, h in units]
        idx = range(len(units))

        def col(i, gate, off):
            h = units[i][1]
            return gate_ref[gate, rows[i], off + h:off + h + 1]

        q = [q_ref[rows[i], hsl[i]] for i in idx]
        k = [k_ref[rows[i], hsl[i]] for i in idx]
        kb = [k[i] * col(i, 0, SM_BETA) for i in idx]
        kq = [_dot_nt(jnp.concatenate([kb[i], q[i]], axis=0).astype(BF16), k[i].astype(BF16))
              for i in idx]
        decay = []
        for i, (c, h) in enumerate(units):
            d = col(i, 1, SM_DECAY) - gct_ref[c][SM_DECAY + h:SM_DECAY + h + 1, :]
            decay.append(jnp.where(ci >= si, jnp.exp(jnp.where(ci >= si, d, 0.0)), 0.0))
        wmat = []
        for i, (c, h) in enumerate(units):
            n_mat = jnp.where(ci > si, -(kq[i][0:CHUNK] * decay[i]), 0.0)
            wmat.append(jnp.concatenate([n_mat, jnp.zeros_like(n_mat)], axis=1) + eye_right)
            ak_ref[c, h, 0:CHUNK, :] = (kq[i][CHUNK:] * decay[i]).astype(BF16)
        pw = 1
        while pw < CHUNK:
            wb = [wmat[i].astype(BF16) for i in idx]
            wmat = [_dot(wb[i][:, 0:CHUNK], wb[i]) + jnp.where(right, wmat[i], 0.0) for i in idx]
            pw *= 2
        eg = [col(i, 2, SM_DECAY) for i in idx]
        rhs = [jnp.concatenate([v_ref[rows[i], hsl[i]] * col(i, 0, SM_BETA), kb[i] * eg[i]],
                               axis=1).astype(BF16) for i in idx]
        sol = [_dot(wmat[i][:, CHUNK:].astype(BF16), rhs[i]) for i in idx]
        for i, (c, h) in enumerate(units):
            u_ref[rows[i], hsl[i]] = sol[i][:, 0:B_HEAD_DIM]
            wq_ref[c, h] = jnp.concatenate([sol[i][:, B_HEAD_DIM:], q[i] * eg[i]],
                                           axis=0).astype(BF16)
            ak_ref[c, h, CHUNK:, :] = (k[i] * col(i, 3, SM_DECAY)).T.astype(BF16)
        return carry

    lax.fori_loop(0, n_chunks // prep_unroll, prep_body, 0)

    def scan_body(c, carry):
        r0 = pl.multiple_of(c * CHUNK, CHUNK)
        rows = pl.ds(r0, CHUNK)
        heads = range(B_HEADS)
        hsl = [slice(h * LANES, (h + 1) * LANES) for h in heads]
        s_prev = [state_ref[h] for h in heads]
        ws = [_dot(wq_ref[c, h], s_prev[h].astype(BF16)) for h in heads]
        v_new = [(u_ref[rows, hsl[h]] - ws[h][0:CHUNK]).astype(BF16) for h in heads]
        r = [_dot(ak_ref[c, h], v_new[h]) for h in heads]
        for h in heads:
            gl = gate_ref[4, rows, SM_DECAY + h:SM_DECAY + h + 1][0:1, :]
            oc_ref[rows, hsl[h]] = ws[h][CHUNK:] + r[h][0:CHUNK]
            state_ref[h] = s_prev[h] * gl + r[h][CHUNK:]
        return carry

    lax.fori_loop(0, n_chunks, scan_body, 0)

    for h in range(B_HEADS):
        hs = slice(h * LANES, (h + 1) * LANES)
        zf = gdn_ref[:, conv_cols + h * LANES:conv_cols + (h + 1) * LANES]
        o_ref[:, hs] = (_rms(oc_ref[:, hs], ng_ref[...]) * (zf * jax.nn.sigmoid(zf))).astype(BF16)


def _gdn(gdn_in, sm, conv_w, alog_row, dtb_row, norm_g, bsz, seq, tc):
    nt = seq // tc
    trow = lambda b, t: (b * nt + t, 0)
    const = lambda b, t: (0, 0)
    n_chunks = tc // CHUNK
    return pl.pallas_call(
        functools.partial(_gdn_body, tc=tc),
        grid=(bsz, nt),
        in_specs=[
            pl.BlockSpec((tc, GDN_WIDTH), trow),
            pl.BlockSpec((tc, LANES), trow),
            pl.BlockSpec((CONV_WIDTH, 3 * B_WIDTH), const),
            pl.BlockSpec((1, LANES), const),
            pl.BlockSpec((1, LANES), const),
            pl.BlockSpec((1, B_HEAD_DIM), const),
        ],
        out_specs=pl.BlockSpec((tc, B_WIDTH), trow),
        out_shape=jax.ShapeDtypeStruct((bsz * seq, B_WIDTH), BF16),
        scratch_shapes=[
            pltpu.VMEM((tc + 8, 3 * B_WIDTH), F32),
            pltpu.VMEM((tc, B_WIDTH), F32),
            pltpu.VMEM((tc, B_WIDTH), F32),
            pltpu.VMEM((tc, B_WIDTH), F32),
            pltpu.VMEM((5, tc, LANES), F32),
            pltpu.VMEM((n_chunks, LANES, CHUNK), F32),
            pltpu.VMEM((B_HEADS, B_HEAD_DIM, B_HEAD_DIM), F32),
            pltpu.VMEM((tc, B_WIDTH), F32),
            pltpu.VMEM((tc, B_WIDTH), F32),
            pltpu.VMEM((n_chunks, B_HEADS, 2 * CHUNK, B_HEAD_DIM), BF16),
            pltpu.VMEM((n_chunks, B_HEADS, CHUNK + B_HEAD_DIM, CHUNK), BF16),
        ],
        compiler_params=pltpu.CompilerParams(
            dimension_semantics=("arbitrary", "arbitrary"), vmem_limit_bytes=VMEM_LIMIT),
        name="gdn",
    )(gdn_in, sm, conv_w, alog_row, dtb_row, norm_g)


def _ffn_body(x_ref, oa_ref, ob_ref, wo_ref, g2_ref, w1_ref, w2_ref, g3_ref, out_ref, *,
              final_norm):
    y = (x_ref[...] + _dot(oa_ref[...], wo_ref[0:A_WIDTH, :])
         + _dot(ob_ref[...], wo_ref[A_WIDTH:, :]))
    h = _rms(y, g2_ref[...]).astype(BF16)
    a = jnp.square(jnp.maximum(_dot(h, w1_ref[...]), 0.0)).astype(BF16)
    acc = y + _dot(a, w2_ref[...])
    out_ref[...] = _rms(acc, g3_ref[...]) if final_norm else acc


def _ffn(x2, oa, ob, wo, g2, w1, w2, g3, tm, final_norm):
    m = x2.shape[0]
    row = lambda i: (i, 0)
    const = lambda i: (0, 0)
    resident = functools.partial(pl.BlockSpec, index_map=const, pipeline_mode=pl.Buffered(1))
    return pl.pallas_call(
        functools.partial(_ffn_body, final_norm=final_norm),
        grid=(m // tm,),
        in_specs=[
            pl.BlockSpec((tm, D_MODEL), row),
            pl.BlockSpec((tm, A_WIDTH), row),
            pl.BlockSpec((tm, B_WIDTH), row),
            resident((D_MODEL, D_MODEL)),
            pl.BlockSpec((1, D_MODEL), const),
            resident((D_MODEL, D_FF)),
            resident((D_FF, D_MODEL)),
            pl.BlockSpec((1, D_MODEL), const),
        ],
        out_specs=pl.BlockSpec((tm, D_MODEL), row),
        out_shape=jax.ShapeDtypeStruct((m, D_MODEL), F32),
        compiler_params=pltpu.CompilerParams(
            dimension_semantics=("arbitrary",), vmem_limit_bytes=VMEM_LIMIT),
        name="ffn",
    )(x2, oa, ob, wo, g2, w1, w2, g3)


def _rope_tables(seq):
    half = A_HEAD_DIM // 2
    inv_freq = 1.0 / (ROPE_THETA ** (jnp.arange(half, dtype=F32) / half))
    ang = jnp.arange(seq).astype(F32)[:, None] * inv_freq[None, :]
    cos = jnp.cos(ang)
    sin = jnp.sin(ang)
    reps = LANES // A_HEAD_DIM
    return (jnp.tile(cos, (1, 2 * reps)),
            jnp.tile(jnp.concatenate([-sin, sin], axis=1), (1, reps)))


def _permute_in_weight(w):
    sizes = (A_WIDTH, KV_WIDTH, KV_WIDTH, IDX_HEADS * IDX_DIM, IDX_DIM, IDX_HEADS,
             B_WIDTH, B_WIDTH, B_WIDTH, B_WIDTH, B_HEADS, B_HEADS)
    parts, off = [], 0
    for s in sizes:
        parts.append(w[:, off:off + s])
        off += s
    qa, ka, va, qi, ki, wi, qb, kb, vb, zb, bb, ab = parts
    pad = jnp.zeros((w.shape[0], LANES - (IDX_DIM + IDX_HEADS + 2 * B_HEADS)), w.dtype)
    return jnp.concatenate([qa, qi, ka, va, qb, kb, vb, zb, ki, wi, bb, ab, pad],
                           axis=1).astype(BF16)


def _lane_row(vals, offset):
    return jnp.zeros((1, LANES), F32).at[0, offset:offset + vals.shape[0]].set(vals.astype(F32))


def kernel(x, norm_mix_g, w_in, conv_w, a_log, dt_bias, gdn_norm_g, w_out,
           norm_ffn_g, w_ff1, w_ff2, norm_final_g):
    bsz, seq, d = x.shape
    depth = w_in.shape[0]
    m = bsz * seq
    cos_t, sin_t = _rope_tables(seq)
    x2 = x.reshape(m, d)
    for l in range(depth):
        qa, qi, kv, gdn_in, sm = _inproj(
            x2, norm_mix_g[l][None, :], _permute_in_weight(w_in[l]), cos_t, sin_t, seq, tm=512)
        o_a = _mixer_a(qa, qi, kv, sm, bsz, seq)
        o_b = _gdn(gdn_in, sm, conv_w[l], _lane_row(a_log[l], SM_DECAY),
                   _lane_row(dt_bias[l], SM_DECAY), gdn_norm_g[l][None, :], bsz, seq, tc=512)
        x2 = _ffn(x2, o_a, o_b, w_out[l].astype(BF16), norm_ffn_g[l][None, :],
                  w_ff1[l].astype(BF16), w_ff2[l].astype(BF16), norm_final_g[None, :],
                  tm=512, final_norm=(l == depth - 1))
    return x2.reshape(bsz, seq, d)
```

```python
import functools

import jax
import jax.numpy as jnp
from jax import lax
from jax.experimental import pallas as pl
from jax.experimental.pallas import tpu as pltpu

F32 = jnp.float32
BF16 = jnp.bfloat16
I32 = jnp.int32

D_MODEL = 1024
CHUNK = 64
A_QUERIES = 256
ROPE_THETA = 10000.0
EPS = 1e-6
A_HEADS = 8
A_KV_HEADS = 2
A_HEAD_DIM = 64
IDX_HEADS = 8
IDX_DIM = 64
TOPK_MAX = 256
B_HEADS = 4
B_HEAD_DIM = 128
CONV_WIDTH = 4
D_FF = 4 * D_MODEL

LANES = 128
A_WIDTH = A_HEADS * A_HEAD_DIM
KV_WIDTH = A_KV_HEADS * A_HEAD_DIM
B_WIDTH = B_HEADS * B_HEAD_DIM
GDN_WIDTH = 4 * B_WIDTH
SM_KI = 0
SM_WI = IDX_DIM
SM_BETA = SM_WI + IDX_HEADS
SM_DECAY = SM_BETA + B_HEADS
C_QA = 0
C_QI = C_QA + A_WIDTH
C_KV = C_QI + A_WIDTH
C_GDN = C_KV + 2 * KV_WIDTH
C_SM = C_GDN + GDN_WIDTH
IN_COLS = C_SM + LANES

VMEM_LIMIT = 56 * 1024 * 1024
CONV_PAD = 8
COUNT_GROUP = 4
COUNT_ROWS = 32
ATT_HEADS = 2
ATT_WAVE = 4
KEY_BLOCK = 2 * LANES
NEG_BIG = -1e30
LOG2_E = 1.4426950408889634
NEG_INF_KEY = -(2 ** 31) + 0x7FFFFF


def _rms(x, g):
    return x * lax.rsqrt(jnp.mean(x * x, axis=-1, keepdims=True) + EPS) * g


def _dot(a, b):
    return jnp.dot(a, b, preferred_element_type=F32)


def _dot_nt(a, b):
    return lax.dot_general(a, b, (((1,), (1,)), ((), ())), preferred_element_type=F32)


def _inproj_body(x_ref, g_ref, w_ref, cos_ref, sin_ref, cw_ref,
                 qa_ref, qi_ref, kv_ref, gdn_ref, sm_ref, xpad_ref, *, tiles_per_seq):
    h = _rms(x_ref[...], g_ref[...]).astype(BF16)
    cos = cos_ref[...]
    sin = sin_ref[...]
    lane = lax.broadcasted_iota(I32, cos.shape, 1)
    first_half = (lane & (A_HEAD_DIM - 1)) < A_HEAD_DIM // 2

    def rope(t):
        swapped = jnp.where(first_half, pltpu.roll(t, LANES - A_HEAD_DIM // 2, 1),
                            pltpu.roll(t, A_HEAD_DIM // 2, 1))
        return t * cos + swapped * sin

    def proj(c0, width):
        return _dot(h, w_ref[:, c0:c0 + width])

    tm = x_ref.shape[0]
    conv_cols = 3 * B_WIDTH

    @pl.when(pl.program_id(0) % tiles_per_seq == 0)
    def _():
        xpad_ref[0:CONV_PAD, :] = jnp.zeros((CONV_PAD, conv_cols), F32)

    @pl.when(pl.program_id(0) % tiles_per_seq != 0)
    def _():
        xpad_ref[0:CONV_PAD, :] = xpad_ref[tm:tm + CONV_PAD, :]

    def stage_conv_input(seg):
        xpad_ref[CONV_PAD:CONV_PAD + tm, seg * B_WIDTH:(seg + 1) * B_WIDTH] = proj(
            C_GDN + seg * B_WIDTH, B_WIDTH)

    def conv_segment(seg):
        for hh in range(B_HEADS):
            cs = slice(seg * B_WIDTH + hh * LANES, seg * B_WIDTH + (hh + 1) * LANES)
            y = jnp.zeros((tm, LANES), F32)
            for jj in range(CONV_WIDTH):
                r = CONV_PAD - (CONV_WIDTH - 1) + jj
                y = y + cw_ref[jj:jj + 1, cs] * xpad_ref[r:r + tm, cs]
            y = y * jax.nn.sigmoid(y)
            if seg < 2:
                y = y * lax.rsqrt(jnp.sum(y * y, axis=-1, keepdims=True) + EPS)
            if seg == 0:
                y = y * (B_HEAD_DIM ** -0.5)
            gdn_ref[:, cs] = y

    stage_conv_input(0)
    acc = proj(C_QA, A_WIDTH)
    conv_segment(0)
    for j in range(A_WIDTH // LANES):
        sl = slice(j * LANES, (j + 1) * LANES)
        qa_ref[:, sl] = (rope(acc[:, sl]) * (A_HEAD_DIM ** -0.5 * LOG2_E)).astype(BF16)
    stage_conv_input(1)
    acc = proj(C_QI, A_WIDTH)
    conv_segment(1)
    for j in range(A_WIDTH // LANES):
        sl = slice(j * LANES, (j + 1) * LANES)
        qi_ref[:, sl] = rope(acc[:, sl]).astype(BF16)
    stage_conv_input(2)
    acc = proj(C_KV, 2 * KV_WIDTH)
    acc_sm = proj(C_SM, LANES)
    z = proj(C_GDN + conv_cols, B_WIDTH)
    conv_segment(2)
    kv_ref[:, 0:KV_WIDTH] = rope(acc[:, 0:KV_WIDTH]).astype(BF16)
    kv_ref[:, KV_WIDTH:] = acc[:, KV_WIDTH:].astype(BF16)
    sm_ref[...] = jnp.where(lane < IDX_DIM, rope(acc_sm), acc_sm)
    gdn_ref[:, conv_cols:] = z * jax.nn.sigmoid(z)


def _inproj(x2, g, w, cos_t, sin_t, conv_w, seq, tm):
    m = x2.shape[0]
    assert seq % tm == 0, (seq, tm)
    nt = seq // tm
    row = lambda i: (i, 0)
    const = lambda i: (0, 0)
    return pl.pallas_call(
        functools.partial(_inproj_body, tiles_per_seq=nt),
        grid=(m // tm,),
        in_specs=[
            pl.BlockSpec((tm, D_MODEL), row),
            pl.BlockSpec((1, D_MODEL), const),
            pl.BlockSpec((D_MODEL, IN_COLS), const, pipeline_mode=pl.Buffered(1)),
            pl.BlockSpec((tm, LANES), lambda i: (i % nt, 0)),
            pl.BlockSpec((tm, LANES), lambda i: (i % nt, 0)),
            pl.BlockSpec((CONV_WIDTH, 3 * B_WIDTH), const),
        ],
        out_specs=[
            pl.BlockSpec((tm, A_WIDTH), row),
            pl.BlockSpec((tm, A_WIDTH), row),
            pl.BlockSpec((tm, 2 * KV_WIDTH), row),
            pl.BlockSpec((tm, GDN_WIDTH), row),
            pl.BlockSpec((tm, LANES), row),
        ],
        out_shape=[
            jax.ShapeDtypeStruct((m, A_WIDTH), BF16),
            jax.ShapeDtypeStruct((m, A_WIDTH), BF16),
            jax.ShapeDtypeStruct((m, 2 * KV_WIDTH), BF16),
            jax.ShapeDtypeStruct((m, GDN_WIDTH), F32),
            jax.ShapeDtypeStruct((m, LANES), F32),
        ],
        scratch_shapes=[pltpu.VMEM((tm + CONV_PAD, 3 * B_WIDTH), F32)],
        compiler_params=pltpu.CompilerParams(
            dimension_semantics=("arbitrary",), vmem_limit_bytes=VMEM_LIMIT),
        name="inproj",
    )(x2, g, w, cos_t, sin_t, conv_w)


def _mixer_a_body(qa_ref, qi_ref, kv_ref, sm_ref, o_ref,
                  isc_ref, vt_ref, qit_ref, qat_ref, acc_ref, bias_ref, *, seq, topk):
    j = pl.program_id(1)
    nkt = (j + 1) * (A_QUERIES // LANES)
    n_heads_pair = A_WIDTH // LANES
    group = A_HEADS // A_KV_HEADS

    @pl.when(j == 0)
    def _():
        per_block = KEY_BLOCK // LANES
        for t in range(seq // LANES):
            vt = kv_ref[t * LANES:(t + 1) * LANES, KV_WIDTH:].astype(F32)
            c0 = (t % per_block) * LANES
            vt_ref[t // per_block, :, c0:c0 + LANES] = vt.T.astype(BF16)

    for p in range(n_heads_pair):
        sl = slice(p * LANES, (p + 1) * LANES)
        t = qi_ref[:, sl].astype(F32).T
        qit_ref[:, (2 * p) * A_QUERIES:(2 * p + 1) * A_QUERIES] = t[0:IDX_DIM].astype(BF16)
        qit_ref[:, (2 * p + 1) * A_QUERIES:(2 * p + 2) * A_QUERIES] = t[IDX_DIM:].astype(BF16)
        t = qa_ref[:, sl].astype(F32).T
        qat_ref[:, (2 * p) * A_QUERIES:(2 * p + 1) * A_QUERIES] = t[0:A_HEAD_DIM].astype(BF16)
        qat_ref[:, (2 * p + 1) * A_QUERIES:(2 * p + 2) * A_QUERIES] = t[A_HEAD_DIM:].astype(BF16)

    q0 = pl.multiple_of(j * A_QUERIES, A_QUERIES)
    w_t = sm_ref[pl.ds(q0, A_QUERIES), :].T[SM_WI:SM_WI + IDX_HEADS, :]
    w_t = w_t * ((IDX_HEADS ** -0.5) * (IDX_DIM ** -0.5))

    qlane = lax.broadcasted_iota(I32, (1, A_QUERIES), 1)
    limit = q0 + (lax.shift_right_logical(qlane, CHUNK.bit_length() - 1) + 1) * CHUNK
    tile_iota = lax.broadcasted_iota(I32, (LANES, A_QUERIES), 0)
    block_iota = lax.broadcasted_iota(I32, (KEY_BLOCK, A_QUERIES), 0)
    nkb = lax.shift_right_logical(nkt + (KEY_BLOCK // LANES - 1), (KEY_BLOCK // LANES).bit_length() - 1)

    def isc_body(kb, carry):
        r0 = pl.multiple_of(kb * KEY_BLOCK, KEY_BLOCK)
        kid = sm_ref[pl.ds(r0, KEY_BLOCK), SM_KI:SM_KI + IDX_DIM].astype(BF16)
        rel = _dot(kid, qit_ref[...])
        acc = jnp.zeros((KEY_BLOCK, A_QUERIES), F32)
        for h in range(IDX_HEADS):
            acc = acc + w_t[h:h + 1, :] * jnp.maximum(rel[:, h * A_QUERIES:(h + 1) * A_QUERIES], 0.0)
        isc_ref[pl.ds(r0, KEY_BLOCK), :] = jnp.where(r0 + block_iota < limit, acc, -jnp.inf)
        return carry

    lax.fori_loop(0, nkb, isc_body, 0)

    ngrp = lax.shift_right_logical(nkt + (COUNT_GROUP - 1), COUNT_GROUP.bit_length() - 1)

    def fill_body(kt, carry):
        r0 = pl.multiple_of(kt * LANES, LANES)
        isc_ref[pl.ds(r0, LANES), :] = jnp.full((LANES, A_QUERIES), -jnp.inf, F32)
        return carry

    lax.fori_loop(nkb * (KEY_BLOCK // LANES), ngrp * COUNT_GROUP, fill_body, 0)

    def count(pred):
        def body(g, acc):
            for t in range(COUNT_GROUP):
                r0 = pl.multiple_of((g * COUNT_GROUP + t) * LANES, LANES)
                hit = jnp.where(pred(isc_ref[pl.ds(r0, LANES), :], r0 + tile_iota), 1.0, 0.0)
                acc = acc + hit.reshape(LANES // COUNT_ROWS, COUNT_ROWS, A_QUERIES).sum(axis=0)
            return acc
        acc = lax.fori_loop(0, ngrp, body, jnp.zeros((COUNT_ROWS, A_QUERIES), F32))
        return acc.sum(axis=0, keepdims=True)

    kf = float(topk)

    def thr_of(key):
        bits = jnp.where(key >= 0, key, key ^ jnp.int32(0x7FFFFFFF))
        return lax.bitcast_convert_type(bits, F32)

    c0 = count(lambda x, s: x >= 0.0)
    lo0 = jnp.where(c0 >= kf, jnp.int32(0), jnp.int32(-2 ** 31))
    c_lo0 = jnp.where(c0 >= kf, c0, -1.0)

    def bis_body(i, st):
        lo, c_lo = st
        trial = lo | lax.shift_left(jnp.int32(1), 30 - i)
        thr = thr_of(trial)
        c = count(lambda x, s: x >= thr)
        ok = c >= kf
        return jnp.where(ok, trial, lo), jnp.where(ok, c, c_lo)

    lo, c_lo = lax.fori_loop(0, 31, bis_body, (lo0, c_lo0))
    kth = jnp.where(lo < jnp.int32(NEG_INF_KEY), -jnp.inf, thr_of(lo))

    idx_bits = (seq - 1).bit_length()

    def tie_search():
        need = kf - count(lambda x, s: x > kth)

        def tie_body(i, last):
            trial = last | lax.shift_left(jnp.int32(1), idx_bits - 1 - i)
            c = count(lambda x, s: (x == kth) & (s < trial))
            return jnp.where(c < need, trial, last)

        return lax.fori_loop(0, idx_bits, tie_body, jnp.zeros((1, A_QUERIES), I32))

    def take_all_ties():
        return jnp.full((1, A_QUERIES), (1 << idx_bits) - 1, I32)

    n_ge = jnp.where(c_lo < 0.0, jnp.inf, c_lo)
    last = lax.cond(jnp.max(n_ge) > kf, tie_search, take_all_ties)

    acc_ref[...] = jnp.zeros_like(acc_ref)

    def att_body(kb, carry):
        m_prev, l_prev = carry
        r0 = pl.multiple_of(kb * KEY_BLOCK, KEY_BLOCK)
        x = isc_ref[pl.ds(r0, KEY_BLOCK), :]
        s_idx = r0 + block_iota
        sel = (s_idx < limit) & ((x > kth) | ((x == kth) & (s_idx <= last)))
        bias_ref[...] = jnp.where(sel, 0.0, NEG_BIG)
        k_tile = kv_ref[pl.ds(r0, KEY_BLOCK), 0:KV_WIDTH]
        v_t = vt_ref[kb]

        units = list(range(A_HEADS // ATT_HEADS))
        uw = ATT_HEADS * A_QUERIES
        us = [slice(u * uw, (u + 1) * uw) for u in units]
        ds = [slice((u * ATT_HEADS // group) * A_HEAD_DIM, (u * ATT_HEADS // group + 1) * A_HEAD_DIM)
              for u in units]
        cs = [slice((u * ATT_HEADS % group) * A_QUERIES, (u * ATT_HEADS % group + ATT_HEADS) * A_QUERIES)
              for u in units]
        m_new, l_new = {}, {}
        for w0 in range(0, len(units), ATT_WAVE):
            wave = units[w0:w0 + ATT_WAVE]
            s = {u: _dot(k_tile[:, ds[u]], qat_ref[:, us[u]])
                 + jnp.concatenate([bias_ref[...]] * ATT_HEADS, axis=1) for u in wave}
            for u in wave:
                m_new[u] = jnp.maximum(m_prev[:, us[u]], jnp.max(s[u], axis=0, keepdims=True))
            alpha = {u: jnp.exp2(m_prev[:, us[u]] - m_new[u]) for u in wave}
            p = {u: jnp.exp2(s[u] - m_new[u]) for u in wave}
            pv = {u: _dot(v_t[ds[u], :], p[u].astype(BF16)) for u in wave}
            for u in wave:
                l_new[u] = alpha[u] * l_prev[:, us[u]] + jnp.sum(p[u], axis=0, keepdims=True)
            for u in wave:
                acc_ref[ds[u], cs[u]] = alpha[u] * acc_ref[ds[u], cs[u]] + pv[u]
        return (jnp.concatenate([m_new[u] for u in units], axis=1),
                jnp.concatenate([l_new[u] for u in units], axis=1))

    m0 = jnp.full((1, A_HEADS * A_QUERIES), NEG_BIG, F32)
    l0 = jnp.zeros((1, A_HEADS * A_QUERIES), F32)
    _, l_fin = lax.fori_loop(0, nkb, att_body, (m0, l0))

    for p in range(n_heads_pair):
        g = (2 * p) // group
        ds = slice(g * A_HEAD_DIM, (g + 1) * A_HEAD_DIM)
        parts = []
        for h in (2 * p, 2 * p + 1):
            hl = h % group
            parts.append(acc_ref[ds, hl * A_QUERIES:(hl + 1) * A_QUERIES]
                         / l_fin[:, h * A_QUERIES:(h + 1) * A_QUERIES])
        o_ref[:, p * LANES:(p + 1) * LANES] = jnp.concatenate(parts, axis=0).T.astype(BF16)


def _mixer_a(qa, qi, kv, sm, bsz, seq):
    nq = seq // A_QUERIES
    assert seq % (COUNT_GROUP * LANES) == 0, seq
    topk = min(TOPK_MAX, seq // 4)
    qrow = lambda b, j: (b * nq + j, 0)
    brow = lambda b, j: (b, 0)
    return pl.pallas_call(
        functools.partial(_mixer_a_body, seq=seq, topk=topk),
        grid=(bsz, nq),
        in_specs=[
            pl.BlockSpec((A_QUERIES, A_WIDTH), qrow),
            pl.BlockSpec((A_QUERIES, A_WIDTH), qrow),
            pl.BlockSpec((seq, 2 * KV_WIDTH), brow),
            pl.BlockSpec((seq, LANES), brow),
        ],
        out_specs=pl.BlockSpec((A_QUERIES, A_WIDTH), qrow),
        out_shape=jax.ShapeDtypeStruct((bsz * seq, A_WIDTH), BF16),
        scratch_shapes=[
            pltpu.VMEM((seq, A_QUERIES), F32),
            pltpu.VMEM((seq // KEY_BLOCK, KV_WIDTH, KEY_BLOCK), BF16),
            pltpu.VMEM((IDX_DIM, IDX_HEADS * A_QUERIES), BF16),
            pltpu.VMEM((A_HEAD_DIM, A_HEADS * A_QUERIES), BF16),
            pltpu.VMEM((KV_WIDTH, (A_HEADS // A_KV_HEADS) * A_QUERIES), F32),
            pltpu.VMEM((KEY_BLOCK, A_QUERIES), F32),
        ],
        compiler_params=pltpu.CompilerParams(
            dimension_semantics=("arbitrary", "arbitrary"), vmem_limit_bytes=VMEM_LIMIT),
        name="mixer_a",
    )(qa, qi, kv, sm)


def _gdn_body(gdn_ref, sm_ref, alog_ref, dtb_ref, ng_ref, o_ref,
              gate_ref, gct_ref, state_ref, oc_ref, u_ref, wq_ref, ak_ref, *, tc):
    tb = pl.program_id(1)
    n_chunks = tc // CHUNK
    prep_unroll = 8
    q_col, k_col, v_col, z_col = (i * B_WIDTH for i in range(4))

    @pl.when(tb == 0)
    def _():
        state_ref[...] = jnp.zeros_like(state_ref)

    sm = sm_ref[...]
    beta = jax.nn.sigmoid(sm)
    z = sm + dtb_ref[...]
    softplus = jnp.maximum(z, 0.0) + jnp.log(1.0 + jnp.exp(-jnp.abs(z)))
    g = -jnp.exp(alog_ref[...]) * softplus
    rin = lax.broadcasted_iota(I32, (tc, LANES), 0) & (CHUNK - 1)
    gc = g
    step = 1
    while step < CHUNK:
        gc = gc + jnp.where(rin >= step, pltpu.roll(gc, step, 0), 0.0)
        step *= 2
    gc3 = gc.reshape(n_chunks, CHUNK, LANES)
    g_last = jnp.broadcast_to(gc3[:, CHUNK - 1:CHUNK, :], gc3.shape).reshape(tc, LANES)
    gate_ref[0] = beta
    gate_ref[1] = gc
    gate_ref[2] = jnp.exp(gc)
    gate_ref[3] = jnp.exp(g_last - gc)
    gate_ref[4] = jnp.exp(g_last)
    for i in range(tc // LANES):
        t = gc[i * LANES:(i + 1) * LANES, :].T
        for half in range(LANES // CHUNK):
            gct_ref[i * (LANES // CHUNK) + half] = t[:, half * CHUNK:(half + 1) * CHUNK]

    ci = lax.broadcasted_iota(I32, (CHUNK, CHUNK), 0)
    si = lax.broadcasted_iota(I32, (CHUNK, CHUNK), 1)
    wl = lax.broadcasted_iota(I32, (CHUNK, 2 * CHUNK), 1)
    wr = lax.broadcasted_iota(I32, (CHUNK, 2 * CHUNK), 0)
    right = wl >= CHUNK
    eye_right = jnp.where(wl == wr + CHUNK, 1.0, 0.0)

    def prep_body(cg, carry):
        units = [(cg * prep_unroll + cc, h) for cc in range(prep_unroll) for h in range(B_HEADS)]
        rows = [pl.ds(pl.multiple_of(c * CHUNK, CHUNK), CHUNK) for c, _ in units]
        hsl = [slice(h * LANES, (h + 1) * LANES) for _, h in units]
        idx = range(len(units))

        def col(i, gate, off):
            h = units[i][1]
            return gate_ref[gate, rows[i], off + h:off + h + 1]

        def seg(i, col0):
            h = units[i][1]
            return gdn_ref[rows[i], col0 + h * LANES:col0 + (h + 1) * LANES]

        q = [seg(i, q_col) for i in idx]
        k = [seg(i, k_col) for i in idx]
        kb = [k[i] * col(i, 0, SM_BETA) for i in idx]
        kq = [_dot_nt(jnp.concatenate([kb[i], q[i]], axis=0).astype(BF16), k[i].astype(BF16))
              for i in idx]
        decay = []
        for i, (c, h) in enumerate(units):
            d = col(i, 1, SM_DECAY) - gct_ref[c][SM_DECAY + h:SM_DECAY + h + 1, :]
            decay.append(jnp.where(ci >= si, jnp.exp(jnp.where(ci >= si, d, 0.0)), 0.0))
        wmat = []
        for i, (c, h) in enumerate(units):
            n_mat = jnp.where(ci > si, -(kq[i][0:CHUNK] * decay[i]), 0.0)
            wmat.append(jnp.concatenate([n_mat, jnp.zeros_like(n_mat)], axis=1) + eye_right)
            ak_ref[c, h, 0:CHUNK, :] = (kq[i][CHUNK:] * decay[i]).astype(BF16)
        pw = 1
        while pw < CHUNK:
            wb = [wmat[i].astype(BF16) for i in idx]
            wmat = [_dot(wb[i][:, 0:CHUNK], wb[i]) + jnp.where(right, wmat[i], 0.0) for i in idx]
            pw *= 2
        eg = [col(i, 2, SM_DECAY) for i in idx]
        rhs = [jnp.concatenate([seg(i, v_col) * col(i, 0, SM_BETA), kb[i] * eg[i]],
                               axis=1).astype(BF16) for i in idx]
        sol = [_dot(wmat[i][:, CHUNK:].astype(BF16), rhs[i]) for i in idx]
        for i, (c, h) in enumerate(units):
            u_ref[rows[i], hsl[i]] = sol[i][:, 0:B_HEAD_DIM]
            wq_ref[c, h] = jnp.concatenate([sol[i][:, B_HEAD_DIM:], q[i] * eg[i]],
                                           axis=0).astype(BF16)
            ak_ref[c, h, CHUNK:, :] = (k[i] * col(i, 3, SM_DECAY)).T.astype(BF16)
        return carry

    lax.fori_loop(0, n_chunks // prep_unroll, prep_body, 0)

    def scan_body(c, carry):
        r0 = pl.multiple_of(c * CHUNK, CHUNK)
        rows = pl.ds(r0, CHUNK)
        heads = range(B_HEADS)
        hsl = [slice(h * LANES, (h + 1) * LANES) for h in heads]
        s_prev = [state_ref[h] for h in heads]
        ws = [_dot(wq_ref[c, h], s_prev[h].astype(BF16)) for h in heads]
        v_new = [(u_ref[rows, hsl[h]] - ws[h][0:CHUNK]).astype(BF16) for h in heads]
        r = [_dot(ak_ref[c, h], v_new[h]) for h in heads]
        for h in heads:
            gl = gate_ref[4, rows, SM_DECAY + h:SM_DECAY + h + 1][0:1, :]
            oc_ref[rows, hsl[h]] = ws[h][CHUNK:] + r[h][0:CHUNK]
            state_ref[h] = s_prev[h] * gl + r[h][CHUNK:]
        return carry

    lax.fori_loop(0, n_chunks, scan_body, 0)

    for h in range(B_HEADS):
        hs = slice(h * LANES, (h + 1) * LANES)
        z_gate = gdn_ref[:, z_col + h * LANES:z_col + (h + 1) * LANES]
        o_ref[:, hs] = (_rms(oc_ref[:, hs], ng_ref[...]) * z_gate).astype(BF16)


def _gdn(gdn_in, sm, alog_row, dtb_row, norm_g, bsz, seq, tc):
    nt = seq // tc
    trow = lambda b, t: (b * nt + t, 0)
    const = lambda b, t: (0, 0)
    n_chunks = tc // CHUNK
    return pl.pallas_call(
        functools.partial(_gdn_body, tc=tc),
        grid=(bsz, nt),
        in_specs=[
            pl.BlockSpec((tc, GDN_WIDTH), trow),
            pl.BlockSpec((tc, LANES), trow),
            pl.BlockSpec((1, LANES), const),
            pl.BlockSpec((1, LANES), const),
            pl.BlockSpec((1, B_HEAD_DIM), const),
        ],
        out_specs=pl.BlockSpec((tc, B_WIDTH), trow),
        out_shape=jax.ShapeDtypeStruct((bsz * seq, B_WIDTH), BF16),
        scratch_shapes=[
            pltpu.VMEM((5, tc, LANES), F32),
            pltpu.VMEM((n_chunks, LANES, CHUNK), F32),
            pltpu.VMEM((B_HEADS, B_HEAD_DIM, B_HEAD_DIM), F32),
            pltpu.VMEM((tc, B_WIDTH), F32),
            pltpu.VMEM((tc, B_WIDTH), F32),
            pltpu.VMEM((n_chunks, B_HEADS, 2 * CHUNK, B_HEAD_DIM), BF16),
            pltpu.VMEM((n_chunks, B_HEADS, CHUNK + B_HEAD_DIM, CHUNK), BF16),
        ],
        compiler_params=pltpu.CompilerParams(
            dimension_semantics=("arbitrary", "arbitrary"), vmem_limit_bytes=VMEM_LIMIT),
        name="gdn",
    )(gdn_in, sm, alog_row, dtb_row, norm_g)


def _ffn_body(x_ref, oa_ref, ob_ref, wo_ref, g2_ref, w1_ref, w2_ref, g3_ref, out_ref, *,
              final_norm):
    y = (x_ref[...] + _dot(oa_ref[...], wo_ref[0:A_WIDTH, :])
         + _dot(ob_ref[...], wo_ref[A_WIDTH:, :]))
    h = _rms(y, g2_ref[...]).astype(BF16)
    a = jnp.square(jnp.maximum(_dot(h, w1_ref[...]), 0.0)).astype(BF16)
    acc = y + _dot(a, w2_ref[...])
    out_ref[...] = _rms(acc, g3_ref[...]) if final_norm else acc


def _ffn(x2, oa, ob, wo, g2, w1, w2, g3, tm, final_norm):
    m = x2.shape[0]
    row = lambda i: (i, 0)
    const = lambda i: (0, 0)
    resident = functools.partial(pl.BlockSpec, index_map=const, pipeline_mode=pl.Buffered(1))
    return pl.pallas_call(
        functools.partial(_ffn_body, final_norm=final_norm),
        grid=(m // tm,),
        in_specs=[
            pl.BlockSpec((tm, D_MODEL), row),
            pl.BlockSpec((tm, A_WIDTH), row),
            pl.BlockSpec((tm, B_WIDTH), row),
            resident((D_MODEL, D_MODEL)),
            pl.BlockSpec((1, D_MODEL), const),
            resident((D_MODEL, D_FF)),
            resident((D_FF, D_MODEL)),
            pl.BlockSpec((1, D_MODEL), const),
        ],
        out_specs=pl.BlockSpec((tm, D_MODEL), row),
        out_shape=jax.ShapeDtypeStruct((m, D_MODEL), F32),
        compiler_params=pltpu.CompilerParams(
            dimension_semantics=("arbitrary",), vmem_limit_bytes=VMEM_LIMIT),
        name="ffn",
    )(x2, oa, ob, wo, g2, w1, w2, g3)


def _rope_tables(seq):
    half = A_HEAD_DIM // 2
    inv_freq = 1.0 / (ROPE_THETA ** (jnp.arange(half, dtype=F32) / half))
    ang = jnp.arange(seq).astype(F32)[:, None] * inv_freq[None, :]
    cos = jnp.cos(ang)
    sin = jnp.sin(ang)
    reps = LANES // A_HEAD_DIM
    return (jnp.tile(cos, (1, 2 * reps)),
            jnp.tile(jnp.concatenate([-sin, sin], axis=1), (1, reps)))


def _permute_in_weight(w):
    sizes = (A_WIDTH, KV_WIDTH, KV_WIDTH, IDX_HEADS * IDX_DIM, IDX_DIM, IDX_HEADS,
             B_WIDTH, B_WIDTH, B_WIDTH, B_WIDTH, B_HEADS, B_HEADS)
    parts, off = [], 0
    for s in sizes:
        parts.append(w[:, off:off + s])
        off += s
    qa, ka, va, qi, ki, wi, qb, kb, vb, zb, bb, ab = parts
    pad = jnp.zeros((w.shape[0], LANES - (IDX_DIM + IDX_HEADS + 2 * B_HEADS)), w.dtype)
    return jnp.concatenate([qa, qi, ka, va, qb, kb, vb, zb, ki, wi, bb, ab, pad],
                           axis=1).astype(BF16)


def _lane_row(vals, offset):
    return jnp.zeros((1, LANES), F32).at[0, offset:offset + vals.shape[0]].set(vals.astype(F32))


def kernel(x, norm_mix_g, w_in, conv_w, a_log, dt_bias, gdn_norm_g, w_out,
           norm_ffn_g, w_ff1, w_ff2, norm_final_g):
    bsz, seq, d = x.shape
    depth = w_in.shape[0]
    m = bsz * seq
    cos_t, sin_t = _rope_tables(seq)
    x2 = x.reshape(m, d)
    for l in range(depth):
        qa, qi, kv, gdn_in, sm = _inproj(
            x2, norm_mix_g[l][None, :], _permute_in_weight(w_in[l]), cos_t, sin_t, conv_w[l],
            seq, tm=512)
        o_a = _mixer_a(qa, qi, kv, sm, bsz, seq)
        o_b = _gdn(gdn_in, sm, _lane_row(a_log[l], SM_DECAY),
                   _lane_row(dt_bias[l], SM_DECAY), gdn_norm_g[l][None, :], bsz, seq, tc=512)
        x2 = _ffn(x2, o_a, o_b, w_out[l].astype(BF16), norm_ffn_g[l][None, :],
                  w_ff1[l].astype(BF16), w_ff2[l].astype(BF16), norm_final_g[None, :],
                  tm=512, final_norm=(l == depth - 1))
    return x2.reshape(bsz, seq, d)
```

```python
import functools

import jax
import jax.numpy as jnp
from jax import lax
from jax.experimental import pallas as pl
from jax.experimental.pallas import tpu as pltpu

F32 = jnp.float32
BF16 = jnp.bfloat16
I32 = jnp.int32

D_MODEL = 1024
CHUNK = 64
A_QUERIES = 256
ROPE_THETA = 10000.0
EPS = 1e-6
A_HEADS = 8
A_KV_HEADS = 2
A_HEAD_DIM = 64
IDX_HEADS = 8
IDX_DIM = 64
TOPK_MAX = 256
B_HEADS = 4
B_HEAD_DIM = 128
CONV_WIDTH = 4
D_FF = 4 * D_MODEL

LANES = 128
A_WIDTH = A_HEADS * A_HEAD_DIM
KV_WIDTH = A_KV_HEADS * A_HEAD_DIM
B_WIDTH = B_HEADS * B_HEAD_DIM
GDN_WIDTH = 4 * B_WIDTH
SM_KI = 0
SM_WI = IDX_DIM
SM_BETA = SM_WI + IDX_HEADS
SM_DECAY = SM_BETA + B_HEADS
C_QA = 0
C_QI = C_QA + A_WIDTH
C_KV = C_QI + A_WIDTH
C_GDN = C_KV + 2 * KV_WIDTH
C_SM = C_GDN + GDN_WIDTH
IN_COLS = C_SM + LANES

VMEM_LIMIT = 56 * 1024 * 1024
CONV_PAD = 8
COUNT_GROUP = 4
COUNT_ROWS = 32
ATT_HEADS = 2
ATT_WAVE = 4
KEY_BLOCK = 2 * LANES
NEG_BIG = -1e30
LOG2_E = 1.4426950408889634
NEG_INF_KEY = -(2 ** 31) + 0x7FFFFF


def _rms(x, g):
    return x * lax.rsqrt(jnp.mean(x * x, axis=-1, keepdims=True) + EPS) * g


def _dot(a, b):
    return jnp.dot(a, b, preferred_element_type=F32)


def _dot_nt(a, b):
    return lax.dot_general(a, b, (((1,), (1,)), ((), ())), preferred_element_type=F32)


def _inproj_body(x_ref, g_ref, w_ref, cos_ref, sin_ref, cw_ref,
                 qa_ref, qi_ref, kv_ref, gdn_ref, sm_ref, xpad_ref, *, tiles_per_seq):
    h = _rms(x_ref[...], g_ref[...]).astype(BF16)
    cos = cos_ref[...]
    sin = sin_ref[...]
    lane = lax.broadcasted_iota(I32, cos.shape, 1)
    first_half = (lane & (A_HEAD_DIM - 1)) < A_HEAD_DIM // 2

    def rope(t):
        swapped = jnp.where(first_half, pltpu.roll(t, LANES - A_HEAD_DIM // 2, 1),
                            pltpu.roll(t, A_HEAD_DIM // 2, 1))
        return t * cos + swapped * sin

    def proj(c0, width):
        return _dot(h, w_ref[:, c0:c0 + width])

    tm = x_ref.shape[0]
    conv_cols = 3 * B_WIDTH

    @pl.when(pl.program_id(0) % tiles_per_seq == 0)
    def _():
        xpad_ref[0:CONV_PAD, :] = jnp.zeros((CONV_PAD, conv_cols), F32)

    @pl.when(pl.program_id(0) % tiles_per_seq != 0)
    def _():
        xpad_ref[0:CONV_PAD, :] = xpad_ref[tm:tm + CONV_PAD, :]

    def stage_conv_input(seg):
        xpad_ref[CONV_PAD:CONV_PAD + tm, seg * B_WIDTH:(seg + 1) * B_WIDTH] = proj(
            C_GDN + seg * B_WIDTH, B_WIDTH)

    def conv_segment(seg):
        for hh in range(B_HEADS):
            cs = slice(seg * B_WIDTH + hh * LANES, seg * B_WIDTH + (hh + 1) * LANES)
            y = jnp.zeros((tm, LANES), F32)
            for jj in range(CONV_WIDTH):
                r = CONV_PAD - (CONV_WIDTH - 1) + jj
                y = y + cw_ref[jj:jj + 1, cs] * xpad_ref[r:r + tm, cs]
            y = y * jax.nn.sigmoid(y)
            if seg < 2:
                y = y * lax.rsqrt(jnp.sum(y * y, axis=-1, keepdims=True) + EPS)
            if seg == 0:
                y = y * (B_HEAD_DIM ** -0.5)
            gdn_ref[:, cs] = y

    stage_conv_input(0)
    acc = proj(C_QA, A_WIDTH)
    conv_segment(0)
    for j in range(A_WIDTH // LANES):
        sl = slice(j * LANES, (j + 1) * LANES)
        qa_ref[:, sl] = (rope(acc[:, sl]) * (A_HEAD_DIM ** -0.5 * LOG2_E)).astype(BF16)
    stage_conv_input(1)
    acc = proj(C_QI, A_WIDTH)
    conv_segment(1)
    for j in range(A_WIDTH // LANES):
        sl = slice(j * LANES, (j + 1) * LANES)
        qi_ref[:, sl] = rope(acc[:, sl]).astype(BF16)
    stage_conv_input(2)
    acc = proj(C_KV, 2 * KV_WIDTH)
    acc_sm = proj(C_SM, LANES)
    z = proj(C_GDN + conv_cols, B_WIDTH)
    conv_segment(2)
    kv_ref[:, 0:KV_WIDTH] = rope(acc[:, 0:KV_WIDTH]).astype(BF16)
    kv_ref[:, KV_WIDTH:] = acc[:, KV_WIDTH:].astype(BF16)
    sm_ref[...] = jnp.where(lane < IDX_DIM, rope(acc_sm), acc_sm)
    gdn_ref[:, conv_cols:] = z * jax.nn.sigmoid(z)


def _inproj(x2, g, w, cos_t, sin_t, conv_w, seq, tm):
    m = x2.shape[0]
    assert seq % tm == 0, (seq, tm)
    nt = seq // tm
    row = lambda i: (i, 0)
    const = lambda i: (0, 0)
    return pl.pallas_call(
        functools.partial(_inproj_body, tiles_per_seq=nt),
        grid=(m // tm,),
        in_specs=[
            pl.BlockSpec((tm, D_MODEL), row),
            pl.BlockSpec((1, D_MODEL), const),
            pl.BlockSpec((D_MODEL, IN_COLS), const, pipeline_mode=pl.Buffered(1)),
            pl.BlockSpec((tm, LANES), lambda i: (i % nt, 0)),
            pl.BlockSpec((tm, LANES), lambda i: (i % nt, 0)),
            pl.BlockSpec((CONV_WIDTH, 3 * B_WIDTH), const),
        ],
        out_specs=[
            pl.BlockSpec((tm, A_WIDTH), row),
            pl.BlockSpec((tm, A_WIDTH), row),
            pl.BlockSpec((tm, 2 * KV_WIDTH), row),
            pl.BlockSpec((tm, GDN_WIDTH), row),
            pl.BlockSpec((tm, LANES), row),
        ],
        out_shape=[
            jax.ShapeDtypeStruct((m, A_WIDTH), BF16),
            jax.ShapeDtypeStruct((m, A_WIDTH), BF16),
            jax.ShapeDtypeStruct((m, 2 * KV_WIDTH), BF16),
            jax.ShapeDtypeStruct((m, GDN_WIDTH), F32),
            jax.ShapeDtypeStruct((m, LANES), F32),
        ],
        scratch_shapes=[pltpu.VMEM((tm + CONV_PAD, 3 * B_WIDTH), F32)],
        compiler_params=pltpu.CompilerParams(
            dimension_semantics=("arbitrary",), vmem_limit_bytes=VMEM_LIMIT),
        name="inproj",
    )(x2, g, w, cos_t, sin_t, conv_w)


def _mixer_a_body(qa_ref, qi_ref, kv_ref, sm_ref, o_ref,
                  isc_ref, isb_ref, vt_ref, qit_ref, qat_ref, acc_ref, bias_ref, *, seq, topk):
    j = pl.program_id(1)
    nkt = (j + 1) * (A_QUERIES // LANES)
    n_heads_pair = A_WIDTH // LANES
    group = A_HEADS // A_KV_HEADS

    @pl.when(j == 0)
    def _():
        per_block = KEY_BLOCK // LANES
        for t in range(seq // LANES):
            vt = kv_ref[t * LANES:(t + 1) * LANES, KV_WIDTH:].astype(F32)
            c0 = (t % per_block) * LANES
            vt_ref[t // per_block, :, c0:c0 + LANES] = vt.T.astype(BF16)

    for p in range(n_heads_pair):
        sl = slice(p * LANES, (p + 1) * LANES)
        t = qi_ref[:, sl].astype(F32).T
        qit_ref[:, (2 * p) * A_QUERIES:(2 * p + 1) * A_QUERIES] = t[0:IDX_DIM].astype(BF16)
        qit_ref[:, (2 * p + 1) * A_QUERIES:(2 * p + 2) * A_QUERIES] = t[IDX_DIM:].astype(BF16)
        t = qa_ref[:, sl].astype(F32).T
        qat_ref[:, (2 * p) * A_QUERIES:(2 * p + 1) * A_QUERIES] = t[0:A_HEAD_DIM].astype(BF16)
        qat_ref[:, (2 * p + 1) * A_QUERIES:(2 * p + 2) * A_QUERIES] = t[A_HEAD_DIM:].astype(BF16)

    q0 = pl.multiple_of(j * A_QUERIES, A_QUERIES)
    w_t = sm_ref[pl.ds(q0, A_QUERIES), :].T[SM_WI:SM_WI + IDX_HEADS, :]
    w_t = w_t * ((IDX_HEADS ** -0.5) * (IDX_DIM ** -0.5))

    qlane = lax.broadcasted_iota(I32, (1, A_QUERIES), 1)
    limit = q0 + (lax.shift_right_logical(qlane, CHUNK.bit_length() - 1) + 1) * CHUNK
    tile_iota = lax.broadcasted_iota(I32, (LANES, A_QUERIES), 0)
    block_iota = lax.broadcasted_iota(I32, (KEY_BLOCK, A_QUERIES), 0)
    nkb = lax.shift_right_logical(nkt + (KEY_BLOCK // LANES - 1), (KEY_BLOCK // LANES).bit_length() - 1)

    def isc_body(kb, carry):
        r0 = pl.multiple_of(kb * KEY_BLOCK, KEY_BLOCK)
        kid = sm_ref[pl.ds(r0, KEY_BLOCK), SM_KI:SM_KI + IDX_DIM].astype(BF16)
        rel = _dot(kid, qit_ref[...])
        acc = jnp.zeros((KEY_BLOCK, A_QUERIES), F32)
        for h in range(IDX_HEADS):
            acc = acc + w_t[h:h + 1, :] * jnp.maximum(rel[:, h * A_QUERIES:(h + 1) * A_QUERIES], 0.0)
        acc = jnp.where(r0 + block_iota < limit, acc, -jnp.inf)
        isc_ref[pl.ds(r0, KEY_BLOCK), :] = acc
        isb_ref[pl.ds(r0, KEY_BLOCK), :] = acc.astype(BF16)
        return carry

    lax.fori_loop(0, nkb, isc_body, 0)

    ngrp = lax.shift_right_logical(nkt + (COUNT_GROUP - 1), COUNT_GROUP.bit_length() - 1)

    def fill_body(kt, carry):
        r0 = pl.multiple_of(kt * LANES, LANES)
        isc_ref[pl.ds(r0, LANES), :] = jnp.full((LANES, A_QUERIES), -jnp.inf, F32)
        isb_ref[pl.ds(r0, LANES), :] = jnp.full((LANES, A_QUERIES), -jnp.inf, BF16)
        return carry

    lax.fori_loop(nkb * (KEY_BLOCK // LANES), ngrp * COUNT_GROUP, fill_body, 0)

    def count(pred):
        def body(g, acc):
            for t in range(COUNT_GROUP):
                r0 = pl.multiple_of((g * COUNT_GROUP + t) * LANES, LANES)
                hit = jnp.where(pred(isc_ref[pl.ds(r0, LANES), :], r0 + tile_iota), 1.0, 0.0)
                acc = acc + hit.reshape(LANES // COUNT_ROWS, COUNT_ROWS, A_QUERIES).sum(axis=0)
            return acc
        acc = lax.fori_loop(0, ngrp, body, jnp.zeros((COUNT_ROWS, A_QUERIES), F32))
        return acc.sum(axis=0, keepdims=True)

    def count_coarse(thr):
        def body(g, acc):
            for t in range(COUNT_GROUP):
                r0 = pl.multiple_of((g * COUNT_GROUP + t) * LANES, LANES)
                hit = jnp.where(isb_ref[pl.ds(r0, LANES), :] >= thr, one_b, zero_b)
                hit = hit.reshape(LANES // COUNT_ROWS, COUNT_ROWS, A_QUERIES)
                part = hit[0]
                for i in range(1, LANES // COUNT_ROWS):
                    part = part + hit[i]
                acc = acc + part
            return acc
        acc = lax.fori_loop(0, ngrp, body, jnp.zeros((COUNT_ROWS, A_QUERIES), BF16))
        return acc.astype(F32).sum(axis=0, keepdims=True)

    kf = float(topk)
    one_b = jnp.ones((), BF16)
    zero_b = jnp.zeros((), BF16)

    def thr_coarse(key):
        bits = jnp.where(key >= 0, key, key ^ jnp.int32(0x7FFF))
        return lax.bitcast_convert_type(lax.shift_left(bits, 16), F32).astype(BF16)

    c0 = count_coarse(jnp.zeros((1, A_QUERIES), BF16))
    lo16 = jnp.where(c0 >= kf, jnp.int32(0), jnp.int32(-2 ** 15))

    def coarse_body(i, lo):
        trial = lo | lax.shift_left(jnp.int32(1), 14 - i)
        c = count_coarse(thr_coarse(trial))
        return jnp.where(c >= kf, trial, lo)

    lo16 = lax.fori_loop(0, 15, coarse_body, lo16)
    lo16 = jnp.maximum(lo16, jnp.int32(NEG_INF_KEY >> 16))
    center = lax.shift_left(lo16, 16) | jnp.where(lo16 < 0, jnp.int32(0xFFFF), jnp.int32(0))

    def thr_of(key):
        bits = jnp.where(key >= 0, key, key ^ jnp.int32(0x7FFFFFFF))
        return jnp.where(key < jnp.int32(NEG_INF_KEY), -jnp.inf, lax.bitcast_convert_type(bits, F32))

    def fine_body(i, st):
        lo, c_lo = st
        trial = lo + lax.shift_left(jnp.int32(1), 16 - i)
        thr = thr_of(trial)
        c = count(lambda x, s: x >= thr)
        ok = c >= kf
        return jnp.where(ok, trial, lo), jnp.where(ok, c, c_lo)

    lo, c_lo = lax.fori_loop(0, 17, fine_body,
                             (center - jnp.int32(1 << 16), jnp.full((1, A_QUERIES), -1.0, F32)))
    kth = thr_of(lo)

    idx_bits = (seq - 1).bit_length()

    def tie_search():
        need = kf - count(lambda x, s: x > kth)

        def tie_body(i, last):
            trial = last | lax.shift_left(jnp.int32(1), idx_bits - 1 - i)
            c = count(lambda x, s: (x == kth) & (s < trial))
            return jnp.where(c < need, trial, last)

        return lax.fori_loop(0, idx_bits, tie_body, jnp.zeros((1, A_QUERIES), I32))

    def take_all_ties():
        return jnp.full((1, A_QUERIES), (1 << idx_bits) - 1, I32)

    n_ge = jnp.where(c_lo < 0.0, jnp.inf, c_lo)
    last = lax.cond(jnp.max(n_ge) > kf, tie_search, take_all_ties)

    acc_ref[...] = jnp.zeros_like(acc_ref)

    def att_body(kb, carry):
        m_prev, l_prev = carry
        r0 = pl.multiple_of(kb * KEY_BLOCK, KEY_BLOCK)
        x = isc_ref[pl.ds(r0, KEY_BLOCK), :]
        s_idx = r0 + block_iota
        sel = (s_idx < limit) & ((x > kth) | ((x == kth) & (s_idx <= last)))
        bias_ref[...] = jnp.where(sel, 0.0, NEG_BIG)
        k_tile = kv_ref[pl.ds(r0, KEY_BLOCK), 0:KV_WIDTH]
        v_t = vt_ref[kb]

        units = list(range(A_HEADS // ATT_HEADS))
        uw = ATT_HEADS * A_QUERIES
        us = [slice(u * uw, (u + 1) * uw) for u in units]
        ds = [slice((u * ATT_HEADS // group) * A_HEAD_DIM, (u * ATT_HEADS // group + 1) * A_HEAD_DIM)
              for u in units]
        cs = [slice((u * ATT_HEADS % group) * A_QUERIES, (u * ATT_HEADS % group + ATT_HEADS) * A_QUERIES)
              for u in units]
        m_new, l_new = {}, {}
        for w0 in range(0, len(units), ATT_WAVE):
            wave = units[w0:w0 + ATT_WAVE]
            s = {u: _dot(k_tile[:, ds[u]], qat_ref[:, us[u]])
                 + jnp.concatenate([bias_ref[...]] * ATT_HEADS, axis=1) for u in wave}
            for u in wave:
                m_new[u] = jnp.maximum(m_prev[:, us[u]], jnp.max(s[u], axis=0, keepdims=True))
            alpha = {u: jnp.exp2(m_prev[:, us[u]] - m_new[u]) for u in wave}
            p = {u: jnp.exp2(s[u] - m_new[u]) for u in wave}
            pv = {u: _dot(v_t[ds[u], :], p[u].astype(BF16)) for u in wave}
            for u in wave:
                l_new[u] = alpha[u] * l_prev[:, us[u]] + jnp.sum(p[u], axis=0, keepdims=True)
            for u in wave:
                acc_ref[ds[u], cs[u]] = alpha[u] * acc_ref[ds[u], cs[u]] + pv[u]
        return (jnp.concatenate([m_new[u] for u in units], axis=1),
                jnp.concatenate([l_new[u] for u in units], axis=1))

    m0 = jnp.full((1, A_HEADS * A_QUERIES), NEG_BIG, F32)
    l0 = jnp.zeros((1, A_HEADS * A_QUERIES), F32)
    _, l_fin = lax.fori_loop(0, nkb, att_body, (m0, l0))

    for p in range(n_heads_pair):
        g = (2 * p) // group
        ds = slice(g * A_HEAD_DIM, (g + 1) * A_HEAD_DIM)
        parts = []
        for h in (2 * p, 2 * p + 1):
            hl = h % group
            parts.append(acc_ref[ds, hl * A_QUERIES:(hl + 1) * A_QUERIES]
                         / l_fin[:, h * A_QUERIES:(h + 1) * A_QUERIES])
        o_ref[:, p * LANES:(p + 1) * LANES] = jnp.concatenate(parts, axis=0).T.astype(BF16)


def _mixer_a(qa, qi, kv, sm, bsz, seq):
    nq = seq // A_QUERIES
    assert seq % (COUNT_GROUP * LANES) == 0, seq
    assert seq // COUNT_ROWS <= 256, seq
    topk = min(TOPK_MAX, seq // 4)
    qrow = lambda b, j: (b * nq + j, 0)
    brow = lambda b, j: (b, 0)
    return pl.pallas_call(
        functools.partial(_mixer_a_body, seq=seq, topk=topk),
        grid=(bsz, nq),
        in_specs=[
            pl.BlockSpec((A_QUERIES, A_WIDTH), qrow),
            pl.BlockSpec((A_QUERIES, A_WIDTH), qrow),
            pl.BlockSpec((seq, 2 * KV_WIDTH), brow),
            pl.BlockSpec((seq, LANES), brow),
        ],
        out_specs=pl.BlockSpec((A_QUERIES, A_WIDTH), qrow),
        out_shape=jax.ShapeDtypeStruct((bsz * seq, A_WIDTH), BF16),
        scratch_shapes=[
            pltpu.VMEM((seq, A_QUERIES), F32),
            pltpu.VMEM((seq, A_QUERIES), BF16),
            pltpu.VMEM((seq // KEY_BLOCK, KV_WIDTH, KEY_BLOCK), BF16),
            pltpu.VMEM((IDX_DIM, IDX_HEADS * A_QUERIES), BF16),
            pltpu.VMEM((A_HEAD_DIM, A_HEADS * A_QUERIES), BF16),
            pltpu.VMEM((KV_WIDTH, (A_HEADS // A_KV_HEADS) * A_QUERIES), F32),
            pltpu.VMEM((KEY_BLOCK, A_QUERIES), F32),
        ],
        compiler_params=pltpu.CompilerParams(
            dimension_semantics=("arbitrary", "arbitrary"), vmem_limit_bytes=VMEM_LIMIT),
        name="mixer_a",
    )(qa, qi, kv, sm)


def _gdn_body(gdn_ref, sm_ref, alog_ref, dtb_ref, ng_ref, o_ref,
              gate_ref, gct_ref, state_ref, oc_ref, u_ref, wq_ref, ak_ref, *, tc):
    tb = pl.program_id(1)
    n_chunks = tc // CHUNK
    prep_unroll = 8
    q_col, k_col, v_col, z_col = (i * B_WIDTH for i in range(4))

    @pl.when(tb == 0)
    def _():
        state_ref[...] = jnp.zeros_like(state_ref)

    sm = sm_ref[...]
    beta = jax.nn.sigmoid(sm)
    z = sm + dtb_ref[...]
    softplus = jnp.maximum(z, 0.0) + jnp.log(1.0 + jnp.exp(-jnp.abs(z)))
    g = -jnp.exp(alog_ref[...]) * softplus
    rin = lax.broadcasted_iota(I32, (tc, LANES), 0) & (CHUNK - 1)
    gc = g
    step = 1
    while step < CHUNK:
        gc = gc + jnp.where(rin >= step, pltpu.roll(gc, step, 0), 0.0)
        step *= 2
    gc3 = gc.reshape(n_chunks, CHUNK, LANES)
    g_last = jnp.broadcast_to(gc3[:, CHUNK - 1:CHUNK, :], gc3.shape).reshape(tc, LANES)
    gate_ref[0] = beta
    gate_ref[1] = gc
    gate_ref[2] = jnp.exp(gc)
    gate_ref[3] = jnp.exp(g_last - gc)
    gate_ref[4] = jnp.exp(g_last)
    for i in range(tc // LANES):
        t = gc[i * LANES:(i + 1) * LANES, :].T
        for half in range(LANES // CHUNK):
            gct_ref[i * (LANES // CHUNK) + half] = t[:, half * CHUNK:(half + 1) * CHUNK]

    ci = lax.broadcasted_iota(I32, (CHUNK, CHUNK), 0)
    si = lax.broadcasted_iota(I32, (CHUNK, CHUNK), 1)
    wl = lax.broadcasted_iota(I32, (CHUNK, 2 * CHUNK), 1)
    wr = lax.broadcasted_iota(I32, (CHUNK, 2 * CHUNK), 0)
    right = wl >= CHUNK
    eye_right = jnp.where(wl == wr + CHUNK, 1.0, 0.0)

    def prep_body(cg, carry):
        units = [(cg * prep_unroll + cc, h) for cc in range(prep_unroll) for h in range(B_HEADS)]
        rows = [pl.ds(pl.multiple_of(c * CHUNK, CHUNK), CHUNK) for c, _ in units]
        hsl = [slice(h * LANES, (h + 1) * LANES) for _, h in units]
        idx = range(len(units))

        def col(i, gate, off):
            h = units[i][1]
            return gate_ref[gate, rows[i], off + h:off + h + 1]

        def seg(i, col0):
            h = units[i][1]
            return gdn_ref[rows[i], col0 + h * LANES:col0 + (h + 1) * LANES]

        q = [seg(i, q_col) for i in idx]
        k = [seg(i, k_col) for i in idx]
        kb = [k[i] * col(i, 0, SM_BETA) for i in idx]
        kq = [_dot_nt(jnp.concatenate([kb[i], q[i]], axis=0).astype(BF16), k[i].astype(BF16))
              for i in idx]
        decay = []
        for i, (c, h) in enumerate(units):
            d = col(i, 1, SM_DECAY) - gct_ref[c][SM_DECAY + h:SM_DECAY + h + 1, :]
            decay.append(jnp.where(ci >= si, jnp.exp(jnp.where(ci >= si, d, 0.0)), 0.0))
        wmat = []
        for i, (c, h) in enumerate(units):
            n_mat = jnp.where(ci > si, -(kq[i][0:CHUNK] * decay[i]), 0.0)
            wmat.append(jnp.concatenate([n_mat, jnp.zeros_like(n_mat)], axis=1) + eye_right)
            ak_ref[c, h, 0:CHUNK, :] = (kq[i][CHUNK:] * decay[i]).astype(BF16)
        pw = 1
        while pw < CHUNK:
            wb = [wmat[i].astype(BF16) for i in idx]
            wmat = [_dot(wb[i][:, 0:CHUNK], wb[i]) + jnp.where(right, wmat[i], 0.0) for i in idx]
            pw *= 2
        eg = [col(i, 2, SM_DECAY) for i in idx]
        rhs = [jnp.concatenate([seg(i, v_col) * col(i, 0, SM_BETA), kb[i] * eg[i]],
                               axis=1).astype(BF16) for i in idx]
        sol = [_dot(wmat[i][:, CHUNK:].astype(BF16), rhs[i]) for i in idx]
        for i, (c, h) in enumerate(units):
            u_ref[rows[i], hsl[i]] = sol[i][:, 0:B_HEAD_DIM]
            wq_ref[c, h] = jnp.concatenate([sol[i][:, B_HEAD_DIM:], q[i] * eg[i]],
                                           axis=0).astype(BF16)
            ak_ref[c, h, CHUNK:, :] = (k[i] * col(i, 3, SM_DECAY)).T.astype(BF16)
        return carry

    lax.fori_loop(0, n_chunks // prep_unroll, prep_body, 0)

    def scan_body(c, carry):
        r0 = pl.multiple_of(c * CHUNK, CHUNK)
        rows = pl.ds(r0, CHUNK)
        heads = range(B_HEADS)
        hsl = [slice(h * LANES, (h + 1) * LANES) for h in heads]
        s_prev = [state_ref[h] for h in heads]
        ws = [_dot(wq_ref[c, h], s_prev[h].astype(BF16)) for h in heads]
        v_new = [(u_ref[rows, hsl[h]] - ws[h][0:CHUNK]).astype(BF16) for h in heads]
        r = [_dot(ak_ref[c, h], v_new[h]) for h in heads]
        for h in heads:
            gl = gate_ref[4, rows, SM_DECAY + h:SM_DECAY + h + 1][0:1, :]
            oc_ref[rows, hsl[h]] = ws[h][CHUNK:] + r[h][0:CHUNK]
            state_ref[h] = s_prev[h] * gl + r[h][CHUNK:]
        return carry

    lax.fori_loop(0, n_chunks, scan_body, 0)

    for h in range(B_HEADS):
        hs = slice(h * LANES, (h + 1) * LANES)
        z_gate = gdn_ref[:, z_col + h * LANES:z_col + (h + 1) * LANES]
        o_ref[:, hs] = (_rms(oc_ref[:, hs], ng_ref[...]) * z_gate).astype(BF16)


def _gdn(gdn_in, sm, alog_row, dtb_row, norm_g, bsz, seq, tc):
    nt = seq // tc
    trow = lambda b, t: (b * nt + t, 0)
    const = lambda b, t: (0, 0)
    n_chunks = tc // CHUNK
    return pl.pallas_call(
        functools.partial(_gdn_body, tc=tc),
        grid=(bsz, nt),
        in_specs=[
            pl.BlockSpec((tc, GDN_WIDTH), trow),
            pl.BlockSpec((tc, LANES), trow),
            pl.BlockSpec((1, LANES), const),
            pl.BlockSpec((1, LANES), const),
            pl.BlockSpec((1, B_HEAD_DIM), const),
        ],
        out_specs=pl.BlockSpec((tc, B_WIDTH), trow),
        out_shape=jax.ShapeDtypeStruct((bsz * seq, B_WIDTH), BF16),
        scratch_shapes=[
            pltpu.VMEM((5, tc, LANES), F32),
            pltpu.VMEM((n_chunks, LANES, CHUNK), F32),
            pltpu.VMEM((B_HEADS, B_HEAD_DIM, B_HEAD_DIM), F32),
            pltpu.VMEM((tc, B_WIDTH), F32),
            pltpu.VMEM((tc, B_WIDTH), F32),
            pltpu.VMEM((n_chunks, B_HEADS, 2 * CHUNK, B_HEAD_DIM), BF16),
            pltpu.VMEM((n_chunks, B_HEADS, CHUNK + B_HEAD_DIM, CHUNK), BF16),
        ],
        compiler_params=pltpu.CompilerParams(
            dimension_semantics=("arbitrary", "arbitrary"), vmem_limit_bytes=VMEM_LIMIT),
        name="gdn",
    )(gdn_in, sm, alog_row, dtb_row, norm_g)


def _ffn_body(x_ref, oa_ref, ob_ref, wo_ref, g2_ref, w1_ref, w2_ref, g3_ref, out_ref, *,
              final_norm):
    y = (x_ref[...] + _dot(oa_ref[...], wo_ref[0:A_WIDTH, :])
         + _dot(ob_ref[...], wo_ref[A_WIDTH:, :]))
    h = _rms(y, g2_ref[...]).astype(BF16)
    a = jnp.square(jnp.maximum(_dot(h, w1_ref[...]), 0.0)).astype(BF16)
    acc = y + _dot(a, w2_ref[...])
    out_ref[...] = _rms(acc, g3_ref[...]) if final_norm else acc


def _ffn(x2, oa, ob, wo, g2, w1, w2, g3, tm, final_norm):
    m = x2.shape[0]
    row = lambda i: (i, 0)
    const = lambda i: (0, 0)
    resident = functools.partial(pl.BlockSpec, index_map=const, pipeline_mode=pl.Buffered(1))
    return pl.pallas_call(
        functools.partial(_ffn_body, final_norm=final_norm),
        grid=(m // tm,),
        in_specs=[
            pl.BlockSpec((tm, D_MODEL), row),
            pl.BlockSpec((tm, A_WIDTH), row),
            pl.BlockSpec((tm, B_WIDTH), row),
            resident((D_MODEL, D_MODEL)),
            pl.BlockSpec((1, D_MODEL), const),
            resident((D_MODEL, D_FF)),
            resident((D_FF, D_MODEL)),
            pl.BlockSpec((1, D_MODEL), const),
        ],
        out_specs=pl.BlockSpec((tm, D_MODEL), row),
        out_shape=jax.ShapeDtypeStruct((m, D_MODEL), F32),
        compiler_params=pltpu.CompilerParams(
            dimension_semantics=("arbitrary",), vmem_limit_bytes=VMEM_LIMIT),
        name="ffn",
    )(x2, oa, ob, wo, g2, w1, w2, g3)


def _rope_tables(seq):
    half = A_HEAD_DIM // 2
    inv_freq = 1.0 / (ROPE_THETA ** (jnp.arange(half, dtype=F32) / half))
    ang = jnp.arange(seq).astype(F32)[:, None] * inv_freq[None, :]
    cos = jnp.cos(ang)
    sin = jnp.sin(ang)
    reps = LANES // A_HEAD_DIM
    return (jnp.tile(cos, (1, 2 * reps)),
            jnp.tile(jnp.concatenate([-sin, sin], axis=1), (1, reps)))


def _permute_in_weight(w):
    sizes = (A_WIDTH, KV_WIDTH, KV_WIDTH, IDX_HEADS * IDX_DIM, IDX_DIM, IDX_HEADS,
             B_WIDTH, B_WIDTH, B_WIDTH, B_WIDTH, B_HEADS, B_HEADS)
    parts, off = [], 0
    for s in sizes:
        parts.append(w[:, off:off + s])
        off += s
    qa, ka, va, qi, ki, wi, qb, kb, vb, zb, bb, ab = parts
    pad = jnp.zeros((w.shape[0], LANES - (IDX_DIM + IDX_HEADS + 2 * B_HEADS)), w.dtype)
    return jnp.concatenate([qa, qi, ka, va, qb, kb, vb, zb, ki, wi, bb, ab, pad],
                           axis=1).astype(BF16)


def _lane_row(vals, offset):
    return jnp.zeros((1, LANES), F32).at[0, offset:offset + vals.shape[0]].set(vals.astype(F32))


def kernel(x, norm_mix_g, w_in, conv_w, a_log, dt_bias, gdn_norm_g, w_out,
           norm_ffn_g, w_ff1, w_ff2, norm_final_g):
    bsz, seq, d = x.shape
    depth = w_in.shape[0]
    m = bsz * seq
    cos_t, sin_t = _rope_tables(seq)
    x2 = x.reshape(m, d)
    for l in range(depth):
        qa, qi, kv, gdn_in, sm = _inproj(
            x2, norm_mix_g[l][None, :], _permute_in_weight(w_in[l]), cos_t, sin_t, conv_w[l],
            seq, tm=512)
        o_a = _mixer_a(qa, qi, kv, sm, bsz, seq)
        o_b = _gdn(gdn_in, sm, _lane_row(a_log[l], SM_DECAY),
                   _lane_row(dt_bias[l], SM_DECAY), gdn_norm_g[l][None, :], bsz, seq, tc=512)
        x2 = _ffn(x2, o_a, o_b, w_out[l].astype(BF16), norm_ffn_g[l][None, :],
                  w_ff1[l].astype(BF16), w_ff2[l].astype(BF16), norm_final_g[None, :],
                  tm=512, final_norm=(l == depth - 1))
    return x2.reshape(bsz, seq, d)
```

```python
import functools

import jax
import jax.numpy as jnp
from jax import lax
from jax.experimental import pallas as pl
from jax.experimental.pallas import tpu as pltpu

F32 = jnp.float32
BF16 = jnp.bfloat16
I32 = jnp.int32

D_MODEL = 1024
CHUNK = 64
A_QUERIES = 256
ROPE_THETA = 10000.0
EPS = 1e-6
A_HEADS = 8
A_KV_HEADS = 2
A_HEAD_DIM = 64
IDX_HEADS = 8
IDX_DIM = 64
TOPK_MAX = 256
B_HEADS = 4
B_HEAD_DIM = 128
CONV_WIDTH = 4
D_FF = 4 * D_MODEL

LANES = 128
A_WIDTH = A_HEADS * A_HEAD_DIM
KV_WIDTH = A_KV_HEADS * A_HEAD_DIM
B_WIDTH = B_HEADS * B_HEAD_DIM
GDN_WIDTH = 4 * B_WIDTH
SM_KI = 0
SM_WI = IDX_DIM
SM_BETA = SM_WI + IDX_HEADS
SM_DECAY = SM_BETA + B_HEADS
C_QA = 0
C_QI = C_QA + A_WIDTH
C_KV = C_QI + A_WIDTH
C_GDN = C_KV + 2 * KV_WIDTH
C_SM = C_GDN + GDN_WIDTH
IN_COLS = C_SM + LANES

VMEM_LIMIT = 56 * 1024 * 1024
CONV_PAD = 8
COUNT_GROUP = 4
COUNT_ROWS = 32
ATT_HEADS = 2
ATT_WAVE = 2
KEY_BLOCK = 2 * LANES
NEG_BIG = -1e30
LOG2_E = 1.4426950408889634
NEG_INF_KEY = -(2 ** 31) + 0x7FFFFF


def _rms(x, g):
    return x * lax.rsqrt(jnp.mean(x * x, axis=-1, keepdims=True) + EPS) * g


def _dot(a, b):
    return jnp.dot(a, b, preferred_element_type=F32)


def _dot_nt(a, b):
    return lax.dot_general(a, b, (((1,), (1,)), ((), ())), preferred_element_type=F32)


def _inproj_body(x_ref, g_ref, w_ref, cos_ref, sin_ref, cw_ref,
                 qa_ref, qi_ref, kv_ref, gdn_ref, sm_ref, xpad_ref, *, tiles_per_seq):
    h = _rms(x_ref[...], g_ref[...]).astype(BF16)
    cos = cos_ref[...]
    sin = sin_ref[...]
    lane = lax.broadcasted_iota(I32, cos.shape, 1)
    first_half = (lane & (A_HEAD_DIM - 1)) < A_HEAD_DIM // 2

    def rope(t):
        swapped = jnp.where(first_half, pltpu.roll(t, LANES - A_HEAD_DIM // 2, 1),
                            pltpu.roll(t, A_HEAD_DIM // 2, 1))
        return t * cos + swapped * sin

    def proj(c0, width):
        return _dot(h, w_ref[:, c0:c0 + width])

    tm = x_ref.shape[0]
    conv_cols = 3 * B_WIDTH

    @pl.when(pl.program_id(0) % tiles_per_seq == 0)
    def _():
        xpad_ref[0:CONV_PAD, :] = jnp.zeros((CONV_PAD, conv_cols), F32)

    @pl.when(pl.program_id(0) % tiles_per_seq != 0)
    def _():
        xpad_ref[0:CONV_PAD, :] = xpad_ref[tm:tm + CONV_PAD, :]

    def stage_conv_input(seg):
        xpad_ref[CONV_PAD:CONV_PAD + tm, seg * B_WIDTH:(seg + 1) * B_WIDTH] = proj(
            C_GDN + seg * B_WIDTH, B_WIDTH)

    def conv_segment(seg):
        for hh in range(B_HEADS):
            cs = slice(seg * B_WIDTH + hh * LANES, seg * B_WIDTH + (hh + 1) * LANES)
            y = jnp.zeros((tm, LANES), F32)
            for jj in range(CONV_WIDTH):
                r = CONV_PAD - (CONV_WIDTH - 1) + jj
                y = y + cw_ref[jj:jj + 1, cs] * xpad_ref[r:r + tm, cs]
            y = y * jax.nn.sigmoid(y)
            if seg < 2:
                y = y * lax.rsqrt(jnp.sum(y * y, axis=-1, keepdims=True) + EPS)
            if seg == 0:
                y = y * (B_HEAD_DIM ** -0.5)
            gdn_ref[:, cs] = y

    stage_conv_input(0)
    acc = proj(C_QA, A_WIDTH)
    conv_segment(0)
    for j in range(A_WIDTH // LANES):
        sl = slice(j * LANES, (j + 1) * LANES)
        qa_ref[:, sl] = (rope(acc[:, sl]) * (A_HEAD_DIM ** -0.5 * LOG2_E)).astype(BF16)
    stage_conv_input(1)
    acc = proj(C_QI, A_WIDTH)
    conv_segment(1)
    for j in range(A_WIDTH // LANES):
        sl = slice(j * LANES, (j + 1) * LANES)
        qi_ref[:, sl] = rope(acc[:, sl]).astype(BF16)
    stage_conv_input(2)
    acc = proj(C_KV, 2 * KV_WIDTH)
    acc_sm = proj(C_SM, LANES)
    z = proj(C_GDN + conv_cols, B_WIDTH)
    conv_segment(2)
    kv_ref[:, 0:KV_WIDTH] = rope(acc[:, 0:KV_WIDTH]).astype(BF16)
    kv_ref[:, KV_WIDTH:] = acc[:, KV_WIDTH:].astype(BF16)
    sm_ref[...] = jnp.where(lane < IDX_DIM, rope(acc_sm), acc_sm)
    gdn_ref[:, conv_cols:] = z * jax.nn.sigmoid(z)


def _inproj(x2, g, w, cos_t, sin_t, conv_w, seq, tm):
    m = x2.shape[0]
    assert seq % tm == 0, (seq, tm)
    nt = seq // tm
    row = lambda i: (i, 0)
    const = lambda i: (0, 0)
    return pl.pallas_call(
        functools.partial(_inproj_body, tiles_per_seq=nt),
        grid=(m // tm,),
        in_specs=[
            pl.BlockSpec((tm, D_MODEL), row),
            pl.BlockSpec((1, D_MODEL), const),
            pl.BlockSpec((D_MODEL, IN_COLS), const, pipeline_mode=pl.Buffered(1)),
            pl.BlockSpec((tm, LANES), lambda i: (i % nt, 0)),
            pl.BlockSpec((tm, LANES), lambda i: (i % nt, 0)),
            pl.BlockSpec((CONV_WIDTH, 3 * B_WIDTH), const),
        ],
        out_specs=[
            pl.BlockSpec((tm, A_WIDTH), row),
            pl.BlockSpec((tm, A_WIDTH), row),
            pl.BlockSpec((tm, 2 * KV_WIDTH), row),
            pl.BlockSpec((tm, GDN_WIDTH), row),
            pl.BlockSpec((tm, LANES), row),
        ],
        out_shape=[
            jax.ShapeDtypeStruct((m, A_WIDTH), BF16),
            jax.ShapeDtypeStruct((m, A_WIDTH), BF16),
            jax.ShapeDtypeStruct((m, 2 * KV_WIDTH), BF16),
            jax.ShapeDtypeStruct((m, GDN_WIDTH), F32),
            jax.ShapeDtypeStruct((m, LANES), F32),
        ],
        scratch_shapes=[pltpu.VMEM((tm + CONV_PAD, 3 * B_WIDTH), F32)],
        compiler_params=pltpu.CompilerParams(
            dimension_semantics=("arbitrary",), vmem_limit_bytes=VMEM_LIMIT),
        name="inproj",
    )(x2, g, w, cos_t, sin_t, conv_w)


def _mixer_a_body(qa_ref, qi_ref, kv_ref, sm_ref, o_ref,
                  isc_ref, isb_ref, vt_ref, qit_ref, qat_ref, acc_ref, bias_ref, *, seq, topk):
    j = pl.program_id(1)
    nkt = (j + 1) * (A_QUERIES // LANES)
    n_heads_pair = A_WIDTH // LANES
    group = A_HEADS // A_KV_HEADS

    @pl.when(j == 0)
    def _():
        per_block = KEY_BLOCK // LANES
        for t in range(seq // LANES):
            vt = kv_ref[t * LANES:(t + 1) * LANES, KV_WIDTH:].astype(F32)
            c0 = (t % per_block) * LANES
            vt_ref[t // per_block, :, c0:c0 + LANES] = vt.T.astype(BF16)

    for p in range(n_heads_pair):
        sl = slice(p * LANES, (p + 1) * LANES)
        t = qi_ref[:, sl].astype(F32).T
        qit_ref[:, (2 * p) * A_QUERIES:(2 * p + 1) * A_QUERIES] = t[0:IDX_DIM].astype(BF16)
        qit_ref[:, (2 * p + 1) * A_QUERIES:(2 * p + 2) * A_QUERIES] = t[IDX_DIM:].astype(BF16)
        t = qa_ref[:, sl].astype(F32).T
        qat_ref[:, (2 * p) * A_QUERIES:(2 * p + 1) * A_QUERIES] = t[0:A_HEAD_DIM].astype(BF16)
        qat_ref[:, (2 * p + 1) * A_QUERIES:(2 * p + 2) * A_QUERIES] = t[A_HEAD_DIM:].astype(BF16)

    q0 = pl.multiple_of(j * A_QUERIES, A_QUERIES)
    w_t = sm_ref[pl.ds(q0, A_QUERIES), :].T[SM_WI:SM_WI + IDX_HEADS, :]
    w_t = w_t * ((IDX_HEADS ** -0.5) * (IDX_DIM ** -0.5))

    qlane = lax.broadcasted_iota(I32, (1, A_QUERIES), 1)
    limit = q0 + (lax.shift_right_logical(qlane, CHUNK.bit_length() - 1) + 1) * CHUNK
    tile_iota = lax.broadcasted_iota(I32, (LANES, A_QUERIES), 0)
    block_iota = lax.broadcasted_iota(I32, (KEY_BLOCK, A_QUERIES), 0)
    nkb = lax.shift_right_logical(nkt + (KEY_BLOCK // LANES - 1), (KEY_BLOCK // LANES).bit_length() - 1)

    def isc_body(kb, carry):
        r0 = pl.multiple_of(kb * KEY_BLOCK, KEY_BLOCK)
        kid = sm_ref[pl.ds(r0, KEY_BLOCK), SM_KI:SM_KI + IDX_DIM].astype(BF16)
        rel = _dot(kid, qit_ref[...])
        acc = jnp.zeros((KEY_BLOCK, A_QUERIES), F32)
        for h in range(IDX_HEADS):
            acc = acc + w_t[h:h + 1, :] * jnp.maximum(rel[:, h * A_QUERIES:(h + 1) * A_QUERIES], 0.0)
        acc = jnp.where(r0 + block_iota < limit, acc, -jnp.inf)
        isc_ref[pl.ds(r0, KEY_BLOCK), :] = acc
        isb_ref[pl.ds(r0, KEY_BLOCK), :] = acc.astype(BF16)
        return carry

    lax.fori_loop(0, nkb, isc_body, 0)

    ngrp = lax.shift_right_logical(nkt + (COUNT_GROUP - 1), COUNT_GROUP.bit_length() - 1)

    def fill_body(kt, carry):
        r0 = pl.multiple_of(kt * LANES, LANES)
        isc_ref[pl.ds(r0, LANES), :] = jnp.full((LANES, A_QUERIES), -jnp.inf, F32)
        isb_ref[pl.ds(r0, LANES), :] = jnp.full((LANES, A_QUERIES), -jnp.inf, BF16)
        return carry

    lax.fori_loop(nkb * (KEY_BLOCK // LANES), ngrp * COUNT_GROUP, fill_body, 0)

    def count(pred):
        def body(g, acc):
            for t in range(COUNT_GROUP):
                r0 = pl.multiple_of((g * COUNT_GROUP + t) * LANES, LANES)
                hit = jnp.where(pred(isc_ref[pl.ds(r0, LANES), :], r0 + tile_iota), 1.0, 0.0)
                acc = acc + hit.reshape(LANES // COUNT_ROWS, COUNT_ROWS, A_QUERIES).sum(axis=0)
            return acc
        acc = lax.fori_loop(0, ngrp, body, jnp.zeros((COUNT_ROWS, A_QUERIES), F32))
        return acc.sum(axis=0, keepdims=True)

    def count_coarse(thr):
        def body(g, acc):
            for t in range(COUNT_GROUP):
                r0 = pl.multiple_of((g * COUNT_GROUP + t) * LANES, LANES)
                hit = jnp.where(isb_ref[pl.ds(r0, LANES), :] >= thr, one_b, zero_b)
                hit = hit.reshape(LANES // COUNT_ROWS, COUNT_ROWS, A_QUERIES)
                part = hit[0]
                for i in range(1, LANES // COUNT_ROWS):
                    part = part + hit[i]
                acc = acc + part
            return acc
        acc = lax.fori_loop(0, ngrp, body, jnp.zeros((COUNT_ROWS, A_QUERIES), BF16))
        return acc.astype(F32).sum(axis=0, keepdims=True)

    kf = float(topk)
    one_b = jnp.ones((), BF16)
    zero_b = jnp.zeros((), BF16)

    def thr_coarse(key):
        bits = jnp.where(key >= 0, key, key ^ jnp.int32(0x7FFF))
        return lax.bitcast_convert_type(lax.shift_left(bits, 16), F32).astype(BF16)

    c0 = count_coarse(jnp.zeros((1, A_QUERIES), BF16))
    lo16 = jnp.where(c0 >= kf, jnp.int32(0), jnp.int32(-2 ** 15))

    def coarse_body(i, lo):
        trial = lo | lax.shift_left(jnp.int32(1), 14 - i)
        c = count_coarse(thr_coarse(trial))
        return jnp.where(c >= kf, trial, lo)

    lo16 = lax.fori_loop(0, 15, coarse_body, lo16)
    lo16 = jnp.maximum(lo16, jnp.int32(NEG_INF_KEY >> 16))
    center = lax.shift_left(lo16, 16) | jnp.where(lo16 < 0, jnp.int32(0xFFFF), jnp.int32(0))

    def thr_of(key):
        bits = jnp.where(key >= 0, key, key ^ jnp.int32(0x7FFFFFFF))
        return jnp.where(key < jnp.int32(NEG_INF_KEY), -jnp.inf, lax.bitcast_convert_type(bits, F32))

    def fine_body(i, st):
        lo, c_lo = st
        trial = lo + lax.shift_left(jnp.int32(1), 16 - i)
        thr = thr_of(trial)
        c = count(lambda x, s: x >= thr)
        ok = c >= kf
        return jnp.where(ok, trial, lo), jnp.where(ok, c, c_lo)

    lo, c_lo = lax.fori_loop(0, 17, fine_body,
                             (center - jnp.int32(1 << 16), jnp.full((1, A_QUERIES), -1.0, F32)))
    kth = thr_of(lo)

    idx_bits = (seq - 1).bit_length()

    def tie_search():
        need = kf - count(lambda x, s: x > kth)

        def tie_body(i, last):
            trial = last | lax.shift_left(jnp.int32(1), idx_bits - 1 - i)
            c = count(lambda x, s: (x == kth) & (s < trial))
            return jnp.where(c < need, trial, last)

        return lax.fori_loop(0, idx_bits, tie_body, jnp.zeros((1, A_QUERIES), I32))

    def take_all_ties():
        return jnp.full((1, A_QUERIES), (1 << idx_bits) - 1, I32)

    n_ge = jnp.where(c_lo < 0.0, jnp.inf, c_lo)
    last = lax.cond(jnp.max(n_ge) > kf, tie_search, take_all_ties)

    acc_ref[...] = jnp.zeros_like(acc_ref)

    def att_body(kb, carry):
        m_prev, l_prev = carry
        r0 = pl.multiple_of(kb * KEY_BLOCK, KEY_BLOCK)
        x = isc_ref[pl.ds(r0, KEY_BLOCK), :]
        s_idx = r0 + block_iota
        sel = (s_idx < limit) & ((x > kth) | ((x == kth) & (s_idx <= last)))
        bias_ref[...] = jnp.where(sel, 0.0, NEG_BIG)
        k_tile = kv_ref[pl.ds(r0, KEY_BLOCK), 0:KV_WIDTH]
        v_t = vt_ref[kb]

        units = list(range(A_HEADS // ATT_HEADS))
        uw = ATT_HEADS * A_QUERIES
        us = [slice(u * uw, (u + 1) * uw) for u in units]
        ds = [slice((u * ATT_HEADS // group) * A_HEAD_DIM, (u * ATT_HEADS // group + 1) * A_HEAD_DIM)
              for u in units]
        cs = [slice((u * ATT_HEADS % group) * A_QUERIES, (u * ATT_HEADS % group + ATT_HEADS) * A_QUERIES)
              for u in units]
        m_new, l_new = {}, {}
        for w0 in range(0, len(units), ATT_WAVE):
            wave = units[w0:w0 + ATT_WAVE]
            s = {u: _dot(k_tile[:, ds[u]], qat_ref[:, us[u]])
                 + jnp.concatenate([bias_ref[...]] * ATT_HEADS, axis=1) for u in wave}
            for u in wave:
                m_new[u] = jnp.maximum(m_prev[:, us[u]], jnp.max(s[u], axis=0, keepdims=True))
            alpha = {u: jnp.exp2(m_prev[:, us[u]] - m_new[u]) for u in wave}
            p = {u: jnp.exp2(s[u] - m_new[u]) for u in wave}
            pv = {u: _dot(v_t[ds[u], :], p[u].astype(BF16)) for u in wave}
            for u in wave:
                l_new[u] = alpha[u] * l_prev[:, us[u]] + jnp.sum(p[u], axis=0, keepdims=True)
            for u in wave:
                acc_ref[ds[u], cs[u]] = alpha[u] * acc_ref[ds[u], cs[u]] + pv[u]
        return (jnp.concatenate([m_new[u] for u in units], axis=1),
                jnp.concatenate([l_new[u] for u in units], axis=1))

    m0 = jnp.full((1, A_HEADS * A_QUERIES), NEG_BIG, F32)
    l0 = jnp.zeros((1, A_HEADS * A_QUERIES), F32)
    _, l_fin = lax.fori_loop(0, nkb, att_body, (m0, l0))

    for p in range(n_heads_pair):
        g = (2 * p) // group
        ds = slice(g * A_HEAD_DIM, (g + 1) * A_HEAD_DIM)
        parts = []
        for h in (2 * p, 2 * p + 1):
            hl = h % group
            parts.append(acc_ref[ds, hl * A_QUERIES:(hl + 1) * A_QUERIES]
                         / l_fin[:, h * A_QUERIES:(h + 1) * A_QUERIES])
        o_ref[:, p * LANES:(p + 1) * LANES] = jnp.concatenate(parts, axis=0).T.astype(BF16)


def _mixer_a(qa, qi, kv, sm, bsz, seq):
    nq = seq // A_QUERIES
    assert seq % (COUNT_GROUP * LANES) == 0, seq
    assert seq // COUNT_ROWS <= 256, seq
    topk = min(TOPK_MAX, seq // 4)
    qrow = lambda b, j: (b * nq + j, 0)
    brow = lambda b, j: (b, 0)
    return pl.pallas_call(
        functools.partial(_mixer_a_body, seq=seq, topk=topk),
        grid=(bsz, nq),
        in_specs=[
            pl.BlockSpec((A_QUERIES, A_WIDTH), qrow),
            pl.BlockSpec((A_QUERIES, A_WIDTH), qrow),
            pl.BlockSpec((seq, 2 * KV_WIDTH), brow),
            pl.BlockSpec((seq, LANES), brow),
        ],
        out_specs=pl.BlockSpec((A_QUERIES, A_WIDTH), qrow),
        out_shape=jax.ShapeDtypeStruct((bsz * seq, A_WIDTH), BF16),
        scratch_shapes=[
            pltpu.VMEM((seq, A_QUERIES), F32),
            pltpu.VMEM((seq, A_QUERIES), BF16),
            pltpu.VMEM((seq // KEY_BLOCK, KV_WIDTH, KEY_BLOCK), BF16),
            pltpu.VMEM((IDX_DIM, IDX_HEADS * A_QUERIES), BF16),
            pltpu.VMEM((A_HEAD_DIM, A_HEADS * A_QUERIES), BF16),
            pltpu.VMEM((KV_WIDTH, (A_HEADS // A_KV_HEADS) * A_QUERIES), F32),
            pltpu.VMEM((KEY_BLOCK, A_QUERIES), F32),
        ],
        compiler_params=pltpu.CompilerParams(
            dimension_semantics=("arbitrary", "arbitrary"), vmem_limit_bytes=VMEM_LIMIT),
        name="mixer_a",
    )(qa, qi, kv, sm)


def _gdn_body(gdn_ref, sm_ref, alog_ref, dtb_ref, ng_ref, o_ref,
              gate_ref, gct_ref, state_ref, oc_ref, u_ref, wq_ref, ak_ref, *, tc):
    tb = pl.program_id(1)
    n_chunks = tc // CHUNK
    prep_unroll = 8
    q_col, k_col, v_col, z_col = (i * B_WIDTH for i in range(4))

    @pl.when(tb == 0)
    def _():
        state_ref[...] = jnp.zeros_like(state_ref)

    sm = sm_ref[...]
    beta = jax.nn.sigmoid(sm)
    z = sm + dtb_ref[...]
    softplus = jnp.maximum(z, 0.0) + jnp.log(1.0 + jnp.exp(-jnp.abs(z)))
    g = -jnp.exp(alog_ref[...]) * softplus
    rin = lax.broadcasted_iota(I32, (tc, LANES), 0) & (CHUNK - 1)
    gc = g
    step = 1
    while step < CHUNK:
        gc = gc + jnp.where(rin >= step, pltpu.roll(gc, step, 0), 0.0)
        step *= 2
    gc3 = gc.reshape(n_chunks, CHUNK, LANES)
    g_last = jnp.broadcast_to(gc3[:, CHUNK - 1:CHUNK, :], gc3.shape).reshape(tc, LANES)
    gate_ref[0] = beta
    gate_ref[1] = gc
    gate_ref[2] = jnp.exp(gc)
    gate_ref[3] = jnp.exp(g_last - gc)
    gate_ref[4] = jnp.exp(g_last)
    for i in range(tc // LANES):
        t = gc[i * LANES:(i + 1) * LANES, :].T
        for half in range(LANES // CHUNK):
            gct_ref[i * (LANES // CHUNK) + half] = t[:, half * CHUNK:(half + 1) * CHUNK]

    ci = lax.broadcasted_iota(I32, (CHUNK, CHUNK), 0)
    si = lax.broadcasted_iota(I32, (CHUNK, CHUNK), 1)
    wl = lax.broadcasted_iota(I32, (CHUNK, 2 * CHUNK), 1)
    wr = lax.broadcasted_iota(I32, (CHUNK, 2 * CHUNK), 0)
    right = wl >= CHUNK
    eye_right = jnp.where(wl == wr + CHUNK, 1.0, 0.0)

    def prep_body(cg, carry):
        units = [(cg * prep_unroll + cc, h) for cc in range(prep_unroll) for h in range(B_HEADS)]
        rows = [pl.ds(pl.multiple_of(c * CHUNK, CHUNK), CHUNK) for c, _ in units]
        hsl = [slice(h * LANES, (h + 1) * LANES) for _, h in units]
        idx = range(len(units))

        def col(i, gate, off):
            h = units[i][1]
            return gate_ref[gate, rows[i], off + h:off + h + 1]

        def seg(i, col0):
            h = units[i][1]
            return gdn_ref[rows[i], col0 + h * LANES:col0 + (h + 1) * LANES]

        q = [seg(i, q_col) for i in idx]
        k = [seg(i, k_col) for i in idx]
        kb = [k[i] * col(i, 0, SM_BETA) for i in idx]
        kq = [_dot_nt(jnp.concatenate([kb[i], q[i]], axis=0).astype(BF16), k[i].astype(BF16))
              for i in idx]
        decay = []
        for i, (c, h) in enumerate(units):
            d = col(i, 1, SM_DECAY) - gct_ref[c][SM_DECAY + h:SM_DECAY + h + 1, :]
            decay.append(jnp.where(ci >= si, jnp.exp(jnp.where(ci >= si, d, 0.0)), 0.0))
        wmat = []
        for i, (c, h) in enumerate(units):
            n_mat = jnp.where(ci > si, -(kq[i][0:CHUNK] * decay[i]), 0.0)
            wmat.append(jnp.concatenate([n_mat, jnp.zeros_like(n_mat)], axis=1) + eye_right)
            ak_ref[c, h, 0:CHUNK, :] = (kq[i][CHUNK:] * decay[i]).astype(BF16)
        pw = 1
        while pw < CHUNK:
            wb = [wmat[i].astype(BF16) for i in idx]
            wmat = [_dot(wb[i][:, 0:CHUNK], wb[i]) + jnp.where(right, wmat[i], 0.0) for i in idx]
            pw *= 2
        eg = [col(i, 2, SM_DECAY) for i in idx]
        rhs = [jnp.concatenate([seg(i, v_col) * col(i, 0, SM_BETA), kb[i] * eg[i]],
                               axis=1).astype(BF16) for i in idx]
        sol = [_dot(wmat[i][:, CHUNK:].astype(BF16), rhs[i]) for i in idx]
        for i, (c, h) in enumerate(units):
            u_ref[rows[i], hsl[i]] = sol[i][:, 0:B_HEAD_DIM]
            wq_ref[c, h] = jnp.concatenate([sol[i][:, B_HEAD_DIM:], q[i] * eg[i]],
                                           axis=0).astype(BF16)
            ak_ref[c, h, CHUNK:, :] = (k[i] * col(i, 3, SM_DECAY)).T.astype(BF16)
        return carry

    lax.fori_loop(0, n_chunks // prep_unroll, prep_body, 0)

    def scan_body(c, carry):
        r0 = pl.multiple_of(c * CHUNK, CHUNK)
        rows = pl.ds(r0, CHUNK)
        heads = range(B_HEADS)
        hsl = [slice(h * LANES, (h + 1) * LANES) for h in heads]
        s_prev = [state_ref[h] for h in heads]
        ws = [_dot(wq_ref[c, h], s_prev[h].astype(BF16)) for h in heads]
        v_new = [(u_ref[rows, hsl[h]] - ws[h][0:CHUNK]).astype(BF16) for h in heads]
        r = [_dot(ak_ref[c, h], v_new[h]) for h in heads]
        for h in heads:
            gl = gate_ref[4, rows, SM_DECAY + h:SM_DECAY + h + 1][0:1, :]
            oc_ref[rows, hsl[h]] = ws[h][CHUNK:] + r[h][0:CHUNK]
            state_ref[h] = s_prev[h] * gl + r[h][CHUNK:]
        return carry

    lax.fori_loop(0, n_chunks, scan_body, 0)

    for h in range(B_HEADS):
        hs = slice(h * LANES, (h + 1) * LANES)
        z_gate = gdn_ref[:, z_col + h * LANES:z_col + (h + 1) * LANES]
        o_ref[:, hs] = (_rms(oc_ref[:, hs], ng_ref[...]) * z_gate).astype(BF16)


def _gdn(gdn_in, sm, alog_row, dtb_row, norm_g, bsz, seq, tc):
    nt = seq // tc
    trow = lambda b, t: (b * nt + t, 0)
    const = lambda b, t: (0, 0)
    n_chunks = tc // CHUNK
    return pl.pallas_call(
        functools.partial(_gdn_body, tc=tc),
        grid=(bsz, nt),
        in_specs=[
            pl.BlockSpec((tc, GDN_WIDTH), trow),
            pl.BlockSpec((tc, LANES), trow),
            pl.BlockSpec((1, LANES), const),
            pl.BlockSpec((1, LANES), const),
            pl.BlockSpec((1, B_HEAD_DIM), const),
        ],
        out_specs=pl.BlockSpec((tc, B_WIDTH), trow),
        out_shape=jax.ShapeDtypeStruct((bsz * seq, B_WIDTH), BF16),
        scratch_shapes=[
            pltpu.VMEM((5, tc, LANES), F32),
            pltpu.VMEM((n_chunks, LANES, CHUNK), F32),
            pltpu.VMEM((B_HEADS, B_HEAD_DIM, B_HEAD_DIM), F32),
            pltpu.VMEM((tc, B_WIDTH), F32),
            pltpu.VMEM((tc, B_WIDTH), F32),
            pltpu.VMEM((n_chunks, B_HEADS, 2 * CHUNK, B_HEAD_DIM), BF16),
            pltpu.VMEM((n_chunks, B_HEADS, CHUNK + B_HEAD_DIM, CHUNK), BF16),
        ],
        compiler_params=pltpu.CompilerParams(
            dimension_semantics=("arbitrary", "arbitrary"), vmem_limit_bytes=VMEM_LIMIT),
        name="gdn",
    )(gdn_in, sm, alog_row, dtb_row, norm_g)


def _ffn_body(x_ref, oa_ref, ob_ref, wo_ref, g2_ref, w1_ref, w2_ref, g3_ref, out_ref, *,
              final_norm):
    y = (x_ref[...] + _dot(oa_ref[...], wo_ref[0:A_WIDTH, :])
         + _dot(ob_ref[...], wo_ref[A_WIDTH:, :]))
    h = _rms(y, g2_ref[...]).astype(BF16)
    a = jnp.square(jnp.maximum(_dot(h, w1_ref[...]), 0.0)).astype(BF16)
    acc = y + _dot(a, w2_ref[...])
    out_ref[...] = _rms(acc, g3_ref[...]) if final_norm else acc


def _ffn(x2, oa, ob, wo, g2, w1, w2, g3, tm, final_norm):
    m = x2.shape[0]
    row = lambda i: (i, 0)
    const = lambda i: (0, 0)
    resident = functools.partial(pl.BlockSpec, index_map=const, pipeline_mode=pl.Buffered(1))
    return pl.pallas_call(
        functools.partial(_ffn_body, final_norm=final_norm),
        grid=(m // tm,),
        in_specs=[
            pl.BlockSpec((tm, D_MODEL), row),
            pl.BlockSpec((tm, A_WIDTH), row),
            pl.BlockSpec((tm, B_WIDTH), row),
            resident((D_MODEL, D_MODEL)),
            pl.BlockSpec((1, D_MODEL), const),
            resident((D_MODEL, D_FF)),
            resident((D_FF, D_MODEL)),
            pl.BlockSpec((1, D_MODEL), const),
        ],
        out_specs=pl.BlockSpec((tm, D_MODEL), row),
        out_shape=jax.ShapeDtypeStruct((m, D_MODEL), F32),
        compiler_params=pltpu.CompilerParams(
            dimension_semantics=("arbitrary",), vmem_limit_bytes=VMEM_LIMIT),
        name="ffn",
    )(x2, oa, ob, wo, g2, w1, w2, g3)


def _rope_tables(seq):
    half = A_HEAD_DIM // 2
    inv_freq = 1.0 / (ROPE_THETA ** (jnp.arange(half, dtype=F32) / half))
    ang = jnp.arange(seq).astype(F32)[:, None] * inv_freq[None, :]
    cos = jnp.cos(ang)
    sin = jnp.sin(ang)
    reps = LANES // A_HEAD_DIM
    return (jnp.tile(cos, (1, 2 * reps)),
            jnp.tile(jnp.concatenate([-sin, sin], axis=1), (1, reps)))


def _permute_in_weight(w):
    sizes = (A_WIDTH, KV_WIDTH, KV_WIDTH, IDX_HEADS * IDX_DIM, IDX_DIM, IDX_HEADS,
             B_WIDTH, B_WIDTH, B_WIDTH, B_WIDTH, B_HEADS, B_HEADS)
    parts, off = [], 0
    for s in sizes:
        parts.append(w[:, off:off + s])
        off += s
    qa, ka, va, qi, ki, wi, qb, kb, vb, zb, bb, ab = parts
    pad = jnp.zeros((w.shape[0], LANES - (IDX_DIM + IDX_HEADS + 2 * B_HEADS)), w.dtype)
    return jnp.concatenate([qa, qi, ka, va, qb, kb, vb, zb, ki, wi, bb, ab, pad],
                           axis=1).astype(BF16)


def _lane_row(vals, offset):
    return jnp.zeros((1, LANES), F32).at[0, offset:offset + vals.shape[0]].set(vals.astype(F32))


def kernel(x, norm_mix_g, w_in, conv_w, a_log, dt_bias, gdn_norm_g, w_out,
           norm_ffn_g, w_ff1, w_ff2, norm_final_g):
    bsz, seq, d = x.shape
    depth = w_in.shape[0]
    m = bsz * seq
    cos_t, sin_t = _rope_tables(seq)
    x2 = x.reshape(m, d)
    for l in range(depth):
        qa, qi, kv, gdn_in, sm = _inproj(
            x2, norm_mix_g[l][None, :], _permute_in_weight(w_in[l]), cos_t, sin_t, conv_w[l],
            seq, tm=512)
        o_a = _mixer_a(qa, qi, kv, sm, bsz, seq)
        o_b = _gdn(gdn_in, sm, _lane_row(a_log[l], SM_DECAY),
                   _lane_row(dt_bias[l], SM_DECAY), gdn_norm_g[l][None, :], bsz, seq, tc=512)
        x2 = _ffn(x2, o_a, o_b, w_out[l].astype(BF16), norm_ffn_g[l][None, :],
                  w_ff1[l].astype(BF16), w_ff2[l].astype(BF16), norm_final_g[None, :],
                  tm=512, final_norm=(l == depth - 1))
    return x2.reshape(bsz, seq, d)
```

```python
import functools

import jax
import jax.numpy as jnp
from jax import lax
from jax.experimental import pallas as pl
from jax.experimental.pallas import tpu as pltpu

F32 = jnp.float32
BF16 = jnp.bfloat16
I32 = jnp.int32

D_MODEL = 1024
CHUNK = 64
A_QUERIES = 256
ROPE_THETA = 10000.0
EPS = 1e-6
A_HEADS = 8
A_KV_HEADS = 2
A_HEAD_DIM = 64
IDX_HEADS = 8
IDX_DIM = 64
TOPK_MAX = 256
B_HEADS = 4
B_HEAD_DIM = 128
CONV_WIDTH = 4
D_FF = 4 * D_MODEL

LANES = 128
A_WIDTH = A_HEADS * A_HEAD_DIM
KV_WIDTH = A_KV_HEADS * A_HEAD_DIM
B_WIDTH = B_HEADS * B_HEAD_DIM
GDN_WIDTH = 4 * B_WIDTH
SM_KI = 0
SM_WI = IDX_DIM
SM_BETA = SM_WI + IDX_HEADS
SM_DECAY = SM_BETA + B_HEADS
C_QA = 0
C_QI = C_QA + A_WIDTH
C_KV = C_QI + A_WIDTH
C_GDN = C_KV + 2 * KV_WIDTH
C_SM = C_GDN + GDN_WIDTH
IN_COLS = C_SM + LANES

VMEM_LIMIT = 56 * 1024 * 1024
CONV_PAD = 8
COUNT_GROUP = 4
COUNT_ROWS = 32
ATT_HEADS = 2
ATT_WAVE = 4
KEY_BLOCK = 2 * LANES
NEG_BIG = -1e30
LOG2_E = 1.4426950408889634
NEG_INF_KEY = -(2 ** 31) + 0x7FFFFF


def _rms(x, g):
    return x * lax.rsqrt(jnp.mean(x * x, axis=-1, keepdims=True) + EPS) * g


def _dot(a, b):
    return jnp.dot(a, b, preferred_element_type=F32)


def _dot_nt(a, b):
    return lax.dot_general(a, b, (((1,), (1,)), ((), ())), preferred_element_type=F32)


def _inproj_body(x_ref, g_ref, w_ref, cos_ref, sin_ref, cw_ref,
                 qa_ref, qi_ref, kv_ref, gdn_ref, sm_ref, xpad_ref, *, tiles_per_seq):
    h = _rms(x_ref[...], g_ref[...]).astype(BF16)
    cos = cos_ref[...]
    sin = sin_ref[...]
    lane = lax.broadcasted_iota(I32, cos.shape, 1)
    first_half = (lane & (A_HEAD_DIM - 1)) < A_HEAD_DIM // 2

    def rope(t):
        swapped = jnp.where(first_half, pltpu.roll(t, LANES - A_HEAD_DIM // 2, 1),
                            pltpu.roll(t, A_HEAD_DIM // 2, 1))
        return t * cos + swapped * sin

    def proj(c0, width):
        return _dot(h, w_ref[:, c0:c0 + width])

    tm = x_ref.shape[0]
    conv_cols = 3 * B_WIDTH

    @pl.when(pl.program_id(0) % tiles_per_seq == 0)
    def _():
        xpad_ref[0:CONV_PAD, :] = jnp.zeros((CONV_PAD, conv_cols), F32)

    @pl.when(pl.program_id(0) % tiles_per_seq != 0)
    def _():
        xpad_ref[0:CONV_PAD, :] = xpad_ref[tm:tm + CONV_PAD, :]

    def stage_conv_input(seg):
        xpad_ref[CONV_PAD:CONV_PAD + tm, seg * B_WIDTH:(seg + 1) * B_WIDTH] = proj(
            C_GDN + seg * B_WIDTH, B_WIDTH)

    def conv_segment(seg):
        for hh in range(B_HEADS):
            cs = slice(seg * B_WIDTH + hh * LANES, seg * B_WIDTH + (hh + 1) * LANES)
            xa = xpad_ref[:, cs]
            y = cw_ref[0:1, cs] * xa
            for jj in range(1, CONV_WIDTH):
                y = cw_ref[jj:jj + 1, cs] * xa + pltpu.roll(y, 1, 0)
            y = y[CONV_PAD:, :]
            y = y * jax.nn.sigmoid(y)
            if seg < 2:
                y = y * lax.rsqrt(jnp.sum(y * y, axis=-1, keepdims=True) + EPS)
            if seg == 0:
                y = y * (B_HEAD_DIM ** -0.5)
            gdn_ref[:, cs] = y

    stage_conv_input(0)
    acc = proj(C_QA, A_WIDTH)
    conv_segment(0)
    for j in range(A_WIDTH // LANES):
        sl = slice(j * LANES, (j + 1) * LANES)
        qa_ref[:, sl] = (rope(acc[:, sl]) * (A_HEAD_DIM ** -0.5 * LOG2_E)).astype(BF16)
    stage_conv_input(1)
    acc = proj(C_QI, A_WIDTH)
    conv_segment(1)
    for j in range(A_WIDTH // LANES):
        sl = slice(j * LANES, (j + 1) * LANES)
        qi_ref[:, sl] = rope(acc[:, sl]).astype(BF16)
    stage_conv_input(2)
    acc = proj(C_KV, 2 * KV_WIDTH)
    acc_sm = proj(C_SM, LANES)
    z = proj(C_GDN + conv_cols, B_WIDTH)
    conv_segment(2)
    kv_ref[:, 0:KV_WIDTH] = rope(acc[:, 0:KV_WIDTH]).astype(BF16)
    kv_ref[:, KV_WIDTH:] = acc[:, KV_WIDTH:].astype(BF16)
    sm_ref[...] = jnp.where(lane < IDX_DIM, rope(acc_sm), acc_sm)
    gdn_ref[:, conv_cols:] = z * jax.nn.sigmoid(z)


def _inproj(x2, g, w, cos_t, sin_t, conv_w, seq, tm):
    m = x2.shape[0]
    assert seq % tm == 0, (seq, tm)
    nt = seq // tm
    row = lambda i: (i, 0)
    const = lambda i: (0, 0)
    return pl.pallas_call(
        functools.partial(_inproj_body, tiles_per_seq=nt),
        grid=(m // tm,),
        in_specs=[
            pl.BlockSpec((tm, D_MODEL), row),
            pl.BlockSpec((1, D_MODEL), const),
            pl.BlockSpec((D_MODEL, IN_COLS), const, pipeline_mode=pl.Buffered(1)),
            pl.BlockSpec((tm, LANES), lambda i: (i % nt, 0)),
            pl.BlockSpec((tm, LANES), lambda i: (i % nt, 0)),
            pl.BlockSpec((CONV_WIDTH, 3 * B_WIDTH), const),
        ],
        out_specs=[
            pl.BlockSpec((tm, A_WIDTH), row),
            pl.BlockSpec((tm, A_WIDTH), row),
            pl.BlockSpec((tm, 2 * KV_WIDTH), row),
            pl.BlockSpec((tm, GDN_WIDTH), row),
            pl.BlockSpec((tm, LANES), row),
        ],
        out_shape=[
            jax.ShapeDtypeStruct((m, A_WIDTH), BF16),
            jax.ShapeDtypeStruct((m, A_WIDTH), BF16),
            jax.ShapeDtypeStruct((m, 2 * KV_WIDTH), BF16),
            jax.ShapeDtypeStruct((m, GDN_WIDTH), F32),
            jax.ShapeDtypeStruct((m, LANES), F32),
        ],
        scratch_shapes=[pltpu.VMEM((tm + CONV_PAD, 3 * B_WIDTH), F32)],
        compiler_params=pltpu.CompilerParams(
            dimension_semantics=("arbitrary",), vmem_limit_bytes=VMEM_LIMIT),
        name="inproj",
    )(x2, g, w, cos_t, sin_t, conv_w)


def _mixer_a_body(qa_ref, qi_ref, kv_ref, sm_ref, o_ref,
                  isc_ref, isb_ref, vt_ref, qit_ref, qat_ref, acc_ref, bias_ref, *, seq, topk):
    j = pl.program_id(1)
    nkt = (j + 1) * (A_QUERIES // LANES)
    n_heads_pair = A_WIDTH // LANES
    group = A_HEADS // A_KV_HEADS

    @pl.when(j == 0)
    def _():
        per_block = KEY_BLOCK // LANES
        for t in range(seq // LANES):
            vt = kv_ref[t * LANES:(t + 1) * LANES, KV_WIDTH:].astype(F32)
            c0 = (t % per_block) * LANES
            vt_ref[t // per_block, :, c0:c0 + LANES] = vt.T.astype(BF16)

    for p in range(n_heads_pair):
        sl = slice(p * LANES, (p + 1) * LANES)
        t = qi_ref[:, sl].astype(F32).T
        qit_ref[:, (2 * p) * A_QUERIES:(2 * p + 1) * A_QUERIES] = t[0:IDX_DIM].astype(BF16)
        qit_ref[:, (2 * p + 1) * A_QUERIES:(2 * p + 2) * A_QUERIES] = t[IDX_DIM:].astype(BF16)
        t = qa_ref[:, sl].astype(F32).T
        qat_ref[:, (2 * p) * A_QUERIES:(2 * p + 1) * A_QUERIES] = t[0:A_HEAD_DIM].astype(BF16)
        qat_ref[:, (2 * p + 1) * A_QUERIES:(2 * p + 2) * A_QUERIES] = t[A_HEAD_DIM:].astype(BF16)

    q0 = pl.multiple_of(j * A_QUERIES, A_QUERIES)
    w_t = sm_ref[pl.ds(q0, A_QUERIES), :].T[SM_WI:SM_WI + IDX_HEADS, :]
    w_t = w_t * ((IDX_HEADS ** -0.5) * (IDX_DIM ** -0.5))

    qlane = lax.broadcasted_iota(I32, (1, A_QUERIES), 1)
    limit = q0 + (lax.shift_right_logical(qlane, CHUNK.bit_length() - 1) + 1) * CHUNK
    tile_iota = lax.broadcasted_iota(I32, (LANES, A_QUERIES), 0)
    block_iota = lax.broadcasted_iota(I32, (KEY_BLOCK, A_QUERIES), 0)
    nkb = lax.shift_right_logical(nkt + (KEY_BLOCK // LANES - 1), (KEY_BLOCK // LANES).bit_length() - 1)

    def isc_body(kb, carry):
        r0 = pl.multiple_of(kb * KEY_BLOCK, KEY_BLOCK)
        kid = sm_ref[pl.ds(r0, KEY_BLOCK), SM_KI:SM_KI + IDX_DIM].astype(BF16)
        rel = _dot(kid, qit_ref[...])
        acc = jnp.zeros((KEY_BLOCK, A_QUERIES), F32)
        for h in range(IDX_HEADS):
            acc = acc + w_t[h:h + 1, :] * jnp.maximum(rel[:, h * A_QUERIES:(h + 1) * A_QUERIES], 0.0)
        acc = jnp.where(r0 + block_iota < limit, acc, -jnp.inf)
        isc_ref[pl.ds(r0, KEY_BLOCK), :] = acc
        isb_ref[pl.ds(r0, KEY_BLOCK), :] = acc.astype(BF16)
        return carry

    lax.fori_loop(0, nkb, isc_body, 0)

    ngrp = lax.shift_right_logical(nkt + (COUNT_GROUP - 1), COUNT_GROUP.bit_length() - 1)

    def fill_body(kt, carry):
        r0 = pl.multiple_of(kt * LANES, LANES)
        isc_ref[pl.ds(r0, LANES), :] = jnp.full((LANES, A_QUERIES), -jnp.inf, F32)
        isb_ref[pl.ds(r0, LANES), :] = jnp.full((LANES, A_QUERIES), -jnp.inf, BF16)
        return carry

    lax.fori_loop(nkb * (KEY_BLOCK // LANES), ngrp * COUNT_GROUP, fill_body, 0)

    def count(pred):
        def body(g, acc):
            for t in range(COUNT_GROUP):
                r0 = pl.multiple_of((g * COUNT_GROUP + t) * LANES, LANES)
                hit = jnp.where(pred(isc_ref[pl.ds(r0, LANES), :], r0 + tile_iota), 1.0, 0.0)
                acc = acc + hit.reshape(LANES // COUNT_ROWS, COUNT_ROWS, A_QUERIES).sum(axis=0)
            return acc
        acc = lax.fori_loop(0, ngrp, body, jnp.zeros((COUNT_ROWS, A_QUERIES), F32))
        return acc.sum(axis=0, keepdims=True)

    def count_coarse(thr):
        def body(g, acc):
            for t in range(COUNT_GROUP):
                r0 = pl.multiple_of((g * COUNT_GROUP + t) * LANES, LANES)
                hit = jnp.where(isb_ref[pl.ds(r0, LANES), :] >= thr, one_b, zero_b)
                hit = hit.reshape(LANES // COUNT_ROWS, COUNT_ROWS, A_QUERIES)
                part = hit[0]
                for i in range(1, LANES // COUNT_ROWS):
                    part = part + hit[i]
                acc = acc + part
            return acc
        acc = lax.fori_loop(0, ngrp, body, jnp.zeros((COUNT_ROWS, A_QUERIES), BF16))
        return acc.astype(F32).sum(axis=0, keepdims=True)

    kf = float(topk)
    one_b = jnp.ones((), BF16)
    zero_b = jnp.zeros((), BF16)

    def thr_coarse(key):
        bits = jnp.where(key >= 0, key, key ^ jnp.int32(0x7FFF))
        return lax.bitcast_convert_type(lax.shift_left(bits, 16), F32).astype(BF16)

    c0 = count_coarse(jnp.zeros((1, A_QUERIES), BF16))
    lo16 = jnp.where(c0 >= kf, jnp.int32(0), jnp.int32(-2 ** 15))

    def coarse_body(i, lo):
        trial = lo | lax.shift_left(jnp.int32(1), 14 - i)
        c = count_coarse(thr_coarse(trial))
        return jnp.where(c >= kf, trial, lo)

    lo16 = lax.fori_loop(0, 15, coarse_body, lo16)
    lo16 = jnp.maximum(lo16, jnp.int32(NEG_INF_KEY >> 16))
    center = lax.shift_left(lo16, 16) | jnp.where(lo16 < 0, jnp.int32(0xFFFF), jnp.int32(0))

    def thr_of(key):
        bits = jnp.where(key >= 0, key, key ^ jnp.int32(0x7FFFFFFF))
        return jnp.where(key < jnp.int32(NEG_INF_KEY), -jnp.inf, lax.bitcast_convert_type(bits, F32))

    def fine_body(i, st):
        lo, c_lo = st
        trial = lo + lax.shift_left(jnp.int32(1), 16 - i)
        thr = thr_of(trial)
        c = count(lambda x, s: x >= thr)
        ok = c >= kf
        return jnp.where(ok, trial, lo), jnp.where(ok, c, c_lo)

    lo, c_lo = lax.fori_loop(0, 17, fine_body,
                             (center - jnp.int32(1 << 16), jnp.full((1, A_QUERIES), -1.0, F32)))
    kth = thr_of(lo)

    idx_bits = (seq - 1).bit_length()

    def tie_search():
        need = kf - count(lambda x, s: x > kth)

        def tie_body(i, last):
            trial = last | lax.shift_left(jnp.int32(1), idx_bits - 1 - i)
            c = count(lambda x, s: (x == kth) & (s < trial))
            return jnp.where(c < need, trial, last)

        return lax.fori_loop(0, idx_bits, tie_body, jnp.zeros((1, A_QUERIES), I32))

    def take_all_ties():
        return jnp.full((1, A_QUERIES), (1 << idx_bits) - 1, I32)

    n_ge = jnp.where(c_lo < 0.0, jnp.inf, c_lo)
    last = lax.cond(jnp.max(n_ge) > kf, tie_search, take_all_ties)

    acc_ref[...] = jnp.zeros_like(acc_ref)

    def att_body(kb, carry):
        m_prev, l_prev = carry
        r0 = pl.multiple_of(kb * KEY_BLOCK, KEY_BLOCK)
        x = isc_ref[pl.ds(r0, KEY_BLOCK), :]
        s_idx = r0 + block_iota
        sel = (s_idx < limit) & ((x > kth) | ((x == kth) & (s_idx <= last)))
        bias_ref[...] = jnp.where(sel, 0.0, NEG_BIG)
        k_tile = kv_ref[pl.ds(r0, KEY_BLOCK), 0:KV_WIDTH]
        v_t = vt_ref[kb]

        units = list(range(A_HEADS // ATT_HEADS))
        uw = ATT_HEADS * A_QUERIES
        us = [slice(u * uw, (u + 1) * uw) for u in units]
        ds = [slice((u * ATT_HEADS // group) * A_HEAD_DIM, (u * ATT_HEADS // group + 1) * A_HEAD_DIM)
              for u in units]
        cs = [slice((u * ATT_HEADS % group) * A_QUERIES, (u * ATT_HEADS % group + ATT_HEADS) * A_QUERIES)
              for u in units]
        m_new, l_new = {}, {}
        for w0 in range(0, len(units), ATT_WAVE):
            wave = units[w0:w0 + ATT_WAVE]
            s = {u: _dot(k_tile[:, ds[u]], qat_ref[:, us[u]])
                 + jnp.concatenate([bias_ref[...]] * ATT_HEADS, axis=1) for u in wave}
            for u in wave:
                m_new[u] = jnp.maximum(m_prev[:, us[u]], jnp.max(s[u], axis=0, keepdims=True))
            alpha = {u: jnp.exp2(m_prev[:, us[u]] - m_new[u]) for u in wave}
            p = {u: jnp.exp2(s[u] - m_new[u]) for u in wave}
            pv = {u: _dot(v_t[ds[u], :], p[u].astype(BF16)) for u in wave}
            for u in wave:
                l_new[u] = alpha[u] * l_prev[:, us[u]] + jnp.sum(p[u], axis=0, keepdims=True)
            for u in wave:
                acc_ref[ds[u], cs[u]] = alpha[u] * acc_ref[ds[u], cs[u]] + pv[u]
        return (jnp.concatenate([m_new[u] for u in units], axis=1),
                jnp.concatenate([l_new[u] for u in units], axis=1))

    m0 = jnp.full((1, A_HEADS * A_QUERIES), NEG_BIG, F32)
    l0 = jnp.zeros((1, A_HEADS * A_QUERIES), F32)
    _, l_fin = lax.fori_loop(0, nkb, att_body, (m0, l0))

    for p in range(n_heads_pair):
        g = (2 * p) // group
        ds = slice(g * A_HEAD_DIM, (g + 1) * A_HEAD_DIM)
        parts = []
        for h in (2 * p, 2 * p + 1):
            hl = h % group
            parts.append(acc_ref[ds, hl * A_QUERIES:(hl + 1) * A_QUERIES]
                         / l_fin[:, h * A_QUERIES:(h + 1) * A_QUERIES])
        o_ref[:, p * LANES:(p + 1) * LANES] = jnp.concatenate(parts, axis=0).T.astype(BF16)


def _mixer_a(qa, qi, kv, sm, bsz, seq):
    nq = seq // A_QUERIES
    assert seq % (COUNT_GROUP * LANES) == 0, seq
    assert seq // COUNT_ROWS <= 256, seq
    topk = min(TOPK_MAX, seq // 4)
    qrow = lambda b, j: (b * nq + j, 0)
    brow = lambda b, j: (b, 0)
    return pl.pallas_call(
        functools.partial(_mixer_a_body, seq=seq, topk=topk),
        grid=(bsz, nq),
        in_specs=[
            pl.BlockSpec((A_QUERIES, A_WIDTH), qrow),
            pl.BlockSpec((A_QUERIES, A_WIDTH), qrow),
            pl.BlockSpec((seq, 2 * KV_WIDTH), brow),
            pl.BlockSpec((seq, LANES), brow),
        ],
        out_specs=pl.BlockSpec((A_QUERIES, A_WIDTH), qrow),
        out_shape=jax.ShapeDtypeStruct((bsz * seq, A_WIDTH), BF16),
        scratch_shapes=[
            pltpu.VMEM((seq, A_QUERIES), F32),
            pltpu.VMEM((seq, A_QUERIES), BF16),
            pltpu.VMEM((seq // KEY_BLOCK, KV_WIDTH, KEY_BLOCK), BF16),
            pltpu.VMEM((IDX_DIM, IDX_HEADS * A_QUERIES), BF16),
            pltpu.VMEM((A_HEAD_DIM, A_HEADS * A_QUERIES), BF16),
            pltpu.VMEM((KV_WIDTH, (A_HEADS // A_KV_HEADS) * A_QUERIES), F32),
            pltpu.VMEM((KEY_BLOCK, A_QUERIES), F32),
        ],
        compiler_params=pltpu.CompilerParams(
            dimension_semantics=("arbitrary", "arbitrary"), vmem_limit_bytes=VMEM_LIMIT),
        name="mixer_a",
    )(qa, qi, kv, sm)


def _gdn_body(gdn_ref, sm_ref, alog_ref, dtb_ref, ng_ref, o_ref,
              gate_ref, gct_ref, state_ref, oc_ref, lhs_ref, bm_ref, *, tc):
    tb = pl.program_id(1)
    n_chunks = tc // CHUNK
    prep_unroll = 8
    q_col, k_col, v_col, z_col = (i * B_WIDTH for i in range(4))

    @pl.when(tb == 0)
    def _():
        state_ref[...] = jnp.zeros_like(state_ref)

    sm = sm_ref[...]
    beta = jax.nn.sigmoid(sm)
    z = sm + dtb_ref[...]
    softplus = jnp.maximum(z, 0.0) + jnp.log(1.0 + jnp.exp(-jnp.abs(z)))
    g = -jnp.exp(alog_ref[...]) * softplus
    rin = lax.broadcasted_iota(I32, (tc, LANES), 0) & (CHUNK - 1)
    gc = g
    step = 1
    while step < CHUNK:
        gc = gc + jnp.where(rin >= step, pltpu.roll(gc, step, 0), 0.0)
        step *= 2
    gc3 = gc.reshape(n_chunks, CHUNK, LANES)
    g_last = jnp.broadcast_to(gc3[:, CHUNK - 1:CHUNK, :], gc3.shape).reshape(tc, LANES)
    gate_ref[0] = beta
    gate_ref[1] = gc
    gate_ref[2] = jnp.exp(gc)
    gate_ref[3] = jnp.exp(g_last - gc)
    gate_ref[4] = jnp.exp(g_last)
    for i in range(tc // LANES):
        t = gc[i * LANES:(i + 1) * LANES, :].T
        for half in range(LANES // CHUNK):
            gct_ref[i * (LANES // CHUNK) + half] = t[:, half * CHUNK:(half + 1) * CHUNK]

    ci = lax.broadcasted_iota(I32, (CHUNK, CHUNK), 0)
    si = lax.broadcasted_iota(I32, (CHUNK, CHUNK), 1)
    wl = lax.broadcasted_iota(I32, (CHUNK, 2 * CHUNK), 1)
    wr = lax.broadcasted_iota(I32, (CHUNK, 2 * CHUNK), 0)
    right = wl >= CHUNK
    eye_right = jnp.where(wl == wr + CHUNK, 1.0, 0.0)

    def prep_body(cg, carry):
        units = [(cg * prep_unroll + cc, h) for cc in range(prep_unroll) for h in range(B_HEADS)]
        rows = [pl.ds(pl.multiple_of(c * CHUNK, CHUNK), CHUNK) for c, _ in units]
        hsl = [slice(h * LANES, (h + 1) * LANES) for _, h in units]
        idx = range(len(units))

        def col(i, gate, off):
            h = units[i][1]
            return gate_ref[gate, rows[i], off + h:off + h + 1]

        def seg(i, col0):
            h = units[i][1]
            return gdn_ref[rows[i], col0 + h * LANES:col0 + (h + 1) * LANES]

        q = [seg(i, q_col) for i in idx]
        k = [seg(i, k_col) for i in idx]
        kb = [k[i] * col(i, 0, SM_BETA) for i in idx]
        kq = [_dot_nt(jnp.concatenate([kb[i], q[i]], axis=0).astype(BF16), k[i].astype(BF16))
              for i in idx]
        decay = []
        for i, (c, h) in enumerate(units):
            d = col(i, 1, SM_DECAY) - gct_ref[c][SM_DECAY + h:SM_DECAY + h + 1, :]
            decay.append(jnp.where(ci >= si, jnp.exp(jnp.where(ci >= si, d, 0.0)), 0.0))
        wmat = []
        for i, (c, h) in enumerate(units):
            n_mat = jnp.where(ci > si, -(kq[i][0:CHUNK] * decay[i]), 0.0)
            wmat.append(jnp.concatenate([n_mat, jnp.zeros_like(n_mat)], axis=1) + eye_right)
        pw = 1
        while pw < CHUNK:
            wb = [wmat[i].astype(BF16) for i in idx]
            wmat = [_dot(wb[i][:, 0:CHUNK], wb[i]) + jnp.where(right, wmat[i], 0.0) for i in idx]
            pw *= 2
        eg = [col(i, 2, SM_DECAY) for i in idx]
        rhs = [jnp.concatenate([seg(i, v_col) * col(i, 0, SM_BETA), kb[i] * eg[i]],
                               axis=1).astype(BF16) for i in idx]
        sol = [_dot(wmat[i][:, CHUNK:].astype(BF16), rhs[i]).astype(BF16) for i in idx]
        att = [(kq[i][CHUNK:] * decay[i]).astype(BF16) for i in idx]
        k_tail_t = [(k[i] * col(i, 3, SM_DECAY)).T.astype(BF16) for i in idx]
        a_uw = [_dot(att[i], sol[i]) for i in idx]
        k_uw = [_dot(k_tail_t[i], sol[i]) for i in idx]
        for i, (c, h) in enumerate(units):
            oc_ref[rows[i], hsl[i]] = a_uw[i][:, 0:B_HEAD_DIM]
            lhs_ref[c, h, 0:CHUNK, :] = (q[i] * eg[i] - a_uw[i][:, B_HEAD_DIM:]).astype(BF16)
            lhs_ref[c, h, CHUNK:, :] = k_uw[i][:, B_HEAD_DIM:].astype(BF16)
            bm_ref[c, h] = k_uw[i][:, 0:B_HEAD_DIM]
        return carry

    lax.fori_loop(0, n_chunks // prep_unroll, prep_body, 0)

    def scan_body(c, carry):
        r0 = pl.multiple_of(c * CHUNK, CHUNK)
        rows = pl.ds(r0, CHUNK)
        heads = range(B_HEADS)
        hsl = [slice(h * LANES, (h + 1) * LANES) for h in heads]
        s_prev = [state_ref[h] for h in heads]
        r = [_dot(lhs_ref[c, h], s_prev[h].astype(BF16)) for h in heads]
        for h in heads:
            gl = gate_ref[4, rows, SM_DECAY + h:SM_DECAY + h + 1][0:1, :]
            oc_ref[rows, hsl[h]] = oc_ref[rows, hsl[h]] + r[h][0:CHUNK]
            state_ref[h] = s_prev[h] * gl + bm_ref[c, h] - r[h][CHUNK:]
        return carry

    lax.fori_loop(0, n_chunks, scan_body, 0)

    for h in range(B_HEADS):
        hs = slice(h * LANES, (h + 1) * LANES)
        z_gate = gdn_ref[:, z_col + h * LANES:z_col + (h + 1) * LANES]
        o_ref[:, hs] = (_rms(oc_ref[:, hs], ng_ref[...]) * z_gate).astype(BF16)


def _gdn(gdn_in, sm, alog_row, dtb_row, norm_g, bsz, seq, tc):
    nt = seq // tc
    trow = lambda b, t: (b * nt + t, 0)
    const = lambda b, t: (0, 0)
    n_chunks = tc // CHUNK
    return pl.pallas_call(
        functools.partial(_gdn_body, tc=tc),
        grid=(bsz, nt),
        in_specs=[
            pl.BlockSpec((tc, GDN_WIDTH), trow),
            pl.BlockSpec((tc, LANES), trow),
            pl.BlockSpec((1, LANES), const),
            pl.BlockSpec((1, LANES), const),
            pl.BlockSpec((1, B_HEAD_DIM), const),
        ],
        out_specs=pl.BlockSpec((tc, B_WIDTH), trow),
        out_shape=jax.ShapeDtypeStruct((bsz * seq, B_WIDTH), BF16),
        scratch_shapes=[
            pltpu.VMEM((5, tc, LANES), F32),
            pltpu.VMEM((n_chunks, LANES, CHUNK), F32),
            pltpu.VMEM((B_HEADS, B_HEAD_DIM, B_HEAD_DIM), F32),
            pltpu.VMEM((tc, B_WIDTH), F32),
            pltpu.VMEM((n_chunks, B_HEADS, CHUNK + B_HEAD_DIM, B_HEAD_DIM), BF16),
            pltpu.VMEM((n_chunks, B_HEADS, B_HEAD_DIM, B_HEAD_DIM), F32),
        ],
        compiler_params=pltpu.CompilerParams(
            dimension_semantics=("arbitrary", "arbitrary"), vmem_limit_bytes=VMEM_LIMIT),
        name="gdn",
    )(gdn_in, sm, alog_row, dtb_row, norm_g)


def _ffn_body(x_ref, oa_ref, ob_ref, wo_ref, g2_ref, w1_ref, w2_ref, g3_ref, out_ref, *,
              final_norm):
    y = (x_ref[...] + _dot(oa_ref[...], wo_ref[0:A_WIDTH, :])
         + _dot(ob_ref[...], wo_ref[A_WIDTH:, :]))
    h = _rms(y, g2_ref[...]).astype(BF16)
    a = jnp.square(jnp.maximum(_dot(h, w1_ref[...]), 0.0)).astype(BF16)
    acc = y + _dot(a, w2_ref[...])
    out_ref[...] = _rms(acc, g3_ref[...]) if final_norm else acc


def _ffn(x2, oa, ob, wo, g2, w1, w2, g3, tm, final_norm):
    m = x2.shape[0]
    row = lambda i: (i, 0)
    const = lambda i: (0, 0)
    resident = functools.partial(pl.BlockSpec, index_map=const, pipeline_mode=pl.Buffered(1))
    return pl.pallas_call(
        functools.partial(_ffn_body, final_norm=final_norm),
        grid=(m // tm,),
        in_specs=[
            pl.BlockSpec((tm, D_MODEL), row),
            pl.BlockSpec((tm, A_WIDTH), row),
            pl.BlockSpec((tm, B_WIDTH), row),
            resident((D_MODEL, D_MODEL)),
            pl.BlockSpec((1, D_MODEL), const),
            resident((D_MODEL, D_FF)),
            resident((D_FF, D_MODEL)),
            pl.BlockSpec((1, D_MODEL), const),
        ],
        out_specs=pl.BlockSpec((tm, D_MODEL), row),
        out_shape=jax.ShapeDtypeStruct((m, D_MODEL), F32),
        compiler_params=pltpu.CompilerParams(
            dimension_semantics=("arbitrary",), vmem_limit_bytes=VMEM_LIMIT),
        name="ffn",
    )(x2, oa, ob, wo, g2, w1, w2, g3)


def _rope_tables(seq):
    half = A_HEAD_DIM // 2
    inv_freq = 1.0 / (ROPE_THETA ** (jnp.arange(half, dtype=F32) / half))
    ang = jnp.arange(seq).astype(F32)[:, None] * inv_freq[None, :]
    cos = jnp.cos(ang)
    sin = jnp.sin(ang)
    reps = LANES // A_HEAD_DIM
    return (jnp.tile(cos, (1, 2 * reps)),
            jnp.tile(jnp.concatenate([-sin, sin], axis=1), (1, reps)))


def _permute_in_weight(w):
    sizes = (A_WIDTH, KV_WIDTH, KV_WIDTH, IDX_HEADS * IDX_DIM, IDX_DIM, IDX_HEADS,
             B_WIDTH, B_WIDTH, B_WIDTH, B_WIDTH, B_HEADS, B_HEADS)
    parts, off = [], 0
    for s in sizes:
        parts.append(w[:, off:off + s])
        off += s
    qa, ka, va, qi, ki, wi, qb, kb, vb, zb, bb, ab = parts
    pad = jnp.zeros((w.shape[0], LANES - (IDX_DIM + IDX_HEADS + 2 * B_HEADS)), w.dtype)
    return jnp.concatenate([qa, qi, ka, va, qb, kb, vb, zb, ki, wi, bb, ab, pad],
                           axis=1).astype(BF16)


def _lane_row(vals, offset):
    return jnp.zeros((1, LANES), F32).at[0, offset:offset + vals.shape[0]].set(vals.astype(F32))


def kernel(x, norm_mix_g, w_in, conv_w, a_log, dt_bias, gdn_norm_g, w_out,
           norm_ffn_g, w_ff1, w_ff2, norm_final_g):
    bsz, seq, d = x.shape
    depth = w_in.shape[0]
    m = bsz * seq
    cos_t, sin_t = _rope_tables(seq)
    x2 = x.reshape(m, d)
    for l in range(depth):
        qa, qi, kv, gdn_in, sm = _inproj(
            x2, norm_mix_g[l][None, :], _permute_in_weight(w_in[l]), cos_t, sin_t, conv_w[l],
            seq, tm=512)
        o_a = _mixer_a(qa, qi, kv, sm, bsz, seq)
        o_b = _gdn(gdn_in, sm, _lane_row(a_log[l], SM_DECAY),
                   _lane_row(dt_bias[l], SM_DECAY), gdn_norm_g[l][None, :], bsz, seq, tc=512)
        x2 = _ffn(x2, o_a, o_b, w_out[l].astype(BF16), norm_ffn_g[l][None, :],
                  w_ff1[l].astype(BF16), w_ff2[l].astype(BF16), norm_final_g[None, :],
                  tm=512, final_norm=(l == depth - 1))
    return x2.reshape(bsz, seq, d)
```

```python
import functools

import jax
import jax.numpy as jnp
from jax import lax
from jax.experimental import pallas as pl
from jax.experimental.pallas import tpu as pltpu

F32 = jnp.float32
BF16 = jnp.bfloat16
I32 = jnp.int32

D_MODEL = 1024
CHUNK = 64
A_QUERIES = 256
ROPE_THETA = 10000.0
EPS = 1e-6
A_HEADS = 8
A_KV_HEADS = 2
A_HEAD_DIM = 64
IDX_HEADS = 8
IDX_DIM = 64
TOPK_MAX = 256
B_HEADS = 4
B_HEAD_DIM = 128
CONV_WIDTH = 4
D_FF = 4 * D_MODEL

LANES = 128
A_WIDTH = A_HEADS * A_HEAD_DIM
KV_WIDTH = A_KV_HEADS * A_HEAD_DIM
B_WIDTH = B_HEADS * B_HEAD_DIM
GDN_WIDTH = 4 * B_WIDTH
SM_KI = 0
SM_WI = IDX_DIM
SM_BETA = SM_WI + IDX_HEADS
SM_DECAY = SM_BETA + B_HEADS
C_QA = 0
C_QI = C_QA + A_WIDTH
C_KV = C_QI + A_WIDTH
C_GDN = C_KV + 2 * KV_WIDTH
C_SM = C_GDN + GDN_WIDTH
IN_COLS = C_SM + LANES
REF_SIZES = (A_WIDTH, KV_WIDTH, KV_WIDTH, IDX_HEADS * IDX_DIM, IDX_DIM, IDX_HEADS,
             B_WIDTH, B_WIDTH, B_WIDTH, B_WIDTH, B_HEADS, B_HEADS)
REF_OFF = tuple(sum(REF_SIZES[:i]) for i in range(len(REF_SIZES)))
IN_DIM = sum(REF_SIZES)
IN_WEIGHT_MOVES = (
    (C_QA, REF_OFF[0], A_WIDTH),
    (C_QI, REF_OFF[3], A_WIDTH),
    (C_KV, REF_OFF[1], 2 * KV_WIDTH),
    (C_GDN, REF_OFF[6], GDN_WIDTH),
    (C_SM + SM_KI, REF_OFF[4], IDX_DIM + IDX_HEADS),
    (C_SM + SM_BETA, REF_OFF[10], 2 * B_HEADS),
)
IN_PAD = IN_COLS - (C_SM + SM_DECAY + B_HEADS)

VMEM_LIMIT = 56 * 1024 * 1024
CONV_PAD = 8
COUNT_GROUP = 4
COUNT_ROWS = 32
ATT_HEADS = 2
ATT_WAVE = 4
KEY_BLOCK = 2 * LANES
NEG_BIG = -1e30
LOG2_E = 1.4426950408889634
NEG_INF_KEY = -(2 ** 31) + 0x7FFFFF


def _rms(x, g):
    return x * lax.rsqrt(jnp.mean(x * x, axis=-1, keepdims=True) + EPS) * g


def _dot(a, b):
    return jnp.dot(a, b, preferred_element_type=F32)


def _dot_nt(a, b):
    return lax.dot_general(a, b, (((1,), (1,)), ((), ())), preferred_element_type=F32)


def _inproj_body(x_ref, g_ref, win_ref, cos_ref, sin_ref, cw_ref,
                 qa_ref, qi_ref, kv_ref, gdn_ref, sm_ref, xpad_ref, w_ref, *, tiles_per_seq):
    @pl.when(pl.program_id(0) == 0)
    def _():
        rows_per_step = 128
        for r0 in range(0, D_MODEL, rows_per_step):
            rs = slice(r0, r0 + rows_per_step)
            for dst, src, width in IN_WEIGHT_MOVES:
                w_ref[rs, dst:dst + width] = win_ref[rs, src:src + width].astype(BF16)
            w_ref[rs, IN_COLS - IN_PAD:IN_COLS] = jnp.zeros((rows_per_step, IN_PAD), BF16)

    h = _rms(x_ref[...], g_ref[...]).astype(BF16)
    cos = cos_ref[...]
    sin = sin_ref[...]
    lane = lax.broadcasted_iota(I32, cos.shape, 1)
    first_half = (lane & (A_HEAD_DIM - 1)) < A_HEAD_DIM // 2

    def rope(t):
        swapped = jnp.where(first_half, pltpu.roll(t, LANES - A_HEAD_DIM // 2, 1),
                            pltpu.roll(t, A_HEAD_DIM // 2, 1))
        return t * cos + swapped * sin

    def proj(c0, width):
        return _dot(h, w_ref[:, c0:c0 + width])

    tm = x_ref.shape[0]
    conv_cols = 3 * B_WIDTH

    @pl.when(pl.program_id(0) % tiles_per_seq == 0)
    def _():
        xpad_ref[0:CONV_PAD, :] = jnp.zeros((CONV_PAD, conv_cols), F32)

    @pl.when(pl.program_id(0) % tiles_per_seq != 0)
    def _():
        xpad_ref[0:CONV_PAD, :] = xpad_ref[tm:tm + CONV_PAD, :]

    def stage_conv_input(seg):
        xpad_ref[CONV_PAD:CONV_PAD + tm, seg * B_WIDTH:(seg + 1) * B_WIDTH] = proj(
            C_GDN + seg * B_WIDTH, B_WIDTH)

    def conv_segment(seg):
        for hh in range(B_HEADS):
            cs = slice(seg * B_WIDTH + hh * LANES, seg * B_WIDTH + (hh + 1) * LANES)
            xa = xpad_ref[:, cs]
            y = cw_ref[0:1, cs] * xa
            for jj in range(1, CONV_WIDTH):
                y = cw_ref[jj:jj + 1, cs] * xa + pltpu.roll(y, 1, 0)
            y = y[CONV_PAD:, :]
            y = y * jax.nn.sigmoid(y)
            if seg < 2:
                y = y * lax.rsqrt(jnp.sum(y * y, axis=-1, keepdims=True) + EPS)
            if seg == 0:
                y = y * (B_HEAD_DIM ** -0.5)
            gdn_ref[:, cs] = y

    stage_conv_input(0)
    acc = proj(C_QA, A_WIDTH)
    conv_segment(0)
    for j in range(A_WIDTH // LANES):
        sl = slice(j * LANES, (j + 1) * LANES)
        qa_ref[:, sl] = (rope(acc[:, sl]) * (A_HEAD_DIM ** -0.5 * LOG2_E)).astype(BF16)
    stage_conv_input(1)
    acc = proj(C_QI, A_WIDTH)
    conv_segment(1)
    for j in range(A_WIDTH // LANES):
        sl = slice(j * LANES, (j + 1) * LANES)
        qi_ref[:, sl] = rope(acc[:, sl]).astype(BF16)
    stage_conv_input(2)
    acc = proj(C_KV, 2 * KV_WIDTH)
    acc_sm = proj(C_SM, LANES)
    z = proj(C_GDN + conv_cols, B_WIDTH)
    conv_segment(2)
    kv_ref[:, 0:KV_WIDTH] = rope(acc[:, 0:KV_WIDTH]).astype(BF16)
    kv_ref[:, KV_WIDTH:] = acc[:, KV_WIDTH:].astype(BF16)
    sm_ref[...] = jnp.where(lane < IDX_DIM, rope(acc_sm), acc_sm)
    gdn_ref[:, conv_cols:] = z * jax.nn.sigmoid(z)


def _inproj(x2, g, w, cos_t, sin_t, conv_w, seq, tm):
    m = x2.shape[0]
    assert seq % tm == 0, (seq, tm)
    nt = seq // tm
    row = lambda i: (i, 0)
    const = lambda i: (0, 0)
    return pl.pallas_call(
        functools.partial(_inproj_body, tiles_per_seq=nt),
        grid=(m // tm,),
        in_specs=[
            pl.BlockSpec((tm, D_MODEL), row),
            pl.BlockSpec((1, D_MODEL), const),
            pl.BlockSpec((D_MODEL, IN_DIM), const, pipeline_mode=pl.Buffered(1)),
            pl.BlockSpec((tm, LANES), lambda i: (i % nt, 0)),
            pl.BlockSpec((tm, LANES), lambda i: (i % nt, 0)),
            pl.BlockSpec((CONV_WIDTH, 3 * B_WIDTH), const),
        ],
        out_specs=[
            pl.BlockSpec((tm, A_WIDTH), row),
            pl.BlockSpec((tm, A_WIDTH), row),
            pl.BlockSpec((tm, 2 * KV_WIDTH), row),
            pl.BlockSpec((tm, GDN_WIDTH), row),
            pl.BlockSpec((tm, LANES), row),
        ],
        out_shape=[
            jax.ShapeDtypeStruct((m, A_WIDTH), BF16),
            jax.ShapeDtypeStruct((m, A_WIDTH), BF16),
            jax.ShapeDtypeStruct((m, 2 * KV_WIDTH), BF16),
            jax.ShapeDtypeStruct((m, GDN_WIDTH), F32),
            jax.ShapeDtypeStruct((m, LANES), F32),
        ],
        scratch_shapes=[
            pltpu.VMEM((tm + CONV_PAD, 3 * B_WIDTH), F32),
            pltpu.VMEM((D_MODEL, IN_COLS), BF16),
        ],
        compiler_params=pltpu.CompilerParams(
            dimension_semantics=("arbitrary",), vmem_limit_bytes=VMEM_LIMIT),
        name="inproj",
    )(x2, g, w, cos_t, sin_t, conv_w)


def _mixer_a_body(qa_ref, qi_ref, kv_ref, sm_ref, o_ref,
                  isc_ref, isb_ref, vt_ref, qit_ref, qat_ref, acc_ref, bias_ref, *, seq, topk):
    j = pl.program_id(1)
    nkt = (j + 1) * (A_QUERIES // LANES)
    n_heads_pair = A_WIDTH // LANES
    group = A_HEADS // A_KV_HEADS

    @pl.when(j == 0)
    def _():
        per_block = KEY_BLOCK // LANES
        for t in range(seq // LANES):
            vt = kv_ref[t * LANES:(t + 1) * LANES, KV_WIDTH:].astype(F32)
            c0 = (t % per_block) * LANES
            vt_ref[t // per_block, :, c0:c0 + LANES] = vt.T.astype(BF16)

    for p in range(n_heads_pair):
        sl = slice(p * LANES, (p + 1) * LANES)
        t = qi_ref[:, sl].astype(F32).T
        qit_ref[:, (2 * p) * A_QUERIES:(2 * p + 1) * A_QUERIES] = t[0:IDX_DIM].astype(BF16)
        qit_ref[:, (2 * p + 1) * A_QUERIES:(2 * p + 2) * A_QUERIES] = t[IDX_DIM:].astype(BF16)
        t = qa_ref[:, sl].astype(F32).T
        qat_ref[:, (2 * p) * A_QUERIES:(2 * p + 1) * A_QUERIES] = t[0:A_HEAD_DIM].astype(BF16)
        qat_ref[:, (2 * p + 1) * A_QUERIES:(2 * p + 2) * A_QUERIES] = t[A_HEAD_DIM:].astype(BF16)

    q0 = pl.multiple_of(j * A_QUERIES, A_QUERIES)
    w_t = sm_ref[pl.ds(q0, A_QUERIES), :].T[SM_WI:SM_WI + IDX_HEADS, :]
    w_t = w_t * ((IDX_HEADS ** -0.5) * (IDX_DIM ** -0.5))

    qlane = lax.broadcasted_iota(I32, (1, A_QUERIES), 1)
    limit = q0 + (lax.shift_right_logical(qlane, CHUNK.bit_length() - 1) + 1) * CHUNK
    tile_iota = lax.broadcasted_iota(I32, (LANES, A_QUERIES), 0)
    block_iota = lax.broadcasted_iota(I32, (KEY_BLOCK, A_QUERIES), 0)
    nkb = lax.shift_right_logical(nkt + (KEY_BLOCK // LANES - 1), (KEY_BLOCK // LANES).bit_length() - 1)

    def isc_body(kb, carry):
        r0 = pl.multiple_of(kb * KEY_BLOCK, KEY_BLOCK)
        kid = sm_ref[pl.ds(r0, KEY_BLOCK), SM_KI:SM_KI + IDX_DIM].astype(BF16)
        rel = _dot(kid, qit_ref[...])
        acc = jnp.zeros((KEY_BLOCK, A_QUERIES), F32)
        for h in range(IDX_HEADS):
            acc = acc + w_t[h:h + 1, :] * jnp.maximum(rel[:, h * A_QUERIES:(h + 1) * A_QUERIES], 0.0)
        acc = jnp.where(r0 + block_iota < limit, acc, -jnp.inf)
        isc_ref[pl.ds(r0, KEY_BLOCK), :] = acc
        isb_ref[pl.ds(r0, KEY_BLOCK), :] = acc.astype(BF16)
        return carry

    lax.fori_loop(0, nkb, isc_body, 0)

    ngrp = lax.shift_right_logical(nkt + (COUNT_GROUP - 1), COUNT_GROUP.bit_length() - 1)

    def fill_body(kt, carry):
        r0 = pl.multiple_of(kt * LANES, LANES)
        isc_ref[pl.ds(r0, LANES), :] = jnp.full((LANES, A_QUERIES), -jnp.inf, F32)
        isb_ref[pl.ds(r0, LANES), :] = jnp.full((LANES, A_QUERIES), -jnp.inf, BF16)
        return carry

    lax.fori_loop(nkb * (KEY_BLOCK // LANES), ngrp * COUNT_GROUP, fill_body, 0)

    def count(pred):
        def body(g, acc):
            for t in range(COUNT_GROUP):
                r0 = pl.multiple_of((g * COUNT_GROUP + t) * LANES, LANES)
                hit = jnp.where(pred(isc_ref[pl.ds(r0, LANES), :], r0 + tile_iota), 1.0, 0.0)
                acc = acc + hit.reshape(LANES // COUNT_ROWS, COUNT_ROWS, A_QUERIES).sum(axis=0)
            return acc
        acc = lax.fori_loop(0, ngrp, body, jnp.zeros((COUNT_ROWS, A_QUERIES), F32))
        return acc.sum(axis=0, keepdims=True)

    def count_coarse(thr):
        def body(g, acc):
            for t in range(COUNT_GROUP):
                r0 = pl.multiple_of((g * COUNT_GROUP + t) * LANES, LANES)
                hit = jnp.where(isb_ref[pl.ds(r0, LANES), :] >= thr, one_b, zero_b)
                hit = hit.reshape(LANES // COUNT_ROWS, COUNT_ROWS, A_QUERIES)
                part = hit[0]
                for i in range(1, LANES // COUNT_ROWS):
                    part = part + hit[i]
                acc = acc + part
            return acc
        acc = lax.fori_loop(0, ngrp, body, jnp.zeros((COUNT_ROWS, A_QUERIES), BF16))
        return acc.astype(F32).sum(axis=0, keepdims=True)

    kf = float(topk)
    one_b = jnp.ones((), BF16)
    zero_b = jnp.zeros((), BF16)

    def thr_coarse(key):
        bits = jnp.where(key >= 0, key, key ^ jnp.int32(0x7FFF))
        return lax.bitcast_convert_type(lax.shift_left(bits, 16), F32).astype(BF16)

    c0 = count_coarse(jnp.zeros((1, A_QUERIES), BF16))
    lo16 = jnp.where(c0 >= kf, jnp.int32(0), jnp.int32(-2 ** 15))

    def coarse_body(i, lo):
        trial = lo | lax.shift_left(jnp.int32(1), 14 - i)
        c = count_coarse(thr_coarse(trial))
        return jnp.where(c >= kf, trial, lo)

    lo16 = lax.fori_loop(0, 15, coarse_body, lo16)
    lo16 = jnp.maximum(lo16, jnp.int32(NEG_INF_KEY >> 16))
    center = lax.shift_left(lo16, 16) | jnp.where(lo16 < 0, jnp.int32(0xFFFF), jnp.int32(0))

    def thr_of(key):
        bits = jnp.where(key >= 0, key, key ^ jnp.int32(0x7FFFFFFF))
        return jnp.where(key < jnp.int32(NEG_INF_KEY), -jnp.inf, lax.bitcast_convert_type(bits, F32))

    def fine_body(i, st):
        lo, c_lo = st
        trial = lo + lax.shift_left(jnp.int32(1), 16 - i)
        thr = thr_of(trial)
        c = count(lambda x, s: x >= thr)
        ok = c >= kf
        return jnp.where(ok, trial, lo), jnp.where(ok, c, c_lo)

    lo, c_lo = lax.fori_loop(0, 17, fine_body,
                             (center - jnp.int32(1 << 16), jnp.full((1, A_QUERIES), -1.0, F32)))
    kth = thr_of(lo)

    idx_bits = (seq - 1).bit_length()

    def tie_search():
        need = kf - count(lambda x, s: x > kth)

        def tie_body(i, last):
            trial = last | lax.shift_left(jnp.int32(1), idx_bits - 1 - i)
            c = count(lambda x, s: (x == kth) & (s < trial))
            return jnp.where(c < need, trial, last)

        return lax.fori_loop(0, idx_bits, tie_body, jnp.zeros((1, A_QUERIES), I32))

    def take_all_ties():
        return jnp.full((1, A_QUERIES), (1 << idx_bits) - 1, I32)

    n_ge = jnp.where(c_lo < 0.0, jnp.inf, c_lo)
    last = lax.cond(jnp.max(n_ge) > kf, tie_search, take_all_ties)

    acc_ref[...] = jnp.zeros_like(acc_ref)

    def att_body(kb, carry):
        m_prev, l_prev = carry
        r0 = pl.multiple_of(kb * KEY_BLOCK, KEY_BLOCK)
        x = isc_ref[pl.ds(r0, KEY_BLOCK), :]
        s_idx = r0 + block_iota
        sel = (s_idx < limit) & ((x > kth) | ((x == kth) & (s_idx <= last)))
        bias_ref[...] = jnp.where(sel, 0.0, NEG_BIG)
        k_tile = kv_ref[pl.ds(r0, KEY_BLOCK), 0:KV_WIDTH]
        v_t = vt_ref[kb]

        units = list(range(A_HEADS // ATT_HEADS))
        uw = ATT_HEADS * A_QUERIES
        us = [slice(u * uw, (u + 1) * uw) for u in units]
        ds = [slice((u * ATT_HEADS // group) * A_HEAD_DIM, (u * ATT_HEADS // group + 1) * A_HEAD_DIM)
              for u in units]
        cs = [slice((u * ATT_HEADS % group) * A_QUERIES, (u * ATT_HEADS % group + ATT_HEADS) * A_QUERIES)
              for u in units]
        m_new, l_new = {}, {}
        for w0 in range(0, len(units), ATT_WAVE):
            wave = units[w0:w0 + ATT_WAVE]
            s = {u: _dot(k_tile[:, ds[u]], qat_ref[:, us[u]])
                 + jnp.concatenate([bias_ref[...]] * ATT_HEADS, axis=1) for u in wave}
            for u in wave:
                m_new[u] = jnp.maximum(m_prev[:, us[u]], jnp.max(s[u], axis=0, keepdims=True))
            alpha = {u: jnp.exp2(m_prev[:, us[u]] - m_new[u]) for u in wave}
            p = {u: jnp.exp2(s[u] - m_new[u]) for u in wave}
            pv = {u: _dot(v_t[ds[u], :], p[u].astype(BF16)) for u in wave}
            for u in wave:
                l_new[u] = alpha[u] * l_prev[:, us[u]] + jnp.sum(p[u], axis=0, keepdims=True)
            for u in wave:
                acc_ref[ds[u], cs[u]] = alpha[u] * acc_ref[ds[u], cs[u]] + pv[u]
        return (jnp.concatenate([m_new[u] for u in units], axis=1),
                jnp.concatenate([l_new[u] for u in units], axis=1))

    m0 = jnp.full((1, A_HEADS * A_QUERIES), NEG_BIG, F32)
    l0 = jnp.zeros((1, A_HEADS * A_QUERIES), F32)
    _, l_fin = lax.fori_loop(0, nkb, att_body, (m0, l0))

    for p in range(n_heads_pair):
        g = (2 * p) // group
        ds = slice(g * A_HEAD_DIM, (g + 1) * A_HEAD_DIM)
        parts = []
        for h in (2 * p, 2 * p + 1):
            hl = h % group
            parts.append(acc_ref[ds, hl * A_QUERIES:(hl + 1) * A_QUERIES]
                         / l_fin[:, h * A_QUERIES:(h + 1) * A_QUERIES])
        o_ref[:, p * LANES:(p + 1) * LANES] = jnp.concatenate(parts, axis=0).T.astype(BF16)


def _mixer_a(qa, qi, kv, sm, bsz, seq):
    nq = seq // A_QUERIES
    assert seq % (COUNT_GROUP * LANES) == 0, seq
    assert seq // COUNT_ROWS <= 256, seq
    topk = min(TOPK_MAX, seq // 4)
    qrow = lambda b, j: (b * nq + j, 0)
    brow = lambda b, j: (b, 0)
    return pl.pallas_call(
        functools.partial(_mixer_a_body, seq=seq, topk=topk),
        grid=(bsz, nq),
        in_specs=[
            pl.BlockSpec((A_QUERIES, A_WIDTH), qrow),
            pl.BlockSpec((A_QUERIES, A_WIDTH), qrow),
            pl.BlockSpec((seq, 2 * KV_WIDTH), brow),
            pl.BlockSpec((seq, LANES), brow),
        ],
        out_specs=pl.BlockSpec((A_QUERIES, A_WIDTH), qrow),
        out_shape=jax.ShapeDtypeStruct((bsz * seq, A_WIDTH), BF16),
        scratch_shapes=[
            pltpu.VMEM((seq, A_QUERIES), F32),
            pltpu.VMEM((seq, A_QUERIES), BF16),
            pltpu.VMEM((seq // KEY_BLOCK, KV_WIDTH, KEY_BLOCK), BF16),
            pltpu.VMEM((IDX_DIM, IDX_HEADS * A_QUERIES), BF16),
            pltpu.VMEM((A_HEAD_DIM, A_HEADS * A_QUERIES), BF16),
            pltpu.VMEM((KV_WIDTH, (A_HEADS // A_KV_HEADS) * A_QUERIES), F32),
            pltpu.VMEM((KEY_BLOCK, A_QUERIES), F32),
        ],
        compiler_params=pltpu.CompilerParams(
            dimension_semantics=("arbitrary", "arbitrary"), vmem_limit_bytes=VMEM_LIMIT),
        name="mixer_a",
    )(qa, qi, kv, sm)


def _gdn_body(gdn_ref, sm_ref, alog_ref, dtb_ref, ng_ref, o_ref,
              gate_ref, gct_ref, state_ref, oc_ref, lhs_ref, bm_ref, *, tc):
    tb = pl.program_id(1)
    n_chunks = tc // CHUNK
    prep_unroll = 8
    q_col, k_col, v_col, z_col = (i * B_WIDTH for i in range(4))

    @pl.when(tb == 0)
    def _():
        state_ref[...] = jnp.zeros_like(state_ref)

    sm = sm_ref[...]
    beta = jax.nn.sigmoid(sm)
    z = sm + dtb_ref[...]
    softplus = jnp.maximum(z, 0.0) + jnp.log(1.0 + jnp.exp(-jnp.abs(z)))
    g = -jnp.exp(alog_ref[...]) * softplus
    rin = lax.broadcasted_iota(I32, (tc, LANES), 0) & (CHUNK - 1)
    gc = g
    step = 1
    while step < CHUNK:
        gc = gc + jnp.where(rin >= step, pltpu.roll(gc, step, 0), 0.0)
        step *= 2
    gc3 = gc.reshape(n_chunks, CHUNK, LANES)
    g_last = jnp.broadcast_to(gc3[:, CHUNK - 1:CHUNK, :], gc3.shape).reshape(tc, LANES)
    gate_ref[0] = beta
    gate_ref[1] = gc
    gate_ref[2] = jnp.exp(gc)
    gate_ref[3] = jnp.exp(g_last - gc)
    gate_ref[4] = jnp.exp(g_last)
    for i in range(tc // LANES):
        t = gc[i * LANES:(i + 1) * LANES, :].T
        for half in range(LANES // CHUNK):
            gct_ref[i * (LANES // CHUNK) + half] = t[:, half * CHUNK:(half + 1) * CHUNK]

    ci = lax.broadcasted_iota(I32, (CHUNK, CHUNK), 0)
    si = lax.broadcasted_iota(I32, (CHUNK, CHUNK), 1)
    wl = lax.broadcasted_iota(I32, (CHUNK, 2 * CHUNK), 1)
    wr = lax.broadcasted_iota(I32, (CHUNK, 2 * CHUNK), 0)
    right = wl >= CHUNK
    eye_right = jnp.where(wl == wr + CHUNK, 1.0, 0.0)

    def prep_body(cg, carry):
        units = [(cg * prep_unroll + cc, h) for cc in range(prep_unroll) for h in range(B_HEADS)]
        rows = [pl.ds(pl.multiple_of(c * CHUNK, CHUNK), CHUNK) for c, _ in units]
        hsl = [slice(h * LANES, (h + 1) * LANES) for _, h in units]
        idx = range(len(units))

        def col(i, gate, off):
            h = units[i][1]
            return gate_ref[gate, rows[i], off + h:off + h + 1]

        def seg(i, col0):
            h = units[i][1]
            return gdn_ref[rows[i], col0 + h * LANES:col0 + (h + 1) * LANES]

        q = [seg(i, q_col) for i in idx]
        k = [seg(i, k_col) for i in idx]
        kb = [k[i] * col(i, 0, SM_BETA) for i in idx]
        kq = [_dot_nt(jnp.concatenate([kb[i], q[i]], axis=0).astype(BF16), k[i].astype(BF16))
              for i in idx]
        decay = []
        for i, (c, h) in enumerate(units):
            d = col(i, 1, SM_DECAY) - gct_ref[c][SM_DECAY + h:SM_DECAY + h + 1, :]
            decay.append(jnp.where(ci >= si, jnp.exp(jnp.where(ci >= si, d, 0.0)), 0.0))
        wmat = []
        for i, (c, h) in enumerate(units):
            n_mat = jnp.where(ci > si, -(kq[i][0:CHUNK] * decay[i]), 0.0)
            wmat.append(jnp.concatenate([n_mat, jnp.zeros_like(n_mat)], axis=1) + eye_right)
        pw = 1
        while pw < CHUNK:
            wb = [wmat[i].astype(BF16) for i in idx]
            wmat = [_dot(wb[i][:, 0:CHUNK], wb[i]) + jnp.where(right, wmat[i], 0.0) for i in idx]
            pw *= 2
        eg = [col(i, 2, SM_DECAY) for i in idx]
        rhs = [jnp.concatenate([seg(i, v_col) * col(i, 0, SM_BETA), kb[i] * eg[i]],
                               axis=1).astype(BF16) for i in idx]
        sol = [_dot(wmat[i][:, CHUNK:].astype(BF16), rhs[i]).astype(BF16) for i in idx]
        att = [(kq[i][CHUNK:] * decay[i]).astype(BF16) for i in idx]
        k_tail_t = [(k[i] * col(i, 3, SM_DECAY)).T.astype(BF16) for i in idx]
        a_uw = [_dot(att[i], sol[i]) for i in idx]
        k_uw = [_dot(k_tail_t[i], sol[i]) for i in idx]
        for i, (c, h) in enumerate(units):
            oc_ref[rows[i], hsl[i]] = a_uw[i][:, 0:B_HEAD_DIM]
            lhs_ref[c, h, 0:CHUNK, :] = (q[i] * eg[i] - a_uw[i][:, B_HEAD_DIM:]).astype(BF16)
            lhs_ref[c, h, CHUNK:, :] = k_uw[i][:, B_HEAD_DIM:].astype(BF16)
            bm_ref[c, h] = k_uw[i][:, 0:B_HEAD_DIM]
        return carry

    lax.fori_loop(0, n_chunks // prep_unroll, prep_body, 0)

    def scan_body(c, carry):
        r0 = pl.multiple_of(c * CHUNK, CHUNK)
        rows = pl.ds(r0, CHUNK)
        heads = range(B_HEADS)
        hsl = [slice(h * LANES, (h + 1) * LANES) for h in heads]
        s_prev = [state_ref[h] for h in heads]
        r = [_dot(lhs_ref[c, h], s_prev[h].astype(BF16)) for h in heads]
        for h in heads:
            gl = gate_ref[4, rows, SM_DECAY + h:SM_DECAY + h + 1][0:1, :]
            oc_ref[rows, hsl[h]] = oc_ref[rows, hsl[h]] + r[h][0:CHUNK]
            state_ref[h] = s_prev[h] * gl + bm_ref[c, h] - r[h][CHUNK:]
        return carry

    lax.fori_loop(0, n_chunks, scan_body, 0)

    for h in range(B_HEADS):
        hs = slice(h * LANES, (h + 1) * LANES)
        z_gate = gdn_ref[:, z_col + h * LANES:z_col + (h + 1) * LANES]
        o_ref[:, hs] = (_rms(oc_ref[:, hs], ng_ref[...]) * z_gate).astype(BF16)


def _gdn(gdn_in, sm, alog_row, dtb_row, norm_g, bsz, seq, tc):
    nt = seq // tc
    trow = lambda b, t: (b * nt + t, 0)
    const = lambda b, t: (0, 0)
    n_chunks = tc // CHUNK
    return pl.pallas_call(
        functools.partial(_gdn_body, tc=tc),
        grid=(bsz, nt),
        in_specs=[
            pl.BlockSpec((tc, GDN_WIDTH), trow),
            pl.BlockSpec((tc, LANES), trow),
            pl.BlockSpec((1, LANES), const),
            pl.BlockSpec((1, LANES), const),
            pl.BlockSpec((1, B_HEAD_DIM), const),
        ],
        out_specs=pl.BlockSpec((tc, B_WIDTH), trow),
        out_shape=jax.ShapeDtypeStruct((bsz * seq, B_WIDTH), BF16),
        scratch_shapes=[
            pltpu.VMEM((5, tc, LANES), F32),
            pltpu.VMEM((n_chunks, LANES, CHUNK), F32),
            pltpu.VMEM((B_HEADS, B_HEAD_DIM, B_HEAD_DIM), F32),
            pltpu.VMEM((tc, B_WIDTH), F32),
            pltpu.VMEM((n_chunks, B_HEADS, CHUNK + B_HEAD_DIM, B_HEAD_DIM), BF16),
            pltpu.VMEM((n_chunks, B_HEADS, B_HEAD_DIM, B_HEAD_DIM), F32),
        ],
        compiler_params=pltpu.CompilerParams(
            dimension_semantics=("arbitrary", "arbitrary"), vmem_limit_bytes=VMEM_LIMIT),
        name="gdn",
    )(gdn_in, sm, alog_row, dtb_row, norm_g)


def _ffn_body(x_ref, oa_ref, ob_ref, wo_ref, g2_ref, w1_ref, w2_ref, g3_ref, out_ref, *,
              final_norm):
    y = (x_ref[...] + _dot(oa_ref[...], wo_ref[0:A_WIDTH, :])
         + _dot(ob_ref[...], wo_ref[A_WIDTH:, :]))
    h = _rms(y, g2_ref[...]).astype(BF16)
    a = jnp.square(jnp.maximum(_dot(h, w1_ref[...]), 0.0)).astype(BF16)
    acc = y + _dot(a, w2_ref[...])
    out_ref[...] = _rms(acc, g3_ref[...]) if final_norm else acc


def _ffn(x2, oa, ob, wo, g2, w1, w2, g3, tm, final_norm):
    m = x2.shape[0]
    row = lambda i: (i, 0)
    const = lambda i: (0, 0)
    resident = functools.partial(pl.BlockSpec, index_map=const, pipeline_mode=pl.Buffered(1))
    return pl.pallas_call(
        functools.partial(_ffn_body, final_norm=final_norm),
        grid=(m // tm,),
        in_specs=[
            pl.BlockSpec((tm, D_MODEL), row),
            pl.BlockSpec((tm, A_WIDTH), row),
            pl.BlockSpec((tm, B_WIDTH), row),
            resident((D_MODEL, D_MODEL)),
            pl.BlockSpec((1, D_MODEL), const),
            resident((D_MODEL, D_FF)),
            resident((D_FF, D_MODEL)),
            pl.BlockSpec((1, D_MODEL), const),
        ],
        out_specs=pl.BlockSpec((tm, D_MODEL), row),
        out_shape=jax.ShapeDtypeStruct((m, D_MODEL), F32),
        compiler_params=pltpu.CompilerParams(
            dimension_semantics=("arbitrary",), vmem_limit_bytes=VMEM_LIMIT),
        name="ffn",
    )(x2, oa, ob, wo, g2, w1, w2, g3)


def _rope_tables(seq):
    half = A_HEAD_DIM // 2
    inv_freq = 1.0 / (ROPE_THETA ** (jnp.arange(half, dtype=F32) / half))
    ang = jnp.arange(seq).astype(F32)[:, None] * inv_freq[None, :]
    cos = jnp.cos(ang)
    sin = jnp.sin(ang)
    reps = LANES // A_HEAD_DIM
    return (jnp.tile(cos, (1, 2 * reps)),
            jnp.tile(jnp.concatenate([-sin, sin], axis=1), (1, reps)))


def _lane_row(vals, offset):
    return jnp.zeros((1, LANES), F32).at[0, offset:offset + vals.shape[0]].set(vals.astype(F32))


def kernel(x, norm_mix_g, w_in, conv_w, a_log, dt_bias, gdn_norm_g, w_out,
           norm_ffn_g, w_ff1, w_ff2, norm_final_g):
    bsz, seq, d = x.shape
    depth = w_in.shape[0]
    m = bsz * seq
    cos_t, sin_t = _rope_tables(seq)
    x2 = x.reshape(m, d)
    for l in range(depth):
        qa, qi, kv, gdn_in, sm = _inproj(
            x2, norm_mix_g[l][None, :], w_in[l], cos_t, sin_t, conv_w[l],
            seq, tm=512)
        o_a = _mixer_a(qa, qi, kv, sm, bsz, seq)
        o_b = _gdn(gdn_in, sm, _lane_row(a_log[l], SM_DECAY),
                   _lane_row(dt_bias[l], SM_DECAY), gdn_norm_g[l][None, :], bsz, seq, tc=512)
        x2 = _ffn(x2, o_a, o_b, w_out[l].astype(BF16), norm_ffn_g[l][None, :],
                  w_ff1[l].astype(BF16), w_ff2[l].astype(BF16), norm_final_g[None, :],
                  tm=512, final_norm=(l == depth - 1))
    return x2.reshape(bsz, seq, d)
```

```python
import functools

import jax
import jax.numpy as jnp
from jax import lax
from jax.experimental import pallas as pl
from jax.experimental.pallas import tpu as pltpu

F32 = jnp.float32
BF16 = jnp.bfloat16
I32 = jnp.int32

D_MODEL = 1024
CHUNK = 64
A_QUERIES = 256
ROPE_THETA = 10000.0
EPS = 1e-6
A_HEADS = 8
A_KV_HEADS = 2
A_HEAD_DIM = 64
IDX_HEADS = 8
IDX_DIM = 64
TOPK_MAX = 256
B_HEADS = 4
B_HEAD_DIM = 128
CONV_WIDTH = 4
D_FF = 4 * D_MODEL

LANES = 128
A_WIDTH = A_HEADS * A_HEAD_DIM
KV_WIDTH = A_KV_HEADS * A_HEAD_DIM
B_WIDTH = B_HEADS * B_HEAD_DIM
GDN_WIDTH = 4 * B_WIDTH
SM_KI = 0
SM_WI = IDX_DIM
SM_BETA = SM_WI + IDX_HEADS
SM_DECAY = SM_BETA + B_HEADS
C_QA = 0
C_QI = C_QA + A_WIDTH
C_KV = C_QI + A_WIDTH
C_GDN = C_KV + 2 * KV_WIDTH
C_SM = C_GDN + GDN_WIDTH
IN_COLS = C_SM + LANES
REF_SIZES = (A_WIDTH, KV_WIDTH, KV_WIDTH, IDX_HEADS * IDX_DIM, IDX_DIM, IDX_HEADS,
             B_WIDTH, B_WIDTH, B_WIDTH, B_WIDTH, B_HEADS, B_HEADS)
REF_OFF = tuple(sum(REF_SIZES[:i]) for i in range(len(REF_SIZES)))
IN_DIM = sum(REF_SIZES)
IN_WEIGHT_MOVES = (
    (C_QA, REF_OFF[0], A_WIDTH),
    (C_QI, REF_OFF[3], A_WIDTH),
    (C_KV, REF_OFF[1], 2 * KV_WIDTH),
    (C_GDN, REF_OFF[6], GDN_WIDTH),
    (C_SM + SM_KI, REF_OFF[4], IDX_DIM + IDX_HEADS),
    (C_SM + SM_BETA, REF_OFF[10], 2 * B_HEADS),
)
IN_PAD = IN_COLS - (C_SM + SM_DECAY + B_HEADS)

VMEM_LIMIT = 56 * 1024 * 1024
CONV_PAD = 8
COUNT_GROUP = 4
COUNT_ROWS = 32
ATT_HEADS = 2
ATT_WAVE = 4
KEY_BLOCK = 2 * LANES
NEG_BIG = -1e30
LOG2_E = 1.4426950408889634
NEG_INF_KEY = -(2 ** 31) + 0x7FFFFF


def _rms(x, g):
    return x * lax.rsqrt(jnp.mean(x * x, axis=-1, keepdims=True) + EPS) * g


def _dot(a, b):
    return jnp.dot(a, b, preferred_element_type=F32)


def _dot_nt(a, b):
    return lax.dot_general(a, b, (((1,), (1,)), ((), ())), preferred_element_type=F32)


def _inproj_body(x_ref, g_ref, win_ref, cos_ref, sin_ref, cw_ref,
                 qa_ref, qi_ref, kv_ref, gdn_ref, sm_ref, xpad_ref, w_ref, *, tiles_per_seq):
    @pl.when(pl.program_id(0) == 0)
    def _():
        rows_per_step = 128
        for r0 in range(0, D_MODEL, rows_per_step):
            rs = slice(r0, r0 + rows_per_step)
            for dst, src, width in IN_WEIGHT_MOVES:
                w_ref[rs, dst:dst + width] = win_ref[rs, src:src + width].astype(BF16)
            w_ref[rs, IN_COLS - IN_PAD:IN_COLS] = jnp.zeros((rows_per_step, IN_PAD), BF16)

    h = _rms(x_ref[...], g_ref[...]).astype(BF16)
    cos = cos_ref[...]
    sin = sin_ref[...]
    lane = lax.broadcasted_iota(I32, cos.shape, 1)
    first_half = (lane & (A_HEAD_DIM - 1)) < A_HEAD_DIM // 2

    def rope(t):
        swapped = jnp.where(first_half, pltpu.roll(t, LANES - A_HEAD_DIM // 2, 1),
                            pltpu.roll(t, A_HEAD_DIM // 2, 1))
        return t * cos + swapped * sin

    def proj(c0, width):
        return _dot(h, w_ref[:, c0:c0 + width])

    tm = x_ref.shape[0]
    conv_cols = 3 * B_WIDTH

    @pl.when(pl.program_id(0) % tiles_per_seq == 0)
    def _():
        xpad_ref[0:CONV_PAD, :] = jnp.zeros((CONV_PAD, conv_cols), F32)

    @pl.when(pl.program_id(0) % tiles_per_seq != 0)
    def _():
        xpad_ref[0:CONV_PAD, :] = xpad_ref[tm:tm + CONV_PAD, :]

    def stage_conv_input(seg):
        xpad_ref[CONV_PAD:CONV_PAD + tm, seg * B_WIDTH:(seg + 1) * B_WIDTH] = proj(
            C_GDN + seg * B_WIDTH, B_WIDTH)

    def conv_segment(seg):
        for hh in range(B_HEADS):
            cs = slice(seg * B_WIDTH + hh * LANES, seg * B_WIDTH + (hh + 1) * LANES)
            xa = xpad_ref[:, cs]
            y = cw_ref[0:1, cs] * xa
            for jj in range(1, CONV_WIDTH):
                y = cw_ref[jj:jj + 1, cs] * xa + pltpu.roll(y, 1, 0)
            y = y[CONV_PAD:, :]
            y = y * jax.nn.sigmoid(y)
            if seg < 2:
                y = y * lax.rsqrt(jnp.sum(y * y, axis=-1, keepdims=True) + EPS)
            if seg == 0:
                y = y * (B_HEAD_DIM ** -0.5)
            gdn_ref[:, cs] = y

    stage_conv_input(0)
    acc = proj(C_QA, A_WIDTH)
    conv_segment(0)
    for j in range(A_WIDTH // LANES):
        sl = slice(j * LANES, (j + 1) * LANES)
        qa_ref[:, sl] = (rope(acc[:, sl]) * (A_HEAD_DIM ** -0.5 * LOG2_E)).astype(BF16)
    stage_conv_input(1)
    acc = proj(C_QI, A_WIDTH)
    conv_segment(1)
    for j in range(A_WIDTH // LANES):
        sl = slice(j * LANES, (j + 1) * LANES)
        qi_ref[:, sl] = rope(acc[:, sl]).astype(BF16)
    stage_conv_input(2)
    acc = proj(C_KV, 2 * KV_WIDTH)
    acc_sm = proj(C_SM, LANES)
    z = proj(C_GDN + conv_cols, B_WIDTH)
    conv_segment(2)
    kv_ref[:, 0:KV_WIDTH] = rope(acc[:, 0:KV_WIDTH]).astype(BF16)
    kv_ref[:, KV_WIDTH:] = acc[:, KV_WIDTH:].astype(BF16)
    sm_ref[...] = jnp.where(lane < IDX_DIM, rope(acc_sm), acc_sm)
    gdn_ref[:, conv_cols:] = z * jax.nn.sigmoid(z)


def _inproj(x2, g, w, layer, cos_t, sin_t, conv_w, seq, tm):
    m = x2.shape[0]
    assert seq % tm == 0, (seq, tm)
    nt = seq // tm
    row = lambda i: (i, 0)
    const = lambda i: (0, 0)
    return pl.pallas_call(
        functools.partial(_inproj_body, tiles_per_seq=nt),
        grid=(m // tm,),
        in_specs=[
            pl.BlockSpec((tm, D_MODEL), row),
            pl.BlockSpec((1, D_MODEL), const),
            pl.BlockSpec((None, D_MODEL, IN_DIM), lambda i: (layer, 0, 0),
                         pipeline_mode=pl.Buffered(1)),
            pl.BlockSpec((tm, LANES), lambda i: (i % nt, 0)),
            pl.BlockSpec((tm, LANES), lambda i: (i % nt, 0)),
            pl.BlockSpec((CONV_WIDTH, 3 * B_WIDTH), const),
        ],
        out_specs=[
            pl.BlockSpec((tm, A_WIDTH), row),
            pl.BlockSpec((tm, A_WIDTH), row),
            pl.BlockSpec((tm, 2 * KV_WIDTH), row),
            pl.BlockSpec((tm, GDN_WIDTH), row),
            pl.BlockSpec((tm, LANES), row),
        ],
        out_shape=[
            jax.ShapeDtypeStruct((m, A_WIDTH), BF16),
            jax.ShapeDtypeStruct((m, A_WIDTH), BF16),
            jax.ShapeDtypeStruct((m, 2 * KV_WIDTH), BF16),
            jax.ShapeDtypeStruct((m, GDN_WIDTH), F32),
            jax.ShapeDtypeStruct((m, LANES), F32),
        ],
        scratch_shapes=[
            pltpu.VMEM((tm + CONV_PAD, 3 * B_WIDTH), F32),
            pltpu.VMEM((D_MODEL, IN_COLS), BF16),
        ],
        compiler_params=pltpu.CompilerParams(
            dimension_semantics=("arbitrary",), vmem_limit_bytes=VMEM_LIMIT),
        name="inproj",
    )(x2, g, w, cos_t, sin_t, conv_w)


def _mixer_a_body(qa_ref, qi_ref, kv_ref, sm_ref, o_ref,
                  isc_ref, isb_ref, vt_ref, qit_ref, qat_ref, acc_ref, bias_ref, *, seq, topk):
    j = pl.program_id(1)
    nkt = (j + 1) * (A_QUERIES // LANES)
    n_heads_pair = A_WIDTH // LANES
    group = A_HEADS // A_KV_HEADS

    @pl.when(j == 0)
    def _():
        per_block = KEY_BLOCK // LANES
        for t in range(seq // LANES):
            vt = kv_ref[t * LANES:(t + 1) * LANES, KV_WIDTH:].astype(F32)
            c0 = (t % per_block) * LANES
            vt_ref[t // per_block, :, c0:c0 + LANES] = vt.T.astype(BF16)

    for p in range(n_heads_pair):
        sl = slice(p * LANES, (p + 1) * LANES)
        t = qi_ref[:, sl].astype(F32).T
        qit_ref[:, (2 * p) * A_QUERIES:(2 * p + 1) * A_QUERIES] = t[0:IDX_DIM].astype(BF16)
        qit_ref[:, (2 * p + 1) * A_QUERIES:(2 * p + 2) * A_QUERIES] = t[IDX_DIM:].astype(BF16)
        t = qa_ref[:, sl].astype(F32).T
        qat_ref[:, (2 * p) * A_QUERIES:(2 * p + 1) * A_QUERIES] = t[0:A_HEAD_DIM].astype(BF16)
        qat_ref[:, (2 * p + 1) * A_QUERIES:(2 * p + 2) * A_QUERIES] = t[A_HEAD_DIM:].astype(BF16)

    q0 = pl.multiple_of(j * A_QUERIES, A_QUERIES)
    w_t = sm_ref[pl.ds(q0, A_QUERIES), :].T[SM_WI:SM_WI + IDX_HEADS, :]
    w_t = w_t * ((IDX_HEADS ** -0.5) * (IDX_DIM ** -0.5))

    qlane = lax.broadcasted_iota(I32, (1, A_QUERIES), 1)
    limit = q0 + (lax.shift_right_logical(qlane, CHUNK.bit_length() - 1) + 1) * CHUNK
    tile_iota = lax.broadcasted_iota(I32, (LANES, A_QUERIES), 0)
    block_iota = lax.broadcasted_iota(I32, (KEY_BLOCK, A_QUERIES), 0)
    nkb = lax.shift_right_logical(nkt + (KEY_BLOCK // LANES - 1), (KEY_BLOCK // LANES).bit_length() - 1)

    def isc_body(kb, carry):
        r0 = pl.multiple_of(kb * KEY_BLOCK, KEY_BLOCK)
        kid = sm_ref[pl.ds(r0, KEY_BLOCK), SM_KI:SM_KI + IDX_DIM].astype(BF16)
        rel = _dot(kid, qit_ref[...])
        acc = jnp.zeros((KEY_BLOCK, A_QUERIES), F32)
        for h in range(IDX_HEADS):
            acc = acc + w_t[h:h + 1, :] * jnp.maximum(rel[:, h * A_QUERIES:(h + 1) * A_QUERIES], 0.0)
        acc = jnp.where(r0 + block_iota < limit, acc, -jnp.inf)
        isc_ref[pl.ds(r0, KEY_BLOCK), :] = acc
        isb_ref[pl.ds(r0, KEY_BLOCK), :] = acc.astype(BF16)
        return carry

    lax.fori_loop(0, nkb, isc_body, 0)

    ngrp = lax.shift_right_logical(nkt + (COUNT_GROUP - 1), COUNT_GROUP.bit_length() - 1)

    def fill_body(kt, carry):
        r0 = pl.multiple_of(kt * LANES, LANES)
        isc_ref[pl.ds(r0, LANES), :] = jnp.full((LANES, A_QUERIES), -jnp.inf, F32)
        isb_ref[pl.ds(r0, LANES), :] = jnp.full((LANES, A_QUERIES), -jnp.inf, BF16)
        return carry

    lax.fori_loop(nkb * (KEY_BLOCK // LANES), ngrp * COUNT_GROUP, fill_body, 0)

    def count(pred):
        def body(g, acc):
            for t in range(COUNT_GROUP):
                r0 = pl.multiple_of((g * COUNT_GROUP + t) * LANES, LANES)
                hit = jnp.where(pred(isc_ref[pl.ds(r0, LANES), :], r0 + tile_iota), 1.0, 0.0)
                acc = acc + hit.reshape(LANES // COUNT_ROWS, COUNT_ROWS, A_QUERIES).sum(axis=0)
            return acc
        acc = lax.fori_loop(0, ngrp, body, jnp.zeros((COUNT_ROWS, A_QUERIES), F32))
        return acc.sum(axis=0, keepdims=True)

    def count_coarse(thr):
        def body(g, acc):
            for t in range(COUNT_GROUP):
                r0 = pl.multiple_of((g * COUNT_GROUP + t) * LANES, LANES)
                hit = jnp.where(isb_ref[pl.ds(r0, LANES), :] >= thr, one_b, zero_b)
                hit = hit.reshape(LANES // COUNT_ROWS, COUNT_ROWS, A_QUERIES)
                part = hit[0]
                for i in range(1, LANES // COUNT_ROWS):
                    part = part + hit[i]
                acc = acc + part
            return acc
        acc = lax.fori_loop(0, ngrp, body, jnp.zeros((COUNT_ROWS, A_QUERIES), BF16))
        return acc.astype(F32).sum(axis=0, keepdims=True)

    kf = float(topk)
    one_b = jnp.ones((), BF16)
    zero_b = jnp.zeros((), BF16)

    def thr_coarse(key):
        bits = jnp.where(key >= 0, key, key ^ jnp.int32(0x7FFF))
        return lax.bitcast_convert_type(lax.shift_left(bits, 16), F32).astype(BF16)

    c0 = count_coarse(jnp.zeros((1, A_QUERIES), BF16))
    lo16 = jnp.where(c0 >= kf, jnp.int32(0), jnp.int32(-2 ** 15))

    def coarse_body(i, lo):
        trial = lo | lax.shift_left(jnp.int32(1), 14 - i)
        c = count_coarse(thr_coarse(trial))
        return jnp.where(c >= kf, trial, lo)

    lo16 = lax.fori_loop(0, 15, coarse_body, lo16)
    lo16 = jnp.maximum(lo16, jnp.int32(NEG_INF_KEY >> 16))
    center = lax.shift_left(lo16, 16) | jnp.where(lo16 < 0, jnp.int32(0xFFFF), jnp.int32(0))

    def thr_of(key):
        bits = jnp.where(key >= 0, key, key ^ jnp.int32(0x7FFFFFFF))
        return jnp.where(key < jnp.int32(NEG_INF_KEY), -jnp.inf, lax.bitcast_convert_type(bits, F32))

    def fine_body(i, st):
        lo, c_lo = st
        trial = lo + lax.shift_left(jnp.int32(1), 16 - i)
        thr = thr_of(trial)
        c = count(lambda x, s: x >= thr)
        ok = c >= kf
        return jnp.where(ok, trial, lo), jnp.where(ok, c, c_lo)

    lo, c_lo = lax.fori_loop(0, 17, fine_body,
                             (center - jnp.int32(1 << 16), jnp.full((1, A_QUERIES), -1.0, F32)))
    kth = thr_of(lo)

    idx_bits = (seq - 1).bit_length()

    def tie_search():
        need = kf - count(lambda x, s: x > kth)

        def tie_body(i, last):
            trial = last | lax.shift_left(jnp.int32(1), idx_bits - 1 - i)
            c = count(lambda x, s: (x == kth) & (s < trial))
            return jnp.where(c < need, trial, last)

        return lax.fori_loop(0, idx_bits, tie_body, jnp.zeros((1, A_QUERIES), I32))

    def take_all_ties():
        return jnp.full((1, A_QUERIES), (1 << idx_bits) - 1, I32)

    n_ge = jnp.where(c_lo < 0.0, jnp.inf, c_lo)
    last = lax.cond(jnp.max(n_ge) > kf, tie_search, take_all_ties)

    acc_ref[...] = jnp.zeros_like(acc_ref)

    def att_body(kb, carry):
        m_prev, l_prev = carry
        r0 = pl.multiple_of(kb * KEY_BLOCK, KEY_BLOCK)
        x = isc_ref[pl.ds(r0, KEY_BLOCK), :]
        s_idx = r0 + block_iota
        sel = (s_idx < limit) & ((x > kth) | ((x == kth) & (s_idx <= last)))
        bias_ref[...] = jnp.where(sel, 0.0, NEG_BIG)
        k_tile = kv_ref[pl.ds(r0, KEY_BLOCK), 0:KV_WIDTH]
        v_t = vt_ref[kb]

        units = list(range(A_HEADS // ATT_HEADS))
        uw = ATT_HEADS * A_QUERIES
        us = [slice(u * uw, (u + 1) * uw) for u in units]
        ds = [slice((u * ATT_HEADS // group) * A_HEAD_DIM, (u * ATT_HEADS // group + 1) * A_HEAD_DIM)
              for u in units]
        cs = [slice((u * ATT_HEADS % group) * A_QUERIES, (u * ATT_HEADS % group + ATT_HEADS) * A_QUERIES)
              for u in units]
        m_new, l_new = {}, {}
        for w0 in range(0, len(units), ATT_WAVE):
            wave = units[w0:w0 + ATT_WAVE]
            s = {u: _dot(k_tile[:, ds[u]], qat_ref[:, us[u]])
                 + jnp.concatenate([bias_ref[...]] * ATT_HEADS, axis=1) for u in wave}
            for u in wave:
                m_new[u] = jnp.maximum(m_prev[:, us[u]], jnp.max(s[u], axis=0, keepdims=True))
            alpha = {u: jnp.exp2(m_prev[:, us[u]] - m_new[u]) for u in wave}
            p = {u: jnp.exp2(s[u] - m_new[u]) for u in wave}
            pv = {u: _dot(v_t[ds[u], :], p[u].astype(BF16)) for u in wave}
            for u in wave:
                l_new[u] = alpha[u] * l_prev[:, us[u]] + jnp.sum(p[u], axis=0, keepdims=True)
            for u in wave:
                acc_ref[ds[u], cs[u]] = alpha[u] * acc_ref[ds[u], cs[u]] + pv[u]
        return (jnp.concatenate([m_new[u] for u in units], axis=1),
                jnp.concatenate([l_new[u] for u in units], axis=1))

    m0 = jnp.full((1, A_HEADS * A_QUERIES), NEG_BIG, F32)
    l0 = jnp.zeros((1, A_HEADS * A_QUERIES), F32)
    _, l_fin = lax.fori_loop(0, nkb, att_body, (m0, l0))

    for p in range(n_heads_pair):
        g = (2 * p) // group
        ds = slice(g * A_HEAD_DIM, (g + 1) * A_HEAD_DIM)
        parts = []
        for h in (2 * p, 2 * p + 1):
            hl = h % group
            parts.append(acc_ref[ds, hl * A_QUERIES:(hl + 1) * A_QUERIES]
                         / l_fin[:, h * A_QUERIES:(h + 1) * A_QUERIES])
        o_ref[:, p * LANES:(p + 1) * LANES] = jnp.concatenate(parts, axis=0).T.astype(BF16)


def _mixer_a(qa, qi, kv, sm, bsz, seq):
    nq = seq // A_QUERIES
    assert seq % (COUNT_GROUP * LANES) == 0, seq
    assert seq // COUNT_ROWS <= 256, seq
    topk = min(TOPK_MAX, seq // 4)
    qrow = lambda b, j: (b * nq + j, 0)
    brow = lambda b, j: (b, 0)
    return pl.pallas_call(
        functools.partial(_mixer_a_body, seq=seq, topk=topk),
        grid=(bsz, nq),
        in_specs=[
            pl.BlockSpec((A_QUERIES, A_WIDTH), qrow),
            pl.BlockSpec((A_QUERIES, A_WIDTH), qrow),
            pl.BlockSpec((seq, 2 * KV_WIDTH), brow),
            pl.BlockSpec((seq, LANES), brow),
        ],
        out_specs=pl.BlockSpec((A_QUERIES, A_WIDTH), qrow),
        out_shape=jax.ShapeDtypeStruct((bsz * seq, A_WIDTH), BF16),
        scratch_shapes=[
            pltpu.VMEM((seq, A_QUERIES), F32),
            pltpu.VMEM((seq, A_QUERIES), BF16),
            pltpu.VMEM((seq // KEY_BLOCK, KV_WIDTH, KEY_BLOCK), BF16),
            pltpu.VMEM((IDX_DIM, IDX_HEADS * A_QUERIES), BF16),
            pltpu.VMEM((A_HEAD_DIM, A_HEADS * A_QUERIES), BF16),
            pltpu.VMEM((KV_WIDTH, (A_HEADS // A_KV_HEADS) * A_QUERIES), F32),
            pltpu.VMEM((KEY_BLOCK, A_QUERIES), F32),
        ],
        compiler_params=pltpu.CompilerParams(
            dimension_semantics=("arbitrary", "arbitrary"), vmem_limit_bytes=VMEM_LIMIT),
        name="mixer_a",
    )(qa, qi, kv, sm)


def _gdn_body(gdn_ref, sm_ref, alog_ref, dtb_ref, ng_ref, o_ref,
              gate_ref, gct_ref, state_ref, oc_ref, lhs_ref, bm_ref, *, tc):
    tb = pl.program_id(1)
    n_chunks = tc // CHUNK
    prep_unroll = 8
    q_col, k_col, v_col, z_col = (i * B_WIDTH for i in range(4))

    @pl.when(tb == 0)
    def _():
        state_ref[...] = jnp.zeros_like(state_ref)

    sm = sm_ref[...]
    beta = jax.nn.sigmoid(sm)
    z = sm + dtb_ref[...]
    softplus = jnp.maximum(z, 0.0) + jnp.log(1.0 + jnp.exp(-jnp.abs(z)))
    g = -jnp.exp(alog_ref[...]) * softplus
    rin = lax.broadcasted_iota(I32, (tc, LANES), 0) & (CHUNK - 1)
    gc = g
    step = 1
    while step < CHUNK:
        gc = gc + jnp.where(rin >= step, pltpu.roll(gc, step, 0), 0.0)
        step *= 2
    gc3 = gc.reshape(n_chunks, CHUNK, LANES)
    g_last = jnp.broadcast_to(gc3[:, CHUNK - 1:CHUNK, :], gc3.shape).reshape(tc, LANES)
    gate_ref[0] = beta
    gate_ref[1] = gc
    gate_ref[2] = jnp.exp(gc)
    gate_ref[3] = jnp.exp(g_last - gc)
    gate_ref[4] = jnp.exp(g_last)
    for i in range(tc // LANES):
        t = gc[i * LANES:(i + 1) * LANES, :].T
        for half in range(LANES // CHUNK):
            gct_ref[i * (LANES // CHUNK) + half] = t[:, half * CHUNK:(half + 1) * CHUNK]

    ci = lax.broadcasted_iota(I32, (CHUNK, CHUNK), 0)
    si = lax.broadcasted_iota(I32, (CHUNK, CHUNK), 1)
    wl = lax.broadcasted_iota(I32, (CHUNK, 2 * CHUNK), 1)
    wr = lax.broadcasted_iota(I32, (CHUNK, 2 * CHUNK), 0)
    right = wl >= CHUNK
    eye_right = jnp.where(wl == wr + CHUNK, 1.0, 0.0)

    def prep_body(cg, carry):
        units = [(cg * prep_unroll + cc, h) for cc in range(prep_unroll) for h in range(B_HEADS)]
        rows = [pl.ds(pl.multiple_of(c * CHUNK, CHUNK), CHUNK) for c, _ in units]
        hsl = [slice(h * LANES, (h + 1) * LANES) for _, h in units]
        idx = range(len(units))

        def col(i, gate, off):
            h = units[i][1]
            return gate_ref[gate, rows[i], off + h:off + h + 1]

        def seg(i, col0):
            h = units[i][1]
            return gdn_ref[rows[i], col0 + h * LANES:col0 + (h + 1) * LANES]

        q = [seg(i, q_col) for i in idx]
        k = [seg(i, k_col) for i in idx]
        kb = [k[i] * col(i, 0, SM_BETA) for i in idx]
        kq = [_dot_nt(jnp.concatenate([kb[i], q[i]], axis=0).astype(BF16), k[i].astype(BF16))
              for i in idx]
        decay = []
        for i, (c, h) in enumerate(units):
            d = col(i, 1, SM_DECAY) - gct_ref[c][SM_DECAY + h:SM_DECAY + h + 1, :]
            decay.append(jnp.where(ci >= si, jnp.exp(jnp.where(ci >= si, d, 0.0)), 0.0))
        wmat = []
        for i, (c, h) in enumerate(units):
            n_mat = jnp.where(ci > si, -(kq[i][0:CHUNK] * decay[i]), 0.0)
            wmat.append(jnp.concatenate([n_mat, jnp.zeros_like(n_mat)], axis=1) + eye_right)
        pw = 1
        while pw < CHUNK:
            wb = [wmat[i].astype(BF16) for i in idx]
            wmat = [_dot(wb[i][:, 0:CHUNK], wb[i]) + jnp.where(right, wmat[i], 0.0) for i in idx]
            pw *= 2
        eg = [col(i, 2, SM_DECAY) for i in idx]
        rhs = [jnp.concatenate([seg(i, v_col) * col(i, 0, SM_BETA), kb[i] * eg[i]],
                               axis=1).astype(BF16) for i in idx]
        sol = [_dot(wmat[i][:, CHUNK:].astype(BF16), rhs[i]).astype(BF16) for i in idx]
        att = [(kq[i][CHUNK:] * decay[i]).astype(BF16) for i in idx]
        k_tail_t = [(k[i] * col(i, 3, SM_DECAY)).T.astype(BF16) for i in idx]
        a_uw = [_dot(att[i], sol[i]) for i in idx]
        k_uw = [_dot(k_tail_t[i], sol[i]) for i in idx]
        for i, (c, h) in enumerate(units):
            oc_ref[rows[i], hsl[i]] = a_uw[i][:, 0:B_HEAD_DIM]
            lhs_ref[c, h, 0:CHUNK, :] = (q[i] * eg[i] - a_uw[i][:, B_HEAD_DIM:]).astype(BF16)
            lhs_ref[c, h, CHUNK:, :] = k_uw[i][:, B_HEAD_DIM:].astype(BF16)
            bm_ref[c, h] = k_uw[i][:, 0:B_HEAD_DIM]
        return carry

    lax.fori_loop(0, n_chunks // prep_unroll, prep_body, 0)

    def scan_body(c, carry):
        r0 = pl.multiple_of(c * CHUNK, CHUNK)
        rows = pl.ds(r0, CHUNK)
        heads = range(B_HEADS)
        hsl = [slice(h * LANES, (h + 1) * LANES) for h in heads]
        s_prev = [state_ref[h] for h in heads]
        r = [_dot(lhs_ref[c, h], s_prev[h].astype(BF16)) for h in heads]
        for h in heads:
            gl = gate_ref[4, rows, SM_DECAY + h:SM_DECAY + h + 1][0:1, :]
            oc_ref[rows, hsl[h]] = oc_ref[rows, hsl[h]] + r[h][0:CHUNK]
            state_ref[h] = s_prev[h] * gl + bm_ref[c, h] - r[h][CHUNK:]
        return carry

    lax.fori_loop(0, n_chunks, scan_body, 0)

    for h in range(B_HEADS):
        hs = slice(h * LANES, (h + 1) * LANES)
        z_gate = gdn_ref[:, z_col + h * LANES:z_col + (h + 1) * LANES]
        o_ref[:, hs] = (_rms(oc_ref[:, hs], ng_ref[...]) * z_gate).astype(BF16)


def _gdn(gdn_in, sm, alog_row, dtb_row, norm_g, bsz, seq, tc):
    nt = seq // tc
    trow = lambda b, t: (b * nt + t, 0)
    const = lambda b, t: (0, 0)
    n_chunks = tc // CHUNK
    return pl.pallas_call(
        functools.partial(_gdn_body, tc=tc),
        grid=(bsz, nt),
        in_specs=[
            pl.BlockSpec((tc, GDN_WIDTH), trow),
            pl.BlockSpec((tc, LANES), trow),
            pl.BlockSpec((1, LANES), const),
            pl.BlockSpec((1, LANES), const),
            pl.BlockSpec((1, B_HEAD_DIM), const),
        ],
        out_specs=pl.BlockSpec((tc, B_WIDTH), trow),
        out_shape=jax.ShapeDtypeStruct((bsz * seq, B_WIDTH), BF16),
        scratch_shapes=[
            pltpu.VMEM((5, tc, LANES), F32),
            pltpu.VMEM((n_chunks, LANES, CHUNK), F32),
            pltpu.VMEM((B_HEADS, B_HEAD_DIM, B_HEAD_DIM), F32),
            pltpu.VMEM((tc, B_WIDTH), F32),
            pltpu.VMEM((n_chunks, B_HEADS, CHUNK + B_HEAD_DIM, B_HEAD_DIM), BF16),
            pltpu.VMEM((n_chunks, B_HEADS, B_HEAD_DIM, B_HEAD_DIM), F32),
        ],
        compiler_params=pltpu.CompilerParams(
            dimension_semantics=("arbitrary", "arbitrary"), vmem_limit_bytes=VMEM_LIMIT),
        name="gdn",
    )(gdn_in, sm, alog_row, dtb_row, norm_g)


def _ffn_body(x_ref, oa_ref, ob_ref, wo_ref, g2_ref, w1_ref, w2_ref, g3_ref, out_ref, *,
              final_norm):
    y = (x_ref[...] + _dot(oa_ref[...], wo_ref[0:A_WIDTH, :])
         + _dot(ob_ref[...], wo_ref[A_WIDTH:, :]))
    h = _rms(y, g2_ref[...]).astype(BF16)
    a = jnp.square(jnp.maximum(_dot(h, w1_ref[...]), 0.0)).astype(BF16)
    acc = y + _dot(a, w2_ref[...])
    out_ref[...] = _rms(acc, g3_ref[...]) if final_norm else acc


def _ffn(x2, oa, ob, wo, g2, w1, w2, g3, tm, final_norm):
    m = x2.shape[0]
    row = lambda i: (i, 0)
    const = lambda i: (0, 0)
    resident = functools.partial(pl.BlockSpec, index_map=const, pipeline_mode=pl.Buffered(1))
    return pl.pallas_call(
        functools.partial(_ffn_body, final_norm=final_norm),
        grid=(m // tm,),
        in_specs=[
            pl.BlockSpec((tm, D_MODEL), row),
            pl.BlockSpec((tm, A_WIDTH), row),
            pl.BlockSpec((tm, B_WIDTH), row),
            resident((D_MODEL, D_MODEL)),
            pl.BlockSpec((1, D_MODEL), const),
            resident((D_MODEL, D_FF)),
            resident((D_FF, D_MODEL)),
            pl.BlockSpec((1, D_MODEL), const),
        ],
        out_specs=pl.BlockSpec((tm, D_MODEL), row),
        out_shape=jax.ShapeDtypeStruct((m, D_MODEL), F32),
        compiler_params=pltpu.CompilerParams(
            dimension_semantics=("arbitrary",), vmem_limit_bytes=VMEM_LIMIT),
        name="ffn",
    )(x2, oa, ob, wo, g2, w1, w2, g3)


def _rope_tables(seq):
    half = A_HEAD_DIM // 2
    inv_freq = 1.0 / (ROPE_THETA ** (jnp.arange(half, dtype=F32) / half))
    ang = jnp.arange(seq).astype(F32)[:, None] * inv_freq[None, :]
    cos = jnp.cos(ang)
    sin = jnp.sin(ang)
    reps = LANES // A_HEAD_DIM
    return (jnp.tile(cos, (1, 2 * reps)),
            jnp.tile(jnp.concatenate([-sin, sin], axis=1), (1, reps)))


def _lane_row(vals, offset):
    return jnp.zeros((1, LANES), F32).at[0, offset:offset + vals.shape[0]].set(vals.astype(F32))


def kernel(x, norm_mix_g, w_in, conv_w, a_log, dt_bias, gdn_norm_g, w_out,
           norm_ffn_g, w_ff1, w_ff2, norm_final_g):
    bsz, seq, d = x.shape
    depth = w_in.shape[0]
    m = bsz * seq
    cos_t, sin_t = _rope_tables(seq)
    x2 = x.reshape(m, d)
    for l in range(depth):
        qa, qi, kv, gdn_in, sm = _inproj(
            x2, norm_mix_g[l][None, :], w_in, l, cos_t, sin_t, conv_w[l],
            seq, tm=512)
        o_a = _mixer_a(qa, qi, kv, sm, bsz, seq)
        o_b = _gdn(gdn_in, sm, _lane_row(a_log[l], SM_DECAY),
                   _lane_row(dt_bias[l], SM_DECAY), gdn_norm_g[l][None, :], bsz, seq, tc=512)
        x2 = _ffn(x2, o_a, o_b, w_out[l].astype(BF16), norm_ffn_g[l][None, :],
                  w_ff1[l].astype(BF16), w_ff2[l].astype(BF16), norm_final_g[None, :],
                  tm=512, final_norm=(l == depth - 1))
    return x2.reshape(bsz, seq, d)
```

```python
import functools

import jax
import jax.numpy as jnp
from jax import lax
from jax.experimental import pallas as pl
from jax.experimental.pallas import tpu as pltpu

F32 = jnp.float32
BF16 = jnp.bfloat16
I32 = jnp.int32

D_MODEL = 1024
CHUNK = 64
A_QUERIES = 256
ROPE_THETA = 10000.0
EPS = 1e-6
A_HEADS = 8
A_KV_HEADS = 2
A_HEAD_DIM = 64
IDX_HEADS = 8
IDX_DIM = 64
TOPK_MAX = 256
B_HEADS = 4
B_HEAD_DIM = 128
CONV_WIDTH = 4
D_FF = 4 * D_MODEL

LANES = 128
A_WIDTH = A_HEADS * A_HEAD_DIM
KV_WIDTH = A_KV_HEADS * A_HEAD_DIM
B_WIDTH = B_HEADS * B_HEAD_DIM
GDN_WIDTH = 4 * B_WIDTH
SM_KI = 0
SM_WI = IDX_DIM
SM_BETA = SM_WI + IDX_HEADS
SM_DECAY = SM_BETA + B_HEADS
C_QA = 0
C_QI = C_QA + A_WIDTH
C_KV = C_QI + A_WIDTH
C_GDN = C_KV + 2 * KV_WIDTH
C_SM = C_GDN + GDN_WIDTH
IN_COLS = C_SM + LANES
REF_SIZES = (A_WIDTH, KV_WIDTH, KV_WIDTH, IDX_HEADS * IDX_DIM, IDX_DIM, IDX_HEADS,
             B_WIDTH, B_WIDTH, B_WIDTH, B_WIDTH, B_HEADS, B_HEADS)
REF_OFF = tuple(sum(REF_SIZES[:i]) for i in range(len(REF_SIZES)))
IN_DIM = sum(REF_SIZES)
IN_WEIGHT_MOVES = (
    (C_QA, REF_OFF[0], A_WIDTH),
    (C_QI, REF_OFF[3], A_WIDTH),
    (C_KV, REF_OFF[1], 2 * KV_WIDTH),
    (C_GDN, REF_OFF[6], GDN_WIDTH),
    (C_SM + SM_KI, REF_OFF[4], IDX_DIM + IDX_HEADS),
    (C_SM + SM_BETA, REF_OFF[10], 2 * B_HEADS),
)
IN_PAD = IN_COLS - (C_SM + SM_DECAY + B_HEADS)

VMEM_LIMIT = 56 * 1024 * 1024
CONV_PAD = 8
COUNT_GROUP = 4
COUNT_ROWS = 32
ATT_HEADS = 2
ATT_WAVE = 4
KEY_BLOCK = 2 * LANES
NEG_BIG = -1e30
LOG2_E = 1.4426950408889634
NEG_INF_KEY = -(2 ** 31) + 0x7FFFFF


def _rms(x, g):
    return x * lax.rsqrt(jnp.mean(x * x, axis=-1, keepdims=True) + EPS) * g


def _dot(a, b):
    return jnp.dot(a, b, preferred_element_type=F32)


def _dot_nt(a, b):
    return lax.dot_general(a, b, (((1,), (1,)), ((), ())), preferred_element_type=F32)


def _inproj_body(x_ref, g_ref, win_ref, cos_ref, sin_ref, cw_ref,
                 qa_ref, qi_ref, kv_ref, gdn_ref, sm_ref, xpad_ref, w_ref, *, tiles_per_seq):
    @pl.when(pl.program_id(0) == 0)
    def _():
        step = 256
        for dst, src, width in IN_WEIGHT_MOVES[:-2]:
            for r0 in range(0, width, step):
                w_ref[dst + r0:dst + r0 + step, :] = win_ref[src + r0:src + r0 + step, :].astype(BF16)
        small = [win_ref[src:src + width, :] for _, src, width in IN_WEIGHT_MOVES[-2:]]
        small.append(jnp.zeros((IN_PAD, D_MODEL), F32))
        w_ref[C_SM:IN_COLS, :] = jnp.concatenate(small, axis=0).astype(BF16)

    h = _rms(x_ref[...], g_ref[...]).astype(BF16)
    cos = cos_ref[...]
    sin = sin_ref[...]
    lane = lax.broadcasted_iota(I32, cos.shape, 1)
    first_half = (lane & (A_HEAD_DIM - 1)) < A_HEAD_DIM // 2

    def rope(t):
        swapped = jnp.where(first_half, pltpu.roll(t, LANES - A_HEAD_DIM // 2, 1),
                            pltpu.roll(t, A_HEAD_DIM // 2, 1))
        return t * cos + swapped * sin

    def proj(c0, width):
        return _dot_nt(h, w_ref[c0:c0 + width, :])

    tm = x_ref.shape[0]
    conv_cols = 3 * B_WIDTH

    @pl.when(pl.program_id(0) % tiles_per_seq == 0)
    def _():
        xpad_ref[0:CONV_PAD, :] = jnp.zeros((CONV_PAD, conv_cols), F32)

    @pl.when(pl.program_id(0) % tiles_per_seq != 0)
    def _():
        xpad_ref[0:CONV_PAD, :] = xpad_ref[tm:tm + CONV_PAD, :]

    def stage_conv_input(seg):
        xpad_ref[CONV_PAD:CONV_PAD + tm, seg * B_WIDTH:(seg + 1) * B_WIDTH] = proj(
            C_GDN + seg * B_WIDTH, B_WIDTH)

    def conv_segment(seg):
        for hh in range(B_HEADS):
            cs = slice(seg * B_WIDTH + hh * LANES, seg * B_WIDTH + (hh + 1) * LANES)
            xa = xpad_ref[:, cs]
            y = cw_ref[0:1, cs] * xa
            for jj in range(1, CONV_WIDTH):
                y = cw_ref[jj:jj + 1, cs] * xa + pltpu.roll(y, 1, 0)
            y = y[CONV_PAD:, :]
            y = y * jax.nn.sigmoid(y)
            if seg < 2:
                y = y * lax.rsqrt(jnp.sum(y * y, axis=-1, keepdims=True) + EPS)
            if seg == 0:
                y = y * (B_HEAD_DIM ** -0.5)
            gdn_ref[:, cs] = y

    stage_conv_input(0)
    acc = proj(C_QA, A_WIDTH)
    conv_segment(0)
    for j in range(A_WIDTH // LANES):
        sl = slice(j * LANES, (j + 1) * LANES)
        qa_ref[:, sl] = (rope(acc[:, sl]) * (A_HEAD_DIM ** -0.5 * LOG2_E)).astype(BF16)
    stage_conv_input(1)
    acc = proj(C_QI, A_WIDTH)
    conv_segment(1)
    for j in range(A_WIDTH // LANES):
        sl = slice(j * LANES, (j + 1) * LANES)
        qi_ref[:, sl] = rope(acc[:, sl]).astype(BF16)
    stage_conv_input(2)
    acc = proj(C_KV, 2 * KV_WIDTH)
    acc_sm = proj(C_SM, LANES)
    z = proj(C_GDN + conv_cols, B_WIDTH)
    conv_segment(2)
    kv_ref[:, 0:KV_WIDTH] = rope(acc[:, 0:KV_WIDTH]).astype(BF16)
    kv_ref[:, KV_WIDTH:] = acc[:, KV_WIDTH:].astype(BF16)
    sm_ref[...] = jnp.where(lane < IDX_DIM, rope(acc_sm), acc_sm)
    gdn_ref[:, conv_cols:] = z * jax.nn.sigmoid(z)


def _inproj(x2, g, w, layer, cos_t, sin_t, conv_w, seq, tm):
    m = x2.shape[0]
    assert seq % tm == 0, (seq, tm)
    nt = seq // tm
    row = lambda i: (i, 0)
    const = lambda i: (0, 0)
    return pl.pallas_call(
        functools.partial(_inproj_body, tiles_per_seq=nt),
        grid=(m // tm,),
        in_specs=[
            pl.BlockSpec((tm, D_MODEL), row),
            pl.BlockSpec((1, D_MODEL), const),
            pl.BlockSpec((None, IN_DIM, D_MODEL), lambda i: (layer, 0, 0),
                         pipeline_mode=pl.Buffered(1)),
            pl.BlockSpec((tm, LANES), lambda i: (i % nt, 0)),
            pl.BlockSpec((tm, LANES), lambda i: (i % nt, 0)),
            pl.BlockSpec((CONV_WIDTH, 3 * B_WIDTH), const),
        ],
        out_specs=[
            pl.BlockSpec((tm, A_WIDTH), row),
            pl.BlockSpec((tm, A_WIDTH), row),
            pl.BlockSpec((tm, 2 * KV_WIDTH), row),
            pl.BlockSpec((tm, GDN_WIDTH), row),
            pl.BlockSpec((tm, LANES), row),
        ],
        out_shape=[
            jax.ShapeDtypeStruct((m, A_WIDTH), BF16),
            jax.ShapeDtypeStruct((m, A_WIDTH), BF16),
            jax.ShapeDtypeStruct((m, 2 * KV_WIDTH), BF16),
            jax.ShapeDtypeStruct((m, GDN_WIDTH), F32),
            jax.ShapeDtypeStruct((m, LANES), F32),
        ],
        scratch_shapes=[
            pltpu.VMEM((tm + CONV_PAD, 3 * B_WIDTH), F32),
            pltpu.VMEM((IN_COLS, D_MODEL), BF16),
        ],
        compiler_params=pltpu.CompilerParams(
            dimension_semantics=("arbitrary",), vmem_limit_bytes=VMEM_LIMIT),
        name="inproj",
    )(x2, g, w, cos_t, sin_t, conv_w)


def _mixer_a_body(qa_ref, qi_ref, kv_ref, sm_ref, o_ref,
                  isc_ref, isb_ref, vt_ref, qit_ref, qat_ref, acc_ref, bias_ref, *, seq, topk):
    j = pl.program_id(1)
    nkt = (j + 1) * (A_QUERIES // LANES)
    n_heads_pair = A_WIDTH // LANES
    group = A_HEADS // A_KV_HEADS

    @pl.when(j == 0)
    def _():
        per_block = KEY_BLOCK // LANES
        for t in range(seq // LANES):
            vt = kv_ref[t * LANES:(t + 1) * LANES, KV_WIDTH:].astype(F32)
            c0 = (t % per_block) * LANES
            vt_ref[t // per_block, :, c0:c0 + LANES] = vt.T.astype(BF16)

    for p in range(n_heads_pair):
        sl = slice(p * LANES, (p + 1) * LANES)
        t = qi_ref[:, sl].astype(F32).T
        qit_ref[:, (2 * p) * A_QUERIES:(2 * p + 1) * A_QUERIES] = t[0:IDX_DIM].astype(BF16)
        qit_ref[:, (2 * p + 1) * A_QUERIES:(2 * p + 2) * A_QUERIES] = t[IDX_DIM:].astype(BF16)
        t = qa_ref[:, sl].astype(F32).T
        qat_ref[:, (2 * p) * A_QUERIES:(2 * p + 1) * A_QUERIES] = t[0:A_HEAD_DIM].astype(BF16)
        qat_ref[:, (2 * p + 1) * A_QUERIES:(2 * p + 2) * A_QUERIES] = t[A_HEAD_DIM:].astype(BF16)

    q0 = pl.multiple_of(j * A_QUERIES, A_QUERIES)
    w_t = sm_ref[pl.ds(q0, A_QUERIES), :].T[SM_WI:SM_WI + IDX_HEADS, :]
    w_t = w_t * ((IDX_HEADS ** -0.5) * (IDX_DIM ** -0.5))

    qlane = lax.broadcasted_iota(I32, (1, A_QUERIES), 1)
    limit = q0 + (lax.shift_right_logical(qlane, CHUNK.bit_length() - 1) + 1) * CHUNK
    tile_iota = lax.broadcasted_iota(I32, (LANES, A_QUERIES), 0)
    block_iota = lax.broadcasted_iota(I32, (KEY_BLOCK, A_QUERIES), 0)
    nkb = lax.shift_right_logical(nkt + (KEY_BLOCK // LANES - 1), (KEY_BLOCK // LANES).bit_length() - 1)

    def isc_body(kb, carry):
        r0 = pl.multiple_of(kb * KEY_BLOCK, KEY_BLOCK)
        kid = sm_ref[pl.ds(r0, KEY_BLOCK), SM_KI:SM_KI + IDX_DIM].astype(BF16)
        rel = _dot(kid, qit_ref[...])
        acc = jnp.zeros((KEY_BLOCK, A_QUERIES), F32)
        for h in range(IDX_HEADS):
            acc = acc + w_t[h:h + 1, :] * jnp.maximum(rel[:, h * A_QUERIES:(h + 1) * A_QUERIES], 0.0)
        acc = jnp.where(r0 + block_iota < limit, acc, -jnp.inf)
        isc_ref[pl.ds(r0, KEY_BLOCK), :] = acc
        isb_ref[pl.ds(r0, KEY_BLOCK), :] = acc.astype(BF16)
        return carry

    lax.fori_loop(0, nkb, isc_body, 0)

    ngrp = lax.shift_right_logical(nkt + (COUNT_GROUP - 1), COUNT_GROUP.bit_length() - 1)

    def fill_body(kt, carry):
        r0 = pl.multiple_of(kt * LANES, LANES)
        isc_ref[pl.ds(r0, LANES), :] = jnp.full((LANES, A_QUERIES), -jnp.inf, F32)
        isb_ref[pl.ds(r0, LANES), :] = jnp.full((LANES, A_QUERIES), -jnp.inf, BF16)
        return carry

    lax.fori_loop(nkb * (KEY_BLOCK // LANES), ngrp * COUNT_GROUP, fill_body, 0)

    def count(pred):
        def body(g, acc):
            for t in range(COUNT_GROUP):
                r0 = pl.multiple_of((g * COUNT_GROUP + t) * LANES, LANES)
                hit = jnp.where(pred(isc_ref[pl.ds(r0, LANES), :], r0 + tile_iota), 1.0, 0.0)
                acc = acc + hit.reshape(LANES // COUNT_ROWS, COUNT_ROWS, A_QUERIES).sum(axis=0)
            return acc
        acc = lax.fori_loop(0, ngrp, body, jnp.zeros((COUNT_ROWS, A_QUERIES), F32))
        return acc.sum(axis=0, keepdims=True)

    def count_coarse(thr):
        def body(g, acc):
            for t in range(COUNT_GROUP):
                r0 = pl.multiple_of((g * COUNT_GROUP + t) * LANES, LANES)
                hit = jnp.where(isb_ref[pl.ds(r0, LANES), :] >= thr, one_b, zero_b)
                hit = hit.reshape(LANES // COUNT_ROWS, COUNT_ROWS, A_QUERIES)
                part = hit[0]
                for i in range(1, LANES // COUNT_ROWS):
                    part = part + hit[i]
                acc = acc + part
            return acc
        acc = lax.fori_loop(0, ngrp, body, jnp.zeros((COUNT_ROWS, A_QUERIES), BF16))
        return acc.astype(F32).sum(axis=0, keepdims=True)

    kf = float(topk)
    one_b = jnp.ones((), BF16)
    zero_b = jnp.zeros((), BF16)

    def thr_coarse(key):
        bits = jnp.where(key >= 0, key, key ^ jnp.int32(0x7FFF))
        return lax.bitcast_convert_type(lax.shift_left(bits, 16), F32).astype(BF16)

    c0 = count_coarse(jnp.zeros((1, A_QUERIES), BF16))
    lo16 = jnp.where(c0 >= kf, jnp.int32(0), jnp.int32(-2 ** 15))

    def coarse_body(i, lo):
        trial = lo | lax.shift_left(jnp.int32(1), 14 - i)
        c = count_coarse(thr_coarse(trial))
        return jnp.where(c >= kf, trial, lo)

    lo16 = lax.fori_loop(0, 15, coarse_body, lo16)
    lo16 = jnp.maximum(lo16, jnp.int32(NEG_INF_KEY >> 16))
    center = lax.shift_left(lo16, 16) | jnp.where(lo16 < 0, jnp.int32(0xFFFF), jnp.int32(0))

    def thr_of(key):
        bits = jnp.where(key >= 0, key, key ^ jnp.int32(0x7FFFFFFF))
        return jnp.where(key < jnp.int32(NEG_INF_KEY), -jnp.inf, lax.bitcast_convert_type(bits, F32))

    def fine_body(i, st):
        lo, c_lo = st
        trial = lo + lax.shift_left(jnp.int32(1), 16 - i)
        thr = thr_of(trial)
        c = count(lambda x, s: x >= thr)
        ok = c >= kf
        return jnp.where(ok, trial, lo), jnp.where(ok, c, c_lo)

    lo, c_lo = lax.fori_loop(0, 17, fine_body,
                             (center - jnp.int32(1 << 16), jnp.full((1, A_QUERIES), -1.0, F32)))
    kth = thr_of(lo)

    idx_bits = (seq - 1).bit_length()

    def tie_search():
        need = kf - count(lambda x, s: x > kth)

        def tie_body(i, last):
            trial = last | lax.shift_left(jnp.int32(1), idx_bits - 1 - i)
            c = count(lambda x, s: (x == kth) & (s < trial))
            return jnp.where(c < need, trial, last)

        return lax.fori_loop(0, idx_bits, tie_body, jnp.zeros((1, A_QUERIES), I32))

    def take_all_ties():
        return jnp.full((1, A_QUERIES), (1 << idx_bits) - 1, I32)

    n_ge = jnp.where(c_lo < 0.0, jnp.inf, c_lo)
    last = lax.cond(jnp.max(n_ge) > kf, tie_search, take_all_ties)

    acc_ref[...] = jnp.zeros_like(acc_ref)

    def att_body(kb, carry):
        m_prev, l_prev = carry
        r0 = pl.multiple_of(kb * KEY_BLOCK, KEY_BLOCK)
        x = isc_ref[pl.ds(r0, KEY_BLOCK), :]
        s_idx = r0 + block_iota
        sel = (s_idx < limit) & ((x > kth) | ((x == kth) & (s_idx <= last)))
        bias_ref[...] = jnp.where(sel, 0.0, NEG_BIG)
        k_tile = kv_ref[pl.ds(r0, KEY_BLOCK), 0:KV_WIDTH]
        v_t = vt_ref[kb]

        units = list(range(A_HEADS // ATT_HEADS))
        uw = ATT_HEADS * A_QUERIES
        us = [slice(u * uw, (u + 1) * uw) for u in units]
        ds = [slice((u * ATT_HEADS // group) * A_HEAD_DIM, (u * ATT_HEADS // group + 1) * A_HEAD_DIM)
              for u in units]
        cs = [slice((u * ATT_HEADS % group) * A_QUERIES, (u * ATT_HEADS % group + ATT_HEADS) * A_QUERIES)
              for u in units]
        m_new, l_new = {}, {}
        for w0 in range(0, len(units), ATT_WAVE):
            wave = units[w0:w0 + ATT_WAVE]
            s = {u: _dot(k_tile[:, ds[u]], qat_ref[:, us[u]])
                 + jnp.concatenate([bias_ref[...]] * ATT_HEADS, axis=1) for u in wave}
            for u in wave:
                m_new[u] = jnp.maximum(m_prev[:, us[u]], jnp.max(s[u], axis=0, keepdims=True))
            alpha = {u: jnp.exp2(m_prev[:, us[u]] - m_new[u]) for u in wave}
            p = {u: jnp.exp2(s[u] - m_new[u]) for u in wave}
            pv = {u: _dot(v_t[ds[u], :], p[u].astype(BF16)) for u in wave}
            for u in wave:
                l_new[u] = alpha[u] * l_prev[:, us[u]] + jnp.sum(p[u], axis=0, keepdims=True)
            for u in wave:
                acc_ref[ds[u], cs[u]] = alpha[u] * acc_ref[ds[u], cs[u]] + pv[u]
        return (jnp.concatenate([m_new[u] for u in units], axis=1),
                jnp.concatenate([l_new[u] for u in units], axis=1))

    m0 = jnp.full((1, A_HEADS * A_QUERIES), NEG_BIG, F32)
    l0 = jnp.zeros((1, A_HEADS * A_QUERIES), F32)
    _, l_fin = lax.fori_loop(0, nkb, att_body, (m0, l0))

    for p in range(n_heads_pair):
        g = (2 * p) // group
        ds = slice(g * A_HEAD_DIM, (g + 1) * A_HEAD_DIM)
        parts = []
        for h in (2 * p, 2 * p + 1):
            hl = h % group
            parts.append(acc_ref[ds, hl * A_QUERIES:(hl + 1) * A_QUERIES]
                         / l_fin[:, h * A_QUERIES:(h + 1) * A_QUERIES])
        o_ref[:, p * LANES:(p + 1) * LANES] = jnp.concatenate(parts, axis=0).T.astype(BF16)


def _mixer_a(qa, qi, kv, sm, bsz, seq):
    nq = seq // A_QUERIES
    assert seq % (COUNT_GROUP * LANES) == 0, seq
    assert seq // COUNT_ROWS <= 256, seq
    topk = min(TOPK_MAX, seq // 4)
    qrow = lambda b, j: (b * nq + j, 0)
    brow = lambda b, j: (b, 0)
    return pl.pallas_call(
        functools.partial(_mixer_a_body, seq=seq, topk=topk),
        grid=(bsz, nq),
        in_specs=[
            pl.BlockSpec((A_QUERIES, A_WIDTH), qrow),
            pl.BlockSpec((A_QUERIES, A_WIDTH), qrow),
            pl.BlockSpec((seq, 2 * KV_WIDTH), brow),
            pl.BlockSpec((seq, LANES), brow),
        ],
        out_specs=pl.BlockSpec((A_QUERIES, A_WIDTH), qrow),
        out_shape=jax.ShapeDtypeStruct((bsz * seq, A_WIDTH), BF16),
        scratch_shapes=[
            pltpu.VMEM((seq, A_QUERIES), F32),
            pltpu.VMEM((seq, A_QUERIES), BF16),
            pltpu.VMEM((seq // KEY_BLOCK, KV_WIDTH, KEY_BLOCK), BF16),
            pltpu.VMEM((IDX_DIM, IDX_HEADS * A_QUERIES), BF16),
            pltpu.VMEM((A_HEAD_DIM, A_HEADS * A_QUERIES), BF16),
            pltpu.VMEM((KV_WIDTH, (A_HEADS // A_KV_HEADS) * A_QUERIES), F32),
            pltpu.VMEM((KEY_BLOCK, A_QUERIES), F32),
        ],
        compiler_params=pltpu.CompilerParams(
            dimension_semantics=("arbitrary", "arbitrary"), vmem_limit_bytes=VMEM_LIMIT),
        name="mixer_a",
    )(qa, qi, kv, sm)


def _gdn_body(gdn_ref, sm_ref, alog_ref, dtb_ref, ng_ref, o_ref,
              gate_ref, gct_ref, state_ref, oc_ref, lhs_ref, bm_ref, *, tc):
    tb = pl.program_id(1)
    n_chunks = tc // CHUNK
    prep_unroll = 8
    q_col, k_col, v_col, z_col = (i * B_WIDTH for i in range(4))

    @pl.when(tb == 0)
    def _():
        state_ref[...] = jnp.zeros_like(state_ref)

    sm = sm_ref[...]
    beta = jax.nn.sigmoid(sm)
    z = sm + dtb_ref[...]
    softplus = jnp.maximum(z, 0.0) + jnp.log(1.0 + jnp.exp(-jnp.abs(z)))
    g = -jnp.exp(alog_ref[...]) * softplus
    rin = lax.broadcasted_iota(I32, (tc, LANES), 0) & (CHUNK - 1)
    gc = g
    step = 1
    while step < CHUNK:
        gc = gc + jnp.where(rin >= step, pltpu.roll(gc, step, 0), 0.0)
        step *= 2
    gc3 = gc.reshape(n_chunks, CHUNK, LANES)
    g_last = jnp.broadcast_to(gc3[:, CHUNK - 1:CHUNK, :], gc3.shape).reshape(tc, LANES)
    gate_ref[0] = beta
    gate_ref[1] = gc
    gate_ref[2] = jnp.exp(gc)
    gate_ref[3] = jnp.exp(g_last - gc)
    gate_ref[4] = jnp.exp(g_last)
    for i in range(tc // LANES):
        t = gc[i * LANES:(i + 1) * LANES, :].T
        for half in range(LANES // CHUNK):
            gct_ref[i * (LANES // CHUNK) + half] = t[:, half * CHUNK:(half + 1) * CHUNK]

    ci = lax.broadcasted_iota(I32, (CHUNK, CHUNK), 0)
    si = lax.broadcasted_iota(I32, (CHUNK, CHUNK), 1)
    wl = lax.broadcasted_iota(I32, (CHUNK, 2 * CHUNK), 1)
    wr = lax.broadcasted_iota(I32, (CHUNK, 2 * CHUNK), 0)
    right = wl >= CHUNK
    eye_right = jnp.where(wl == wr + CHUNK, 1.0, 0.0)

    def prep_body(cg, carry):
        units = [(cg * prep_unroll + cc, h) for cc in range(prep_unroll) for h in range(B_HEADS)]
        rows = [pl.ds(pl.multiple_of(c * CHUNK, CHUNK), CHUNK) for c, _ in units]
        hsl = [slice(h * LANES, (h + 1) * LANES) for _, h in units]
        idx = range(len(units))

        def col(i, gate, off):
            h = units[i][1]
            return gate_ref[gate, rows[i], off + h:off + h + 1]

        def seg(i, col0):
            h = units[i][1]
            return gdn_ref[rows[i], col0 + h * LANES:col0 + (h + 1) * LANES]

        q = [seg(i, q_col) for i in idx]
        k = [seg(i, k_col) for i in idx]
        kb = [k[i] * col(i, 0, SM_BETA) for i in idx]
        kq = [_dot_nt(jnp.concatenate([kb[i], q[i]], axis=0).astype(BF16), k[i].astype(BF16))
              for i in idx]
        decay = []
        for i, (c, h) in enumerate(units):
            d = col(i, 1, SM_DECAY) - gct_ref[c][SM_DECAY + h:SM_DECAY + h + 1, :]
            decay.append(jnp.where(ci >= si, jnp.exp(jnp.where(ci >= si, d, 0.0)), 0.0))
        wmat = []
        for i, (c, h) in enumerate(units):
            n_mat = jnp.where(ci > si, -(kq[i][0:CHUNK] * decay[i]), 0.0)
            wmat.append(jnp.concatenate([n_mat, jnp.zeros_like(n_mat)], axis=1) + eye_right)
        pw = 1
        while pw < CHUNK:
            wb = [wmat[i].astype(BF16) for i in idx]
            wmat = [_dot(wb[i][:, 0:CHUNK], wb[i]) + jnp.where(right, wmat[i], 0.0) for i in idx]
            pw *= 2
        eg = [col(i, 2, SM_DECAY) for i in idx]
        rhs = [jnp.concatenate([seg(i, v_col) * col(i, 0, SM_BETA), kb[i] * eg[i]],
                               axis=1).astype(BF16) for i in idx]
        sol = [_dot(wmat[i][:, CHUNK:].astype(BF16), rhs[i]).astype(BF16) for i in idx]
        att = [(kq[i][CHUNK:] * decay[i]).astype(BF16) for i in idx]
        k_tail_t = [(k[i] * col(i, 3, SM_DECAY)).T.astype(BF16) for i in idx]
        a_uw = [_dot(att[i], sol[i]) for i in idx]
        k_uw = [_dot(k_tail_t[i], sol[i]) for i in idx]
        for i, (c, h) in enumerate(units):
            oc_ref[rows[i], hsl[i]] = a_uw[i][:, 0:B_HEAD_DIM]
            lhs_ref[c, h, 0:CHUNK, :] = (q[i] * eg[i] - a_uw[i][:, B_HEAD_DIM:]).astype(BF16)
            lhs_ref[c, h, CHUNK:, :] = k_uw[i][:, B_HEAD_DIM:].astype(BF16)
            bm_ref[c, h] = k_uw[i][:, 0:B_HEAD_DIM]
        return carry

    lax.fori_loop(0, n_chunks // prep_unroll, prep_body, 0)

    def scan_body(c, carry):
        r0 = pl.multiple_of(c * CHUNK, CHUNK)
        rows = pl.ds(r0, CHUNK)
        heads = range(B_HEADS)
        hsl = [slice(h * LANES, (h + 1) * LANES) for h in heads]
        s_prev = [state_ref[h] for h in heads]
        r = [_dot(lhs_ref[c, h], s_prev[h].astype(BF16)) for h in heads]
        for h in heads:
            gl = gate_ref[4, rows, SM_DECAY + h:SM_DECAY + h + 1][0:1, :]
            oc_ref[rows, hsl[h]] = oc_ref[rows, hsl[h]] + r[h][0:CHUNK]
            state_ref[h] = s_prev[h] * gl + bm_ref[c, h] - r[h][CHUNK:]
        return carry

    lax.fori_loop(0, n_chunks, scan_body, 0)

    for h in range(B_HEADS):
        hs = slice(h * LANES, (h + 1) * LANES)
        z_gate = gdn_ref[:, z_col + h * LANES:z_col + (h + 1) * LANES]
        o_ref[:, hs] = (_rms(oc_ref[:, hs], ng_ref[...]) * z_gate).astype(BF16)


def _gdn(gdn_in, sm, alog_row, dtb_row, norm_g, bsz, seq, tc):
    nt = seq // tc
    trow = lambda b, t: (b * nt + t, 0)
    const = lambda b, t: (0, 0)
    n_chunks = tc // CHUNK
    return pl.pallas_call(
        functools.partial(_gdn_body, tc=tc),
        grid=(bsz, nt),
        in_specs=[
            pl.BlockSpec((tc, GDN_WIDTH), trow),
            pl.BlockSpec((tc, LANES), trow),
            pl.BlockSpec((1, LANES), const),
            pl.BlockSpec((1, LANES), const),
            pl.BlockSpec((1, B_HEAD_DIM), const),
        ],
        out_specs=pl.BlockSpec((tc, B_WIDTH), trow),
        out_shape=jax.ShapeDtypeStruct((bsz * seq, B_WIDTH), BF16),
        scratch_shapes=[
            pltpu.VMEM((5, tc, LANES), F32),
            pltpu.VMEM((n_chunks, LANES, CHUNK), F32),
            pltpu.VMEM((B_HEADS, B_HEAD_DIM, B_HEAD_DIM), F32),
            pltpu.VMEM((tc, B_WIDTH), F32),
            pltpu.VMEM((n_chunks, B_HEADS, CHUNK + B_HEAD_DIM, B_HEAD_DIM), BF16),
            pltpu.VMEM((n_chunks, B_HEADS, B_HEAD_DIM, B_HEAD_DIM), F32),
        ],
        compiler_params=pltpu.CompilerParams(
            dimension_semantics=("arbitrary", "arbitrary"), vmem_limit_bytes=VMEM_LIMIT),
        name="gdn",
    )(gdn_in, sm, alog_row, dtb_row, norm_g)


def _ffn_body(x_ref, oa_ref, ob_ref, wo_ref, g2_ref, w1_ref, w2_ref, g3_ref, out_ref, *,
              final_norm):
    y = (x_ref[...] + _dot(oa_ref[...], wo_ref[0:A_WIDTH, :])
         + _dot(ob_ref[...], wo_ref[A_WIDTH:, :]))
    h = _rms(y, g2_ref[...]).astype(BF16)
    a = jnp.square(jnp.maximum(_dot(h, w1_ref[...]), 0.0)).astype(BF16)
    acc = y + _dot(a, w2_ref[...])
    out_ref[...] = _rms(acc, g3_ref[...]) if final_norm else acc


def _ffn(x2, oa, ob, wo, g2, w1, w2, g3, tm, final_norm):
    m = x2.shape[0]
    row = lambda i: (i, 0)
    const = lambda i: (0, 0)
    resident = functools.partial(pl.BlockSpec, index_map=const, pipeline_mode=pl.Buffered(1))
    return pl.pallas_call(
        functools.partial(_ffn_body, final_norm=final_norm),
        grid=(m // tm,),
        in_specs=[
            pl.BlockSpec((tm, D_MODEL), row),
            pl.BlockSpec((tm, A_WIDTH), row),
            pl.BlockSpec((tm, B_WIDTH), row),
            resident((D_MODEL, D_MODEL)),
            pl.BlockSpec((1, D_MODEL), const),
            resident((D_MODEL, D_FF)),
            resident((D_FF, D_MODEL)),
            pl.BlockSpec((1, D_MODEL), const),
        ],
        out_specs=pl.BlockSpec((tm, D_MODEL), row),
        out_shape=jax.ShapeDtypeStruct((m, D_MODEL), F32),
        compiler_params=pltpu.CompilerParams(
            dimension_semantics=("arbitrary",), vmem_limit_bytes=VMEM_LIMIT),
        name="ffn",
    )(x2, oa, ob, wo, g2, w1, w2, g3)


def _rope_tables(seq):
    half = A_HEAD_DIM // 2
    inv_freq = 1.0 / (ROPE_THETA ** (jnp.arange(half, dtype=F32) / half))
    ang = jnp.arange(seq).astype(F32)[:, None] * inv_freq[None, :]
    cos = jnp.cos(ang)
    sin = jnp.sin(ang)
    reps = LANES // A_HEAD_DIM
    return (jnp.tile(cos, (1, 2 * reps)),
            jnp.tile(jnp.concatenate([-sin, sin], axis=1), (1, reps)))


def _lane_row(vals, offset):
    return jnp.zeros((1, LANES), F32).at[0, offset:offset + vals.shape[0]].set(vals.astype(F32))


def kernel(x, norm_mix_g, w_in, conv_w, a_log, dt_bias, gdn_norm_g, w_out,
           norm_ffn_g, w_ff1, w_ff2, norm_final_g):
    bsz, seq, d = x.shape
    depth = w_in.shape[0]
    m = bsz * seq
    cos_t, sin_t = _rope_tables(seq)
    x2 = x.reshape(m, d)
    for l in range(depth):
        qa, qi, kv, gdn_in, sm = _inproj(
            x2, norm_mix_g[l][None, :], jnp.swapaxes(w_in, 1, 2), l, cos_t, sin_t, conv_w[l],
            seq, tm=512)
        o_a = _mixer_a(qa, qi, kv, sm, bsz, seq)
        o_b = _gdn(gdn_in, sm, _lane_row(a_log[l], SM_DECAY),
                   _lane_row(dt_bias[l], SM_DECAY), gdn_norm_g[l][None, :], bsz, seq, tc=512)
        x2 = _ffn(x2, o_a, o_b, w_out[l].astype(BF16), norm_ffn_g[l][None, :],
                  w_ff1[l].astype(BF16), w_ff2[l].astype(BF16), norm_final_g[None, :],
                  tm=512, final_norm=(l == depth - 1))
    return x2.reshape(bsz, seq, d)
```

```python
import functools

import jax
import jax.numpy as jnp
from jax import lax
from jax.experimental import pallas as pl
from jax.experimental.pallas import tpu as pltpu

F32 = jnp.float32
BF16 = jnp.bfloat16
I32 = jnp.int32

D_MODEL = 1024
CHUNK = 64
A_QUERIES = 256
ROPE_THETA = 10000.0
EPS = 1e-6
A_HEADS = 8
A_KV_HEADS = 2
A_HEAD_DIM = 64
IDX_HEADS = 8
IDX_DIM = 64
TOPK_MAX = 256
B_HEADS = 4
B_HEAD_DIM = 128
CONV_WIDTH = 4
D_FF = 4 * D_MODEL

LANES = 128
A_WIDTH = A_HEADS * A_HEAD_DIM
KV_WIDTH = A_KV_HEADS * A_HEAD_DIM
B_WIDTH = B_HEADS * B_HEAD_DIM
GDN_WIDTH = 4 * B_WIDTH
SM_KI = 0
SM_WI = IDX_DIM
SM_BETA = SM_WI + IDX_HEADS
SM_DECAY = SM_BETA + B_HEADS
C_QA = 0
C_QI = C_QA + A_WIDTH
C_KV = C_QI + A_WIDTH
C_GDN = C_KV + 2 * KV_WIDTH
C_SM = C_GDN + GDN_WIDTH
IN_COLS = C_SM + LANES
REF_SIZES = (A_WIDTH, KV_WIDTH, KV_WIDTH, IDX_HEADS * IDX_DIM, IDX_DIM, IDX_HEADS,
             B_WIDTH, B_WIDTH, B_WIDTH, B_WIDTH, B_HEADS, B_HEADS)
REF_OFF = tuple(sum(REF_SIZES[:i]) for i in range(len(REF_SIZES)))
IN_DIM = sum(REF_SIZES)
IN_WEIGHT_MOVES = (
    (C_QA, REF_OFF[0], A_WIDTH),
    (C_QI, REF_OFF[3], A_WIDTH),
    (C_KV, REF_OFF[1], 2 * KV_WIDTH),
    (C_GDN, REF_OFF[6], GDN_WIDTH),
    (C_SM + SM_KI, REF_OFF[4], IDX_DIM + IDX_HEADS),
    (C_SM + SM_BETA, REF_OFF[10], 2 * B_HEADS),
)
IN_PAD = IN_COLS - (C_SM + SM_DECAY + B_HEADS)

VMEM_LIMIT = 56 * 1024 * 1024
CONV_PAD = 8
COUNT_GROUP = 4
COUNT_ROWS = 32
ATT_HEADS = 2
ATT_WAVE = 4
KEY_BLOCK = 2 * LANES
NEG_BIG = -1e30
LOG2_E = 1.4426950408889634
NEG_INF_KEY = -(2 ** 31) + 0x7FFFFF


def _rms(x, g):
    return x * lax.rsqrt(jnp.mean(x * x, axis=-1, keepdims=True) + EPS) * g


def _dot(a, b):
    return jnp.dot(a, b, preferred_element_type=F32)


def _dot_nt(a, b):
    return lax.dot_general(a, b, (((1,), (1,)), ((), ())), preferred_element_type=F32)


def _inproj_body(x_ref, g_ref, win_ref, cos_ref, sin_ref, cw_ref,
                 qa_ref, qi_ref, kv_ref, gdn_ref, sm_ref, xpad_ref, w_ref, *, tiles_per_seq):
    @pl.when(pl.program_id(0) == 0)
    def _():
        step = 256
        for dst, src, width in IN_WEIGHT_MOVES[:-2]:
            for r0 in range(0, width, step):
                w_ref[dst + r0:dst + r0 + step, :] = win_ref[src + r0:src + r0 + step, :].astype(BF16)
        small = [win_ref[src:src + width, :] for _, src, width in IN_WEIGHT_MOVES[-2:]]
        small.append(jnp.zeros((IN_PAD, D_MODEL), F32))
        w_ref[C_SM:IN_COLS, :] = jnp.concatenate(small, axis=0).astype(BF16)

    h = _rms(x_ref[...], g_ref[...]).astype(BF16)
    cos = cos_ref[...]
    sin = sin_ref[...]
    lane = lax.broadcasted_iota(I32, cos.shape, 1)
    first_half = (lane & (A_HEAD_DIM - 1)) < A_HEAD_DIM // 2

    def rope(t):
        swapped = jnp.where(first_half, pltpu.roll(t, LANES - A_HEAD_DIM // 2, 1),
                            pltpu.roll(t, A_HEAD_DIM // 2, 1))
        return t * cos + swapped * sin

    def proj(c0, width):
        return _dot_nt(h, w_ref[c0:c0 + width, :])

    tm = x_ref.shape[0]
    conv_cols = 3 * B_WIDTH

    @pl.when(pl.program_id(0) % tiles_per_seq == 0)
    def _():
        xpad_ref[0:CONV_PAD, :] = jnp.zeros((CONV_PAD, conv_cols), F32)

    @pl.when(pl.program_id(0) % tiles_per_seq != 0)
    def _():
        xpad_ref[0:CONV_PAD, :] = xpad_ref[tm:tm + CONV_PAD, :]

    def stage_conv_input(seg):
        xpad_ref[CONV_PAD:CONV_PAD + tm, seg * B_WIDTH:(seg + 1) * B_WIDTH] = proj(
            C_GDN + seg * B_WIDTH, B_WIDTH)

    def conv_segment(seg):
        for hh in range(B_HEADS):
            cs = slice(seg * B_WIDTH + hh * LANES, seg * B_WIDTH + (hh + 1) * LANES)
            xa = xpad_ref[:, cs]
            y = cw_ref[0:1, cs] * xa
            for jj in range(1, CONV_WIDTH):
                y = cw_ref[jj:jj + 1, cs] * xa + pltpu.roll(y, 1, 0)
            y = y[CONV_PAD:, :]
            y = y * jax.nn.sigmoid(y)
            if seg < 2:
                y = y * lax.rsqrt(jnp.sum(y * y, axis=-1, keepdims=True) + EPS)
            if seg == 0:
                y = y * (B_HEAD_DIM ** -0.5)
            gdn_ref[:, cs] = y

    stage_conv_input(0)
    acc = proj(C_QA, A_WIDTH)
    conv_segment(0)
    for j in range(A_WIDTH // LANES):
        sl = slice(j * LANES, (j + 1) * LANES)
        qa_ref[:, sl] = (rope(acc[:, sl]) * (A_HEAD_DIM ** -0.5 * LOG2_E)).astype(BF16)
    stage_conv_input(1)
    acc = proj(C_QI, A_WIDTH)
    conv_segment(1)
    for j in range(A_WIDTH // LANES):
        sl = slice(j * LANES, (j + 1) * LANES)
        qi_ref[:, sl] = rope(acc[:, sl]).astype(BF16)
    stage_conv_input(2)
    acc = proj(C_KV, 2 * KV_WIDTH)
    acc_sm = proj(C_SM, LANES)
    z = proj(C_GDN + conv_cols, B_WIDTH)
    conv_segment(2)
    kv_ref[:, 0:KV_WIDTH] = rope(acc[:, 0:KV_WIDTH]).astype(BF16)
    kv_ref[:, KV_WIDTH:] = acc[:, KV_WIDTH:].astype(BF16)
    sm_ref[...] = jnp.where(lane < IDX_DIM, rope(acc_sm), acc_sm)
    gdn_ref[:, conv_cols:] = z * jax.nn.sigmoid(z)


def _inproj(x2, g, w, layer, cos_t, sin_t, conv_w, seq, tm):
    m = x2.shape[0]
    assert seq % tm == 0, (seq, tm)
    nt = seq // tm
    row = lambda i: (i, 0)
    const = lambda i: (0, 0)
    return pl.pallas_call(
        functools.partial(_inproj_body, tiles_per_seq=nt),
        grid=(m // tm,),
        in_specs=[
            pl.BlockSpec((tm, D_MODEL), row),
            pl.BlockSpec((1, D_MODEL), const),
            pl.BlockSpec((None, IN_DIM, D_MODEL), lambda i: (layer, 0, 0),
                         pipeline_mode=pl.Buffered(1)),
            pl.BlockSpec((tm, LANES), lambda i: (i % nt, 0)),
            pl.BlockSpec((tm, LANES), lambda i: (i % nt, 0)),
            pl.BlockSpec((CONV_WIDTH, 3 * B_WIDTH), const),
        ],
        out_specs=[
            pl.BlockSpec((tm, A_WIDTH), row),
            pl.BlockSpec((tm, A_WIDTH), row),
            pl.BlockSpec((tm, 2 * KV_WIDTH), row),
            pl.BlockSpec((tm, GDN_WIDTH), row),
            pl.BlockSpec((tm, LANES), row),
        ],
        out_shape=[
            jax.ShapeDtypeStruct((m, A_WIDTH), BF16),
            jax.ShapeDtypeStruct((m, A_WIDTH), BF16),
            jax.ShapeDtypeStruct((m, 2 * KV_WIDTH), BF16),
            jax.ShapeDtypeStruct((m, GDN_WIDTH), F32),
            jax.ShapeDtypeStruct((m, LANES), F32),
        ],
        scratch_shapes=[
            pltpu.VMEM((tm + CONV_PAD, 3 * B_WIDTH), F32),
            pltpu.VMEM((IN_COLS, D_MODEL), BF16),
        ],
        compiler_params=pltpu.CompilerParams(
            dimension_semantics=("arbitrary",), vmem_limit_bytes=VMEM_LIMIT),
        name="inproj",
    )(x2, g, w, cos_t, sin_t, conv_w)


def _mixer_a_body(qa_ref, qi_ref, kv_ref, sm_ref, o_ref,
                  isc_ref, isb_ref, vt_ref, qit_ref, qat_ref, acc_ref, bias_ref, *, seq, topk):
    j = pl.program_id(1)
    nkt = (j + 1) * (A_QUERIES // LANES)
    n_heads_pair = A_WIDTH // LANES
    group = A_HEADS // A_KV_HEADS

    @pl.when(j == 0)
    def _():
        per_block = KEY_BLOCK // LANES
        for t in range(seq // LANES):
            vt = kv_ref[t * LANES:(t + 1) * LANES, KV_WIDTH:].astype(F32)
            c0 = (t % per_block) * LANES
            vt_ref[t // per_block, :, c0:c0 + LANES] = vt.T.astype(BF16)

    for p in range(n_heads_pair):
        sl = slice(p * LANES, (p + 1) * LANES)
        t = qi_ref[:, sl].astype(F32).T
        qit_ref[:, (2 * p) * A_QUERIES:(2 * p + 1) * A_QUERIES] = t[0:IDX_DIM].astype(BF16)
        qit_ref[:, (2 * p + 1) * A_QUERIES:(2 * p + 2) * A_QUERIES] = t[IDX_DIM:].astype(BF16)
        t = qa_ref[:, sl].astype(F32).T
        qat_ref[:, (2 * p) * A_QUERIES:(2 * p + 1) * A_QUERIES] = t[0:A_HEAD_DIM].astype(BF16)
        qat_ref[:, (2 * p + 1) * A_QUERIES:(2 * p + 2) * A_QUERIES] = t[A_HEAD_DIM:].astype(BF16)

    q0 = pl.multiple_of(j * A_QUERIES, A_QUERIES)
    w_t = sm_ref[pl.ds(q0, A_QUERIES), :].T[SM_WI:SM_WI + IDX_HEADS, :]
    w_t = w_t * ((IDX_HEADS ** -0.5) * (IDX_DIM ** -0.5))

    qlane = lax.broadcasted_iota(I32, (1, A_QUERIES), 1)
    limit = q0 + (lax.shift_right_logical(qlane, CHUNK.bit_length() - 1) + 1) * CHUNK
    tile_iota = lax.broadcasted_iota(I32, (LANES, A_QUERIES), 0)
    block_iota = lax.broadcasted_iota(I32, (KEY_BLOCK, A_QUERIES), 0)
    nkb = lax.shift_right_logical(nkt + (KEY_BLOCK // LANES - 1), (KEY_BLOCK // LANES).bit_length() - 1)

    def isc_body(kb, carry):
        r0 = pl.multiple_of(kb * KEY_BLOCK, KEY_BLOCK)
        kid = sm_ref[pl.ds(r0, KEY_BLOCK), SM_KI:SM_KI + IDX_DIM].astype(BF16)
        rel = _dot(kid, qit_ref[...])
        acc = jnp.zeros((KEY_BLOCK, A_QUERIES), F32)
        for h in range(IDX_HEADS):
            acc = acc + w_t[h:h + 1, :] * jnp.maximum(rel[:, h * A_QUERIES:(h + 1) * A_QUERIES], 0.0)
        acc = jnp.where(r0 + block_iota < limit, acc, -jnp.inf)
        isc_ref[pl.ds(r0, KEY_BLOCK), :] = acc
        isb_ref[pl.ds(r0, KEY_BLOCK), :] = acc.astype(BF16)
        return carry

    lax.fori_loop(0, nkb, isc_body, 0)

    ngrp = lax.shift_right_logical(nkt + (COUNT_GROUP - 1), COUNT_GROUP.bit_length() - 1)

    def fill_body(kt, carry):
        r0 = pl.multiple_of(kt * LANES, LANES)
        isc_ref[pl.ds(r0, LANES), :] = jnp.full((LANES, A_QUERIES), -jnp.inf, F32)
        isb_ref[pl.ds(r0, LANES), :] = jnp.full((LANES, A_QUERIES), -jnp.inf, BF16)
        return carry

    lax.fori_loop(nkb * (KEY_BLOCK // LANES), ngrp * COUNT_GROUP, fill_body, 0)

    def count(pred):
        def body(g, acc):
            for t in range(COUNT_GROUP):
                r0 = pl.multiple_of((g * COUNT_GROUP + t) * LANES, LANES)
                hit = jnp.where(pred(isc_ref[pl.ds(r0, LANES), :], r0 + tile_iota), 1.0, 0.0)
                acc = acc + hit.reshape(LANES // COUNT_ROWS, COUNT_ROWS, A_QUERIES).sum(axis=0)
            return acc
        acc = lax.fori_loop(0, ngrp, body, jnp.zeros((COUNT_ROWS, A_QUERIES), F32))
        return acc.sum(axis=0, keepdims=True)

    def count_coarse(thr):
        def body(g, acc):
            for t in range(COUNT_GROUP):
                r0 = pl.multiple_of((g * COUNT_GROUP + t) * LANES, LANES)
                hit = jnp.where(isb_ref[pl.ds(r0, LANES), :] >= thr, one_b, zero_b)
                hit = hit.reshape(LANES // COUNT_ROWS, COUNT_ROWS, A_QUERIES)
                part = hit[0]
                for i in range(1, LANES // COUNT_ROWS):
                    part = part + hit[i]
                acc = acc + part
            return acc
        acc = lax.fori_loop(0, ngrp, body, jnp.zeros((COUNT_ROWS, A_QUERIES), BF16))
        return acc.astype(F32).sum(axis=0, keepdims=True)

    kf = float(topk)
    one_b = jnp.ones((), BF16)
    zero_b = jnp.zeros((), BF16)

    def thr_coarse(key):
        bits = jnp.where(key >= 0, key, key ^ jnp.int32(0x7FFF))
        return lax.bitcast_convert_type(lax.shift_left(bits, 16), F32).astype(BF16)

    c0 = count_coarse(jnp.zeros((1, A_QUERIES), BF16))
    lo16 = jnp.where(c0 >= kf, jnp.int32(0), jnp.int32(-2 ** 15))

    def coarse_body(i, lo):
        trial = lo | lax.shift_left(jnp.int32(1), 14 - i)
        c = count_coarse(thr_coarse(trial))
        return jnp.where(c >= kf, trial, lo)

    lo16 = lax.fori_loop(0, 15, coarse_body, lo16)
    lo16 = jnp.maximum(lo16, jnp.int32(NEG_INF_KEY >> 16))
    center = lax.shift_left(lo16, 16) | jnp.where(lo16 < 0, jnp.int32(0xFFFF), jnp.int32(0))

    def thr_of(key):
        bits = jnp.where(key >= 0, key, key ^ jnp.int32(0x7FFFFFFF))
        return jnp.where(key < jnp.int32(NEG_INF_KEY), -jnp.inf, lax.bitcast_convert_type(bits, F32))

    def fine_body(i, st):
        lo, c_lo = st
        trial = lo + lax.shift_left(jnp.int32(1), 16 - i)
        thr = thr_of(trial)
        c = count(lambda x, s: x >= thr)
        ok = c >= kf
        return jnp.where(ok, trial, lo), jnp.where(ok, c, c_lo)

    lo, c_lo = lax.fori_loop(0, 17, fine_body,
                             (center - jnp.int32(1 << 16), jnp.full((1, A_QUERIES), -1.0, F32)))
    kth = thr_of(lo)

    idx_bits = (seq - 1).bit_length()

    def tie_search():
        need = kf - count(lambda x, s: x > kth)

        def tie_body(i, last):
            trial = last | lax.shift_left(jnp.int32(1), idx_bits - 1 - i)
            c = count(lambda x, s: (x == kth) & (s < trial))
            return jnp.where(c < need, trial, last)

        return lax.fori_loop(0, idx_bits, tie_body, jnp.zeros((1, A_QUERIES), I32))

    def take_all_ties():
        return jnp.full((1, A_QUERIES), (1 << idx_bits) - 1, I32)

    n_ge = jnp.where(c_lo < 0.0, jnp.inf, c_lo)
    last = lax.cond(jnp.max(n_ge) > kf, tie_search, take_all_ties)

    acc_ref[...] = jnp.zeros_like(acc_ref)

    def att_body(kb, carry):
        m_prev, l_prev = carry
        r0 = pl.multiple_of(kb * KEY_BLOCK, KEY_BLOCK)
        x = isc_ref[pl.ds(r0, KEY_BLOCK), :]
        s_idx = r0 + block_iota
        sel = (s_idx < limit) & ((x > kth) | ((x == kth) & (s_idx <= last)))
        bias_ref[...] = jnp.where(sel, 0.0, NEG_BIG)
        k_tile = kv_ref[pl.ds(r0, KEY_BLOCK), 0:KV_WIDTH]
        v_t = vt_ref[kb]

        units = list(range(A_HEADS // ATT_HEADS))
        uw = ATT_HEADS * A_QUERIES
        us = [slice(u * uw, (u + 1) * uw) for u in units]
        ds = [slice((u * ATT_HEADS // group) * A_HEAD_DIM, (u * ATT_HEADS // group + 1) * A_HEAD_DIM)
              for u in units]
        cs = [slice((u * ATT_HEADS % group) * A_QUERIES, (u * ATT_HEADS % group + ATT_HEADS) * A_QUERIES)
              for u in units]
        m_new, l_new = {}, {}
        for w0 in range(0, len(units), ATT_WAVE):
            wave = units[w0:w0 + ATT_WAVE]
            s = {u: _dot(k_tile[:, ds[u]], qat_ref[:, us[u]])
                 + jnp.concatenate([bias_ref[...]] * ATT_HEADS, axis=1) for u in wave}
            for u in wave:
                m_new[u] = jnp.maximum(m_prev[:, us[u]], jnp.max(s[u], axis=0, keepdims=True))
            alpha = {u: jnp.exp2(m_prev[:, us[u]] - m_new[u]) for u in wave}
            p = {u: jnp.exp2(s[u] - m_new[u]) for u in wave}
            pv = {u: _dot(v_t[ds[u], :], p[u].astype(BF16)) for u in wave}
            for u in wave:
                l_new[u] = alpha[u] * l_prev[:, us[u]] + jnp.sum(p[u], axis=0, keepdims=True)
            for u in wave:
                acc_ref[ds[u], cs[u]] = alpha[u] * acc_ref[ds[u], cs[u]] + pv[u]
        return (jnp.concatenate([m_new[u] for u in units], axis=1),
                jnp.concatenate([l_new[u] for u in units], axis=1))

    m0 = jnp.full((1, A_HEADS * A_QUERIES), NEG_BIG, F32)
    l0 = jnp.zeros((1, A_HEADS * A_QUERIES), F32)
    _, l_fin = lax.fori_loop(0, nkb, att_body, (m0, l0))

    for p in range(n_heads_pair):
        g = (2 * p) // group
        ds = slice(g * A_HEAD_DIM, (g + 1) * A_HEAD_DIM)
        parts = []
        for h in (2 * p, 2 * p + 1):
            hl = h % group
            parts.append(acc_ref[ds, hl * A_QUERIES:(hl + 1) * A_QUERIES]
                         / l_fin[:, h * A_QUERIES:(h + 1) * A_QUERIES])
        o_ref[:, p * LANES:(p + 1) * LANES] = jnp.concatenate(parts, axis=0).T.astype(BF16)


def _mixer_a(qa, qi, kv, sm, bsz, seq):
    nq = seq // A_QUERIES
    assert seq % (COUNT_GROUP * LANES) == 0, seq
    assert seq // COUNT_ROWS <= 256, seq
    topk = min(TOPK_MAX, seq // 4)
    qrow = lambda b, j: (b * nq + j, 0)
    brow = lambda b, j: (b, 0)
    return pl.pallas_call(
        functools.partial(_mixer_a_body, seq=seq, topk=topk),
        grid=(bsz, nq),
        in_specs=[
            pl.BlockSpec((A_QUERIES, A_WIDTH), qrow),
            pl.BlockSpec((A_QUERIES, A_WIDTH), qrow),
            pl.BlockSpec((seq, 2 * KV_WIDTH), brow),
            pl.BlockSpec((seq, LANES), brow),
        ],
        out_specs=pl.BlockSpec((A_QUERIES, A_WIDTH), qrow),
        out_shape=jax.ShapeDtypeStruct((bsz * seq, A_WIDTH), BF16),
        scratch_shapes=[
            pltpu.VMEM((seq, A_QUERIES), F32),
            pltpu.VMEM((seq, A_QUERIES), BF16),
            pltpu.VMEM((seq // KEY_BLOCK, KV_WIDTH, KEY_BLOCK), BF16),
            pltpu.VMEM((IDX_DIM, IDX_HEADS * A_QUERIES), BF16),
            pltpu.VMEM((A_HEAD_DIM, A_HEADS * A_QUERIES), BF16),
            pltpu.VMEM((KV_WIDTH, (A_HEADS // A_KV_HEADS) * A_QUERIES), F32),
            pltpu.VMEM((KEY_BLOCK, A_QUERIES), F32),
        ],
        compiler_params=pltpu.CompilerParams(
            dimension_semantics=("arbitrary", "arbitrary"), vmem_limit_bytes=VMEM_LIMIT),
        name="mixer_a",
    )(qa, qi, kv, sm)


def _gdn_body(gdn_ref, sm_ref, alog_ref, dtb_ref, ng_ref, o_ref,
              gate_ref, gct_ref, state_ref, oc_ref, lhs_ref, bm_ref, *, tc):
    tb = pl.program_id(1)
    n_chunks = tc // CHUNK
    prep_unroll = 8
    q_col, k_col, v_col, z_col = (i * B_WIDTH for i in range(4))

    @pl.when(tb == 0)
    def _():
        state_ref[...] = jnp.zeros_like(state_ref)

    sm = sm_ref[...]
    beta = jax.nn.sigmoid(sm)
    z = sm + dtb_ref[...]
    softplus = jnp.maximum(z, 0.0) + jnp.log(1.0 + jnp.exp(-jnp.abs(z)))
    g = -jnp.exp(alog_ref[...]) * softplus
    rin = lax.broadcasted_iota(I32, (tc, LANES), 0) & (CHUNK - 1)
    gc = g
    step = 1
    while step < CHUNK:
        gc = gc + jnp.where(rin >= step, pltpu.roll(gc, step, 0), 0.0)
        step *= 2
    gc3 = gc.reshape(n_chunks, CHUNK, LANES)
    g_last = jnp.broadcast_to(gc3[:, CHUNK - 1:CHUNK, :], gc3.shape).reshape(tc, LANES)
    gate_ref[0] = beta
    gate_ref[1] = gc
    gate_ref[2] = jnp.exp(gc)
    gate_ref[3] = jnp.exp(g_last - gc)
    gate_ref[4] = jnp.exp(g_last)
    for i in range(tc // LANES):
        t = gc[i * LANES:(i + 1) * LANES, :].T
        for half in range(LANES // CHUNK):
            gct_ref[i * (LANES // CHUNK) + half] = t[:, half * CHUNK:(half + 1) * CHUNK]

    ci = lax.broadcasted_iota(I32, (CHUNK, CHUNK), 0)
    si = lax.broadcasted_iota(I32, (CHUNK, CHUNK), 1)
    wl = lax.broadcasted_iota(I32, (CHUNK, 2 * CHUNK), 1)
    wr = lax.broadcasted_iota(I32, (CHUNK, 2 * CHUNK), 0)
    right = wl >= CHUNK
    eye_right = jnp.where(wl == wr + CHUNK, 1.0, 0.0)

    def prep_body(cg, carry):
        units = [(cg * prep_unroll + cc, h) for cc in range(prep_unroll) for h in range(B_HEADS)]
        rows = [pl.ds(pl.multiple_of(c * CHUNK, CHUNK), CHUNK) for c, _ in units]
        hsl = [slice(h * LANES, (h + 1) * LANES) for _, h in units]
        idx = range(len(units))

        def col(i, gate, off):
            h = units[i][1]
            return gate_ref[gate, rows[i], off + h:off + h + 1]

        def seg(i, col0):
            h = units[i][1]
            return gdn_ref[rows[i], col0 + h * LANES:col0 + (h + 1) * LANES]

        q = [seg(i, q_col) for i in idx]
        k = [seg(i, k_col) for i in idx]
        kb = [k[i] * col(i, 0, SM_BETA) for i in idx]
        kq = [_dot_nt(jnp.concatenate([kb[i], q[i]], axis=0).astype(BF16), k[i].astype(BF16))
              for i in idx]
        decay = []
        for i, (c, h) in enumerate(units):
            d = col(i, 1, SM_DECAY) - gct_ref[c][SM_DECAY + h:SM_DECAY + h + 1, :]
            decay.append(jnp.where(ci >= si, jnp.exp(jnp.where(ci >= si, d, 0.0)), 0.0))
        wmat = []
        for i, (c, h) in enumerate(units):
            n_mat = jnp.where(ci > si, -(kq[i][0:CHUNK] * decay[i]), 0.0)
            wmat.append(jnp.concatenate([n_mat, jnp.zeros_like(n_mat)], axis=1) + eye_right)
        pw = 1
        while pw < CHUNK:
            wb = [wmat[i].astype(BF16) for i in idx]
            wmat = [_dot(wb[i][:, 0:CHUNK], wb[i]) + jnp.where(right, wmat[i], 0.0) for i in idx]
            pw *= 2
        eg = [col(i, 2, SM_DECAY) for i in idx]
        rhs = [jnp.concatenate([seg(i, v_col) * col(i, 0, SM_BETA), kb[i] * eg[i]],
                               axis=1).astype(BF16) for i in idx]
        sol = [_dot(wmat[i][:, CHUNK:].astype(BF16), rhs[i]).astype(BF16) for i in idx]
        att = [(kq[i][CHUNK:] * decay[i]).astype(BF16) for i in idx]
        k_tail_t = [(k[i] * col(i, 3, SM_DECAY)).T.astype(BF16) for i in idx]
        a_uw = [_dot(att[i], sol[i]) for i in idx]
        k_uw = [_dot(k_tail_t[i], sol[i]) for i in idx]
        for i, (c, h) in enumerate(units):
            oc_ref[rows[i], hsl[i]] = a_uw[i][:, 0:B_HEAD_DIM]
            lhs_ref[c, h, 0:CHUNK, :] = (q[i] * eg[i] - a_uw[i][:, B_HEAD_DIM:]).astype(BF16)
            lhs_ref[c, h, CHUNK:, :] = k_uw[i][:, B_HEAD_DIM:].astype(BF16)
            bm_ref[c, h] = k_uw[i][:, 0:B_HEAD_DIM]
        return carry

    lax.fori_loop(0, n_chunks // prep_unroll, prep_body, 0)

    def scan_body(c, carry):
        r0 = pl.multiple_of(c * CHUNK, CHUNK)
        rows = pl.ds(r0, CHUNK)
        heads = range(B_HEADS)
        hsl = [slice(h * LANES, (h + 1) * LANES) for h in heads]
        s_prev = [state_ref[h] for h in heads]
        r = [_dot(lhs_ref[c, h], s_prev[h].astype(BF16)) for h in heads]
        for h in heads:
            gl = gate_ref[4, rows, SM_DECAY + h:SM_DECAY + h + 1][0:1, :]
            oc_ref[rows, hsl[h]] = oc_ref[rows, hsl[h]] + r[h][0:CHUNK]
            state_ref[h] = s_prev[h] * gl + bm_ref[c, h] - r[h][CHUNK:]
        return carry

    lax.fori_loop(0, n_chunks, scan_body, 0)

    for h in range(B_HEADS):
        hs = slice(h * LANES, (h + 1) * LANES)
        z_gate = gdn_ref[:, z_col + h * LANES:z_col + (h + 1) * LANES]
        o_ref[:, hs] = (_rms(oc_ref[:, hs], ng_ref[...]) * z_gate).astype(BF16)


def _gdn(gdn_in, sm, alog_row, dtb_row, norm_g, bsz, seq, tc):
    nt = seq // tc
    trow = lambda b, t: (b * nt + t, 0)
    const = lambda b, t: (0, 0)
    n_chunks = tc // CHUNK
    return pl.pallas_call(
        functools.partial(_gdn_body, tc=tc),
        grid=(bsz, nt),
        in_specs=[
            pl.BlockSpec((tc, GDN_WIDTH), trow),
            pl.BlockSpec((tc, LANES), trow),
            pl.BlockSpec((1, LANES), const),
            pl.BlockSpec((1, LANES), const),
            pl.BlockSpec((1, B_HEAD_DIM), const),
        ],
        out_specs=pl.BlockSpec((tc, B_WIDTH), trow),
        out_shape=jax.ShapeDtypeStruct((bsz * seq, B_WIDTH), BF16),
        scratch_shapes=[
            pltpu.VMEM((5, tc, LANES), F32),
            pltpu.VMEM((n_chunks, LANES, CHUNK), F32),
            pltpu.VMEM((B_HEADS, B_HEAD_DIM, B_HEAD_DIM), F32),
            pltpu.VMEM((tc, B_WIDTH), F32),
            pltpu.VMEM((n_chunks, B_HEADS, CHUNK + B_HEAD_DIM, B_HEAD_DIM), BF16),
            pltpu.VMEM((n_chunks, B_HEADS, B_HEAD_DIM, B_HEAD_DIM), F32),
        ],
        compiler_params=pltpu.CompilerParams(
            dimension_semantics=("arbitrary", "arbitrary"), vmem_limit_bytes=VMEM_LIMIT),
        name="gdn",
    )(gdn_in, sm, alog_row, dtb_row, norm_g)


def _ffn_body(x_ref, oa_ref, ob_ref, wo_ref, g2_ref, w1_ref, w2_ref, g3_ref, out_ref, *,
              final_norm):
    y = (x_ref[...] + _dot(oa_ref[...], wo_ref[0:A_WIDTH, :])
         + _dot(ob_ref[...], wo_ref[A_WIDTH:, :]))
    h = _rms(y, g2_ref[...]).astype(BF16)
    a = jnp.square(jnp.maximum(_dot(h, w1_ref[...]), 0.0)).astype(BF16)
    acc = y + _dot(a, w2_ref[...])
    out_ref[...] = _rms(acc, g3_ref[...]) if final_norm else acc


def _ffn(x2, oa, ob, wo, g2, w1, w2, g3, tm, final_norm):
    m = x2.shape[0]
    row = lambda i: (i, 0)
    const = lambda i: (0, 0)
    resident = functools.partial(pl.BlockSpec, index_map=const, pipeline_mode=pl.Buffered(1))
    return pl.pallas_call(
        functools.partial(_ffn_body, final_norm=final_norm),
        grid=(m // tm,),
        in_specs=[
            pl.BlockSpec((tm, D_MODEL), row),
            pl.BlockSpec((tm, A_WIDTH), row),
            pl.BlockSpec((tm, B_WIDTH), row),
            resident((D_MODEL, D_MODEL)),
            pl.BlockSpec((1, D_MODEL), const),
            resident((D_MODEL, D_FF)),
            resident((D_FF, D_MODEL)),
            pl.BlockSpec((1, D_MODEL), const),
        ],
        out_specs=pl.BlockSpec((tm, D_MODEL), row),
        out_shape=jax.ShapeDtypeStruct((m, D_MODEL), F32),
        compiler_params=pltpu.CompilerParams(
            dimension_semantics=("arbitrary",), vmem_limit_bytes=VMEM_LIMIT),
        name="ffn",
    )(x2, oa, ob, wo, g2, w1, w2, g3)


def _rope_tables(seq):
    half = A_HEAD_DIM // 2
    inv_freq = 1.0 / (ROPE_THETA ** (jnp.arange(half, dtype=F32) / half))
    ang = jnp.arange(seq).astype(F32)[:, None] * inv_freq[None, :]
    cos = jnp.cos(ang)
    sin = jnp.sin(ang)
    reps = LANES // A_HEAD_DIM
    return (jnp.tile(cos, (1, 2 * reps)),
            jnp.tile(jnp.concatenate([-sin, sin], axis=1), (1, reps)))


def _lane_row(vals, offset):
    return jnp.zeros((1, LANES), F32).at[0, offset:offset + vals.shape[0]].set(vals.astype(F32))


def kernel(x, norm_mix_g, w_in, conv_w, a_log, dt_bias, gdn_norm_g, w_out,
           norm_ffn_g, w_ff1, w_ff2, norm_final_g):
    bsz, seq, d = x.shape
    depth = w_in.shape[0]
    m = bsz * seq
    cos_t, sin_t = _rope_tables(seq)
    x2 = x.reshape(m, d)
    for l in range(depth):
        qa, qi, kv, gdn_in, sm = _inproj(
            x2, norm_mix_g[l][None, :], jnp.swapaxes(w_in, 1, 2), l, cos_t, sin_t, conv_w[l],
            seq, tm=512)
        o_a = _mixer_a(qa, qi, kv, sm, bsz, seq)
        o_b = _gdn(gdn_in, sm, _lane_row(a_log[l], SM_DECAY),
                   _lane_row(dt_bias[l], SM_DECAY), gdn_norm_g[l][None, :], bsz, seq, tc=1024)
        x2 = _ffn(x2, o_a, o_b, w_out[l].astype(BF16), norm_ffn_g[l][None, :],
                  w_ff1[l].astype(BF16), w_ff2[l].astype(BF16), norm_final_g[None, :],
                  tm=512, final_norm=(l == depth - 1))
    return x2.reshape(bsz, seq, d)
```

```python
import functools

import jax
import jax.numpy as jnp
from jax import lax
from jax.experimental import pallas as pl
from jax.experimental.pallas import tpu as pltpu

F32 = jnp.float32
BF16 = jnp.bfloat16
I32 = jnp.int32

D_MODEL = 1024
CHUNK = 64
A_QUERIES = 256
ROPE_THETA = 10000.0
EPS = 1e-6
A_HEADS = 8
A_KV_HEADS = 2
A_HEAD_DIM = 64
IDX_HEADS = 8
IDX_DIM = 64
TOPK_MAX = 256
B_HEADS = 4
B_HEAD_DIM = 128
CONV_WIDTH = 4
D_FF = 4 * D_MODEL

LANES = 128
A_WIDTH = A_HEADS * A_HEAD_DIM
KV_WIDTH = A_KV_HEADS * A_HEAD_DIM
B_WIDTH = B_HEADS * B_HEAD_DIM
GDN_WIDTH = 4 * B_WIDTH
SM_KI = 0
SM_WI = IDX_DIM
SM_BETA = SM_WI + IDX_HEADS
SM_DECAY = SM_BETA + B_HEADS
C_QA = 0
C_QI = C_QA + A_WIDTH
C_KV = C_QI + A_WIDTH
C_GDN = C_KV + 2 * KV_WIDTH
C_SM = C_GDN + GDN_WIDTH
IN_COLS = C_SM + LANES
REF_SIZES = (A_WIDTH, KV_WIDTH, KV_WIDTH, IDX_HEADS * IDX_DIM, IDX_DIM, IDX_HEADS,
             B_WIDTH, B_WIDTH, B_WIDTH, B_WIDTH, B_HEADS, B_HEADS)
REF_OFF = tuple(sum(REF_SIZES[:i]) for i in range(len(REF_SIZES)))
IN_DIM = sum(REF_SIZES)
IN_WEIGHT_MOVES = (
    (C_QA, REF_OFF[0], A_WIDTH),
    (C_QI, REF_OFF[3], A_WIDTH),
    (C_KV, REF_OFF[1], 2 * KV_WIDTH),
    (C_GDN, REF_OFF[6], GDN_WIDTH),
    (C_SM + SM_KI, REF_OFF[4], IDX_DIM + IDX_HEADS),
    (C_SM + SM_BETA, REF_OFF[10], 2 * B_HEADS),
)
IN_PAD = IN_COLS - (C_SM + SM_DECAY + B_HEADS)

VMEM_LIMIT = 56 * 1024 * 1024
CONV_PAD = 8
COUNT_GROUP = 4
COUNT_ROWS = 32
V_ROWS = A_HEAD_DIM + 16
ATT_HEADS = 2
ATT_WAVE = 4
KEY_BLOCK = 2 * LANES
NEG_BIG = -1e30
LOG2_E = 1.4426950408889634
NEG_INF_KEY = -(2 ** 31) + 0x7FFFFF


def _rms(x, g):
    return x * lax.rsqrt(jnp.mean(x * x, axis=-1, keepdims=True) + EPS) * g


def _dot(a, b):
    return jnp.dot(a, b, preferred_element_type=F32)


def _dot_nt(a, b):
    return lax.dot_general(a, b, (((1,), (1,)), ((), ())), preferred_element_type=F32)


def _inproj_body(x_ref, g_ref, win_ref, cos_ref, sin_ref, cw_ref,
                 qa_ref, qi_ref, kv_ref, gdn_ref, sm_ref, xpad_ref, w_ref, *, tiles_per_seq):
    @pl.when(pl.program_id(0) == 0)
    def _():
        step = 256
        for dst, src, width in IN_WEIGHT_MOVES[:-2]:
            for r0 in range(0, width, step):
                w_ref[dst + r0:dst + r0 + step, :] = win_ref[src + r0:src + r0 + step, :].astype(BF16)
        small = [win_ref[src:src + width, :] for _, src, width in IN_WEIGHT_MOVES[-2:]]
        small.append(jnp.zeros((IN_PAD, D_MODEL), F32))
        w_ref[C_SM:IN_COLS, :] = jnp.concatenate(small, axis=0).astype(BF16)

    h = _rms(x_ref[...], g_ref[...]).astype(BF16)
    cos = cos_ref[...]
    sin = sin_ref[...]
    lane = lax.broadcasted_iota(I32, cos.shape, 1)
    first_half = (lane & (A_HEAD_DIM - 1)) < A_HEAD_DIM // 2

    def rope(t):
        swapped = jnp.where(first_half, pltpu.roll(t, LANES - A_HEAD_DIM // 2, 1),
                            pltpu.roll(t, A_HEAD_DIM // 2, 1))
        return t * cos + swapped * sin

    def proj(c0, width):
        return _dot_nt(h, w_ref[c0:c0 + width, :])

    tm = x_ref.shape[0]
    conv_cols = 3 * B_WIDTH

    @pl.when(pl.program_id(0) % tiles_per_seq == 0)
    def _():
        xpad_ref[0:CONV_PAD, :] = jnp.zeros((CONV_PAD, conv_cols), F32)

    @pl.when(pl.program_id(0) % tiles_per_seq != 0)
    def _():
        xpad_ref[0:CONV_PAD, :] = xpad_ref[tm:tm + CONV_PAD, :]

    def stage_conv_input(seg):
        xpad_ref[CONV_PAD:CONV_PAD + tm, seg * B_WIDTH:(seg + 1) * B_WIDTH] = proj(
            C_GDN + seg * B_WIDTH, B_WIDTH)

    def conv_segment(seg):
        for hh in range(B_HEADS):
            cs = slice(seg * B_WIDTH + hh * LANES, seg * B_WIDTH + (hh + 1) * LANES)
            xa = xpad_ref[:, cs]
            y = cw_ref[0:1, cs] * xa
            for jj in range(1, CONV_WIDTH):
                y = cw_ref[jj:jj + 1, cs] * xa + pltpu.roll(y, 1, 0)
            y = y[CONV_PAD:, :]
            y = y * jax.nn.sigmoid(y)
            if seg < 2:
                y = y * lax.rsqrt(jnp.sum(y * y, axis=-1, keepdims=True) + EPS)
            if seg == 0:
                y = y * (B_HEAD_DIM ** -0.5)
            gdn_ref[:, cs] = y

    stage_conv_input(0)
    acc = proj(C_QA, A_WIDTH)
    conv_segment(0)
    for j in range(A_WIDTH // LANES):
        sl = slice(j * LANES, (j + 1) * LANES)
        qa_ref[:, sl] = (rope(acc[:, sl]) * (A_HEAD_DIM ** -0.5 * LOG2_E)).astype(BF16)
    stage_conv_input(1)
    acc = proj(C_QI, A_WIDTH)
    conv_segment(1)
    for j in range(A_WIDTH // LANES):
        sl = slice(j * LANES, (j + 1) * LANES)
        qi_ref[:, sl] = rope(acc[:, sl]).astype(BF16)
    stage_conv_input(2)
    acc = proj(C_KV, 2 * KV_WIDTH)
    acc_sm = proj(C_SM, LANES)
    z = proj(C_GDN + conv_cols, B_WIDTH)
    conv_segment(2)
    kv_ref[:, 0:KV_WIDTH] = rope(acc[:, 0:KV_WIDTH]).astype(BF16)
    kv_ref[:, KV_WIDTH:] = acc[:, KV_WIDTH:].astype(BF16)
    sm_ref[...] = jnp.where(lane < IDX_DIM, rope(acc_sm), acc_sm)
    gdn_ref[:, conv_cols:] = z * jax.nn.sigmoid(z)


def _inproj(x2, g, w, layer, cos_t, sin_t, conv_w, seq, tm):
    m = x2.shape[0]
    assert seq % tm == 0, (seq, tm)
    nt = seq // tm
    row = lambda i: (i, 0)
    const = lambda i: (0, 0)
    return pl.pallas_call(
        functools.partial(_inproj_body, tiles_per_seq=nt),
        grid=(m // tm,),
        in_specs=[
            pl.BlockSpec((tm, D_MODEL), row),
            pl.BlockSpec((1, D_MODEL), const),
            pl.BlockSpec((None, IN_DIM, D_MODEL), lambda i: (layer, 0, 0),
                         pipeline_mode=pl.Buffered(1)),
            pl.BlockSpec((tm, LANES), lambda i: (i % nt, 0)),
            pl.BlockSpec((tm, LANES), lambda i: (i % nt, 0)),
            pl.BlockSpec((CONV_WIDTH, 3 * B_WIDTH), const),
        ],
        out_specs=[
            pl.BlockSpec((tm, A_WIDTH), row),
            pl.BlockSpec((tm, A_WIDTH), row),
            pl.BlockSpec((tm, 2 * KV_WIDTH), row),
            pl.BlockSpec((tm, GDN_WIDTH), row),
            pl.BlockSpec((tm, LANES), row),
        ],
        out_shape=[
            jax.ShapeDtypeStruct((m, A_WIDTH), BF16),
            jax.ShapeDtypeStruct((m, A_WIDTH), BF16),
            jax.ShapeDtypeStruct((m, 2 * KV_WIDTH), BF16),
            jax.ShapeDtypeStruct((m, GDN_WIDTH), F32),
            jax.ShapeDtypeStruct((m, LANES), F32),
        ],
        scratch_shapes=[
            pltpu.VMEM((tm + CONV_PAD, 3 * B_WIDTH), F32),
            pltpu.VMEM((IN_COLS, D_MODEL), BF16),
        ],
        compiler_params=pltpu.CompilerParams(
            dimension_semantics=("arbitrary",), vmem_limit_bytes=VMEM_LIMIT),
        name="inproj",
    )(x2, g, w, cos_t, sin_t, conv_w)


def _mixer_a_body(qa_ref, qi_ref, kv_ref, sm_ref, o_ref,
                  isc_ref, isb_ref, vt_ref, qit_ref, qat_ref, acc_ref, bias_ref, *, seq, topk):
    j = pl.program_id(1)
    nkt = (j + 1) * (A_QUERIES // LANES)
    n_heads_pair = A_WIDTH // LANES
    group = A_HEADS // A_KV_HEADS

    @pl.when(j == 0)
    def _():
        per_block = KEY_BLOCK // LANES
        for t in range(seq // LANES):
            vt = kv_ref[t * LANES:(t + 1) * LANES, KV_WIDTH:].astype(F32).T.astype(BF16)
            c0 = (t % per_block) * LANES
            for g in range(A_KV_HEADS):
                vt_ref[t // per_block, g * V_ROWS:g * V_ROWS + A_HEAD_DIM, c0:c0 + LANES] = (
                    vt[g * A_HEAD_DIM:(g + 1) * A_HEAD_DIM])
        for g in range(A_KV_HEADS):
            vt_ref[:, g * V_ROWS + A_HEAD_DIM:(g + 1) * V_ROWS, :] = jnp.ones(
                (seq // KEY_BLOCK, V_ROWS - A_HEAD_DIM, KEY_BLOCK), BF16)

    for p in range(n_heads_pair):
        sl = slice(p * LANES, (p + 1) * LANES)
        t = qi_ref[:, sl].astype(F32).T
        qit_ref[:, (2 * p) * A_QUERIES:(2 * p + 1) * A_QUERIES] = t[0:IDX_DIM].astype(BF16)
        qit_ref[:, (2 * p + 1) * A_QUERIES:(2 * p + 2) * A_QUERIES] = t[IDX_DIM:].astype(BF16)
        t = qa_ref[:, sl].astype(F32).T
        qat_ref[:, (2 * p) * A_QUERIES:(2 * p + 1) * A_QUERIES] = t[0:A_HEAD_DIM].astype(BF16)
        qat_ref[:, (2 * p + 1) * A_QUERIES:(2 * p + 2) * A_QUERIES] = t[A_HEAD_DIM:].astype(BF16)

    q0 = pl.multiple_of(j * A_QUERIES, A_QUERIES)
    w_t = sm_ref[pl.ds(q0, A_QUERIES), :].T[SM_WI:SM_WI + IDX_HEADS, :]
    w_t = w_t * ((IDX_HEADS ** -0.5) * (IDX_DIM ** -0.5))

    qlane = lax.broadcasted_iota(I32, (1, A_QUERIES), 1)
    limit = q0 + (lax.shift_right_logical(qlane, CHUNK.bit_length() - 1) + 1) * CHUNK
    tile_iota = lax.broadcasted_iota(I32, (LANES, A_QUERIES), 0)
    block_iota = lax.broadcasted_iota(I32, (KEY_BLOCK, A_QUERIES), 0)
    nkb = lax.shift_right_logical(nkt + (KEY_BLOCK // LANES - 1), (KEY_BLOCK // LANES).bit_length() - 1)

    def isc_body(kb, carry):
        r0 = pl.multiple_of(kb * KEY_BLOCK, KEY_BLOCK)
        kid = sm_ref[pl.ds(r0, KEY_BLOCK), SM_KI:SM_KI + IDX_DIM].astype(BF16)
        rel = _dot(kid, qit_ref[...])
        acc = jnp.zeros((KEY_BLOCK, A_QUERIES), F32)
        for h in range(IDX_HEADS):
            acc = acc + w_t[h:h + 1, :] * jnp.maximum(rel[:, h * A_QUERIES:(h + 1) * A_QUERIES], 0.0)
        acc = jnp.where(r0 + block_iota < limit, acc, -jnp.inf)
        isc_ref[pl.ds(r0, KEY_BLOCK), :] = acc
        isb_ref[pl.ds(r0, KEY_BLOCK), :] = acc.astype(BF16)
        return carry

    lax.fori_loop(0, nkb, isc_body, 0)

    ngrp = lax.shift_right_logical(nkt + (COUNT_GROUP - 1), COUNT_GROUP.bit_length() - 1)

    def fill_body(kt, carry):
        r0 = pl.multiple_of(kt * LANES, LANES)
        isc_ref[pl.ds(r0, LANES), :] = jnp.full((LANES, A_QUERIES), -jnp.inf, F32)
        isb_ref[pl.ds(r0, LANES), :] = jnp.full((LANES, A_QUERIES), -jnp.inf, BF16)
        return carry

    lax.fori_loop(nkb * (KEY_BLOCK // LANES), ngrp * COUNT_GROUP, fill_body, 0)

    def count(pred):
        def body(g, acc):
            for t in range(COUNT_GROUP):
                r0 = pl.multiple_of((g * COUNT_GROUP + t) * LANES, LANES)
                hit = jnp.where(pred(isc_ref[pl.ds(r0, LANES), :], r0 + tile_iota), 1.0, 0.0)
                acc = acc + hit.reshape(LANES // COUNT_ROWS, COUNT_ROWS, A_QUERIES).sum(axis=0)
            return acc
        acc = lax.fori_loop(0, ngrp, body, jnp.zeros((COUNT_ROWS, A_QUERIES), F32))
        return acc.sum(axis=0, keepdims=True)

    def count_coarse(thr):
        def body(g, acc):
            for t in range(COUNT_GROUP):
                r0 = pl.multiple_of((g * COUNT_GROUP + t) * LANES, LANES)
                hit = jnp.where(isb_ref[pl.ds(r0, LANES), :] >= thr, one_b, zero_b)
                hit = hit.reshape(LANES // COUNT_ROWS, COUNT_ROWS, A_QUERIES)
                part = hit[0]
                for i in range(1, LANES // COUNT_ROWS):
                    part = part + hit[i]
                acc = acc + part
            return acc
        acc = lax.fori_loop(0, ngrp, body, jnp.zeros((COUNT_ROWS, A_QUERIES), BF16))
        return acc.astype(F32).sum(axis=0, keepdims=True)

    kf = float(topk)
    one_b = jnp.ones((), BF16)
    zero_b = jnp.zeros((), BF16)

    def thr_coarse(key):
        bits = jnp.where(key >= 0, key, key ^ jnp.int32(0x7FFF))
        return lax.bitcast_convert_type(lax.shift_left(bits, 16), F32).astype(BF16)

    c0 = count_coarse(jnp.zeros((1, A_QUERIES), BF16))
    lo16 = jnp.where(c0 >= kf, jnp.int32(0), jnp.int32(-2 ** 15))

    def coarse_body(i, lo):
        trial = lo | lax.shift_left(jnp.int32(1), 14 - i)
        c = count_coarse(thr_coarse(trial))
        return jnp.where(c >= kf, trial, lo)

    lo16 = lax.fori_loop(0, 15, coarse_body, lo16)
    lo16 = jnp.maximum(lo16, jnp.int32(NEG_INF_KEY >> 16))
    center = lax.shift_left(lo16, 16) | jnp.where(lo16 < 0, jnp.int32(0xFFFF), jnp.int32(0))

    def thr_of(key):
        bits = jnp.where(key >= 0, key, key ^ jnp.int32(0x7FFFFFFF))
        return jnp.where(key < jnp.int32(NEG_INF_KEY), -jnp.inf, lax.bitcast_convert_type(bits, F32))

    def fine_body(i, st):
        lo, c_lo = st
        trial = lo + lax.shift_left(jnp.int32(1), 16 - i)
        thr = thr_of(trial)
        c = count(lambda x, s: x >= thr)
        ok = c >= kf
        return jnp.where(ok, trial, lo), jnp.where(ok, c, c_lo)

    lo, c_lo = lax.fori_loop(0, 17, fine_body,
                             (center - jnp.int32(1 << 16), jnp.full((1, A_QUERIES), -1.0, F32)))
    kth = thr_of(lo)

    idx_bits = (seq - 1).bit_length()

    def tie_search():
        need = kf - count(lambda x, s: x > kth)

        def tie_body(i, last):
            trial = last | lax.shift_left(jnp.int32(1), idx_bits - 1 - i)
            c = count(lambda x, s: (x == kth) & (s < trial))
            return jnp.where(c < need, trial, last)

        return lax.fori_loop(0, idx_bits, tie_body, jnp.zeros((1, A_QUERIES), I32))

    def take_all_ties():
        return jnp.full((1, A_QUERIES), (1 << idx_bits) - 1, I32)

    n_ge = jnp.where(c_lo < 0.0, jnp.inf, c_lo)
    last = lax.cond(jnp.max(n_ge) > kf, tie_search, take_all_ties)

    acc_ref[...] = jnp.zeros_like(acc_ref)

    def att_body(kb, m_prev):
        r0 = pl.multiple_of(kb * KEY_BLOCK, KEY_BLOCK)
        x = isc_ref[pl.ds(r0, KEY_BLOCK), :]
        s_idx = r0 + block_iota
        sel = (s_idx < limit) & ((x > kth) | ((x == kth) & (s_idx <= last)))
        bias_ref[...] = jnp.where(sel, 0.0, NEG_BIG)
        k_tile = kv_ref[pl.ds(r0, KEY_BLOCK), 0:KV_WIDTH]
        v_t = vt_ref[kb]

        units = list(range(A_HEADS // ATT_HEADS))
        uw = ATT_HEADS * A_QUERIES
        us = [slice(u * uw, (u + 1) * uw) for u in units]
        kd = [slice((u * ATT_HEADS // group) * A_HEAD_DIM, (u * ATT_HEADS // group + 1) * A_HEAD_DIM)
              for u in units]
        vr = [slice((u * ATT_HEADS // group) * V_ROWS, (u * ATT_HEADS // group + 1) * V_ROWS)
              for u in units]
        cs = [slice((u * ATT_HEADS % group) * A_QUERIES, (u * ATT_HEADS % group + ATT_HEADS) * A_QUERIES)
              for u in units]
        m_new = {}
        for w0 in range(0, len(units), ATT_WAVE):
            wave = units[w0:w0 + ATT_WAVE]
            s = {u: _dot(k_tile[:, kd[u]], qat_ref[:, us[u]])
                 + jnp.concatenate([bias_ref[...]] * ATT_HEADS, axis=1) for u in wave}
            for u in wave:
                m_new[u] = jnp.maximum(m_prev[:, us[u]], jnp.max(s[u], axis=0, keepdims=True))
            alpha = {u: jnp.exp2(m_prev[:, us[u]] - m_new[u]) for u in wave}
            p = {u: jnp.exp2(s[u] - m_new[u]).astype(BF16) for u in wave}
            pv = {u: _dot(v_t[vr[u], :], p[u]) for u in wave}
            for u in wave:
                acc_ref[vr[u], cs[u]] = alpha[u] * acc_ref[vr[u], cs[u]] + pv[u]
        return jnp.concatenate([m_new[u] for u in units], axis=1)

    lax.fori_loop(0, nkb, att_body, jnp.full((1, A_HEADS * A_QUERIES), NEG_BIG, F32))

    for p in range(n_heads_pair):
        g = (2 * p) // group
        parts = []
        for h in (2 * p, 2 * p + 1):
            hq = slice((h % group) * A_QUERIES, (h % group + 1) * A_QUERIES)
            parts.append(acc_ref[g * V_ROWS:g * V_ROWS + A_HEAD_DIM, hq]
                         / acc_ref[g * V_ROWS + A_HEAD_DIM:g * V_ROWS + A_HEAD_DIM + 1, hq])
        o_ref[:, p * LANES:(p + 1) * LANES] = jnp.concatenate(parts, axis=0).T.astype(BF16)


def _mixer_a(qa, qi, kv, sm, bsz, seq):
    nq = seq // A_QUERIES
    assert seq % (COUNT_GROUP * LANES) == 0, seq
    assert seq // COUNT_ROWS <= 256, seq
    topk = min(TOPK_MAX, seq // 4)
    qrow = lambda b, j: (b * nq + j, 0)
    brow = lambda b, j: (b, 0)
    return pl.pallas_call(
        functools.partial(_mixer_a_body, seq=seq, topk=topk),
        grid=(bsz, nq),
        in_specs=[
            pl.BlockSpec((A_QUERIES, A_WIDTH), qrow),
            pl.BlockSpec((A_QUERIES, A_WIDTH), qrow),
            pl.BlockSpec((seq, 2 * KV_WIDTH), brow),
            pl.BlockSpec((seq, LANES), brow),
        ],
        out_specs=pl.BlockSpec((A_QUERIES, A_WIDTH), qrow),
        out_shape=jax.ShapeDtypeStruct((bsz * seq, A_WIDTH), BF16),
        scratch_shapes=[
            pltpu.VMEM((seq, A_QUERIES), F32),
            pltpu.VMEM((seq, A_QUERIES), BF16),
            pltpu.VMEM((seq // KEY_BLOCK, A_KV_HEADS * V_ROWS, KEY_BLOCK), BF16),
            pltpu.VMEM((IDX_DIM, IDX_HEADS * A_QUERIES), BF16),
            pltpu.VMEM((A_HEAD_DIM, A_HEADS * A_QUERIES), BF16),
            pltpu.VMEM((A_KV_HEADS * V_ROWS, (A_HEADS // A_KV_HEADS) * A_QUERIES), F32),
            pltpu.VMEM((KEY_BLOCK, A_QUERIES), F32),
        ],
        compiler_params=pltpu.CompilerParams(
            dimension_semantics=("arbitrary", "arbitrary"), vmem_limit_bytes=VMEM_LIMIT),
        name="mixer_a",
    )(qa, qi, kv, sm)


def _gdn_body(gdn_ref, sm_ref, alog_ref, dtb_ref, ng_ref, o_ref,
              gate_ref, gct_ref, state_ref, oc_ref, lhs_ref, bm_ref, *, tc):
    tb = pl.program_id(1)
    n_chunks = tc // CHUNK
    prep_unroll = 8
    q_col, k_col, v_col, z_col = (i * B_WIDTH for i in range(4))

    @pl.when(tb == 0)
    def _():
        state_ref[...] = jnp.zeros_like(state_ref)

    sm = sm_ref[...]
    beta = jax.nn.sigmoid(sm)
    z = sm + dtb_ref[...]
    softplus = jnp.maximum(z, 0.0) + jnp.log(1.0 + jnp.exp(-jnp.abs(z)))
    g = -jnp.exp(alog_ref[...]) * softplus
    rin = lax.broadcasted_iota(I32, (tc, LANES), 0) & (CHUNK - 1)
    gc = g
    step = 1
    while step < CHUNK:
        gc = gc + jnp.where(rin >= step, pltpu.roll(gc, step, 0), 0.0)
        step *= 2
    gc3 = gc.reshape(n_chunks, CHUNK, LANES)
    g_last = jnp.broadcast_to(gc3[:, CHUNK - 1:CHUNK, :], gc3.shape).reshape(tc, LANES)
    gate_ref[0] = beta
    gate_ref[1] = gc
    gate_ref[2] = jnp.exp(gc)
    gate_ref[3] = jnp.exp(g_last - gc)
    gate_ref[4] = jnp.exp(g_last)
    for i in range(tc // LANES):
        t = gc[i * LANES:(i + 1) * LANES, :].T
        for half in range(LANES // CHUNK):
            gct_ref[i * (LANES // CHUNK) + half] = t[:, half * CHUNK:(half + 1) * CHUNK]

    ci = lax.broadcasted_iota(I32, (CHUNK, CHUNK), 0)
    si = lax.broadcasted_iota(I32, (CHUNK, CHUNK), 1)
    wl = lax.broadcasted_iota(I32, (CHUNK, 2 * CHUNK), 1)
    wr = lax.broadcasted_iota(I32, (CHUNK, 2 * CHUNK), 0)
    right = wl >= CHUNK
    eye_right = jnp.where(wl == wr + CHUNK, 1.0, 0.0)

    def prep_body(cg, carry):
        units = [(cg * prep_unroll + cc, h) for cc in range(prep_unroll) for h in range(B_HEADS)]
        rows = [pl.ds(pl.multiple_of(c * CHUNK, CHUNK), CHUNK) for c, _ in units]
        hsl = [slice(h * LANES, (h + 1) * LANES) for _, h in units]
        idx = range(len(units))

        def col(i, gate, off):
            h = units[i][1]
            return gate_ref[gate, rows[i], off + h:off + h + 1]

        def seg(i, col0):
            h = units[i][1]
            return gdn_ref[rows[i], col0 + h * LANES:col0 + (h + 1) * LANES]

        q = [seg(i, q_col) for i in idx]
        k = [seg(i, k_col) for i in idx]
        kb = [k[i] * col(i, 0, SM_BETA) for i in idx]
        kq = [_dot_nt(jnp.concatenate([kb[i], q[i]], axis=0).astype(BF16), k[i].astype(BF16))
              for i in idx]
        decay = []
        for i, (c, h) in enumerate(units):
            d = col(i, 1, SM_DECAY) - gct_ref[c][SM_DECAY + h:SM_DECAY + h + 1, :]
            decay.append(jnp.where(ci >= si, jnp.exp(jnp.where(ci >= si, d, 0.0)), 0.0))
        wmat = []
        for i, (c, h) in enumerate(units):
            n_mat = jnp.where(ci > si, -(kq[i][0:CHUNK] * decay[i]), 0.0)
            wmat.append(jnp.concatenate([n_mat, jnp.zeros_like(n_mat)], axis=1) + eye_right)
        pw = 1
        while pw < CHUNK:
            wb = [wmat[i].astype(BF16) for i in idx]
            wmat = [_dot(wb[i][:, 0:CHUNK], wb[i]) + jnp.where(right, wmat[i], 0.0) for i in idx]
            pw *= 2
        eg = [col(i, 2, SM_DECAY) for i in idx]
        rhs = [jnp.concatenate([seg(i, v_col) * col(i, 0, SM_BETA), kb[i] * eg[i]],
                               axis=1).astype(BF16) for i in idx]
        sol = [_dot(wmat[i][:, CHUNK:].astype(BF16), rhs[i]).astype(BF16) for i in idx]
        att = [(kq[i][CHUNK:] * decay[i]).astype(BF16) for i in idx]
        k_tail_t = [(k[i] * col(i, 3, SM_DECAY)).T.astype(BF16) for i in idx]
        a_uw = [_dot(att[i], sol[i]) for i in idx]
        k_uw = [_dot(k_tail_t[i], sol[i]) for i in idx]
        for i, (c, h) in enumerate(units):
            oc_ref[rows[i], hsl[i]] = a_uw[i][:, 0:B_HEAD_DIM]
            lhs_ref[c, h, 0:CHUNK, :] = (q[i] * eg[i] - a_uw[i][:, B_HEAD_DIM:]).astype(BF16)
            lhs_ref[c, h, CHUNK:, :] = k_uw[i][:, B_HEAD_DIM:].astype(BF16)
            bm_ref[c, h] = k_uw[i][:, 0:B_HEAD_DIM]
        return carry

    lax.fori_loop(0, n_chunks // prep_unroll, prep_body, 0)

    def scan_body(c, carry):
        r0 = pl.multiple_of(c * CHUNK, CHUNK)
        rows = pl.ds(r0, CHUNK)
        heads = range(B_HEADS)
        hsl = [slice(h * LANES, (h + 1) * LANES) for h in heads]
        s_prev = [state_ref[h] for h in heads]
        r = [_dot(lhs_ref[c, h], s_prev[h].astype(BF16)) for h in heads]
        for h in heads:
            gl = gate_ref[4, rows, SM_DECAY + h:SM_DECAY + h + 1][0:1, :]
            oc_ref[rows, hsl[h]] = oc_ref[rows, hsl[h]] + r[h][0:CHUNK]
            state_ref[h] = s_prev[h] * gl + bm_ref[c, h] - r[h][CHUNK:]
        return carry

    lax.fori_loop(0, n_chunks, scan_body, 0)

    for h in range(B_HEADS):
        hs = slice(h * LANES, (h + 1) * LANES)
        z_gate = gdn_ref[:, z_col + h * LANES:z_col + (h + 1) * LANES]
        o_ref[:, hs] = (_rms(oc_ref[:, hs], ng_ref[...]) * z_gate).astype(BF16)


def _gdn(gdn_in, sm, alog_row, dtb_row, norm_g, bsz, seq, tc):
    nt = seq // tc
    trow = lambda b, t: (b * nt + t, 0)
    const = lambda b, t: (0, 0)
    n_chunks = tc // CHUNK
    return pl.pallas_call(
        functools.partial(_gdn_body, tc=tc),
        grid=(bsz, nt),
        in_specs=[
            pl.BlockSpec((tc, GDN_WIDTH), trow),
            pl.BlockSpec((tc, LANES), trow),
            pl.BlockSpec((1, LANES), const),
            pl.BlockSpec((1, LANES), const),
            pl.BlockSpec((1, B_HEAD_DIM), const),
        ],
        out_specs=pl.BlockSpec((tc, B_WIDTH), trow),
        out_shape=jax.ShapeDtypeStruct((bsz * seq, B_WIDTH), BF16),
        scratch_shapes=[
            pltpu.VMEM((5, tc, LANES), F32),
            pltpu.VMEM((n_chunks, LANES, CHUNK), F32),
            pltpu.VMEM((B_HEADS, B_HEAD_DIM, B_HEAD_DIM), F32),
            pltpu.VMEM((tc, B_WIDTH), F32),
            pltpu.VMEM((n_chunks, B_HEADS, CHUNK + B_HEAD_DIM, B_HEAD_DIM), BF16),
            pltpu.VMEM((n_chunks, B_HEADS, B_HEAD_DIM, B_HEAD_DIM), F32),
        ],
        compiler_params=pltpu.CompilerParams(
            dimension_semantics=("arbitrary", "arbitrary"), vmem_limit_bytes=VMEM_LIMIT),
        name="gdn",
    )(gdn_in, sm, alog_row, dtb_row, norm_g)


def _ffn_body(x_ref, oa_ref, ob_ref, wo_ref, g2_ref, w1_ref, w2_ref, g3_ref, out_ref, *,
              final_norm):
    y = (x_ref[...] + _dot(oa_ref[...], wo_ref[0:A_WIDTH, :])
         + _dot(ob_ref[...], wo_ref[A_WIDTH:, :]))
    h = _rms(y, g2_ref[...]).astype(BF16)
    a = jnp.square(jnp.maximum(_dot(h, w1_ref[...]), 0.0)).astype(BF16)
    acc = y + _dot(a, w2_ref[...])
    out_ref[...] = _rms(acc, g3_ref[...]) if final_norm else acc


def _ffn(x2, oa, ob, wo, g2, w1, w2, g3, tm, final_norm):
    m = x2.shape[0]
    row = lambda i: (i, 0)
    const = lambda i: (0, 0)
    resident = functools.partial(pl.BlockSpec, index_map=const, pipeline_mode=pl.Buffered(1))
    return pl.pallas_call(
        functools.partial(_ffn_body, final_norm=final_norm),
        grid=(m // tm,),
        in_specs=[
            pl.BlockSpec((tm, D_MODEL), row),
            pl.BlockSpec((tm, A_WIDTH), row),
            pl.BlockSpec((tm, B_WIDTH), row),
            resident((D_MODEL, D_MODEL)),
            pl.BlockSpec((1, D_MODEL), const),
            resident((D_MODEL, D_FF)),
            resident((D_FF, D_MODEL)),
            pl.BlockSpec((1, D_MODEL), const),
        ],
        out_specs=pl.BlockSpec((tm, D_MODEL), row),
        out_shape=jax.ShapeDtypeStruct((m, D_MODEL), F32),
        compiler_params=pltpu.CompilerParams(
            dimension_semantics=("arbitrary",), vmem_limit_bytes=VMEM_LIMIT),
        name="ffn",
    )(x2, oa, ob, wo, g2, w1, w2, g3)


def _rope_tables(seq):
    half = A_HEAD_DIM // 2
    inv_freq = 1.0 / (ROPE_THETA ** (jnp.arange(half, dtype=F32) / half))
    ang = jnp.arange(seq).astype(F32)[:, None] * inv_freq[None, :]
    cos = jnp.cos(ang)
    sin = jnp.sin(ang)
    reps = LANES // A_HEAD_DIM
    return (jnp.tile(cos, (1, 2 * reps)),
            jnp.tile(jnp.concatenate([-sin, sin], axis=1), (1, reps)))


def _lane_row(vals, offset):
    return jnp.zeros((1, LANES), F32).at[0, offset:offset + vals.shape[0]].set(vals.astype(F32))


def kernel(x, norm_mix_g, w_in, conv_w, a_log, dt_bias, gdn_norm_g, w_out,
           norm_ffn_g, w_ff1, w_ff2, norm_final_g):
    bsz, seq, d = x.shape
    depth = w_in.shape[0]
    m = bsz * seq
    cos_t, sin_t = _rope_tables(seq)
    x2 = x.reshape(m, d)
    for l in range(depth):
        qa, qi, kv, gdn_in, sm = _inproj(
            x2, norm_mix_g[l][None, :], jnp.swapaxes(w_in, 1, 2), l, cos_t, sin_t, conv_w[l],
            seq, tm=512)
        o_a = _mixer_a(qa, qi, kv, sm, bsz, seq)
        o_b = _gdn(gdn_in, sm, _lane_row(a_log[l], SM_DECAY),
                   _lane_row(dt_bias[l], SM_DECAY), gdn_norm_g[l][None, :], bsz, seq, tc=512)
        x2 = _ffn(x2, o_a, o_b, w_out[l].astype(BF16), norm_ffn_g[l][None, :],
                  w_ff1[l].astype(BF16), w_ff2[l].astype(BF16), norm_final_g[None, :],
                  tm=512, final_norm=(l == depth - 1))
    return x2.reshape(bsz, seq, d)
```

```python
import functools

import jax
import jax.numpy as jnp
from jax import lax
from jax.experimental import pallas as pl
from jax.experimental.pallas import tpu as pltpu

F32 = jnp.float32
BF16 = jnp.bfloat16
I32 = jnp.int32

D_MODEL = 1024
CHUNK = 64
A_QUERIES = 256
ROPE_THETA = 10000.0
EPS = 1e-6
A_HEADS = 8
A_KV_HEADS = 2
A_HEAD_DIM = 64
IDX_HEADS = 8
IDX_DIM = 64
TOPK_MAX = 256
B_HEADS = 4
B_HEAD_DIM = 128
CONV_WIDTH = 4
D_FF = 4 * D_MODEL

LANES = 128
A_WIDTH = A_HEADS * A_HEAD_DIM
KV_WIDTH = A_KV_HEADS * A_HEAD_DIM
B_WIDTH = B_HEADS * B_HEAD_DIM
GDN_WIDTH = 4 * B_WIDTH
SM_KI = 0
SM_WI = IDX_DIM
SM_BETA = SM_WI + IDX_HEADS
SM_DECAY = SM_BETA + B_HEADS
C_QA = 0
C_QI = C_QA + A_WIDTH
C_KV = C_QI + A_WIDTH
C_GDN = C_KV + 2 * KV_WIDTH
C_SM = C_GDN + GDN_WIDTH
IN_COLS = C_SM + LANES
REF_SIZES = (A_WIDTH, KV_WIDTH, KV_WIDTH, IDX_HEADS * IDX_DIM, IDX_DIM, IDX_HEADS,
             B_WIDTH, B_WIDTH, B_WIDTH, B_WIDTH, B_HEADS, B_HEADS)
REF_OFF = tuple(sum(REF_SIZES[:i]) for i in range(len(REF_SIZES)))
IN_DIM = sum(REF_SIZES)
IN_WEIGHT_MOVES = (
    (C_QA, REF_OFF[0], A_WIDTH),
    (C_QI, REF_OFF[3], A_WIDTH),
    (C_KV, REF_OFF[1], 2 * KV_WIDTH),
    (C_GDN, REF_OFF[6], GDN_WIDTH),
    (C_SM + SM_KI, REF_OFF[4], IDX_DIM + IDX_HEADS),
    (C_SM + SM_BETA, REF_OFF[10], 2 * B_HEADS),
)
IN_PAD = IN_COLS - (C_SM + SM_DECAY + B_HEADS)

VMEM_LIMIT = 56 * 1024 * 1024
CONV_PAD = 8
COUNT_GROUP = 4
COUNT_ROWS = 32
V_ROWS = A_HEAD_DIM + 16
ATT_HEADS = 2
ATT_WAVE = 4
KEY_BLOCK = 2 * LANES
NEG_BIG = -1e30
LOG2_E = 1.4426950408889634
NEG_INF_KEY = -(2 ** 31) + 0x7FFFFF


def _rms(x, g):
    return x * lax.rsqrt(jnp.mean(x * x, axis=-1, keepdims=True) + EPS) * g


def _dot(a, b):
    return jnp.dot(a, b, preferred_element_type=F32)


def _dot_nt(a, b):
    return lax.dot_general(a, b, (((1,), (1,)), ((), ())), preferred_element_type=F32)


def _inproj_body(x_ref, g_ref, win_ref, cos_ref, sin_ref, cw_ref,
                 qa_ref, qi_ref, kv_ref, gdn_ref, sm_ref, xpad_ref, w_ref, *, tiles_per_seq):
    @pl.when(pl.program_id(0) == 0)
    def _():
        step = 256
        for dst, src, width in IN_WEIGHT_MOVES[:-2]:
            for r0 in range(0, width, step):
                w_ref[dst + r0:dst + r0 + step, :] = win_ref[src + r0:src + r0 + step, :].astype(BF16)
        small = [win_ref[src:src + width, :] for _, src, width in IN_WEIGHT_MOVES[-2:]]
        small.append(jnp.zeros((IN_PAD, D_MODEL), F32))
        w_ref[C_SM:IN_COLS, :] = jnp.concatenate(small, axis=0).astype(BF16)

    h = _rms(x_ref[...], g_ref[...]).astype(BF16)
    cos = cos_ref[...]
    sin = sin_ref[...]
    lane = lax.broadcasted_iota(I32, cos.shape, 1)
    first_half = (lane & (A_HEAD_DIM - 1)) < A_HEAD_DIM // 2

    def rope(t):
        swapped = jnp.where(first_half, pltpu.roll(t, LANES - A_HEAD_DIM // 2, 1),
                            pltpu.roll(t, A_HEAD_DIM // 2, 1))
        return t * cos + swapped * sin

    def proj(c0, width):
        return _dot_nt(h, w_ref[c0:c0 + width, :])

    tm = x_ref.shape[0]
    conv_cols = 3 * B_WIDTH

    @pl.when(pl.program_id(0) % tiles_per_seq == 0)
    def _():
        xpad_ref[0:CONV_PAD, :] = jnp.zeros((CONV_PAD, conv_cols), F32)

    @pl.when(pl.program_id(0) % tiles_per_seq != 0)
    def _():
        xpad_ref[0:CONV_PAD, :] = xpad_ref[tm:tm + CONV_PAD, :]

    def stage_conv_input(seg):
        xpad_ref[CONV_PAD:CONV_PAD + tm, seg * B_WIDTH:(seg + 1) * B_WIDTH] = proj(
            C_GDN + seg * B_WIDTH, B_WIDTH)

    def conv_segment(seg):
        for hh in range(B_HEADS):
            cs = slice(seg * B_WIDTH + hh * LANES, seg * B_WIDTH + (hh + 1) * LANES)
            xa = xpad_ref[:, cs]
            y = cw_ref[0:1, cs] * xa
            for jj in range(1, CONV_WIDTH):
                y = cw_ref[jj:jj + 1, cs] * xa + pltpu.roll(y, 1, 0)
            y = y[CONV_PAD:, :]
            y = y * jax.nn.sigmoid(y)
            if seg < 2:
                y = y * lax.rsqrt(jnp.sum(y * y, axis=-1, keepdims=True) + EPS)
            if seg == 0:
                y = y * (B_HEAD_DIM ** -0.5)
            gdn_ref[:, cs] = y

    stage_conv_input(0)
    acc = proj(C_QA, A_WIDTH)
    conv_segment(0)
    for j in range(A_WIDTH // LANES):
        sl = slice(j * LANES, (j + 1) * LANES)
        qa_ref[:, sl] = (rope(acc[:, sl]) * (A_HEAD_DIM ** -0.5 * LOG2_E)).astype(BF16)
    stage_conv_input(1)
    acc = proj(C_QI, A_WIDTH)
    conv_segment(1)
    for j in range(A_WIDTH // LANES):
        sl = slice(j * LANES, (j + 1) * LANES)
        qi_ref[:, sl] = rope(acc[:, sl]).astype(BF16)
    stage_conv_input(2)
    acc = proj(C_KV, 2 * KV_WIDTH)
    acc_sm = proj(C_SM, LANES)
    z = proj(C_GDN + conv_cols, B_WIDTH)
    conv_segment(2)
    kv_ref[:, 0:KV_WIDTH] = rope(acc[:, 0:KV_WIDTH]).astype(BF16)
    kv_ref[:, KV_WIDTH:] = acc[:, KV_WIDTH:].astype(BF16)
    sm_ref[...] = jnp.where(lane < IDX_DIM, rope(acc_sm), acc_sm)
    gdn_ref[:, conv_cols:] = z * jax.nn.sigmoid(z)


def _inproj(x2, g, w, layer, cos_t, sin_t, conv_w, seq, tm):
    m = x2.shape[0]
    assert seq % tm == 0, (seq, tm)
    nt = seq // tm
    row = lambda i: (i, 0)
    const = lambda i: (0, 0)
    return pl.pallas_call(
        functools.partial(_inproj_body, tiles_per_seq=nt),
        grid=(m // tm,),
        in_specs=[
            pl.BlockSpec((tm, D_MODEL), row),
            pl.BlockSpec((1, D_MODEL), const),
            pl.BlockSpec((None, IN_DIM, D_MODEL), lambda i: (layer, 0, 0),
                         pipeline_mode=pl.Buffered(1)),
            pl.BlockSpec((tm, LANES), lambda i: (i % nt, 0)),
            pl.BlockSpec((tm, LANES), lambda i: (i % nt, 0)),
            pl.BlockSpec((CONV_WIDTH, 3 * B_WIDTH), const),
        ],
        out_specs=[
            pl.BlockSpec((tm, A_WIDTH), row),
            pl.BlockSpec((tm, A_WIDTH), row),
            pl.BlockSpec((tm, 2 * KV_WIDTH), row),
            pl.BlockSpec((tm, GDN_WIDTH), row),
            pl.BlockSpec((tm, LANES), row),
        ],
        out_shape=[
            jax.ShapeDtypeStruct((m, A_WIDTH), BF16),
            jax.ShapeDtypeStruct((m, A_WIDTH), BF16),
            jax.ShapeDtypeStruct((m, 2 * KV_WIDTH), BF16),
            jax.ShapeDtypeStruct((m, GDN_WIDTH), F32),
            jax.ShapeDtypeStruct((m, LANES), F32),
        ],
        scratch_shapes=[
            pltpu.VMEM((tm + CONV_PAD, 3 * B_WIDTH), F32),
            pltpu.VMEM((IN_COLS, D_MODEL), BF16),
        ],
        compiler_params=pltpu.CompilerParams(
            dimension_semantics=("arbitrary",), vmem_limit_bytes=VMEM_LIMIT),
        name="inproj",
    )(x2, g, w, cos_t, sin_t, conv_w)


def _mixer_a_body(qa_ref, qi_ref, kv_ref, sm_ref, o_ref,
                  isc_ref, isb_ref, vt_ref, qit_ref, qat_ref, acc_ref, bias_ref, *, seq, topk):
    j = pl.program_id(1)
    nkt = (j + 1) * (A_QUERIES // LANES)
    n_heads_pair = A_WIDTH // LANES
    group = A_HEADS // A_KV_HEADS

    @pl.when(j == 0)
    def _():
        per_block = KEY_BLOCK // LANES
        for t in range(seq // LANES):
            vt = kv_ref[t * LANES:(t + 1) * LANES, KV_WIDTH:].astype(F32).T.astype(BF16)
            c0 = (t % per_block) * LANES
            for g in range(A_KV_HEADS):
                vt_ref[t // per_block, g * V_ROWS:g * V_ROWS + A_HEAD_DIM, c0:c0 + LANES] = (
                    vt[g * A_HEAD_DIM:(g + 1) * A_HEAD_DIM])
        for g in range(A_KV_HEADS):
            vt_ref[:, g * V_ROWS + A_HEAD_DIM:(g + 1) * V_ROWS, :] = jnp.ones(
                (seq // KEY_BLOCK, V_ROWS - A_HEAD_DIM, KEY_BLOCK), BF16)

    for p in range(n_heads_pair):
        sl = slice(p * LANES, (p + 1) * LANES)
        t = qi_ref[:, sl].astype(F32).T
        qit_ref[:, (2 * p) * A_QUERIES:(2 * p + 1) * A_QUERIES] = t[0:IDX_DIM].astype(BF16)
        qit_ref[:, (2 * p + 1) * A_QUERIES:(2 * p + 2) * A_QUERIES] = t[IDX_DIM:].astype(BF16)
        t = qa_ref[:, sl].astype(F32).T
        qat_ref[:, (2 * p) * A_QUERIES:(2 * p + 1) * A_QUERIES] = t[0:A_HEAD_DIM].astype(BF16)
        qat_ref[:, (2 * p + 1) * A_QUERIES:(2 * p + 2) * A_QUERIES] = t[A_HEAD_DIM:].astype(BF16)

    q0 = pl.multiple_of(j * A_QUERIES, A_QUERIES)
    w_t = sm_ref[pl.ds(q0, A_QUERIES), :].T[SM_WI:SM_WI + IDX_HEADS, :]
    w_t = w_t * ((IDX_HEADS ** -0.5) * (IDX_DIM ** -0.5))

    qlane = lax.broadcasted_iota(I32, (1, A_QUERIES), 1)
    limit = q0 + (lax.shift_right_logical(qlane, CHUNK.bit_length() - 1) + 1) * CHUNK
    tile_iota = lax.broadcasted_iota(I32, (LANES, A_QUERIES), 0)
    block_iota = lax.broadcasted_iota(I32, (KEY_BLOCK, A_QUERIES), 0)
    nkb = lax.shift_right_logical(nkt + (KEY_BLOCK // LANES - 1), (KEY_BLOCK // LANES).bit_length() - 1)

    def isc_body(kb, carry):
        r0 = pl.multiple_of(kb * KEY_BLOCK, KEY_BLOCK)
        kid = sm_ref[pl.ds(r0, KEY_BLOCK), SM_KI:SM_KI + IDX_DIM].astype(BF16)
        rel = _dot(kid, qit_ref[...])
        acc = jnp.zeros((KEY_BLOCK, A_QUERIES), F32)
        for h in range(IDX_HEADS):
            acc = acc + w_t[h:h + 1, :] * jnp.maximum(rel[:, h * A_QUERIES:(h + 1) * A_QUERIES], 0.0)
        acc = jnp.where(r0 + block_iota < limit, acc, -jnp.inf)
        isc_ref[pl.ds(r0, KEY_BLOCK), :] = acc
        isb_ref[pl.ds(r0, KEY_BLOCK), :] = acc.astype(BF16)
        return carry

    lax.fori_loop(0, nkb, isc_body, 0)

    ngrp = lax.shift_right_logical(nkt + (COUNT_GROUP - 1), COUNT_GROUP.bit_length() - 1)

    def fill_body(kt, carry):
        r0 = pl.multiple_of(kt * LANES, LANES)
        isc_ref[pl.ds(r0, LANES), :] = jnp.full((LANES, A_QUERIES), -jnp.inf, F32)
        isb_ref[pl.ds(r0, LANES), :] = jnp.full((LANES, A_QUERIES), -jnp.inf, BF16)
        return carry

    lax.fori_loop(nkb * (KEY_BLOCK // LANES), ngrp * COUNT_GROUP, fill_body, 0)

    def count(pred):
        def body(g, acc):
            for t in range(COUNT_GROUP):
                r0 = pl.multiple_of((g * COUNT_GROUP + t) * LANES, LANES)
                hit = jnp.where(pred(isc_ref[pl.ds(r0, LANES), :], r0 + tile_iota), 1.0, 0.0)
                acc = acc + hit.reshape(LANES // COUNT_ROWS, COUNT_ROWS, A_QUERIES).sum(axis=0)
            return acc
        acc = lax.fori_loop(0, ngrp, body, jnp.zeros((COUNT_ROWS, A_QUERIES), F32))
        return acc.sum(axis=0, keepdims=True)

    def count_coarse(thr):
        def body(g, acc):
            for t in range(COUNT_GROUP):
                r0 = pl.multiple_of((g * COUNT_GROUP + t) * LANES, LANES)
                hit = jnp.where(isb_ref[pl.ds(r0, LANES), :] >= thr, one_b, zero_b)
                hit = hit.reshape(LANES // COUNT_ROWS, COUNT_ROWS, A_QUERIES)
                part = hit[0]
                for i in range(1, LANES // COUNT_ROWS):
                    part = part + hit[i]
                acc = acc + part
            return acc
        acc = lax.fori_loop(0, ngrp, body, jnp.zeros((COUNT_ROWS, A_QUERIES), BF16))
        return acc.astype(F32).sum(axis=0, keepdims=True)

    kf = float(topk)
    one_b = jnp.ones((), BF16)
    zero_b = jnp.zeros((), BF16)

    def thr_coarse(key):
        bits = jnp.where(key >= 0, key, key ^ jnp.int32(0x7FFF))
        return lax.bitcast_convert_type(lax.shift_left(bits, 16), F32).astype(BF16)

    c0 = count_coarse(jnp.zeros((1, A_QUERIES), BF16))
    lo16 = jnp.where(c0 >= kf, jnp.int32(0), jnp.int32(-2 ** 15))

    def coarse_body(i, lo):
        trial = lo | lax.shift_left(jnp.int32(1), 14 - i)
        c = count_coarse(thr_coarse(trial))
        return jnp.where(c >= kf, trial, lo)

    lo16 = lax.fori_loop(0, 15, coarse_body, lo16)
    lo16 = jnp.maximum(lo16, jnp.int32(NEG_INF_KEY >> 16))
    center = lax.shift_left(lo16, 16) | jnp.where(lo16 < 0, jnp.int32(0xFFFF), jnp.int32(0))

    def thr_of(key):
        bits = jnp.where(key >= 0, key, key ^ jnp.int32(0x7FFFFFFF))
        return jnp.where(key < jnp.int32(NEG_INF_KEY), -jnp.inf, lax.bitcast_convert_type(bits, F32))

    def fine_body(i, st):
        lo, c_lo = st
        trial = lo + lax.shift_left(jnp.int32(1), 16 - i)
        thr = thr_of(trial)
        c = count(lambda x, s: x >= thr)
        ok = c >= kf
        return jnp.where(ok, trial, lo), jnp.where(ok, c, c_lo)

    lo, c_lo = lax.fori_loop(0, 17, fine_body,
                             (center - jnp.int32(1 << 16), jnp.full((1, A_QUERIES), -1.0, F32)))
    kth = thr_of(lo)

    idx_bits = (seq - 1).bit_length()

    def tie_search():
        need = kf - count(lambda x, s: x > kth)

        def tie_body(i, last):
            trial = last | lax.shift_left(jnp.int32(1), idx_bits - 1 - i)
            c = count(lambda x, s: (x == kth) & (s < trial))
            return jnp.where(c < need, trial, last)

        return lax.fori_loop(0, idx_bits, tie_body, jnp.zeros((1, A_QUERIES), I32))

    def take_all_ties():
        return jnp.full((1, A_QUERIES), (1 << idx_bits) - 1, I32)

    n_ge = jnp.where(limit <= topk, 0.0, jnp.where(c_lo < 0.0, jnp.inf, c_lo))
    last = lax.cond(jnp.max(n_ge) > kf, tie_search, take_all_ties)

    acc_ref[...] = jnp.zeros_like(acc_ref)

    def att_body(kb, m_prev):
        r0 = pl.multiple_of(kb * KEY_BLOCK, KEY_BLOCK)
        x = isc_ref[pl.ds(r0, KEY_BLOCK), :]
        s_idx = r0 + block_iota
        sel = (s_idx < limit) & ((x > kth) | ((x == kth) & (s_idx <= last)))
        bias_ref[...] = jnp.where(sel, 0.0, NEG_BIG)
        k_tile = kv_ref[pl.ds(r0, KEY_BLOCK), 0:KV_WIDTH]
        v_t = vt_ref[kb]

        units = list(range(A_HEADS // ATT_HEADS))
        uw = ATT_HEADS * A_QUERIES
        us = [slice(u * uw, (u + 1) * uw) for u in units]
        kd = [slice((u * ATT_HEADS // group) * A_HEAD_DIM, (u * ATT_HEADS // group + 1) * A_HEAD_DIM)
              for u in units]
        vr = [slice((u * ATT_HEADS // group) * V_ROWS, (u * ATT_HEADS // group + 1) * V_ROWS)
              for u in units]
        cs = [slice((u * ATT_HEADS % group) * A_QUERIES, (u * ATT_HEADS % group + ATT_HEADS) * A_QUERIES)
              for u in units]
        m_new = {}
        for w0 in range(0, len(units), ATT_WAVE):
            wave = units[w0:w0 + ATT_WAVE]
            s = {u: _dot(k_tile[:, kd[u]], qat_ref[:, us[u]])
                 + jnp.concatenate([bias_ref[...]] * ATT_HEADS, axis=1) for u in wave}
            for u in wave:
                m_new[u] = jnp.maximum(m_prev[:, us[u]], jnp.max(s[u], axis=0, keepdims=True))
            alpha = {u: jnp.exp2(m_prev[:, us[u]] - m_new[u]) for u in wave}
            p = {u: jnp.exp2(s[u] - m_new[u]).astype(BF16) for u in wave}
            pv = {u: _dot(v_t[vr[u], :], p[u]) for u in wave}
            for u in wave:
                acc_ref[vr[u], cs[u]] = alpha[u] * acc_ref[vr[u], cs[u]] + pv[u]
        return jnp.concatenate([m_new[u] for u in units], axis=1)

    lax.fori_loop(0, nkb, att_body, jnp.full((1, A_HEADS * A_QUERIES), NEG_BIG, F32))

    for p in range(n_heads_pair):
        g = (2 * p) // group
        parts = []
        for h in (2 * p, 2 * p + 1):
            hq = slice((h % group) * A_QUERIES, (h % group + 1) * A_QUERIES)
            parts.append(acc_ref[g * V_ROWS:g * V_ROWS + A_HEAD_DIM, hq]
                         / acc_ref[g * V_ROWS + A_HEAD_DIM:g * V_ROWS + A_HEAD_DIM + 1, hq])
        o_ref[:, p * LANES:(p + 1) * LANES] = jnp.concatenate(parts, axis=0).T.astype(BF16)


def _mixer_a(qa, qi, kv, sm, bsz, seq):
    nq = seq // A_QUERIES
    assert seq % (COUNT_GROUP * LANES) == 0, seq
    assert seq // COUNT_ROWS <= 256, seq
    topk = min(TOPK_MAX, seq // 4)
    qrow = lambda b, j: (b * nq + j, 0)
    brow = lambda b, j: (b, 0)
    return pl.pallas_call(
        functools.partial(_mixer_a_body, seq=seq, topk=topk),
        grid=(bsz, nq),
        in_specs=[
            pl.BlockSpec((A_QUERIES, A_WIDTH), qrow),
            pl.BlockSpec((A_QUERIES, A_WIDTH), qrow),
            pl.BlockSpec((seq, 2 * KV_WIDTH), brow),
            pl.BlockSpec((seq, LANES), brow),
        ],
        out_specs=pl.BlockSpec((A_QUERIES, A_WIDTH), qrow),
        out_shape=jax.ShapeDtypeStruct((bsz * seq, A_WIDTH), BF16),
        scratch_shapes=[
            pltpu.VMEM((seq, A_QUERIES), F32),
            pltpu.VMEM((seq, A_QUERIES), BF16),
            pltpu.VMEM((seq // KEY_BLOCK, A_KV_HEADS * V_ROWS, KEY_BLOCK), BF16),
            pltpu.VMEM((IDX_DIM, IDX_HEADS * A_QUERIES), BF16),
            pltpu.VMEM((A_HEAD_DIM, A_HEADS * A_QUERIES), BF16),
            pltpu.VMEM((A_KV_HEADS * V_ROWS, (A_HEADS // A_KV_HEADS) * A_QUERIES), F32),
            pltpu.VMEM((KEY_BLOCK, A_QUERIES), F32),
        ],
        compiler_params=pltpu.CompilerParams(
            dimension_semantics=("arbitrary", "arbitrary"), vmem_limit_bytes=VMEM_LIMIT),
        name="mixer_a",
    )(qa, qi, kv, sm)


def _gdn_body(gdn_ref, sm_ref, alog_ref, dtb_ref, ng_ref, o_ref,
              gate_ref, gct_ref, state_ref, oc_ref, lhs_ref, bm_ref, *, tc):
    tb = pl.program_id(1)
    n_chunks = tc // CHUNK
    prep_unroll = 8
    q_col, k_col, v_col, z_col = (i * B_WIDTH for i in range(4))

    @pl.when(tb == 0)
    def _():
        state_ref[...] = jnp.zeros_like(state_ref)

    sm = sm_ref[...]
    beta = jax.nn.sigmoid(sm)
    z = sm + dtb_ref[...]
    softplus = jnp.maximum(z, 0.0) + jnp.log(1.0 + jnp.exp(-jnp.abs(z)))
    g = -jnp.exp(alog_ref[...]) * softplus
    rin = lax.broadcasted_iota(I32, (tc, LANES), 0) & (CHUNK - 1)
    gc = g
    step = 1
    while step < CHUNK:
        gc = gc + jnp.where(rin >= step, pltpu.roll(gc, step, 0), 0.0)
        step *= 2
    gc3 = gc.reshape(n_chunks, CHUNK, LANES)
    g_last = jnp.broadcast_to(gc3[:, CHUNK - 1:CHUNK, :], gc3.shape).reshape(tc, LANES)
    gate_ref[0] = beta
    gate_ref[1] = gc
    gate_ref[2] = jnp.exp(gc)
    gate_ref[3] = jnp.exp(g_last - gc)
    gate_ref[4] = jnp.exp(g_last)
    for i in range(tc // LANES):
        t = gc[i * LANES:(i + 1) * LANES, :].T
        for half in range(LANES // CHUNK):
            gct_ref[i * (LANES // CHUNK) + half] = t[:, half * CHUNK:(half + 1) * CHUNK]

    ci = lax.broadcasted_iota(I32, (CHUNK, CHUNK), 0)
    si = lax.broadcasted_iota(I32, (CHUNK, CHUNK), 1)
    wl = lax.broadcasted_iota(I32, (CHUNK, 2 * CHUNK), 1)
    wr = lax.broadcasted_iota(I32, (CHUNK, 2 * CHUNK), 0)
    right = wl >= CHUNK
    eye_right = jnp.where(wl == wr + CHUNK, 1.0, 0.0)

    def prep_body(cg, carry):
        units = [(cg * prep_unroll + cc, h) for cc in range(prep_unroll) for h in range(B_HEADS)]
        rows = [pl.ds(pl.multiple_of(c * CHUNK, CHUNK), CHUNK) for c, _ in units]
        hsl = [slice(h * LANES, (h + 1) * LANES) for _, h in units]
        idx = range(len(units))

        def col(i, gate, off):
            h = units[i][1]
            return gate_ref[gate, rows[i], off + h:off + h + 1]

        def seg(i, col0):
            h = units[i][1]
            return gdn_ref[rows[i], col0 + h * LANES:col0 + (h + 1) * LANES]

        q = [seg(i, q_col) for i in idx]
        k = [seg(i, k_col) for i in idx]
        kb = [k[i] * col(i, 0, SM_BETA) for i in idx]
        kq = [_dot_nt(jnp.concatenate([kb[i], q[i]], axis=0).astype(BF16), k[i].astype(BF16))
              for i in idx]
        decay = []
        for i, (c, h) in enumerate(units):
            d = col(i, 1, SM_DECAY) - gct_ref[c][SM_DECAY + h:SM_DECAY + h + 1, :]
            decay.append(jnp.where(ci >= si, jnp.exp(jnp.where(ci >= si, d, 0.0)), 0.0))
        wmat = []
        for i, (c, h) in enumerate(units):
            n_mat = jnp.where(ci > si, -(kq[i][0:CHUNK] * decay[i]), 0.0)
            wmat.append(jnp.concatenate([n_mat, jnp.zeros_like(n_mat)], axis=1) + eye_right)
        pw = 1
        while pw < CHUNK:
            wb = [wmat[i].astype(BF16) for i in idx]
            wmat = [_dot(wb[i][:, 0:CHUNK], wb[i]) + jnp.where(right, wmat[i], 0.0) for i in idx]
            pw *= 2
        eg = [col(i, 2, SM_DECAY) for i in idx]
        rhs = [jnp.concatenate([seg(i, v_col) * col(i, 0, SM_BETA), kb[i] * eg[i]],
                               axis=1).astype(BF16) for i in idx]
        sol = [_dot(wmat[i][:, CHUNK:].astype(BF16), rhs[i]).astype(BF16) for i in idx]
        att = [(kq[i][CHUNK:] * decay[i]).astype(BF16) for i in idx]
        k_tail_t = [(k[i] * col(i, 3, SM_DECAY)).T.astype(BF16) for i in idx]
        a_uw = [_dot(att[i], sol[i]) for i in idx]
        k_uw = [_dot(k_tail_t[i], sol[i]) for i in idx]
        for i, (c, h) in enumerate(units):
            oc_ref[rows[i], hsl[i]] = a_uw[i][:, 0:B_HEAD_DIM]
            lhs_ref[c, h, 0:CHUNK, :] = (q[i] * eg[i] - a_uw[i][:, B_HEAD_DIM:]).astype(BF16)
            lhs_ref[c, h, CHUNK:, :] = k_uw[i][:, B_HEAD_DIM:].astype(BF16)
            bm_ref[c, h] = k_uw[i][:, 0:B_HEAD_DIM]
        return carry

    lax.fori_loop(0, n_chunks // prep_unroll, prep_body, 0)

    def scan_body(c, carry):
        r0 = pl.multiple_of(c * CHUNK, CHUNK)
        rows = pl.ds(r0, CHUNK)
        heads = range(B_HEADS)
        hsl = [slice(h * LANES, (h + 1) * LANES) for h in heads]
        s_prev = [state_ref[h] for h in heads]
        r = [_dot(lhs_ref[c, h], s_prev[h].astype(BF16)) for h in heads]
        for h in heads:
            gl = gate_ref[4, rows, SM_DECAY + h:SM_DECAY + h + 1][0:1, :]
            oc_ref[rows, hsl[h]] = oc_ref[rows, hsl[h]] + r[h][0:CHUNK]
            state_ref[h] = s_prev[h] * gl + bm_ref[c, h] - r[h][CHUNK:]
        return carry

    lax.fori_loop(0, n_chunks, scan_body, 0)

    for h in range(B_HEADS):
        hs = slice(h * LANES, (h + 1) * LANES)
        z_gate = gdn_ref[:, z_col + h * LANES:z_col + (h + 1) * LANES]
        o_ref[:, hs] = (_rms(oc_ref[:, hs], ng_ref[...]) * z_gate).astype(BF16)


def _gdn(gdn_in, sm, alog_row, dtb_row, norm_g, bsz, seq, tc):
    nt = seq // tc
    trow = lambda b, t: (b * nt + t, 0)
    const = lambda b, t: (0, 0)
    n_chunks = tc // CHUNK
    return pl.pallas_call(
        functools.partial(_gdn_body, tc=tc),
        grid=(bsz, nt),
        in_specs=[
            pl.BlockSpec((tc, GDN_WIDTH), trow),
            pl.BlockSpec((tc, LANES), trow),
            pl.BlockSpec((1, LANES), const),
            pl.BlockSpec((1, LANES), const),
            pl.BlockSpec((1, B_HEAD_DIM), const),
        ],
        out_specs=pl.BlockSpec((tc, B_WIDTH), trow),
        out_shape=jax.ShapeDtypeStruct((bsz * seq, B_WIDTH), BF16),
        scratch_shapes=[
            pltpu.VMEM((5, tc, LANES), F32),
            pltpu.VMEM((n_chunks, LANES, CHUNK), F32),
            pltpu.VMEM((B_HEADS, B_HEAD_DIM, B_HEAD_DIM), F32),
            pltpu.VMEM((tc, B_WIDTH), F32),
            pltpu.VMEM((n_chunks, B_HEADS, CHUNK + B_HEAD_DIM, B_HEAD_DIM), BF16),
            pltpu.VMEM((n_chunks, B_HEADS, B_HEAD_DIM, B_HEAD_DIM), F32),
        ],
        compiler_params=pltpu.CompilerParams(
            dimension_semantics=("arbitrary", "arbitrary"), vmem_limit_bytes=VMEM_LIMIT),
        name="gdn",
    )(gdn_in, sm, alog_row, dtb_row, norm_g)


def _ffn_body(x_ref, oa_ref, ob_ref, wo_ref, g2_ref, w1_ref, w2_ref, g3_ref, out_ref, *,
              final_norm):
    y = (x_ref[...] + _dot(oa_ref[...], wo_ref[0:A_WIDTH, :])
         + _dot(ob_ref[...], wo_ref[A_WIDTH:, :]))
    h = _rms(y, g2_ref[...]).astype(BF16)
    a = jnp.square(jnp.maximum(_dot(h, w1_ref[...]), 0.0)).astype(BF16)
    acc = y + _dot(a, w2_ref[...])
    out_ref[...] = _rms(acc, g3_ref[...]) if final_norm else acc


def _ffn(x2, oa, ob, wo, g2, w1, w2, g3, tm, final_norm):
    m = x2.shape[0]
    row = lambda i: (i, 0)
    const = lambda i: (0, 0)
    resident = functools.partial(pl.BlockSpec, index_map=const, pipeline_mode=pl.Buffered(1))
    return pl.pallas_call(
        functools.partial(_ffn_body, final_norm=final_norm),
        grid=(m // tm,),
        in_specs=[
            pl.BlockSpec((tm, D_MODEL), row),
            pl.BlockSpec((tm, A_WIDTH), row),
            pl.BlockSpec((tm, B_WIDTH), row),
            resident((D_MODEL, D_MODEL)),
            pl.BlockSpec((1, D_MODEL), const),
            resident((D_MODEL, D_FF)),
            resident((D_FF, D_MODEL)),
            pl.BlockSpec((1, D_MODEL), const),
        ],
        out_specs=pl.BlockSpec((tm, D_MODEL), row),
        out_shape=jax.ShapeDtypeStruct((m, D_MODEL), F32),
        compiler_params=pltpu.CompilerParams(
            dimension_semantics=("arbitrary",), vmem_limit_bytes=VMEM_LIMIT),
        name="ffn",
    )(x2, oa, ob, wo, g2, w1, w2, g3)


def _rope_tables(seq):
    half = A_HEAD_DIM // 2
    inv_freq = 1.0 / (ROPE_THETA ** (jnp.arange(half, dtype=F32) / half))
    ang = jnp.arange(seq).astype(F32)[:, None] * inv_freq[None, :]
    cos = jnp.cos(ang)
    sin = jnp.sin(ang)
    reps = LANES // A_HEAD_DIM
    return (jnp.tile(cos, (1, 2 * reps)),
            jnp.tile(jnp.concatenate([-sin, sin], axis=1), (1, reps)))


def _lane_row(vals, offset):
    return jnp.zeros((1, LANES), F32).at[0, offset:offset + vals.shape[0]].set(vals.astype(F32))


def kernel(x, norm_mix_g, w_in, conv_w, a_log, dt_bias, gdn_norm_g, w_out,
           norm_ffn_g, w_ff1, w_ff2, norm_final_g):
    bsz, seq, d = x.shape
    depth = w_in.shape[0]
    m = bsz * seq
    cos_t, sin_t = _rope_tables(seq)
    x2 = x.reshape(m, d)
    for l in range(depth):
        qa, qi, kv, gdn_in, sm = _inproj(
            x2, norm_mix_g[l][None, :], jnp.swapaxes(w_in, 1, 2), l, cos_t, sin_t, conv_w[l],
            seq, tm=512)
        o_a = _mixer_a(qa, qi, kv, sm, bsz, seq)
        o_b = _gdn(gdn_in, sm, _lane_row(a_log[l], SM_DECAY),
                   _lane_row(dt_bias[l], SM_DECAY), gdn_norm_g[l][None, :], bsz, seq, tc=512)
        x2 = _ffn(x2, o_a, o_b, w_out[l].astype(BF16), norm_ffn_g[l][None, :],
                  w_ff1[l].astype(BF16), w_ff2[l].astype(BF16), norm_final_g[None, :],
                  tm=512, final_norm=(l == depth - 1))
    return x2.reshape(bsz, seq, d)
```

```python
import functools

import jax
import jax.numpy as jnp
from jax import lax
from jax.experimental import pallas as pl
from jax.experimental.pallas import tpu as pltpu

F32 = jnp.float32
BF16 = jnp.bfloat16
I32 = jnp.int32

D_MODEL = 1024
CHUNK = 64
A_QUERIES = 256
ROPE_THETA = 10000.0
EPS = 1e-6
A_HEADS = 8
A_KV_HEADS = 2
A_HEAD_DIM = 64
IDX_HEADS = 8
IDX_DIM = 64
TOPK_MAX = 256
B_HEADS = 4
B_HEAD_DIM = 128
CONV_WIDTH = 4
D_FF = 4 * D_MODEL

LANES = 128
A_WIDTH = A_HEADS * A_HEAD_DIM
KV_WIDTH = A_KV_HEADS * A_HEAD_DIM
B_WIDTH = B_HEADS * B_HEAD_DIM
GDN_WIDTH = 4 * B_WIDTH
SM_KI = 0
SM_WI = IDX_DIM
SM_BETA = SM_WI + IDX_HEADS
SM_DECAY = SM_BETA + B_HEADS
C_QA = 0
C_QI = C_QA + A_WIDTH
C_KV = C_QI + A_WIDTH
C_GDN = C_KV + 2 * KV_WIDTH
C_SM = C_GDN + GDN_WIDTH
IN_COLS = C_SM + LANES
REF_SIZES = (A_WIDTH, KV_WIDTH, KV_WIDTH, IDX_HEADS * IDX_DIM, IDX_DIM, IDX_HEADS,
             B_WIDTH, B_WIDTH, B_WIDTH, B_WIDTH, B_HEADS, B_HEADS)
REF_OFF = tuple(sum(REF_SIZES[:i]) for i in range(len(REF_SIZES)))
IN_DIM = sum(REF_SIZES)
IN_WEIGHT_MOVES = (
    (C_QA, REF_OFF[0], A_WIDTH),
    (C_QI, REF_OFF[3], A_WIDTH),
    (C_KV, REF_OFF[1], 2 * KV_WIDTH),
    (C_GDN, REF_OFF[6], GDN_WIDTH),
    (C_SM + SM_KI, REF_OFF[4], IDX_DIM + IDX_HEADS),
    (C_SM + SM_BETA, REF_OFF[10], 2 * B_HEADS),
)
IN_PAD = IN_COLS - (C_SM + SM_DECAY + B_HEADS)

VMEM_LIMIT = 56 * 1024 * 1024
CONV_PAD = 8
COUNT_GROUP = 4
COUNT_ROWS = 32
V_ROWS = A_HEAD_DIM + 16
ATT_HEADS = 2
ATT_WAVE = 4
KEY_BLOCK = 2 * LANES
NEG_BIG = -1e30
LOG2_E = 1.4426950408889634
NEG_INF_KEY = -(2 ** 31) + 0x7FFFFF


def _rms(x, g):
    return x * lax.rsqrt(jnp.mean(x * x, axis=-1, keepdims=True) + EPS) * g


def _dot(a, b):
    return jnp.dot(a, b, preferred_element_type=F32)


def _dot_nt(a, b):
    return lax.dot_general(a, b, (((1,), (1,)), ((), ())), preferred_element_type=F32)


def _inproj_body(x_ref, g_ref, win_ref, cos_ref, sin_ref, cw_ref,
                 qa_ref, qi_ref, kv_ref, gdn_ref, sm_ref, xpad_ref, w_ref, *, tiles_per_seq):
    @pl.when(pl.program_id(0) == 0)
    def _():
        step = 256
        for dst, src, width in IN_WEIGHT_MOVES[:-2]:
            for r0 in range(0, width, step):
                w_ref[dst + r0:dst + r0 + step, :] = win_ref[src + r0:src + r0 + step, :].astype(BF16)
        small = [win_ref[src:src + width, :] for _, src, width in IN_WEIGHT_MOVES[-2:]]
        small.append(jnp.zeros((IN_PAD, D_MODEL), F32))
        w_ref[C_SM:IN_COLS, :] = jnp.concatenate(small, axis=0).astype(BF16)

    h = _rms(x_ref[...], g_ref[...]).astype(BF16)
    cos = cos_ref[...]
    sin = sin_ref[...]
    lane = lax.broadcasted_iota(I32, cos.shape, 1)
    first_half = (lane & (A_HEAD_DIM - 1)) < A_HEAD_DIM // 2

    def rope(t):
        swapped = jnp.where(first_half, pltpu.roll(t, LANES - A_HEAD_DIM // 2, 1),
                            pltpu.roll(t, A_HEAD_DIM // 2, 1))
        return t * cos + swapped * sin

    def proj(c0, width):
        return _dot_nt(h, w_ref[c0:c0 + width, :])

    tm = x_ref.shape[0]
    conv_cols = 3 * B_WIDTH

    @pl.when(pl.program_id(0) % tiles_per_seq == 0)
    def _():
        xpad_ref[0:CONV_PAD, :] = jnp.zeros((CONV_PAD, conv_cols), F32)

    @pl.when(pl.program_id(0) % tiles_per_seq != 0)
    def _():
        xpad_ref[0:CONV_PAD, :] = xpad_ref[tm:tm + CONV_PAD, :]

    def stage_conv_input(seg):
        xpad_ref[CONV_PAD:CONV_PAD + tm, seg * B_WIDTH:(seg + 1) * B_WIDTH] = proj(
            C_GDN + seg * B_WIDTH, B_WIDTH)

    def conv_segment(seg):
        for hh in range(B_HEADS):
            cs = slice(seg * B_WIDTH + hh * LANES, seg * B_WIDTH + (hh + 1) * LANES)
            xa = xpad_ref[:, cs]
            y = cw_ref[0:1, cs] * xa
            for jj in range(1, CONV_WIDTH):
                y = cw_ref[jj:jj + 1, cs] * xa + pltpu.roll(y, 1, 0)
            y = y[CONV_PAD:, :]
            y = y * jax.nn.sigmoid(y)
            if seg < 2:
                y = y * lax.rsqrt(jnp.sum(y * y, axis=-1, keepdims=True) + EPS)
            if seg == 0:
                y = y * (B_HEAD_DIM ** -0.5)
            gdn_ref[:, cs] = y

    def store_queries_transposed(out_ref, acc, scale):
        for j in range(A_WIDTH // LANES):
            t = rope(acc[:, j * LANES:(j + 1) * LANES])
            if scale != 1.0:
                t = t * scale
            for b in range(tm // A_QUERIES):
                tt = t[b * A_QUERIES:(b + 1) * A_QUERIES, :].T
                for hh in range(LANES // A_HEAD_DIM):
                    h0 = (j * (LANES // A_HEAD_DIM) + hh) * A_QUERIES
                    out_ref[b, :, h0:h0 + A_QUERIES] = (
                        tt[hh * A_HEAD_DIM:(hh + 1) * A_HEAD_DIM, :].astype(BF16))

    stage_conv_input(0)
    acc = proj(C_QA, A_WIDTH)
    conv_segment(0)
    store_queries_transposed(qa_ref, acc, A_HEAD_DIM ** -0.5 * LOG2_E)
    stage_conv_input(1)
    acc = proj(C_QI, A_WIDTH)
    conv_segment(1)
    store_queries_transposed(qi_ref, acc, 1.0)
    stage_conv_input(2)
    acc = proj(C_KV, 2 * KV_WIDTH)
    acc_sm = proj(C_SM, LANES)
    z = proj(C_GDN + conv_cols, B_WIDTH)
    conv_segment(2)
    kv_ref[:, 0:KV_WIDTH] = rope(acc[:, 0:KV_WIDTH]).astype(BF16)
    kv_ref[:, KV_WIDTH:] = acc[:, KV_WIDTH:].astype(BF16)
    sm_ref[...] = jnp.where(lane < IDX_DIM, rope(acc_sm), acc_sm)
    gdn_ref[:, conv_cols:] = z * jax.nn.sigmoid(z)


def _inproj(x2, g, w, layer, cos_t, sin_t, conv_w, seq, tm):
    m = x2.shape[0]
    assert seq % tm == 0 and tm % A_QUERIES == 0, (seq, tm)
    nt = seq // tm
    row = lambda i: (i, 0)
    const = lambda i: (0, 0)
    return pl.pallas_call(
        functools.partial(_inproj_body, tiles_per_seq=nt),
        grid=(m // tm,),
        in_specs=[
            pl.BlockSpec((tm, D_MODEL), row),
            pl.BlockSpec((1, D_MODEL), const),
            pl.BlockSpec((None, IN_DIM, D_MODEL), lambda i: (layer, 0, 0),
                         pipeline_mode=pl.Buffered(1)),
            pl.BlockSpec((tm, LANES), lambda i: (i % nt, 0)),
            pl.BlockSpec((tm, LANES), lambda i: (i % nt, 0)),
            pl.BlockSpec((CONV_WIDTH, 3 * B_WIDTH), const),
        ],
        out_specs=[
            pl.BlockSpec((tm // A_QUERIES, A_HEAD_DIM, A_HEADS * A_QUERIES), lambda i: (i, 0, 0)),
            pl.BlockSpec((tm // A_QUERIES, IDX_DIM, IDX_HEADS * A_QUERIES), lambda i: (i, 0, 0)),
            pl.BlockSpec((tm, 2 * KV_WIDTH), row),
            pl.BlockSpec((tm, GDN_WIDTH), row),
            pl.BlockSpec((tm, LANES), row),
        ],
        out_shape=[
            jax.ShapeDtypeStruct((m // A_QUERIES, A_HEAD_DIM, A_HEADS * A_QUERIES), BF16),
            jax.ShapeDtypeStruct((m // A_QUERIES, IDX_DIM, IDX_HEADS * A_QUERIES), BF16),
            jax.ShapeDtypeStruct((m, 2 * KV_WIDTH), BF16),
            jax.ShapeDtypeStruct((m, GDN_WIDTH), F32),
            jax.ShapeDtypeStruct((m, LANES), F32),
        ],
        scratch_shapes=[
            pltpu.VMEM((tm + CONV_PAD, 3 * B_WIDTH), F32),
            pltpu.VMEM((IN_COLS, D_MODEL), BF16),
        ],
        compiler_params=pltpu.CompilerParams(
            dimension_semantics=("arbitrary",), vmem_limit_bytes=VMEM_LIMIT),
        name="inproj",
    )(x2, g, w, cos_t, sin_t, conv_w)


def _mixer_a_body(qat_ref, qit_ref, kv_ref, sm_ref, o_ref,
                  isc_ref, isb_ref, vt_ref, acc_ref, bias_ref, *, seq, topk):
    j = pl.program_id(1)
    nkt = (j + 1) * (A_QUERIES // LANES)
    n_heads_pair = A_WIDTH // LANES
    group = A_HEADS // A_KV_HEADS

    @pl.when(j == 0)
    def _():
        per_block = KEY_BLOCK // LANES
        for t in range(seq // LANES):
            vt = kv_ref[t * LANES:(t + 1) * LANES, KV_WIDTH:].astype(F32).T.astype(BF16)
            c0 = (t % per_block) * LANES
            for g in range(A_KV_HEADS):
                vt_ref[t // per_block, g * V_ROWS:g * V_ROWS + A_HEAD_DIM, c0:c0 + LANES] = (
                    vt[g * A_HEAD_DIM:(g + 1) * A_HEAD_DIM])
        for g in range(A_KV_HEADS):
            vt_ref[:, g * V_ROWS + A_HEAD_DIM:(g + 1) * V_ROWS, :] = jnp.ones(
                (seq // KEY_BLOCK, V_ROWS - A_HEAD_DIM, KEY_BLOCK), BF16)

    q0 = pl.multiple_of(j * A_QUERIES, A_QUERIES)
    w_t = sm_ref[pl.ds(q0, A_QUERIES), :].T[SM_WI:SM_WI + IDX_HEADS, :]
    w_t = w_t * ((IDX_HEADS ** -0.5) * (IDX_DIM ** -0.5))

    qlane = lax.broadcasted_iota(I32, (1, A_QUERIES), 1)
    limit = q0 + (lax.shift_right_logical(qlane, CHUNK.bit_length() - 1) + 1) * CHUNK
    tile_iota = lax.broadcasted_iota(I32, (LANES, A_QUERIES), 0)
    block_iota = lax.broadcasted_iota(I32, (KEY_BLOCK, A_QUERIES), 0)
    nkb = lax.shift_right_logical(nkt + (KEY_BLOCK // LANES - 1), (KEY_BLOCK // LANES).bit_length() - 1)

    def isc_body(kb, carry):
        r0 = pl.multiple_of(kb * KEY_BLOCK, KEY_BLOCK)
        kid = sm_ref[pl.ds(r0, KEY_BLOCK), SM_KI:SM_KI + IDX_DIM].astype(BF16)
        rel = _dot(kid, qit_ref[...])
        acc = jnp.zeros((KEY_BLOCK, A_QUERIES), F32)
        for h in range(IDX_HEADS):
            acc = acc + w_t[h:h + 1, :] * jnp.maximum(rel[:, h * A_QUERIES:(h + 1) * A_QUERIES], 0.0)
        acc = jnp.where(r0 + block_iota < limit, acc, -jnp.inf)
        isc_ref[pl.ds(r0, KEY_BLOCK), :] = acc
        isb_ref[pl.ds(r0, KEY_BLOCK), :] = acc.astype(BF16)
        return carry

    lax.fori_loop(0, nkb, isc_body, 0)

    ngrp = lax.shift_right_logical(nkt + (COUNT_GROUP - 1), COUNT_GROUP.bit_length() - 1)

    def fill_body(kt, carry):
        r0 = pl.multiple_of(kt * LANES, LANES)
        isc_ref[pl.ds(r0, LANES), :] = jnp.full((LANES, A_QUERIES), -jnp.inf, F32)
        isb_ref[pl.ds(r0, LANES), :] = jnp.full((LANES, A_QUERIES), -jnp.inf, BF16)
        return carry

    lax.fori_loop(nkb * (KEY_BLOCK // LANES), ngrp * COUNT_GROUP, fill_body, 0)

    def count(pred):
        def body(g, acc):
            for t in range(COUNT_GROUP):
                r0 = pl.multiple_of((g * COUNT_GROUP + t) * LANES, LANES)
                hit = jnp.where(pred(isc_ref[pl.ds(r0, LANES), :], r0 + tile_iota), 1.0, 0.0)
                acc = acc + hit.reshape(LANES // COUNT_ROWS, COUNT_ROWS, A_QUERIES).sum(axis=0)
            return acc
        acc = lax.fori_loop(0, ngrp, body, jnp.zeros((COUNT_ROWS, A_QUERIES), F32))
        return acc.sum(axis=0, keepdims=True)

    def count_coarse(thr):
        def body(g, acc):
            for t in range(COUNT_GROUP):
                r0 = pl.multiple_of((g * COUNT_GROUP + t) * LANES, LANES)
                hit = jnp.where(isb_ref[pl.ds(r0, LANES), :] >= thr, one_b, zero_b)
                hit = hit.reshape(LANES // COUNT_ROWS, COUNT_ROWS, A_QUERIES)
                part = hit[0]
                for i in range(1, LANES // COUNT_ROWS):
                    part = part + hit[i]
                acc = acc + part
            return acc
        acc = lax.fori_loop(0, ngrp, body, jnp.zeros((COUNT_ROWS, A_QUERIES), BF16))
        return acc.astype(F32).sum(axis=0, keepdims=True)

    kf = float(topk)
    one_b = jnp.ones((), BF16)
    zero_b = jnp.zeros((), BF16)

    def thr_coarse(key):
        bits = jnp.where(key >= 0, key, key ^ jnp.int32(0x7FFF))
        return lax.bitcast_convert_type(lax.shift_left(bits, 16), F32).astype(BF16)

    c0 = count_coarse(jnp.zeros((1, A_QUERIES), BF16))
    lo16 = jnp.where(c0 >= kf, jnp.int32(0), jnp.int32(-2 ** 15))

    def coarse_body(i, lo):
        trial = lo | lax.shift_left(jnp.int32(1), 14 - i)
        c = count_coarse(thr_coarse(trial))
        return jnp.where(c >= kf, trial, lo)

    lo16 = lax.fori_loop(0, 15, coarse_body, lo16)
    lo16 = jnp.maximum(lo16, jnp.int32(NEG_INF_KEY >> 16))
    center = lax.shift_left(lo16, 16) | jnp.where(lo16 < 0, jnp.int32(0xFFFF), jnp.int32(0))

    def thr_of(key):
        bits = jnp.where(key >= 0, key, key ^ jnp.int32(0x7FFFFFFF))
        return jnp.where(key < jnp.int32(NEG_INF_KEY), -jnp.inf, lax.bitcast_convert_type(bits, F32))

    def fine_body(i, st):
        lo, c_lo = st
        trial = lo + lax.shift_left(jnp.int32(1), 16 - i)
        thr = thr_of(trial)
        c = count(lambda x, s: x >= thr)
        ok = c >= kf
        return jnp.where(ok, trial, lo), jnp.where(ok, c, c_lo)

    lo, c_lo = lax.fori_loop(0, 17, fine_body,
                             (center - jnp.int32(1 << 16), jnp.full((1, A_QUERIES), -1.0, F32)))
    kth = thr_of(lo)

    idx_bits = (seq - 1).bit_length()

    def tie_search():
        need = kf - count(lambda x, s: x > kth)

        def tie_body(i, last):
            trial = last | lax.shift_left(jnp.int32(1), idx_bits - 1 - i)
            c = count(lambda x, s: (x == kth) & (s < trial))
            return jnp.where(c < need, trial, last)

        return lax.fori_loop(0, idx_bits, tie_body, jnp.zeros((1, A_QUERIES), I32))

    def take_all_ties():
        return jnp.full((1, A_QUERIES), (1 << idx_bits) - 1, I32)

    n_ge = jnp.where(limit <= topk, 0.0, jnp.where(c_lo < 0.0, jnp.inf, c_lo))
    last = lax.cond(jnp.max(n_ge) > kf, tie_search, take_all_ties)

    acc_ref[...] = jnp.zeros_like(acc_ref)

    def att_body(kb, m_prev):
        r0 = pl.multiple_of(kb * KEY_BLOCK, KEY_BLOCK)
        x = isc_ref[pl.ds(r0, KEY_BLOCK), :]
        s_idx = r0 + block_iota
        sel = (s_idx < limit) & ((x > kth) | ((x == kth) & (s_idx <= last)))
        bias_ref[...] = jnp.where(sel, 0.0, NEG_BIG)
        k_tile = kv_ref[pl.ds(r0, KEY_BLOCK), 0:KV_WIDTH]
        v_t = vt_ref[kb]

        units = list(range(A_HEADS // ATT_HEADS))
        uw = ATT_HEADS * A_QUERIES
        us = [slice(u * uw, (u + 1) * uw) for u in units]
        kd = [slice((u * ATT_HEADS // group) * A_HEAD_DIM, (u * ATT_HEADS // group + 1) * A_HEAD_DIM)
              for u in units]
        vr = [slice((u * ATT_HEADS // group) * V_ROWS, (u * ATT_HEADS // group + 1) * V_ROWS)
              for u in units]
        cs = [slice((u * ATT_HEADS % group) * A_QUERIES, (u * ATT_HEADS % group + ATT_HEADS) * A_QUERIES)
              for u in units]
        m_new = {}
        for w0 in range(0, len(units), ATT_WAVE):
            wave = units[w0:w0 + ATT_WAVE]
            s = {u: _dot(k_tile[:, kd[u]], qat_ref[:, us[u]])
                 + jnp.concatenate([bias_ref[...]] * ATT_HEADS, axis=1) for u in wave}
            for u in wave:
                m_new[u] = jnp.maximum(m_prev[:, us[u]], jnp.max(s[u], axis=0, keepdims=True))
            alpha = {u: jnp.exp2(m_prev[:, us[u]] - m_new[u]) for u in wave}
            p = {u: jnp.exp2(s[u] - m_new[u]).astype(BF16) for u in wave}
            pv = {u: _dot(v_t[vr[u], :], p[u]) for u in wave}
            for u in wave:
                acc_ref[vr[u], cs[u]] = alpha[u] * acc_ref[vr[u], cs[u]] + pv[u]
        return jnp.concatenate([m_new[u] for u in units], axis=1)

    lax.fori_loop(0, nkb, att_body, jnp.full((1, A_HEADS * A_QUERIES), NEG_BIG, F32))

    for p in range(n_heads_pair):
        g = (2 * p) // group
        parts = []
        for h in (2 * p, 2 * p + 1):
            hq = slice((h % group) * A_QUERIES, (h % group + 1) * A_QUERIES)
            parts.append(acc_ref[g * V_ROWS:g * V_ROWS + A_HEAD_DIM, hq]
                         / acc_ref[g * V_ROWS + A_HEAD_DIM:g * V_ROWS + A_HEAD_DIM + 1, hq])
        o_ref[:, p * LANES:(p + 1) * LANES] = jnp.concatenate(parts, axis=0).T.astype(BF16)


def _mixer_a(qa, qi, kv, sm, bsz, seq):
    nq = seq // A_QUERIES
    assert seq % (COUNT_GROUP * LANES) == 0, seq
    assert seq // COUNT_ROWS <= 256, seq
    topk = min(TOPK_MAX, seq // 4)
    qrow = lambda b, j: (b * nq + j, 0)
    qblock = lambda b, j: (b * nq + j, 0, 0)
    brow = lambda b, j: (b, 0)
    return pl.pallas_call(
        functools.partial(_mixer_a_body, seq=seq, topk=topk),
        grid=(bsz, nq),
        in_specs=[
            pl.BlockSpec((None, A_HEAD_DIM, A_HEADS * A_QUERIES), qblock),
            pl.BlockSpec((None, IDX_DIM, IDX_HEADS * A_QUERIES), qblock),
            pl.BlockSpec((seq, 2 * KV_WIDTH), brow),
            pl.BlockSpec((seq, LANES), brow),
        ],
        out_specs=pl.BlockSpec((A_QUERIES, A_WIDTH), qrow),
        out_shape=jax.ShapeDtypeStruct((bsz * seq, A_WIDTH), BF16),
        scratch_shapes=[
            pltpu.VMEM((seq, A_QUERIES), F32),
            pltpu.VMEM((seq, A_QUERIES), BF16),
            pltpu.VMEM((seq // KEY_BLOCK, A_KV_HEADS * V_ROWS, KEY_BLOCK), BF16),
            pltpu.VMEM((A_KV_HEADS * V_ROWS, (A_HEADS // A_KV_HEADS) * A_QUERIES), F32),
            pltpu.VMEM((KEY_BLOCK, A_QUERIES), F32),
        ],
        compiler_params=pltpu.CompilerParams(
            dimension_semantics=("arbitrary", "arbitrary"), vmem_limit_bytes=VMEM_LIMIT),
        name="mixer_a",
    )(qa, qi, kv, sm)


def _gdn_body(gdn_ref, sm_ref, alog_ref, dtb_ref, ng_ref, o_ref,
              gate_ref, gct_ref, state_ref, oc_ref, lhs_ref, bm_ref, *, tc):
    tb = pl.program_id(1)
    n_chunks = tc // CHUNK
    prep_unroll = 8
    q_col, k_col, v_col, z_col = (i * B_WIDTH for i in range(4))

    @pl.when(tb == 0)
    def _():
        state_ref[...] = jnp.zeros_like(state_ref)

    sm = sm_ref[...]
    beta = jax.nn.sigmoid(sm)
    z = sm + dtb_ref[...]
    softplus = jnp.maximum(z, 0.0) + jnp.log(1.0 + jnp.exp(-jnp.abs(z)))
    g = -jnp.exp(alog_ref[...]) * softplus
    rin = lax.broadcasted_iota(I32, (tc, LANES), 0) & (CHUNK - 1)
    gc = g
    step = 1
    while step < CHUNK:
        gc = gc + jnp.where(rin >= step, pltpu.roll(gc, step, 0), 0.0)
        step *= 2
    gc3 = gc.reshape(n_chunks, CHUNK, LANES)
    g_last = jnp.broadcast_to(gc3[:, CHUNK - 1:CHUNK, :], gc3.shape).reshape(tc, LANES)
    gate_ref[0] = beta
    gate_ref[1] = gc
    gate_ref[2] = jnp.exp(gc)
    gate_ref[3] = jnp.exp(g_last - gc)
    gate_ref[4] = jnp.exp(g_last)
    for i in range(tc // LANES):
        t = gc[i * LANES:(i + 1) * LANES, :].T
        for half in range(LANES // CHUNK):
            gct_ref[i * (LANES // CHUNK) + half] = t[:, half * CHUNK:(half + 1) * CHUNK]

    ci = lax.broadcasted_iota(I32, (CHUNK, CHUNK), 0)
    si = lax.broadcasted_iota(I32, (CHUNK, CHUNK), 1)
    wl = lax.broadcasted_iota(I32, (CHUNK, 2 * CHUNK), 1)
    wr = lax.broadcasted_iota(I32, (CHUNK, 2 * CHUNK), 0)
    right = wl >= CHUNK
    eye_right = jnp.where(wl == wr + CHUNK, 1.0, 0.0)

    def prep_body(cg, carry):
        units = [(cg * prep_unroll + cc, h) for cc in range(prep_unroll) for h in range(B_HEADS)]
        rows = [pl.ds(pl.multiple_of(c * CHUNK, CHUNK), CHUNK) for c, _ in units]
        hsl = [slice(h * LANES, (h + 1) * LANES) for _, h in units]
        idx = range(len(units))

        def col(i, gate, off):
            h = units[i][1]
            return gate_ref[gate, rows[i], off + h:off + h + 1]

        def seg(i, col0):
            h = units[i][1]
            return gdn_ref[rows[i], col0 + h * LANES:col0 + (h + 1) * LANES]

        q = [seg(i, q_col) for i in idx]
        k = [seg(i, k_col) for i in idx]
        kb = [k[i] * col(i, 0, SM_BETA) for i in idx]
        kq = [_dot_nt(jnp.concatenate([kb[i], q[i]], axis=0).astype(BF16), k[i].astype(BF16))
              for i in idx]
        decay = []
        for i, (c, h) in enumerate(units):
            d = col(i, 1, SM_DECAY) - gct_ref[c][SM_DECAY + h:SM_DECAY + h + 1, :]
            decay.append(jnp.where(ci >= si, jnp.exp(jnp.where(ci >= si, d, 0.0)), 0.0))
        wmat = []
        for i, (c, h) in enumerate(units):
            n_mat = jnp.where(ci > si, -(kq[i][0:CHUNK] * decay[i]), 0.0)
            wmat.append(jnp.concatenate([n_mat, jnp.zeros_like(n_mat)], axis=1) + eye_right)
        pw = 1
        while pw < CHUNK:
            wb = [wmat[i].astype(BF16) for i in idx]
            wmat = [_dot(wb[i][:, 0:CHUNK], wb[i]) + jnp.where(right, wmat[i], 0.0) for i in idx]
            pw *= 2
        eg = [col(i, 2, SM_DECAY) for i in idx]
        rhs = [jnp.concatenate([seg(i, v_col) * col(i, 0, SM_BETA), kb[i] * eg[i]],
                               axis=1).astype(BF16) for i in idx]
        sol = [_dot(wmat[i][:, CHUNK:].astype(BF16), rhs[i]).astype(BF16) for i in idx]
        att = [(kq[i][CHUNK:] * decay[i]).astype(BF16) for i in idx]
        k_tail_t = [(k[i] * col(i, 3, SM_DECAY)).T.astype(BF16) for i in idx]
        a_uw = [_dot(att[i], sol[i]) for i in idx]
        k_uw = [_dot(k_tail_t[i], sol[i]) for i in idx]
        for i, (c, h) in enumerate(units):
            oc_ref[rows[i], hsl[i]] = a_uw[i][:, 0:B_HEAD_DIM]
            lhs_ref[c, h, 0:CHUNK, :] = (q[i] * eg[i] - a_uw[i][:, B_HEAD_DIM:]).astype(BF16)
            lhs_ref[c, h, CHUNK:, :] = k_uw[i][:, B_HEAD_DIM:].astype(BF16)
            bm_ref[c, h] = k_uw[i][:, 0:B_HEAD_DIM]
        return carry

    lax.fori_loop(0, n_chunks // prep_unroll, prep_body, 0)

    def scan_body(c, carry):
        r0 = pl.multiple_of(c * CHUNK, CHUNK)
        rows = pl.ds(r0, CHUNK)
        heads = range(B_HEADS)
        hsl = [slice(h * LANES, (h + 1) * LANES) for h in heads]
        s_prev = [state_ref[h] for h in heads]
        r = [_dot(lhs_ref[c, h], s_prev[h].astype(BF16)) for h in heads]
        for h in heads:
            gl = gate_ref[4, rows, SM_DECAY + h:SM_DECAY + h + 1][0:1, :]
            oc_ref[rows, hsl[h]] = oc_ref[rows, hsl[h]] + r[h][0:CHUNK]
            state_ref[h] = s_prev[h] * gl + bm_ref[c, h] - r[h][CHUNK:]
        return carry

    lax.fori_loop(0, n_chunks, scan_body, 0)

    for h in range(B_HEADS):
        hs = slice(h * LANES, (h + 1) * LANES)
        z_gate = gdn_ref[:, z_col + h * LANES:z_col + (h + 1) * LANES]
        o_ref[:, hs] = (_rms(oc_ref[:, hs], ng_ref[...]) * z_gate).astype(BF16)


def _gdn(gdn_in, sm, alog_row, dtb_row, norm_g, bsz, seq, tc):
    nt = seq // tc
    trow = lambda b, t: (b * nt + t, 0)
    const = lambda b, t: (0, 0)
    n_chunks = tc // CHUNK
    return pl.pallas_call(
        functools.partial(_gdn_body, tc=tc),
        grid=(bsz, nt),
        in_specs=[
            pl.BlockSpec((tc, GDN_WIDTH), trow),
            pl.BlockSpec((tc, LANES), trow),
            pl.BlockSpec((1, LANES), const),
            pl.BlockSpec((1, LANES), const),
            pl.BlockSpec((1, B_HEAD_DIM), const),
        ],
        out_specs=pl.BlockSpec((tc, B_WIDTH), trow),
        out_shape=jax.ShapeDtypeStruct((bsz * seq, B_WIDTH), BF16),
        scratch_shapes=[
            pltpu.VMEM((5, tc, LANES), F32),
            pltpu.VMEM((n_chunks, LANES, CHUNK), F32),
            pltpu.VMEM((B_HEADS, B_HEAD_DIM, B_HEAD_DIM), F32),
            pltpu.VMEM((tc, B_WIDTH), F32),
            pltpu.VMEM((n_chunks, B_HEADS, CHUNK + B_HEAD_DIM, B_HEAD_DIM), BF16),
            pltpu.VMEM((n_chunks, B_HEADS, B_HEAD_DIM, B_HEAD_DIM), F32),
        ],
        compiler_params=pltpu.CompilerParams(
            dimension_semantics=("arbitrary", "arbitrary"), vmem_limit_bytes=VMEM_LIMIT),
        name="gdn",
    )(gdn_in, sm, alog_row, dtb_row, norm_g)


def _ffn_body(x_ref, oa_ref, ob_ref, wo_ref, g2_ref, w1_ref, w2_ref, g3_ref, out_ref, *,
              final_norm):
    y = (x_ref[...] + _dot(oa_ref[...], wo_ref[0:A_WIDTH, :])
         + _dot(ob_ref[...], wo_ref[A_WIDTH:, :]))
    h = _rms(y, g2_ref[...]).astype(BF16)
    a = jnp.square(jnp.maximum(_dot(h, w1_ref[...]), 0.0)).astype(BF16)
    acc = y + _dot(a, w2_ref[...])
    out_ref[...] = _rms(acc, g3_ref[...]) if final_norm else acc


def _ffn(x2, oa, ob, wo, g2, w1, w2, g3, tm, final_norm):
    m = x2.shape[0]
    row = lambda i: (i, 0)
    const = lambda i: (0, 0)
    resident = functools.partial(pl.BlockSpec, index_map=const, pipeline_mode=pl.Buffered(1))
    return pl.pallas_call(
        functools.partial(_ffn_body, final_norm=final_norm),
        grid=(m // tm,),
        in_specs=[
            pl.BlockSpec((tm, D_MODEL), row),
            pl.BlockSpec((tm, A_WIDTH), row),
            pl.BlockSpec((tm, B_WIDTH), row),
            resident((D_MODEL, D_MODEL)),
            pl.BlockSpec((1, D_MODEL), const),
            resident((D_MODEL, D_FF)),
            resident((D_FF, D_MODEL)),
            pl.BlockSpec((1, D_MODEL), const),
        ],
        out_specs=pl.BlockSpec((tm, D_MODEL), row),
        out_shape=jax.ShapeDtypeStruct((m, D_MODEL), F32),
        compiler_params=pltpu.CompilerParams(
            dimension_semantics=("arbitrary",), vmem_limit_bytes=VMEM_LIMIT),
        name="ffn",
    )(x2, oa, ob, wo, g2, w1, w2, g3)


def _rope_tables(seq):
    half = A_HEAD_DIM // 2
    inv_freq = 1.0 / (ROPE_THETA ** (jnp.arange(half, dtype=F32) / half))
    ang = jnp.arange(seq).astype(F32)[:, None] * inv_freq[None, :]
    cos = jnp.cos(ang)
    sin = jnp.sin(ang)
    reps = LANES // A_HEAD_DIM
    return (jnp.tile(cos, (1, 2 * reps)),
            jnp.tile(jnp.concatenate([-sin, sin], axis=1), (1, reps)))


def _lane_row(vals, offset):
    return jnp.zeros((1, LANES), F32).at[0, offset:offset + vals.shape[0]].set(vals.astype(F32))


def kernel(x, norm_mix_g, w_in, conv_w, a_log, dt_bias, gdn_norm_g, w_out,
           norm_ffn_g, w_ff1, w_ff2, norm_final_g):
    bsz, seq, d = x.shape
    depth = w_in.shape[0]
    m = bsz * seq
    cos_t, sin_t = _rope_tables(seq)
    x2 = x.reshape(m, d)
    for l in range(depth):
        qa, qi, kv, gdn_in, sm = _inproj(
            x2, norm_mix_g[l][None, :], jnp.swapaxes(w_in, 1, 2), l, cos_t, sin_t, conv_w[l],
            seq, tm=512)
        o_a = _mixer_a(qa, qi, kv, sm, bsz, seq)
        o_b = _gdn(gdn_in, sm, _lane_row(a_log[l], SM_DECAY),
                   _lane_row(dt_bias[l], SM_DECAY), gdn_norm_g[l][None, :], bsz, seq, tc=512)
        x2 = _ffn(x2, o_a, o_b, w_out[l].astype(BF16), norm_ffn_g[l][None, :],
                  w_ff1[l].astype(BF16), w_ff2[l].astype(BF16), norm_final_g[None, :],
                  tm=512, final_norm=(l == depth - 1))
    return x2.reshape(bsz, seq, d)
```

```python
import functools

import jax
import jax.numpy as jnp
from jax import lax
from jax.experimental import pallas as pl
from jax.experimental.pallas import tpu as pltpu

F32 = jnp.float32
BF16 = jnp.bfloat16
I32 = jnp.int32

D_MODEL = 1024
CHUNK = 64
A_QUERIES = 256
ROPE_THETA = 10000.0
EPS = 1e-6
A_HEADS = 8
A_KV_HEADS = 2
A_HEAD_DIM = 64
IDX_HEADS = 8
IDX_DIM = 64
TOPK_MAX = 256
B_HEADS = 4
B_HEAD_DIM = 128
CONV_WIDTH = 4
D_FF = 4 * D_MODEL

LANES = 128
A_WIDTH = A_HEADS * A_HEAD_DIM
KV_WIDTH = A_KV_HEADS * A_HEAD_DIM
B_WIDTH = B_HEADS * B_HEAD_DIM
GDN_WIDTH = 4 * B_WIDTH
SM_KI = 0
SM_WI = IDX_DIM
SM_BETA = SM_WI + IDX_HEADS
SM_DECAY = SM_BETA + B_HEADS
C_QA = 0
C_QI = C_QA + A_WIDTH
C_KV = C_QI + A_WIDTH
C_GDN = C_KV + 2 * KV_WIDTH
C_SM = C_GDN + GDN_WIDTH
IN_COLS = C_SM + LANES
REF_SIZES = (A_WIDTH, KV_WIDTH, KV_WIDTH, IDX_HEADS * IDX_DIM, IDX_DIM, IDX_HEADS,
             B_WIDTH, B_WIDTH, B_WIDTH, B_WIDTH, B_HEADS, B_HEADS)
REF_OFF = tuple(sum(REF_SIZES[:i]) for i in range(len(REF_SIZES)))
IN_DIM = sum(REF_SIZES)
IN_WEIGHT_MOVES = (
    (C_QA, REF_OFF[0], A_WIDTH),
    (C_QI, REF_OFF[3], A_WIDTH),
    (C_KV, REF_OFF[1], 2 * KV_WIDTH),
    (C_GDN, REF_OFF[6], GDN_WIDTH),
    (C_SM + SM_KI, REF_OFF[4], IDX_DIM + IDX_HEADS),
    (C_SM + SM_BETA, REF_OFF[10], 2 * B_HEADS),
)
IN_PAD = IN_COLS - (C_SM + SM_DECAY + B_HEADS)

VMEM_LIMIT = 56 * 1024 * 1024
CONV_PAD = 8
COUNT_GROUP = 4
FINE_STEPS = 17
FINE_CHECK = 10
COUNT_ROWS = 32
V_ROWS = A_HEAD_DIM + 16
ATT_HEADS = 2
ATT_WAVE = 4
KEY_BLOCK = 2 * LANES
NEG_BIG = -1e30
LOG2_E = 1.4426950408889634
NEG_INF_KEY = -(2 ** 31) + 0x7FFFFF


def _rms(x, g):
    return x * lax.rsqrt(jnp.mean(x * x, axis=-1, keepdims=True) + EPS) * g


def _dot(a, b):
    return jnp.dot(a, b, preferred_element_type=F32)


def _dot_nt(a, b):
    return lax.dot_general(a, b, (((1,), (1,)), ((), ())), preferred_element_type=F32)


def _inproj_body(x_ref, g_ref, win_ref, cos_ref, sin_ref, cw_ref,
                 qa_ref, qi_ref, kv_ref, gdn_ref, sm_ref, xpad_ref, w_ref, *, tiles_per_seq):
    @pl.when(pl.program_id(0) == 0)
    def _():
        step = 256
        for dst, src, width in IN_WEIGHT_MOVES[:-2]:
            for r0 in range(0, width, step):
                w_ref[dst + r0:dst + r0 + step, :] = win_ref[src + r0:src + r0 + step, :].astype(BF16)
        small = [win_ref[src:src + width, :] for _, src, width in IN_WEIGHT_MOVES[-2:]]
        small.append(jnp.zeros((IN_PAD, D_MODEL), F32))
        w_ref[C_SM:IN_COLS, :] = jnp.concatenate(small, axis=0).astype(BF16)

    h = _rms(x_ref[...], g_ref[...]).astype(BF16)
    cos = cos_ref[...]
    sin = sin_ref[...]
    lane = lax.broadcasted_iota(I32, cos.shape, 1)
    first_half = (lane & (A_HEAD_DIM - 1)) < A_HEAD_DIM // 2

    def rope(t):
        swapped = jnp.where(first_half, pltpu.roll(t, LANES - A_HEAD_DIM // 2, 1),
                            pltpu.roll(t, A_HEAD_DIM // 2, 1))
        return t * cos + swapped * sin

    def proj(c0, width):
        return _dot_nt(h, w_ref[c0:c0 + width, :])

    tm = x_ref.shape[0]
    conv_cols = 3 * B_WIDTH

    @pl.when(pl.program_id(0) % tiles_per_seq == 0)
    def _():
        xpad_ref[0:CONV_PAD, :] = jnp.zeros((CONV_PAD, conv_cols), F32)

    @pl.when(pl.program_id(0) % tiles_per_seq != 0)
    def _():
        xpad_ref[0:CONV_PAD, :] = xpad_ref[tm:tm + CONV_PAD, :]

    def stage_conv_input(seg):
        xpad_ref[CONV_PAD:CONV_PAD + tm, seg * B_WIDTH:(seg + 1) * B_WIDTH] = proj(
            C_GDN + seg * B_WIDTH, B_WIDTH)

    def conv_segment(seg):
        for hh in range(B_HEADS):
            cs = slice(seg * B_WIDTH + hh * LANES, seg * B_WIDTH + (hh + 1) * LANES)
            xa = xpad_ref[:, cs]
            y = cw_ref[0:1, cs] * xa
            for jj in range(1, CONV_WIDTH):
                y = cw_ref[jj:jj + 1, cs] * xa + pltpu.roll(y, 1, 0)
            y = y[CONV_PAD:, :]
            y = y * jax.nn.sigmoid(y)
            if seg < 2:
                y = y * lax.rsqrt(jnp.sum(y * y, axis=-1, keepdims=True) + EPS)
            if seg == 0:
                y = y * (B_HEAD_DIM ** -0.5)
            gdn_ref[:, cs] = y

    def store_queries_transposed(out_ref, acc, scale):
        for j in range(A_WIDTH // LANES):
            t = rope(acc[:, j * LANES:(j + 1) * LANES])
            if scale != 1.0:
                t = t * scale
            for b in range(tm // A_QUERIES):
                tt = t[b * A_QUERIES:(b + 1) * A_QUERIES, :].T
                for hh in range(LANES // A_HEAD_DIM):
                    h0 = (j * (LANES // A_HEAD_DIM) + hh) * A_QUERIES
                    out_ref[b, :, h0:h0 + A_QUERIES] = (
                        tt[hh * A_HEAD_DIM:(hh + 1) * A_HEAD_DIM, :].astype(BF16))

    stage_conv_input(0)
    acc = proj(C_QA, A_WIDTH)
    conv_segment(0)
    store_queries_transposed(qa_ref, acc, A_HEAD_DIM ** -0.5 * LOG2_E)
    stage_conv_input(1)
    acc = proj(C_QI, A_WIDTH)
    conv_segment(1)
    store_queries_transposed(qi_ref, acc, 1.0)
    stage_conv_input(2)
    acc = proj(C_KV, 2 * KV_WIDTH)
    acc_sm = proj(C_SM, LANES)
    z = proj(C_GDN + conv_cols, B_WIDTH)
    conv_segment(2)
    kv_ref[:, 0:KV_WIDTH] = rope(acc[:, 0:KV_WIDTH]).astype(BF16)
    kv_ref[:, KV_WIDTH:] = acc[:, KV_WIDTH:].astype(BF16)
    sm_ref[...] = jnp.where(lane < IDX_DIM, rope(acc_sm), acc_sm)
    gdn_ref[:, conv_cols:] = z * jax.nn.sigmoid(z)


def _inproj(x2, g, w, layer, cos_t, sin_t, conv_w, seq, tm):
    m = x2.shape[0]
    assert seq % tm == 0 and tm % A_QUERIES == 0, (seq, tm)
    nt = seq // tm
    row = lambda i: (i, 0)
    const = lambda i: (0, 0)
    return pl.pallas_call(
        functools.partial(_inproj_body, tiles_per_seq=nt),
        grid=(m // tm,),
        in_specs=[
            pl.BlockSpec((tm, D_MODEL), row),
            pl.BlockSpec((1, D_MODEL), const),
            pl.BlockSpec((None, IN_DIM, D_MODEL), lambda i: (layer, 0, 0),
                         pipeline_mode=pl.Buffered(1)),
            pl.BlockSpec((tm, LANES), lambda i: (i % nt, 0)),
            pl.BlockSpec((tm, LANES), lambda i: (i % nt, 0)),
            pl.BlockSpec((CONV_WIDTH, 3 * B_WIDTH), const),
        ],
        out_specs=[
            pl.BlockSpec((tm // A_QUERIES, A_HEAD_DIM, A_HEADS * A_QUERIES), lambda i: (i, 0, 0)),
            pl.BlockSpec((tm // A_QUERIES, IDX_DIM, IDX_HEADS * A_QUERIES), lambda i: (i, 0, 0)),
            pl.BlockSpec((tm, 2 * KV_WIDTH), row),
            pl.BlockSpec((tm, GDN_WIDTH), row),
            pl.BlockSpec((tm, LANES), row),
        ],
        out_shape=[
            jax.ShapeDtypeStruct((m // A_QUERIES, A_HEAD_DIM, A_HEADS * A_QUERIES), BF16),
            jax.ShapeDtypeStruct((m // A_QUERIES, IDX_DIM, IDX_HEADS * A_QUERIES), BF16),
            jax.ShapeDtypeStruct((m, 2 * KV_WIDTH), BF16),
            jax.ShapeDtypeStruct((m, GDN_WIDTH), F32),
            jax.ShapeDtypeStruct((m, LANES), F32),
        ],
        scratch_shapes=[
            pltpu.VMEM((tm + CONV_PAD, 3 * B_WIDTH), F32),
            pltpu.VMEM((IN_COLS, D_MODEL), BF16),
        ],
        compiler_params=pltpu.CompilerParams(
            dimension_semantics=("arbitrary",), vmem_limit_bytes=VMEM_LIMIT),
        name="inproj",
    )(x2, g, w, cos_t, sin_t, conv_w)


def _mixer_a_body(qat_ref, qit_ref, kv_ref, sm_ref, o_ref,
                  isc_ref, isb_ref, vt_ref, acc_ref, bias_ref, *, seq, topk):
    j = pl.program_id(1)
    nkt = (j + 1) * (A_QUERIES // LANES)
    n_heads_pair = A_WIDTH // LANES
    group = A_HEADS // A_KV_HEADS

    @pl.when(j == 0)
    def _():
        per_block = KEY_BLOCK // LANES
        for t in range(seq // LANES):
            vt = kv_ref[t * LANES:(t + 1) * LANES, KV_WIDTH:].astype(F32).T.astype(BF16)
            c0 = (t % per_block) * LANES
            for g in range(A_KV_HEADS):
                vt_ref[t // per_block, g * V_ROWS:g * V_ROWS + A_HEAD_DIM, c0:c0 + LANES] = (
                    vt[g * A_HEAD_DIM:(g + 1) * A_HEAD_DIM])
        for g in range(A_KV_HEADS):
            vt_ref[:, g * V_ROWS + A_HEAD_DIM:(g + 1) * V_ROWS, :] = jnp.ones(
                (seq // KEY_BLOCK, V_ROWS - A_HEAD_DIM, KEY_BLOCK), BF16)

    q0 = pl.multiple_of(j * A_QUERIES, A_QUERIES)
    w_t = sm_ref[pl.ds(q0, A_QUERIES), :].T[SM_WI:SM_WI + IDX_HEADS, :]
    w_t = w_t * ((IDX_HEADS ** -0.5) * (IDX_DIM ** -0.5))

    qlane = lax.broadcasted_iota(I32, (1, A_QUERIES), 1)
    limit = q0 + (lax.shift_right_logical(qlane, CHUNK.bit_length() - 1) + 1) * CHUNK
    tile_iota = lax.broadcasted_iota(I32, (LANES, A_QUERIES), 0)
    block_iota = lax.broadcasted_iota(I32, (KEY_BLOCK, A_QUERIES), 0)
    nkb = lax.shift_right_logical(nkt + (KEY_BLOCK // LANES - 1), (KEY_BLOCK // LANES).bit_length() - 1)

    def isc_body(kb, carry):
        r0 = pl.multiple_of(kb * KEY_BLOCK, KEY_BLOCK)
        kid = sm_ref[pl.ds(r0, KEY_BLOCK), SM_KI:SM_KI + IDX_DIM].astype(BF16)
        rel = _dot(kid, qit_ref[...])
        acc = jnp.zeros((KEY_BLOCK, A_QUERIES), F32)
        for h in range(IDX_HEADS):
            acc = acc + w_t[h:h + 1, :] * jnp.maximum(rel[:, h * A_QUERIES:(h + 1) * A_QUERIES], 0.0)
        acc = jnp.where(r0 + block_iota < limit, acc, -jnp.inf)
        isc_ref[pl.ds(r0, KEY_BLOCK), :] = acc
        isb_ref[pl.ds(r0, KEY_BLOCK), :] = acc.astype(BF16)
        return carry

    lax.fori_loop(0, nkb, isc_body, 0)

    ngrp = lax.shift_right_logical(nkt + (COUNT_GROUP - 1), COUNT_GROUP.bit_length() - 1)

    def fill_body(kt, carry):
        r0 = pl.multiple_of(kt * LANES, LANES)
        isc_ref[pl.ds(r0, LANES), :] = jnp.full((LANES, A_QUERIES), -jnp.inf, F32)
        isb_ref[pl.ds(r0, LANES), :] = jnp.full((LANES, A_QUERIES), -jnp.inf, BF16)
        return carry

    lax.fori_loop(nkb * (KEY_BLOCK // LANES), ngrp * COUNT_GROUP, fill_body, 0)

    def count(pred):
        def body(g, acc):
            for t in range(COUNT_GROUP):
                r0 = pl.multiple_of((g * COUNT_GROUP + t) * LANES, LANES)
                hit = jnp.where(pred(isc_ref[pl.ds(r0, LANES), :], r0 + tile_iota), 1.0, 0.0)
                acc = acc + hit.reshape(LANES // COUNT_ROWS, COUNT_ROWS, A_QUERIES).sum(axis=0)
            return acc
        acc = lax.fori_loop(0, ngrp, body, jnp.zeros((COUNT_ROWS, A_QUERIES), F32))
        return acc.sum(axis=0, keepdims=True)

    def count_coarse(thr):
        def body(g, acc):
            for t in range(COUNT_GROUP):
                r0 = pl.multiple_of((g * COUNT_GROUP + t) * LANES, LANES)
                hit = jnp.where(isb_ref[pl.ds(r0, LANES), :] >= thr, one_b, zero_b)
                hit = hit.reshape(LANES // COUNT_ROWS, COUNT_ROWS, A_QUERIES)
                part = hit[0]
                for i in range(1, LANES // COUNT_ROWS):
                    part = part + hit[i]
                acc = acc + part
            return acc
        acc = lax.fori_loop(0, ngrp, body, jnp.zeros((COUNT_ROWS, A_QUERIES), BF16))
        return acc.astype(F32).sum(axis=0, keepdims=True)

    kf = float(topk)
    one_b = jnp.ones((), BF16)
    zero_b = jnp.zeros((), BF16)

    def thr_coarse(key):
        bits = jnp.where(key >= 0, key, key ^ jnp.int32(0x7FFF))
        return lax.bitcast_convert_type(lax.shift_left(bits, 16), F32).astype(BF16)

    c0 = count_coarse(jnp.zeros((1, A_QUERIES), BF16))
    lo16 = jnp.where(c0 >= kf, jnp.int32(0), jnp.int32(-2 ** 15))

    def coarse_body(i, lo):
        trial = lo | lax.shift_left(jnp.int32(1), 14 - i)
        c = count_coarse(thr_coarse(trial))
        return jnp.where(c >= kf, trial, lo)

    lo16 = lax.fori_loop(0, 15, coarse_body, lo16)
    lo16 = jnp.maximum(lo16, jnp.int32(NEG_INF_KEY >> 16))
    center = lax.shift_left(lo16, 16) | jnp.where(lo16 < 0, jnp.int32(0xFFFF), jnp.int32(0))

    def thr_of(key):
        bits = jnp.where(key >= 0, key, key ^ jnp.int32(0x7FFFFFFF))
        return jnp.where(key < jnp.int32(NEG_INF_KEY), -jnp.inf, lax.bitcast_convert_type(bits, F32))

    def fine_body(i, st):
        lo, c_lo = st
        trial = lo + lax.shift_left(jnp.int32(1), FINE_STEPS - 1 - i)
        thr = thr_of(trial)
        c = count(lambda x, s: x >= thr)
        ok = c >= kf
        return jnp.where(ok, trial, lo), jnp.where(ok, c, c_lo)

    state = lax.fori_loop(0, FINE_CHECK, fine_body,
                          (center - jnp.int32(1 << 16), jnp.full((1, A_QUERIES), -1.0, F32)))
    settled = jnp.max(jnp.where(limit <= topk, 0.0, jnp.abs(state[1] - kf))) == 0.0
    lo, c_lo = lax.cond(settled, lambda: state,
                        lambda: lax.fori_loop(FINE_CHECK, FINE_STEPS, fine_body, state))
    kth = thr_of(lo)

    idx_bits = (seq - 1).bit_length()

    def tie_search():
        need = kf - count(lambda x, s: x > kth)

        def tie_body(i, last):
            trial = last | lax.shift_left(jnp.int32(1), idx_bits - 1 - i)
            c = count(lambda x, s: (x == kth) & (s < trial))
            return jnp.where(c < need, trial, last)

        return lax.fori_loop(0, idx_bits, tie_body, jnp.zeros((1, A_QUERIES), I32))

    def take_all_ties():
        return jnp.full((1, A_QUERIES), (1 << idx_bits) - 1, I32)

    n_ge = jnp.where(limit <= topk, 0.0, jnp.where(c_lo < 0.0, jnp.inf, c_lo))
    last = lax.cond(jnp.max(n_ge) > kf, tie_search, take_all_ties)

    acc_ref[...] = jnp.zeros_like(acc_ref)

    def att_body(kb, m_prev):
        r0 = pl.multiple_of(kb * KEY_BLOCK, KEY_BLOCK)
        x = isc_ref[pl.ds(r0, KEY_BLOCK), :]
        s_idx = r0 + block_iota
        sel = (s_idx < limit) & ((x > kth) | ((x == kth) & (s_idx <= last)))
        bias_ref[...] = jnp.where(sel, 0.0, NEG_BIG)
        k_tile = kv_ref[pl.ds(r0, KEY_BLOCK), 0:KV_WIDTH]
        v_t = vt_ref[kb]

        units = list(range(A_HEADS // ATT_HEADS))
        uw = ATT_HEADS * A_QUERIES
        us = [slice(u * uw, (u + 1) * uw) for u in units]
        kd = [slice((u * ATT_HEADS // group) * A_HEAD_DIM, (u * ATT_HEADS // group + 1) * A_HEAD_DIM)
              for u in units]
        vr = [slice((u * ATT_HEADS // group) * V_ROWS, (u * ATT_HEADS // group + 1) * V_ROWS)
              for u in units]
        cs = [slice((u * ATT_HEADS % group) * A_QUERIES, (u * ATT_HEADS % group + ATT_HEADS) * A_QUERIES)
              for u in units]
        m_new = {}
        for w0 in range(0, len(units), ATT_WAVE):
            wave = units[w0:w0 + ATT_WAVE]
            s = {u: _dot(k_tile[:, kd[u]], qat_ref[:, us[u]])
                 + jnp.concatenate([bias_ref[...]] * ATT_HEADS, axis=1) for u in wave}
            for u in wave:
                m_new[u] = jnp.maximum(m_prev[:, us[u]], jnp.max(s[u], axis=0, keepdims=True))
            alpha = {u: jnp.exp2(m_prev[:, us[u]] - m_new[u]) for u in wave}
            p = {u: jnp.exp2(s[u] - m_new[u]).astype(BF16) for u in wave}
            pv = {u: _dot(v_t[vr[u], :], p[u]) for u in wave}
            for u in wave:
                acc_ref[vr[u], cs[u]] = alpha[u] * acc_ref[vr[u], cs[u]] + pv[u]
        return jnp.concatenate([m_new[u] for u in units], axis=1)

    lax.fori_loop(0, nkb, att_body, jnp.full((1, A_HEADS * A_QUERIES), NEG_BIG, F32))

    for p in range(n_heads_pair):
        g = (2 * p) // group
        parts = []
        for h in (2 * p, 2 * p + 1):
            hq = slice((h % group) * A_QUERIES, (h % group + 1) * A_QUERIES)
            parts.append(acc_ref[g * V_ROWS:g * V_ROWS + A_HEAD_DIM, hq]
                         / acc_ref[g * V_ROWS + A_HEAD_DIM:g * V_ROWS + A_HEAD_DIM + 1, hq])
        o_ref[:, p * LANES:(p + 1) * LANES] = jnp.concatenate(parts, axis=0).T.astype(BF16)


def _mixer_a(qa, qi, kv, sm, bsz, seq):
    nq = seq // A_QUERIES
    assert seq % (COUNT_GROUP * LANES) == 0, seq
    assert seq // COUNT_ROWS <= 256, seq
    topk = min(TOPK_MAX, seq // 4)
    qrow = lambda b, j: (b * nq + j, 0)
    qblock = lambda b, j: (b * nq + j, 0, 0)
    brow = lambda b, j: (b, 0)
    return pl.pallas_call(
        functools.partial(_mixer_a_body, seq=seq, topk=topk),
        grid=(bsz, nq),
        in_specs=[
            pl.BlockSpec((None, A_HEAD_DIM, A_HEADS * A_QUERIES), qblock),
            pl.BlockSpec((None, IDX_DIM, IDX_HEADS * A_QUERIES), qblock),
            pl.BlockSpec((seq, 2 * KV_WIDTH), brow),
            pl.BlockSpec((seq, LANES), brow),
        ],
        out_specs=pl.BlockSpec((A_QUERIES, A_WIDTH), qrow),
        out_shape=jax.ShapeDtypeStruct((bsz * seq, A_WIDTH), BF16),
        scratch_shapes=[
            pltpu.VMEM((seq, A_QUERIES), F32),
            pltpu.VMEM((seq, A_QUERIES), BF16),
            pltpu.VMEM((seq // KEY_BLOCK, A_KV_HEADS * V_ROWS, KEY_BLOCK), BF16),
            pltpu.VMEM((A_KV_HEADS * V_ROWS, (A_HEADS // A_KV_HEADS) * A_QUERIES), F32),
            pltpu.VMEM((KEY_BLOCK, A_QUERIES), F32),
        ],
        compiler_params=pltpu.CompilerParams(
            dimension_semantics=("arbitrary", "arbitrary"), vmem_limit_bytes=VMEM_LIMIT),
        name="mixer_a",
    )(qa, qi, kv, sm)


def _gdn_body(gdn_ref, sm_ref, alog_ref, dtb_ref, ng_ref, o_ref,
              gate_ref, gct_ref, state_ref, oc_ref, lhs_ref, bm_ref, *, tc):
    tb = pl.program_id(1)
    n_chunks = tc // CHUNK
    prep_unroll = 8
    q_col, k_col, v_col, z_col = (i * B_WIDTH for i in range(4))

    @pl.when(tb == 0)
    def _():
        state_ref[...] = jnp.zeros_like(state_ref)

    sm = sm_ref[...]
    beta = jax.nn.sigmoid(sm)
    z = sm + dtb_ref[...]
    softplus = jnp.maximum(z, 0.0) + jnp.log(1.0 + jnp.exp(-jnp.abs(z)))
    g = -jnp.exp(alog_ref[...]) * softplus
    rin = lax.broadcasted_iota(I32, (tc, LANES), 0) & (CHUNK - 1)
    gc = g
    step = 1
    while step < CHUNK:
        gc = gc + jnp.where(rin >= step, pltpu.roll(gc, step, 0), 0.0)
        step *= 2
    gc3 = gc.reshape(n_chunks, CHUNK, LANES)
    g_last = jnp.broadcast_to(gc3[:, CHUNK - 1:CHUNK, :], gc3.shape).reshape(tc, LANES)
    gate_ref[0] = beta
    gate_ref[1] = gc
    gate_ref[2] = jnp.exp(gc)
    gate_ref[3] = jnp.exp(g_last - gc)
    gate_ref[4] = jnp.exp(g_last)
    for i in range(tc // LANES):
        t = gc[i * LANES:(i + 1) * LANES, :].T
        for half in range(LANES // CHUNK):
            gct_ref[i * (LANES // CHUNK) + half] = t[:, half * CHUNK:(half + 1) * CHUNK]

    ci = lax.broadcasted_iota(I32, (CHUNK, CHUNK), 0)
    si = lax.broadcasted_iota(I32, (CHUNK, CHUNK), 1)
    wl = lax.broadcasted_iota(I32, (CHUNK, 2 * CHUNK), 1)
    wr = lax.broadcasted_iota(I32, (CHUNK, 2 * CHUNK), 0)
    right = wl >= CHUNK
    eye_right = jnp.where(wl == wr + CHUNK, 1.0, 0.0)

    def prep_body(cg, carry):
        units = [(cg * prep_unroll + cc, h) for cc in range(prep_unroll) for h in range(B_HEADS)]
        rows = [pl.ds(pl.multiple_of(c * CHUNK, CHUNK), CHUNK) for c, _ in units]
        hsl = [slice(h * LANES, (h + 1) * LANES) for _, h in units]
        idx = range(len(units))

        def col(i, gate, off):
            h = units[i][1]
            return gate_ref[gate, rows[i], off + h:off + h + 1]

        def seg(i, col0):
            h = units[i][1]
            return gdn_ref[rows[i], col0 + h * LANES:col0 + (h + 1) * LANES]

        q = [seg(i, q_col) for i in idx]
        k = [seg(i, k_col) for i in idx]
        kb = [k[i] * col(i, 0, SM_BETA) for i in idx]
        kq = [_dot_nt(jnp.concatenate([kb[i], q[i]], axis=0).astype(BF16), k[i].astype(BF16))
              for i in idx]
        decay = []
        for i, (c, h) in enumerate(units):
            d = col(i, 1, SM_DECAY) - gct_ref[c][SM_DECAY + h:SM_DECAY + h + 1, :]
            decay.append(jnp.where(ci >= si, jnp.exp(jnp.where(ci >= si, d, 0.0)), 0.0))
        wmat = []
        for i, (c, h) in enumerate(units):
            n_mat = jnp.where(ci > si, -(kq[i][0:CHUNK] * decay[i]), 0.0)
            wmat.append(jnp.concatenate([n_mat, jnp.zeros_like(n_mat)], axis=1) + eye_right)
        pw = 1
        while pw < CHUNK:
            wb = [wmat[i].astype(BF16) for i in idx]
            wmat = [_dot(wb[i][:, 0:CHUNK], wb[i]) + jnp.where(right, wmat[i], 0.0) for i in idx]
            pw *= 2
        eg = [col(i, 2, SM_DECAY) for i in idx]
        rhs = [jnp.concatenate([seg(i, v_col) * col(i, 0, SM_BETA), kb[i] * eg[i]],
                               axis=1).astype(BF16) for i in idx]
        sol = [_dot(wmat[i][:, CHUNK:].astype(BF16), rhs[i]).astype(BF16) for i in idx]
        att = [(kq[i][CHUNK:] * decay[i]).astype(BF16) for i in idx]
        k_tail_t = [(k[i] * col(i, 3, SM_DECAY)).T.astype(BF16) for i in idx]
        a_uw = [_dot(att[i], sol[i]) for i in idx]
        k_uw = [_dot(k_tail_t[i], sol[i]) for i in idx]
        for i, (c, h) in enumerate(units):
            oc_ref[rows[i], hsl[i]] = a_uw[i][:, 0:B_HEAD_DIM]
            lhs_ref[c, h, 0:CHUNK, :] = (q[i] * eg[i] - a_uw[i][:, B_HEAD_DIM:]).astype(BF16)
            lhs_ref[c, h, CHUNK:, :] = k_uw[i][:, B_HEAD_DIM:].astype(BF16)
            bm_ref[c, h] = k_uw[i][:, 0:B_HEAD_DIM]
        return carry

    lax.fori_loop(0, n_chunks // prep_unroll, prep_body, 0)

    def scan_body(c, carry):
        r0 = pl.multiple_of(c * CHUNK, CHUNK)
        rows = pl.ds(r0, CHUNK)
        heads = range(B_HEADS)
        hsl = [slice(h * LANES, (h + 1) * LANES) for h in heads]
        s_prev = [state_ref[h] for h in heads]
        r = [_dot(lhs_ref[c, h], s_prev[h].astype(BF16)) for h in heads]
        for h in heads:
            gl = gate_ref[4, rows, SM_DECAY + h:SM_DECAY + h + 1][0:1, :]
            oc_ref[rows, hsl[h]] = oc_ref[rows, hsl[h]] + r[h][0:CHUNK]
            state_ref[h] = s_prev[h] * gl + bm_ref[c, h] - r[h][CHUNK:]
        return carry

    lax.fori_loop(0, n_chunks, scan_body, 0)

    for h in range(B_HEADS):
        hs = slice(h * LANES, (h + 1) * LANES)
        z_gate = gdn_ref[:, z_col + h * LANES:z_col + (h + 1) * LANES]
        o_ref[:, hs] = (_rms(oc_ref[:, hs], ng_ref[...]) * z_gate).astype(BF16)


def _gdn(gdn_in, sm, alog_row, dtb_row, norm_g, bsz, seq, tc):
    nt = seq // tc
    trow = lambda b, t: (b * nt + t, 0)
    const = lambda b, t: (0, 0)
    n_chunks = tc // CHUNK
    return pl.pallas_call(
        functools.partial(_gdn_body, tc=tc),
        grid=(bsz, nt),
        in_specs=[
            pl.BlockSpec((tc, GDN_WIDTH), trow),
            pl.BlockSpec((tc, LANES), trow),
            pl.BlockSpec((1, LANES), const),
            pl.BlockSpec((1, LANES), const),
            pl.BlockSpec((1, B_HEAD_DIM), const),
        ],
        out_specs=pl.BlockSpec((tc, B_WIDTH), trow),
        out_shape=jax.ShapeDtypeStruct((bsz * seq, B_WIDTH), BF16),
        scratch_shapes=[
            pltpu.VMEM((5, tc, LANES), F32),
            pltpu.VMEM((n_chunks, LANES, CHUNK), F32),
            pltpu.VMEM((B_HEADS, B_HEAD_DIM, B_HEAD_DIM), F32),
            pltpu.VMEM((tc, B_WIDTH), F32),
            pltpu.VMEM((n_chunks, B_HEADS, CHUNK + B_HEAD_DIM, B_HEAD_DIM), BF16),
            pltpu.VMEM((n_chunks, B_HEADS, B_HEAD_DIM, B_HEAD_DIM), F32),
        ],
        compiler_params=pltpu.CompilerParams(
            dimension_semantics=("arbitrary", "arbitrary"), vmem_limit_bytes=VMEM_LIMIT),
        name="gdn",
    )(gdn_in, sm, alog_row, dtb_row, norm_g)


def _ffn_body(x_ref, oa_ref, ob_ref, wo_ref, g2_ref, w1_ref, w2_ref, g3_ref, out_ref, *,
              final_norm):
    y = (x_ref[...] + _dot(oa_ref[...], wo_ref[0:A_WIDTH, :])
         + _dot(ob_ref[...], wo_ref[A_WIDTH:, :]))
    h = _rms(y, g2_ref[...]).astype(BF16)
    a = jnp.square(jnp.maximum(_dot(h, w1_ref[...]), 0.0)).astype(BF16)
    acc = y + _dot(a, w2_ref[...])
    out_ref[...] = _rms(acc, g3_ref[...]) if final_norm else acc


def _ffn(x2, oa, ob, wo, g2, w1, w2, g3, tm, final_norm):
    m = x2.shape[0]
    row = lambda i: (i, 0)
    const = lambda i: (0, 0)
    resident = functools.partial(pl.BlockSpec, index_map=const, pipeline_mode=pl.Buffered(1))
    return pl.pallas_call(
        functools.partial(_ffn_body, final_norm=final_norm),
        grid=(m // tm,),
        in_specs=[
            pl.BlockSpec((tm, D_MODEL), row),
            pl.BlockSpec((tm, A_WIDTH), row),
            pl.BlockSpec((tm, B_WIDTH), row),
            resident((D_MODEL, D_MODEL)),
            pl.BlockSpec((1, D_MODEL), const),
            resident((D_MODEL, D_FF)),
            resident((D_FF, D_MODEL)),
            pl.BlockSpec((1, D_MODEL), const),
        ],
        out_specs=pl.BlockSpec((tm, D_MODEL), row),
        out_shape=jax.ShapeDtypeStruct((m, D_MODEL), F32),
        compiler_params=pltpu.CompilerParams(
            dimension_semantics=("arbitrary",), vmem_limit_bytes=VMEM_LIMIT),
        name="ffn",
    )(x2, oa, ob, wo, g2, w1, w2, g3)


def _rope_tables(seq):
    half = A_HEAD_DIM // 2
    inv_freq = 1.0 / (ROPE_THETA ** (jnp.arange(half, dtype=F32) / half))
    ang = jnp.arange(seq).astype(F32)[:, None] * inv_freq[None, :]
    cos = jnp.cos(ang)
    sin = jnp.sin(ang)
    reps = LANES // A_HEAD_DIM
    return (jnp.tile(cos, (1, 2 * reps)),
            jnp.tile(jnp.concatenate([-sin, sin], axis=1), (1, reps)))


def _lane_row(vals, offset):
    return jnp.zeros((1, LANES), F32).at[0, offset:offset + vals.shape[0]].set(vals.astype(F32))


def kernel(x, norm_mix_g, w_in, conv_w, a_log, dt_bias, gdn_norm_g, w_out,
           norm_ffn_g, w_ff1, w_ff2, norm_final_g):
    bsz, seq, d = x.shape
    depth = w_in.shape[0]
    m = bsz * seq
    cos_t, sin_t = _rope_tables(seq)
    x2 = x.reshape(m, d)
    for l in range(depth):
        qa, qi, kv, gdn_in, sm = _inproj(
            x2, norm_mix_g[l][None, :], jnp.swapaxes(w_in, 1, 2), l, cos_t, sin_t, conv_w[l],
            seq, tm=512)
        o_a = _mixer_a(qa, qi, kv, sm, bsz, seq)
        o_b = _gdn(gdn_in, sm, _lane_row(a_log[l], SM_DECAY),
                   _lane_row(dt_bias[l], SM_DECAY), gdn_norm_g[l][None, :], bsz, seq, tc=512)
        x2 = _ffn(x2, o_a, o_b, w_out[l].astype(BF16), norm_ffn_g[l][None, :],
                  w_ff1[l].astype(BF16), w_ff2[l].astype(BF16), norm_final_g[None, :],
                  tm=512, final_norm=(l == depth - 1))
    return x2.reshape(bsz, seq, d)
```

```python
import functools

import jax
import jax.numpy as jnp
from jax import lax
from jax.experimental import pallas as pl
from jax.experimental.pallas import tpu as pltpu

F32 = jnp.float32
BF16 = jnp.bfloat16
I32 = jnp.int32

D_MODEL = 1024
CHUNK = 64
A_QUERIES = 256
ROPE_THETA = 10000.0
EPS = 1e-6
A_HEADS = 8
A_KV_HEADS = 2
A_HEAD_DIM = 64
IDX_HEADS = 8
IDX_DIM = 64
TOPK_MAX = 256
B_HEADS = 4
B_HEAD_DIM = 128
CONV_WIDTH = 4
D_FF = 4 * D_MODEL

LANES = 128
A_WIDTH = A_HEADS * A_HEAD_DIM
KV_WIDTH = A_KV_HEADS * A_HEAD_DIM
B_WIDTH = B_HEADS * B_HEAD_DIM
GDN_WIDTH = 4 * B_WIDTH
SM_KI = 0
SM_WI = IDX_DIM
SM_BETA = SM_WI + IDX_HEADS
SM_DECAY = SM_BETA + B_HEADS
C_QA = 0
C_QI = C_QA + A_WIDTH
C_KV = C_QI + A_WIDTH
C_GDN = C_KV + 2 * KV_WIDTH
C_SM = C_GDN + GDN_WIDTH
IN_COLS = C_SM + LANES
REF_SIZES = (A_WIDTH, KV_WIDTH, KV_WIDTH, IDX_HEADS * IDX_DIM, IDX_DIM, IDX_HEADS,
             B_WIDTH, B_WIDTH, B_WIDTH, B_WIDTH, B_HEADS, B_HEADS)
REF_OFF = tuple(sum(REF_SIZES[:i]) for i in range(len(REF_SIZES)))
IN_DIM = sum(REF_SIZES)
IN_WEIGHT_MOVES = (
    (C_QA, REF_OFF[0], A_WIDTH),
    (C_QI, REF_OFF[3], A_WIDTH),
    (C_KV, REF_OFF[1], 2 * KV_WIDTH),
    (C_GDN, REF_OFF[6], GDN_WIDTH),
    (C_SM + SM_KI, REF_OFF[4], IDX_DIM + IDX_HEADS),
    (C_SM + SM_BETA, REF_OFF[10], 2 * B_HEADS),
)
IN_PAD = IN_COLS - (C_SM + SM_DECAY + B_HEADS)

VMEM_LIMIT = 56 * 1024 * 1024
CONV_PAD = 8
COUNT_GROUP = 4
FINE_STEPS = 17
FINE_CHECK = 10
COUNT_ROWS = 32
V_ROWS = A_HEAD_DIM + 16
ATT_HEADS = 2
ATT_WAVE = 4
KEY_BLOCK = 2 * LANES
NEG_BIG = -1e30
LOG2_E = 1.4426950408889634
NEG_INF_KEY = -(2 ** 31) + 0x7FFFFF


def _rms(x, g):
    return x * lax.rsqrt(jnp.mean(x * x, axis=-1, keepdims=True) + EPS) * g


def _dot(a, b):
    return jnp.dot(a, b, preferred_element_type=F32)


def _dot_nt(a, b):
    return lax.dot_general(a, b, (((1,), (1,)), ((), ())), preferred_element_type=F32)


def _inproj_body(x_ref, g_ref, win_ref, cos_ref, sin_ref, cw_ref,
                 qa_ref, qi_ref, kv_ref, gdn_ref, sm_ref, xpad_ref, w_ref, *, tiles_per_seq):
    @pl.when(pl.program_id(0) == 0)
    def _():
        step = 256
        for dst, src, width in IN_WEIGHT_MOVES[:-2]:
            for r0 in range(0, width, step):
                w_ref[dst + r0:dst + r0 + step, :] = win_ref[src + r0:src + r0 + step, :].astype(BF16)
        small = [win_ref[src:src + width, :] for _, src, width in IN_WEIGHT_MOVES[-2:]]
        small.append(jnp.zeros((IN_PAD, D_MODEL), F32))
        w_ref[C_SM:IN_COLS, :] = jnp.concatenate(small, axis=0).astype(BF16)

    h = _rms(x_ref[...], g_ref[...]).astype(BF16)
    cos = cos_ref[...]
    sin = sin_ref[...]
    lane = lax.broadcasted_iota(I32, cos.shape, 1)
    first_half = (lane & (A_HEAD_DIM - 1)) < A_HEAD_DIM // 2

    def rope(t):
        swapped = jnp.where(first_half, pltpu.roll(t, LANES - A_HEAD_DIM // 2, 1),
                            pltpu.roll(t, A_HEAD_DIM // 2, 1))
        return t * cos + swapped * sin

    def proj(c0, width):
        return _dot_nt(h, w_ref[c0:c0 + width, :])

    tm = x_ref.shape[0]
    conv_cols = 3 * B_WIDTH

    @pl.when(pl.program_id(0) % tiles_per_seq == 0)
    def _():
        xpad_ref[0:CONV_PAD, :] = jnp.zeros((CONV_PAD, conv_cols), F32)

    @pl.when(pl.program_id(0) % tiles_per_seq != 0)
    def _():
        xpad_ref[0:CONV_PAD, :] = xpad_ref[tm:tm + CONV_PAD, :]

    def stage_conv_input(seg):
        xpad_ref[CONV_PAD:CONV_PAD + tm, seg * B_WIDTH:(seg + 1) * B_WIDTH] = proj(
            C_GDN + seg * B_WIDTH, B_WIDTH)

    def conv_segment(seg):
        for hh in range(B_HEADS):
            cs = slice(seg * B_WIDTH + hh * LANES, seg * B_WIDTH + (hh + 1) * LANES)
            xa = xpad_ref[:, cs]
            y = cw_ref[0:1, cs] * xa
            for jj in range(1, CONV_WIDTH):
                y = cw_ref[jj:jj + 1, cs] * xa + pltpu.roll(y, 1, 0)
            y = y[CONV_PAD:, :]
            y = y * jax.nn.sigmoid(y)
            if seg < 2:
                y = y * lax.rsqrt(jnp.sum(y * y, axis=-1, keepdims=True) + EPS)
            if seg == 0:
                y = y * (B_HEAD_DIM ** -0.5)
            gdn_ref[:, cs] = y

    def store_queries_transposed(out_ref, acc, scale):
        for j in range(A_WIDTH // LANES):
            t = rope(acc[:, j * LANES:(j + 1) * LANES])
            if scale != 1.0:
                t = t * scale
            for b in range(tm // A_QUERIES):
                tt = t[b * A_QUERIES:(b + 1) * A_QUERIES, :].T
                for hh in range(LANES // A_HEAD_DIM):
                    h0 = (j * (LANES // A_HEAD_DIM) + hh) * A_QUERIES
                    out_ref[b, :, h0:h0 + A_QUERIES] = (
                        tt[hh * A_HEAD_DIM:(hh + 1) * A_HEAD_DIM, :].astype(BF16))

    stage_conv_input(0)
    acc = proj(C_QA, A_WIDTH)
    conv_segment(0)
    store_queries_transposed(qa_ref, acc, A_HEAD_DIM ** -0.5 * LOG2_E)
    stage_conv_input(1)
    acc = proj(C_QI, A_WIDTH)
    conv_segment(1)
    store_queries_transposed(qi_ref, acc, 1.0)
    stage_conv_input(2)
    acc = proj(C_KV, 2 * KV_WIDTH)
    acc_sm = proj(C_SM, LANES)
    z = proj(C_GDN + conv_cols, B_WIDTH)
    conv_segment(2)
    kv_ref[:, 0:KV_WIDTH] = rope(acc[:, 0:KV_WIDTH]).astype(BF16)
    kv_ref[:, KV_WIDTH:] = acc[:, KV_WIDTH:].astype(BF16)
    sm_ref[...] = jnp.where(lane < IDX_DIM, rope(acc_sm), acc_sm)
    gdn_ref[:, conv_cols:] = z * jax.nn.sigmoid(z)


def _inproj(x2, g, w, layer, cos_t, sin_t, conv_w, seq, tm):
    m = x2.shape[0]
    assert seq % tm == 0 and tm % A_QUERIES == 0, (seq, tm)
    nt = seq // tm
    row = lambda i: (i, 0)
    const = lambda i: (0, 0)
    return pl.pallas_call(
        functools.partial(_inproj_body, tiles_per_seq=nt),
        grid=(m // tm,),
        in_specs=[
            pl.BlockSpec((tm, D_MODEL), row),
            pl.BlockSpec((1, D_MODEL), const),
            pl.BlockSpec((None, IN_DIM, D_MODEL), lambda i: (layer, 0, 0),
                         pipeline_mode=pl.Buffered(1)),
            pl.BlockSpec((tm, LANES), lambda i: (i % nt, 0)),
            pl.BlockSpec((tm, LANES), lambda i: (i % nt, 0)),
            pl.BlockSpec((CONV_WIDTH, 3 * B_WIDTH), const),
        ],
        out_specs=[
            pl.BlockSpec((tm // A_QUERIES, A_HEAD_DIM, A_HEADS * A_QUERIES), lambda i: (i, 0, 0)),
            pl.BlockSpec((tm // A_QUERIES, IDX_DIM, IDX_HEADS * A_QUERIES), lambda i: (i, 0, 0)),
            pl.BlockSpec((tm, 2 * KV_WIDTH), row),
            pl.BlockSpec((tm, GDN_WIDTH), row),
            pl.BlockSpec((tm, LANES), row),
        ],
        out_shape=[
            jax.ShapeDtypeStruct((m // A_QUERIES, A_HEAD_DIM, A_HEADS * A_QUERIES), BF16),
            jax.ShapeDtypeStruct((m // A_QUERIES, IDX_DIM, IDX_HEADS * A_QUERIES), BF16),
            jax.ShapeDtypeStruct((m, 2 * KV_WIDTH), BF16),
            jax.ShapeDtypeStruct((m, GDN_WIDTH), F32),
            jax.ShapeDtypeStruct((m, LANES), F32),
        ],
        scratch_shapes=[
            pltpu.VMEM((tm + CONV_PAD, 3 * B_WIDTH), F32),
            pltpu.VMEM((IN_COLS, D_MODEL), BF16),
        ],
        compiler_params=pltpu.CompilerParams(
            dimension_semantics=("arbitrary",), vmem_limit_bytes=VMEM_LIMIT),
        name="inproj",
    )(x2, g, w, cos_t, sin_t, conv_w)


def _mixer_a_body(qat_ref, qit_ref, kv_ref, sm_ref, o_ref,
                  isc_ref, isb_ref, vt_ref, acc_ref, bias_ref, *, seq, topk):
    j = pl.program_id(1)
    nkt = (j + 1) * (A_QUERIES // LANES)
    n_heads_pair = A_WIDTH // LANES
    group = A_HEADS // A_KV_HEADS

    @pl.when(j == 0)
    def _():
        per_block = KEY_BLOCK // LANES
        for t in range(seq // LANES):
            vt = kv_ref[t * LANES:(t + 1) * LANES, KV_WIDTH:].astype(F32).T.astype(BF16)
            c0 = (t % per_block) * LANES
            for g in range(A_KV_HEADS):
                vt_ref[t // per_block, g * V_ROWS:g * V_ROWS + A_HEAD_DIM, c0:c0 + LANES] = (
                    vt[g * A_HEAD_DIM:(g + 1) * A_HEAD_DIM])
        for g in range(A_KV_HEADS):
            vt_ref[:, g * V_ROWS + A_HEAD_DIM:(g + 1) * V_ROWS, :] = jnp.ones(
                (seq // KEY_BLOCK, V_ROWS - A_HEAD_DIM, KEY_BLOCK), BF16)

    q0 = pl.multiple_of(j * A_QUERIES, A_QUERIES)
    w_t = sm_ref[pl.ds(q0, A_QUERIES), :].T[SM_WI:SM_WI + IDX_HEADS, :]
    w_t = w_t * ((IDX_HEADS ** -0.5) * (IDX_DIM ** -0.5))

    qlane = lax.broadcasted_iota(I32, (1, A_QUERIES), 1)
    limit = q0 + (lax.shift_right_logical(qlane, CHUNK.bit_length() - 1) + 1) * CHUNK
    tile_iota = lax.broadcasted_iota(I32, (LANES, A_QUERIES), 0)
    block_iota = lax.broadcasted_iota(I32, (KEY_BLOCK, A_QUERIES), 0)
    nkb = lax.shift_right_logical(nkt + (KEY_BLOCK // LANES - 1), (KEY_BLOCK // LANES).bit_length() - 1)

    def isc_body(kb, carry):
        r0 = pl.multiple_of(kb * KEY_BLOCK, KEY_BLOCK)
        kid = sm_ref[pl.ds(r0, KEY_BLOCK), SM_KI:SM_KI + IDX_DIM].astype(BF16)
        rel = _dot(kid, qit_ref[...])
        acc = jnp.zeros((KEY_BLOCK, A_QUERIES), F32)
        for h in range(IDX_HEADS):
            acc = acc + w_t[h:h + 1, :] * jnp.maximum(rel[:, h * A_QUERIES:(h + 1) * A_QUERIES], 0.0)
        acc = jnp.where(r0 + block_iota < limit, acc, -jnp.inf)
        isc_ref[pl.ds(r0, KEY_BLOCK), :] = acc
        isb_ref[pl.ds(r0, KEY_BLOCK), :] = acc.astype(BF16)
        return carry

    lax.fori_loop(0, nkb, isc_body, 0)

    ngrp = lax.shift_right_logical(nkt + (COUNT_GROUP - 1), COUNT_GROUP.bit_length() - 1)

    def fill_body(kt, carry):
        r0 = pl.multiple_of(kt * LANES, LANES)
        isc_ref[pl.ds(r0, LANES), :] = jnp.full((LANES, A_QUERIES), -jnp.inf, F32)
        isb_ref[pl.ds(r0, LANES), :] = jnp.full((LANES, A_QUERIES), -jnp.inf, BF16)
        return carry

    lax.fori_loop(nkb * (KEY_BLOCK // LANES), ngrp * COUNT_GROUP, fill_body, 0)

    def count(pred):
        def body(g, acc):
            for t in range(COUNT_GROUP):
                r0 = pl.multiple_of((g * COUNT_GROUP + t) * LANES, LANES)
                hit = jnp.where(pred(isc_ref[pl.ds(r0, LANES), :], r0 + tile_iota), 1.0, 0.0)
                acc = acc + hit.reshape(LANES // COUNT_ROWS, COUNT_ROWS, A_QUERIES).sum(axis=0)
            return acc
        acc = lax.fori_loop(0, ngrp, body, jnp.zeros((COUNT_ROWS, A_QUERIES), F32))
        return acc.sum(axis=0, keepdims=True)

    def count_coarse(thr):
        def body(g, acc):
            for t in range(COUNT_GROUP):
                r0 = pl.multiple_of((g * COUNT_GROUP + t) * LANES, LANES)
                hit = jnp.where(isb_ref[pl.ds(r0, LANES), :] >= thr, one_b, zero_b)
                hit = hit.reshape(LANES // COUNT_ROWS, COUNT_ROWS, A_QUERIES)
                part = hit[0]
                for i in range(1, LANES // COUNT_ROWS):
                    part = part + hit[i]
                acc = acc + part
            return acc
        acc = lax.fori_loop(0, ngrp, body, jnp.zeros((COUNT_ROWS, A_QUERIES), BF16))
        return acc.astype(F32).sum(axis=0, keepdims=True)

    kf = float(topk)
    one_b = jnp.ones((), BF16)
    zero_b = jnp.zeros((), BF16)

    def thr_coarse(key):
        bits = jnp.where(key >= 0, key, key ^ jnp.int32(0x7FFF))
        return lax.bitcast_convert_type(lax.shift_left(bits, 16), F32).astype(BF16)

    c0 = count_coarse(jnp.zeros((1, A_QUERIES), BF16))
    lo16 = jnp.where(c0 >= kf, jnp.int32(0), jnp.int32(-2 ** 15))

    def coarse_body(i, lo):
        trial = lo | lax.shift_left(jnp.int32(1), 14 - i)
        c = count_coarse(thr_coarse(trial))
        return jnp.where(c >= kf, trial, lo)

    lo16 = lax.fori_loop(0, 15, coarse_body, lo16)
    lo16 = jnp.maximum(lo16, jnp.int32(NEG_INF_KEY >> 16))
    center = lax.shift_left(lo16, 16) | jnp.where(lo16 < 0, jnp.int32(0xFFFF), jnp.int32(0))

    def thr_of(key):
        bits = jnp.where(key >= 0, key, key ^ jnp.int32(0x7FFFFFFF))
        return jnp.where(key < jnp.int32(NEG_INF_KEY), -jnp.inf, lax.bitcast_convert_type(bits, F32))

    def fine_body(i, st):
        lo, c_lo = st
        trial = lo + lax.shift_left(jnp.int32(1), FINE_STEPS - 1 - i)
        thr = thr_of(trial)
        c = count(lambda x, s: x >= thr)
        ok = c >= kf
        return jnp.where(ok, trial, lo), jnp.where(ok, c, c_lo)

    state = lax.fori_loop(0, FINE_CHECK, fine_body,
                          (center - jnp.int32(1 << 16), jnp.full((1, A_QUERIES), -1.0, F32)))
    settled = jnp.max(jnp.where(limit <= topk, 0.0, jnp.abs(state[1] - kf))) == 0.0
    lo, c_lo = lax.cond(settled, lambda: state,
                        lambda: lax.fori_loop(FINE_CHECK, FINE_STEPS, fine_body, state))
    kth = thr_of(lo)

    n_ge = jnp.where(limit <= topk, 0.0, jnp.where(c_lo < 0.0, jnp.inf, c_lo))
    has_ties = jnp.max(n_ge) > kf

    acc_ref[...] = jnp.zeros_like(acc_ref)

    def select_plain(kb, x, s_idx, carry):
        return (s_idx < limit) & (x >= kth), carry

    def make_select_ties():
        need = kf - count(lambda x, s: x > kth)
        tri = jnp.where(lax.broadcasted_iota(I32, (KEY_BLOCK, KEY_BLOCK), 1)
                        <= lax.broadcasted_iota(I32, (KEY_BLOCK, KEY_BLOCK), 0), 1.0, 0.0).astype(BF16)

        def select_ties(kb, x, s_idx, before):
            eq = x == kth
            rank = before + _dot(tri, jnp.where(eq, 1.0, 0.0).astype(BF16))
            sel = (s_idx < limit) & ((x > kth) | (eq & (rank <= need)))
            return sel, rank[KEY_BLOCK - 1:KEY_BLOCK, :]

        return select_ties

    def att_body(select, kb, state):
        m_prev, sel_carry = state
        r0 = pl.multiple_of(kb * KEY_BLOCK, KEY_BLOCK)
        x = isc_ref[pl.ds(r0, KEY_BLOCK), :]
        s_idx = r0 + block_iota
        sel, sel_carry = select(kb, x, s_idx, sel_carry)
        bias_ref[...] = jnp.where(sel, 0.0, NEG_BIG)
        k_tile = kv_ref[pl.ds(r0, KEY_BLOCK), 0:KV_WIDTH]
        v_t = vt_ref[kb]

        units = list(range(A_HEADS // ATT_HEADS))
        uw = ATT_HEADS * A_QUERIES
        us = [slice(u * uw, (u + 1) * uw) for u in units]
        kd = [slice((u * ATT_HEADS // group) * A_HEAD_DIM, (u * ATT_HEADS // group + 1) * A_HEAD_DIM)
              for u in units]
        vr = [slice((u * ATT_HEADS // group) * V_ROWS, (u * ATT_HEADS // group + 1) * V_ROWS)
              for u in units]
        cs = [slice((u * ATT_HEADS % group) * A_QUERIES, (u * ATT_HEADS % group + ATT_HEADS) * A_QUERIES)
              for u in units]
        m_new = {}
        for w0 in range(0, len(units), ATT_WAVE):
            wave = units[w0:w0 + ATT_WAVE]
            s = {u: _dot(k_tile[:, kd[u]], qat_ref[:, us[u]])
                 + jnp.concatenate([bias_ref[...]] * ATT_HEADS, axis=1) for u in wave}
            for u in wave:
                m_new[u] = jnp.maximum(m_prev[:, us[u]], jnp.max(s[u], axis=0, keepdims=True))
            alpha = {u: jnp.exp2(m_prev[:, us[u]] - m_new[u]) for u in wave}
            p = {u: jnp.exp2(s[u] - m_new[u]).astype(BF16) for u in wave}
            pv = {u: _dot(v_t[vr[u], :], p[u]) for u in wave}
            for u in wave:
                acc_ref[vr[u], cs[u]] = alpha[u] * acc_ref[vr[u], cs[u]] + pv[u]
        return jnp.concatenate([m_new[u] for u in units], axis=1), sel_carry

    def attend(select):
        init = (jnp.full((1, A_HEADS * A_QUERIES), NEG_BIG, F32), jnp.zeros((1, A_QUERIES), F32))
        lax.fori_loop(0, nkb, functools.partial(att_body, select), init)

    lax.cond(has_ties, lambda: attend(make_select_ties()), lambda: attend(select_plain))

    for p in range(n_heads_pair):
        g = (2 * p) // group
        parts = []
        for h in (2 * p, 2 * p + 1):
            hq = slice((h % group) * A_QUERIES, (h % group + 1) * A_QUERIES)
            parts.append(acc_ref[g * V_ROWS:g * V_ROWS + A_HEAD_DIM, hq]
                         / acc_ref[g * V_ROWS + A_HEAD_DIM:g * V_ROWS + A_HEAD_DIM + 1, hq])
        o_ref[:, p * LANES:(p + 1) * LANES] = jnp.concatenate(parts, axis=0).T.astype(BF16)


def _mixer_a(qa, qi, kv, sm, bsz, seq):
    nq = seq // A_QUERIES
    assert seq % (COUNT_GROUP * LANES) == 0, seq
    assert seq // COUNT_ROWS <= 256, seq
    topk = min(TOPK_MAX, seq // 4)
    qrow = lambda b, j: (b * nq + j, 0)
    qblock = lambda b, j: (b * nq + j, 0, 0)
    brow = lambda b, j: (b, 0)
    return pl.pallas_call(
        functools.partial(_mixer_a_body, seq=seq, topk=topk),
        grid=(bsz, nq),
        in_specs=[
            pl.BlockSpec((None, A_HEAD_DIM, A_HEADS * A_QUERIES), qblock),
            pl.BlockSpec((None, IDX_DIM, IDX_HEADS * A_QUERIES), qblock),
            pl.BlockSpec((seq, 2 * KV_WIDTH), brow),
            pl.BlockSpec((seq, LANES), brow),
        ],
        out_specs=pl.BlockSpec((A_QUERIES, A_WIDTH), qrow),
        out_shape=jax.ShapeDtypeStruct((bsz * seq, A_WIDTH), BF16),
        scratch_shapes=[
            pltpu.VMEM((seq, A_QUERIES), F32),
            pltpu.VMEM((seq, A_QUERIES), BF16),
            pltpu.VMEM((seq // KEY_BLOCK, A_KV_HEADS * V_ROWS, KEY_BLOCK), BF16),
            pltpu.VMEM((A_KV_HEADS * V_ROWS, (A_HEADS // A_KV_HEADS) * A_QUERIES), F32),
            pltpu.VMEM((KEY_BLOCK, A_QUERIES), F32),
        ],
        compiler_params=pltpu.CompilerParams(
            dimension_semantics=("arbitrary", "arbitrary"), vmem_limit_bytes=VMEM_LIMIT),
        name="mixer_a",
    )(qa, qi, kv, sm)


def _gdn_body(gdn_ref, sm_ref, alog_ref, dtb_ref, ng_ref, o_ref,
              gate_ref, gct_ref, state_ref, oc_ref, lhs_ref, bm_ref, *, tc):
    tb = pl.program_id(1)
    n_chunks = tc // CHUNK
    prep_unroll = 8
    q_col, k_col, v_col, z_col = (i * B_WIDTH for i in range(4))

    @pl.when(tb == 0)
    def _():
        state_ref[...] = jnp.zeros_like(state_ref)

    sm = sm_ref[...]
    beta = jax.nn.sigmoid(sm)
    z = sm + dtb_ref[...]
    softplus = jnp.maximum(z, 0.0) + jnp.log(1.0 + jnp.exp(-jnp.abs(z)))
    g = -jnp.exp(alog_ref[...]) * softplus
    rin = lax.broadcasted_iota(I32, (tc, LANES), 0) & (CHUNK - 1)
    gc = g
    step = 1
    while step < CHUNK:
        gc = gc + jnp.where(rin >= step, pltpu.roll(gc, step, 0), 0.0)
        step *= 2
    gc3 = gc.reshape(n_chunks, CHUNK, LANES)
    g_last = jnp.broadcast_to(gc3[:, CHUNK - 1:CHUNK, :], gc3.shape).reshape(tc, LANES)
    gate_ref[0] = beta
    gate_ref[1] = gc
    gate_ref[2] = jnp.exp(gc)
    gate_ref[3] = jnp.exp(g_last - gc)
    gate_ref[4] = jnp.exp(g_last)
    for i in range(tc // LANES):
        t = gc[i * LANES:(i + 1) * LANES, :].T
        for half in range(LANES // CHUNK):
            gct_ref[i * (LANES // CHUNK) + half] = t[:, half * CHUNK:(half + 1) * CHUNK]

    ci = lax.broadcasted_iota(I32, (CHUNK, CHUNK), 0)
    si = lax.broadcasted_iota(I32, (CHUNK, CHUNK), 1)
    wl = lax.broadcasted_iota(I32, (CHUNK, 2 * CHUNK), 1)
    wr = lax.broadcasted_iota(I32, (CHUNK, 2 * CHUNK), 0)
    right = wl >= CHUNK
    eye_right = jnp.where(wl == wr + CHUNK, 1.0, 0.0)

    def prep_body(cg, carry):
        units = [(cg * prep_unroll + cc, h) for cc in range(prep_unroll) for h in range(B_HEADS)]
        rows = [pl.ds(pl.multiple_of(c * CHUNK, CHUNK), CHUNK) for c, _ in units]
        hsl = [slice(h * LANES, (h + 1) * LANES) for _, h in units]
        idx = range(len(units))

        def col(i, gate, off):
            h = units[i][1]
            return gate_ref[gate, rows[i], off + h:off + h + 1]

        def seg(i, col0):
            h = units[i][1]
            return gdn_ref[rows[i], col0 + h * LANES:col0 + (h + 1) * LANES]

        q = [seg(i, q_col) for i in idx]
        k = [seg(i, k_col) for i in idx]
        kb = [k[i] * col(i, 0, SM_BETA) for i in idx]
        kq = [_dot_nt(jnp.concatenate([kb[i], q[i]], axis=0).astype(BF16), k[i].astype(BF16))
              for i in idx]
        decay = []
        for i, (c, h) in enumerate(units):
            d = col(i, 1, SM_DECAY) - gct_ref[c][SM_DECAY + h:SM_DECAY + h + 1, :]
            decay.append(jnp.where(ci >= si, jnp.exp(jnp.where(ci >= si, d, 0.0)), 0.0))
        wmat = []
        for i, (c, h) in enumerate(units):
            n_mat = jnp.where(ci > si, -(kq[i][0:CHUNK] * decay[i]), 0.0)
            wmat.append(jnp.concatenate([n_mat, jnp.zeros_like(n_mat)], axis=1) + eye_right)
        pw = 1
        while pw < CHUNK:
            wb = [wmat[i].astype(BF16) for i in idx]
            wmat = [_dot(wb[i][:, 0:CHUNK], wb[i]) + jnp.where(right, wmat[i], 0.0) for i in idx]
            pw *= 2
        eg = [col(i, 2, SM_DECAY) for i in idx]
        rhs = [jnp.concatenate([seg(i, v_col) * col(i, 0, SM_BETA), kb[i] * eg[i]],
                               axis=1).astype(BF16) for i in idx]
        sol = [_dot(wmat[i][:, CHUNK:].astype(BF16), rhs[i]).astype(BF16) for i in idx]
        att = [(kq[i][CHUNK:] * decay[i]).astype(BF16) for i in idx]
        k_tail_t = [(k[i] * col(i, 3, SM_DECAY)).T.astype(BF16) for i in idx]
        a_uw = [_dot(att[i], sol[i]) for i in idx]
        k_uw = [_dot(k_tail_t[i], sol[i]) for i in idx]
        for i, (c, h) in enumerate(units):
            oc_ref[rows[i], hsl[i]] = a_uw[i][:, 0:B_HEAD_DIM]
            lhs_ref[c, h, 0:CHUNK, :] = (q[i] * eg[i] - a_uw[i][:, B_HEAD_DIM:]).astype(BF16)
            lhs_ref[c, h, CHUNK:, :] = k_uw[i][:, B_HEAD_DIM:].astype(BF16)
            bm_ref[c, h] = k_uw[i][:, 0:B_HEAD_DIM]
        return carry

    lax.fori_loop(0, n_chunks // prep_unroll, prep_body, 0)

    def scan_body(c, carry):
        r0 = pl.multiple_of(c * CHUNK, CHUNK)
        rows = pl.ds(r0, CHUNK)
        heads = range(B_HEADS)
        hsl = [slice(h * LANES, (h + 1) * LANES) for h in heads]
        s_prev = [state_ref[h] for h in heads]
        r = [_dot(lhs_ref[c, h], s_prev[h].astype(BF16)) for h in heads]
        for h in heads:
            gl = gate_ref[4, rows, SM_DECAY + h:SM_DECAY + h + 1][0:1, :]
            oc_ref[rows, hsl[h]] = oc_ref[rows, hsl[h]] + r[h][0:CHUNK]
            state_ref[h] = s_prev[h] * gl + bm_ref[c, h] - r[h][CHUNK:]
        return carry

    lax.fori_loop(0, n_chunks, scan_body, 0)

    for h in range(B_HEADS):
        hs = slice(h * LANES, (h + 1) * LANES)
        z_gate = gdn_ref[:, z_col + h * LANES:z_col + (h + 1) * LANES]
        o_ref[:, hs] = (_rms(oc_ref[:, hs], ng_ref[...]) * z_gate).astype(BF16)


def _gdn(gdn_in, sm, alog_row, dtb_row, norm_g, bsz, seq, tc):
    nt = seq // tc
    trow = lambda b, t: (b * nt + t, 0)
    const = lambda b, t: (0, 0)
    n_chunks = tc // CHUNK
    return pl.pallas_call(
        functools.partial(_gdn_body, tc=tc),
        grid=(bsz, nt),
        in_specs=[
            pl.BlockSpec((tc, GDN_WIDTH), trow),
            pl.BlockSpec((tc, LANES), trow),
            pl.BlockSpec((1, LANES), const),
            pl.BlockSpec((1, LANES), const),
            pl.BlockSpec((1, B_HEAD_DIM), const),
        ],
        out_specs=pl.BlockSpec((tc, B_WIDTH), trow),
        out_shape=jax.ShapeDtypeStruct((bsz * seq, B_WIDTH), BF16),
        scratch_shapes=[
            pltpu.VMEM((5, tc, LANES), F32),
            pltpu.VMEM((n_chunks, LANES, CHUNK), F32),
            pltpu.VMEM((B_HEADS, B_HEAD_DIM, B_HEAD_DIM), F32),
            pltpu.VMEM((tc, B_WIDTH), F32),
            pltpu.VMEM((n_chunks, B_HEADS, CHUNK + B_HEAD_DIM, B_HEAD_DIM), BF16),
            pltpu.VMEM((n_chunks, B_HEADS, B_HEAD_DIM, B_HEAD_DIM), F32),
        ],
        compiler_params=pltpu.CompilerParams(
            dimension_semantics=("arbitrary", "arbitrary"), vmem_limit_bytes=VMEM_LIMIT),
        name="gdn",
    )(gdn_in, sm, alog_row, dtb_row, norm_g)


def _ffn_body(x_ref, oa_ref, ob_ref, wo_ref, g2_ref, w1_ref, w2_ref, g3_ref, out_ref, *,
              final_norm):
    y = (x_ref[...] + _dot(oa_ref[...], wo_ref[0:A_WIDTH, :])
         + _dot(ob_ref[...], wo_ref[A_WIDTH:, :]))
    h = _rms(y, g2_ref[...]).astype(BF16)
    a = jnp.square(jnp.maximum(_dot(h, w1_ref[...]), 0.0)).astype(BF16)
    acc = y + _dot(a, w2_ref[...])
    out_ref[...] = _rms(acc, g3_ref[...]) if final_norm else acc


def _ffn(x2, oa, ob, wo, g2, w1, w2, g3, tm, final_norm):
    m = x2.shape[0]
    row = lambda i: (i, 0)
    const = lambda i: (0, 0)
    resident = functools.partial(pl.BlockSpec, index_map=const, pipeline_mode=pl.Buffered(1))
    return pl.pallas_call(
        functools.partial(_ffn_body, final_norm=final_norm),
        grid=(m // tm,),
        in_specs=[
            pl.BlockSpec((tm, D_MODEL), row),
            pl.BlockSpec((tm, A_WIDTH), row),
            pl.BlockSpec((tm, B_WIDTH), row),
            resident((D_MODEL, D_MODEL)),
            pl.BlockSpec((1, D_MODEL), const),
            resident((D_MODEL, D_FF)),
            resident((D_FF, D_MODEL)),
            pl.BlockSpec((1, D_MODEL), const),
        ],
        out_specs=pl.BlockSpec((tm, D_MODEL), row),
        out_shape=jax.ShapeDtypeStruct((m, D_MODEL), F32),
        compiler_params=pltpu.CompilerParams(
            dimension_semantics=("arbitrary",), vmem_limit_bytes=VMEM_LIMIT),
        name="ffn",
    )(x2, oa, ob, wo, g2, w1, w2, g3)


def _rope_tables(seq):
    half = A_HEAD_DIM // 2
    inv_freq = 1.0 / (ROPE_THETA ** (jnp.arange(half, dtype=F32) / half))
    ang = jnp.arange(seq).astype(F32)[:, None] * inv_freq[None, :]
    cos = jnp.cos(ang)
    sin = jnp.sin(ang)
    reps = LANES // A_HEAD_DIM
    return (jnp.tile(cos, (1, 2 * reps)),
            jnp.tile(jnp.concatenate([-sin, sin], axis=1), (1, reps)))


def _lane_row(vals, offset):
    return jnp.zeros((1, LANES), F32).at[0, offset:offset + vals.shape[0]].set(vals.astype(F32))


def kernel(x, norm_mix_g, w_in, conv_w, a_log, dt_bias, gdn_norm_g, w_out,
           norm_ffn_g, w_ff1, w_ff2, norm_final_g):
    bsz, seq, d = x.shape
    depth = w_in.shape[0]
    m = bsz * seq
    cos_t, sin_t = _rope_tables(seq)
    x2 = x.reshape(m, d)
    for l in range(depth):
        qa, qi, kv, gdn_in, sm = _inproj(
            x2, norm_mix_g[l][None, :], jnp.swapaxes(w_in, 1, 2), l, cos_t, sin_t, conv_w[l],
            seq, tm=512)
        o_a = _mixer_a(qa, qi, kv, sm, bsz, seq)
        o_b = _gdn(gdn_in, sm, _lane_row(a_log[l], SM_DECAY),
                   _lane_row(dt_bias[l], SM_DECAY), gdn_norm_g[l][None, :], bsz, seq, tc=512)
        x2 = _ffn(x2, o_a, o_b, w_out[l].astype(BF16), norm_ffn_g[l][None, :],
                  w_ff1[l].astype(BF16), w_ff2[l].astype(BF16), norm_final_g[None, :],
                  tm=512, final_norm=(l == depth - 1))
    return x2.reshape(bsz, seq, d)
```

```python
import functools

import jax
import jax.numpy as jnp
from jax import lax
from jax.experimental import pallas as pl
from jax.experimental.pallas import tpu as pltpu

F32 = jnp.float32
BF16 = jnp.bfloat16
I32 = jnp.int32

D_MODEL = 1024
CHUNK = 64
A_QUERIES = 256
ROPE_THETA = 10000.0
EPS = 1e-6
A_HEADS = 8
A_KV_HEADS = 2
A_HEAD_DIM = 64
IDX_HEADS = 8
IDX_DIM = 64
TOPK_MAX = 256
B_HEADS = 4
B_HEAD_DIM = 128
CONV_WIDTH = 4
D_FF = 4 * D_MODEL

LANES = 128
A_WIDTH = A_HEADS * A_HEAD_DIM
KV_WIDTH = A_KV_HEADS * A_HEAD_DIM
B_WIDTH = B_HEADS * B_HEAD_DIM
GDN_WIDTH = 4 * B_WIDTH
SM_KI = 0
SM_WI = IDX_DIM
SM_BETA = SM_WI + IDX_HEADS
SM_DECAY = SM_BETA + B_HEADS
C_QA = 0
C_QI = C_QA + A_WIDTH
C_KV = C_QI + A_WIDTH
C_GDN = C_KV + 2 * KV_WIDTH
C_SM = C_GDN + GDN_WIDTH
IN_COLS = C_SM + LANES
REF_SIZES = (A_WIDTH, KV_WIDTH, KV_WIDTH, IDX_HEADS * IDX_DIM, IDX_DIM, IDX_HEADS,
             B_WIDTH, B_WIDTH, B_WIDTH, B_WIDTH, B_HEADS, B_HEADS)
REF_OFF = tuple(sum(REF_SIZES[:i]) for i in range(len(REF_SIZES)))
IN_DIM = sum(REF_SIZES)
IN_WEIGHT_MOVES = (
    (C_QA, REF_OFF[0], A_WIDTH),
    (C_QI, REF_OFF[3], A_WIDTH),
    (C_KV, REF_OFF[1], 2 * KV_WIDTH),
    (C_GDN, REF_OFF[6], GDN_WIDTH),
    (C_SM + SM_KI, REF_OFF[4], IDX_DIM + IDX_HEADS),
    (C_SM + SM_BETA, REF_OFF[10], 2 * B_HEADS),
)
IN_PAD = IN_COLS - (C_SM + SM_DECAY + B_HEADS)

VMEM_LIMIT = 56 * 1024 * 1024
CONV_PAD = 8
COUNT_GROUP = 4
FINE_STEPS = 17
FINE_CHECK = 6
COUNT_ROWS = 32
V_ROWS = A_HEAD_DIM + 16
ATT_HEADS = 2
ATT_WAVE = 4
KEY_BLOCK = 2 * LANES
NEG_BIG = -1e30
LOG2_E = 1.4426950408889634
NEG_INF_KEY = -(2 ** 31) + 0x7FFFFF
POS_INF_KEY = 0x7F800000


def _rms(x, g):
    return x * lax.rsqrt(jnp.mean(x * x, axis=-1, keepdims=True) + EPS) * g


def _dot(a, b):
    return jnp.dot(a, b, preferred_element_type=F32)


def _dot_nt(a, b):
    return lax.dot_general(a, b, (((1,), (1,)), ((), ())), preferred_element_type=F32)


def _inproj_body(x_ref, g_ref, win_ref, cos_ref, sin_ref, cw_ref,
                 qa_ref, qi_ref, kv_ref, gdn_ref, sm_ref, xpad_ref, w_ref, *, tiles_per_seq):
    @pl.when(pl.program_id(0) == 0)
    def _():
        step = 256
        for dst, src, width in IN_WEIGHT_MOVES[:-2]:
            for r0 in range(0, width, step):
                w_ref[dst + r0:dst + r0 + step, :] = win_ref[src + r0:src + r0 + step, :].astype(BF16)
        small = [win_ref[src:src + width, :] for _, src, width in IN_WEIGHT_MOVES[-2:]]
        small.append(jnp.zeros((IN_PAD, D_MODEL), F32))
        w_ref[C_SM:IN_COLS, :] = jnp.concatenate(small, axis=0).astype(BF16)

    h = _rms(x_ref[...], g_ref[...]).astype(BF16)
    cos = cos_ref[...]
    sin = sin_ref[...]
    lane = lax.broadcasted_iota(I32, cos.shape, 1)
    first_half = (lane & (A_HEAD_DIM - 1)) < A_HEAD_DIM // 2

    def rope(t):
        swapped = jnp.where(first_half, pltpu.roll(t, LANES - A_HEAD_DIM // 2, 1),
                            pltpu.roll(t, A_HEAD_DIM // 2, 1))
        return t * cos + swapped * sin

    def proj(c0, width):
        return _dot_nt(h, w_ref[c0:c0 + width, :])

    tm = x_ref.shape[0]
    conv_cols = 3 * B_WIDTH

    @pl.when(pl.program_id(0) % tiles_per_seq == 0)
    def _():
        xpad_ref[0:CONV_PAD, :] = jnp.zeros((CONV_PAD, conv_cols), F32)

    @pl.when(pl.program_id(0) % tiles_per_seq != 0)
    def _():
        xpad_ref[0:CONV_PAD, :] = xpad_ref[tm:tm + CONV_PAD, :]

    def stage_conv_input(seg):
        xpad_ref[CONV_PAD:CONV_PAD + tm, seg * B_WIDTH:(seg + 1) * B_WIDTH] = proj(
            C_GDN + seg * B_WIDTH, B_WIDTH)

    def conv_segment(seg):
        for hh in range(B_HEADS):
            cs = slice(seg * B_WIDTH + hh * LANES, seg * B_WIDTH + (hh + 1) * LANES)
            xa = xpad_ref[:, cs]
            y = cw_ref[0:1, cs] * xa
            for jj in range(1, CONV_WIDTH):
                y = cw_ref[jj:jj + 1, cs] * xa + pltpu.roll(y, 1, 0)
            y = y[CONV_PAD:, :]
            y = y * jax.nn.sigmoid(y)
            if seg < 2:
                y = y * lax.rsqrt(jnp.sum(y * y, axis=-1, keepdims=True) + EPS)
            if seg == 0:
                y = y * (B_HEAD_DIM ** -0.5)
            gdn_ref[:, cs] = y

    def store_queries_transposed(out_ref, acc, scale):
        for j in range(A_WIDTH // LANES):
            t = rope(acc[:, j * LANES:(j + 1) * LANES])
            if scale != 1.0:
                t = t * scale
            for b in range(tm // A_QUERIES):
                tt = t[b * A_QUERIES:(b + 1) * A_QUERIES, :].T
                for hh in range(LANES // A_HEAD_DIM):
                    h0 = (j * (LANES // A_HEAD_DIM) + hh) * A_QUERIES
                    out_ref[b, :, h0:h0 + A_QUERIES] = (
                        tt[hh * A_HEAD_DIM:(hh + 1) * A_HEAD_DIM, :].astype(BF16))

    stage_conv_input(0)
    acc = proj(C_QA, A_WIDTH)
    conv_segment(0)
    store_queries_transposed(qa_ref, acc, A_HEAD_DIM ** -0.5 * LOG2_E)
    stage_conv_input(1)
    acc = proj(C_QI, A_WIDTH)
    conv_segment(1)
    store_queries_transposed(qi_ref, acc, 1.0)
    stage_conv_input(2)
    acc = proj(C_KV, 2 * KV_WIDTH)
    acc_sm = proj(C_SM, LANES)
    z = proj(C_GDN + conv_cols, B_WIDTH)
    conv_segment(2)
    kv_ref[:, 0:KV_WIDTH] = rope(acc[:, 0:KV_WIDTH]).astype(BF16)
    kv_ref[:, KV_WIDTH:] = acc[:, KV_WIDTH:].astype(BF16)
    sm_ref[...] = jnp.where(lane < IDX_DIM, rope(acc_sm), acc_sm)
    gdn_ref[:, conv_cols:] = z * jax.nn.sigmoid(z)


def _inproj(x2, g, w, layer, cos_t, sin_t, conv_w, seq, tm):
    m = x2.shape[0]
    assert seq % tm == 0 and tm % A_QUERIES == 0, (seq, tm)
    nt = seq // tm
    row = lambda i: (i, 0)
    const = lambda i: (0, 0)
    return pl.pallas_call(
        functools.partial(_inproj_body, tiles_per_seq=nt),
        grid=(m // tm,),
        in_specs=[
            pl.BlockSpec((tm, D_MODEL), row),
            pl.BlockSpec((1, D_MODEL), const),
            pl.BlockSpec((None, IN_DIM, D_MODEL), lambda i: (layer, 0, 0),
                         pipeline_mode=pl.Buffered(1)),
            pl.BlockSpec((tm, LANES), lambda i: (i % nt, 0)),
            pl.BlockSpec((tm, LANES), lambda i: (i % nt, 0)),
            pl.BlockSpec((CONV_WIDTH, 3 * B_WIDTH), const),
        ],
        out_specs=[
            pl.BlockSpec((tm // A_QUERIES, A_HEAD_DIM, A_HEADS * A_QUERIES), lambda i: (i, 0, 0)),
            pl.BlockSpec((tm // A_QUERIES, IDX_DIM, IDX_HEADS * A_QUERIES), lambda i: (i, 0, 0)),
            pl.BlockSpec((tm, 2 * KV_WIDTH), row),
            pl.BlockSpec((tm, GDN_WIDTH), row),
            pl.BlockSpec((tm, LANES), row),
        ],
        out_shape=[
            jax.ShapeDtypeStruct((m // A_QUERIES, A_HEAD_DIM, A_HEADS * A_QUERIES), BF16),
            jax.ShapeDtypeStruct((m // A_QUERIES, IDX_DIM, IDX_HEADS * A_QUERIES), BF16),
            jax.ShapeDtypeStruct((m, 2 * KV_WIDTH), BF16),
            jax.ShapeDtypeStruct((m, GDN_WIDTH), F32),
            jax.ShapeDtypeStruct((m, LANES), F32),
        ],
        scratch_shapes=[
            pltpu.VMEM((tm + CONV_PAD, 3 * B_WIDTH), F32),
            pltpu.VMEM((IN_COLS, D_MODEL), BF16),
        ],
        compiler_params=pltpu.CompilerParams(
            dimension_semantics=("arbitrary",), vmem_limit_bytes=VMEM_LIMIT),
        name="inproj",
    )(x2, g, w, cos_t, sin_t, conv_w)


def _mixer_a_body(qat_ref, qit_ref, kv_ref, sm_ref, o_ref,
                  isc_ref, isb_ref, vt_ref, acc_ref, bias_ref, *, seq, topk):
    j = pl.program_id(1)
    nkt = (j + 1) * (A_QUERIES // LANES)
    n_heads_pair = A_WIDTH // LANES
    group = A_HEADS // A_KV_HEADS

    @pl.when(j == 0)
    def _():
        per_block = KEY_BLOCK // LANES
        for t in range(seq // LANES):
            vt = kv_ref[t * LANES:(t + 1) * LANES, KV_WIDTH:].astype(F32).T.astype(BF16)
            c0 = (t % per_block) * LANES
            for g in range(A_KV_HEADS):
                vt_ref[t // per_block, g * V_ROWS:g * V_ROWS + A_HEAD_DIM, c0:c0 + LANES] = (
                    vt[g * A_HEAD_DIM:(g + 1) * A_HEAD_DIM])
        for g in range(A_KV_HEADS):
            vt_ref[:, g * V_ROWS + A_HEAD_DIM:(g + 1) * V_ROWS, :] = jnp.ones(
                (seq // KEY_BLOCK, V_ROWS - A_HEAD_DIM, KEY_BLOCK), BF16)

    q0 = pl.multiple_of(j * A_QUERIES, A_QUERIES)
    w_t = sm_ref[pl.ds(q0, A_QUERIES), :].T[SM_WI:SM_WI + IDX_HEADS, :]
    w_t = w_t * ((IDX_HEADS ** -0.5) * (IDX_DIM ** -0.5))

    qlane = lax.broadcasted_iota(I32, (1, A_QUERIES), 1)
    limit = q0 + (lax.shift_right_logical(qlane, CHUNK.bit_length() - 1) + 1) * CHUNK
    tile_iota = lax.broadcasted_iota(I32, (LANES, A_QUERIES), 0)
    block_iota = lax.broadcasted_iota(I32, (KEY_BLOCK, A_QUERIES), 0)
    nkb = lax.shift_right_logical(nkt + (KEY_BLOCK // LANES - 1), (KEY_BLOCK // LANES).bit_length() - 1)

    def isc_body(kb, carry):
        r0 = pl.multiple_of(kb * KEY_BLOCK, KEY_BLOCK)
        kid = sm_ref[pl.ds(r0, KEY_BLOCK), SM_KI:SM_KI + IDX_DIM].astype(BF16)
        rel = _dot(kid, qit_ref[...])
        acc = jnp.zeros((KEY_BLOCK, A_QUERIES), F32)
        for h in range(IDX_HEADS):
            acc = acc + w_t[h:h + 1, :] * jnp.maximum(rel[:, h * A_QUERIES:(h + 1) * A_QUERIES], 0.0)
        acc = jnp.where(r0 + block_iota < limit, acc, -jnp.inf)
        isc_ref[pl.ds(r0, KEY_BLOCK), :] = acc
        isb_ref[pl.ds(r0, KEY_BLOCK), :] = acc.astype(BF16)
        return carry

    lax.fori_loop(0, nkb, isc_body, 0)

    ngrp = lax.shift_right_logical(nkt + (COUNT_GROUP - 1), COUNT_GROUP.bit_length() - 1)

    def fill_body(kt, carry):
        r0 = pl.multiple_of(kt * LANES, LANES)
        isc_ref[pl.ds(r0, LANES), :] = jnp.full((LANES, A_QUERIES), -jnp.inf, F32)
        isb_ref[pl.ds(r0, LANES), :] = jnp.full((LANES, A_QUERIES), -jnp.inf, BF16)
        return carry

    lax.fori_loop(nkb * (KEY_BLOCK // LANES), ngrp * COUNT_GROUP, fill_body, 0)

    def count(pred):
        def body(g, acc):
            for t in range(COUNT_GROUP):
                r0 = pl.multiple_of((g * COUNT_GROUP + t) * LANES, LANES)
                hit = jnp.where(pred(isc_ref[pl.ds(r0, LANES), :], r0 + tile_iota), 1.0, 0.0)
                acc = acc + hit.reshape(LANES // COUNT_ROWS, COUNT_ROWS, A_QUERIES).sum(axis=0)
            return acc
        acc = lax.fori_loop(0, ngrp, body, jnp.zeros((COUNT_ROWS, A_QUERIES), F32))
        return acc.sum(axis=0, keepdims=True)

    def count_coarse(thr):
        def body(g, acc):
            for t in range(COUNT_GROUP):
                r0 = pl.multiple_of((g * COUNT_GROUP + t) * LANES, LANES)
                hit = jnp.where(isb_ref[pl.ds(r0, LANES), :] >= thr, one_b, zero_b)
                hit = hit.reshape(LANES // COUNT_ROWS, COUNT_ROWS, A_QUERIES)
                part = hit[0]
                for i in range(1, LANES // COUNT_ROWS):
                    part = part + hit[i]
                acc = acc + part
            return acc
        acc = lax.fori_loop(0, ngrp, body, jnp.zeros((COUNT_ROWS, A_QUERIES), BF16))
        return acc.astype(F32).sum(axis=0, keepdims=True)

    kf = float(topk)
    one_b = jnp.ones((), BF16)
    zero_b = jnp.zeros((), BF16)

    def thr_coarse(key):
        bits = jnp.where(key >= 0, key, key ^ jnp.int32(0x7FFF))
        return lax.bitcast_convert_type(lax.shift_left(bits, 16), F32).astype(BF16)

    c0 = count_coarse(jnp.zeros((1, A_QUERIES), BF16))
    lo16 = jnp.where(c0 >= kf, jnp.int32(0), jnp.int32(-2 ** 15))

    def coarse_body(i, lo):
        trial = lo | lax.shift_left(jnp.int32(1), 14 - i)
        c = count_coarse(thr_coarse(trial))
        return jnp.where(c >= kf, trial, lo)

    lo16 = lax.fori_loop(0, 15, coarse_body, lo16)
    lo16 = jnp.maximum(lo16, jnp.int32(NEG_INF_KEY >> 16))
    center = lax.shift_left(lo16, 16) | jnp.where(lo16 < 0, jnp.int32(0xFFFF), jnp.int32(0))

    def thr_of(key):
        bits = jnp.where(key >= 0, key, key ^ jnp.int32(0x7FFFFFFF))
        return jnp.where(key < jnp.int32(NEG_INF_KEY), -jnp.inf, lax.bitcast_convert_type(bits, F32))

    def fine_body(i, st):
        lo, c_lo, c_hi = st
        trial = lo + lax.shift_left(jnp.int32(1), FINE_STEPS - 1 - i)
        thr = thr_of(trial)
        c = count(lambda x, s: x >= thr)
        ok = c >= kf
        return jnp.where(ok, trial, lo), jnp.where(ok, c, c_lo), jnp.where(ok, c_hi, c)

    unknown = jnp.full((1, A_QUERIES), -1.0, F32)
    state = lax.fori_loop(0, FINE_CHECK, fine_body, (center - jnp.int32(1 << 16), unknown, unknown))
    lo, c_lo, c_hi = state

    hi = lo + jnp.int32(1 << (FINE_STEPS - FINE_CHECK))
    hi_thr = thr_of(hi)

    def max_below(bound):
        def body(g, acc):
            for t in range(COUNT_GROUP):
                r0 = pl.multiple_of((g * COUNT_GROUP + t) * LANES, LANES)
                x = isc_ref[pl.ds(r0, LANES), :]
                part = jnp.where(x < bound, x, -jnp.inf)
                acc = jnp.maximum(acc, part.reshape(LANES // COUNT_ROWS, COUNT_ROWS, A_QUERIES).max(axis=0))
            return acc
        acc = lax.fori_loop(0, ngrp, body, jnp.full((COUNT_ROWS, A_QUERIES), -jnp.inf, F32))
        return acc.max(axis=0, keepdims=True)

    v = max_below(hi_thr)
    n_v = count(lambda x, s: x == v)
    by_value = ((c_hi >= 0.0) & (c_hi + n_v >= kf)
                & (hi < jnp.int32(POS_INF_KEY)) & (lo >= jnp.int32(NEG_INF_KEY)))
    done = by_value | (c_lo == kf) | (limit <= topk)
    settled = jnp.min(jnp.where(done, 1.0, 0.0)) > 0.0

    def finish_now():
        return jnp.where(by_value, v, thr_of(lo)), jnp.where(by_value, c_hi + n_v, c_lo)

    def finish_search():
        lo_f, c_lo_f, _ = lax.fori_loop(FINE_CHECK, FINE_STEPS, fine_body, state)
        return thr_of(lo_f), c_lo_f

    kth, n_at = lax.cond(settled, finish_now, finish_search)

    n_ge = jnp.where(limit <= topk, 0.0, jnp.where(n_at < 0.0, jnp.inf, n_at))
    has_ties = jnp.max(n_ge) > kf

    acc_ref[...] = jnp.zeros_like(acc_ref)

    def select_plain(kb, x, s_idx, carry):
        return (s_idx < limit) & (x >= kth), carry

    def make_select_ties():
        need = kf - count(lambda x, s: x > kth)
        tri = jnp.where(lax.broadcasted_iota(I32, (KEY_BLOCK, KEY_BLOCK), 1)
                        <= lax.broadcasted_iota(I32, (KEY_BLOCK, KEY_BLOCK), 0), 1.0, 0.0).astype(BF16)

        def select_ties(kb, x, s_idx, before):
            eq = x == kth
            rank = before + _dot(tri, jnp.where(eq, 1.0, 0.0).astype(BF16))
            sel = (s_idx < limit) & ((x > kth) | (eq & (rank <= need)))
            return sel, rank[KEY_BLOCK - 1:KEY_BLOCK, :]

        return select_ties

    def att_body(select, kb, state):
        m_prev, sel_carry = state
        r0 = pl.multiple_of(kb * KEY_BLOCK, KEY_BLOCK)
        x = isc_ref[pl.ds(r0, KEY_BLOCK), :]
        s_idx = r0 + block_iota
        sel, sel_carry = select(kb, x, s_idx, sel_carry)
        bias_ref[...] = jnp.where(sel, 0.0, NEG_BIG)
        k_tile = kv_ref[pl.ds(r0, KEY_BLOCK), 0:KV_WIDTH]
        v_t = vt_ref[kb]

        units = list(range(A_HEADS // ATT_HEADS))
        uw = ATT_HEADS * A_QUERIES
        us = [slice(u * uw, (u + 1) * uw) for u in units]
        kd = [slice((u * ATT_HEADS // group) * A_HEAD_DIM, (u * ATT_HEADS // group + 1) * A_HEAD_DIM)
              for u in units]
        vr = [slice((u * ATT_HEADS // group) * V_ROWS, (u * ATT_HEADS // group + 1) * V_ROWS)
              for u in units]
        cs = [slice((u * ATT_HEADS % group) * A_QUERIES, (u * ATT_HEADS % group + ATT_HEADS) * A_QUERIES)
              for u in units]
        m_new = {}
        for w0 in range(0, len(units), ATT_WAVE):
            wave = units[w0:w0 + ATT_WAVE]
            s = {u: _dot(k_tile[:, kd[u]], qat_ref[:, us[u]])
                 + jnp.concatenate([bias_ref[...]] * ATT_HEADS, axis=1) for u in wave}
            for u in wave:
                m_new[u] = jnp.maximum(m_prev[:, us[u]], jnp.max(s[u], axis=0, keepdims=True))
            alpha = {u: jnp.exp2(m_prev[:, us[u]] - m_new[u]) for u in wave}
            p = {u: jnp.exp2(s[u] - m_new[u]).astype(BF16) for u in wave}
            pv = {u: _dot(v_t[vr[u], :], p[u]) for u in wave}
            for u in wave:
                acc_ref[vr[u], cs[u]] = alpha[u] * acc_ref[vr[u], cs[u]] + pv[u]
        return jnp.concatenate([m_new[u] for u in units], axis=1), sel_carry

    def attend(select):
        init = (jnp.full((1, A_HEADS * A_QUERIES), NEG_BIG, F32), jnp.zeros((1, A_QUERIES), F32))
        lax.fori_loop(0, nkb, functools.partial(att_body, select), init)

    lax.cond(has_ties, lambda: attend(make_select_ties()), lambda: attend(select_plain))

    for p in range(n_heads_pair):
        g = (2 * p) // group
        parts = []
        for h in (2 * p, 2 * p + 1):
            hq = slice((h % group) * A_QUERIES, (h % group + 1) * A_QUERIES)
            parts.append(acc_ref[g * V_ROWS:g * V_ROWS + A_HEAD_DIM, hq]
                         / acc_ref[g * V_ROWS + A_HEAD_DIM:g * V_ROWS + A_HEAD_DIM + 1, hq])
        o_ref[:, p * LANES:(p + 1) * LANES] = jnp.concatenate(parts, axis=0).T.astype(BF16)


def _mixer_a(qa, qi, kv, sm, bsz, seq):
    nq = seq // A_QUERIES
    assert seq % (COUNT_GROUP * LANES) == 0, seq
    assert seq // COUNT_ROWS <= 256, seq
    topk = min(TOPK_MAX, seq // 4)
    qrow = lambda b, j: (b * nq + j, 0)
    qblock = lambda b, j: (b * nq + j, 0, 0)
    brow = lambda b, j: (b, 0)
    return pl.pallas_call(
        functools.partial(_mixer_a_body, seq=seq, topk=topk),
        grid=(bsz, nq),
        in_specs=[
            pl.BlockSpec((None, A_HEAD_DIM, A_HEADS * A_QUERIES), qblock),
            pl.BlockSpec((None, IDX_DIM, IDX_HEADS * A_QUERIES), qblock),
            pl.BlockSpec((seq, 2 * KV_WIDTH), brow),
            pl.BlockSpec((seq, LANES), brow),
        ],
        out_specs=pl.BlockSpec((A_QUERIES, A_WIDTH), qrow),
        out_shape=jax.ShapeDtypeStruct((bsz * seq, A_WIDTH), BF16),
        scratch_shapes=[
            pltpu.VMEM((seq, A_QUERIES), F32),
            pltpu.VMEM((seq, A_QUERIES), BF16),
            pltpu.VMEM((seq // KEY_BLOCK, A_KV_HEADS * V_ROWS, KEY_BLOCK), BF16),
            pltpu.VMEM((A_KV_HEADS * V_ROWS, (A_HEADS // A_KV_HEADS) * A_QUERIES), F32),
            pltpu.VMEM((KEY_BLOCK, A_QUERIES), F32),
        ],
        compiler_params=pltpu.CompilerParams(
            dimension_semantics=("arbitrary", "arbitrary"), vmem_limit_bytes=VMEM_LIMIT),
        name="mixer_a",
    )(qa, qi, kv, sm)


def _gdn_body(gdn_ref, sm_ref, alog_ref, dtb_ref, ng_ref, o_ref,
              gate_ref, gct_ref, state_ref, oc_ref, lhs_ref, bm_ref, *, tc):
    tb = pl.program_id(1)
    n_chunks = tc // CHUNK
    prep_unroll = 8
    q_col, k_col, v_col, z_col = (i * B_WIDTH for i in range(4))

    @pl.when(tb == 0)
    def _():
        state_ref[...] = jnp.zeros_like(state_ref)

    sm = sm_ref[...]
    beta = jax.nn.sigmoid(sm)
    z = sm + dtb_ref[...]
    softplus = jnp.maximum(z, 0.0) + jnp.log(1.0 + jnp.exp(-jnp.abs(z)))
    g = -jnp.exp(alog_ref[...]) * softplus
    rin = lax.broadcasted_iota(I32, (tc, LANES), 0) & (CHUNK - 1)
    gc = g
    step = 1
    while step < CHUNK:
        gc = gc + jnp.where(rin >= step, pltpu.roll(gc, step, 0), 0.0)
        step *= 2
    gc3 = gc.reshape(n_chunks, CHUNK, LANES)
    g_last = jnp.broadcast_to(gc3[:, CHUNK - 1:CHUNK, :], gc3.shape).reshape(tc, LANES)
    gate_ref[0] = beta
    gate_ref[1] = gc
    gate_ref[2] = jnp.exp(gc)
    gate_ref[3] = jnp.exp(g_last - gc)
    gate_ref[4] = jnp.exp(g_last)
    for i in range(tc // LANES):
        t = gc[i * LANES:(i + 1) * LANES, :].T
        for half in range(LANES // CHUNK):
            gct_ref[i * (LANES // CHUNK) + half] = t[:, half * CHUNK:(half + 1) * CHUNK]

    ci = lax.broadcasted_iota(I32, (CHUNK, CHUNK), 0)
    si = lax.broadcasted_iota(I32, (CHUNK, CHUNK), 1)
    wl = lax.broadcasted_iota(I32, (CHUNK, 2 * CHUNK), 1)
    wr = lax.broadcasted_iota(I32, (CHUNK, 2 * CHUNK), 0)
    right = wl >= CHUNK
    eye_right = jnp.where(wl == wr + CHUNK, 1.0, 0.0)

    def prep_body(cg, carry):
        units = [(cg * prep_unroll + cc, h) for cc in range(prep_unroll) for h in range(B_HEADS)]
        rows = [pl.ds(pl.multiple_of(c * CHUNK, CHUNK), CHUNK) for c, _ in units]
        hsl = [slice(h * LANES, (h + 1) * LANES) for _, h in units]
        idx = range(len(units))

        def col(i, gate, off):
            h = units[i][1]
            return gate_ref[gate, rows[i], off + h:off + h + 1]

        def seg(i, col0):
            h = units[i][1]
            return gdn_ref[rows[i], col0 + h * LANES:col0 + (h + 1) * LANES]

        q = [seg(i, q_col) for i in idx]
        k = [seg(i, k_col) for i in idx]
        kb = [k[i] * col(i, 0, SM_BETA) for i in idx]
        kq = [_dot_nt(jnp.concatenate([kb[i], q[i]], axis=0).astype(BF16), k[i].astype(BF16))
              for i in idx]
        decay = []
        for i, (c, h) in enumerate(units):
            d = col(i, 1, SM_DECAY) - gct_ref[c][SM_DECAY + h:SM_DECAY + h + 1, :]
            decay.append(jnp.where(ci >= si, jnp.exp(jnp.where(ci >= si, d, 0.0)), 0.0))
        wmat = []
        for i, (c, h) in enumerate(units):
            n_mat = jnp.where(ci > si, -(kq[i][0:CHUNK] * decay[i]), 0.0)
            wmat.append(jnp.concatenate([n_mat, jnp.zeros_like(n_mat)], axis=1) + eye_right)
        pw = 1
        while pw < CHUNK:
            wb = [wmat[i].astype(BF16) for i in idx]
            wmat = [_dot(wb[i][:, 0:CHUNK], wb[i]) + jnp.where(right, wmat[i], 0.0) for i in idx]
            pw *= 2
        eg = [col(i, 2, SM_DECAY) for i in idx]
        rhs = [jnp.concatenate([seg(i, v_col) * col(i, 0, SM_BETA), kb[i] * eg[i]],
                               axis=1).astype(BF16) for i in idx]
        sol = [_dot(wmat[i][:, CHUNK:].astype(BF16), rhs[i]).astype(BF16) for i in idx]
        att = [(kq[i][CHUNK:] * decay[i]).astype(BF16) for i in idx]
        k_tail_t = [(k[i] * col(i, 3, SM_DECAY)).T.astype(BF16) for i in idx]
        a_uw = [_dot(att[i], sol[i]) for i in idx]
        k_uw = [_dot(k_tail_t[i], sol[i]) for i in idx]
        for i, (c, h) in enumerate(units):
            oc_ref[rows[i], hsl[i]] = a_uw[i][:, 0:B_HEAD_DIM]
            lhs_ref[c, h, 0:CHUNK, :] = (q[i] * eg[i] - a_uw[i][:, B_HEAD_DIM:]).astype(BF16)
            lhs_ref[c, h, CHUNK:, :] = k_uw[i][:, B_HEAD_DIM:].astype(BF16)
            bm_ref[c, h] = k_uw[i][:, 0:B_HEAD_DIM]
        return carry

    lax.fori_loop(0, n_chunks // prep_unroll, prep_body, 0)

    def scan_body(c, carry):
        r0 = pl.multiple_of(c * CHUNK, CHUNK)
        rows = pl.ds(r0, CHUNK)
        heads = range(B_HEADS)
        hsl = [slice(h * LANES, (h + 1) * LANES) for h in heads]
        s_prev = [state_ref[h] for h in heads]
        r = [_dot(lhs_ref[c, h], s_prev[h].astype(BF16)) for h in heads]
        for h in heads:
            gl = gate_ref[4, rows, SM_DECAY + h:SM_DECAY + h + 1][0:1, :]
            oc_ref[rows, hsl[h]] = oc_ref[rows, hsl[h]] + r[h][0:CHUNK]
            state_ref[h] = s_prev[h] * gl + bm_ref[c, h] - r[h][CHUNK:]
        return carry

    lax.fori_loop(0, n_chunks, scan_body, 0)

    for h in range(B_HEADS):
        hs = slice(h * LANES, (h + 1) * LANES)
        z_gate = gdn_ref[:, z_col + h * LANES:z_col + (h + 1) * LANES]
        o_ref[:, hs] = (_rms(oc_ref[:, hs], ng_ref[...]) * z_gate).astype(BF16)


def _gdn(gdn_in, sm, alog_row, dtb_row, norm_g, bsz, seq, tc):
    nt = seq // tc
    trow = lambda b, t: (b * nt + t, 0)
    const = lambda b, t: (0, 0)
    n_chunks = tc // CHUNK
    return pl.pallas_call(
        functools.partial(_gdn_body, tc=tc),
        grid=(bsz, nt),
        in_specs=[
            pl.BlockSpec((tc, GDN_WIDTH), trow),
            pl.BlockSpec((tc, LANES), trow),
            pl.BlockSpec((1, LANES), const),
            pl.BlockSpec((1, LANES), const),
            pl.BlockSpec((1, B_HEAD_DIM), const),
        ],
        out_specs=pl.BlockSpec((tc, B_WIDTH), trow),
        out_shape=jax.ShapeDtypeStruct((bsz * seq, B_WIDTH), BF16),
        scratch_shapes=[
            pltpu.VMEM((5, tc, LANES), F32),
            pltpu.VMEM((n_chunks, LANES, CHUNK), F32),
            pltpu.VMEM((B_HEADS, B_HEAD_DIM, B_HEAD_DIM), F32),
            pltpu.VMEM((tc, B_WIDTH), F32),
            pltpu.VMEM((n_chunks, B_HEADS, CHUNK + B_HEAD_DIM, B_HEAD_DIM), BF16),
            pltpu.VMEM((n_chunks, B_HEADS, B_HEAD_DIM, B_HEAD_DIM), F32),
        ],
        compiler_params=pltpu.CompilerParams(
            dimension_semantics=("arbitrary", "arbitrary"), vmem_limit_bytes=VMEM_LIMIT),
        name="gdn",
    )(gdn_in, sm, alog_row, dtb_row, norm_g)


def _ffn_body(x_ref, oa_ref, ob_ref, wo_ref, g2_ref, w1_ref, w2_ref, g3_ref, out_ref, *,
              final_norm):
    y = (x_ref[...] + _dot(oa_ref[...], wo_ref[0:A_WIDTH, :])
         + _dot(ob_ref[...], wo_ref[A_WIDTH:, :]))
    h = _rms(y, g2_ref[...]).astype(BF16)
    a = jnp.square(jnp.maximum(_dot(h, w1_ref[...]), 0.0)).astype(BF16)
    acc = y + _dot(a, w2_ref[...])
    out_ref[...] = _rms(acc, g3_ref[...]) if final_norm else acc


def _ffn(x2, oa, ob, wo, g2, w1, w2, g3, tm, final_norm):
    m = x2.shape[0]
    row = lambda i: (i, 0)
    const = lambda i: (0, 0)
    resident = functools.partial(pl.BlockSpec, index_map=const, pipeline_mode=pl.Buffered(1))
    return pl.pallas_call(
        functools.partial(_ffn_body, final_norm=final_norm),
        grid=(m // tm,),
        in_specs=[
            pl.BlockSpec((tm, D_MODEL), row),
            pl.BlockSpec((tm, A_WIDTH), row),
            pl.BlockSpec((tm, B_WIDTH), row),
            resident((D_MODEL, D_MODEL)),
            pl.BlockSpec((1, D_MODEL), const),
            resident((D_MODEL, D_FF)),
            resident((D_FF, D_MODEL)),
            pl.BlockSpec((1, D_MODEL), const),
        ],
        out_specs=pl.BlockSpec((tm, D_MODEL), row),
        out_shape=jax.ShapeDtypeStruct((m, D_MODEL), F32),
        compiler_params=pltpu.CompilerParams(
            dimension_semantics=("arbitrary",), vmem_limit_bytes=VMEM_LIMIT),
        name="ffn",
    )(x2, oa, ob, wo, g2, w1, w2, g3)


def _rope_tables(seq):
    half = A_HEAD_DIM // 2
    inv_freq = 1.0 / (ROPE_THETA ** (jnp.arange(half, dtype=F32) / half))
    ang = jnp.arange(seq).astype(F32)[:, None] * inv_freq[None, :]
    cos = jnp.cos(ang)
    sin = jnp.sin(ang)
    reps = LANES // A_HEAD_DIM
    return (jnp.tile(cos, (1, 2 * reps)),
            jnp.tile(jnp.concatenate([-sin, sin], axis=1), (1, reps)))


def _lane_row(vals, offset):
    return jnp.zeros((1, LANES), F32).at[0, offset:offset + vals.shape[0]].set(vals.astype(F32))


def kernel(x, norm_mix_g, w_in, conv_w, a_log, dt_bias, gdn_norm_g, w_out,
           norm_ffn_g, w_ff1, w_ff2, norm_final_g):
    bsz, seq, d = x.shape
    depth = w_in.shape[0]
    m = bsz * seq
    cos_t, sin_t = _rope_tables(seq)
    x2 = x.reshape(m, d)
    for l in range(depth):
        qa, qi, kv, gdn_in, sm = _inproj(
            x2, norm_mix_g[l][None, :], jnp.swapaxes(w_in, 1, 2), l, cos_t, sin_t, conv_w[l],
            seq, tm=512)
        o_a = _mixer_a(qa, qi, kv, sm, bsz, seq)
        o_b = _gdn(gdn_in, sm, _lane_row(a_log[l], SM_DECAY),
                   _lane_row(dt_bias[l], SM_DECAY), gdn_norm_g[l][None, :], bsz, seq, tc=512)
        x2 = _ffn(x2, o_a, o_b, w_out[l].astype(BF16), norm_ffn_g[l][None, :],
                  w_ff1[l].astype(BF16), w_ff2[l].astype(BF16), norm_final_g[None, :],
                  tm=512, final_norm=(l == depth - 1))
    return x2.reshape(bsz, seq, d)
```

```python
import functools

import jax
import jax.numpy as jnp
from jax import lax
from jax.experimental import pallas as pl
from jax.experimental.pallas import tpu as pltpu

F32 = jnp.float32
BF16 = jnp.bfloat16
I32 = jnp.int32

D_MODEL = 1024
CHUNK = 64
A_QUERIES = 256
ROPE_THETA = 10000.0
EPS = 1e-6
A_HEADS = 8
A_KV_HEADS = 2
A_HEAD_DIM = 64
IDX_HEADS = 8
IDX_DIM = 64
TOPK_MAX = 256
B_HEADS = 4
B_HEAD_DIM = 128
CONV_WIDTH = 4
D_FF = 4 * D_MODEL

LANES = 128
A_WIDTH = A_HEADS * A_HEAD_DIM
KV_WIDTH = A_KV_HEADS * A_HEAD_DIM
B_WIDTH = B_HEADS * B_HEAD_DIM
GDN_WIDTH = 4 * B_WIDTH
SM_KI = 0
SM_WI = IDX_DIM
SM_BETA = SM_WI + IDX_HEADS
SM_DECAY = SM_BETA + B_HEADS
C_QA = 0
C_QI = C_QA + A_WIDTH
C_KV = C_QI + A_WIDTH
C_GDN = C_KV + 2 * KV_WIDTH
C_SM = C_GDN + GDN_WIDTH
IN_COLS = C_SM + LANES
REF_SIZES = (A_WIDTH, KV_WIDTH, KV_WIDTH, IDX_HEADS * IDX_DIM, IDX_DIM, IDX_HEADS,
             B_WIDTH, B_WIDTH, B_WIDTH, B_WIDTH, B_HEADS, B_HEADS)
REF_OFF = tuple(sum(REF_SIZES[:i]) for i in range(len(REF_SIZES)))
IN_DIM = sum(REF_SIZES)
IN_WEIGHT_MOVES = (
    (C_QA, REF_OFF[0], A_WIDTH),
    (C_QI, REF_OFF[3], A_WIDTH),
    (C_KV, REF_OFF[1], 2 * KV_WIDTH),
    (C_GDN, REF_OFF[6], GDN_WIDTH),
    (C_SM + SM_KI, REF_OFF[4], IDX_DIM + IDX_HEADS),
    (C_SM + SM_BETA, REF_OFF[10], 2 * B_HEADS),
)
IN_PAD = IN_COLS - (C_SM + SM_DECAY + B_HEADS)

VMEM_LIMIT = 56 * 1024 * 1024
CONV_PAD = 8
COUNT_GROUP = 4
FINE_STEPS = 17
FINE_CHECK = 6
COUNT_ROWS = 32
V_ROWS = A_HEAD_DIM + 16
ATT_HEADS = 2
ATT_WAVE = 4
KEY_BLOCK = 2 * LANES
NEG_BIG = -1e30
LOG2_E = 1.4426950408889634
NEG_INF_KEY = -(2 ** 31) + 0x7FFFFF
POS_INF_KEY = 0x7F800000


def _rms(x, g):
    return x * lax.rsqrt(jnp.mean(x * x, axis=-1, keepdims=True) + EPS) * g


def _dot(a, b):
    return jnp.dot(a, b, preferred_element_type=F32)


def _dot_nt(a, b):
    return lax.dot_general(a, b, (((1,), (1,)), ((), ())), preferred_element_type=F32)


def _inproj_body(x_ref, g_ref, win_ref, cos_ref, sin_ref, cw_ref,
                 qa_ref, qi_ref, kv_ref, gdn_ref, sm_ref, xpad_ref, w_ref, *, tiles_per_seq):
    @pl.when(pl.program_id(0) == 0)
    def _():
        step = 256
        for dst, src, width in IN_WEIGHT_MOVES[:-2]:
            for r0 in range(0, width, step):
                w_ref[dst + r0:dst + r0 + step, :] = win_ref[src + r0:src + r0 + step, :].astype(BF16)
        small = [win_ref[src:src + width, :] for _, src, width in IN_WEIGHT_MOVES[-2:]]
        small.append(jnp.zeros((IN_PAD, D_MODEL), F32))
        w_ref[C_SM:IN_COLS, :] = jnp.concatenate(small, axis=0).astype(BF16)

    h = _rms(x_ref[...], g_ref[...]).astype(BF16)
    cos = cos_ref[...]
    sin = sin_ref[...]
    lane = lax.broadcasted_iota(I32, cos.shape, 1)
    first_half = (lane & (A_HEAD_DIM - 1)) < A_HEAD_DIM // 2

    def rope(t):
        swapped = jnp.where(first_half, pltpu.roll(t, LANES - A_HEAD_DIM // 2, 1),
                            pltpu.roll(t, A_HEAD_DIM // 2, 1))
        return t * cos + swapped * sin

    def proj(c0, width):
        return _dot_nt(h, w_ref[c0:c0 + width, :])

    tm = x_ref.shape[0]
    conv_cols = 3 * B_WIDTH

    @pl.when(pl.program_id(0) % tiles_per_seq == 0)
    def _():
        xpad_ref[0:CONV_PAD, :] = jnp.zeros((CONV_PAD, conv_cols), F32)

    @pl.when(pl.program_id(0) % tiles_per_seq != 0)
    def _():
        xpad_ref[0:CONV_PAD, :] = xpad_ref[tm:tm + CONV_PAD, :]

    def stage_conv_input(seg):
        xpad_ref[CONV_PAD:CONV_PAD + tm, seg * B_WIDTH:(seg + 1) * B_WIDTH] = proj(
            C_GDN + seg * B_WIDTH, B_WIDTH)

    def conv_segment(seg):
        for hh in range(B_HEADS):
            cs = slice(seg * B_WIDTH + hh * LANES, seg * B_WIDTH + (hh + 1) * LANES)
            xa = xpad_ref[:, cs]
            y = cw_ref[0:1, cs] * xa
            for jj in range(1, CONV_WIDTH):
                y = cw_ref[jj:jj + 1, cs] * xa + pltpu.roll(y, 1, 0)
            y = y[CONV_PAD:, :]
            y = y * jax.nn.sigmoid(y)
            if seg < 2:
                y = y * lax.rsqrt(jnp.sum(y * y, axis=-1, keepdims=True) + EPS)
            if seg == 0:
                y = y * (B_HEAD_DIM ** -0.5)
            gdn_ref[:, cs] = y

    def store_queries_transposed(out_ref, acc, scale):
        for j in range(A_WIDTH // LANES):
            t = rope(acc[:, j * LANES:(j + 1) * LANES])
            if scale != 1.0:
                t = t * scale
            for b in range(tm // A_QUERIES):
                tt = t[b * A_QUERIES:(b + 1) * A_QUERIES, :].T
                for hh in range(LANES // A_HEAD_DIM):
                    h0 = (j * (LANES // A_HEAD_DIM) + hh) * A_QUERIES
                    out_ref[b, :, h0:h0 + A_QUERIES] = (
                        tt[hh * A_HEAD_DIM:(hh + 1) * A_HEAD_DIM, :].astype(BF16))

    stage_conv_input(0)
    acc = proj(C_QA, A_WIDTH)
    conv_segment(0)
    store_queries_transposed(qa_ref, acc, A_HEAD_DIM ** -0.5 * LOG2_E)
    stage_conv_input(1)
    acc = proj(C_QI, A_WIDTH)
    conv_segment(1)
    store_queries_transposed(qi_ref, acc, 1.0)
    stage_conv_input(2)
    acc = proj(C_KV, 2 * KV_WIDTH)
    acc_sm = proj(C_SM, LANES)
    z = proj(C_GDN + conv_cols, B_WIDTH)
    conv_segment(2)
    kv_ref[:, 0:KV_WIDTH] = rope(acc[:, 0:KV_WIDTH]).astype(BF16)
    kv_ref[:, KV_WIDTH:] = acc[:, KV_WIDTH:].astype(BF16)
    sm_ref[...] = jnp.where(lane < IDX_DIM, rope(acc_sm), acc_sm)
    gdn_ref[:, conv_cols:] = z * jax.nn.sigmoid(z)


def _inproj(x2, g, w, layer, cos_t, sin_t, conv_w, seq, tm):
    m = x2.shape[0]
    assert seq % tm == 0 and tm % A_QUERIES == 0, (seq, tm)
    nt = seq // tm
    row = lambda i: (i, 0)
    const = lambda i: (0, 0)
    return pl.pallas_call(
        functools.partial(_inproj_body, tiles_per_seq=nt),
        grid=(m // tm,),
        in_specs=[
            pl.BlockSpec((tm, D_MODEL), row),
            pl.BlockSpec((1, D_MODEL), const),
            pl.BlockSpec((None, IN_DIM, D_MODEL), lambda i: (layer, 0, 0),
                         pipeline_mode=pl.Buffered(1)),
            pl.BlockSpec((tm, LANES), lambda i: (i % nt, 0)),
            pl.BlockSpec((tm, LANES), lambda i: (i % nt, 0)),
            pl.BlockSpec((CONV_WIDTH, 3 * B_WIDTH), const),
        ],
        out_specs=[
            pl.BlockSpec((tm // A_QUERIES, A_HEAD_DIM, A_HEADS * A_QUERIES), lambda i: (i, 0, 0)),
            pl.BlockSpec((tm // A_QUERIES, IDX_DIM, IDX_HEADS * A_QUERIES), lambda i: (i, 0, 0)),
            pl.BlockSpec((tm, 2 * KV_WIDTH), row),
            pl.BlockSpec((tm, GDN_WIDTH), row),
            pl.BlockSpec((tm, LANES), row),
        ],
        out_shape=[
            jax.ShapeDtypeStruct((m // A_QUERIES, A_HEAD_DIM, A_HEADS * A_QUERIES), BF16),
            jax.ShapeDtypeStruct((m // A_QUERIES, IDX_DIM, IDX_HEADS * A_QUERIES), BF16),
            jax.ShapeDtypeStruct((m, 2 * KV_WIDTH), BF16),
            jax.ShapeDtypeStruct((m, GDN_WIDTH), F32),
            jax.ShapeDtypeStruct((m, LANES), F32),
        ],
        scratch_shapes=[
            pltpu.VMEM((tm + CONV_PAD, 3 * B_WIDTH), F32),
            pltpu.VMEM((IN_COLS, D_MODEL), BF16),
        ],
        compiler_params=pltpu.CompilerParams(
            dimension_semantics=("arbitrary",), vmem_limit_bytes=VMEM_LIMIT),
        name="inproj",
    )(x2, g, w, cos_t, sin_t, conv_w)


def _mixer_a_body(qat_ref, qit_ref, kv_ref, sm_ref, o_ref,
                  isc_ref, isb_ref, vt_ref, acc_ref, bias_ref, *, seq, topk):
    j = pl.program_id(1)
    nkt = (j + 1) * (A_QUERIES // LANES)
    n_heads_pair = A_WIDTH // LANES
    group = A_HEADS // A_KV_HEADS

    @pl.when(j == 0)
    def _():
        per_block = KEY_BLOCK // LANES
        for t in range(seq // LANES):
            vt = kv_ref[t * LANES:(t + 1) * LANES, KV_WIDTH:].astype(F32).T.astype(BF16)
            c0 = (t % per_block) * LANES
            for g in range(A_KV_HEADS):
                vt_ref[t // per_block, g * V_ROWS:g * V_ROWS + A_HEAD_DIM, c0:c0 + LANES] = (
                    vt[g * A_HEAD_DIM:(g + 1) * A_HEAD_DIM])
        for g in range(A_KV_HEADS):
            vt_ref[:, g * V_ROWS + A_HEAD_DIM:(g + 1) * V_ROWS, :] = jnp.ones(
                (seq // KEY_BLOCK, V_ROWS - A_HEAD_DIM, KEY_BLOCK), BF16)

    q0 = pl.multiple_of(j * A_QUERIES, A_QUERIES)
    w_t = sm_ref[pl.ds(q0, A_QUERIES), :].T[SM_WI:SM_WI + IDX_HEADS, :]
    w_t = w_t * ((IDX_HEADS ** -0.5) * (IDX_DIM ** -0.5))

    qlane = lax.broadcasted_iota(I32, (1, A_QUERIES), 1)
    limit = q0 + (lax.shift_right_logical(qlane, CHUNK.bit_length() - 1) + 1) * CHUNK
    tile_iota = lax.broadcasted_iota(I32, (LANES, A_QUERIES), 0)
    block_iota = lax.broadcasted_iota(I32, (KEY_BLOCK, A_QUERIES), 0)
    nkb = lax.shift_right_logical(nkt + (KEY_BLOCK // LANES - 1), (KEY_BLOCK // LANES).bit_length() - 1)

    def isc_body(kb, carry):
        r0 = pl.multiple_of(kb * KEY_BLOCK, KEY_BLOCK)
        kid = sm_ref[pl.ds(r0, KEY_BLOCK), SM_KI:SM_KI + IDX_DIM].astype(BF16)
        rel = _dot(kid, qit_ref[...])
        acc = jnp.zeros((KEY_BLOCK, A_QUERIES), F32)
        for h in range(IDX_HEADS):
            acc = acc + w_t[h:h + 1, :] * jnp.maximum(rel[:, h * A_QUERIES:(h + 1) * A_QUERIES], 0.0)
        acc = jnp.where(r0 + block_iota < limit, acc, -jnp.inf)
        isc_ref[pl.ds(r0, KEY_BLOCK), :] = acc
        isb_ref[pl.ds(r0, KEY_BLOCK), :] = acc.astype(BF16)
        return carry

    lax.fori_loop(0, nkb, isc_body, 0)

    ngrp = lax.shift_right_logical(nkt + (COUNT_GROUP - 1), COUNT_GROUP.bit_length() - 1)

    def fill_body(kt, carry):
        r0 = pl.multiple_of(kt * LANES, LANES)
        isc_ref[pl.ds(r0, LANES), :] = jnp.full((LANES, A_QUERIES), -jnp.inf, F32)
        isb_ref[pl.ds(r0, LANES), :] = jnp.full((LANES, A_QUERIES), -jnp.inf, BF16)
        return carry

    lax.fori_loop(nkb * (KEY_BLOCK // LANES), ngrp * COUNT_GROUP, fill_body, 0)

    def count(pred):
        def body(g, acc):
            for t in range(COUNT_GROUP):
                r0 = pl.multiple_of((g * COUNT_GROUP + t) * LANES, LANES)
                hit = jnp.where(pred(isc_ref[pl.ds(r0, LANES), :], r0 + tile_iota), 1.0, 0.0)
                acc = acc + hit.reshape(LANES // COUNT_ROWS, COUNT_ROWS, A_QUERIES).sum(axis=0)
            return acc
        acc = lax.fori_loop(0, ngrp, body, jnp.zeros((COUNT_ROWS, A_QUERIES), F32))
        return acc.sum(axis=0, keepdims=True)

    def count_coarse(thr):
        def body(g, acc):
            for t in range(COUNT_GROUP):
                r0 = pl.multiple_of((g * COUNT_GROUP + t) * LANES, LANES)
                hit = jnp.where(isb_ref[pl.ds(r0, LANES), :] >= thr, one_b, zero_b)
                hit = hit.reshape(LANES // COUNT_ROWS, COUNT_ROWS, A_QUERIES)
                part = hit[0]
                for i in range(1, LANES // COUNT_ROWS):
                    part = part + hit[i]
                acc = acc + part
            return acc
        acc = lax.fori_loop(0, ngrp, body, jnp.zeros((COUNT_ROWS, A_QUERIES), BF16))
        return acc.astype(F32).sum(axis=0, keepdims=True)

    kf = float(topk)
    one_b = jnp.ones((), BF16)
    zero_b = jnp.zeros((), BF16)

    def thr_coarse(key):
        bits = jnp.where(key >= 0, key, key ^ jnp.int32(0x7FFF))
        return lax.bitcast_convert_type(lax.shift_left(bits, 16), F32).astype(BF16)

    c0 = count_coarse(jnp.zeros((1, A_QUERIES), BF16))
    lo16 = jnp.where(c0 >= kf, jnp.int32(0), jnp.int32(-2 ** 15))

    def coarse_body(i, lo):
        trial = lo | lax.shift_left(jnp.int32(1), 14 - i)
        c = count_coarse(thr_coarse(trial))
        return jnp.where(c >= kf, trial, lo)

    lo16 = lax.fori_loop(0, 15, coarse_body, lo16)
    lo16 = jnp.maximum(lo16, jnp.int32(NEG_INF_KEY >> 16))
    center = lax.shift_left(lo16, 16) | jnp.where(lo16 < 0, jnp.int32(0xFFFF), jnp.int32(0))

    def thr_of(key):
        bits = jnp.where(key >= 0, key, key ^ jnp.int32(0x7FFFFFFF))
        return jnp.where(key < jnp.int32(NEG_INF_KEY), -jnp.inf, lax.bitcast_convert_type(bits, F32))

    def fine_body(i, st):
        lo, c_lo, c_hi = st
        trial = lo + lax.shift_left(jnp.int32(1), FINE_STEPS - 1 - i)
        thr = thr_of(trial)
        c = count(lambda x, s: x >= thr)
        ok = c >= kf
        return jnp.where(ok, trial, lo), jnp.where(ok, c, c_lo), jnp.where(ok, c_hi, c)

    top = thr_of(center + jnp.int32(1 << (FINE_STEPS - 1)))
    state = lax.fori_loop(0, FINE_CHECK, fine_body,
                          (center - jnp.int32(1 << (FINE_STEPS - 1)), jnp.full((1, A_QUERIES), -1.0, F32),
                           count(lambda x, s: x >= top)))
    lo, c_lo, c_hi = state

    hi = lo + jnp.int32(1 << (FINE_STEPS - FINE_CHECK))
    hi_thr = thr_of(hi)

    def max_below(bound):
        def body(g, acc):
            for t in range(COUNT_GROUP):
                r0 = pl.multiple_of((g * COUNT_GROUP + t) * LANES, LANES)
                x = isc_ref[pl.ds(r0, LANES), :]
                part = jnp.where(x < bound, x, -jnp.inf)
                acc = jnp.maximum(acc, part.reshape(LANES // COUNT_ROWS, COUNT_ROWS, A_QUERIES).max(axis=0))
            return acc
        acc = lax.fori_loop(0, ngrp, body, jnp.full((COUNT_ROWS, A_QUERIES), -jnp.inf, F32))
        return acc.max(axis=0, keepdims=True)

    v = max_below(hi_thr)
    n_v = count(lambda x, s: x == v)
    by_value = ((c_hi >= 0.0) & (c_hi + n_v >= kf)
                & (hi < jnp.int32(POS_INF_KEY)) & (lo >= jnp.int32(NEG_INF_KEY)))
    done = by_value | (c_lo == kf) | (limit <= topk)
    settled = jnp.min(jnp.where(done, 1.0, 0.0)) > 0.0

    def finish_now():
        return jnp.where(by_value, v, thr_of(lo)), jnp.where(by_value, c_hi + n_v, c_lo)

    def finish_search():
        lo_f, c_lo_f, _ = lax.fori_loop(FINE_CHECK, FINE_STEPS, fine_body, state)
        return thr_of(lo_f), c_lo_f

    kth, n_at = lax.cond(settled, finish_now, finish_search)

    n_ge = jnp.where(limit <= topk, 0.0, jnp.where(n_at < 0.0, jnp.inf, n_at))
    has_ties = jnp.max(n_ge) > kf

    acc_ref[...] = jnp.zeros_like(acc_ref)

    def select_plain(kb, x, s_idx, carry):
        return (s_idx < limit) & (x >= kth), carry

    def make_select_ties():
        need = kf - count(lambda x, s: x > kth)
        tri = jnp.where(lax.broadcasted_iota(I32, (KEY_BLOCK, KEY_BLOCK), 1)
                        <= lax.broadcasted_iota(I32, (KEY_BLOCK, KEY_BLOCK), 0), 1.0, 0.0).astype(BF16)

        def select_ties(kb, x, s_idx, before):
            eq = x == kth
            rank = before + _dot(tri, jnp.where(eq, 1.0, 0.0).astype(BF16))
            sel = (s_idx < limit) & ((x > kth) | (eq & (rank <= need)))
            return sel, rank[KEY_BLOCK - 1:KEY_BLOCK, :]

        return select_ties

    def att_body(select, kb, state):
        m_prev, sel_carry = state
        r0 = pl.multiple_of(kb * KEY_BLOCK, KEY_BLOCK)
        x = isc_ref[pl.ds(r0, KEY_BLOCK), :]
        s_idx = r0 + block_iota
        sel, sel_carry = select(kb, x, s_idx, sel_carry)
        bias_ref[...] = jnp.where(sel, 0.0, NEG_BIG)
        k_tile = kv_ref[pl.ds(r0, KEY_BLOCK), 0:KV_WIDTH]
        v_t = vt_ref[kb]

        units = list(range(A_HEADS // ATT_HEADS))
        uw = ATT_HEADS * A_QUERIES
        us = [slice(u * uw, (u + 1) * uw) for u in units]
        kd = [slice((u * ATT_HEADS // group) * A_HEAD_DIM, (u * ATT_HEADS // group + 1) * A_HEAD_DIM)
              for u in units]
        vr = [slice((u * ATT_HEADS // group) * V_ROWS, (u * ATT_HEADS // group + 1) * V_ROWS)
              for u in units]
        cs = [slice((u * ATT_HEADS % group) * A_QUERIES, (u * ATT_HEADS % group + ATT_HEADS) * A_QUERIES)
              for u in units]
        m_new = {}
        for w0 in range(0, len(units), ATT_WAVE):
            wave = units[w0:w0 + ATT_WAVE]
            s = {u: _dot(k_tile[:, kd[u]], qat_ref[:, us[u]])
                 + jnp.concatenate([bias_ref[...]] * ATT_HEADS, axis=1) for u in wave}
            for u in wave:
                m_new[u] = jnp.maximum(m_prev[:, us[u]], jnp.max(s[u], axis=0, keepdims=True))
            alpha = {u: jnp.exp2(m_prev[:, us[u]] - m_new[u]) for u in wave}
            p = {u: jnp.exp2(s[u] - m_new[u]).astype(BF16) for u in wave}
            pv = {u: _dot(v_t[vr[u], :], p[u]) for u in wave}
            for u in wave:
                acc_ref[vr[u], cs[u]] = alpha[u] * acc_ref[vr[u], cs[u]] + pv[u]
        return jnp.concatenate([m_new[u] for u in units], axis=1), sel_carry

    def attend(select):
        init = (jnp.full((1, A_HEADS * A_QUERIES), NEG_BIG, F32), jnp.zeros((1, A_QUERIES), F32))
        lax.fori_loop(0, nkb, functools.partial(att_body, select), init)

    lax.cond(has_ties, lambda: attend(make_select_ties()), lambda: attend(select_plain))

    for p in range(n_heads_pair):
        g = (2 * p) // group
        parts = []
        for h in (2 * p, 2 * p + 1):
            hq = slice((h % group) * A_QUERIES, (h % group + 1) * A_QUERIES)
            parts.append(acc_ref[g * V_ROWS:g * V_ROWS + A_HEAD_DIM, hq]
                         / acc_ref[g * V_ROWS + A_HEAD_DIM:g * V_ROWS + A_HEAD_DIM + 1, hq])
        o_ref[:, p * LANES:(p + 1) * LANES] = jnp.concatenate(parts, axis=0).T.astype(BF16)


def _mixer_a(qa, qi, kv, sm, bsz, seq):
    nq = seq // A_QUERIES
    assert seq % (COUNT_GROUP * LANES) == 0, seq
    assert seq // COUNT_ROWS <= 256, seq
    topk = min(TOPK_MAX, seq // 4)
    qrow = lambda b, j: (b * nq + j, 0)
    qblock = lambda b, j: (b * nq + j, 0, 0)
    brow = lambda b, j: (b, 0)
    return pl.pallas_call(
        functools.partial(_mixer_a_body, seq=seq, topk=topk),
        grid=(bsz, nq),
        in_specs=[
            pl.BlockSpec((None, A_HEAD_DIM, A_HEADS * A_QUERIES), qblock),
            pl.BlockSpec((None, IDX_DIM, IDX_HEADS * A_QUERIES), qblock),
            pl.BlockSpec((seq, 2 * KV_WIDTH), brow),
            pl.BlockSpec((seq, LANES), brow),
        ],
        out_specs=pl.BlockSpec((A_QUERIES, A_WIDTH), qrow),
        out_shape=jax.ShapeDtypeStruct((bsz * seq, A_WIDTH), BF16),
        scratch_shapes=[
            pltpu.VMEM((seq, A_QUERIES), F32),
            pltpu.VMEM((seq, A_QUERIES), BF16),
            pltpu.VMEM((seq // KEY_BLOCK, A_KV_HEADS * V_ROWS, KEY_BLOCK), BF16),
            pltpu.VMEM((A_KV_HEADS * V_ROWS, (A_HEADS // A_KV_HEADS) * A_QUERIES), F32),
            pltpu.VMEM((KEY_BLOCK, A_QUERIES), F32),
        ],
        compiler_params=pltpu.CompilerParams(
            dimension_semantics=("arbitrary", "arbitrary"), vmem_limit_bytes=VMEM_LIMIT),
        name="mixer_a",
    )(qa, qi, kv, sm)


def _gdn_body(gdn_ref, sm_ref, alog_ref, dtb_ref, ng_ref, o_ref,
              gate_ref, gct_ref, state_ref, oc_ref, lhs_ref, bm_ref, *, tc):
    tb = pl.program_id(1)
    n_chunks = tc // CHUNK
    prep_unroll = 8
    q_col, k_col, v_col, z_col = (i * B_WIDTH for i in range(4))

    @pl.when(tb == 0)
    def _():
        state_ref[...] = jnp.zeros_like(state_ref)

    sm = sm_ref[...]
    beta = jax.nn.sigmoid(sm)
    z = sm + dtb_ref[...]
    softplus = jnp.maximum(z, 0.0) + jnp.log(1.0 + jnp.exp(-jnp.abs(z)))
    g = -jnp.exp(alog_ref[...]) * softplus
    rin = lax.broadcasted_iota(I32, (tc, LANES), 0) & (CHUNK - 1)
    gc = g
    step = 1
    while step < CHUNK:
        gc = gc + jnp.where(rin >= step, pltpu.roll(gc, step, 0), 0.0)
        step *= 2
    gc3 = gc.reshape(n_chunks, CHUNK, LANES)
    g_last = jnp.broadcast_to(gc3[:, CHUNK - 1:CHUNK, :], gc3.shape).reshape(tc, LANES)
    gate_ref[0] = beta
    gate_ref[1] = gc
    gate_ref[2] = jnp.exp(gc)
    gate_ref[3] = jnp.exp(g_last - gc)
    gate_ref[4] = jnp.exp(g_last)
    for i in range(tc // LANES):
        t = gc[i * LANES:(i + 1) * LANES, :].T
        for half in range(LANES // CHUNK):
            gct_ref[i * (LANES // CHUNK) + half] = t[:, half * CHUNK:(half + 1) * CHUNK]

    ci = lax.broadcasted_iota(I32, (CHUNK, CHUNK), 0)
    si = lax.broadcasted_iota(I32, (CHUNK, CHUNK), 1)
    wl = lax.broadcasted_iota(I32, (CHUNK, 2 * CHUNK), 1)
    wr = lax.broadcasted_iota(I32, (CHUNK, 2 * CHUNK), 0)
    right = wl >= CHUNK
    eye_right = jnp.where(wl == wr + CHUNK, 1.0, 0.0)

    def prep_body(cg, carry):
        units = [(cg * prep_unroll + cc, h) for cc in range(prep_unroll) for h in range(B_HEADS)]
        rows = [pl.ds(pl.multiple_of(c * CHUNK, CHUNK), CHUNK) for c, _ in units]
        hsl = [slice(h * LANES, (h + 1) * LANES) for _, h in units]
        idx = range(len(units))

        def col(i, gate, off):
            h = units[i][1]
            return gate_ref[gate, rows[i], off + h:off + h + 1]

        def seg(i, col0):
            h = units[i][1]
            return gdn_ref[rows[i], col0 + h * LANES:col0 + (h + 1) * LANES]

        q = [seg(i, q_col) for i in idx]
        k = [seg(i, k_col) for i in idx]
        kb = [k[i] * col(i, 0, SM_BETA) for i in idx]
        kq = [_dot_nt(jnp.concatenate([kb[i], q[i]], axis=0).astype(BF16), k[i].astype(BF16))
              for i in idx]
        decay = []
        for i, (c, h) in enumerate(units):
            d = col(i, 1, SM_DECAY) - gct_ref[c][SM_DECAY + h:SM_DECAY + h + 1, :]
            decay.append(jnp.where(ci >= si, jnp.exp(jnp.where(ci >= si, d, 0.0)), 0.0))
        wmat = []
        for i, (c, h) in enumerate(units):
            n_mat = jnp.where(ci > si, -(kq[i][0:CHUNK] * decay[i]), 0.0)
            wmat.append(jnp.concatenate([n_mat, jnp.zeros_like(n_mat)], axis=1) + eye_right)
        pw = 1
        while pw < CHUNK:
            wb = [wmat[i].astype(BF16) for i in idx]
            wmat = [_dot(wb[i][:, 0:CHUNK], wb[i]) + jnp.where(right, wmat[i], 0.0) for i in idx]
            pw *= 2
        eg = [col(i, 2, SM_DECAY) for i in idx]
        rhs = [jnp.concatenate([seg(i, v_col) * col(i, 0, SM_BETA), kb[i] * eg[i]],
                               axis=1).astype(BF16) for i in idx]
        sol = [_dot(wmat[i][:, CHUNK:].astype(BF16), rhs[i]).astype(BF16) for i in idx]
        att = [(kq[i][CHUNK:] * decay[i]).astype(BF16) for i in idx]
        k_tail_t = [(k[i] * col(i, 3, SM_DECAY)).T.astype(BF16) for i in idx]
        a_uw = [_dot(att[i], sol[i]) for i in idx]
        k_uw = [_dot(k_tail_t[i], sol[i]) for i in idx]
        for i, (c, h) in enumerate(units):
            oc_ref[rows[i], hsl[i]] = a_uw[i][:, 0:B_HEAD_DIM]
            lhs_ref[c, h, 0:CHUNK, :] = (q[i] * eg[i] - a_uw[i][:, B_HEAD_DIM:]).astype(BF16)
            lhs_ref[c, h, CHUNK:, :] = k_uw[i][:, B_HEAD_DIM:].astype(BF16)
            bm_ref[c, h] = k_uw[i][:, 0:B_HEAD_DIM]
        return carry

    lax.fori_loop(0, n_chunks // prep_unroll, prep_body, 0)

    def scan_body(c, carry):
        r0 = pl.multiple_of(c * CHUNK, CHUNK)
        rows = pl.ds(r0, CHUNK)
        heads = range(B_HEADS)
        hsl = [slice(h * LANES, (h + 1) * LANES) for h in heads]
        s_prev = [state_ref[h] for h in heads]
        r = [_dot(lhs_ref[c, h], s_prev[h].astype(BF16)) for h in heads]
        for h in heads:
            gl = gate_ref[4, rows, SM_DECAY + h:SM_DECAY + h + 1][0:1, :]
            oc_ref[rows, hsl[h]] = oc_ref[rows, hsl[h]] + r[h][0:CHUNK]
            state_ref[h] = s_prev[h] * gl + bm_ref[c, h] - r[h][CHUNK:]
        return carry

    lax.fori_loop(0, n_chunks, scan_body, 0)

    for h in range(B_HEADS):
        hs = slice(h * LANES, (h + 1) * LANES)
        z_gate = gdn_ref[:, z_col + h * LANES:z_col + (h + 1) * LANES]
        o_ref[:, hs] = (_rms(oc_ref[:, hs], ng_ref[...]) * z_gate).astype(BF16)


def _gdn(gdn_in, sm, alog_row, dtb_row, norm_g, bsz, seq, tc):
    nt = seq // tc
    trow = lambda b, t: (b * nt + t, 0)
    const = lambda b, t: (0, 0)
    n_chunks = tc // CHUNK
    return pl.pallas_call(
        functools.partial(_gdn_body, tc=tc),
        grid=(bsz, nt),
        in_specs=[
            pl.BlockSpec((tc, GDN_WIDTH), trow),
            pl.BlockSpec((tc, LANES), trow),
            pl.BlockSpec((1, LANES), const),
            pl.BlockSpec((1, LANES), const),
            pl.BlockSpec((1, B_HEAD_DIM), const),
        ],
        out_specs=pl.BlockSpec((tc, B_WIDTH), trow),
        out_shape=jax.ShapeDtypeStruct((bsz * seq, B_WIDTH), BF16),
        scratch_shapes=[
            pltpu.VMEM((5, tc, LANES), F32),
            pltpu.VMEM((n_chunks, LANES, CHUNK), F32),
            pltpu.VMEM((B_HEADS, B_HEAD_DIM, B_HEAD_DIM), F32),
            pltpu.VMEM((tc, B_WIDTH), F32),
            pltpu.VMEM((n_chunks, B_HEADS, CHUNK + B_HEAD_DIM, B_HEAD_DIM), BF16),
            pltpu.VMEM((n_chunks, B_HEADS, B_HEAD_DIM, B_HEAD_DIM), F32),
        ],
        compiler_params=pltpu.CompilerParams(
            dimension_semantics=("arbitrary", "arbitrary"), vmem_limit_bytes=VMEM_LIMIT),
        name="gdn",
    )(gdn_in, sm, alog_row, dtb_row, norm_g)


def _ffn_body(x_ref, oa_ref, ob_ref, wo_ref, g2_ref, w1_ref, w2_ref, g3_ref, out_ref, *,
              final_norm):
    y = (x_ref[...] + _dot(oa_ref[...], wo_ref[0:A_WIDTH, :])
         + _dot(ob_ref[...], wo_ref[A_WIDTH:, :]))
    h = _rms(y, g2_ref[...]).astype(BF16)
    a = jnp.square(jnp.maximum(_dot(h, w1_ref[...]), 0.0)).astype(BF16)
    acc = y + _dot(a, w2_ref[...])
    out_ref[...] = _rms(acc, g3_ref[...]) if final_norm else acc


def _ffn(x2, oa, ob, wo, g2, w1, w2, g3, tm, final_norm):
    m = x2.shape[0]
    row = lambda i: (i, 0)
    const = lambda i: (0, 0)
    resident = functools.partial(pl.BlockSpec, index_map=const, pipeline_mode=pl.Buffered(1))
    return pl.pallas_call(
        functools.partial(_ffn_body, final_norm=final_norm),
        grid=(m // tm,),
        in_specs=[
            pl.BlockSpec((tm, D_MODEL), row),
            pl.BlockSpec((tm, A_WIDTH), row),
            pl.BlockSpec((tm, B_WIDTH), row),
            resident((D_MODEL, D_MODEL)),
            pl.BlockSpec((1, D_MODEL), const),
            resident((D_MODEL, D_FF)),
            resident((D_FF, D_MODEL)),
            pl.BlockSpec((1, D_MODEL), const),
        ],
        out_specs=pl.BlockSpec((tm, D_MODEL), row),
        out_shape=jax.ShapeDtypeStruct((m, D_MODEL), F32),
        compiler_params=pltpu.CompilerParams(
            dimension_semantics=("arbitrary",), vmem_limit_bytes=VMEM_LIMIT),
        name="ffn",
    )(x2, oa, ob, wo, g2, w1, w2, g3)


def _rope_tables(seq):
    half = A_HEAD_DIM // 2
    inv_freq = 1.0 / (ROPE_THETA ** (jnp.arange(half, dtype=F32) / half))
    ang = jnp.arange(seq).astype(F32)[:, None] * inv_freq[None, :]
    cos = jnp.cos(ang)
    sin = jnp.sin(ang)
    reps = LANES // A_HEAD_DIM
    return (jnp.tile(cos, (1, 2 * reps)),
            jnp.tile(jnp.concatenate([-sin, sin], axis=1), (1, reps)))


def _lane_row(vals, offset):
    return jnp.zeros((1, LANES), F32).at[0, offset:offset + vals.shape[0]].set(vals.astype(F32))


def kernel(x, norm_mix_g, w_in, conv_w, a_log, dt_bias, gdn_norm_g, w_out,
           norm_ffn_g, w_ff1, w_ff2, norm_final_g):
    bsz, seq, d = x.shape
    depth = w_in.shape[0]
    m = bsz * seq
    cos_t, sin_t = _rope_tables(seq)
    x2 = x.reshape(m, d)
    for l in range(depth):
        qa, qi, kv, gdn_in, sm = _inproj(
            x2, norm_mix_g[l][None, :], jnp.swapaxes(w_in, 1, 2), l, cos_t, sin_t, conv_w[l],
            seq, tm=512)
        o_a = _mixer_a(qa, qi, kv, sm, bsz, seq)
        o_b = _gdn(gdn_in, sm, _lane_row(a_log[l], SM_DECAY),
                   _lane_row(dt_bias[l], SM_DECAY), gdn_norm_g[l][None, :], bsz, seq, tc=512)
        x2 = _ffn(x2, o_a, o_b, w_out[l].astype(BF16), norm_ffn_g[l][None, :],
                  w_ff1[l].astype(BF16), w_ff2[l].astype(BF16), norm_final_g[None, :],
                  tm=512, final_norm=(l == depth - 1))
    return x2.reshape(bsz, seq, d)
```

```python
import functools

import jax
import jax.numpy as jnp
from jax import lax
from jax.experimental import pallas as pl
from jax.experimental.pallas import tpu as pltpu

F32 = jnp.float32
BF16 = jnp.bfloat16
I32 = jnp.int32

D_MODEL = 1024
CHUNK = 64
A_QUERIES = 256
ROPE_THETA = 10000.0
EPS = 1e-6
A_HEADS = 8
A_KV_HEADS = 2
A_HEAD_DIM = 64
IDX_HEADS = 8
IDX_DIM = 64
TOPK_MAX = 256
B_HEADS = 4
B_HEAD_DIM = 128
CONV_WIDTH = 4
D_FF = 4 * D_MODEL

LANES = 128
A_WIDTH = A_HEADS * A_HEAD_DIM
KV_WIDTH = A_KV_HEADS * A_HEAD_DIM
B_WIDTH = B_HEADS * B_HEAD_DIM
GDN_WIDTH = 4 * B_WIDTH
SM_KI = 0
SM_WI = IDX_DIM
SM_BETA = SM_WI + IDX_HEADS
SM_DECAY = SM_BETA + B_HEADS
C_QA = 0
C_QI = C_QA + A_WIDTH
C_KV = C_QI + A_WIDTH
C_GDN = C_KV + 2 * KV_WIDTH
C_SM = C_GDN + GDN_WIDTH
IN_COLS = C_SM + LANES
REF_SIZES = (A_WIDTH, KV_WIDTH, KV_WIDTH, IDX_HEADS * IDX_DIM, IDX_DIM, IDX_HEADS,
             B_WIDTH, B_WIDTH, B_WIDTH, B_WIDTH, B_HEADS, B_HEADS)
REF_OFF = tuple(sum(REF_SIZES[:i]) for i in range(len(REF_SIZES)))
IN_DIM = sum(REF_SIZES)
IN_WEIGHT_MOVES = (
    (C_QA, REF_OFF[0], A_WIDTH),
    (C_QI, REF_OFF[3], A_WIDTH),
    (C_KV, REF_OFF[1], 2 * KV_WIDTH),
    (C_GDN, REF_OFF[6], GDN_WIDTH),
    (C_SM + SM_KI, REF_OFF[4], IDX_DIM + IDX_HEADS),
    (C_SM + SM_BETA, REF_OFF[10], 2 * B_HEADS),
)
IN_PAD = IN_COLS - (C_SM + SM_DECAY + B_HEADS)

VMEM_LIMIT = 56 * 1024 * 1024
CONV_PAD = 8
COUNT_GROUP = 4
FINE_STEPS = 17
FINE_CHECK = 6
COUNT_ROWS = 32
V_ROWS = A_HEAD_DIM + 16
ATT_HEADS = 2
ATT_WAVE = 4
KEY_BLOCK = 2 * LANES
NEG_BIG = -1e30
LOG2_E = 1.4426950408889634
NEG_INF_KEY = -(2 ** 31) + 0x7FFFFF
POS_INF_KEY = 0x7F800000


def _rms(x, g):
    return x * lax.rsqrt(jnp.mean(x * x, axis=-1, keepdims=True) + EPS) * g


def _dot(a, b):
    return jnp.dot(a, b, preferred_element_type=F32)


def _dot_nt(a, b):
    return lax.dot_general(a, b, (((1,), (1,)), ((), ())), preferred_element_type=F32)


def _inproj_body(x_ref, g_ref, win_ref, cos_ref, sin_ref, cw_ref,
                 qa_ref, qi_ref, kv_ref, gdn_ref, sm_ref, xpad_ref, w_ref, *, tiles_per_seq):
    @pl.when(pl.program_id(0) == 0)
    def _():
        step = 256
        for dst, src, width in IN_WEIGHT_MOVES[:-2]:
            for r0 in range(0, width, step):
                w_ref[dst + r0:dst + r0 + step, :] = win_ref[src + r0:src + r0 + step, :].astype(BF16)
        small = [win_ref[src:src + width, :] for _, src, width in IN_WEIGHT_MOVES[-2:]]
        small.append(jnp.zeros((IN_PAD, D_MODEL), F32))
        w_ref[C_SM:IN_COLS, :] = jnp.concatenate(small, axis=0).astype(BF16)

    h = _rms(x_ref[...], g_ref[...]).astype(BF16)
    cos = cos_ref[...]
    sin = sin_ref[...]
    lane = lax.broadcasted_iota(I32, cos.shape, 1)
    first_half = (lane & (A_HEAD_DIM - 1)) < A_HEAD_DIM // 2

    def rope(t):
        swapped = jnp.where(first_half, pltpu.roll(t, LANES - A_HEAD_DIM // 2, 1),
                            pltpu.roll(t, A_HEAD_DIM // 2, 1))
        return t * cos + swapped * sin

    def proj(c0, width):
        return _dot_nt(h, w_ref[c0:c0 + width, :])

    tm = x_ref.shape[0]
    conv_cols = 3 * B_WIDTH

    @pl.when(pl.program_id(0) % tiles_per_seq == 0)
    def _():
        xpad_ref[0:CONV_PAD, :] = jnp.zeros((CONV_PAD, conv_cols), F32)

    @pl.when(pl.program_id(0) % tiles_per_seq != 0)
    def _():
        xpad_ref[0:CONV_PAD, :] = xpad_ref[tm:tm + CONV_PAD, :]

    def stage_conv_input(seg):
        xpad_ref[CONV_PAD:CONV_PAD + tm, seg * B_WIDTH:(seg + 1) * B_WIDTH] = proj(
            C_GDN + seg * B_WIDTH, B_WIDTH)

    def conv_segment(seg):
        for hh in range(B_HEADS):
            cs = slice(seg * B_WIDTH + hh * LANES, seg * B_WIDTH + (hh + 1) * LANES)
            xa = xpad_ref[:, cs]
            y = cw_ref[0:1, cs] * xa
            for jj in range(1, CONV_WIDTH):
                y = cw_ref[jj:jj + 1, cs] * xa + pltpu.roll(y, 1, 0)
            y = y[CONV_PAD:, :]
            y = y * jax.nn.sigmoid(y)
            if seg < 2:
                y = y * lax.rsqrt(jnp.sum(y * y, axis=-1, keepdims=True) + EPS)
            if seg == 0:
                y = y * (B_HEAD_DIM ** -0.5)
            gdn_ref[:, cs] = y

    def store_queries_transposed(out_ref, acc, scale):
        for j in range(A_WIDTH // LANES):
            t = rope(acc[:, j * LANES:(j + 1) * LANES])
            if scale != 1.0:
                t = t * scale
            for b in range(tm // A_QUERIES):
                tt = t[b * A_QUERIES:(b + 1) * A_QUERIES, :].T
                for hh in range(LANES // A_HEAD_DIM):
                    h0 = (j * (LANES // A_HEAD_DIM) + hh) * A_QUERIES
                    out_ref[b, :, h0:h0 + A_QUERIES] = (
                        tt[hh * A_HEAD_DIM:(hh + 1) * A_HEAD_DIM, :].astype(BF16))

    stage_conv_input(0)
    acc = proj(C_QA, A_WIDTH)
    conv_segment(0)
    store_queries_transposed(qa_ref, acc, A_HEAD_DIM ** -0.5 * LOG2_E)
    stage_conv_input(1)
    acc = proj(C_QI, A_WIDTH)
    conv_segment(1)
    store_queries_transposed(qi_ref, acc, 1.0)
    stage_conv_input(2)
    acc = proj(C_KV, 2 * KV_WIDTH)
    acc_sm = proj(C_SM, LANES)
    z = proj(C_GDN + conv_cols, B_WIDTH)
    conv_segment(2)
    kv_ref[:, 0:KV_WIDTH] = rope(acc[:, 0:KV_WIDTH]).astype(BF16)
    kv_ref[:, KV_WIDTH:] = acc[:, KV_WIDTH:].astype(BF16)
    sm_ref[...] = jnp.where(lane < IDX_DIM, rope(acc_sm), acc_sm)
    gdn_ref[:, conv_cols:] = z * jax.nn.sigmoid(z)


def _inproj(x2, g, w, layer, cos_t, sin_t, conv_w, seq, tm):
    m = x2.shape[0]
    assert seq % tm == 0 and tm % A_QUERIES == 0, (seq, tm)
    nt = seq // tm
    row = lambda i: (i, 0)
    const = lambda i: (0, 0)
    return pl.pallas_call(
        functools.partial(_inproj_body, tiles_per_seq=nt),
        grid=(m // tm,),
        in_specs=[
            pl.BlockSpec((tm, D_MODEL), row),
            pl.BlockSpec((1, D_MODEL), const),
            pl.BlockSpec((None, IN_DIM, D_MODEL), lambda i: (layer, 0, 0),
                         pipeline_mode=pl.Buffered(1)),
            pl.BlockSpec((tm, LANES), lambda i: (i % nt, 0)),
            pl.BlockSpec((tm, LANES), lambda i: (i % nt, 0)),
            pl.BlockSpec((CONV_WIDTH, 3 * B_WIDTH), const),
        ],
        out_specs=[
            pl.BlockSpec((tm // A_QUERIES, A_HEAD_DIM, A_HEADS * A_QUERIES), lambda i: (i, 0, 0)),
            pl.BlockSpec((tm // A_QUERIES, IDX_DIM, IDX_HEADS * A_QUERIES), lambda i: (i, 0, 0)),
            pl.BlockSpec((tm, 2 * KV_WIDTH), row),
            pl.BlockSpec((tm, GDN_WIDTH), row),
            pl.BlockSpec((tm, LANES), row),
        ],
        out_shape=[
            jax.ShapeDtypeStruct((m // A_QUERIES, A_HEAD_DIM, A_HEADS * A_QUERIES), BF16),
            jax.ShapeDtypeStruct((m // A_QUERIES, IDX_DIM, IDX_HEADS * A_QUERIES), BF16),
            jax.ShapeDtypeStruct((m, 2 * KV_WIDTH), BF16),
            jax.ShapeDtypeStruct((m, GDN_WIDTH), F32),
            jax.ShapeDtypeStruct((m, LANES), F32),
        ],
        scratch_shapes=[
            pltpu.VMEM((tm + CONV_PAD, 3 * B_WIDTH), F32),
            pltpu.VMEM((IN_COLS, D_MODEL), BF16),
        ],
        compiler_params=pltpu.CompilerParams(
            dimension_semantics=("arbitrary",), vmem_limit_bytes=VMEM_LIMIT),
        name="inproj",
    )(x2, g, w, cos_t, sin_t, conv_w)


def _mixer_a_body(qat_ref, qit_ref, kv_ref, sm_ref, o_ref,
                  isc_ref, isb_ref, vt_ref, acc_ref, bias_ref, *, seq, topk):
    j = pl.program_id(1)
    nkt = (j + 1) * (A_QUERIES // LANES)
    n_heads_pair = A_WIDTH // LANES
    group = A_HEADS // A_KV_HEADS

    @pl.when(j == 0)
    def _():
        per_block = KEY_BLOCK // LANES
        for t in range(seq // LANES):
            vt = kv_ref[t * LANES:(t + 1) * LANES, KV_WIDTH:].astype(F32).T.astype(BF16)
            c0 = (t % per_block) * LANES
            for g in range(A_KV_HEADS):
                vt_ref[t // per_block, g * V_ROWS:g * V_ROWS + A_HEAD_DIM, c0:c0 + LANES] = (
                    vt[g * A_HEAD_DIM:(g + 1) * A_HEAD_DIM])
        for g in range(A_KV_HEADS):
            vt_ref[:, g * V_ROWS + A_HEAD_DIM:(g + 1) * V_ROWS, :] = jnp.ones(
                (seq // KEY_BLOCK, V_ROWS - A_HEAD_DIM, KEY_BLOCK), BF16)

    q0 = pl.multiple_of(j * A_QUERIES, A_QUERIES)
    w_t = sm_ref[pl.ds(q0, A_QUERIES), :].T[SM_WI:SM_WI + IDX_HEADS, :]
    w_t = w_t * ((IDX_HEADS ** -0.5) * (IDX_DIM ** -0.5))

    qlane = lax.broadcasted_iota(I32, (1, A_QUERIES), 1)
    limit = q0 + (lax.shift_right_logical(qlane, CHUNK.bit_length() - 1) + 1) * CHUNK
    tile_iota = lax.broadcasted_iota(I32, (LANES, A_QUERIES), 0)
    block_iota = lax.broadcasted_iota(I32, (KEY_BLOCK, A_QUERIES), 0)
    nkb = lax.shift_right_logical(nkt + (KEY_BLOCK // LANES - 1), (KEY_BLOCK // LANES).bit_length() - 1)

    def isc_body(kb, carry):
        r0 = pl.multiple_of(kb * KEY_BLOCK, KEY_BLOCK)
        kid = sm_ref[pl.ds(r0, KEY_BLOCK), SM_KI:SM_KI + IDX_DIM].astype(BF16)
        rel = _dot(kid, qit_ref[...])
        acc = jnp.zeros((KEY_BLOCK, A_QUERIES), F32)
        for h in range(IDX_HEADS):
            acc = acc + w_t[h:h + 1, :] * jnp.maximum(rel[:, h * A_QUERIES:(h + 1) * A_QUERIES], 0.0)
        acc = jnp.where(r0 + block_iota < limit, acc, -jnp.inf)
        isc_ref[pl.ds(r0, KEY_BLOCK), :] = acc
        isb_ref[pl.ds(r0, KEY_BLOCK), :] = acc.astype(BF16)
        return carry

    lax.fori_loop(0, nkb, isc_body, 0)

    ngrp = lax.shift_right_logical(nkt + (COUNT_GROUP - 1), COUNT_GROUP.bit_length() - 1)

    def fill_body(kt, carry):
        r0 = pl.multiple_of(kt * LANES, LANES)
        isc_ref[pl.ds(r0, LANES), :] = jnp.full((LANES, A_QUERIES), -jnp.inf, F32)
        isb_ref[pl.ds(r0, LANES), :] = jnp.full((LANES, A_QUERIES), -jnp.inf, BF16)
        return carry

    lax.fori_loop(nkb * (KEY_BLOCK // LANES), ngrp * COUNT_GROUP, fill_body, 0)

    def count(pred):
        def body(g, acc):
            for t in range(COUNT_GROUP):
                r0 = pl.multiple_of((g * COUNT_GROUP + t) * LANES, LANES)
                hit = jnp.where(pred(isc_ref[pl.ds(r0, LANES), :], r0 + tile_iota), 1.0, 0.0)
                acc = acc + hit.reshape(LANES // COUNT_ROWS, COUNT_ROWS, A_QUERIES).sum(axis=0)
            return acc
        acc = lax.fori_loop(0, ngrp, body, jnp.zeros((COUNT_ROWS, A_QUERIES), F32))
        return acc.sum(axis=0, keepdims=True)

    def count_coarse(thr):
        def body(g, acc):
            for t in range(COUNT_GROUP):
                r0 = pl.multiple_of((g * COUNT_GROUP + t) * LANES, LANES)
                hit = jnp.where(isb_ref[pl.ds(r0, LANES), :] >= thr, one_b, zero_b)
                hit = hit.reshape(LANES // COUNT_ROWS, COUNT_ROWS, A_QUERIES)
                part = hit[0]
                for i in range(1, LANES // COUNT_ROWS):
                    part = part + hit[i]
                acc = acc + part
            return acc
        acc = lax.fori_loop(0, ngrp, body, jnp.zeros((COUNT_ROWS, A_QUERIES), BF16))
        return acc.astype(F32).sum(axis=0, keepdims=True)

    kf = float(topk)
    one_b = jnp.ones((), BF16)
    zero_b = jnp.zeros((), BF16)

    def thr_coarse(key):
        bits = jnp.where(key >= 0, key, key ^ jnp.int32(0x7FFF))
        return lax.bitcast_convert_type(lax.shift_left(bits, 16), F32).astype(BF16)

    c0 = count_coarse(jnp.zeros((1, A_QUERIES), BF16))
    lo16 = jnp.where(c0 >= kf, jnp.int32(0), jnp.int32(-2 ** 15))

    def coarse_body(i, lo):
        trial = lo | lax.shift_left(jnp.int32(1), 14 - i)
        c = count_coarse(thr_coarse(trial))
        return jnp.where(c >= kf, trial, lo)

    lo16 = lax.fori_loop(0, 15, coarse_body, lo16)
    lo16 = jnp.maximum(lo16, jnp.int32(NEG_INF_KEY >> 16))
    center = lax.shift_left(lo16, 16) | jnp.where(lo16 < 0, jnp.int32(0xFFFF), jnp.int32(0))

    def thr_of(key):
        bits = jnp.where(key >= 0, key, key ^ jnp.int32(0x7FFFFFFF))
        return jnp.where(key < jnp.int32(NEG_INF_KEY), -jnp.inf, lax.bitcast_convert_type(bits, F32))

    def fine_body(i, st):
        lo, c_lo, c_hi = st
        trial = lo + lax.shift_left(jnp.int32(1), FINE_STEPS - 1 - i)
        thr = thr_of(trial)
        c = count(lambda x, s: x >= thr)
        ok = c >= kf
        return jnp.where(ok, trial, lo), jnp.where(ok, c, c_lo), jnp.where(ok, c_hi, c)

    top = thr_of(center + jnp.int32(1 << (FINE_STEPS - 1)))
    state = lax.fori_loop(0, FINE_CHECK, fine_body,
                          (center - jnp.int32(1 << (FINE_STEPS - 1)), jnp.full((1, A_QUERIES), -1.0, F32),
                           count(lambda x, s: x >= top)))
    lo, c_lo, c_hi = state

    hi = lo + jnp.int32(1 << (FINE_STEPS - FINE_CHECK))
    hi_thr = thr_of(hi)

    def max_below(bound):
        def body(g, acc):
            for t in range(COUNT_GROUP):
                r0 = pl.multiple_of((g * COUNT_GROUP + t) * LANES, LANES)
                x = isc_ref[pl.ds(r0, LANES), :]
                part = jnp.where(x < bound, x, -jnp.inf)
                acc = jnp.maximum(acc, part.reshape(LANES // COUNT_ROWS, COUNT_ROWS, A_QUERIES).max(axis=0))
            return acc
        acc = lax.fori_loop(0, ngrp, body, jnp.full((COUNT_ROWS, A_QUERIES), -jnp.inf, F32))
        return acc.max(axis=0, keepdims=True)

    v = max_below(hi_thr)
    n_v = count(lambda x, s: x == v)
    by_value = ((c_hi >= 0.0) & (c_hi + n_v >= kf)
                & (hi < jnp.int32(POS_INF_KEY)) & (lo >= jnp.int32(NEG_INF_KEY)))
    done = by_value | (c_lo == kf) | (limit <= topk)
    settled = jnp.min(jnp.where(done, 1.0, 0.0)) > 0.0

    def finish_now():
        return (jnp.where(by_value, v, thr_of(lo)), jnp.where(by_value, c_hi + n_v, c_lo),
                jnp.where(by_value, c_hi, -1.0))

    def finish_search():
        lo_f, c_lo_f, _ = lax.fori_loop(FINE_CHECK, FINE_STEPS, fine_body, state)
        kth_f = thr_of(lo_f)
        return kth_f, c_lo_f, count(lambda x, s: x > kth_f)

    kth, n_at, n_above = lax.cond(settled, finish_now, finish_search)

    n_ge = jnp.where(limit <= topk, 0.0, jnp.where(n_at < 0.0, jnp.inf, n_at))
    has_ties = jnp.max(n_ge) > kf

    acc_ref[...] = jnp.zeros_like(acc_ref)

    def select_plain(kb, x, s_idx, carry):
        return (s_idx < limit) & (x >= kth), carry

    def make_select_ties():
        need = jnp.where(n_above < 0.0, kf, kf - n_above)
        tri = jnp.where(lax.broadcasted_iota(I32, (KEY_BLOCK, KEY_BLOCK), 1)
                        <= lax.broadcasted_iota(I32, (KEY_BLOCK, KEY_BLOCK), 0), 1.0, 0.0).astype(BF16)

        def select_ties(kb, x, s_idx, before):
            eq = x == kth
            rank = before + _dot(tri, jnp.where(eq, 1.0, 0.0).astype(BF16))
            sel = (s_idx < limit) & ((x > kth) | (eq & (rank <= need)))
            return sel, rank[KEY_BLOCK - 1:KEY_BLOCK, :]

        return select_ties

    def att_body(select, kb, state):
        m_prev, sel_carry = state
        r0 = pl.multiple_of(kb * KEY_BLOCK, KEY_BLOCK)
        x = isc_ref[pl.ds(r0, KEY_BLOCK), :]
        s_idx = r0 + block_iota
        sel, sel_carry = select(kb, x, s_idx, sel_carry)
        bias_ref[...] = jnp.where(sel, 0.0, NEG_BIG)
        k_tile = kv_ref[pl.ds(r0, KEY_BLOCK), 0:KV_WIDTH]
        v_t = vt_ref[kb]

        units = list(range(A_HEADS // ATT_HEADS))
        uw = ATT_HEADS * A_QUERIES
        us = [slice(u * uw, (u + 1) * uw) for u in units]
        kd = [slice((u * ATT_HEADS // group) * A_HEAD_DIM, (u * ATT_HEADS // group + 1) * A_HEAD_DIM)
              for u in units]
        vr = [slice((u * ATT_HEADS // group) * V_ROWS, (u * ATT_HEADS // group + 1) * V_ROWS)
              for u in units]
        cs = [slice((u * ATT_HEADS % group) * A_QUERIES, (u * ATT_HEADS % group + ATT_HEADS) * A_QUERIES)
              for u in units]
        m_new = {}
        for w0 in range(0, len(units), ATT_WAVE):
            wave = units[w0:w0 + ATT_WAVE]
            s = {u: _dot(k_tile[:, kd[u]], qat_ref[:, us[u]])
                 + jnp.concatenate([bias_ref[...]] * ATT_HEADS, axis=1) for u in wave}
            for u in wave:
                m_new[u] = jnp.maximum(m_prev[:, us[u]], jnp.max(s[u], axis=0, keepdims=True))
            alpha = {u: jnp.exp2(m_prev[:, us[u]] - m_new[u]) for u in wave}
            p = {u: jnp.exp2(s[u] - m_new[u]).astype(BF16) for u in wave}
            pv = {u: _dot(v_t[vr[u], :], p[u]) for u in wave}
            for u in wave:
                acc_ref[vr[u], cs[u]] = alpha[u] * acc_ref[vr[u], cs[u]] + pv[u]
        return jnp.concatenate([m_new[u] for u in units], axis=1), sel_carry

    def attend(select):
        init = (jnp.full((1, A_HEADS * A_QUERIES), NEG_BIG, F32), jnp.zeros((1, A_QUERIES), F32))
        lax.fori_loop(0, nkb, functools.partial(att_body, select), init)

    lax.cond(has_ties, lambda: attend(make_select_ties()), lambda: attend(select_plain))

    for p in range(n_heads_pair):
        g = (2 * p) // group
        parts = []
        for h in (2 * p, 2 * p + 1):
            hq = slice((h % group) * A_QUERIES, (h % group + 1) * A_QUERIES)
            parts.append(acc_ref[g * V_ROWS:g * V_ROWS + A_HEAD_DIM, hq]
                         / acc_ref[g * V_ROWS + A_HEAD_DIM:g * V_ROWS + A_HEAD_DIM + 1, hq])
        o_ref[:, p * LANES:(p + 1) * LANES] = jnp.concatenate(parts, axis=0).T.astype(BF16)


def _mixer_a(qa, qi, kv, sm, bsz, seq):
    nq = seq // A_QUERIES
    assert seq % (COUNT_GROUP * LANES) == 0, seq
    assert seq // COUNT_ROWS <= 256, seq
    topk = min(TOPK_MAX, seq // 4)
    qrow = lambda b, j: (b * nq + j, 0)
    qblock = lambda b, j: (b * nq + j, 0, 0)
    brow = lambda b, j: (b, 0)
    return pl.pallas_call(
        functools.partial(_mixer_a_body, seq=seq, topk=topk),
        grid=(bsz, nq),
        in_specs=[
            pl.BlockSpec((None, A_HEAD_DIM, A_HEADS * A_QUERIES), qblock),
            pl.BlockSpec((None, IDX_DIM, IDX_HEADS * A_QUERIES), qblock),
            pl.BlockSpec((seq, 2 * KV_WIDTH), brow),
            pl.BlockSpec((seq, LANES), brow),
        ],
        out_specs=pl.BlockSpec((A_QUERIES, A_WIDTH), qrow),
        out_shape=jax.ShapeDtypeStruct((bsz * seq, A_WIDTH), BF16),
        scratch_shapes=[
            pltpu.VMEM((seq, A_QUERIES), F32),
            pltpu.VMEM((seq, A_QUERIES), BF16),
            pltpu.VMEM((seq // KEY_BLOCK, A_KV_HEADS * V_ROWS, KEY_BLOCK), BF16),
            pltpu.VMEM((A_KV_HEADS * V_ROWS, (A_HEADS // A_KV_HEADS) * A_QUERIES), F32),
            pltpu.VMEM((KEY_BLOCK, A_QUERIES), F32),
        ],
        compiler_params=pltpu.CompilerParams(
            dimension_semantics=("arbitrary", "arbitrary"), vmem_limit_bytes=VMEM_LIMIT),
        name="mixer_a",
    )(qa, qi, kv, sm)


def _gdn_body(gdn_ref, sm_ref, alog_ref, dtb_ref, ng_ref, o_ref,
              gate_ref, gct_ref, state_ref, oc_ref, lhs_ref, bm_ref, *, tc):
    tb = pl.program_id(1)
    n_chunks = tc // CHUNK
    prep_unroll = 8
    q_col, k_col, v_col, z_col = (i * B_WIDTH for i in range(4))

    @pl.when(tb == 0)
    def _():
        state_ref[...] = jnp.zeros_like(state_ref)

    sm = sm_ref[...]
    beta = jax.nn.sigmoid(sm)
    z = sm + dtb_ref[...]
    softplus = jnp.maximum(z, 0.0) + jnp.log(1.0 + jnp.exp(-jnp.abs(z)))
    g = -jnp.exp(alog_ref[...]) * softplus
    rin = lax.broadcasted_iota(I32, (tc, LANES), 0) & (CHUNK - 1)
    gc = g
    step = 1
    while step < CHUNK:
        gc = gc + jnp.where(rin >= step, pltpu.roll(gc, step, 0), 0.0)
        step *= 2
    gc3 = gc.reshape(n_chunks, CHUNK, LANES)
    g_last = jnp.broadcast_to(gc3[:, CHUNK - 1:CHUNK, :], gc3.shape).reshape(tc, LANES)
    gate_ref[0] = beta
    gate_ref[1] = gc
    gate_ref[2] = jnp.exp(gc)
    gate_ref[3] = jnp.exp(g_last - gc)
    gate_ref[4] = jnp.exp(g_last)
    for i in range(tc // LANES):
        t = gc[i * LANES:(i + 1) * LANES, :].T
        for half in range(LANES // CHUNK):
            gct_ref[i * (LANES // CHUNK) + half] = t[:, half * CHUNK:(half + 1) * CHUNK]

    ci = lax.broadcasted_iota(I32, (CHUNK, CHUNK), 0)
    si = lax.broadcasted_iota(I32, (CHUNK, CHUNK), 1)
    wl = lax.broadcasted_iota(I32, (CHUNK, 2 * CHUNK), 1)
    wr = lax.broadcasted_iota(I32, (CHUNK, 2 * CHUNK), 0)
    right = wl >= CHUNK
    eye_right = jnp.where(wl == wr + CHUNK, 1.0, 0.0)

    def prep_body(cg, carry):
        units = [(cg * prep_unroll + cc, h) for cc in range(prep_unroll) for h in range(B_HEADS)]
        rows = [pl.ds(pl.multiple_of(c * CHUNK, CHUNK), CHUNK) for c, _ in units]
        hsl = [slice(h * LANES, (h + 1) * LANES) for _, h in units]
        idx = range(len(units))

        def col(i, gate, off):
            h = units[i][1]
            return gate_ref[gate, rows[i], off + h:off + h + 1]

        def seg(i, col0):
            h = units[i][1]
            return gdn_ref[rows[i], col0 + h * LANES:col0 + (h + 1) * LANES]

        q = [seg(i, q_col) for i in idx]
        k = [seg(i, k_col) for i in idx]
        kb = [k[i] * col(i, 0, SM_BETA) for i in idx]
        kq = [_dot_nt(jnp.concatenate([kb[i], q[i]], axis=0).astype(BF16), k[i].astype(BF16))
              for i in idx]
        decay = []
        for i, (c, h) in enumerate(units):
            d = col(i, 1, SM_DECAY) - gct_ref[c][SM_DECAY + h:SM_DECAY + h + 1, :]
            decay.append(jnp.where(ci >= si, jnp.exp(jnp.where(ci >= si, d, 0.0)), 0.0))
        wmat = []
        for i, (c, h) in enumerate(units):
            n_mat = jnp.where(ci > si, -(kq[i][0:CHUNK] * decay[i]), 0.0)
            wmat.append(jnp.concatenate([n_mat, jnp.zeros_like(n_mat)], axis=1) + eye_right)
        pw = 1
        while pw < CHUNK:
            wb = [wmat[i].astype(BF16) for i in idx]
            wmat = [_dot(wb[i][:, 0:CHUNK], wb[i]) + jnp.where(right, wmat[i], 0.0) for i in idx]
            pw *= 2
        eg = [col(i, 2, SM_DECAY) for i in idx]
        rhs = [jnp.concatenate([seg(i, v_col) * col(i, 0, SM_BETA), kb[i] * eg[i]],
                               axis=1).astype(BF16) for i in idx]
        sol = [_dot(wmat[i][:, CHUNK:].astype(BF16), rhs[i]).astype(BF16) for i in idx]
        att = [(kq[i][CHUNK:] * decay[i]).astype(BF16) for i in idx]
        k_tail_t = [(k[i] * col(i, 3, SM_DECAY)).T.astype(BF16) for i in idx]
        a_uw = [_dot(att[i], sol[i]) for i in idx]
        k_uw = [_dot(k_tail_t[i], sol[i]) for i in idx]
        for i, (c, h) in enumerate(units):
            oc_ref[rows[i], hsl[i]] = a_uw[i][:, 0:B_HEAD_DIM]
            lhs_ref[c, h, 0:CHUNK, :] = (q[i] * eg[i] - a_uw[i][:, B_HEAD_DIM:]).astype(BF16)
            lhs_ref[c, h, CHUNK:, :] = k_uw[i][:, B_HEAD_DIM:].astype(BF16)
            bm_ref[c, h] = k_uw[i][:, 0:B_HEAD_DIM]
        return carry

    lax.fori_loop(0, n_chunks // prep_unroll, prep_body, 0)

    def scan_body(c, carry):
        r0 = pl.multiple_of(c * CHUNK, CHUNK)
        rows = pl.ds(r0, CHUNK)
        heads = range(B_HEADS)
        hsl = [slice(h * LANES, (h + 1) * LANES) for h in heads]
        s_prev = [state_ref[h] for h in heads]
        r = [_dot(lhs_ref[c, h], s_prev[h].astype(BF16)) for h in heads]
        for h in heads:
            gl = gate_ref[4, rows, SM_DECAY + h:SM_DECAY + h + 1][0:1, :]
            oc_ref[rows, hsl[h]] = oc_ref[rows, hsl[h]] + r[h][0:CHUNK]
            state_ref[h] = s_prev[h] * gl + bm_ref[c, h] - r[h][CHUNK:]
        return carry

    lax.fori_loop(0, n_chunks, scan_body, 0)

    for h in range(B_HEADS):
        hs = slice(h * LANES, (h + 1) * LANES)
        z_gate = gdn_ref[:, z_col + h * LANES:z_col + (h + 1) * LANES]
        o_ref[:, hs] = (_rms(oc_ref[:, hs], ng_ref[...]) * z_gate).astype(BF16)


def _gdn(gdn_in, sm, alog_row, dtb_row, norm_g, bsz, seq, tc):
    nt = seq // tc
    trow = lambda b, t: (b * nt + t, 0)
    const = lambda b, t: (0, 0)
    n_chunks = tc // CHUNK
    return pl.pallas_call(
        functools.partial(_gdn_body, tc=tc),
        grid=(bsz, nt),
        in_specs=[
            pl.BlockSpec((tc, GDN_WIDTH), trow),
            pl.BlockSpec((tc, LANES), trow),
            pl.BlockSpec((1, LANES), const),
            pl.BlockSpec((1, LANES), const),
            pl.BlockSpec((1, B_HEAD_DIM), const),
        ],
        out_specs=pl.BlockSpec((tc, B_WIDTH), trow),
        out_shape=jax.ShapeDtypeStruct((bsz * seq, B_WIDTH), BF16),
        scratch_shapes=[
            pltpu.VMEM((5, tc, LANES), F32),
            pltpu.VMEM((n_chunks, LANES, CHUNK), F32),
            pltpu.VMEM((B_HEADS, B_HEAD_DIM, B_HEAD_DIM), F32),
            pltpu.VMEM((tc, B_WIDTH), F32),
            pltpu.VMEM((n_chunks, B_HEADS, CHUNK + B_HEAD_DIM, B_HEAD_DIM), BF16),
            pltpu.VMEM((n_chunks, B_HEADS, B_HEAD_DIM, B_HEAD_DIM), F32),
        ],
        compiler_params=pltpu.CompilerParams(
            dimension_semantics=("arbitrary", "arbitrary"), vmem_limit_bytes=VMEM_LIMIT),
        name="gdn",
    )(gdn_in, sm, alog_row, dtb_row, norm_g)


def _ffn_body(x_ref, oa_ref, ob_ref, wo_ref, g2_ref, w1_ref, w2_ref, g3_ref, out_ref, *,
              final_norm):
    y = (x_ref[...] + _dot(oa_ref[...], wo_ref[0:A_WIDTH, :])
         + _dot(ob_ref[...], wo_ref[A_WIDTH:, :]))
    h = _rms(y, g2_ref[...]).astype(BF16)
    a = jnp.square(jnp.maximum(_dot(h, w1_ref[...]), 0.0)).astype(BF16)
    acc = y + _dot(a, w2_ref[...])
    out_ref[...] = _rms(acc, g3_ref[...]) if final_norm else acc


def _ffn(x2, oa, ob, wo, g2, w1, w2, g3, tm, final_norm):
    m = x2.shape[0]
    row = lambda i: (i, 0)
    const = lambda i: (0, 0)
    resident = functools.partial(pl.BlockSpec, index_map=const, pipeline_mode=pl.Buffered(1))
    return pl.pallas_call(
        functools.partial(_ffn_body, final_norm=final_norm),
        grid=(m // tm,),
        in_specs=[
            pl.BlockSpec((tm, D_MODEL), row),
            pl.BlockSpec((tm, A_WIDTH), row),
            pl.BlockSpec((tm, B_WIDTH), row),
            resident((D_MODEL, D_MODEL)),
            pl.BlockSpec((1, D_MODEL), const),
            resident((D_MODEL, D_FF)),
            resident((D_FF, D_MODEL)),
            pl.BlockSpec((1, D_MODEL), const),
        ],
        out_specs=pl.BlockSpec((tm, D_MODEL), row),
        out_shape=jax.ShapeDtypeStruct((m, D_MODEL), F32),
        compiler_params=pltpu.CompilerParams(
            dimension_semantics=("arbitrary",), vmem_limit_bytes=VMEM_LIMIT),
        name="ffn",
    )(x2, oa, ob, wo, g2, w1, w2, g3)


def _rope_tables(seq):
    half = A_HEAD_DIM // 2
    inv_freq = 1.0 / (ROPE_THETA ** (jnp.arange(half, dtype=F32) / half))
    ang = jnp.arange(seq).astype(F32)[:, None] * inv_freq[None, :]
    cos = jnp.cos(ang)
    sin = jnp.sin(ang)
    reps = LANES // A_HEAD_DIM
    return (jnp.tile(cos, (1, 2 * reps)),
            jnp.tile(jnp.concatenate([-sin, sin], axis=1), (1, reps)))


def _lane_row(vals, offset):
    return jnp.zeros((1, LANES), F32).at[0, offset:offset + vals.shape[0]].set(vals.astype(F32))


def kernel(x, norm_mix_g, w_in, conv_w, a_log, dt_bias, gdn_norm_g, w_out,
           norm_ffn_g, w_ff1, w_ff2, norm_final_g):
    bsz, seq, d = x.shape
    depth = w_in.shape[0]
    m = bsz * seq
    cos_t, sin_t = _rope_tables(seq)
    x2 = x.reshape(m, d)
    for l in range(depth):
        qa, qi, kv, gdn_in, sm = _inproj(
            x2, norm_mix_g[l][None, :], jnp.swapaxes(w_in, 1, 2), l, cos_t, sin_t, conv_w[l],
            seq, tm=512)
        o_a = _mixer_a(qa, qi, kv, sm, bsz, seq)
        o_b = _gdn(gdn_in, sm, _lane_row(a_log[l], SM_DECAY),
                   _lane_row(dt_bias[l], SM_DECAY), gdn_norm_g[l][None, :], bsz, seq, tc=512)
        x2 = _ffn(x2, o_a, o_b, w_out[l].astype(BF16), norm_ffn_g[l][None, :],
                  w_ff1[l].astype(BF16), w_ff2[l].astype(BF16), norm_final_g[None, :],
                  tm=512, final_norm=(l == depth - 1))
    return x2.reshape(bsz, seq, d)
```

```python
import functools

import jax
import jax.numpy as jnp
from jax import lax
from jax.experimental import pallas as pl
from jax.experimental.pallas import tpu as pltpu

F32 = jnp.float32
BF16 = jnp.bfloat16
I32 = jnp.int32

D_MODEL = 1024
CHUNK = 64
A_QUERIES = 256
ROPE_THETA = 10000.0
EPS = 1e-6
A_HEADS = 8
A_KV_HEADS = 2
A_HEAD_DIM = 64
IDX_HEADS = 8
IDX_DIM = 64
TOPK_MAX = 256
B_HEADS = 4
B_HEAD_DIM = 128
CONV_WIDTH = 4
D_FF = 4 * D_MODEL

LANES = 128
A_WIDTH = A_HEADS * A_HEAD_DIM
KV_WIDTH = A_KV_HEADS * A_HEAD_DIM
B_WIDTH = B_HEADS * B_HEAD_DIM
GDN_WIDTH = 4 * B_WIDTH
SM_KI = 0
SM_WI = IDX_DIM
SM_BETA = SM_WI + IDX_HEADS
SM_DECAY = SM_BETA + B_HEADS
C_QA = 0
C_QI = C_QA + A_WIDTH
C_KV = C_QI + A_WIDTH
C_GDN = C_KV + 2 * KV_WIDTH
C_SM = C_GDN + GDN_WIDTH
IN_COLS = C_SM + LANES
REF_SIZES = (A_WIDTH, KV_WIDTH, KV_WIDTH, IDX_HEADS * IDX_DIM, IDX_DIM, IDX_HEADS,
             B_WIDTH, B_WIDTH, B_WIDTH, B_WIDTH, B_HEADS, B_HEADS)
REF_OFF = tuple(sum(REF_SIZES[:i]) for i in range(len(REF_SIZES)))
IN_DIM = sum(REF_SIZES)
IN_WEIGHT_MOVES = (
    (C_QA, REF_OFF[0], A_WIDTH),
    (C_QI, REF_OFF[3], A_WIDTH),
    (C_KV, REF_OFF[1], 2 * KV_WIDTH),
    (C_GDN, REF_OFF[6], GDN_WIDTH),
    (C_SM + SM_KI, REF_OFF[4], IDX_DIM + IDX_HEADS),
    (C_SM + SM_BETA, REF_OFF[10], 2 * B_HEADS),
)
IN_PAD = IN_COLS - (C_SM + SM_DECAY + B_HEADS)

VMEM_LIMIT = 56 * 1024 * 1024
CONV_PAD = 8
COUNT_GROUP = 4
FINE_STEPS = 17
FINE_CHECK = 6
COUNT_ROWS = 32
V_ROWS = A_HEAD_DIM + 16
ATT_HEADS = 2
ATT_WAVE = 4
KEY_BLOCK = 2 * LANES
NEG_BIG = -1e30
BOUND_SLACK = 1.001
MIN_WEIGHT_SUM = 2.0 ** -60
LOG2_E = 1.4426950408889634
NEG_INF_KEY = -(2 ** 31) + 0x7FFFFF
POS_INF_KEY = 0x7F800000


def _rms(x, g):
    return x * lax.rsqrt(jnp.mean(x * x, axis=-1, keepdims=True) + EPS) * g


def _dot(a, b):
    return jnp.dot(a, b, preferred_element_type=F32)


def _dot_nt(a, b):
    return lax.dot_general(a, b, (((1,), (1,)), ((), ())), preferred_element_type=F32)


def _inproj_body(x_ref, g_ref, win_ref, cos_ref, sin_ref, cw_ref,
                 qa_ref, qi_ref, kv_ref, gdn_ref, sm_ref, xpad_ref, w_ref, *, tiles_per_seq):
    @pl.when(pl.program_id(0) == 0)
    def _():
        step = 256
        for dst, src, width in IN_WEIGHT_MOVES[:-2]:
            for r0 in range(0, width, step):
                w_ref[dst + r0:dst + r0 + step, :] = win_ref[src + r0:src + r0 + step, :].astype(BF16)
        small = [win_ref[src:src + width, :] for _, src, width in IN_WEIGHT_MOVES[-2:]]
        small.append(jnp.zeros((IN_PAD, D_MODEL), F32))
        w_ref[C_SM:IN_COLS, :] = jnp.concatenate(small, axis=0).astype(BF16)

    h = _rms(x_ref[...], g_ref[...]).astype(BF16)
    cos = cos_ref[...]
    sin = sin_ref[...]
    lane = lax.broadcasted_iota(I32, cos.shape, 1)
    first_half = (lane & (A_HEAD_DIM - 1)) < A_HEAD_DIM // 2

    def rope(t):
        swapped = jnp.where(first_half, pltpu.roll(t, LANES - A_HEAD_DIM // 2, 1),
                            pltpu.roll(t, A_HEAD_DIM // 2, 1))
        return t * cos + swapped * sin

    def proj(c0, width):
        return _dot_nt(h, w_ref[c0:c0 + width, :])

    tm = x_ref.shape[0]
    conv_cols = 3 * B_WIDTH

    @pl.when(pl.program_id(0) % tiles_per_seq == 0)
    def _():
        xpad_ref[0:CONV_PAD, :] = jnp.zeros((CONV_PAD, conv_cols), F32)

    @pl.when(pl.program_id(0) % tiles_per_seq != 0)
    def _():
        xpad_ref[0:CONV_PAD, :] = xpad_ref[tm:tm + CONV_PAD, :]

    def stage_conv_input(seg):
        xpad_ref[CONV_PAD:CONV_PAD + tm, seg * B_WIDTH:(seg + 1) * B_WIDTH] = proj(
            C_GDN + seg * B_WIDTH, B_WIDTH)

    def conv_segment(seg):
        for hh in range(B_HEADS):
            cs = slice(seg * B_WIDTH + hh * LANES, seg * B_WIDTH + (hh + 1) * LANES)
            xa = xpad_ref[:, cs]
            y = cw_ref[0:1, cs] * xa
            for jj in range(1, CONV_WIDTH):
                y = cw_ref[jj:jj + 1, cs] * xa + pltpu.roll(y, 1, 0)
            y = y[CONV_PAD:, :]
            y = y * jax.nn.sigmoid(y)
            if seg < 2:
                y = y * lax.rsqrt(jnp.sum(y * y, axis=-1, keepdims=True) + EPS)
            if seg == 0:
                y = y * (B_HEAD_DIM ** -0.5)
            gdn_ref[:, cs] = y

    def store_queries_transposed(out_ref, acc, scale):
        for j in range(A_WIDTH // LANES):
            t = rope(acc[:, j * LANES:(j + 1) * LANES])
            if scale != 1.0:
                t = t * scale
            for b in range(tm // A_QUERIES):
                tt = t[b * A_QUERIES:(b + 1) * A_QUERIES, :].T
                for hh in range(LANES // A_HEAD_DIM):
                    h0 = (j * (LANES // A_HEAD_DIM) + hh) * A_QUERIES
                    out_ref[b, :, h0:h0 + A_QUERIES] = (
                        tt[hh * A_HEAD_DIM:(hh + 1) * A_HEAD_DIM, :].astype(BF16))

    stage_conv_input(0)
    acc = proj(C_QA, A_WIDTH)
    conv_segment(0)
    store_queries_transposed(qa_ref, acc, A_HEAD_DIM ** -0.5 * LOG2_E)
    stage_conv_input(1)
    acc = proj(C_QI, A_WIDTH)
    conv_segment(1)
    store_queries_transposed(qi_ref, acc, 1.0)
    stage_conv_input(2)
    acc = proj(C_KV, 2 * KV_WIDTH)
    acc_sm = proj(C_SM, LANES)
    z = proj(C_GDN + conv_cols, B_WIDTH)
    conv_segment(2)
    kv_ref[:, 0:KV_WIDTH] = rope(acc[:, 0:KV_WIDTH]).astype(BF16)
    kv_ref[:, KV_WIDTH:] = acc[:, KV_WIDTH:].astype(BF16)
    sm_ref[...] = jnp.where(lane < IDX_DIM, rope(acc_sm), acc_sm)
    gdn_ref[:, conv_cols:] = z * jax.nn.sigmoid(z)


def _inproj(x2, g, w, layer, cos_t, sin_t, conv_w, seq, tm):
    m = x2.shape[0]
    assert seq % tm == 0 and tm % A_QUERIES == 0, (seq, tm)
    nt = seq // tm
    row = lambda i: (i, 0)
    const = lambda i: (0, 0)
    return pl.pallas_call(
        functools.partial(_inproj_body, tiles_per_seq=nt),
        grid=(m // tm,),
        in_specs=[
            pl.BlockSpec((tm, D_MODEL), row),
            pl.BlockSpec((1, D_MODEL), const),
            pl.BlockSpec((None, IN_DIM, D_MODEL), lambda i: (layer, 0, 0),
                         pipeline_mode=pl.Buffered(1)),
            pl.BlockSpec((tm, LANES), lambda i: (i % nt, 0)),
            pl.BlockSpec((tm, LANES), lambda i: (i % nt, 0)),
            pl.BlockSpec((CONV_WIDTH, 3 * B_WIDTH), const),
        ],
        out_specs=[
            pl.BlockSpec((tm // A_QUERIES, A_HEAD_DIM, A_HEADS * A_QUERIES), lambda i: (i, 0, 0)),
            pl.BlockSpec((tm // A_QUERIES, IDX_DIM, IDX_HEADS * A_QUERIES), lambda i: (i, 0, 0)),
            pl.BlockSpec((tm, 2 * KV_WIDTH), row),
            pl.BlockSpec((tm, GDN_WIDTH), row),
            pl.BlockSpec((tm, LANES), row),
        ],
        out_shape=[
            jax.ShapeDtypeStruct((m // A_QUERIES, A_HEAD_DIM, A_HEADS * A_QUERIES), BF16),
            jax.ShapeDtypeStruct((m // A_QUERIES, IDX_DIM, IDX_HEADS * A_QUERIES), BF16),
            jax.ShapeDtypeStruct((m, 2 * KV_WIDTH), BF16),
            jax.ShapeDtypeStruct((m, GDN_WIDTH), F32),
            jax.ShapeDtypeStruct((m, LANES), F32),
        ],
        scratch_shapes=[
            pltpu.VMEM((tm + CONV_PAD, 3 * B_WIDTH), F32),
            pltpu.VMEM((IN_COLS, D_MODEL), BF16),
        ],
        compiler_params=pltpu.CompilerParams(
            dimension_semantics=("arbitrary",), vmem_limit_bytes=VMEM_LIMIT),
        name="inproj",
    )(x2, g, w, cos_t, sin_t, conv_w)


def _mixer_a_body(qat_ref, qit_ref, kv_ref, sm_ref, o_ref,
                  isc_ref, isb_ref, vt_ref, acc_ref, bias_ref, knorm_ref, *, seq, topk):
    j = pl.program_id(1)
    nkt = (j + 1) * (A_QUERIES // LANES)
    n_heads_pair = A_WIDTH // LANES
    group = A_HEADS // A_KV_HEADS

    @pl.when(j == 0)
    def _():
        per_block = KEY_BLOCK // LANES
        for t in range(seq // LANES):
            vt = kv_ref[t * LANES:(t + 1) * LANES, KV_WIDTH:].astype(F32).T.astype(BF16)
            c0 = (t % per_block) * LANES
            for g in range(A_KV_HEADS):
                vt_ref[t // per_block, g * V_ROWS:g * V_ROWS + A_HEAD_DIM, c0:c0 + LANES] = (
                    vt[g * A_HEAD_DIM:(g + 1) * A_HEAD_DIM])
        for g in range(A_KV_HEADS):
            vt_ref[:, g * V_ROWS + A_HEAD_DIM:(g + 1) * V_ROWS, :] = jnp.ones(
                (seq // KEY_BLOCK, V_ROWS - A_HEAD_DIM, KEY_BLOCK), BF16)
            kg = kv_ref[:, g * A_HEAD_DIM:(g + 1) * A_HEAD_DIM].astype(F32)
            k_norm = jnp.sqrt(jnp.max(jnp.sum(kg * kg, axis=1, keepdims=True), axis=0, keepdims=True))
            knorm_ref[g:g + 1, :] = jnp.broadcast_to(k_norm, (1, LANES))

    q0 = pl.multiple_of(j * A_QUERIES, A_QUERIES)
    w_t = sm_ref[pl.ds(q0, A_QUERIES), :].T[SM_WI:SM_WI + IDX_HEADS, :]
    w_t = w_t * ((IDX_HEADS ** -0.5) * (IDX_DIM ** -0.5))

    qlane = lax.broadcasted_iota(I32, (1, A_QUERIES), 1)
    limit = q0 + (lax.shift_right_logical(qlane, CHUNK.bit_length() - 1) + 1) * CHUNK
    tile_iota = lax.broadcasted_iota(I32, (LANES, A_QUERIES), 0)
    block_iota = lax.broadcasted_iota(I32, (KEY_BLOCK, A_QUERIES), 0)
    nkb = lax.shift_right_logical(nkt + (KEY_BLOCK // LANES - 1), (KEY_BLOCK // LANES).bit_length() - 1)

    def isc_body(kb, carry):
        r0 = pl.multiple_of(kb * KEY_BLOCK, KEY_BLOCK)
        kid = sm_ref[pl.ds(r0, KEY_BLOCK), SM_KI:SM_KI + IDX_DIM].astype(BF16)
        rel = _dot(kid, qit_ref[...])
        acc = jnp.zeros((KEY_BLOCK, A_QUERIES), F32)
        for h in range(IDX_HEADS):
            acc = acc + w_t[h:h + 1, :] * jnp.maximum(rel[:, h * A_QUERIES:(h + 1) * A_QUERIES], 0.0)
        acc = jnp.where(r0 + block_iota < limit, acc, -jnp.inf)
        isc_ref[pl.ds(r0, KEY_BLOCK), :] = acc
        isb_ref[pl.ds(r0, KEY_BLOCK), :] = acc.astype(BF16)
        return carry

    lax.fori_loop(0, nkb, isc_body, 0)

    ngrp = lax.shift_right_logical(nkt + (COUNT_GROUP - 1), COUNT_GROUP.bit_length() - 1)

    def fill_body(kt, carry):
        r0 = pl.multiple_of(kt * LANES, LANES)
        isc_ref[pl.ds(r0, LANES), :] = jnp.full((LANES, A_QUERIES), -jnp.inf, F32)
        isb_ref[pl.ds(r0, LANES), :] = jnp.full((LANES, A_QUERIES), -jnp.inf, BF16)
        return carry

    lax.fori_loop(nkb * (KEY_BLOCK // LANES), ngrp * COUNT_GROUP, fill_body, 0)

    def count(pred):
        def body(g, acc):
            for t in range(COUNT_GROUP):
                r0 = pl.multiple_of((g * COUNT_GROUP + t) * LANES, LANES)
                hit = jnp.where(pred(isc_ref[pl.ds(r0, LANES), :], r0 + tile_iota), 1.0, 0.0)
                acc = acc + hit.reshape(LANES // COUNT_ROWS, COUNT_ROWS, A_QUERIES).sum(axis=0)
            return acc
        acc = lax.fori_loop(0, ngrp, body, jnp.zeros((COUNT_ROWS, A_QUERIES), F32))
        return acc.sum(axis=0, keepdims=True)

    def count_coarse(thr):
        def body(g, acc):
            for t in range(COUNT_GROUP):
                r0 = pl.multiple_of((g * COUNT_GROUP + t) * LANES, LANES)
                hit = jnp.where(isb_ref[pl.ds(r0, LANES), :] >= thr, one_b, zero_b)
                hit = hit.reshape(LANES // COUNT_ROWS, COUNT_ROWS, A_QUERIES)
                part = hit[0]
                for i in range(1, LANES // COUNT_ROWS):
                    part = part + hit[i]
                acc = acc + part
            return acc
        acc = lax.fori_loop(0, ngrp, body, jnp.zeros((COUNT_ROWS, A_QUERIES), BF16))
        return acc.astype(F32).sum(axis=0, keepdims=True)

    kf = float(topk)
    one_b = jnp.ones((), BF16)
    zero_b = jnp.zeros((), BF16)

    def thr_coarse(key):
        bits = jnp.where(key >= 0, key, key ^ jnp.int32(0x7FFF))
        return lax.bitcast_convert_type(lax.shift_left(bits, 16), F32).astype(BF16)

    c0 = count_coarse(jnp.zeros((1, A_QUERIES), BF16))
    lo16 = jnp.where(c0 >= kf, jnp.int32(0), jnp.int32(-2 ** 15))

    def coarse_body(i, lo):
        trial = lo | lax.shift_left(jnp.int32(1), 14 - i)
        c = count_coarse(thr_coarse(trial))
        return jnp.where(c >= kf, trial, lo)

    lo16 = lax.fori_loop(0, 15, coarse_body, lo16)
    lo16 = jnp.maximum(lo16, jnp.int32(NEG_INF_KEY >> 16))
    center = lax.shift_left(lo16, 16) | jnp.where(lo16 < 0, jnp.int32(0xFFFF), jnp.int32(0))

    def thr_of(key):
        bits = jnp.where(key >= 0, key, key ^ jnp.int32(0x7FFFFFFF))
        return jnp.where(key < jnp.int32(NEG_INF_KEY), -jnp.inf, lax.bitcast_convert_type(bits, F32))

    def fine_body(i, st):
        lo, c_lo, c_hi = st
        trial = lo + lax.shift_left(jnp.int32(1), FINE_STEPS - 1 - i)
        thr = thr_of(trial)
        c = count(lambda x, s: x >= thr)
        ok = c >= kf
        return jnp.where(ok, trial, lo), jnp.where(ok, c, c_lo), jnp.where(ok, c_hi, c)

    top = thr_of(center + jnp.int32(1 << (FINE_STEPS - 1)))
    state = lax.fori_loop(0, FINE_CHECK, fine_body,
                          (center - jnp.int32(1 << (FINE_STEPS - 1)), jnp.full((1, A_QUERIES), -1.0, F32),
                           count(lambda x, s: x >= top)))
    lo, c_lo, c_hi = state

    hi = lo + jnp.int32(1 << (FINE_STEPS - FINE_CHECK))
    hi_thr = thr_of(hi)

    def max_below(bound):
        def body(g, acc):
            for t in range(COUNT_GROUP):
                r0 = pl.multiple_of((g * COUNT_GROUP + t) * LANES, LANES)
                x = isc_ref[pl.ds(r0, LANES), :]
                part = jnp.where(x < bound, x, -jnp.inf)
                acc = jnp.maximum(acc, part.reshape(LANES // COUNT_ROWS, COUNT_ROWS, A_QUERIES).max(axis=0))
            return acc
        acc = lax.fori_loop(0, ngrp, body, jnp.full((COUNT_ROWS, A_QUERIES), -jnp.inf, F32))
        return acc.max(axis=0, keepdims=True)

    v = max_below(hi_thr)
    n_v = count(lambda x, s: x == v)
    by_value = ((c_hi >= 0.0) & (c_hi + n_v >= kf)
                & (hi < jnp.int32(POS_INF_KEY)) & (lo >= jnp.int32(NEG_INF_KEY)))
    done = by_value | (c_lo == kf) | (limit <= topk)
    settled = jnp.min(jnp.where(done, 1.0, 0.0)) > 0.0

    def finish_now():
        return (jnp.where(by_value, v, thr_of(lo)), jnp.where(by_value, c_hi + n_v, c_lo),
                jnp.where(by_value, c_hi, -1.0))

    def finish_search():
        lo_f, c_lo_f, _ = lax.fori_loop(FINE_CHECK, FINE_STEPS, fine_body, state)
        kth_f = thr_of(lo_f)
        return kth_f, c_lo_f, count(lambda x, s: x > kth_f)

    kth, n_at, n_above = lax.cond(settled, finish_now, finish_search)

    n_ge = jnp.where(limit <= topk, 0.0, jnp.where(n_at < 0.0, jnp.inf, n_at))
    has_ties = jnp.max(n_ge) > kf

    acc_ref[...] = jnp.zeros_like(acc_ref)

    def select_plain(kb, x, s_idx, carry):
        return (s_idx < limit) & (x >= kth), carry

    def make_select_ties():
        need = jnp.where(n_above < 0.0, kf, kf - n_above)
        tri = jnp.where(lax.broadcasted_iota(I32, (KEY_BLOCK, KEY_BLOCK), 1)
                        <= lax.broadcasted_iota(I32, (KEY_BLOCK, KEY_BLOCK), 0), 1.0, 0.0).astype(BF16)

        def select_ties(kb, x, s_idx, before):
            eq = x == kth
            rank = before + _dot(tri, jnp.where(eq, 1.0, 0.0).astype(BF16))
            sel = (s_idx < limit) & ((x > kth) | (eq & (rank <= need)))
            return sel, rank[KEY_BLOCK - 1:KEY_BLOCK, :]

        return select_ties

    units = list(range(A_HEADS // ATT_HEADS))
    uw = ATT_HEADS * A_QUERIES
    us = [slice(u * uw, (u + 1) * uw) for u in units]
    kd = [slice((u * ATT_HEADS // group) * A_HEAD_DIM, (u * ATT_HEADS // group + 1) * A_HEAD_DIM)
          for u in units]
    vr = [slice((u * ATT_HEADS // group) * V_ROWS, (u * ATT_HEADS // group + 1) * V_ROWS)
          for u in units]
    cs = [slice((u * ATT_HEADS % group) * A_QUERIES, (u * ATT_HEADS % group + ATT_HEADS) * A_QUERIES)
          for u in units]

    def block_inputs(select, kb, sel_carry):
        r0 = pl.multiple_of(kb * KEY_BLOCK, KEY_BLOCK)
        x = isc_ref[pl.ds(r0, KEY_BLOCK), :]
        sel, sel_carry = select(kb, x, r0 + block_iota, sel_carry)
        bias_ref[...] = jnp.where(sel, 0.0, NEG_BIG)
        return kv_ref[pl.ds(r0, KEY_BLOCK), 0:KV_WIDTH], vt_ref[kb], sel_carry

    q_norm = jnp.sqrt(jnp.sum(jnp.square(qat_ref[...].astype(F32)), axis=0, keepdims=True))
    shift = [q_norm[:, us[u]] * (knorm_ref[u * ATT_HEADS // group:u * ATT_HEADS // group + 1, 0:1]
                                  * BOUND_SLACK) for u in units]

    def fast_body(select, kb, sel_carry):
        k_tile, v_t, sel_carry = block_inputs(select, kb, sel_carry)
        for w0 in range(0, len(units), ATT_WAVE):
            wave = units[w0:w0 + ATT_WAVE]
            p = {u: jnp.exp2(_dot(k_tile[:, kd[u]], qat_ref[:, us[u]])
                             + jnp.concatenate([bias_ref[...]] * ATT_HEADS, axis=1)
                             - shift[u]).astype(BF16) for u in wave}
            pv = {u: _dot(v_t[vr[u], :], p[u]) for u in wave}
            for u in wave:
                acc_ref[vr[u], cs[u]] += pv[u]
        return sel_carry

    def att_body(select, kb, state):
        m_prev, sel_carry = state
        k_tile, v_t, sel_carry = block_inputs(select, kb, sel_carry)
        m_new = {}
        for w0 in range(0, len(units), ATT_WAVE):
            wave = units[w0:w0 + ATT_WAVE]
            s = {u: _dot(k_tile[:, kd[u]], qat_ref[:, us[u]])
                 + jnp.concatenate([bias_ref[...]] * ATT_HEADS, axis=1) for u in wave}
            for u in wave:
                m_new[u] = jnp.maximum(m_prev[:, us[u]], jnp.max(s[u], axis=0, keepdims=True))
            alpha = {u: jnp.exp2(m_prev[:, us[u]] - m_new[u]) for u in wave}
            p = {u: jnp.exp2(s[u] - m_new[u]).astype(BF16) for u in wave}
            pv = {u: _dot(v_t[vr[u], :], p[u]) for u in wave}
            for u in wave:
                acc_ref[vr[u], cs[u]] = alpha[u] * acc_ref[vr[u], cs[u]] + pv[u]
        return jnp.concatenate([m_new[u] for u in units], axis=1), sel_carry

    def attend(select):
        no_ties_seen = jnp.zeros((1, A_QUERIES), F32)
        lax.fori_loop(0, nkb, functools.partial(fast_body, select), no_ties_seen)
        sums = jnp.concatenate([acc_ref[g * V_ROWS + A_HEAD_DIM:g * V_ROWS + A_HEAD_DIM + 1, :]
                                for g in range(A_KV_HEADS)], axis=1)

        @pl.when(jnp.logical_not(jnp.min(sums) >= MIN_WEIGHT_SUM))
        def _():
            acc_ref[...] = jnp.zeros_like(acc_ref)
            init = (jnp.full((1, A_HEADS * A_QUERIES), NEG_BIG, F32), no_ties_seen)
            lax.fori_loop(0, nkb, functools.partial(att_body, select), init)

    lax.cond(has_ties, lambda: attend(make_select_ties()), lambda: attend(select_plain))

    for p in range(n_heads_pair):
        g = (2 * p) // group
        parts = []
        for h in (2 * p, 2 * p + 1):
            hq = slice((h % group) * A_QUERIES, (h % group + 1) * A_QUERIES)
            parts.append(acc_ref[g * V_ROWS:g * V_ROWS + A_HEAD_DIM, hq]
                         / acc_ref[g * V_ROWS + A_HEAD_DIM:g * V_ROWS + A_HEAD_DIM + 1, hq])
        o_ref[:, p * LANES:(p + 1) * LANES] = jnp.concatenate(parts, axis=0).T.astype(BF16)


def _mixer_a(qa, qi, kv, sm, bsz, seq):
    nq = seq // A_QUERIES
    assert seq % (COUNT_GROUP * LANES) == 0, seq
    assert seq // COUNT_ROWS <= 256, seq
    topk = min(TOPK_MAX, seq // 4)
    qrow = lambda b, j: (b * nq + j, 0)
    qblock = lambda b, j: (b * nq + j, 0, 0)
    brow = lambda b, j: (b, 0)
    return pl.pallas_call(
        functools.partial(_mixer_a_body, seq=seq, topk=topk),
        grid=(bsz, nq),
        in_specs=[
            pl.BlockSpec((None, A_HEAD_DIM, A_HEADS * A_QUERIES), qblock),
            pl.BlockSpec((None, IDX_DIM, IDX_HEADS * A_QUERIES), qblock),
            pl.BlockSpec((seq, 2 * KV_WIDTH), brow),
            pl.BlockSpec((seq, LANES), brow),
        ],
        out_specs=pl.BlockSpec((A_QUERIES, A_WIDTH), qrow),
        out_shape=jax.ShapeDtypeStruct((bsz * seq, A_WIDTH), BF16),
        scratch_shapes=[
            pltpu.VMEM((seq, A_QUERIES), F32),
            pltpu.VMEM((seq, A_QUERIES), BF16),
            pltpu.VMEM((seq // KEY_BLOCK, A_KV_HEADS * V_ROWS, KEY_BLOCK), BF16),
            pltpu.VMEM((A_KV_HEADS * V_ROWS, (A_HEADS // A_KV_HEADS) * A_QUERIES), F32),
            pltpu.VMEM((KEY_BLOCK, A_QUERIES), F32),
            pltpu.VMEM((8, LANES), F32),
        ],
        compiler_params=pltpu.CompilerParams(
            dimension_semantics=("arbitrary", "arbitrary"), vmem_limit_bytes=VMEM_LIMIT),
        name="mixer_a",
    )(qa, qi, kv, sm)


def _gdn_body(gdn_ref, sm_ref, alog_ref, dtb_ref, ng_ref, o_ref,
              gate_ref, gct_ref, state_ref, oc_ref, lhs_ref, bm_ref, *, tc):
    tb = pl.program_id(1)
    n_chunks = tc // CHUNK
    prep_unroll = 8
    q_col, k_col, v_col, z_col = (i * B_WIDTH for i in range(4))

    @pl.when(tb == 0)
    def _():
        state_ref[...] = jnp.zeros_like(state_ref)

    sm = sm_ref[...]
    beta = jax.nn.sigmoid(sm)
    z = sm + dtb_ref[...]
    softplus = jnp.maximum(z, 0.0) + jnp.log(1.0 + jnp.exp(-jnp.abs(z)))
    g = -jnp.exp(alog_ref[...]) * softplus
    rin = lax.broadcasted_iota(I32, (tc, LANES), 0) & (CHUNK - 1)
    gc = g
    step = 1
    while step < CHUNK:
        gc = gc + jnp.where(rin >= step, pltpu.roll(gc, step, 0), 0.0)
        step *= 2
    gc3 = gc.reshape(n_chunks, CHUNK, LANES)
    g_last = jnp.broadcast_to(gc3[:, CHUNK - 1:CHUNK, :], gc3.shape).reshape(tc, LANES)
    gate_ref[0] = beta
    gate_ref[1] = gc
    gate_ref[2] = jnp.exp(gc)
    gate_ref[3] = jnp.exp(g_last - gc)
    gate_ref[4] = jnp.exp(g_last)
    for i in range(tc // LANES):
        t = gc[i * LANES:(i + 1) * LANES, :].T
        for half in range(LANES // CHUNK):
            gct_ref[i * (LANES // CHUNK) + half] = t[:, half * CHUNK:(half + 1) * CHUNK]

    ci = lax.broadcasted_iota(I32, (CHUNK, CHUNK), 0)
    si = lax.broadcasted_iota(I32, (CHUNK, CHUNK), 1)
    wl = lax.broadcasted_iota(I32, (CHUNK, 2 * CHUNK), 1)
    wr = lax.broadcasted_iota(I32, (CHUNK, 2 * CHUNK), 0)
    right = wl >= CHUNK
    eye_right = jnp.where(wl == wr + CHUNK, 1.0, 0.0)

    def prep_body(cg, carry):
        units = [(cg * prep_unroll + cc, h) for cc in range(prep_unroll) for h in range(B_HEADS)]
        rows = [pl.ds(pl.multiple_of(c * CHUNK, CHUNK), CHUNK) for c, _ in units]
        hsl = [slice(h * LANES, (h + 1) * LANES) for _, h in units]
        idx = range(len(units))

        def col(i, gate, off):
            h = units[i][1]
            return gate_ref[gate, rows[i], off + h:off + h + 1]

        def seg(i, col0):
            h = units[i][1]
            return gdn_ref[rows[i], col0 + h * LANES:col0 + (h + 1) * LANES]

        q = [seg(i, q_col) for i in idx]
        k = [seg(i, k_col) for i in idx]
        kb = [k[i] * col(i, 0, SM_BETA) for i in idx]
        kq = [_dot_nt(jnp.concatenate([kb[i], q[i]], axis=0).astype(BF16), k[i].astype(BF16))
              for i in idx]
        decay = []
        for i, (c, h) in enumerate(units):
            d = col(i, 1, SM_DECAY) - gct_ref[c][SM_DECAY + h:SM_DECAY + h + 1, :]
            decay.append(jnp.where(ci >= si, jnp.exp(jnp.where(ci >= si, d, 0.0)), 0.0))
        wmat = []
        for i, (c, h) in enumerate(units):
            n_mat = jnp.where(ci > si, -(kq[i][0:CHUNK] * decay[i]), 0.0)
            wmat.append(jnp.concatenate([n_mat, jnp.zeros_like(n_mat)], axis=1) + eye_right)
        pw = 1
        while pw < CHUNK:
            wb = [wmat[i].astype(BF16) for i in idx]
            wmat = [_dot(wb[i][:, 0:CHUNK], wb[i]) + jnp.where(right, wmat[i], 0.0) for i in idx]
            pw *= 2
        eg = [col(i, 2, SM_DECAY) for i in idx]
        rhs = [jnp.concatenate([seg(i, v_col) * col(i, 0, SM_BETA), kb[i] * eg[i]],
                               axis=1).astype(BF16) for i in idx]
        sol = [_dot(wmat[i][:, CHUNK:].astype(BF16), rhs[i]).astype(BF16) for i in idx]
        att = [(kq[i][CHUNK:] * decay[i]).astype(BF16) for i in idx]
        k_tail_t = [(k[i] * col(i, 3, SM_DECAY)).T.astype(BF16) for i in idx]
        a_uw = [_dot(att[i], sol[i]) for i in idx]
        k_uw = [_dot(k_tail_t[i], sol[i]) for i in idx]
        for i, (c, h) in enumerate(units):
            oc_ref[rows[i], hsl[i]] = a_uw[i][:, 0:B_HEAD_DIM]
            lhs_ref[c, h, 0:CHUNK, :] = (q[i] * eg[i] - a_uw[i][:, B_HEAD_DIM:]).astype(BF16)
            lhs_ref[c, h, CHUNK:, :] = k_uw[i][:, B_HEAD_DIM:].astype(BF16)
            bm_ref[c, h] = k_uw[i][:, 0:B_HEAD_DIM]
        return carry

    lax.fori_loop(0, n_chunks // prep_unroll, prep_body, 0)

    def scan_body(c, carry):
        r0 = pl.multiple_of(c * CHUNK, CHUNK)
        rows = pl.ds(r0, CHUNK)
        heads = range(B_HEADS)
        hsl = [slice(h * LANES, (h + 1) * LANES) for h in heads]
        s_prev = [state_ref[h] for h in heads]
        r = [_dot(lhs_ref[c, h], s_prev[h].astype(BF16)) for h in heads]
        for h in heads:
            gl = gate_ref[4, rows, SM_DECAY + h:SM_DECAY + h + 1][0:1, :]
            oc_ref[rows, hsl[h]] = oc_ref[rows, hsl[h]] + r[h][0:CHUNK]
            state_ref[h] = s_prev[h] * gl + bm_ref[c, h] - r[h][CHUNK:]
        return carry

    lax.fori_loop(0, n_chunks, scan_body, 0)

    for h in range(B_HEADS):
        hs = slice(h * LANES, (h + 1) * LANES)
        z_gate = gdn_ref[:, z_col + h * LANES:z_col + (h + 1) * LANES]
        o_ref[:, hs] = (_rms(oc_ref[:, hs], ng_ref[...]) * z_gate).astype(BF16)


def _gdn(gdn_in, sm, alog_row, dtb_row, norm_g, bsz, seq, tc):
    nt = seq // tc
    trow = lambda b, t: (b * nt + t, 0)
    const = lambda b, t: (0, 0)
    n_chunks = tc // CHUNK
    return pl.pallas_call(
        functools.partial(_gdn_body, tc=tc),
        grid=(bsz, nt),
        in_specs=[
            pl.BlockSpec((tc, GDN_WIDTH), trow),
            pl.BlockSpec((tc, LANES), trow),
            pl.BlockSpec((1, LANES), const),
            pl.BlockSpec((1, LANES), const),
            pl.BlockSpec((1, B_HEAD_DIM), const),
        ],
        out_specs=pl.BlockSpec((tc, B_WIDTH), trow),
        out_shape=jax.ShapeDtypeStruct((bsz * seq, B_WIDTH), BF16),
        scratch_shapes=[
            pltpu.VMEM((5, tc, LANES), F32),
            pltpu.VMEM((n_chunks, LANES, CHUNK), F32),
            pltpu.VMEM((B_HEADS, B_HEAD_DIM, B_HEAD_DIM), F32),
            pltpu.VMEM((tc, B_WIDTH), F32),
            pltpu.VMEM((n_chunks, B_HEADS, CHUNK + B_HEAD_DIM, B_HEAD_DIM), BF16),
            pltpu.VMEM((n_chunks, B_HEADS, B_HEAD_DIM, B_HEAD_DIM), F32),
        ],
        compiler_params=pltpu.CompilerParams(
            dimension_semantics=("arbitrary", "arbitrary"), vmem_limit_bytes=VMEM_LIMIT),
        name="gdn",
    )(gdn_in, sm, alog_row, dtb_row, norm_g)


def _ffn_body(x_ref, oa_ref, ob_ref, wo_ref, g2_ref, w1_ref, w2_ref, g3_ref, out_ref, *,
              final_norm):
    y = (x_ref[...] + _dot(oa_ref[...], wo_ref[0:A_WIDTH, :])
         + _dot(ob_ref[...], wo_ref[A_WIDTH:, :]))
    h = _rms(y, g2_ref[...]).astype(BF16)
    a = jnp.square(jnp.maximum(_dot(h, w1_ref[...]), 0.0)).astype(BF16)
    acc = y + _dot(a, w2_ref[...])
    out_ref[...] = _rms(acc, g3_ref[...]) if final_norm else acc


def _ffn(x2, oa, ob, wo, g2, w1, w2, g3, tm, final_norm):
    m = x2.shape[0]
    row = lambda i: (i, 0)
    const = lambda i: (0, 0)
    resident = functools.partial(pl.BlockSpec, index_map=const, pipeline_mode=pl.Buffered(1))
    return pl.pallas_call(
        functools.partial(_ffn_body, final_norm=final_norm),
        grid=(m // tm,),
        in_specs=[
            pl.BlockSpec((tm, D_MODEL), row),
            pl.BlockSpec((tm, A_WIDTH), row),
            pl.BlockSpec((tm, B_WIDTH), row),
            resident((D_MODEL, D_MODEL)),
            pl.BlockSpec((1, D_MODEL), const),
            resident((D_MODEL, D_FF)),
            resident((D_FF, D_MODEL)),
            pl.BlockSpec((1, D_MODEL), const),
        ],
        out_specs=pl.BlockSpec((tm, D_MODEL), row),
        out_shape=jax.ShapeDtypeStruct((m, D_MODEL), F32),
        compiler_params=pltpu.CompilerParams(
            dimension_semantics=("arbitrary",), vmem_limit_bytes=VMEM_LIMIT),
        name="ffn",
    )(x2, oa, ob, wo, g2, w1, w2, g3)


def _rope_tables(seq):
    half = A_HEAD_DIM // 2
    inv_freq = 1.0 / (ROPE_THETA ** (jnp.arange(half, dtype=F32) / half))
    ang = jnp.arange(seq).astype(F32)[:, None] * inv_freq[None, :]
    cos = jnp.cos(ang)
    sin = jnp.sin(ang)
    reps = LANES // A_HEAD_DIM
    return (jnp.tile(cos, (1, 2 * reps)),
            jnp.tile(jnp.concatenate([-sin, sin], axis=1), (1, reps)))


def _lane_row(vals, offset):
    return jnp.zeros((1, LANES), F32).at[0, offset:offset + vals.shape[0]].set(vals.astype(F32))


def kernel(x, norm_mix_g, w_in, conv_w, a_log, dt_bias, gdn_norm_g, w_out,
           norm_ffn_g, w_ff1, w_ff2, norm_final_g):
    bsz, seq, d = x.shape
    depth = w_in.shape[0]
    m = bsz * seq
    cos_t, sin_t = _rope_tables(seq)
    x2 = x.reshape(m, d)
    for l in range(depth):
        qa, qi, kv, gdn_in, sm = _inproj(
            x2, norm_mix_g[l][None, :], jnp.swapaxes(w_in, 1, 2), l, cos_t, sin_t, conv_w[l],
            seq, tm=512)
        o_a = _mixer_a(qa, qi, kv, sm, bsz, seq)
        o_b = _gdn(gdn_in, sm, _lane_row(a_log[l], SM_DECAY),
                   _lane_row(dt_bias[l], SM_DECAY), gdn_norm_g[l][None, :], bsz, seq, tc=512)
        x2 = _ffn(x2, o_a, o_b, w_out[l].astype(BF16), norm_ffn_g[l][None, :],
                  w_ff1[l].astype(BF16), w_ff2[l].astype(BF16), norm_final_g[None, :],
                  tm=512, final_norm=(l == depth - 1))
    return x2.reshape(bsz, seq, d)
```

```python
import functools

import jax
import jax.numpy as jnp
from jax import lax
from jax.experimental import pallas as pl
from jax.experimental.pallas import tpu as pltpu

F32 = jnp.float32
BF16 = jnp.bfloat16
I32 = jnp.int32

D_MODEL = 1024
CHUNK = 64
A_QUERIES = 256
ROPE_THETA = 10000.0
EPS = 1e-6
A_HEADS = 8
A_KV_HEADS = 2
A_HEAD_DIM = 64
IDX_HEADS = 8
IDX_DIM = 64
TOPK_MAX = 256
B_HEADS = 4
B_HEAD_DIM = 128
CONV_WIDTH = 4
D_FF = 4 * D_MODEL

LANES = 128
A_WIDTH = A_HEADS * A_HEAD_DIM
KV_WIDTH = A_KV_HEADS * A_HEAD_DIM
B_WIDTH = B_HEADS * B_HEAD_DIM
GDN_WIDTH = 4 * B_WIDTH
SM_KI = 0
SM_WI = IDX_DIM
SM_BETA = SM_WI + IDX_HEADS
SM_DECAY = SM_BETA + B_HEADS
C_QA = 0
C_QI = C_QA + A_WIDTH
C_KV = C_QI + A_WIDTH
C_GDN = C_KV + 2 * KV_WIDTH
C_SM = C_GDN + GDN_WIDTH
IN_COLS = C_SM + LANES
REF_SIZES = (A_WIDTH, KV_WIDTH, KV_WIDTH, IDX_HEADS * IDX_DIM, IDX_DIM, IDX_HEADS,
             B_WIDTH, B_WIDTH, B_WIDTH, B_WIDTH, B_HEADS, B_HEADS)
REF_OFF = tuple(sum(REF_SIZES[:i]) for i in range(len(REF_SIZES)))
IN_DIM = sum(REF_SIZES)
IN_WEIGHT_MOVES = (
    (C_QA, REF_OFF[0], A_WIDTH),
    (C_QI, REF_OFF[3], A_WIDTH),
    (C_KV, REF_OFF[1], 2 * KV_WIDTH),
    (C_GDN, REF_OFF[6], GDN_WIDTH),
    (C_SM + SM_KI, REF_OFF[4], IDX_DIM + IDX_HEADS),
    (C_SM + SM_BETA, REF_OFF[10], 2 * B_HEADS),
)
IN_PAD = IN_COLS - (C_SM + SM_DECAY + B_HEADS)

VMEM_LIMIT = 56 * 1024 * 1024
CONV_PAD = 8
COUNT_GROUP = 4
FINE_STEPS = 17
FINE_CHECK = 6
COUNT_ROWS = 32
V_ROWS = A_HEAD_DIM + 16
ATT_HEADS = 2
ATT_WAVE = 4
KEY_BLOCK = 2 * LANES
NEG_BIG = -1e30
BOUND_SLACK = 1.001
MIN_WEIGHT_SUM = 2.0 ** -60
LOG2_E = 1.4426950408889634
NEG_INF_KEY = -(2 ** 31) + 0x7FFFFF
POS_INF_KEY = 0x7F800000


def _rms(x, g):
    return x * lax.rsqrt(jnp.mean(x * x, axis=-1, keepdims=True) + EPS) * g


def _dot(a, b):
    return jnp.dot(a, b, preferred_element_type=F32)


def _dot_nt(a, b):
    return lax.dot_general(a, b, (((1,), (1,)), ((), ())), preferred_element_type=F32)


def _inproj_body(x_ref, g_ref, win_ref, cos_ref, sin_ref, cw_ref,
                 qa_ref, qi_ref, kv_ref, gdn_ref, sm_ref, xpad_ref, w_ref, *, tiles_per_seq):
    @pl.when(pl.program_id(0) == 0)
    def _():
        step = 256
        for dst, src, width in IN_WEIGHT_MOVES[:-2]:
            for r0 in range(0, width, step):
                w_ref[dst + r0:dst + r0 + step, :] = win_ref[src + r0:src + r0 + step, :].astype(BF16)
        small = [win_ref[src:src + width, :] for _, src, width in IN_WEIGHT_MOVES[-2:]]
        small.append(jnp.zeros((IN_PAD, D_MODEL), F32))
        w_ref[C_SM:IN_COLS, :] = jnp.concatenate(small, axis=0).astype(BF16)

    h = _rms(x_ref[...], g_ref[...]).astype(BF16)
    cos = cos_ref[...]
    sin = sin_ref[...]
    lane = lax.broadcasted_iota(I32, cos.shape, 1)
    first_half = (lane & (A_HEAD_DIM - 1)) < A_HEAD_DIM // 2

    def rope(t):
        swapped = jnp.where(first_half, pltpu.roll(t, LANES - A_HEAD_DIM // 2, 1),
                            pltpu.roll(t, A_HEAD_DIM // 2, 1))
        return t * cos + swapped * sin

    def proj(c0, width):
        return _dot_nt(h, w_ref[c0:c0 + width, :])

    tm = x_ref.shape[0]
    conv_cols = 3 * B_WIDTH

    @pl.when(pl.program_id(0) % tiles_per_seq == 0)
    def _():
        xpad_ref[0:CONV_PAD, :] = jnp.zeros((CONV_PAD, conv_cols), F32)

    @pl.when(pl.program_id(0) % tiles_per_seq != 0)
    def _():
        xpad_ref[0:CONV_PAD, :] = xpad_ref[tm:tm + CONV_PAD, :]

    def stage_conv_input(seg):
        xpad_ref[CONV_PAD:CONV_PAD + tm, seg * B_WIDTH:(seg + 1) * B_WIDTH] = proj(
            C_GDN + seg * B_WIDTH, B_WIDTH)

    def conv_segment(seg):
        for hh in range(B_HEADS):
            cs = slice(seg * B_WIDTH + hh * LANES, seg * B_WIDTH + (hh + 1) * LANES)
            xa = xpad_ref[:, cs]
            y = cw_ref[0:1, cs] * xa
            for jj in range(1, CONV_WIDTH):
                y = cw_ref[jj:jj + 1, cs] * xa + pltpu.roll(y, 1, 0)
            y = y[CONV_PAD:, :]
            y = y * jax.nn.sigmoid(y)
            if seg < 2:
                y = y * lax.rsqrt(jnp.sum(y * y, axis=-1, keepdims=True) + EPS)
            if seg == 0:
                y = y * (B_HEAD_DIM ** -0.5)
            gdn_ref[:, cs] = y

    def store_queries_transposed(out_ref, acc, scale):
        for j in range(A_WIDTH // LANES):
            t = rope(acc[:, j * LANES:(j + 1) * LANES])
            if scale != 1.0:
                t = t * scale
            for b in range(tm // A_QUERIES):
                tt = t[b * A_QUERIES:(b + 1) * A_QUERIES, :].T
                for hh in range(LANES // A_HEAD_DIM):
                    h0 = (j * (LANES // A_HEAD_DIM) + hh) * A_QUERIES
                    out_ref[b, :, h0:h0 + A_QUERIES] = (
                        tt[hh * A_HEAD_DIM:(hh + 1) * A_HEAD_DIM, :].astype(BF16))

    stage_conv_input(0)
    acc = proj(C_QA, A_WIDTH)
    conv_segment(0)
    store_queries_transposed(qa_ref, acc, A_HEAD_DIM ** -0.5 * LOG2_E)
    stage_conv_input(1)
    acc = proj(C_QI, A_WIDTH)
    conv_segment(1)
    store_queries_transposed(qi_ref, acc, 1.0)
    stage_conv_input(2)
    acc = proj(C_KV, 2 * KV_WIDTH)
    acc_sm = proj(C_SM, LANES)
    z = proj(C_GDN + conv_cols, B_WIDTH)
    conv_segment(2)
    kv_ref[:, 0:KV_WIDTH] = rope(acc[:, 0:KV_WIDTH]).astype(BF16)
    kv_ref[:, KV_WIDTH:] = acc[:, KV_WIDTH:].astype(BF16)
    sm_ref[...] = jnp.where(lane < IDX_DIM, rope(acc_sm), acc_sm)
    gdn_ref[:, conv_cols:] = z * jax.nn.sigmoid(z)


def _inproj(x2, g, w, layer, cos_t, sin_t, conv_w, seq, tm):
    m = x2.shape[0]
    assert seq % tm == 0 and tm % A_QUERIES == 0, (seq, tm)
    nt = seq // tm
    row = lambda i: (i, 0)
    const = lambda i: (0, 0)
    return pl.pallas_call(
        functools.partial(_inproj_body, tiles_per_seq=nt),
        grid=(m // tm,),
        in_specs=[
            pl.BlockSpec((tm, D_MODEL), row),
            pl.BlockSpec((1, D_MODEL), const),
            pl.BlockSpec((None, IN_DIM, D_MODEL), lambda i: (layer, 0, 0),
                         pipeline_mode=pl.Buffered(1)),
            pl.BlockSpec((tm, LANES), lambda i: (i % nt, 0)),
            pl.BlockSpec((tm, LANES), lambda i: (i % nt, 0)),
            pl.BlockSpec((CONV_WIDTH, 3 * B_WIDTH), const),
        ],
        out_specs=[
            pl.BlockSpec((tm // A_QUERIES, A_HEAD_DIM, A_HEADS * A_QUERIES), lambda i: (i, 0, 0)),
            pl.BlockSpec((tm // A_QUERIES, IDX_DIM, IDX_HEADS * A_QUERIES), lambda i: (i, 0, 0)),
            pl.BlockSpec((tm, 2 * KV_WIDTH), row),
            pl.BlockSpec((tm, GDN_WIDTH), row),
            pl.BlockSpec((tm, LANES), row),
        ],
        out_shape=[
            jax.ShapeDtypeStruct((m // A_QUERIES, A_HEAD_DIM, A_HEADS * A_QUERIES), BF16),
            jax.ShapeDtypeStruct((m // A_QUERIES, IDX_DIM, IDX_HEADS * A_QUERIES), BF16),
            jax.ShapeDtypeStruct((m, 2 * KV_WIDTH), BF16),
            jax.ShapeDtypeStruct((m, GDN_WIDTH), F32),
            jax.ShapeDtypeStruct((m, LANES), F32),
        ],
        scratch_shapes=[
            pltpu.VMEM((tm + CONV_PAD, 3 * B_WIDTH), F32),
            pltpu.VMEM((IN_COLS, D_MODEL), BF16),
        ],
        compiler_params=pltpu.CompilerParams(
            dimension_semantics=("arbitrary",), vmem_limit_bytes=VMEM_LIMIT),
        name="inproj",
    )(x2, g, w, cos_t, sin_t, conv_w)


def _mixer_a_body(qat_ref, qit_ref, kv_ref, sm_ref, o_ref,
                  isc_ref, isb_ref, vt_ref, acc_ref, bias_ref, knorm_ref, *, seq, topk):
    j = pl.program_id(1)
    nkt = (j + 1) * (A_QUERIES // LANES)
    n_heads_pair = A_WIDTH // LANES
    group = A_HEADS // A_KV_HEADS

    @pl.when(j == 0)
    def _():
        per_block = KEY_BLOCK // LANES
        for t in range(seq // LANES):
            vt = kv_ref[t * LANES:(t + 1) * LANES, KV_WIDTH:].astype(F32).T.astype(BF16)
            c0 = (t % per_block) * LANES
            for g in range(A_KV_HEADS):
                vt_ref[t // per_block, g * V_ROWS:g * V_ROWS + A_HEAD_DIM, c0:c0 + LANES] = (
                    vt[g * A_HEAD_DIM:(g + 1) * A_HEAD_DIM])
        for g in range(A_KV_HEADS):
            vt_ref[:, g * V_ROWS + A_HEAD_DIM:(g + 1) * V_ROWS, :] = jnp.ones(
                (seq // KEY_BLOCK, V_ROWS - A_HEAD_DIM, KEY_BLOCK), BF16)
            kg = kv_ref[:, g * A_HEAD_DIM:(g + 1) * A_HEAD_DIM].astype(F32)
            k_norm = jnp.sqrt(jnp.max(jnp.sum(kg * kg, axis=1, keepdims=True), axis=0, keepdims=True))
            knorm_ref[g:g + 1, :] = jnp.broadcast_to(k_norm, (1, LANES))

    q0 = pl.multiple_of(j * A_QUERIES, A_QUERIES)
    w_t = sm_ref[pl.ds(q0, A_QUERIES), :].T[SM_WI:SM_WI + IDX_HEADS, :]
    w_t = w_t * ((IDX_HEADS ** -0.5) * (IDX_DIM ** -0.5))

    qlane = lax.broadcasted_iota(I32, (1, A_QUERIES), 1)
    limit = q0 + (lax.shift_right_logical(qlane, CHUNK.bit_length() - 1) + 1) * CHUNK
    tile_iota = lax.broadcasted_iota(I32, (LANES, A_QUERIES), 0)
    block_iota = lax.broadcasted_iota(I32, (KEY_BLOCK, A_QUERIES), 0)
    nkb = lax.shift_right_logical(nkt + (KEY_BLOCK // LANES - 1), (KEY_BLOCK // LANES).bit_length() - 1)

    def isc_body(kb, carry):
        r0 = pl.multiple_of(kb * KEY_BLOCK, KEY_BLOCK)
        kid = sm_ref[pl.ds(r0, KEY_BLOCK), SM_KI:SM_KI + IDX_DIM].astype(BF16)
        rel = _dot(kid, qit_ref[...])
        acc = jnp.zeros((KEY_BLOCK, A_QUERIES), F32)
        for h in range(IDX_HEADS):
            acc = acc + w_t[h:h + 1, :] * jnp.maximum(rel[:, h * A_QUERIES:(h + 1) * A_QUERIES], 0.0)
        acc = jnp.where(r0 + block_iota < limit, acc, -jnp.inf)
        isc_ref[pl.ds(r0, KEY_BLOCK), :] = acc
        rounded = acc.astype(BF16)
        isb_ref[pl.ds(r0, KEY_BLOCK), :] = rounded
        hit = jnp.where(rounded.astype(F32) >= 0.0, 1.0, 0.0)
        return carry + hit.reshape(KEY_BLOCK // COUNT_ROWS, COUNT_ROWS, A_QUERIES).sum(axis=0)

    n_nonneg = lax.fori_loop(0, nkb, isc_body, jnp.zeros((COUNT_ROWS, A_QUERIES), F32))
    n_nonneg = n_nonneg.sum(axis=0, keepdims=True)

    ngrp = lax.shift_right_logical(nkt + (COUNT_GROUP - 1), COUNT_GROUP.bit_length() - 1)

    def fill_body(kt, carry):
        r0 = pl.multiple_of(kt * LANES, LANES)
        isc_ref[pl.ds(r0, LANES), :] = jnp.full((LANES, A_QUERIES), -jnp.inf, F32)
        isb_ref[pl.ds(r0, LANES), :] = jnp.full((LANES, A_QUERIES), -jnp.inf, BF16)
        return carry

    lax.fori_loop(nkb * (KEY_BLOCK // LANES), ngrp * COUNT_GROUP, fill_body, 0)

    def count(pred):
        def body(g, acc):
            for t in range(COUNT_GROUP):
                r0 = pl.multiple_of((g * COUNT_GROUP + t) * LANES, LANES)
                hit = jnp.where(pred(isc_ref[pl.ds(r0, LANES), :], r0 + tile_iota), 1.0, 0.0)
                acc = acc + hit.reshape(LANES // COUNT_ROWS, COUNT_ROWS, A_QUERIES).sum(axis=0)
            return acc
        acc = lax.fori_loop(0, ngrp, body, jnp.zeros((COUNT_ROWS, A_QUERIES), F32))
        return acc.sum(axis=0, keepdims=True)

    def count_coarse(thr):
        def body(g, acc):
            for t in range(COUNT_GROUP):
                r0 = pl.multiple_of((g * COUNT_GROUP + t) * LANES, LANES)
                hit = jnp.where(isb_ref[pl.ds(r0, LANES), :] >= thr, one_b, zero_b)
                hit = hit.reshape(LANES // COUNT_ROWS, COUNT_ROWS, A_QUERIES)
                part = hit[0]
                for i in range(1, LANES // COUNT_ROWS):
                    part = part + hit[i]
                acc = acc + part
            return acc
        acc = lax.fori_loop(0, ngrp, body, jnp.zeros((COUNT_ROWS, A_QUERIES), BF16))
        return acc.astype(F32).sum(axis=0, keepdims=True)

    kf = float(topk)
    one_b = jnp.ones((), BF16)
    zero_b = jnp.zeros((), BF16)

    def thr_coarse(key):
        bits = jnp.where(key >= 0, key, key ^ jnp.int32(0x7FFF))
        return lax.bitcast_convert_type(lax.shift_left(bits, 16), F32).astype(BF16)

    lo16 = jnp.where(n_nonneg >= kf, jnp.int32(0), jnp.int32(-2 ** 15))

    def coarse_body(i, lo):
        trial = lo | lax.shift_left(jnp.int32(1), 14 - i)
        c = count_coarse(thr_coarse(trial))
        return jnp.where(c >= kf, trial, lo)

    lo16 = lax.fori_loop(0, 15, coarse_body, lo16)
    lo16 = jnp.maximum(lo16, jnp.int32(NEG_INF_KEY >> 16))
    center = lax.shift_left(lo16, 16) | jnp.where(lo16 < 0, jnp.int32(0xFFFF), jnp.int32(0))

    def thr_of(key):
        bits = jnp.where(key >= 0, key, key ^ jnp.int32(0x7FFFFFFF))
        return jnp.where(key < jnp.int32(NEG_INF_KEY), -jnp.inf, lax.bitcast_convert_type(bits, F32))

    def fine_body(i, st):
        lo, c_lo = st
        trial = lo + lax.shift_left(jnp.int32(1), FINE_STEPS - 1 - i)
        thr = thr_of(trial)
        c = count(lambda x, s: x >= thr)
        ok = c >= kf
        return jnp.where(ok, trial, lo), jnp.where(ok, c, c_lo)

    state = lax.fori_loop(0, FINE_CHECK, fine_body,
                          (center - jnp.int32(1 << (FINE_STEPS - 1)), jnp.full((1, A_QUERIES), -1.0, F32)))
    lo, c_lo = state

    hi = lo + jnp.int32(1 << (FINE_STEPS - FINE_CHECK))
    hi_thr = thr_of(hi)

    def split_at(bound):
        def body(g, st):
            below_max, n_other = st
            for t in range(COUNT_GROUP):
                r0 = pl.multiple_of((g * COUNT_GROUP + t) * LANES, LANES)
                x = isc_ref[pl.ds(r0, LANES), :]
                below = x < bound
                part = jnp.where(below, x, -jnp.inf).reshape(LANES // COUNT_ROWS, COUNT_ROWS, A_QUERIES)
                hit = jnp.where(below, 0.0, 1.0).reshape(LANES // COUNT_ROWS, COUNT_ROWS, A_QUERIES)
                below_max = jnp.maximum(below_max, part.max(axis=0))
                n_other = n_other + hit.sum(axis=0)
            return below_max, n_other
        below_max, n_other = lax.fori_loop(
            0, ngrp, body, (jnp.full((COUNT_ROWS, A_QUERIES), -jnp.inf, F32),
                            jnp.zeros((COUNT_ROWS, A_QUERIES), F32)))
        return below_max.max(axis=0, keepdims=True), n_other.sum(axis=0, keepdims=True)

    v, c_hi = split_at(hi_thr)
    n_v = count(lambda x, s: x == v)
    by_value = ((c_hi + n_v >= kf) & (c_hi < kf)
                & (hi < jnp.int32(POS_INF_KEY)) & (lo >= jnp.int32(NEG_INF_KEY)))
    done = by_value | (c_lo == kf) | (limit <= topk)
    settled = jnp.min(jnp.where(done, 1.0, 0.0)) > 0.0

    def finish_now():
        return (jnp.where(by_value, v, thr_of(lo)), jnp.where(by_value, c_hi + n_v, c_lo),
                jnp.where(by_value, c_hi, -1.0))

    def finish_search():
        lo_f, c_lo_f = lax.fori_loop(FINE_CHECK, FINE_STEPS, fine_body, state)
        kth_f = thr_of(lo_f)
        return kth_f, c_lo_f, count(lambda x, s: x > kth_f)

    kth, n_at, n_above = lax.cond(settled, finish_now, finish_search)

    n_ge = jnp.where(limit <= topk, 0.0, jnp.where(n_at < 0.0, jnp.inf, n_at))
    has_ties = jnp.max(n_ge) > kf

    acc_ref[...] = jnp.zeros_like(acc_ref)

    def select_plain(kb, x, s_idx, carry):
        return (s_idx < limit) & (x >= kth), carry

    def make_select_ties():
        need = jnp.where(n_above < 0.0, kf, kf - n_above)
        tri = jnp.where(lax.broadcasted_iota(I32, (KEY_BLOCK, KEY_BLOCK), 1)
                        <= lax.broadcasted_iota(I32, (KEY_BLOCK, KEY_BLOCK), 0), 1.0, 0.0).astype(BF16)

        def select_ties(kb, x, s_idx, before):
            eq = x == kth
            rank = before + _dot(tri, jnp.where(eq, 1.0, 0.0).astype(BF16))
            sel = (s_idx < limit) & ((x > kth) | (eq & (rank <= need)))
            return sel, rank[KEY_BLOCK - 1:KEY_BLOCK, :]

        return select_ties

    units = list(range(A_HEADS // ATT_HEADS))
    uw = ATT_HEADS * A_QUERIES
    us = [slice(u * uw, (u + 1) * uw) for u in units]
    kd = [slice((u * ATT_HEADS // group) * A_HEAD_DIM, (u * ATT_HEADS // group + 1) * A_HEAD_DIM)
          for u in units]
    vr = [slice((u * ATT_HEADS // group) * V_ROWS, (u * ATT_HEADS // group + 1) * V_ROWS)
          for u in units]
    cs = [slice((u * ATT_HEADS % group) * A_QUERIES, (u * ATT_HEADS % group + ATT_HEADS) * A_QUERIES)
          for u in units]

    def block_inputs(select, kb, sel_carry):
        r0 = pl.multiple_of(kb * KEY_BLOCK, KEY_BLOCK)
        x = isc_ref[pl.ds(r0, KEY_BLOCK), :]
        sel, sel_carry = select(kb, x, r0 + block_iota, sel_carry)
        bias_ref[...] = jnp.where(sel, 0.0, NEG_BIG)
        return kv_ref[pl.ds(r0, KEY_BLOCK), 0:KV_WIDTH], vt_ref[kb], sel_carry

    q_norm = jnp.sqrt(jnp.sum(jnp.square(qat_ref[...].astype(F32)), axis=0, keepdims=True))
    shift = [q_norm[:, us[u]] * (knorm_ref[u * ATT_HEADS // group:u * ATT_HEADS // group + 1, 0:1]
                                  * BOUND_SLACK) for u in units]

    def fast_body(select, kb, sel_carry):
        k_tile, v_t, sel_carry = block_inputs(select, kb, sel_carry)
        for w0 in range(0, len(units), ATT_WAVE):
            wave = units[w0:w0 + ATT_WAVE]
            p = {u: jnp.exp2(_dot(k_tile[:, kd[u]], qat_ref[:, us[u]])
                             + jnp.concatenate([bias_ref[...]] * ATT_HEADS, axis=1)
                             - shift[u]).astype(BF16) for u in wave}
            pv = {u: _dot(v_t[vr[u], :], p[u]) for u in wave}
            for u in wave:
                acc_ref[vr[u], cs[u]] += pv[u]
        return sel_carry

    def att_body(select, kb, state):
        m_prev, sel_carry = state
        k_tile, v_t, sel_carry = block_inputs(select, kb, sel_carry)
        m_new = {}
        for w0 in range(0, len(units), ATT_WAVE):
            wave = units[w0:w0 + ATT_WAVE]
            s = {u: _dot(k_tile[:, kd[u]], qat_ref[:, us[u]])
                 + jnp.concatenate([bias_ref[...]] * ATT_HEADS, axis=1) for u in wave}
            for u in wave:
                m_new[u] = jnp.maximum(m_prev[:, us[u]], jnp.max(s[u], axis=0, keepdims=True))
            alpha = {u: jnp.exp2(m_prev[:, us[u]] - m_new[u]) for u in wave}
            p = {u: jnp.exp2(s[u] - m_new[u]).astype(BF16) for u in wave}
            pv = {u: _dot(v_t[vr[u], :], p[u]) for u in wave}
            for u in wave:
                acc_ref[vr[u], cs[u]] = alpha[u] * acc_ref[vr[u], cs[u]] + pv[u]
        return jnp.concatenate([m_new[u] for u in units], axis=1), sel_carry

    def attend(select):
        no_ties_seen = jnp.zeros((1, A_QUERIES), F32)
        lax.fori_loop(0, nkb, functools.partial(fast_body, select), no_ties_seen)
        sums = jnp.concatenate([acc_ref[g * V_ROWS + A_HEAD_DIM:g * V_ROWS + A_HEAD_DIM + 1, :]
                                for g in range(A_KV_HEADS)], axis=1)

        @pl.when(jnp.logical_not(jnp.min(sums) >= MIN_WEIGHT_SUM))
        def _():
            acc_ref[...] = jnp.zeros_like(acc_ref)
            init = (jnp.full((1, A_HEADS * A_QUERIES), NEG_BIG, F32), no_ties_seen)
            lax.fori_loop(0, nkb, functools.partial(att_body, select), init)

    lax.cond(has_ties, lambda: attend(make_select_ties()), lambda: attend(select_plain))

    for p in range(n_heads_pair):
        g = (2 * p) // group
        parts = []
        for h in (2 * p, 2 * p + 1):
            hq = slice((h % group) * A_QUERIES, (h % group + 1) * A_QUERIES)
            parts.append(acc_ref[g * V_ROWS:g * V_ROWS + A_HEAD_DIM, hq]
                         / acc_ref[g * V_ROWS + A_HEAD_DIM:g * V_ROWS + A_HEAD_DIM + 1, hq])
        o_ref[:, p * LANES:(p + 1) * LANES] = jnp.concatenate(parts, axis=0).T.astype(BF16)


def _mixer_a(qa, qi, kv, sm, bsz, seq):
    nq = seq // A_QUERIES
    assert seq % (COUNT_GROUP * LANES) == 0, seq
    assert seq // COUNT_ROWS <= 256, seq
    topk = min(TOPK_MAX, seq // 4)
    qrow = lambda b, j: (b * nq + j, 0)
    qblock = lambda b, j: (b * nq + j, 0, 0)
    brow = lambda b, j: (b, 0)
    return pl.pallas_call(
        functools.partial(_mixer_a_body, seq=seq, topk=topk),
        grid=(bsz, nq),
        in_specs=[
            pl.BlockSpec((None, A_HEAD_DIM, A_HEADS * A_QUERIES), qblock),
            pl.BlockSpec((None, IDX_DIM, IDX_HEADS * A_QUERIES), qblock),
            pl.BlockSpec((seq, 2 * KV_WIDTH), brow),
            pl.BlockSpec((seq, LANES), brow),
        ],
        out_specs=pl.BlockSpec((A_QUERIES, A_WIDTH), qrow),
        out_shape=jax.ShapeDtypeStruct((bsz * seq, A_WIDTH), BF16),
        scratch_shapes=[
            pltpu.VMEM((seq, A_QUERIES), F32),
            pltpu.VMEM((seq, A_QUERIES), BF16),
            pltpu.VMEM((seq // KEY_BLOCK, A_KV_HEADS * V_ROWS, KEY_BLOCK), BF16),
            pltpu.VMEM((A_KV_HEADS * V_ROWS, (A_HEADS // A_KV_HEADS) * A_QUERIES), F32),
            pltpu.VMEM((KEY_BLOCK, A_QUERIES), F32),
            pltpu.VMEM((8, LANES), F32),
        ],
        compiler_params=pltpu.CompilerParams(
            dimension_semantics=("arbitrary", "arbitrary"), vmem_limit_bytes=VMEM_LIMIT),
        name="mixer_a",
    )(qa, qi, kv, sm)


def _gdn_body(gdn_ref, sm_ref, alog_ref, dtb_ref, ng_ref, o_ref,
              gate_ref, gct_ref, state_ref, oc_ref, lhs_ref, bm_ref, *, tc):
    tb = pl.program_id(1)
    n_chunks = tc // CHUNK
    prep_unroll = 8
    q_col, k_col, v_col, z_col = (i * B_WIDTH for i in range(4))

    @pl.when(tb == 0)
    def _():
        state_ref[...] = jnp.zeros_like(state_ref)

    sm = sm_ref[...]
    beta = jax.nn.sigmoid(sm)
    z = sm + dtb_ref[...]
    softplus = jnp.maximum(z, 0.0) + jnp.log(1.0 + jnp.exp(-jnp.abs(z)))
    g = -jnp.exp(alog_ref[...]) * softplus
    rin = lax.broadcasted_iota(I32, (tc, LANES), 0) & (CHUNK - 1)
    gc = g
    step = 1
    while step < CHUNK:
        gc = gc + jnp.where(rin >= step, pltpu.roll(gc, step, 0), 0.0)
        step *= 2
    gc3 = gc.reshape(n_chunks, CHUNK, LANES)
    g_last = jnp.broadcast_to(gc3[:, CHUNK - 1:CHUNK, :], gc3.shape).reshape(tc, LANES)
    gate_ref[0] = beta
    gate_ref[1] = gc
    gate_ref[2] = jnp.exp(gc)
    gate_ref[3] = jnp.exp(g_last - gc)
    gate_ref[4] = jnp.exp(g_last)
    for i in range(tc // LANES):
        t = gc[i * LANES:(i + 1) * LANES, :].T
        for half in range(LANES // CHUNK):
            gct_ref[i * (LANES // CHUNK) + half] = t[:, half * CHUNK:(half + 1) * CHUNK]

    ci = lax.broadcasted_iota(I32, (CHUNK, CHUNK), 0)
    si = lax.broadcasted_iota(I32, (CHUNK, CHUNK), 1)
    wl = lax.broadcasted_iota(I32, (CHUNK, 2 * CHUNK), 1)
    wr = lax.broadcasted_iota(I32, (CHUNK, 2 * CHUNK), 0)
    right = wl >= CHUNK
    eye_right = jnp.where(wl == wr + CHUNK, 1.0, 0.0)

    def prep_body(cg, carry):
        units = [(cg * prep_unroll + cc, h) for cc in range(prep_unroll) for h in range(B_HEADS)]
        rows = [pl.ds(pl.multiple_of(c * CHUNK, CHUNK), CHUNK) for c, _ in units]
        hsl = [slice(h * LANES, (h + 1) * LANES) for _, h in units]
        idx = range(len(units))

        def col(i, gate, off):
            h = units[i][1]
            return gate_ref[gate, rows[i], off + h:off + h + 1]

        def seg(i, col0):
            h = units[i][1]
            return gdn_ref[rows[i], col0 + h * LANES:col0 + (h + 1) * LANES]

        q = [seg(i, q_col) for i in idx]
        k = [seg(i, k_col) for i in idx]
        kb = [k[i] * col(i, 0, SM_BETA) for i in idx]
        kq = [_dot_nt(jnp.concatenate([kb[i], q[i]], axis=0).astype(BF16), k[i].astype(BF16))
              for i in idx]
        decay = []
        for i, (c, h) in enumerate(units):
            d = col(i, 1, SM_DECAY) - gct_ref[c][SM_DECAY + h:SM_DECAY + h + 1, :]
            decay.append(jnp.where(ci >= si, jnp.exp(jnp.where(ci >= si, d, 0.0)), 0.0))
        wmat = []
        for i, (c, h) in enumerate(units):
            n_mat = jnp.where(ci > si, -(kq[i][0:CHUNK] * decay[i]), 0.0)
            wmat.append(jnp.concatenate([n_mat, jnp.zeros_like(n_mat)], axis=1) + eye_right)
        pw = 1
        while pw < CHUNK:
            wb = [wmat[i].astype(BF16) for i in idx]
            wmat = [_dot(wb[i][:, 0:CHUNK], wb[i]) + jnp.where(right, wmat[i], 0.0) for i in idx]
            pw *= 2
        eg = [col(i, 2, SM_DECAY) for i in idx]
        rhs = [jnp.concatenate([seg(i, v_col) * col(i, 0, SM_BETA), kb[i] * eg[i]],
                               axis=1).astype(BF16) for i in idx]
        sol = [_dot(wmat[i][:, CHUNK:].astype(BF16), rhs[i]).astype(BF16) for i in idx]
        att = [(kq[i][CHUNK:] * decay[i]).astype(BF16) for i in idx]
        k_tail_t = [(k[i] * col(i, 3, SM_DECAY)).T.astype(BF16) for i in idx]
        a_uw = [_dot(att[i], sol[i]) for i in idx]
        k_uw = [_dot(k_tail_t[i], sol[i]) for i in idx]
        for i, (c, h) in enumerate(units):
            oc_ref[rows[i], hsl[i]] = a_uw[i][:, 0:B_HEAD_DIM]
            lhs_ref[c, h, 0:CHUNK, :] = (q[i] * eg[i] - a_uw[i][:, B_HEAD_DIM:]).astype(BF16)
            lhs_ref[c, h, CHUNK:, :] = k_uw[i][:, B_HEAD_DIM:].astype(BF16)
            bm_ref[c, h] = k_uw[i][:, 0:B_HEAD_DIM]
        return carry

    lax.fori_loop(0, n_chunks // prep_unroll, prep_body, 0)

    def scan_body(c, carry):
        r0 = pl.multiple_of(c * CHUNK, CHUNK)
        rows = pl.ds(r0, CHUNK)
        heads = range(B_HEADS)
        hsl = [slice(h * LANES, (h + 1) * LANES) for h in heads]
        s_prev = [state_ref[h] for h in heads]
        r = [_dot(lhs_ref[c, h], s_prev[h].astype(BF16)) for h in heads]
        for h in heads:
            gl = gate_ref[4, rows, SM_DECAY + h:SM_DECAY + h + 1][0:1, :]
            oc_ref[rows, hsl[h]] = oc_ref[rows, hsl[h]] + r[h][0:CHUNK]
            state_ref[h] = s_prev[h] * gl + bm_ref[c, h] - r[h][CHUNK:]
        return carry

    lax.fori_loop(0, n_chunks, scan_body, 0)

    for h in range(B_HEADS):
        hs = slice(h * LANES, (h + 1) * LANES)
        z_gate = gdn_ref[:, z_col + h * LANES:z_col + (h + 1) * LANES]
        o_ref[:, hs] = (_rms(oc_ref[:, hs], ng_ref[...]) * z_gate).astype(BF16)


def _gdn(gdn_in, sm, alog_row, dtb_row, norm_g, bsz, seq, tc):
    nt = seq // tc
    trow = lambda b, t: (b * nt + t, 0)
    const = lambda b, t: (0, 0)
    n_chunks = tc // CHUNK
    return pl.pallas_call(
        functools.partial(_gdn_body, tc=tc),
        grid=(bsz, nt),
        in_specs=[
            pl.BlockSpec((tc, GDN_WIDTH), trow),
            pl.BlockSpec((tc, LANES), trow),
            pl.BlockSpec((1, LANES), const),
            pl.BlockSpec((1, LANES), const),
            pl.BlockSpec((1, B_HEAD_DIM), const),
        ],
        out_specs=pl.BlockSpec((tc, B_WIDTH), trow),
        out_shape=jax.ShapeDtypeStruct((bsz * seq, B_WIDTH), BF16),
        scratch_shapes=[
            pltpu.VMEM((5, tc, LANES), F32),
            pltpu.VMEM((n_chunks, LANES, CHUNK), F32),
            pltpu.VMEM((B_HEADS, B_HEAD_DIM, B_HEAD_DIM), F32),
            pltpu.VMEM((tc, B_WIDTH), F32),
            pltpu.VMEM((n_chunks, B_HEADS, CHUNK + B_HEAD_DIM, B_HEAD_DIM), BF16),
            pltpu.VMEM((n_chunks, B_HEADS, B_HEAD_DIM, B_HEAD_DIM), F32),
        ],
        compiler_params=pltpu.CompilerParams(
            dimension_semantics=("arbitrary", "arbitrary"), vmem_limit_bytes=VMEM_LIMIT),
        name="gdn",
    )(gdn_in, sm, alog_row, dtb_row, norm_g)


def _ffn_body(x_ref, oa_ref, ob_ref, wo_ref, g2_ref, w1_ref, w2_ref, g3_ref, out_ref, *,
              final_norm):
    y = (x_ref[...] + _dot(oa_ref[...], wo_ref[0:A_WIDTH, :])
         + _dot(ob_ref[...], wo_ref[A_WIDTH:, :]))
    h = _rms(y, g2_ref[...]).astype(BF16)
    a = jnp.square(jnp.maximum(_dot(h, w1_ref[...]), 0.0)).astype(BF16)
    acc = y + _dot(a, w2_ref[...])
    out_ref[...] = _rms(acc, g3_ref[...]) if final_norm else acc


def _ffn(x2, oa, ob, wo, g2, w1, w2, g3, tm, final_norm):
    m = x2.shape[0]
    row = lambda i: (i, 0)
    const = lambda i: (0, 0)
    resident = functools.partial(pl.BlockSpec, index_map=const, pipeline_mode=pl.Buffered(1))
    return pl.pallas_call(
        functools.partial(_ffn_body, final_norm=final_norm),
        grid=(m // tm,),
        in_specs=[
            pl.BlockSpec((tm, D_MODEL), row),
            pl.BlockSpec((tm, A_WIDTH), row),
            pl.BlockSpec((tm, B_WIDTH), row),
            resident((D_MODEL, D_MODEL)),
            pl.BlockSpec((1, D_MODEL), const),
            resident((D_MODEL, D_FF)),
            resident((D_FF, D_MODEL)),
            pl.BlockSpec((1, D_MODEL), const),
        ],
        out_specs=pl.BlockSpec((tm, D_MODEL), row),
        out_shape=jax.ShapeDtypeStruct((m, D_MODEL), F32),
        compiler_params=pltpu.CompilerParams(
            dimension_semantics=("arbitrary",), vmem_limit_bytes=VMEM_LIMIT),
        name="ffn",
    )(x2, oa, ob, wo, g2, w1, w2, g3)


def _rope_tables(seq):
    half = A_HEAD_DIM // 2
    inv_freq = 1.0 / (ROPE_THETA ** (jnp.arange(half, dtype=F32) / half))
    ang = jnp.arange(seq).astype(F32)[:, None] * inv_freq[None, :]
    cos = jnp.cos(ang)
    sin = jnp.sin(ang)
    reps = LANES // A_HEAD_DIM
    return (jnp.tile(cos, (1, 2 * reps)),
            jnp.tile(jnp.concatenate([-sin, sin], axis=1), (1, reps)))


def _lane_row(vals, offset):
    return jnp.zeros((1, LANES), F32).at[0, offset:offset + vals.shape[0]].set(vals.astype(F32))


def kernel(x, norm_mix_g, w_in, conv_w, a_log, dt_bias, gdn_norm_g, w_out,
           norm_ffn_g, w_ff1, w_ff2, norm_final_g):
    bsz, seq, d = x.shape
    depth = w_in.shape[0]
    m = bsz * seq
    cos_t, sin_t = _rope_tables(seq)
    x2 = x.reshape(m, d)
    for l in range(depth):
        qa, qi, kv, gdn_in, sm = _inproj(
            x2, norm_mix_g[l][None, :], jnp.swapaxes(w_in, 1, 2), l, cos_t, sin_t, conv_w[l],
            seq, tm=512)
        o_a = _mixer_a(qa, qi, kv, sm, bsz, seq)
        o_b = _gdn(gdn_in, sm, _lane_row(a_log[l], SM_DECAY),
                   _lane_row(dt_bias[l], SM_DECAY), gdn_norm_g[l][None, :], bsz, seq, tc=512)
        x2 = _ffn(x2, o_a, o_b, w_out[l].astype(BF16), norm_ffn_g[l][None, :],
                  w_ff1[l].astype(BF16), w_ff2[l].astype(BF16), norm_final_g[None, :],
                  tm=512, final_norm=(l == depth - 1))
    return x2.reshape(bsz, seq, d)
```

```python
import functools

import jax
import jax.numpy as jnp
from jax import lax
from jax.experimental import pallas as pl
from jax.experimental.pallas import tpu as pltpu

F32 = jnp.float32
BF16 = jnp.bfloat16
I32 = jnp.int32

D_MODEL = 1024
CHUNK = 64
A_QUERIES = 256
ROPE_THETA = 10000.0
EPS = 1e-6
A_HEADS = 8
A_KV_HEADS = 2
A_HEAD_DIM = 64
IDX_HEADS = 8
IDX_DIM = 64
TOPK_MAX = 256
B_HEADS = 4
B_HEAD_DIM = 128
CONV_WIDTH = 4
D_FF = 4 * D_MODEL

LANES = 128
A_WIDTH = A_HEADS * A_HEAD_DIM
KV_WIDTH = A_KV_HEADS * A_HEAD_DIM
B_WIDTH = B_HEADS * B_HEAD_DIM
GDN_WIDTH = 4 * B_WIDTH
SM_KI = 0
SM_WI = IDX_DIM
SM_BETA = SM_WI + IDX_HEADS
SM_DECAY = SM_BETA + B_HEADS
C_QA = 0
C_QI = C_QA + A_WIDTH
C_KV = C_QI + A_WIDTH
C_GDN = C_KV + 2 * KV_WIDTH
C_SM = C_GDN + GDN_WIDTH
IN_COLS = C_SM + LANES
REF_SIZES = (A_WIDTH, KV_WIDTH, KV_WIDTH, IDX_HEADS * IDX_DIM, IDX_DIM, IDX_HEADS,
             B_WIDTH, B_WIDTH, B_WIDTH, B_WIDTH, B_HEADS, B_HEADS)
REF_OFF = tuple(sum(REF_SIZES[:i]) for i in range(len(REF_SIZES)))
IN_DIM = sum(REF_SIZES)
IN_WEIGHT_MOVES = (
    (C_QA, REF_OFF[0], A_WIDTH),
    (C_QI, REF_OFF[3], A_WIDTH),
    (C_KV, REF_OFF[1], 2 * KV_WIDTH),
    (C_GDN, REF_OFF[6], GDN_WIDTH),
    (C_SM + SM_KI, REF_OFF[4], IDX_DIM + IDX_HEADS),
    (C_SM + SM_BETA, REF_OFF[10], 2 * B_HEADS),
)
IN_PAD = IN_COLS - (C_SM + SM_DECAY + B_HEADS)

VMEM_LIMIT = 56 * 1024 * 1024
GDN_SEQS = 2
CONV_PAD = 8
COUNT_GROUP = 4
FINE_STEPS = 17
FINE_CHECK = 6
COUNT_ROWS = 32
V_ROWS = A_HEAD_DIM + 16
ATT_HEADS = 2
ATT_WAVE = 4
KEY_BLOCK = 2 * LANES
NEG_BIG = -1e30
BOUND_SLACK = 1.001
MIN_WEIGHT_SUM = 2.0 ** -60
LOG2_E = 1.4426950408889634
NEG_INF_KEY = -(2 ** 31) + 0x7FFFFF
POS_INF_KEY = 0x7F800000


def _rms(x, g):
    return x * lax.rsqrt(jnp.mean(x * x, axis=-1, keepdims=True) + EPS) * g


def _dot(a, b):
    return jnp.dot(a, b, preferred_element_type=F32)


def _dot_nt(a, b):
    return lax.dot_general(a, b, (((1,), (1,)), ((), ())), preferred_element_type=F32)


def _inproj_body(x_ref, g_ref, win_ref, cos_ref, sin_ref, cw_ref,
                 qa_ref, qi_ref, kv_ref, gdn_ref, sm_ref, xpad_ref, w_ref, *, tiles_per_seq):
    @pl.when(pl.program_id(0) == 0)
    def _():
        step = 256
        for dst, src, width in IN_WEIGHT_MOVES[:-2]:
            for r0 in range(0, width, step):
                w_ref[dst + r0:dst + r0 + step, :] = win_ref[src + r0:src + r0 + step, :].astype(BF16)
        small = [win_ref[src:src + width, :] for _, src, width in IN_WEIGHT_MOVES[-2:]]
        small.append(jnp.zeros((IN_PAD, D_MODEL), F32))
        w_ref[C_SM:IN_COLS, :] = jnp.concatenate(small, axis=0).astype(BF16)

    h = _rms(x_ref[...], g_ref[...]).astype(BF16)
    cos = cos_ref[...]
    sin = sin_ref[...]
    lane = lax.broadcasted_iota(I32, cos.shape, 1)
    first_half = (lane & (A_HEAD_DIM - 1)) < A_HEAD_DIM // 2

    def rope(t):
        swapped = jnp.where(first_half, pltpu.roll(t, LANES - A_HEAD_DIM // 2, 1),
                            pltpu.roll(t, A_HEAD_DIM // 2, 1))
        return t * cos + swapped * sin

    def proj(c0, width):
        return _dot_nt(h, w_ref[c0:c0 + width, :])

    tm = x_ref.shape[0]
    conv_cols = 3 * B_WIDTH

    @pl.when(pl.program_id(0) % tiles_per_seq == 0)
    def _():
        xpad_ref[0:CONV_PAD, :] = jnp.zeros((CONV_PAD, conv_cols), F32)

    @pl.when(pl.program_id(0) % tiles_per_seq != 0)
    def _():
        xpad_ref[0:CONV_PAD, :] = xpad_ref[tm:tm + CONV_PAD, :]

    def stage_conv_input(seg):
        xpad_ref[CONV_PAD:CONV_PAD + tm, seg * B_WIDTH:(seg + 1) * B_WIDTH] = proj(
            C_GDN + seg * B_WIDTH, B_WIDTH)

    def conv_segment(seg):
        for hh in range(B_HEADS):
            cs = slice(seg * B_WIDTH + hh * LANES, seg * B_WIDTH + (hh + 1) * LANES)
            xa = xpad_ref[:, cs]
            y = cw_ref[0:1, cs] * xa
            for jj in range(1, CONV_WIDTH):
                y = cw_ref[jj:jj + 1, cs] * xa + pltpu.roll(y, 1, 0)
            y = y[CONV_PAD:, :]
            y = y * jax.nn.sigmoid(y)
            if seg < 2:
                y = y * lax.rsqrt(jnp.sum(y * y, axis=-1, keepdims=True) + EPS)
            if seg == 0:
                y = y * (B_HEAD_DIM ** -0.5)
            gdn_ref[:, cs] = y

    def store_queries_transposed(out_ref, acc, scale):
        for j in range(A_WIDTH // LANES):
            t = rope(acc[:, j * LANES:(j + 1) * LANES])
            if scale != 1.0:
                t = t * scale
            for b in range(tm // A_QUERIES):
                tt = t[b * A_QUERIES:(b + 1) * A_QUERIES, :].T
                for hh in range(LANES // A_HEAD_DIM):
                    h0 = (j * (LANES // A_HEAD_DIM) + hh) * A_QUERIES
                    out_ref[b, :, h0:h0 + A_QUERIES] = (
                        tt[hh * A_HEAD_DIM:(hh + 1) * A_HEAD_DIM, :].astype(BF16))

    stage_conv_input(0)
    acc = proj(C_QA, A_WIDTH)
    conv_segment(0)
    store_queries_transposed(qa_ref, acc, A_HEAD_DIM ** -0.5 * LOG2_E)
    stage_conv_input(1)
    acc = proj(C_QI, A_WIDTH)
    conv_segment(1)
    store_queries_transposed(qi_ref, acc, 1.0)
    stage_conv_input(2)
    acc = proj(C_KV, 2 * KV_WIDTH)
    acc_sm = proj(C_SM, LANES)
    z = proj(C_GDN + conv_cols, B_WIDTH)
    conv_segment(2)
    kv_ref[:, 0:KV_WIDTH] = rope(acc[:, 0:KV_WIDTH]).astype(BF16)
    kv_ref[:, KV_WIDTH:] = acc[:, KV_WIDTH:].astype(BF16)
    sm_ref[...] = jnp.where(lane < IDX_DIM, rope(acc_sm), acc_sm)
    gdn_ref[:, conv_cols:] = z * jax.nn.sigmoid(z)


def _inproj(x2, g, w, layer, cos_t, sin_t, conv_w, seq, tm):
    m = x2.shape[0]
    assert seq % tm == 0 and tm % A_QUERIES == 0, (seq, tm)
    nt = seq // tm
    row = lambda i: (i, 0)
    const = lambda i: (0, 0)
    return pl.pallas_call(
        functools.partial(_inproj_body, tiles_per_seq=nt),
        grid=(m // tm,),
        in_specs=[
            pl.BlockSpec((tm, D_MODEL), row),
            pl.BlockSpec((1, D_MODEL), const),
            pl.BlockSpec((None, IN_DIM, D_MODEL), lambda i: (layer, 0, 0),
                         pipeline_mode=pl.Buffered(1)),
            pl.BlockSpec((tm, LANES), lambda i: (i % nt, 0)),
            pl.BlockSpec((tm, LANES), lambda i: (i % nt, 0)),
            pl.BlockSpec((CONV_WIDTH, 3 * B_WIDTH), const),
        ],
        out_specs=[
            pl.BlockSpec((tm // A_QUERIES, A_HEAD_DIM, A_HEADS * A_QUERIES), lambda i: (i, 0, 0)),
            pl.BlockSpec((tm // A_QUERIES, IDX_DIM, IDX_HEADS * A_QUERIES), lambda i: (i, 0, 0)),
            pl.BlockSpec((tm, 2 * KV_WIDTH), row),
            pl.BlockSpec((tm, GDN_WIDTH), row),
            pl.BlockSpec((tm, LANES), row),
        ],
        out_shape=[
            jax.ShapeDtypeStruct((m // A_QUERIES, A_HEAD_DIM, A_HEADS * A_QUERIES), BF16),
            jax.ShapeDtypeStruct((m // A_QUERIES, IDX_DIM, IDX_HEADS * A_QUERIES), BF16),
            jax.ShapeDtypeStruct((m, 2 * KV_WIDTH), BF16),
            jax.ShapeDtypeStruct((m, GDN_WIDTH), F32),
            jax.ShapeDtypeStruct((m, LANES), F32),
        ],
        scratch_shapes=[
            pltpu.VMEM((tm + CONV_PAD, 3 * B_WIDTH), F32),
            pltpu.VMEM((IN_COLS, D_MODEL), BF16),
        ],
        compiler_params=pltpu.CompilerParams(
            dimension_semantics=("arbitrary",), vmem_limit_bytes=VMEM_LIMIT),
        name="inproj",
    )(x2, g, w, cos_t, sin_t, conv_w)


def _mixer_a_body(qat_ref, qit_ref, kv_ref, sm_ref, o_ref,
                  isc_ref, isb_ref, vt_ref, acc_ref, bias_ref, knorm_ref, *, seq, topk):
    j = pl.program_id(1)
    nkt = (j + 1) * (A_QUERIES // LANES)
    n_heads_pair = A_WIDTH // LANES
    group = A_HEADS // A_KV_HEADS

    @pl.when(j == 0)
    def _():
        per_block = KEY_BLOCK // LANES
        for t in range(seq // LANES):
            vt = kv_ref[t * LANES:(t + 1) * LANES, KV_WIDTH:].astype(F32).T.astype(BF16)
            c0 = (t % per_block) * LANES
            for g in range(A_KV_HEADS):
                vt_ref[t // per_block, g * V_ROWS:g * V_ROWS + A_HEAD_DIM, c0:c0 + LANES] = (
                    vt[g * A_HEAD_DIM:(g + 1) * A_HEAD_DIM])
        for g in range(A_KV_HEADS):
            vt_ref[:, g * V_ROWS + A_HEAD_DIM:(g + 1) * V_ROWS, :] = jnp.ones(
                (seq // KEY_BLOCK, V_ROWS - A_HEAD_DIM, KEY_BLOCK), BF16)
            kg = kv_ref[:, g * A_HEAD_DIM:(g + 1) * A_HEAD_DIM].astype(F32)
            k_norm = jnp.sqrt(jnp.max(jnp.sum(kg * kg, axis=1, keepdims=True), axis=0, keepdims=True))
            knorm_ref[g:g + 1, :] = jnp.broadcast_to(k_norm, (1, LANES))

    q0 = pl.multiple_of(j * A_QUERIES, A_QUERIES)
    w_t = sm_ref[pl.ds(q0, A_QUERIES), :].T[SM_WI:SM_WI + IDX_HEADS, :]
    w_t = w_t * ((IDX_HEADS ** -0.5) * (IDX_DIM ** -0.5))

    qlane = lax.broadcasted_iota(I32, (1, A_QUERIES), 1)
    limit = q0 + (lax.shift_right_logical(qlane, CHUNK.bit_length() - 1) + 1) * CHUNK
    tile_iota = lax.broadcasted_iota(I32, (LANES, A_QUERIES), 0)
    block_iota = lax.broadcasted_iota(I32, (KEY_BLOCK, A_QUERIES), 0)
    nkb = lax.shift_right_logical(nkt + (KEY_BLOCK // LANES - 1), (KEY_BLOCK // LANES).bit_length() - 1)

    def isc_body(kb, carry):
        r0 = pl.multiple_of(kb * KEY_BLOCK, KEY_BLOCK)
        kid = sm_ref[pl.ds(r0, KEY_BLOCK), SM_KI:SM_KI + IDX_DIM].astype(BF16)
        rel = _dot(kid, qit_ref[...])
        acc = jnp.zeros((KEY_BLOCK, A_QUERIES), F32)
        for h in range(IDX_HEADS):
            acc = acc + w_t[h:h + 1, :] * jnp.maximum(rel[:, h * A_QUERIES:(h + 1) * A_QUERIES], 0.0)
        acc = jnp.where(r0 + block_iota < limit, acc, -jnp.inf)
        isc_ref[pl.ds(r0, KEY_BLOCK), :] = acc
        rounded = acc.astype(BF16)
        isb_ref[pl.ds(r0, KEY_BLOCK), :] = rounded
        hit = jnp.where(rounded.astype(F32) >= 0.0, 1.0, 0.0)
        return carry + hit.reshape(KEY_BLOCK // COUNT_ROWS, COUNT_ROWS, A_QUERIES).sum(axis=0)

    n_nonneg = lax.fori_loop(0, nkb, isc_body, jnp.zeros((COUNT_ROWS, A_QUERIES), F32))
    n_nonneg = n_nonneg.sum(axis=0, keepdims=True)

    ngrp = lax.shift_right_logical(nkt + (COUNT_GROUP - 1), COUNT_GROUP.bit_length() - 1)

    def fill_body(kt, carry):
        r0 = pl.multiple_of(kt * LANES, LANES)
        isc_ref[pl.ds(r0, LANES), :] = jnp.full((LANES, A_QUERIES), -jnp.inf, F32)
        isb_ref[pl.ds(r0, LANES), :] = jnp.full((LANES, A_QUERIES), -jnp.inf, BF16)
        return carry

    lax.fori_loop(nkb * (KEY_BLOCK // LANES), ngrp * COUNT_GROUP, fill_body, 0)

    def count(pred):
        def body(g, acc):
            for t in range(COUNT_GROUP):
                r0 = pl.multiple_of((g * COUNT_GROUP + t) * LANES, LANES)
                hit = jnp.where(pred(isc_ref[pl.ds(r0, LANES), :], r0 + tile_iota), 1.0, 0.0)
                acc = acc + hit.reshape(LANES // COUNT_ROWS, COUNT_ROWS, A_QUERIES).sum(axis=0)
            return acc
        acc = lax.fori_loop(0, ngrp, body, jnp.zeros((COUNT_ROWS, A_QUERIES), F32))
        return acc.sum(axis=0, keepdims=True)

    def count_coarse(thr):
        def body(g, acc):
            for t in range(COUNT_GROUP):
                r0 = pl.multiple_of((g * COUNT_GROUP + t) * LANES, LANES)
                hit = jnp.where(isb_ref[pl.ds(r0, LANES), :] >= thr, one_b, zero_b)
                hit = hit.reshape(LANES // COUNT_ROWS, COUNT_ROWS, A_QUERIES)
                part = hit[0]
                for i in range(1, LANES // COUNT_ROWS):
                    part = part + hit[i]
                acc = acc + part
            return acc
        acc = lax.fori_loop(0, ngrp, body, jnp.zeros((COUNT_ROWS, A_QUERIES), BF16))
        return acc.astype(F32).sum(axis=0, keepdims=True)

    kf = float(topk)
    one_b = jnp.ones((), BF16)
    zero_b = jnp.zeros((), BF16)

    def thr_coarse(key):
        bits = jnp.where(key >= 0, key, key ^ jnp.int32(0x7FFF))
        return lax.bitcast_convert_type(lax.shift_left(bits, 16), F32).astype(BF16)

    lo16 = jnp.where(n_nonneg >= kf, jnp.int32(0), jnp.int32(-2 ** 15))

    def coarse_body(i, lo):
        trial = lo | lax.shift_left(jnp.int32(1), 14 - i)
        c = count_coarse(thr_coarse(trial))
        return jnp.where(c >= kf, trial, lo)

    lo16 = lax.fori_loop(0, 15, coarse_body, lo16)
    lo16 = jnp.maximum(lo16, jnp.int32(NEG_INF_KEY >> 16))
    center = lax.shift_left(lo16, 16) | jnp.where(lo16 < 0, jnp.int32(0xFFFF), jnp.int32(0))

    def thr_of(key):
        bits = jnp.where(key >= 0, key, key ^ jnp.int32(0x7FFFFFFF))
        return jnp.where(key < jnp.int32(NEG_INF_KEY), -jnp.inf, lax.bitcast_convert_type(bits, F32))

    def fine_body(i, st):
        lo, c_lo = st
        trial = lo + lax.shift_left(jnp.int32(1), FINE_STEPS - 1 - i)
        thr = thr_of(trial)
        c = count(lambda x, s: x >= thr)
        ok = c >= kf
        return jnp.where(ok, trial, lo), jnp.where(ok, c, c_lo)

    state = lax.fori_loop(0, FINE_CHECK, fine_body,
                          (center - jnp.int32(1 << (FINE_STEPS - 1)), jnp.full((1, A_QUERIES), -1.0, F32)))
    lo, c_lo = state

    hi = lo + jnp.int32(1 << (FINE_STEPS - FINE_CHECK))
    hi_thr = thr_of(hi)

    def split_at(bound):
        def body(g, st):
            below_max, n_other = st
            for t in range(COUNT_GROUP):
                r0 = pl.multiple_of((g * COUNT_GROUP + t) * LANES, LANES)
                x = isc_ref[pl.ds(r0, LANES), :]
                below = x < bound
                part = jnp.where(below, x, -jnp.inf).reshape(LANES // COUNT_ROWS, COUNT_ROWS, A_QUERIES)
                hit = jnp.where(below, 0.0, 1.0).reshape(LANES // COUNT_ROWS, COUNT_ROWS, A_QUERIES)
                below_max = jnp.maximum(below_max, part.max(axis=0))
                n_other = n_other + hit.sum(axis=0)
            return below_max, n_other
        below_max, n_other = lax.fori_loop(
            0, ngrp, body, (jnp.full((COUNT_ROWS, A_QUERIES), -jnp.inf, F32),
                            jnp.zeros((COUNT_ROWS, A_QUERIES), F32)))
        return below_max.max(axis=0, keepdims=True), n_other.sum(axis=0, keepdims=True)

    v, c_hi = split_at(hi_thr)
    n_v = count(lambda x, s: x == v)
    by_value = ((c_hi + n_v >= kf) & (c_hi < kf)
                & (hi < jnp.int32(POS_INF_KEY)) & (lo >= jnp.int32(NEG_INF_KEY)))
    done = by_value | (c_lo == kf) | (limit <= topk)
    settled = jnp.min(jnp.where(done, 1.0, 0.0)) > 0.0

    def finish_now():
        return (jnp.where(by_value, v, thr_of(lo)), jnp.where(by_value, c_hi + n_v, c_lo),
                jnp.where(by_value, c_hi, -1.0))

    def finish_search():
        lo_f, c_lo_f = lax.fori_loop(FINE_CHECK, FINE_STEPS, fine_body, state)
        kth_f = thr_of(lo_f)
        return kth_f, c_lo_f, count(lambda x, s: x > kth_f)

    kth, n_at, n_above = lax.cond(settled, finish_now, finish_search)

    n_ge = jnp.where(limit <= topk, 0.0, jnp.where(n_at < 0.0, jnp.inf, n_at))
    has_ties = jnp.max(n_ge) > kf

    acc_ref[...] = jnp.zeros_like(acc_ref)

    def select_plain(kb, x, s_idx, carry):
        return (s_idx < limit) & (x >= kth), carry

    def make_select_ties():
        need = jnp.where(n_above < 0.0, kf, kf - n_above)
        tri = jnp.where(lax.broadcasted_iota(I32, (KEY_BLOCK, KEY_BLOCK), 1)
                        <= lax.broadcasted_iota(I32, (KEY_BLOCK, KEY_BLOCK), 0), 1.0, 0.0).astype(BF16)

        def select_ties(kb, x, s_idx, before):
            eq = x == kth
            rank = before + _dot(tri, jnp.where(eq, 1.0, 0.0).astype(BF16))
            sel = (s_idx < limit) & ((x > kth) | (eq & (rank <= need)))
            return sel, rank[KEY_BLOCK - 1:KEY_BLOCK, :]

        return select_ties

    units = list(range(A_HEADS // ATT_HEADS))
    uw = ATT_HEADS * A_QUERIES
    us = [slice(u * uw, (u + 1) * uw) for u in units]
    kd = [slice((u * ATT_HEADS // group) * A_HEAD_DIM, (u * ATT_HEADS // group + 1) * A_HEAD_DIM)
          for u in units]
    vr = [slice((u * ATT_HEADS // group) * V_ROWS, (u * ATT_HEADS // group + 1) * V_ROWS)
          for u in units]
    cs = [slice((u * ATT_HEADS % group) * A_QUERIES, (u * ATT_HEADS % group + ATT_HEADS) * A_QUERIES)
          for u in units]

    def block_inputs(select, kb, sel_carry):
        r0 = pl.multiple_of(kb * KEY_BLOCK, KEY_BLOCK)
        x = isc_ref[pl.ds(r0, KEY_BLOCK), :]
        sel, sel_carry = select(kb, x, r0 + block_iota, sel_carry)
        bias_ref[...] = jnp.where(sel, 0.0, NEG_BIG)
        return kv_ref[pl.ds(r0, KEY_BLOCK), 0:KV_WIDTH], vt_ref[kb], sel_carry

    q_norm = jnp.sqrt(jnp.sum(jnp.square(qat_ref[...].astype(F32)), axis=0, keepdims=True))
    shift = [q_norm[:, us[u]] * (knorm_ref[u * ATT_HEADS // group:u * ATT_HEADS // group + 1, 0:1]
                                  * BOUND_SLACK) for u in units]

    def fast_body(select, kb, sel_carry):
        k_tile, v_t, sel_carry = block_inputs(select, kb, sel_carry)
        for w0 in range(0, len(units), ATT_WAVE):
            wave = units[w0:w0 + ATT_WAVE]
            p = {u: jnp.exp2(_dot(k_tile[:, kd[u]], qat_ref[:, us[u]])
                             + jnp.concatenate([bias_ref[...]] * ATT_HEADS, axis=1)
                             - shift[u]).astype(BF16) for u in wave}
            pv = {u: _dot(v_t[vr[u], :], p[u]) for u in wave}
            for u in wave:
                acc_ref[vr[u], cs[u]] += pv[u]
        return sel_carry

    def att_body(select, kb, state):
        m_prev, sel_carry = state
        k_tile, v_t, sel_carry = block_inputs(select, kb, sel_carry)
        m_new = {}
        for w0 in range(0, len(units), ATT_WAVE):
            wave = units[w0:w0 + ATT_WAVE]
            s = {u: _dot(k_tile[:, kd[u]], qat_ref[:, us[u]])
                 + jnp.concatenate([bias_ref[...]] * ATT_HEADS, axis=1) for u in wave}
            for u in wave:
                m_new[u] = jnp.maximum(m_prev[:, us[u]], jnp.max(s[u], axis=0, keepdims=True))
            alpha = {u: jnp.exp2(m_prev[:, us[u]] - m_new[u]) for u in wave}
            p = {u: jnp.exp2(s[u] - m_new[u]).astype(BF16) for u in wave}
            pv = {u: _dot(v_t[vr[u], :], p[u]) for u in wave}
            for u in wave:
                acc_ref[vr[u], cs[u]] = alpha[u] * acc_ref[vr[u], cs[u]] + pv[u]
        return jnp.concatenate([m_new[u] for u in units], axis=1), sel_carry

    def attend(select):
        no_ties_seen = jnp.zeros((1, A_QUERIES), F32)
        lax.fori_loop(0, nkb, functools.partial(fast_body, select), no_ties_seen)
        sums = jnp.concatenate([acc_ref[g * V_ROWS + A_HEAD_DIM:g * V_ROWS + A_HEAD_DIM + 1, :]
                                for g in range(A_KV_HEADS)], axis=1)

        @pl.when(jnp.logical_not(jnp.min(sums) >= MIN_WEIGHT_SUM))
        def _():
            acc_ref[...] = jnp.zeros_like(acc_ref)
            init = (jnp.full((1, A_HEADS * A_QUERIES), NEG_BIG, F32), no_ties_seen)
            lax.fori_loop(0, nkb, functools.partial(att_body, select), init)

    lax.cond(has_ties, lambda: attend(make_select_ties()), lambda: attend(select_plain))

    for p in range(n_heads_pair):
        g = (2 * p) // group
        parts = []
        for h in (2 * p, 2 * p + 1):
            hq = slice((h % group) * A_QUERIES, (h % group + 1) * A_QUERIES)
            parts.append(acc_ref[g * V_ROWS:g * V_ROWS + A_HEAD_DIM, hq]
                         / acc_ref[g * V_ROWS + A_HEAD_DIM:g * V_ROWS + A_HEAD_DIM + 1, hq])
        o_ref[:, p * LANES:(p + 1) * LANES] = jnp.concatenate(parts, axis=0).T.astype(BF16)


def _mixer_a(qa, qi, kv, sm, bsz, seq):
    nq = seq // A_QUERIES
    assert seq % (COUNT_GROUP * LANES) == 0, seq
    assert seq // COUNT_ROWS <= 256, seq
    topk = min(TOPK_MAX, seq // 4)
    qrow = lambda b, j: (b * nq + j, 0)
    qblock = lambda b, j: (b * nq + j, 0, 0)
    brow = lambda b, j: (b, 0)
    return pl.pallas_call(
        functools.partial(_mixer_a_body, seq=seq, topk=topk),
        grid=(bsz, nq),
        in_specs=[
            pl.BlockSpec((None, A_HEAD_DIM, A_HEADS * A_QUERIES), qblock),
            pl.BlockSpec((None, IDX_DIM, IDX_HEADS * A_QUERIES), qblock),
            pl.BlockSpec((seq, 2 * KV_WIDTH), brow),
            pl.BlockSpec((seq, LANES), brow),
        ],
        out_specs=pl.BlockSpec((A_QUERIES, A_WIDTH), qrow),
        out_shape=jax.ShapeDtypeStruct((bsz * seq, A_WIDTH), BF16),
        scratch_shapes=[
            pltpu.VMEM((seq, A_QUERIES), F32),
            pltpu.VMEM((seq, A_QUERIES), BF16),
            pltpu.VMEM((seq // KEY_BLOCK, A_KV_HEADS * V_ROWS, KEY_BLOCK), BF16),
            pltpu.VMEM((A_KV_HEADS * V_ROWS, (A_HEADS // A_KV_HEADS) * A_QUERIES), F32),
            pltpu.VMEM((KEY_BLOCK, A_QUERIES), F32),
            pltpu.VMEM((8, LANES), F32),
        ],
        compiler_params=pltpu.CompilerParams(
            dimension_semantics=("arbitrary", "arbitrary"), vmem_limit_bytes=VMEM_LIMIT),
        name="mixer_a",
    )(qa, qi, kv, sm)


def _gdn_prepare(gdn_ref, sm_ref, alog_ref, dtb_ref, gate_ref, gct_ref, oc_ref, lhs_ref, bm_ref, *, tc):
    n_chunks = tc // CHUNK
    prep_unroll = 8
    q_col, k_col, v_col = (i * B_WIDTH for i in range(3))

    sm = sm_ref[...]
    beta = jax.nn.sigmoid(sm)
    z = sm + dtb_ref[...]
    softplus = jnp.maximum(z, 0.0) + jnp.log(1.0 + jnp.exp(-jnp.abs(z)))
    g = -jnp.exp(alog_ref[...]) * softplus
    rin = lax.broadcasted_iota(I32, (tc, LANES), 0) & (CHUNK - 1)
    gc = g
    step = 1
    while step < CHUNK:
        gc = gc + jnp.where(rin >= step, pltpu.roll(gc, step, 0), 0.0)
        step *= 2
    gc3 = gc.reshape(n_chunks, CHUNK, LANES)
    g_last = jnp.broadcast_to(gc3[:, CHUNK - 1:CHUNK, :], gc3.shape).reshape(tc, LANES)
    gate_ref[0] = beta
    gate_ref[1] = gc
    gate_ref[2] = jnp.exp(gc)
    gate_ref[3] = jnp.exp(g_last - gc)
    gate_ref[4] = jnp.exp(g_last)
    for i in range(tc // LANES):
        t = gc[i * LANES:(i + 1) * LANES, :].T
        for half in range(LANES // CHUNK):
            gct_ref[i * (LANES // CHUNK) + half] = t[:, half * CHUNK:(half + 1) * CHUNK]

    ci = lax.broadcasted_iota(I32, (CHUNK, CHUNK), 0)
    si = lax.broadcasted_iota(I32, (CHUNK, CHUNK), 1)
    wl = lax.broadcasted_iota(I32, (CHUNK, 2 * CHUNK), 1)
    wr = lax.broadcasted_iota(I32, (CHUNK, 2 * CHUNK), 0)
    right = wl >= CHUNK
    eye_right = jnp.where(wl == wr + CHUNK, 1.0, 0.0)

    def prep_body(cg, carry):
        units = [(cg * prep_unroll + cc, h) for cc in range(prep_unroll) for h in range(B_HEADS)]
        rows = [pl.ds(pl.multiple_of(c * CHUNK, CHUNK), CHUNK) for c, _ in units]
        hsl = [slice(h * LANES, (h + 1) * LANES) for _, h in units]
        idx = range(len(units))

        def col(i, gate, off):
            h = units[i][1]
            return gate_ref[gate, rows[i], off + h:off + h + 1]

        def seg(i, col0):
            h = units[i][1]
            return gdn_ref[rows[i], col0 + h * LANES:col0 + (h + 1) * LANES]

        q = [seg(i, q_col) for i in idx]
        k = [seg(i, k_col) for i in idx]
        kb = [k[i] * col(i, 0, SM_BETA) for i in idx]
        kq = [_dot_nt(jnp.concatenate([kb[i], q[i]], axis=0).astype(BF16), k[i].astype(BF16))
              for i in idx]
        decay = []
        for i, (c, h) in enumerate(units):
            d = col(i, 1, SM_DECAY) - gct_ref[c][SM_DECAY + h:SM_DECAY + h + 1, :]
            decay.append(jnp.where(ci >= si, jnp.exp(jnp.where(ci >= si, d, 0.0)), 0.0))
        wmat = []
        for i, (c, h) in enumerate(units):
            n_mat = jnp.where(ci > si, -(kq[i][0:CHUNK] * decay[i]), 0.0)
            wmat.append(jnp.concatenate([n_mat, jnp.zeros_like(n_mat)], axis=1) + eye_right)
        pw = 1
        while pw < CHUNK:
            wb = [wmat[i].astype(BF16) for i in idx]
            wmat = [_dot(wb[i][:, 0:CHUNK], wb[i]) + jnp.where(right, wmat[i], 0.0) for i in idx]
            pw *= 2
        eg = [col(i, 2, SM_DECAY) for i in idx]
        rhs = [jnp.concatenate([seg(i, v_col) * col(i, 0, SM_BETA), kb[i] * eg[i]],
                               axis=1).astype(BF16) for i in idx]
        sol = [_dot(wmat[i][:, CHUNK:].astype(BF16), rhs[i]).astype(BF16) for i in idx]
        att = [(kq[i][CHUNK:] * decay[i]).astype(BF16) for i in idx]
        k_tail_t = [(k[i] * col(i, 3, SM_DECAY)).T.astype(BF16) for i in idx]
        a_uw = [_dot(att[i], sol[i]) for i in idx]
        k_uw = [_dot(k_tail_t[i], sol[i]) for i in idx]
        for i, (c, h) in enumerate(units):
            oc_ref[rows[i], hsl[i]] = a_uw[i][:, 0:B_HEAD_DIM]
            lhs_ref[c, h, 0:CHUNK, :] = (q[i] * eg[i] - a_uw[i][:, B_HEAD_DIM:]).astype(BF16)
            lhs_ref[c, h, CHUNK:, :] = k_uw[i][:, B_HEAD_DIM:].astype(BF16)
            bm_ref[c, h] = k_uw[i][:, 0:B_HEAD_DIM]
        return carry

    lax.fori_loop(0, n_chunks // prep_unroll, prep_body, 0)


def _gdn_body(gdn_ref, sm_ref, alog_ref, dtb_ref, ng_ref, o_ref,
              gate_ref, gct_ref, state_ref, oc_ref, lhs_ref, bm_ref, *, tc):
    n_seqs = gdn_ref.shape[0]
    tb = pl.program_id(1)
    n_chunks = tc // CHUNK
    z_col = 3 * B_WIDTH

    @pl.when(tb == 0)
    def _():
        state_ref[...] = jnp.zeros_like(state_ref)

    for s in range(n_seqs):
        _gdn_prepare(gdn_ref.at[s], sm_ref.at[s], alog_ref, dtb_ref, gate_ref.at[s], gct_ref.at[s],
                     oc_ref.at[s], lhs_ref.at[s], bm_ref.at[s], tc=tc)

    chains = [(s, h) for s in range(n_seqs) for h in range(B_HEADS)]

    def scan_body(c, carry):
        r0 = pl.multiple_of(c * CHUNK, CHUNK)
        rows = pl.ds(r0, CHUNK)
        s_prev = [state_ref[s, h] for s, h in chains]
        r = [_dot(lhs_ref[s, c, h], s_prev[i].astype(BF16))
             for i, (s, h) in enumerate(chains)]
        for i, (s, h) in enumerate(chains):
            hs = slice(h * LANES, (h + 1) * LANES)
            gl = gate_ref[s, 4, rows, SM_DECAY + h:SM_DECAY + h + 1][0:1, :]
            oc_ref[s, rows, hs] = oc_ref[s, rows, hs] + r[i][0:CHUNK]
            state_ref[s, h] = s_prev[i] * gl + bm_ref[s, c, h] - r[i][CHUNK:]
        return carry

    lax.fori_loop(0, n_chunks, scan_body, 0)

    for s, h in chains:
        hs = slice(h * LANES, (h + 1) * LANES)
        z_gate = gdn_ref[s, :, z_col + h * LANES:z_col + (h + 1) * LANES]
        o_ref[s, :, hs] = (_rms(oc_ref[s, :, hs], ng_ref[...]) * z_gate).astype(BF16)


def _gdn(gdn_in, sm, alog_row, dtb_row, norm_g, bsz, seq, tc):
    ns = GDN_SEQS if bsz % GDN_SEQS == 0 else 1
    nt = seq // tc
    blk = lambda b, t: (b, t, 0)
    const = lambda b, t: (0, 0)
    n_chunks = tc // CHUNK
    out = pl.pallas_call(
        functools.partial(_gdn_body, tc=tc),
        grid=(bsz // ns, nt),
        in_specs=[
            pl.BlockSpec((ns, tc, GDN_WIDTH), blk),
            pl.BlockSpec((ns, tc, LANES), blk),
            pl.BlockSpec((1, LANES), const),
            pl.BlockSpec((1, LANES), const),
            pl.BlockSpec((1, B_HEAD_DIM), const),
        ],
        out_specs=pl.BlockSpec((ns, tc, B_WIDTH), blk),
        out_shape=jax.ShapeDtypeStruct((bsz, seq, B_WIDTH), BF16),
        scratch_shapes=[
            pltpu.VMEM((ns, 5, tc, LANES), F32),
            pltpu.VMEM((ns, n_chunks, LANES, CHUNK), F32),
            pltpu.VMEM((ns, B_HEADS, B_HEAD_DIM, B_HEAD_DIM), F32),
            pltpu.VMEM((ns, tc, B_WIDTH), F32),
            pltpu.VMEM((ns, n_chunks, B_HEADS, CHUNK + B_HEAD_DIM, B_HEAD_DIM), BF16),
            pltpu.VMEM((ns, n_chunks, B_HEADS, B_HEAD_DIM, B_HEAD_DIM), F32),
        ],
        compiler_params=pltpu.CompilerParams(
            dimension_semantics=("arbitrary", "arbitrary"), vmem_limit_bytes=VMEM_LIMIT),
        name="gdn",
    )(gdn_in.reshape(bsz, seq, GDN_WIDTH), sm.reshape(bsz, seq, LANES), alog_row, dtb_row, norm_g)
    return out.reshape(bsz * seq, B_WIDTH)


def _ffn_body(x_ref, oa_ref, ob_ref, wo_ref, g2_ref, w1_ref, w2_ref, g3_ref, out_ref, *,
              final_norm):
    y = (x_ref[...] + _dot(oa_ref[...], wo_ref[0:A_WIDTH, :])
         + _dot(ob_ref[...], wo_ref[A_WIDTH:, :]))
    h = _rms(y, g2_ref[...]).astype(BF16)
    a = jnp.square(jnp.maximum(_dot(h, w1_ref[...]), 0.0)).astype(BF16)
    acc = y + _dot(a, w2_ref[...])
    out_ref[...] = _rms(acc, g3_ref[...]) if final_norm else acc


def _ffn(x2, oa, ob, wo, g2, w1, w2, g3, tm, final_norm):
    m = x2.shape[0]
    row = lambda i: (i, 0)
    const = lambda i: (0, 0)
    resident = functools.partial(pl.BlockSpec, index_map=const, pipeline_mode=pl.Buffered(1))
    return pl.pallas_call(
        functools.partial(_ffn_body, final_norm=final_norm),
        grid=(m // tm,),
        in_specs=[
            pl.BlockSpec((tm, D_MODEL), row),
            pl.BlockSpec((tm, A_WIDTH), row),
            pl.BlockSpec((tm, B_WIDTH), row),
            resident((D_MODEL, D_MODEL)),
            pl.BlockSpec((1, D_MODEL), const),
            resident((D_MODEL, D_FF)),
            resident((D_FF, D_MODEL)),
            pl.BlockSpec((1, D_MODEL), const),
        ],
        out_specs=pl.BlockSpec((tm, D_MODEL), row),
        out_shape=jax.ShapeDtypeStruct((m, D_MODEL), F32),
        compiler_params=pltpu.CompilerParams(
            dimension_semantics=("arbitrary",), vmem_limit_bytes=VMEM_LIMIT),
        name="ffn",
    )(x2, oa, ob, wo, g2, w1, w2, g3)


def _rope_tables(seq):
    half = A_HEAD_DIM // 2
    inv_freq = 1.0 / (ROPE_THETA ** (jnp.arange(half, dtype=F32) / half))
    ang = jnp.arange(seq).astype(F32)[:, None] * inv_freq[None, :]
    cos = jnp.cos(ang)
    sin = jnp.sin(ang)
    reps = LANES // A_HEAD_DIM
    return (jnp.tile(cos, (1, 2 * reps)),
            jnp.tile(jnp.concatenate([-sin, sin], axis=1), (1, reps)))


def _lane_row(vals, offset):
    return jnp.zeros((1, LANES), F32).at[0, offset:offset + vals.shape[0]].set(vals.astype(F32))


def kernel(x, norm_mix_g, w_in, conv_w, a_log, dt_bias, gdn_norm_g, w_out,
           norm_ffn_g, w_ff1, w_ff2, norm_final_g):
    bsz, seq, d = x.shape
    depth = w_in.shape[0]
    m = bsz * seq
    cos_t, sin_t = _rope_tables(seq)
    x2 = x.reshape(m, d)
    for l in range(depth):
        qa, qi, kv, gdn_in, sm = _inproj(
            x2, norm_mix_g[l][None, :], jnp.swapaxes(w_in, 1, 2), l, cos_t, sin_t, conv_w[l],
            seq, tm=512)
        o_a = _mixer_a(qa, qi, kv, sm, bsz, seq)
        o_b = _gdn(gdn_in, sm, _lane_row(a_log[l], SM_DECAY),
                   _lane_row(dt_bias[l], SM_DECAY), gdn_norm_g[l][None, :], bsz, seq, tc=512)
        x2 = _ffn(x2, o_a, o_b, w_out[l].astype(BF16), norm_ffn_g[l][None, :],
                  w_ff1[l].astype(BF16), w_ff2[l].astype(BF16), norm_final_g[None, :],
                  tm=512, final_norm=(l == depth - 1))
    return x2.reshape(bsz, seq, d)
```

```python
import functools

import jax
import jax.numpy as jnp
from jax import lax
from jax.experimental import pallas as pl
from jax.experimental.pallas import tpu as pltpu

F32 = jnp.float32
BF16 = jnp.bfloat16
I32 = jnp.int32

D_MODEL = 1024
CHUNK = 64
A_QUERIES = 256
ROPE_THETA = 10000.0
EPS = 1e-6
A_HEADS = 8
A_KV_HEADS = 2
A_HEAD_DIM = 64
IDX_HEADS = 8
IDX_DIM = 64
TOPK_MAX = 256
B_HEADS = 4
B_HEAD_DIM = 128
CONV_WIDTH = 4
D_FF = 4 * D_MODEL

LANES = 128
A_WIDTH = A_HEADS * A_HEAD_DIM
KV_WIDTH = A_KV_HEADS * A_HEAD_DIM
B_WIDTH = B_HEADS * B_HEAD_DIM
GDN_WIDTH = 4 * B_WIDTH
SM_KI = 0
SM_WI = IDX_DIM
SM_BETA = SM_WI + IDX_HEADS
SM_DECAY = SM_BETA + B_HEADS
C_QA = 0
C_QI = C_QA + A_WIDTH
C_KV = C_QI + A_WIDTH
C_GDN = C_KV + 2 * KV_WIDTH
C_SM = C_GDN + GDN_WIDTH
IN_COLS = C_SM + LANES
REF_SIZES = (A_WIDTH, KV_WIDTH, KV_WIDTH, IDX_HEADS * IDX_DIM, IDX_DIM, IDX_HEADS,
             B_WIDTH, B_WIDTH, B_WIDTH, B_WIDTH, B_HEADS, B_HEADS)
REF_OFF = tuple(sum(REF_SIZES[:i]) for i in range(len(REF_SIZES)))
IN_DIM = sum(REF_SIZES)
IN_WEIGHT_MOVES = (
    (C_QA, REF_OFF[0], A_WIDTH),
    (C_QI, REF_OFF[3], A_WIDTH),
    (C_KV, REF_OFF[1], 2 * KV_WIDTH),
    (C_GDN, REF_OFF[6], GDN_WIDTH),
    (C_SM + SM_KI, REF_OFF[4], IDX_DIM + IDX_HEADS),
    (C_SM + SM_BETA, REF_OFF[10], 2 * B_HEADS),
)
IN_PAD = IN_COLS - (C_SM + SM_DECAY + B_HEADS)

VMEM_LIMIT = 56 * 1024 * 1024
GDN_SEQS = 2
CONV_PAD = 8
COUNT_GROUP = 4
FINE_STEPS = 17
FINE_CHECK = 6
COUNT_ROWS = 32
V_ROWS = A_HEAD_DIM + 16
ATT_HEADS = 2
ATT_WAVE = 4
KEY_BLOCK = 2 * LANES
NEG_BIG = -1e30
BOUND_SLACK = 1.001
MIN_WEIGHT_SUM = 2.0 ** -60
LOG2_E = 1.4426950408889634
NEG_INF_KEY = -(2 ** 31) + 0x7FFFFF
POS_INF_KEY = 0x7F800000


def _rms(x, g):
    return x * lax.rsqrt(jnp.mean(x * x, axis=-1, keepdims=True) + EPS) * g


def _dot(a, b):
    return jnp.dot(a, b, preferred_element_type=F32)


def _dot_nt(a, b):
    return lax.dot_general(a, b, (((1,), (1,)), ((), ())), preferred_element_type=F32)


def _inproj_body(x_ref, g_ref, win_ref, cos_ref, sin_ref, cw_ref, wo_in, w1_in, w2_in,
                 qa_ref, qi_ref, kv_ref, gdn_ref, sm_ref, wo_out, w1_out, w2_out,
                 xpad_ref, w_ref, *, tiles_per_seq):
    for src, dst in ((wo_in, wo_out), (w1_in, w1_out), (w2_in, w2_out)):
        dst[...] = src[...].astype(BF16)

    @pl.when(pl.program_id(0) == 0)
    def _():
        step = 256
        for dst, src, width in IN_WEIGHT_MOVES[:-2]:
            for r0 in range(0, width, step):
                w_ref[dst + r0:dst + r0 + step, :] = win_ref[src + r0:src + r0 + step, :].astype(BF16)
        small = [win_ref[src:src + width, :] for _, src, width in IN_WEIGHT_MOVES[-2:]]
        small.append(jnp.zeros((IN_PAD, D_MODEL), F32))
        w_ref[C_SM:IN_COLS, :] = jnp.concatenate(small, axis=0).astype(BF16)

    h = _rms(x_ref[...], g_ref[...]).astype(BF16)
    cos = cos_ref[...]
    sin = sin_ref[...]
    lane = lax.broadcasted_iota(I32, cos.shape, 1)
    first_half = (lane & (A_HEAD_DIM - 1)) < A_HEAD_DIM // 2

    def rope(t):
        swapped = jnp.where(first_half, pltpu.roll(t, LANES - A_HEAD_DIM // 2, 1),
                            pltpu.roll(t, A_HEAD_DIM // 2, 1))
        return t * cos + swapped * sin

    def proj(c0, width):
        return _dot_nt(h, w_ref[c0:c0 + width, :])

    tm = x_ref.shape[0]
    conv_cols = 3 * B_WIDTH

    @pl.when(pl.program_id(0) % tiles_per_seq == 0)
    def _():
        xpad_ref[0:CONV_PAD, :] = jnp.zeros((CONV_PAD, conv_cols), F32)

    @pl.when(pl.program_id(0) % tiles_per_seq != 0)
    def _():
        xpad_ref[0:CONV_PAD, :] = xpad_ref[tm:tm + CONV_PAD, :]

    def stage_conv_input(seg):
        xpad_ref[CONV_PAD:CONV_PAD + tm, seg * B_WIDTH:(seg + 1) * B_WIDTH] = proj(
            C_GDN + seg * B_WIDTH, B_WIDTH)

    def conv_segment(seg):
        for hh in range(B_HEADS):
            cs = slice(seg * B_WIDTH + hh * LANES, seg * B_WIDTH + (hh + 1) * LANES)
            xa = xpad_ref[:, cs]
            y = cw_ref[0:1, cs] * xa
            for jj in range(1, CONV_WIDTH):
                y = cw_ref[jj:jj + 1, cs] * xa + pltpu.roll(y, 1, 0)
            y = y[CONV_PAD:, :]
            y = y * jax.nn.sigmoid(y)
            if seg < 2:
                y = y * lax.rsqrt(jnp.sum(y * y, axis=-1, keepdims=True) + EPS)
            if seg == 0:
                y = y * (B_HEAD_DIM ** -0.5)
            gdn_ref[:, cs] = y

    def store_queries_transposed(out_ref, acc, scale):
        for j in range(A_WIDTH // LANES):
            t = rope(acc[:, j * LANES:(j + 1) * LANES])
            if scale != 1.0:
                t = t * scale
            for b in range(tm // A_QUERIES):
                tt = t[b * A_QUERIES:(b + 1) * A_QUERIES, :].T
                for hh in range(LANES // A_HEAD_DIM):
                    h0 = (j * (LANES // A_HEAD_DIM) + hh) * A_QUERIES
                    out_ref[b, :, h0:h0 + A_QUERIES] = (
                        tt[hh * A_HEAD_DIM:(hh + 1) * A_HEAD_DIM, :].astype(BF16))

    stage_conv_input(0)
    acc = proj(C_QA, A_WIDTH)
    conv_segment(0)
    store_queries_transposed(qa_ref, acc, A_HEAD_DIM ** -0.5 * LOG2_E)
    stage_conv_input(1)
    acc = proj(C_QI, A_WIDTH)
    conv_segment(1)
    store_queries_transposed(qi_ref, acc, 1.0)
    stage_conv_input(2)
    acc = proj(C_KV, 2 * KV_WIDTH)
    acc_sm = proj(C_SM, LANES)
    z = proj(C_GDN + conv_cols, B_WIDTH)
    conv_segment(2)
    kv_ref[:, 0:KV_WIDTH] = rope(acc[:, 0:KV_WIDTH]).astype(BF16)
    kv_ref[:, KV_WIDTH:] = acc[:, KV_WIDTH:].astype(BF16)
    sm_ref[...] = jnp.where(lane < IDX_DIM, rope(acc_sm), acc_sm)
    gdn_ref[:, conv_cols:] = z * jax.nn.sigmoid(z)


def _inproj(x2, g, w, layer, cos_t, sin_t, conv_w, later_weights, seq, tm):
    m = x2.shape[0]
    assert seq % tm == 0 and tm % A_QUERIES == 0, (seq, tm)
    nt = seq // tm
    steps = m // tm
    row = lambda i: (i, 0)
    const = lambda i: (0, 0)
    slab_specs, slab_shapes = [], []
    for lw in later_weights:
        rows, cols = lw.shape[1:]
        assert rows % steps == 0 and (rows // steps) % 16 == 0, (rows, steps)
        slab_specs.append((pl.BlockSpec((None, rows // steps, cols), lambda i: (layer, i, 0)),
                           pl.BlockSpec((rows // steps, cols), row)))
        slab_shapes.append(jax.ShapeDtypeStruct((rows, cols), BF16))
    return pl.pallas_call(
        functools.partial(_inproj_body, tiles_per_seq=nt),
        grid=(m // tm,),
        in_specs=[
            pl.BlockSpec((tm, D_MODEL), row),
            pl.BlockSpec((1, D_MODEL), const),
            pl.BlockSpec((None, IN_DIM, D_MODEL), lambda i: (layer, 0, 0),
                         pipeline_mode=pl.Buffered(1)),
            pl.BlockSpec((tm, LANES), lambda i: (i % nt, 0)),
            pl.BlockSpec((tm, LANES), lambda i: (i % nt, 0)),
            pl.BlockSpec((CONV_WIDTH, 3 * B_WIDTH), const),
        ] + [s_in for s_in, _ in slab_specs],
        out_specs=[
            pl.BlockSpec((tm // A_QUERIES, A_HEAD_DIM, A_HEADS * A_QUERIES), lambda i: (i, 0, 0)),
            pl.BlockSpec((tm // A_QUERIES, IDX_DIM, IDX_HEADS * A_QUERIES), lambda i: (i, 0, 0)),
            pl.BlockSpec((tm, 2 * KV_WIDTH), row),
            pl.BlockSpec((tm, GDN_WIDTH), row),
            pl.BlockSpec((tm, LANES), row),
        ] + [s_out for _, s_out in slab_specs],
        out_shape=[
            jax.ShapeDtypeStruct((m // A_QUERIES, A_HEAD_DIM, A_HEADS * A_QUERIES), BF16),
            jax.ShapeDtypeStruct((m // A_QUERIES, IDX_DIM, IDX_HEADS * A_QUERIES), BF16),
            jax.ShapeDtypeStruct((m, 2 * KV_WIDTH), BF16),
            jax.ShapeDtypeStruct((m, GDN_WIDTH), F32),
            jax.ShapeDtypeStruct((m, LANES), F32),
        ] + slab_shapes,
        scratch_shapes=[
            pltpu.VMEM((tm + CONV_PAD, 3 * B_WIDTH), F32),
            pltpu.VMEM((IN_COLS, D_MODEL), BF16),
        ],
        compiler_params=pltpu.CompilerParams(
            dimension_semantics=("arbitrary",), vmem_limit_bytes=VMEM_LIMIT),
        name="inproj",
    )(x2, g, w, cos_t, sin_t, conv_w, *later_weights)


def _mixer_a_body(qat_ref, qit_ref, kv_ref, sm_ref, o_ref,
                  isc_ref, isb_ref, vt_ref, acc_ref, bias_ref, knorm_ref, *, seq, topk):
    j = pl.program_id(1)
    nkt = (j + 1) * (A_QUERIES // LANES)
    n_heads_pair = A_WIDTH // LANES
    group = A_HEADS // A_KV_HEADS

    @pl.when(j == 0)
    def _():
        per_block = KEY_BLOCK // LANES
        for t in range(seq // LANES):
            vt = kv_ref[t * LANES:(t + 1) * LANES, KV_WIDTH:].astype(F32).T.astype(BF16)
            c0 = (t % per_block) * LANES
            for g in range(A_KV_HEADS):
                vt_ref[t // per_block, g * V_ROWS:g * V_ROWS + A_HEAD_DIM, c0:c0 + LANES] = (
                    vt[g * A_HEAD_DIM:(g + 1) * A_HEAD_DIM])
        for g in range(A_KV_HEADS):
            vt_ref[:, g * V_ROWS + A_HEAD_DIM:(g + 1) * V_ROWS, :] = jnp.ones(
                (seq // KEY_BLOCK, V_ROWS - A_HEAD_DIM, KEY_BLOCK), BF16)
            kg = kv_ref[:, g * A_HEAD_DIM:(g + 1) * A_HEAD_DIM].astype(F32)
            k_norm = jnp.sqrt(jnp.max(jnp.sum(kg * kg, axis=1, keepdims=True), axis=0, keepdims=True))
            knorm_ref[g:g + 1, :] = jnp.broadcast_to(k_norm, (1, LANES))

    q0 = pl.multiple_of(j * A_QUERIES, A_QUERIES)
    w_t = sm_ref[pl.ds(q0, A_QUERIES), :].T[SM_WI:SM_WI + IDX_HEADS, :]
    w_t = w_t * ((IDX_HEADS ** -0.5) * (IDX_DIM ** -0.5))

    qlane = lax.broadcasted_iota(I32, (1, A_QUERIES), 1)
    limit = q0 + (lax.shift_right_logical(qlane, CHUNK.bit_length() - 1) + 1) * CHUNK
    tile_iota = lax.broadcasted_iota(I32, (LANES, A_QUERIES), 0)
    block_iota = lax.broadcasted_iota(I32, (KEY_BLOCK, A_QUERIES), 0)
    nkb = lax.shift_right_logical(nkt + (KEY_BLOCK // LANES - 1), (KEY_BLOCK // LANES).bit_length() - 1)

    def isc_body(kb, carry):
        r0 = pl.multiple_of(kb * KEY_BLOCK, KEY_BLOCK)
        kid = sm_ref[pl.ds(r0, KEY_BLOCK), SM_KI:SM_KI + IDX_DIM].astype(BF16)
        rel = _dot(kid, qit_ref[...])
        acc = jnp.zeros((KEY_BLOCK, A_QUERIES), F32)
        for h in range(IDX_HEADS):
            acc = acc + w_t[h:h + 1, :] * jnp.maximum(rel[:, h * A_QUERIES:(h + 1) * A_QUERIES], 0.0)
        acc = jnp.where(r0 + block_iota < limit, acc, -jnp.inf)
        isc_ref[pl.ds(r0, KEY_BLOCK), :] = acc
        rounded = acc.astype(BF16)
        isb_ref[pl.ds(r0, KEY_BLOCK), :] = rounded
        hit = jnp.where(rounded.astype(F32) >= 0.0, 1.0, 0.0)
        return carry + hit.reshape(KEY_BLOCK // COUNT_ROWS, COUNT_ROWS, A_QUERIES).sum(axis=0)

    n_nonneg = lax.fori_loop(0, nkb, isc_body, jnp.zeros((COUNT_ROWS, A_QUERIES), F32))
    n_nonneg = n_nonneg.sum(axis=0, keepdims=True)

    ngrp = lax.shift_right_logical(nkt + (COUNT_GROUP - 1), COUNT_GROUP.bit_length() - 1)

    def fill_body(kt, carry):
        r0 = pl.multiple_of(kt * LANES, LANES)
        isc_ref[pl.ds(r0, LANES), :] = jnp.full((LANES, A_QUERIES), -jnp.inf, F32)
        isb_ref[pl.ds(r0, LANES), :] = jnp.full((LANES, A_QUERIES), -jnp.inf, BF16)
        return carry

    lax.fori_loop(nkb * (KEY_BLOCK // LANES), ngrp * COUNT_GROUP, fill_body, 0)

    def count(pred):
        def body(g, acc):
            for t in range(COUNT_GROUP):
                r0 = pl.multiple_of((g * COUNT_GROUP + t) * LANES, LANES)
                hit = jnp.where(pred(isc_ref[pl.ds(r0, LANES), :], r0 + tile_iota), 1.0, 0.0)
                acc = acc + hit.reshape(LANES // COUNT_ROWS, COUNT_ROWS, A_QUERIES).sum(axis=0)
            return acc
        acc = lax.fori_loop(0, ngrp, body, jnp.zeros((COUNT_ROWS, A_QUERIES), F32))
        return acc.sum(axis=0, keepdims=True)

    def count_coarse(thr):
        def body(g, acc):
            for t in range(COUNT_GROUP):
                r0 = pl.multiple_of((g * COUNT_GROUP + t) * LANES, LANES)
                hit = jnp.where(isb_ref[pl.ds(r0, LANES), :] >= thr, one_b, zero_b)
                hit = hit.reshape(LANES // COUNT_ROWS, COUNT_ROWS, A_QUERIES)
                part = hit[0]
                for i in range(1, LANES // COUNT_ROWS):
                    part = part + hit[i]
                acc = acc + part
            return acc
        acc = lax.fori_loop(0, ngrp, body, jnp.zeros((COUNT_ROWS, A_QUERIES), BF16))
        return acc.astype(F32).sum(axis=0, keepdims=True)

    kf = float(topk)
    one_b = jnp.ones((), BF16)
    zero_b = jnp.zeros((), BF16)

    def thr_coarse(key):
        bits = jnp.where(key >= 0, key, key ^ jnp.int32(0x7FFF))
        return lax.bitcast_convert_type(lax.shift_left(bits, 16), F32).astype(BF16)

    lo16 = jnp.where(n_nonneg >= kf, jnp.int32(0), jnp.int32(-2 ** 15))

    def coarse_body(i, lo):
        trial = lo | lax.shift_left(jnp.int32(1), 14 - i)
        c = count_coarse(thr_coarse(trial))
        return jnp.where(c >= kf, trial, lo)

    lo16 = lax.fori_loop(0, 15, coarse_body, lo16)
    lo16 = jnp.maximum(lo16, jnp.int32(NEG_INF_KEY >> 16))
    center = lax.shift_left(lo16, 16) | jnp.where(lo16 < 0, jnp.int32(0xFFFF), jnp.int32(0))

    def thr_of(key):
        bits = jnp.where(key >= 0, key, key ^ jnp.int32(0x7FFFFFFF))
        return jnp.where(key < jnp.int32(NEG_INF_KEY), -jnp.inf, lax.bitcast_convert_type(bits, F32))

    def fine_body(i, st):
        lo, c_lo = st
        trial = lo + lax.shift_left(jnp.int32(1), FINE_STEPS - 1 - i)
        thr = thr_of(trial)
        c = count(lambda x, s: x >= thr)
        ok = c >= kf
        return jnp.where(ok, trial, lo), jnp.where(ok, c, c_lo)

    state = lax.fori_loop(0, FINE_CHECK, fine_body,
                          (center - jnp.int32(1 << (FINE_STEPS - 1)), jnp.full((1, A_QUERIES), -1.0, F32)))
    lo, c_lo = state

    hi = lo + jnp.int32(1 << (FINE_STEPS - FINE_CHECK))
    hi_thr = thr_of(hi)

    def split_at(bound):
        def body(g, st):
            below_max, n_other = st
            for t in range(COUNT_GROUP):
                r0 = pl.multiple_of((g * COUNT_GROUP + t) * LANES, LANES)
                x = isc_ref[pl.ds(r0, LANES), :]
                below = x < bound
                part = jnp.where(below, x, -jnp.inf).reshape(LANES // COUNT_ROWS, COUNT_ROWS, A_QUERIES)
                hit = jnp.where(below, 0.0, 1.0).reshape(LANES // COUNT_ROWS, COUNT_ROWS, A_QUERIES)
                below_max = jnp.maximum(below_max, part.max(axis=0))
                n_other = n_other + hit.sum(axis=0)
            return below_max, n_other
        below_max, n_other = lax.fori_loop(
            0, ngrp, body, (jnp.full((COUNT_ROWS, A_QUERIES), -jnp.inf, F32),
                            jnp.zeros((COUNT_ROWS, A_QUERIES), F32)))
        return below_max.max(axis=0, keepdims=True), n_other.sum(axis=0, keepdims=True)

    v, c_hi = split_at(hi_thr)
    n_v = count(lambda x, s: x == v)
    by_value = ((c_hi + n_v >= kf) & (c_hi < kf)
                & (hi < jnp.int32(POS_INF_KEY)) & (lo >= jnp.int32(NEG_INF_KEY)))
    done = by_value | (c_lo == kf) | (limit <= topk)
    settled = jnp.min(jnp.where(done, 1.0, 0.0)) > 0.0

    def finish_now():
        return (jnp.where(by_value, v, thr_of(lo)), jnp.where(by_value, c_hi + n_v, c_lo),
                jnp.where(by_value, c_hi, -1.0))

    def finish_search():
        lo_f, c_lo_f = lax.fori_loop(FINE_CHECK, FINE_STEPS, fine_body, state)
        kth_f = thr_of(lo_f)
        return kth_f, c_lo_f, count(lambda x, s: x > kth_f)

    kth, n_at, n_above = lax.cond(settled, finish_now, finish_search)

    n_ge = jnp.where(limit <= topk, 0.0, jnp.where(n_at < 0.0, jnp.inf, n_at))
    has_ties = jnp.max(n_ge) > kf

    acc_ref[...] = jnp.zeros_like(acc_ref)

    def select_plain(kb, x, s_idx, carry):
        return (s_idx < limit) & (x >= kth), carry

    def make_select_ties():
        need = jnp.where(n_above < 0.0, kf, kf - n_above)
        tri = jnp.where(lax.broadcasted_iota(I32, (KEY_BLOCK, KEY_BLOCK), 1)
                        <= lax.broadcasted_iota(I32, (KEY_BLOCK, KEY_BLOCK), 0), 1.0, 0.0).astype(BF16)

        def select_ties(kb, x, s_idx, before):
            eq = x == kth
            rank = before + _dot(tri, jnp.where(eq, 1.0, 0.0).astype(BF16))
            sel = (s_idx < limit) & ((x > kth) | (eq & (rank <= need)))
            return sel, rank[KEY_BLOCK - 1:KEY_BLOCK, :]

        return select_ties

    units = list(range(A_HEADS // ATT_HEADS))
    uw = ATT_HEADS * A_QUERIES
    us = [slice(u * uw, (u + 1) * uw) for u in units]
    kd = [slice((u * ATT_HEADS // group) * A_HEAD_DIM, (u * ATT_HEADS // group + 1) * A_HEAD_DIM)
          for u in units]
    vr = [slice((u * ATT_HEADS // group) * V_ROWS, (u * ATT_HEADS // group + 1) * V_ROWS)
          for u in units]
    cs = [slice((u * ATT_HEADS % group) * A_QUERIES, (u * ATT_HEADS % group + ATT_HEADS) * A_QUERIES)
          for u in units]

    def block_inputs(select, kb, sel_carry):
        r0 = pl.multiple_of(kb * KEY_BLOCK, KEY_BLOCK)
        x = isc_ref[pl.ds(r0, KEY_BLOCK), :]
        sel, sel_carry = select(kb, x, r0 + block_iota, sel_carry)
        bias_ref[...] = jnp.where(sel, 0.0, NEG_BIG)
        return kv_ref[pl.ds(r0, KEY_BLOCK), 0:KV_WIDTH], vt_ref[kb], sel_carry

    q_norm = jnp.sqrt(jnp.sum(jnp.square(qat_ref[...].astype(F32)), axis=0, keepdims=True))
    shift = [q_norm[:, us[u]] * (knorm_ref[u * ATT_HEADS // group:u * ATT_HEADS // group + 1, 0:1]
                                  * BOUND_SLACK) for u in units]

    def fast_body(select, kb, sel_carry):
        k_tile, v_t, sel_carry = block_inputs(select, kb, sel_carry)
        for w0 in range(0, len(units), ATT_WAVE):
            wave = units[w0:w0 + ATT_WAVE]
            p = {u: jnp.exp2(_dot(k_tile[:, kd[u]], qat_ref[:, us[u]])
                             + jnp.concatenate([bias_ref[...]] * ATT_HEADS, axis=1)
                             - shift[u]).astype(BF16) for u in wave}
            pv = {u: _dot(v_t[vr[u], :], p[u]) for u in wave}
            for u in wave:
                acc_ref[vr[u], cs[u]] += pv[u]
        return sel_carry

    def att_body(select, kb, state):
        m_prev, sel_carry = state
        k_tile, v_t, sel_carry = block_inputs(select, kb, sel_carry)
        m_new = {}
        for w0 in range(0, len(units), ATT_WAVE):
            wave = units[w0:w0 + ATT_WAVE]
            s = {u: _dot(k_tile[:, kd[u]], qat_ref[:, us[u]])
                 + jnp.concatenate([bias_ref[...]] * ATT_HEADS, axis=1) for u in wave}
            for u in wave:
                m_new[u] = jnp.maximum(m_prev[:, us[u]], jnp.max(s[u], axis=0, keepdims=True))
            alpha = {u: jnp.exp2(m_prev[:, us[u]] - m_new[u]) for u in wave}
            p = {u: jnp.exp2(s[u] - m_new[u]).astype(BF16) for u in wave}
            pv = {u: _dot(v_t[vr[u], :], p[u]) for u in wave}
            for u in wave:
                acc_ref[vr[u], cs[u]] = alpha[u] * acc_ref[vr[u], cs[u]] + pv[u]
        return jnp.concatenate([m_new[u] for u in units], axis=1), sel_carry

    def attend(select):
        no_ties_seen = jnp.zeros((1, A_QUERIES), F32)
        lax.fori_loop(0, nkb, functools.partial(fast_body, select), no_ties_seen)
        sums = jnp.concatenate([acc_ref[g * V_ROWS + A_HEAD_DIM:g * V_ROWS + A_HEAD_DIM + 1, :]
                                for g in range(A_KV_HEADS)], axis=1)

        @pl.when(jnp.logical_not(jnp.min(sums) >= MIN_WEIGHT_SUM))
        def _():
            acc_ref[...] = jnp.zeros_like(acc_ref)
            init = (jnp.full((1, A_HEADS * A_QUERIES), NEG_BIG, F32), no_ties_seen)
            lax.fori_loop(0, nkb, functools.partial(att_body, select), init)

    lax.cond(has_ties, lambda: attend(make_select_ties()), lambda: attend(select_plain))

    for p in range(n_heads_pair):
        g = (2 * p) // group
        parts = []
        for h in (2 * p, 2 * p + 1):
            hq = slice((h % group) * A_QUERIES, (h % group + 1) * A_QUERIES)
            parts.append(acc_ref[g * V_ROWS:g * V_ROWS + A_HEAD_DIM, hq]
                         / acc_ref[g * V_ROWS + A_HEAD_DIM:g * V_ROWS + A_HEAD_DIM + 1, hq])
        o_ref[:, p * LANES:(p + 1) * LANES] = jnp.concatenate(parts, axis=0).T.astype(BF16)


def _mixer_a(qa, qi, kv, sm, bsz, seq):
    nq = seq // A_QUERIES
    assert seq % (COUNT_GROUP * LANES) == 0, seq
    assert seq // COUNT_ROWS <= 256, seq
    topk = min(TOPK_MAX, seq // 4)
    qrow = lambda b, j: (b * nq + j, 0)
    qblock = lambda b, j: (b * nq + j, 0, 0)
    brow = lambda b, j: (b, 0)
    return pl.pallas_call(
        functools.partial(_mixer_a_body, seq=seq, topk=topk),
        grid=(bsz, nq),
        in_specs=[
            pl.BlockSpec((None, A_HEAD_DIM, A_HEADS * A_QUERIES), qblock),
            pl.BlockSpec((None, IDX_DIM, IDX_HEADS * A_QUERIES), qblock),
            pl.BlockSpec((seq, 2 * KV_WIDTH), brow),
            pl.BlockSpec((seq, LANES), brow),
        ],
        out_specs=pl.BlockSpec((A_QUERIES, A_WIDTH), qrow),
        out_shape=jax.ShapeDtypeStruct((bsz * seq, A_WIDTH), BF16),
        scratch_shapes=[
            pltpu.VMEM((seq, A_QUERIES), F32),
            pltpu.VMEM((seq, A_QUERIES), BF16),
            pltpu.VMEM((seq // KEY_BLOCK, A_KV_HEADS * V_ROWS, KEY_BLOCK), BF16),
            pltpu.VMEM((A_KV_HEADS * V_ROWS, (A_HEADS // A_KV_HEADS) * A_QUERIES), F32),
            pltpu.VMEM((KEY_BLOCK, A_QUERIES), F32),
            pltpu.VMEM((8, LANES), F32),
        ],
        compiler_params=pltpu.CompilerParams(
            dimension_semantics=("arbitrary", "arbitrary"), vmem_limit_bytes=VMEM_LIMIT),
        name="mixer_a",
    )(qa, qi, kv, sm)


def _gdn_prepare(gdn_ref, sm_ref, alog_ref, dtb_ref, gate_ref, gct_ref, oc_ref, lhs_ref, bm_ref, *, tc):
    n_chunks = tc // CHUNK
    prep_unroll = 8
    q_col, k_col, v_col = (i * B_WIDTH for i in range(3))

    sm = sm_ref[...]
    beta = jax.nn.sigmoid(sm)
    z = sm + dtb_ref[...]
    softplus = jnp.maximum(z, 0.0) + jnp.log(1.0 + jnp.exp(-jnp.abs(z)))
    g = -jnp.exp(alog_ref[...]) * softplus
    rin = lax.broadcasted_iota(I32, (tc, LANES), 0) & (CHUNK - 1)
    gc = g
    step = 1
    while step < CHUNK:
        gc = gc + jnp.where(rin >= step, pltpu.roll(gc, step, 0), 0.0)
        step *= 2
    gc3 = gc.reshape(n_chunks, CHUNK, LANES)
    g_last = jnp.broadcast_to(gc3[:, CHUNK - 1:CHUNK, :], gc3.shape).reshape(tc, LANES)
    gate_ref[0] = beta
    gate_ref[1] = gc
    gate_ref[2] = jnp.exp(gc)
    gate_ref[3] = jnp.exp(g_last - gc)
    gate_ref[4] = jnp.exp(g_last)
    for i in range(tc // LANES):
        t = gc[i * LANES:(i + 1) * LANES, :].T
        for half in range(LANES // CHUNK):
            gct_ref[i * (LANES // CHUNK) + half] = t[:, half * CHUNK:(half + 1) * CHUNK]

    ci = lax.broadcasted_iota(I32, (CHUNK, CHUNK), 0)
    si = lax.broadcasted_iota(I32, (CHUNK, CHUNK), 1)
    wl = lax.broadcasted_iota(I32, (CHUNK, 2 * CHUNK), 1)
    wr = lax.broadcasted_iota(I32, (CHUNK, 2 * CHUNK), 0)
    right = wl >= CHUNK
    eye_right = jnp.where(wl == wr + CHUNK, 1.0, 0.0)

    def prep_body(cg, carry):
        units = [(cg * prep_unroll + cc, h) for cc in range(prep_unroll) for h in range(B_HEADS)]
        rows = [pl.ds(pl.multiple_of(c * CHUNK, CHUNK), CHUNK) for c, _ in units]
        hsl = [slice(h * LANES, (h + 1) * LANES) for _, h in units]
        idx = range(len(units))

        def col(i, gate, off):
            h = units[i][1]
            return gate_ref[gate, rows[i], off + h:off + h + 1]

        def seg(i, col0):
            h = units[i][1]
            return gdn_ref[rows[i], col0 + h * LANES:col0 + (h + 1) * LANES]

        q = [seg(i, q_col) for i in idx]
        k = [seg(i, k_col) for i in idx]
        kb = [k[i] * col(i, 0, SM_BETA) for i in idx]
        kq = [_dot_nt(jnp.concatenate([kb[i], q[i]], axis=0).astype(BF16), k[i].astype(BF16))
              for i in idx]
        decay = []
        for i, (c, h) in enumerate(units):
            d = col(i, 1, SM_DECAY) - gct_ref[c][SM_DECAY + h:SM_DECAY + h + 1, :]
            decay.append(jnp.where(ci >= si, jnp.exp(jnp.where(ci >= si, d, 0.0)), 0.0))
        wmat = []
        for i, (c, h) in enumerate(units):
            n_mat = jnp.where(ci > si, -(kq[i][0:CHUNK] * decay[i]), 0.0)
            wmat.append(jnp.concatenate([n_mat, jnp.zeros_like(n_mat)], axis=1) + eye_right)
        pw = 1
        while pw < CHUNK:
            wb = [wmat[i].astype(BF16) for i in idx]
            wmat = [_dot(wb[i][:, 0:CHUNK], wb[i]) + jnp.where(right, wmat[i], 0.0) for i in idx]
            pw *= 2
        eg = [col(i, 2, SM_DECAY) for i in idx]
        rhs = [jnp.concatenate([seg(i, v_col) * col(i, 0, SM_BETA), kb[i] * eg[i]],
                               axis=1).astype(BF16) for i in idx]
        sol = [_dot(wmat[i][:, CHUNK:].astype(BF16), rhs[i]).astype(BF16) for i in idx]
        att = [(kq[i][CHUNK:] * decay[i]).astype(BF16) for i in idx]
        k_tail_t = [(k[i] * col(i, 3, SM_DECAY)).T.astype(BF16) for i in idx]
        a_uw = [_dot(att[i], sol[i]) for i in idx]
        k_uw = [_dot(k_tail_t[i], sol[i]) for i in idx]
        for i, (c, h) in enumerate(units):
            oc_ref[rows[i], hsl[i]] = a_uw[i][:, 0:B_HEAD_DIM]
            lhs_ref[c, h, 0:CHUNK, :] = (q[i] * eg[i] - a_uw[i][:, B_HEAD_DIM:]).astype(BF16)
            lhs_ref[c, h, CHUNK:, :] = k_uw[i][:, B_HEAD_DIM:].astype(BF16)
            bm_ref[c, h] = k_uw[i][:, 0:B_HEAD_DIM]
        return carry

    lax.fori_loop(0, n_chunks // prep_unroll, prep_body, 0)


def _gdn_body(gdn_ref, sm_ref, alog_ref, dtb_ref, ng_ref, o_ref,
              gate_ref, gct_ref, state_ref, oc_ref, lhs_ref, bm_ref, *, tc):
    n_seqs = gdn_ref.shape[0]
    tb = pl.program_id(1)
    n_chunks = tc // CHUNK
    z_col = 3 * B_WIDTH

    @pl.when(tb == 0)
    def _():
        state_ref[...] = jnp.zeros_like(state_ref)

    for s in range(n_seqs):
        _gdn_prepare(gdn_ref.at[s], sm_ref.at[s], alog_ref, dtb_ref, gate_ref.at[s], gct_ref.at[s],
                     oc_ref.at[s], lhs_ref.at[s], bm_ref.at[s], tc=tc)

    chains = [(s, h) for s in range(n_seqs) for h in range(B_HEADS)]

    def scan_body(c, carry):
        r0 = pl.multiple_of(c * CHUNK, CHUNK)
        rows = pl.ds(r0, CHUNK)
        s_prev = [state_ref[s, h] for s, h in chains]
        r = [_dot(lhs_ref[s, c, h], s_prev[i].astype(BF16))
             for i, (s, h) in enumerate(chains)]
        for i, (s, h) in enumerate(chains):
            hs = slice(h * LANES, (h + 1) * LANES)
            gl = gate_ref[s, 4, rows, SM_DECAY + h:SM_DECAY + h + 1][0:1, :]
            oc_ref[s, rows, hs] = oc_ref[s, rows, hs] + r[i][0:CHUNK]
            state_ref[s, h] = s_prev[i] * gl + bm_ref[s, c, h] - r[i][CHUNK:]
        return carry

    lax.fori_loop(0, n_chunks, scan_body, 0)

    for s, h in chains:
        hs = slice(h * LANES, (h + 1) * LANES)
        z_gate = gdn_ref[s, :, z_col + h * LANES:z_col + (h + 1) * LANES]
        o_ref[s, :, hs] = (_rms(oc_ref[s, :, hs], ng_ref[...]) * z_gate).astype(BF16)


def _gdn(gdn_in, sm, alog_row, dtb_row, norm_g, bsz, seq, tc):
    ns = GDN_SEQS if bsz % GDN_SEQS == 0 else 1
    nt = seq // tc
    blk = lambda b, t: (b, t, 0)
    const = lambda b, t: (0, 0)
    n_chunks = tc // CHUNK
    out = pl.pallas_call(
        functools.partial(_gdn_body, tc=tc),
        grid=(bsz // ns, nt),
        in_specs=[
            pl.BlockSpec((ns, tc, GDN_WIDTH), blk),
            pl.BlockSpec((ns, tc, LANES), blk),
            pl.BlockSpec((1, LANES), const),
            pl.BlockSpec((1, LANES), const),
            pl.BlockSpec((1, B_HEAD_DIM), const),
        ],
        out_specs=pl.BlockSpec((ns, tc, B_WIDTH), blk),
        out_shape=jax.ShapeDtypeStruct((bsz, seq, B_WIDTH), BF16),
        scratch_shapes=[
            pltpu.VMEM((ns, 5, tc, LANES), F32),
            pltpu.VMEM((ns, n_chunks, LANES, CHUNK), F32),
            pltpu.VMEM((ns, B_HEADS, B_HEAD_DIM, B_HEAD_DIM), F32),
            pltpu.VMEM((ns, tc, B_WIDTH), F32),
            pltpu.VMEM((ns, n_chunks, B_HEADS, CHUNK + B_HEAD_DIM, B_HEAD_DIM), BF16),
            pltpu.VMEM((ns, n_chunks, B_HEADS, B_HEAD_DIM, B_HEAD_DIM), F32),
        ],
        compiler_params=pltpu.CompilerParams(
            dimension_semantics=("arbitrary", "arbitrary"), vmem_limit_bytes=VMEM_LIMIT),
        name="gdn",
    )(gdn_in.reshape(bsz, seq, GDN_WIDTH), sm.reshape(bsz, seq, LANES), alog_row, dtb_row, norm_g)
    return out.reshape(bsz * seq, B_WIDTH)


def _ffn_body(x_ref, oa_ref, ob_ref, wo_ref, g2_ref, w1_ref, w2_ref, g3_ref, out_ref, *,
              final_norm):
    y = (x_ref[...] + _dot(oa_ref[...], wo_ref[0:A_WIDTH, :])
         + _dot(ob_ref[...], wo_ref[A_WIDTH:, :]))
    h = _rms(y, g2_ref[...]).astype(BF16)
    a = jnp.square(jnp.maximum(_dot(h, w1_ref[...]), 0.0)).astype(BF16)
    acc = y + _dot(a, w2_ref[...])
    out_ref[...] = _rms(acc, g3_ref[...]) if final_norm else acc


def _ffn(x2, oa, ob, wo, g2, w1, w2, g3, tm, final_norm):
    m = x2.shape[0]
    row = lambda i: (i, 0)
    const = lambda i: (0, 0)
    resident = functools.partial(pl.BlockSpec, index_map=const, pipeline_mode=pl.Buffered(1))
    return pl.pallas_call(
        functools.partial(_ffn_body, final_norm=final_norm),
        grid=(m // tm,),
        in_specs=[
            pl.BlockSpec((tm, D_MODEL), row),
            pl.BlockSpec((tm, A_WIDTH), row),
            pl.BlockSpec((tm, B_WIDTH), row),
            resident((D_MODEL, D_MODEL)),
            pl.BlockSpec((1, D_MODEL), const),
            resident((D_MODEL, D_FF)),
            resident((D_FF, D_MODEL)),
            pl.BlockSpec((1, D_MODEL), const),
        ],
        out_specs=pl.BlockSpec((tm, D_MODEL), row),
        out_shape=jax.ShapeDtypeStruct((m, D_MODEL), F32),
        compiler_params=pltpu.CompilerParams(
            dimension_semantics=("arbitrary",), vmem_limit_bytes=VMEM_LIMIT),
        name="ffn",
    )(x2, oa, ob, wo, g2, w1, w2, g3)


def _rope_tables(seq):
    half = A_HEAD_DIM // 2
    inv_freq = 1.0 / (ROPE_THETA ** (jnp.arange(half, dtype=F32) / half))
    ang = jnp.arange(seq).astype(F32)[:, None] * inv_freq[None, :]
    cos = jnp.cos(ang)
    sin = jnp.sin(ang)
    reps = LANES // A_HEAD_DIM
    return (jnp.tile(cos, (1, 2 * reps)),
            jnp.tile(jnp.concatenate([-sin, sin], axis=1), (1, reps)))


def _lane_row(vals, offset):
    return jnp.zeros((1, LANES), F32).at[0, offset:offset + vals.shape[0]].set(vals.astype(F32))


def kernel(x, norm_mix_g, w_in, conv_w, a_log, dt_bias, gdn_norm_g, w_out,
           norm_ffn_g, w_ff1, w_ff2, norm_final_g):
    bsz, seq, d = x.shape
    depth = w_in.shape[0]
    m = bsz * seq
    cos_t, sin_t = _rope_tables(seq)
    x2 = x.reshape(m, d)
    for l in range(depth):
        qa, qi, kv, gdn_in, sm, wo_b, w1_b, w2_b = _inproj(
            x2, norm_mix_g[l][None, :], jnp.swapaxes(w_in, 1, 2), l, cos_t, sin_t, conv_w[l],
            (w_out, w_ff1, w_ff2), seq, tm=512)
        o_a = _mixer_a(qa, qi, kv, sm, bsz, seq)
        o_b = _gdn(gdn_in, sm, _lane_row(a_log[l], SM_DECAY),
                   _lane_row(dt_bias[l], SM_DECAY), gdn_norm_g[l][None, :], bsz, seq, tc=512)
        x2 = _ffn(x2, o_a, o_b, wo_b, norm_ffn_g[l][None, :], w1_b, w2_b, norm_final_g[None, :],
                  tm=512, final_norm=(l == depth - 1))
    return x2.reshape(bsz, seq, d)
```
